```python
import math
import jax, jax.numpy as jnp
from jax import lax
import numpy as np

D_MODEL = 2048
BATCH = 8
SEQ = 4096
DEPTH = 2

HEAD_DIM = 128
N_HEADS = D_MODEL // HEAD_DIM
N_HEADS_FOX = N_HEADS // 2
N_HEADS_DIL = N_HEADS - N_HEADS_FOX
D_FOX = N_HEADS_FOX * HEAD_DIM
D_DIL = N_HEADS_DIL * HEAD_DIM
D_MIX = D_FOX + D_DIL
SPLIT_SIZES = (D_FOX, D_FOX, D_FOX, N_HEADS_FOX, D_DIL, D_DIL, D_DIL)
SPLIT_POINTS = tuple(int(s) for s in np.cumsum(SPLIT_SIZES)[:-1])
N_IN = int(sum(SPLIT_SIZES))
D_FF = 4 * D_MODEL
Q_BLOCK = 128
DIL_PATTERNS = ((128, 1), (512, 4), (2048, 16))
DIL_BLOCK = 128
REL_BUCKETS = 32
REL_MAX_DISTANCE = 2048
NORM_EPS = 1e-6
NEG_INF = -1e30

kernel_name = "hymba_fox_dilated_hybrid"


def rms_norm(x, g):
    xf = x.astype(jnp.float32)
    y = xf * lax.rsqrt(jnp.mean(xf * xf, axis=-1, keepdims=True) + NORM_EPS)
    return (y * g.astype(jnp.float32)).astype(x.dtype)


def fox_attention(q, k, v, log_f):
    B, S, H, E = q.shape
    scale = E ** -0.5
    c = jnp.cumsum(log_f, axis=1).transpose(0, 2, 1)
    qh = q.transpose(0, 2, 1, 3)
    kh = k.transpose(0, 2, 1, 3)
    vh = v.transpose(0, 2, 1, 3)
    k_pos = jnp.arange(S)
    n_blk = S // Q_BLOCK

    def block(i):
        start = i * Q_BLOCK
        qb = lax.dynamic_slice_in_dim(qh, start, Q_BLOCK, axis=2)
        cb = lax.dynamic_slice_in_dim(c, start, Q_BLOCK, axis=2)
        logits = jnp.einsum('bhqe,bhke->bhqk', qb, kh,
                            preferred_element_type=jnp.float32) * scale
        logits = logits + (cb[..., :, None] - c[..., None, :])
        q_pos = start + jnp.arange(Q_BLOCK)
        causal = k_pos[None, :] <= q_pos[:, None]
        logits = jnp.where(causal, logits, NEG_INF)
        p = jax.nn.softmax(logits, axis=-1)
        return jnp.einsum('bhqk,bhke->bqhe', p.astype(v.dtype), vh)

    out = lax.map(block, jnp.arange(n_blk))
    return out.transpose(1, 0, 2, 3, 4).reshape(B, S, H, E)


def rel_bucket(dist):
    max_exact = REL_BUCKETS // 2
    d = jnp.maximum(dist.astype(jnp.float32), 1.0)
    large = max_exact + (jnp.log(d / max_exact) / math.log(REL_MAX_DISTANCE / max_exact)
                         * (REL_BUCKETS - max_exact)).astype(jnp.int32)
    large = jnp.minimum(large, REL_BUCKETS - 1)
    return jnp.where(dist < max_exact, dist, large)


def dilated_pattern(q, k, v, rel_bias, window, dilation):
    B, S, H, E = q.shape
    scale = E ** -0.5
    span = window // dilation
    group = dilation * DIL_BLOCK
    s_pad = -(-S // group) * group
    pad = s_pad - S
    n_sub = s_pad // dilation
    n_blk = n_sub // DIL_BLOCK

    def to_blocks(t):
        t = jnp.pad(t, ((0, 0), (0, pad), (0, 0), (0, 0)))
        t = t.reshape(B, n_sub, dilation, H, E).transpose(0, 2, 3, 1, 4)
        return t.reshape(B, dilation, H, n_blk, DIL_BLOCK, E)

    def with_prev(t):
        prev = jnp.pad(t, ((0, 0), (0, 0), (0, 0), (1, 0), (0, 0), (0, 0)))[:, :, :, :-1]
        return jnp.concatenate([prev, t], axis=4)

    qb = to_blocks(q)
    kc = with_prev(to_blocks(k))
    vc = with_prev(to_blocks(v))
    logits = jnp.einsum('bdhnqe,bdhnke->bdhnqk', qb, kc,
                        preferred_element_type=jnp.float32) * scale
    i = jnp.arange(DIL_BLOCK)[:, None]
    j = jnp.arange(2 * DIL_BLOCK)[None, :]
    rel = DIL_BLOCK + i - j
    in_band = (rel >= 0) & (rel <= span)
    bias = rel_bias.astype(jnp.float32)[rel_bucket(jnp.clip(rel, 0, span) * dilation)]
    bias = bias.transpose(2, 0, 1)
    key_valid = (jnp.arange(n_blk)[:, None, None] > 0) | (j[None] >= DIL_BLOCK)
    mask = in_band[None] & key_valid
    logits = logits + bias[None, None, :, None]
    logits = jnp.where(mask[None, None, None], logits, NEG_INF)
    m = jnp.max(logits, axis=-1, keepdims=True)
    p = jnp.exp(logits - m)
    s = jnp.sum(p, axis=-1, keepdims=True)
    o = jnp.einsum('bdhnqk,bdhnke->bdhnqe', p, vc.astype(jnp.float32)) / s
    lse = (m + jnp.log(s))[..., 0]

    def from_blocks(t):
        tail = t.shape[5:]
        t = t.reshape((B, dilation, H, n_sub) + tail)
        t = jnp.moveaxis(t, 3, 1)
        return t.reshape((B, s_pad, H) + tail)[:, :S]

    return from_blocks(o), from_blocks(lse)


def dilated_attention(q, k, v, rel_bias):
    outs, lses = [], []
    for window, dilation in DIL_PATTERNS:
        o, l = dilated_pattern(q, k, v, rel_bias, window, dilation)
        outs.append(o)
        lses.append(l)
    alpha = jax.nn.softmax(jnp.stack(lses, axis=0), axis=0)
    return jnp.einsum('pbsh,pbshe->bshe', alpha, jnp.stack(outs, axis=0))


def _fwd_setup_inputs(seed: int = 0) -> dict:
    key = jax.random.key(seed)
    ks = jax.random.split(key, 12)
    f32 = jnp.float32
    return {
        "x": jax.random.normal(ks[0], (BATCH, SEQ, D_MODEL), f32),
        "norm1_g": 1.0 + 0.02 * jax.random.normal(ks[1], (DEPTH, D_MODEL), f32),
        "w_in": jax.random.normal(ks[2], (DEPTH, D_MODEL, N_IN), f32) * D_MODEL ** -0.5,
        "forget_b": jax.random.uniform(ks[3], (DEPTH, N_HEADS_FOX), f32, 1.0, 4.0),
        "rel_bias": 0.5 * jax.random.normal(ks[4], (REL_BUCKETS, N_HEADS_DIL), f32),
        "outnorm_a_g": 1.0 + 0.02 * jax.random.normal(ks[5], (DEPTH, D_FOX), f32),
        "outnorm_b_g": 1.0 + 0.02 * jax.random.normal(ks[6], (DEPTH, D_DIL), f32),
        "w_out": jax.random.normal(ks[7], (DEPTH, D_MIX, D_MODEL), f32) * D_MIX ** -0.5,
        "norm2_g": 1.0 + 0.02 * jax.random.normal(ks[8], (DEPTH, D_MODEL), f32),
        "w_mlp_in": jax.random.normal(ks[9], (DEPTH, D_MODEL, D_FF), f32) * D_MODEL ** -0.5,
        "w_mlp_out": jax.random.normal(ks[10], (DEPTH, D_FF, D_MODEL), f32) * D_FF ** -0.5,
        "final_norm_g": 1.0 + 0.02 * jax.random.normal(ks[11], (D_MODEL,), f32),
    }


def _fwd_reference(x, norm1_g, w_in, forget_b, rel_bias, outnorm_a_g, outnorm_b_g, w_out,
              norm2_g, w_mlp_in, w_mlp_out, final_norm_g):
    B, S, _ = x.shape
    for l in range(DEPTH):
        h = rms_norm(x, norm1_g[l])
        proj = h @ w_in[l]
        q_a, k_a, v_a, f_a, q_b, k_b, v_b = jnp.split(proj, SPLIT_POINTS, axis=-1)
        heads_a = lambda t: t.reshape(B, S, N_HEADS_FOX, HEAD_DIM)
        heads_b = lambda t: t.reshape(B, S, N_HEADS_DIL, HEAD_DIM)
        log_f = jax.nn.log_sigmoid(f_a.astype(jnp.float32) + forget_b[l].astype(jnp.float32))
        y_a = fox_attention(heads_a(q_a), heads_a(k_a), heads_a(v_a), log_f)
        y_a = y_a.reshape(B, S, D_FOX)
        y_b = dilated_attention(heads_b(q_b), heads_b(k_b), heads_b(v_b), rel_bias)
        y_b = y_b.reshape(B, S, D_DIL).astype(x.dtype)
        mixed = jnp.concatenate([rms_norm(y_a, outnorm_a_g[l]),
                                 rms_norm(y_b, outnorm_b_g[l])], axis=-1)
        x = x + mixed @ w_out[l]
        h = rms_norm(x, norm2_g[l])
        x = x + jnp.square(jax.nn.relu(h @ w_mlp_in[l])) @ w_mlp_out[l]
    return rms_norm(x, final_norm_g)


import jax as _jax
import jax.numpy as _jnp

TWIN_FORMAT = 'train_step'
FWD_PARAMS = ['x', 'norm1_g', 'w_in', 'forget_b', 'rel_bias', 'outnorm_a_g', 'outnorm_b_g', 'w_out', 'norm2_g', 'w_mlp_in', 'w_mlp_out', 'final_norm_g']
TWIN_WEIGHTS = ['norm1_g', 'w_in', 'forget_b', 'rel_bias', 'outnorm_a_g', 'outnorm_b_g', 'w_out', 'norm2_g', 'w_mlp_in', 'w_mlp_out', 'final_norm_g']
TWIN_DIFF_INPUT = 'x'
TWIN_INPUTS = ['x', 'norm1_g', 'w_in', 'forget_b', 'rel_bias', 'outnorm_a_g', 'outnorm_b_g', 'w_out', 'norm2_g', 'w_mlp_in', 'w_mlp_out', 'final_norm_g', 'loss_target', 'm_norm1_g', 'm_w_in', 'm_forget_b', 'm_rel_bias', 'm_outnorm_a_g', 'm_outnorm_b_g', 'm_w_out', 'm_norm2_g', 'm_w_mlp_in', 'm_w_mlp_out', 'm_final_norm_g', 'v_norm1_g', 'v_w_in', 'v_forget_b', 'v_rel_bias', 'v_outnorm_a_g', 'v_outnorm_b_g', 'v_w_out', 'v_norm2_g', 'v_w_mlp_in', 'v_w_mlp_out', 'v_final_norm_g']
TWIN_OUTPUTS = ['loss', 'grad_x', 'grad_norm1_g', 'grad_w_in', 'grad_forget_b', 'grad_rel_bias', 'grad_outnorm_a_g', 'grad_outnorm_b_g', 'grad_w_out', 'grad_norm2_g', 'grad_w_mlp_in', 'grad_w_mlp_out', 'grad_final_norm_g', 'delta_norm1_g', 'delta_w_in', 'delta_forget_b', 'delta_rel_bias', 'delta_outnorm_a_g', 'delta_outnorm_b_g', 'delta_w_out', 'delta_norm2_g', 'delta_w_mlp_in', 'delta_w_mlp_out', 'delta_final_norm_g', 'new_m_norm1_g', 'new_m_w_in', 'new_m_forget_b', 'new_m_rel_bias', 'new_m_outnorm_a_g', 'new_m_outnorm_b_g', 'new_m_w_out', 'new_m_norm2_g', 'new_m_w_mlp_in', 'new_m_w_mlp_out', 'new_m_final_norm_g', 'new_v_norm1_g', 'new_v_w_in', 'new_v_forget_b', 'new_v_rel_bias', 'new_v_outnorm_a_g', 'new_v_outnorm_b_g', 'new_v_w_out', 'new_v_norm2_g', 'new_v_w_mlp_in', 'new_v_w_mlp_out', 'new_v_final_norm_g']
TWIN_LEAF_KINDS = {'loss': 'loss', 'grad_x': 'grad_x', 'grad_norm1_g': 'grad_w', 'grad_w_in': 'grad_w', 'grad_forget_b': 'grad_w', 'grad_rel_bias': 'grad_w', 'grad_outnorm_a_g': 'grad_w', 'grad_outnorm_b_g': 'grad_w', 'grad_w_out': 'grad_w', 'grad_norm2_g': 'grad_w', 'grad_w_mlp_in': 'grad_w', 'grad_w_mlp_out': 'grad_w', 'grad_final_norm_g': 'grad_w', 'delta_norm1_g': 'delta_w', 'delta_w_in': 'delta_w', 'delta_forget_b': 'delta_w', 'delta_rel_bias': 'delta_w', 'delta_outnorm_a_g': 'delta_w', 'delta_outnorm_b_g': 'delta_w', 'delta_w_out': 'delta_w', 'delta_norm2_g': 'delta_w', 'delta_w_mlp_in': 'delta_w', 'delta_w_mlp_out': 'delta_w', 'delta_final_norm_g': 'delta_w', 'new_m_norm1_g': 'new_m', 'new_m_w_in': 'new_m', 'new_m_forget_b': 'new_m', 'new_m_rel_bias': 'new_m', 'new_m_outnorm_a_g': 'new_m', 'new_m_outnorm_b_g': 'new_m', 'new_m_w_out': 'new_m', 'new_m_norm2_g': 'new_m', 'new_m_w_mlp_in': 'new_m', 'new_m_w_mlp_out': 'new_m', 'new_m_final_norm_g': 'new_m', 'new_v_norm1_g': 'new_v', 'new_v_w_in': 'new_v', 'new_v_forget_b': 'new_v', 'new_v_rel_bias': 'new_v', 'new_v_outnorm_a_g': 'new_v', 'new_v_outnorm_b_g': 'new_v', 'new_v_w_out': 'new_v', 'new_v_norm2_g': 'new_v', 'new_v_w_mlp_in': 'new_v', 'new_v_w_mlp_out': 'new_v', 'new_v_final_norm_g': 'new_v'}


def _forward(args):
    return _fwd_reference(*[args[k] for k in FWD_PARAMS])


def _output_shape():
    def fwd():
        inp = _fwd_setup_inputs(0)
        return _fwd_reference(*[inp[k] for k in FWD_PARAMS])
    out = _jax.eval_shape(fwd)
    return out.shape, out.dtype

N_MICROBATCH = 1
ADAM_LR = 0.001
ADAM_B1 = 0.9
ADAM_B2 = 0.999
ADAM_EPS = 1e-08
ADAM_WD = 0.01
ADAM_STEP = 10
PER_EXAMPLE_BATCH_AXIS = {'x': 0, 'loss_target': 0}
SHARED_INPUTS = []
_WEIGHT_DTYPES = {'norm1_g': _jnp.float32, 'w_in': _jnp.float32, 'forget_b': _jnp.float32, 'rel_bias': _jnp.float32, 'outnorm_a_g': _jnp.float32, 'outnorm_b_g': _jnp.float32, 'w_out': _jnp.float32, 'norm2_g': _jnp.float32, 'w_mlp_in': _jnp.float32, 'w_mlp_out': _jnp.float32, 'final_norm_g': _jnp.float32}
MOMENT_SCALE = {'norm1_g': 9.370122e-02, 'w_in': 5.359586e-02, 'forget_b': 4.217834e-01, 'rel_bias': 1.294711e-01, 'outnorm_a_g': 6.573366e-02, 'outnorm_b_g': 7.705354e-02, 'w_out': 6.923966e-02, 'norm2_g': 6.177186e-02, 'w_mlp_in': 3.138835e-02, 'w_mlp_out': 8.249054e-02, 'final_norm_g': 1.647361e+01}


def _to_microbatches(a, axis):
    t = _jnp.moveaxis(a, axis, 0)
    t = t.reshape((N_MICROBATCH, t.shape[0] // N_MICROBATCH) + t.shape[1:])
    return _jnp.moveaxis(t, 1, axis + 1)


def setup_inputs(seed: int = 0) -> dict:
    inp = _fwd_setup_inputs(seed)
    key = _jax.random.fold_in(_jax.random.key(seed), 7919)
    shape, _ = _output_shape()
    out = dict(inp)
    out["loss_target"] = _jax.random.normal(_jax.random.fold_in(key, 0), shape, _jnp.float32)
    for i, name in enumerate(TWIN_WEIGHTS):
        w = inp[name].astype(_jnp.float32)
        if MOMENT_SCALE is None:
            s = _jnp.sqrt(_jnp.mean(_jnp.square(w)) + 1e-30)
        else:
            s = MOMENT_SCALE[name]
        km, kv = _jax.random.split(_jax.random.fold_in(key, i + 1))
        out[name] = w
        out["m_" + name] = s * _jax.random.normal(km, w.shape, _jnp.float32)
        out["v_" + name] = (s * s) * _jax.random.uniform(kv, w.shape, _jnp.float32, 0.5, 1.5)
    if N_MICROBATCH > 1:
        for name, axis in PER_EXAMPLE_BATCH_AXIS.items():
            out[name] = _to_microbatches(out[name], axis)
    return {'x': out['x'], 'norm1_g': out['norm1_g'], 'w_in': out['w_in'], 'forget_b': out['forget_b'], 'rel_bias': out['rel_bias'], 'outnorm_a_g': out['outnorm_a_g'], 'outnorm_b_g': out['outnorm_b_g'], 'w_out': out['w_out'], 'norm2_g': out['norm2_g'], 'w_mlp_in': out['w_mlp_in'], 'w_mlp_out': out['w_mlp_out'], 'final_norm_g': out['final_norm_g'], 'loss_target': out['loss_target'], 'm_norm1_g': out['m_norm1_g'], 'm_w_in': out['m_w_in'], 'm_forget_b': out['m_forget_b'], 'm_rel_bias': out['m_rel_bias'], 'm_outnorm_a_g': out['m_outnorm_a_g'], 'm_outnorm_b_g': out['m_outnorm_b_g'], 'm_w_out': out['m_w_out'], 'm_norm2_g': out['m_norm2_g'], 'm_w_mlp_in': out['m_w_mlp_in'], 'm_w_mlp_out': out['m_w_mlp_out'], 'm_final_norm_g': out['m_final_norm_g'], 'v_norm1_g': out['v_norm1_g'], 'v_w_in': out['v_w_in'], 'v_forget_b': out['v_forget_b'], 'v_rel_bias': out['v_rel_bias'], 'v_outnorm_a_g': out['v_outnorm_a_g'], 'v_outnorm_b_g': out['v_outnorm_b_g'], 'v_w_out': out['v_w_out'], 'v_norm2_g': out['v_norm2_g'], 'v_w_mlp_in': out['v_w_mlp_in'], 'v_w_mlp_out': out['v_w_mlp_out'], 'v_final_norm_g': out['v_final_norm_g']}


def _loss(weights, diff, rest, loss_target):
    with _jax.named_scope("forward"):
        args = {**rest, TWIN_DIFF_INPUT: diff, **{k: w.astype(_WEIGHT_DTYPES[k]) for k, w in weights.items()}}
        y = _forward(args)
    with _jax.named_scope("loss_head"):
        err = _jnp.square(y.astype(_jnp.float32) - loss_target)
        return 0.5 * _jnp.sum(_jnp.mean(err, axis=-1)) if err.ndim else 0.5 * err


def _adamw(w, g, m, v):
    m = ADAM_B1 * m + (1.0 - ADAM_B1) * g
    v = ADAM_B2 * v + (1.0 - ADAM_B2) * _jnp.square(g)
    m_hat = m / (1.0 - ADAM_B1 ** ADAM_STEP)
    v_hat = v / (1.0 - ADAM_B2 ** ADAM_STEP)
    delta = -ADAM_LR * (m_hat / (_jnp.sqrt(v_hat) + ADAM_EPS) + ADAM_WD * w)
    return delta, m, v


def reference(x, norm1_g, w_in, forget_b, rel_bias, outnorm_a_g, outnorm_b_g, w_out, norm2_g, w_mlp_in, w_mlp_out, final_norm_g, loss_target, m_norm1_g, m_w_in, m_forget_b, m_rel_bias, m_outnorm_a_g, m_outnorm_b_g, m_w_out, m_norm2_g, m_w_mlp_in, m_w_mlp_out, m_final_norm_g, v_norm1_g, v_w_in, v_forget_b, v_rel_bias, v_outnorm_a_g, v_outnorm_b_g, v_w_out, v_norm2_g, v_w_mlp_in, v_w_mlp_out, v_final_norm_g):
    given = dict(x=x, norm1_g=norm1_g, w_in=w_in, forget_b=forget_b, rel_bias=rel_bias, outnorm_a_g=outnorm_a_g, outnorm_b_g=outnorm_b_g, w_out=w_out, norm2_g=norm2_g, w_mlp_in=w_mlp_in, w_mlp_out=w_mlp_out, final_norm_g=final_norm_g, loss_target=loss_target, m_norm1_g=m_norm1_g, m_w_in=m_w_in, m_forget_b=m_forget_b, m_rel_bias=m_rel_bias, m_outnorm_a_g=m_outnorm_a_g, m_outnorm_b_g=m_outnorm_b_g, m_w_out=m_w_out, m_norm2_g=m_norm2_g, m_w_mlp_in=m_w_mlp_in, m_w_mlp_out=m_w_mlp_out, m_final_norm_g=m_final_norm_g, v_norm1_g=v_norm1_g, v_w_in=v_w_in, v_forget_b=v_forget_b, v_rel_bias=v_rel_bias, v_outnorm_a_g=v_outnorm_a_g, v_outnorm_b_g=v_outnorm_b_g, v_w_out=v_w_out, v_norm2_g=v_norm2_g, v_w_mlp_in=v_w_mlp_in, v_w_mlp_out=v_w_mlp_out, v_final_norm_g=v_final_norm_g)
    weights = {n: given[n] for n in TWIN_WEIGHTS}
    shared = {n: given[n] for n in SHARED_INPUTS}
    per_example = {n: given[n] for n in ['x']}
    grad_fn = _jax.value_and_grad(_loss, argnums=(0, 1))

    def one_microbatch(ex, loss_target):
        ex = dict(ex)
        diff = ex.pop(TWIN_DIFF_INPUT)
        return grad_fn(weights, diff, {**shared, **ex}, loss_target)

    if N_MICROBATCH == 1:
        loss, (grad_w, grad_x) = one_microbatch(per_example, given["loss_target"])
    else:
        def body(carry, xs):
            loss_sum, grad_sum = carry
            l_k, (gw_k, gx_k) = one_microbatch(xs[0], xs[1])
            with _jax.named_scope("update"):
                return (loss_sum + l_k, _jax.tree.map(_jnp.add, grad_sum, gw_k)), gx_k

        init = (_jnp.zeros((), _jnp.float32), _jax.tree.map(_jnp.zeros_like, weights))
        (loss, grad_w), grad_x = _jax.lax.scan(body, init, (per_example, given["loss_target"]))
    with _jax.named_scope("update"):
        delta_w, new_m, new_v = {}, {}, {}
        for n in TWIN_WEIGHTS:
            delta_w[n], new_m[n], new_v[n] = _adamw(weights[n], grad_w[n], given["m_" + n], given["v_" + n])
    return (loss, grad_x, *[grad_w[n] for n in TWIN_WEIGHTS], *[delta_w[n] for n in TWIN_WEIGHTS],
            *[new_m[n] for n in TWIN_WEIGHTS], *[new_v[n] for n in TWIN_WEIGHTS])
```

```python
import functools
import math

import numpy as np
import jax
import jax.numpy as jnp
from jax import lax
from jax.experimental import pallas as pl
from jax.experimental.pallas import tpu as pltpu

F32 = jnp.float32
BF16 = jnp.bfloat16

HEAD_DIM = 128
DIL_PATTERNS = ((128, 1), (512, 4), (2048, 16))
DIL_BLOCK = 128
REL_BUCKETS = 32
REL_MAX_DISTANCE = 2048
NORM_EPS = 1e-6
NEG_INF = -1e30
ADAM_LR = 0.001
ADAM_B1 = 0.9
ADAM_B2 = 0.999
ADAM_EPS = 1e-08
ADAM_WD = 0.01
ADAM_STEP = 10

N_DEV = 8
V7X_VMEM_LIMIT_BYTES = 56 * 1024 * 1024
MESH = pl.DeviceIdType.MESH


def _params(*semantics):
    return pltpu.CompilerParams(dimension_semantics=semantics, vmem_limit_bytes=V7X_VMEM_LIMIT_BYTES)


def _fit(dim, want, unit=128):
    if dim <= want:
        return dim
    t = want - want % unit
    while dim % t:
        t -= unit
    return t


def _matmul(a, b, *, name, ta=False, tb=False, out_dtype=F32, tm=1024, tn=1024, tk=2048,
            addend=None, mode=None, aux=None, out_chunks=None):
    m_dim, k_dim = (a.shape[1], a.shape[0]) if ta else a.shape
    n_dim = b.shape[0] if tb else b.shape[1]
    assert (b.shape[1] if tb else b.shape[0]) == k_dim
    tm, tn, tk = _fit(m_dim, tm), _fit(n_dim // (out_chunks or 1), tn), _fit(k_dim, tk)
    assert m_dim % tm == 0 and n_dim % tn == 0 and k_dim % tk == 0, (name, a.shape, b.shape)
    nk = k_dim // tk
    a_spec = (pl.BlockSpec((tk, tm), lambda i, j, k: (k, i)) if ta
              else pl.BlockSpec((tm, tk), lambda i, j, k: (i, k)))
    b_spec = (pl.BlockSpec((tn, tk), lambda i, j, k: (j, k)) if tb
              else pl.BlockSpec((tk, tn), lambda i, j, k: (k, j)))
    mn_spec = pl.BlockSpec((tm, tn), lambda i, j, k: (i, j))
    if out_chunks is None:
        o_spec, o_shape = mn_spec, (m_dim, n_dim)
    else:
        per = n_dim // out_chunks // tn
        assert per * tn * out_chunks == n_dim
        o_spec = pl.BlockSpec((None, tm, tn), lambda i, j, k: (j // per, i, j % per))
        o_shape = (out_chunks, m_dim, n_dim // out_chunks)
    dims = (((0 if ta else 1,), (1 if tb else 0,)), ((), ()))
    n_out = 2 if mode == "relu_sq" else 1
    in_specs, operands = [a_spec, b_spec], [a, b]
    if addend is not None:
        in_specs.append(mn_spec)
        operands.append(addend)
    if mode == "mul2":
        in_specs.append(mn_spec)
        operands.append(aux)

    def body(*refs):
        a_ref, b_ref = refs[0], refs[1]
        pos = 2
        add_ref = aux_ref = None
        if addend is not None:
            add_ref, pos = refs[pos], pos + 1
        if mode == "mul2":
            aux_ref, pos = refs[pos], pos + 1
        outs = refs[pos:pos + n_out]

        def finish(acc):
            if add_ref is not None:
                acc = acc + add_ref[...].astype(F32)
            if mode == "relu_sq":
                r = jnp.maximum(acc, 0.0)
                outs[0][...] = r.astype(outs[0].dtype)
                outs[1][...] = (r * r).astype(outs[1].dtype)
            elif mode == "mul2":
                outs[0][...] = (acc * (2.0 * aux_ref[...].astype(F32))).astype(outs[0].dtype)
            else:
                outs[0][...] = acc.astype(outs[0].dtype)

        part = lax.dot_general(a_ref[...].astype(BF16), b_ref[...].astype(BF16), dims,
                               preferred_element_type=F32)
        if nk == 1:
            finish(part)
        else:
            acc_ref = refs[-1]
            k = pl.program_id(2)

            @pl.when(k == 0)
            def _():
                acc_ref[...] = part

            @pl.when(k > 0)
            def _():
                acc_ref[...] += part

            @pl.when(k == nk - 1)
            def _():
                finish(acc_ref[...])

    out_shape = [jax.ShapeDtypeStruct(o_shape, out_dtype)] * n_out
    res = pl.pallas_call(
        body, name=name, grid=(m_dim // tm, n_dim // tn, nk),
        in_specs=in_specs, out_specs=[o_spec] * n_out, out_shape=out_shape,
        scratch_shapes=[pltpu.VMEM((tm, tn), F32)] if nk > 1 else [],
        compiler_params=_params("parallel", "parallel", "arbitrary"),
    )(*operands)
    return res if n_out > 1 else res[0]


def _norm_fwd_math(x, g):
    r = lax.rsqrt(jnp.mean(x * x, axis=-1, keepdims=True) + NORM_EPS)
    return (x * r) * g


def _norm_bwd_math(x, g, dy):
    r = lax.rsqrt(jnp.mean(x * x, axis=-1, keepdims=True) + NORM_EPS)
    xh = x * r
    dxh = dy * g
    dx = r * (dxh - xh * jnp.mean(dxh * xh, axis=-1, keepdims=True))
    return dx, jnp.sum(dy * xh, axis=0, keepdims=True)


def _row_tile(rows, want=256):
    t = min(rows, want)
    assert rows % t == 0
    return t


def _rmsnorm_fwd(x, g, *, name):
    s, d = x.shape
    tr = _row_tile(s)

    def body(x_ref, g_ref, h_ref):
        h_ref[...] = _norm_fwd_math(x_ref[...], g_ref[...]).astype(BF16)

    return pl.pallas_call(
        body, name=name, grid=(s // tr,),
        in_specs=[pl.BlockSpec((tr, d), lambda i: (i, 0)), pl.BlockSpec((1, d), lambda i: (0, 0))],
        out_specs=pl.BlockSpec((tr, d), lambda i: (i, 0)),
        out_shape=jax.ShapeDtypeStruct((s, d), BF16),
        compiler_params=_params("parallel"),
    )(x, g)


def _rmsnorm_bwd(x, g, dh, dres, *, name):
    s, d = x.shape
    tr = _row_tile(s)

    def body(x_ref, g_ref, dh_ref, dres_ref, dx_ref, dg_ref):
        dx, dg = _norm_bwd_math(x_ref[...], g_ref[...], dh_ref[...])
        dx_ref[...] = dres_ref[...] + dx

        @pl.when(pl.program_id(0) == 0)
        def _():
            dg_ref[...] = dg

        @pl.when(pl.program_id(0) > 0)
        def _():
            dg_ref[...] += dg

    row = pl.BlockSpec((tr, d), lambda i: (i, 0))
    vec = pl.BlockSpec((1, d), lambda i: (0, 0))
    return pl.pallas_call(
        body, name=name, grid=(s // tr,),
        in_specs=[row, vec, row, row], out_specs=[row, vec],
        out_shape=[jax.ShapeDtypeStruct((s, d), F32), jax.ShapeDtypeStruct((1, d), F32)],
        compiler_params=_params("arbitrary"),
    )(x, g, dh, dres)


def _outnorm_fwd(ya, yb, ga, gb, *, name):
    s, da = ya.shape
    db = yb.shape[1]
    tr = _row_tile(s)

    def body(ya_ref, yb_ref, ga_ref, gb_ref, o_ref):
        o_ref[:, :da] = _norm_fwd_math(ya_ref[...], ga_ref[...]).astype(BF16)
        o_ref[:, da:] = _norm_fwd_math(yb_ref[...], gb_ref[...]).astype(BF16)

    return pl.pallas_call(
        body, name=name, grid=(s // tr,),
        in_specs=[pl.BlockSpec((tr, da), lambda i: (i, 0)), pl.BlockSpec((tr, db), lambda i: (i, 0)),
                  pl.BlockSpec((1, da), lambda i: (0, 0)), pl.BlockSpec((1, db), lambda i: (0, 0))],
        out_specs=pl.BlockSpec((tr, da + db), lambda i: (i, 0)),
        out_shape=jax.ShapeDtypeStruct((s, da + db), BF16),
        compiler_params=_params("parallel"),
    )(ya, yb, ga, gb)


def _outnorm_bwd(ya, yb, ga, gb, dmixed, *, name):
    s, da = ya.shape
    db = yb.shape[1]
    tr = _row_tile(s)

    def body(ya_ref, yb_ref, ga_ref, gb_ref, dm_ref, dya_ref, dyb_ref, dga_ref, dgb_ref):
        dxa, dga = _norm_bwd_math(ya_ref[...], ga_ref[...], dm_ref[:, :da])
        dxb, dgb = _norm_bwd_math(yb_ref[...], gb_ref[...], dm_ref[:, da:])
        dya_ref[...] = dxa
        dyb_ref[...] = dxb

        @pl.when(pl.program_id(0) == 0)
        def _():
            dga_ref[...] = dga
            dgb_ref[...] = dgb

        @pl.when(pl.program_id(0) > 0)
        def _():
            dga_ref[...] += dga
            dgb_ref[...] += dgb

    ra = pl.BlockSpec((tr, da), lambda i: (i, 0))
    rb = pl.BlockSpec((tr, db), lambda i: (i, 0))
    va = pl.BlockSpec((1, da), lambda i: (0, 0))
    vb = pl.BlockSpec((1, db), lambda i: (0, 0))
    return pl.pallas_call(
        body, name=name, grid=(s // tr,),
        in_specs=[ra, rb, va, vb, pl.BlockSpec((tr, da + db), lambda i: (i, 0))],
        out_specs=[ra, rb, va, vb],
        out_shape=[jax.ShapeDtypeStruct((s, da), F32), jax.ShapeDtypeStruct((s, db), F32),
                   jax.ShapeDtypeStruct((1, da), F32), jax.ShapeDtypeStruct((1, db), F32)],
        compiler_params=_params("arbitrary"),
    )(ya, yb, ga, gb, dmixed)


def _loss_head(x, g, target, *, name):
    s, d = x.shape
    tr = _row_tile(s)

    def body(x_ref, g_ref, t_ref, dx_ref, loss_ref, dg_ref):
        xv, gv = x_ref[...], g_ref[...]
        err = _norm_fwd_math(xv, gv) - t_ref[...]
        part = 0.5 * jnp.sum(jnp.mean(err * err, axis=-1, keepdims=True), axis=0, keepdims=True)
        dx, dg = _norm_bwd_math(xv, gv, err * (1.0 / d))
        dx_ref[...] = dx
        part = jnp.broadcast_to(part, (8, 128))

        @pl.when(pl.program_id(0) == 0)
        def _():
            dg_ref[...] = dg
            loss_ref[...] = part

        @pl.when(pl.program_id(0) > 0)
        def _():
            dg_ref[...] += dg
            loss_ref[...] += part

    row = pl.BlockSpec((tr, d), lambda i: (i, 0))
    vec = pl.BlockSpec((1, d), lambda i: (0, 0))
    return pl.pallas_call(
        body, name=name, grid=(s // tr,),
        in_specs=[row, vec, row],
        out_specs=[row, pl.BlockSpec((8, 128), lambda i: (0, 0)), vec],
        out_shape=[jax.ShapeDtypeStruct((s, d), F32), jax.ShapeDtypeStruct((8, 128), F32),
                   jax.ShapeDtypeStruct((1, d), F32)],
        compiler_params=_params("arbitrary"),
    )(x, g, target)


def _split3(x):
    hi = x.astype(BF16)
    rem = x - hi.astype(F32)
    mid = rem.astype(BF16)
    lo = (rem - mid.astype(F32)).astype(BF16)
    return hi, mid, lo


def _tri_sum(tri, x):
    hi, mid, lo = _split3(x)
    dot = functools.partial(jnp.dot, preferred_element_type=F32)
    return dot(tri, hi) + dot(tri, mid) + dot(tri, lo)


def _gates_fwd(f, fb, *, name):
    s, w = f.shape
    tb = 128
    nb = s // tb

    def body(f_ref, fb_ref, c_ref, carry):
        @pl.when(pl.program_id(0) == 0)
        def _():
            carry[...] = jnp.zeros_like(carry)

        logf = jax.nn.log_sigmoid(f_ref[...] + fb_ref[...])
        row = lax.broadcasted_iota(jnp.int32, (tb, tb), 0)
        col = lax.broadcasted_iota(jnp.int32, (tb, tb), 1)
        tri = (row >= col).astype(BF16)
        c = _tri_sum(tri, logf) + carry[...]
        c_ref[...] = c
        carry[...] = c[tb - 1:tb, :]

    return pl.pallas_call(
        body, name=name, grid=(nb,),
        in_specs=[pl.BlockSpec((tb, w), lambda i: (i, 0)), pl.BlockSpec((1, w), lambda i: (0, 0))],
        out_specs=pl.BlockSpec((tb, w), lambda i: (i, 0)),
        out_shape=jax.ShapeDtypeStruct((s, w), F32),
        scratch_shapes=[pltpu.VMEM((1, w), F32)],
        compiler_params=_params("arbitrary"),
    )(f, fb)


def _gates_bwd(f, fb, dc, *, name):
    s, w = f.shape
    tb = 128
    nb = s // tb

    def body(f_ref, fb_ref, dc_ref, df_ref, dfb_ref, carry):
        @pl.when(pl.program_id(0) == 0)
        def _():
            carry[...] = jnp.zeros_like(carry)
            dfb_ref[...] = jnp.zeros_like(dfb_ref)

        row = lax.broadcasted_iota(jnp.int32, (tb, tb), 0)
        col = lax.broadcasted_iota(jnp.int32, (tb, tb), 1)
        tri = (row <= col).astype(BF16)
        dlogf = _tri_sum(tri, dc_ref[...]) + carry[...]
        carry[...] = dlogf[0:1, :]
        df = dlogf * jax.nn.sigmoid(-(f_ref[...] + fb_ref[...]))
        df_ref[...] = df
        dfb_ref[...] += jnp.sum(df, axis=0, keepdims=True)

    rev = pl.BlockSpec((tb, w), lambda i: (nb - 1 - i, 0))
    vec = pl.BlockSpec((1, w), lambda i: (0, 0))
    return pl.pallas_call(
        body, name=name, grid=(nb,),
        in_specs=[rev, vec, rev], out_specs=[rev, vec],
        out_shape=[jax.ShapeDtypeStruct((s, w), F32), jax.ShapeDtypeStruct((1, w), F32)],
        scratch_shapes=[pltpu.VMEM((1, w), F32)],
        compiler_params=_params("arbitrary"),
    )(f, fb, dc)


def _nt(a, b):
    return lax.dot_general(a, b, (((1,), (1,)), ((), ())), preferred_element_type=F32)


def _tn(a, b):
    return lax.dot_general(a, b, (((0,), (0,)), ((), ())), preferred_element_type=F32)


def _nn(a, b):
    return jnp.dot(a, b, preferred_element_type=F32)


def _fox_logits(q, k, cq, ck, q0, k0, scale):
    s = _nt(q, k) * scale + (cq - ck)
    row = q0 + lax.broadcasted_iota(jnp.int32, s.shape, 0)
    col = k0 + lax.broadcasted_iota(jnp.int32, s.shape, 1)
    return jnp.where(col <= row, s, NEG_INF)


def _fox_fwd(qkv, cq, ck, n_heads, *, name):
    s = qkv.shape[0]
    e = HEAD_DIM
    t = min(512, s)
    nt = s // t
    scale = e ** -0.5

    def body(q_ref, k_ref, v_ref, cq_ref, ck_ref, o_ref, lse_ref, m_scr, l_scr, acc_scr):
        qi, ki = pl.program_id(1), pl.program_id(2)

        @pl.when(ki == 0)
        def _():
            m_scr[...] = jnp.full_like(m_scr, NEG_INF)
            l_scr[...] = jnp.zeros_like(l_scr)
            acc_scr[...] = jnp.zeros_like(acc_scr)

        @pl.when(ki <= qi)
        def _():
            sc = _fox_logits(q_ref[...], k_ref[...], cq_ref[...], ck_ref[...], qi * t, ki * t, scale)
            m_new = jnp.maximum(m_scr[...], jnp.max(sc, axis=-1, keepdims=True))
            alpha = jnp.exp(m_scr[...] - m_new)
            p = jnp.exp(sc - m_new)
            l_scr[...] = alpha * l_scr[...] + jnp.sum(p, axis=-1, keepdims=True)
            acc_scr[...] = alpha * acc_scr[...] + _nn(p.astype(BF16), v_ref[...])
            m_scr[...] = m_new

        @pl.when(ki == qi)
        def _():
            o_ref[...] = acc_scr[...] / l_scr[...]
            lse_ref[...] = m_scr[...] + jnp.log(l_scr[...])

    h_ = n_heads
    return pl.pallas_call(
        body, name=name, grid=(h_, nt, nt),
        in_specs=[
            pl.BlockSpec((t, e), lambda h, qi, ki: (qi, h)),
            pl.BlockSpec((t, e), lambda h, qi, ki: (jnp.minimum(ki, qi), h_ + h)),
            pl.BlockSpec((t, e), lambda h, qi, ki: (jnp.minimum(ki, qi), 2 * h_ + h)),
            pl.BlockSpec((None, t, 1), lambda h, qi, ki: (h, qi, 0)),
            pl.BlockSpec((None, 1, t), lambda h, qi, ki: (h, 0, jnp.minimum(ki, qi))),
        ],
        out_specs=[pl.BlockSpec((t, e), lambda h, qi, ki: (qi, h)),
                   pl.BlockSpec((None, t, 1), lambda h, qi, ki: (h, qi, 0))],
        out_shape=[jax.ShapeDtypeStruct((s, h_ * e), F32), jax.ShapeDtypeStruct((h_, s, 1), F32)],
        scratch_shapes=[pltpu.VMEM((t, 1), F32), pltpu.VMEM((t, 1), F32), pltpu.VMEM((t, e), F32)],
        compiler_params=_params("parallel", "parallel", "arbitrary"),
    )(qkv, qkv, qkv, cq, ck)


def _fox_bwd(qkv, cq, ck, lse, y, dy, n_heads, *, name):
    s = qkv.shape[0]
    e = HEAD_DIM
    t = min(512, s)
    nt = s // t
    scale = e ** -0.5

    def body(q_ref, k_ref, v_ref, cq_ref, ck_ref, lse_ref, y_ref, dy_ref,
             dq_ref, dk_ref, dv_ref, dcq_ref, dck_ref, dq_scr, dk_scr, dv_scr, dck_scr):
        ki, qi = pl.program_id(1), pl.program_id(2)

        @pl.when(jnp.logical_and(ki == 0, qi == 0))
        def _():
            dq_scr[...] = jnp.zeros_like(dq_scr)
            dcq_ref[...] = jnp.zeros_like(dcq_ref)

        @pl.when(qi == 0)
        def _():
            dk_scr[...] = jnp.zeros_like(dk_scr)
            dv_scr[...] = jnp.zeros_like(dv_scr)
            dck_scr[...] = jnp.zeros_like(dck_scr)

        @pl.when(qi >= ki)
        def _():
            q, k, v = q_ref[...], k_ref[...], v_ref[...]
            sc = _fox_logits(q, k, cq_ref[...], ck_ref[...], qi * t, ki * t, scale)
            p = jnp.exp(sc - lse_ref[...])
            do = dy_ref[...]
            delta = jnp.sum(do * y_ref[...], axis=-1, keepdims=True)
            do = do.astype(BF16)
            dv_scr[...] += _tn(p.astype(BF16), do)
            ds = p * (_nt(do, v) - delta)
            dck_scr[...] -= jnp.sum(ds, axis=0, keepdims=True)
            rows = pl.ds(pl.multiple_of(qi * t, t), t)
            dcq_ref[rows, :] += jnp.sum(ds, axis=-1, keepdims=True)
            ds = ds.astype(BF16)
            dq_scr[rows, :] += scale * _nn(ds, k)
            dk_scr[...] += scale * _tn(ds, q)

        @pl.when(qi == nt - 1)
        def _():
            dk_ref[...] = dk_scr[...].astype(BF16)
            dv_ref[...] = dv_scr[...].astype(BF16)
            dck_ref[...] = dck_scr[...]

        @pl.when(jnp.logical_and(ki == nt - 1, qi == nt - 1))
        def _():
            dq_ref[...] = dq_scr[...].astype(BF16)

    h_ = n_heads
    qrow = lambda ki, qi: jnp.maximum(qi, ki)
    return pl.pallas_call(
        body, name=name, grid=(h_, nt, nt),
        in_specs=[
            pl.BlockSpec((t, e), lambda h, ki, qi: (qrow(ki, qi), h)),
            pl.BlockSpec((t, e), lambda h, ki, qi: (ki, h_ + h)),
            pl.BlockSpec((t, e), lambda h, ki, qi: (ki, 2 * h_ + h)),
            pl.BlockSpec((None, t, 1), lambda h, ki, qi: (h, qrow(ki, qi), 0)),
            pl.BlockSpec((None, 1, t), lambda h, ki, qi: (h, 0, ki)),
            pl.BlockSpec((None, t, 1), lambda h, ki, qi: (h, qrow(ki, qi), 0)),
            pl.BlockSpec((t, e), lambda h, ki, qi: (qrow(ki, qi), h)),
            pl.BlockSpec((t, e), lambda h, ki, qi: (qrow(ki, qi), h)),
        ],
        out_specs=[pl.BlockSpec((s, e), lambda h, ki, qi: (0, h)),
                   pl.BlockSpec((t, e), lambda h, ki, qi: (ki, h)),
                   pl.BlockSpec((t, e), lambda h, ki, qi: (ki, h)),
                   pl.BlockSpec((None, s, 1), lambda h, ki, qi: (h, 0, 0)),
                   pl.BlockSpec((None, 1, t), lambda h, ki, qi: (h, 0, ki))],
        out_shape=[jax.ShapeDtypeStruct((s, h_ * e), BF16)] * 3
        + [jax.ShapeDtypeStruct((h_, s, 1), F32), jax.ShapeDtypeStruct((h_, 1, s), F32)],
        scratch_shapes=[pltpu.VMEM((s, e), F32), pltpu.VMEM((t, e), F32), pltpu.VMEM((t, e), F32),
                        pltpu.VMEM((1, t), F32)],
        compiler_params=_params("parallel", "arbitrary", "arbitrary"),
    )(qkv, qkv, qkv, cq, ck, lse, y, dy)


def _rel_bucket_table(dilation, span):
    dist = np.arange(span + 1, dtype=np.int64) * dilation
    max_exact = REL_BUCKETS // 2
    d = np.maximum(dist.astype(np.float32), np.float32(1.0))
    large = max_exact + (np.log(d / np.float32(max_exact)) / np.float32(math.log(REL_MAX_DISTANCE / max_exact))
                         * np.float32(REL_BUCKETS - max_exact)).astype(np.int32)
    large = np.minimum(large, REL_BUCKETS - 1)
    return np.where(dist < max_exact, dist, large)


def _bucket_bands(dilation, span):
    table = _rel_bucket_table(dilation, span)
    bands = []
    for n, b in enumerate(table):
        if bands and bands[-1][0] == int(b):
            bands[-1][2] = n
        else:
            assert not any(bb[0] == int(b) for bb in bands)
            bands.append([int(b), n, n])
    return [tuple(b) for b in bands]


def _steps_back():
    i = lax.broadcasted_iota(jnp.int32, (DIL_BLOCK, DIL_BLOCK), 0)
    j = lax.broadcasted_iota(jnp.int32, (DIL_BLOCK, DIL_BLOCK), 1)
    return i - j, DIL_BLOCK + i - j


def _bias_tiles(rb_ref, h, bands, span):
    n_cur, n_prev = _steps_back()
    t_cur = jnp.zeros((DIL_BLOCK, DIL_BLOCK), F32)
    t_prev = jnp.zeros((DIL_BLOCK, DIL_BLOCK), F32)
    for b, lo, hi in bands:
        val = rb_ref[b, h]
        t_cur = jnp.where((n_cur >= lo) & (n_cur <= hi), val, t_cur)
        t_prev = jnp.where((n_prev >= lo) & (n_prev <= hi), val, t_prev)
    t_cur = jnp.where(n_cur >= 0, t_cur, NEG_INF)
    t_prev = jnp.where(n_prev <= span, t_prev, NEG_INF)
    return t_cur, t_prev


def _dil_rows(start, dilation):
    return pl.ds(start, DIL_BLOCK, stride=dilation) if dilation > 1 else pl.ds(start, DIL_BLOCK)


def _dil_fwd(qkv, rel_bias, n_heads, col0, *, name):
    s = qkv.shape[0]
    e = HEAD_DIM
    scale = e ** -0.5
    n_pat = len(DIL_PATTERNS)
    for window, d in DIL_PATTERNS:
        assert window // d == DIL_BLOCK and s % (d * DIL_BLOCK) == 0
    bands = [_bucket_bands(d, w // d) for w, d in DIL_PATTERNS]

    def body(rb_ref, q_ref, k_ref, v_ref, y_ref, lse_ref, qf, kf, vf, *scr):
        o_scr, l_scr = scr[:n_pat], scr[n_pat:]
        h = pl.program_id(0)
        qf[...] = q_ref[...].astype(F32)
        kf[...] = k_ref[...].astype(F32)
        vf[...] = v_ref[...].astype(F32)
        for pi, (window, d) in enumerate(DIL_PATTERNS):
            t_cur, t_prev = _bias_tiles(rb_ref, h, bands[pi], window // d)
            group = d * DIL_BLOCK
            for r in range(d):
                def blk(n, carry, r=r, d=d, pi=pi, group=group, t_cur=t_cur, t_prev=t_prev):
                    q0 = pl.multiple_of(n * group, DIL_BLOCK) + r
                    p0 = pl.multiple_of(jnp.maximum(n - 1, 0) * group, DIL_BLOCK) + r
                    cur, prev = _dil_rows(q0, d), _dil_rows(p0, d)
                    qb = qf[cur, :].astype(BF16)
                    s_c = _nt(qb, kf[cur, :].astype(BF16)) * scale + t_cur
                    s_p = _nt(qb, kf[prev, :].astype(BF16)) * scale + (t_prev + jnp.where(n == 0, NEG_INF, 0.0))
                    m = jnp.maximum(jnp.max(s_c, axis=-1, keepdims=True), jnp.max(s_p, axis=-1, keepdims=True))
                    p_c, p_p = jnp.exp(s_c - m), jnp.exp(s_p - m)
                    l = jnp.sum(p_c, axis=-1, keepdims=True) + jnp.sum(p_p, axis=-1, keepdims=True)
                    o = _nn(p_c.astype(BF16), vf[cur, :].astype(BF16)) + _nn(p_p.astype(BF16), vf[prev, :].astype(BF16))
                    o_scr[pi][cur, :] = o / l
                    l_scr[pi][cur, :] = m + jnp.log(l)
                    return carry
                lax.fori_loop(0, s // group, blk, 0)
        lses = [l_scr[pi][...] for pi in range(n_pat)]
        m = functools.reduce(jnp.maximum, lses)
        ws = [jnp.exp(l - m) for l in lses]
        tot = functools.reduce(jnp.add, ws)
        y = functools.reduce(jnp.add, [w * o_scr[pi][...] for pi, w in enumerate(ws)])
        y_ref[...] = y / tot
        lse_ref[...] = m + jnp.log(tot)

    h_ = n_heads
    return pl.pallas_call(
        body, name=name, grid=(h_,),
        in_specs=[pl.BlockSpec(memory_space=pltpu.SMEM),
                  pl.BlockSpec((s, e), lambda h: (0, col0 + h)),
                  pl.BlockSpec((s, e), lambda h: (0, col0 + h_ + h)),
                  pl.BlockSpec((s, e), lambda h: (0, col0 + 2 * h_ + h))],
        out_specs=[pl.BlockSpec((s, e), lambda h: (0, h)), pl.BlockSpec((None, s, 1), lambda h: (h, 0, 0))],
        out_shape=[jax.ShapeDtypeStruct((s, h_ * e), F32), jax.ShapeDtypeStruct((h_, s, 1), F32)],
        scratch_shapes=[pltpu.VMEM((s, e), F32)] * 3 + [pltpu.VMEM((s, e), F32)] * n_pat
        + [pltpu.VMEM((s, 1), F32)] * n_pat,
        compiler_params=_params("parallel"),
    )(rel_bias, qkv, qkv, qkv)


def _dil_bwd(qkv, rel_bias, lse, y, dy, n_heads, col0, *, name):
    s = qkv.shape[0]
    e = HEAD_DIM
    scale = e ** -0.5
    bands = [_bucket_bands(d, w // d) for w, d in DIL_PATTERNS]

    def body(rb_ref, q_ref, k_ref, v_ref, lse_ref, y_ref, dy_ref, dq_ref, dk_ref, dv_ref, drb_ref,
             qf, kf, vf, dqf, dkf, dvf, delta, dt_cur, dt_prev):
        h = pl.program_id(0)
        qf[...] = q_ref[...].astype(F32)
        kf[...] = k_ref[...].astype(F32)
        vf[...] = v_ref[...].astype(F32)
        dqf[...] = jnp.zeros_like(dqf)
        dkf[...] = jnp.zeros_like(dkf)
        dvf[...] = jnp.zeros_like(dvf)
        delta[...] = jnp.sum(dy_ref[...] * y_ref[...], axis=-1, keepdims=True)
        lane = lax.broadcasted_iota(jnp.int32, (1, 128), 1)
        drb = jnp.zeros((1, 128), F32)
        n_cur, n_prev = _steps_back()
        for pi, (window, d) in enumerate(DIL_PATTERNS):
            t_cur, t_prev = _bias_tiles(rb_ref, h, bands[pi], window // d)
            group = d * DIL_BLOCK
            dt_cur[...] = jnp.zeros_like(dt_cur)
            dt_prev[...] = jnp.zeros_like(dt_prev)
            for r in range(d):
                def blk(n, carry, r=r, d=d, group=group, t_cur=t_cur, t_prev=t_prev):
                    q0 = pl.multiple_of(n * group, DIL_BLOCK) + r
                    p0 = pl.multiple_of(jnp.maximum(n - 1, 0) * group, DIL_BLOCK) + r
                    cur, prev = _dil_rows(q0, d), _dil_rows(p0, d)
                    qb = qf[cur, :].astype(BF16)
                    kc, kp = kf[cur, :].astype(BF16), kf[prev, :].astype(BF16)
                    vc, vp = vf[cur, :].astype(BF16), vf[prev, :].astype(BF16)
                    lse_b, delta_b = lse_ref[cur, :], delta[cur, :]
                    do = dy_ref[cur, :].astype(BF16)
                    s_c = _nt(qb, kc) * scale + t_cur
                    s_p = _nt(qb, kp) * scale + (t_prev + jnp.where(n == 0, NEG_INF, 0.0))
                    p_c, p_p = jnp.exp(s_c - lse_b), jnp.exp(s_p - lse_b)
                    ds_c = p_c * (_nt(do, vc) - delta_b)
                    ds_p = p_p * (_nt(do, vp) - delta_b)
                    dt_cur[...] += ds_c
                    dt_prev[...] += ds_p
                    p_c, p_p = p_c.astype(BF16), p_p.astype(BF16)
                    ds_c, ds_p = ds_c.astype(BF16), ds_p.astype(BF16)
                    dqf[cur, :] += scale * (_nn(ds_c, kc) + _nn(ds_p, kp))
                    dkf[cur, :] += scale * _tn(ds_c, qb)
                    dvf[cur, :] += _tn(p_c, do)
                    dkf[prev, :] += scale * _tn(ds_p, qb)
                    dvf[prev, :] += _tn(p_p, do)
                    return carry
                lax.fori_loop(0, s // group, blk, 0)
            dtc, dtp = dt_cur[...], dt_prev[...]
            for b, lo, hi in bands[pi]:
                tot = (jnp.sum(jnp.where((n_cur >= lo) & (n_cur <= hi), dtc, 0.0))
                       + jnp.sum(jnp.where((n_prev >= lo) & (n_prev <= hi), dtp, 0.0)))
                drb = drb + jnp.where(lane == b, tot, 0.0)
        dq_ref[...] = dqf[...].astype(BF16)
        dk_ref[...] = dkf[...].astype(BF16)
        dv_ref[...] = dvf[...].astype(BF16)
        drb_ref[...] = drb

    h_ = n_heads
    col = pl.BlockSpec((s, e), lambda h: (0, h))
    return pl.pallas_call(
        body, name=name, grid=(h_,),
        in_specs=[pl.BlockSpec(memory_space=pltpu.SMEM),
                  pl.BlockSpec((s, e), lambda h: (0, col0 + h)),
                  pl.BlockSpec((s, e), lambda h: (0, col0 + h_ + h)),
                  pl.BlockSpec((s, e), lambda h: (0, col0 + 2 * h_ + h)),
                  pl.BlockSpec((None, s, 1), lambda h: (h, 0, 0)), col, col],
        out_specs=[col, col, col, pl.BlockSpec((None, 1, 128), lambda h: (h, 0, 0))],
        out_shape=[jax.ShapeDtypeStruct((s, h_ * e), BF16)] * 3 + [jax.ShapeDtypeStruct((h_, 1, 128), F32)],
        scratch_shapes=[pltpu.VMEM((s, e), F32)] * 6 + [pltpu.VMEM((s, 1), F32)]
        + [pltpu.VMEM((DIL_BLOCK, DIL_BLOCK), F32)] * 2,
        compiler_params=_params("parallel"),
    )(rel_bias, qkv, qkv, qkv, lse, y, dy)


def _place():
    x, y, c = lax.axis_index("x"), lax.axis_index("y"), lax.axis_index("c")
    chips = [(1 - x, y), (x, 1 - y), (1 - x, 1 - y)]
    return x, y, c, chips


def _block_index(px, py, pc):
    return 4 * px + 2 * py + pc


HBM_SPEC = pl.BlockSpec(memory_space=pltpu.HBM)


def _all_gather(blocks, *, name):
    n = len(blocks)

    def body(*refs):
        ins, outs = refs[:n], refs[n:2 * n]
        send_sems, recv_sems, local_sems = refs[2 * n:]
        x, y, c, chips = _place()
        me, sibling = (x, y, c), (x, y, 1 - c)

        def copy(a, k, block, to, src=None):
            slot = outs[a].at[_block_index(*block)]
            return pltpu.make_async_remote_copy(
                src_ref=slot if src is None else src, dst_ref=slot,
                send_sem=send_sems.at[a, k], recv_sem=recv_sems.at[a, k],
                device_id=to, device_id_type=MESH)

        started = []
        mine = []
        for a in range(n):
            loc = pltpu.make_async_copy(ins[a], outs[a].at[_block_index(*me)], local_sems.at[a])
            loc.start()
            mine.append(loc)
            first = [copy(a, 0, me, sibling, src=ins[a])]
            first += [copy(a, 1 + j, me, (*chip, c), src=ins[a]) for j, chip in enumerate(chips)]
            for cp in first:
                cp.start()
            started += first
        for a in range(n):
            for j, chip in enumerate(chips):
                copy(a, 1 + j, (*chip, c), me).wait_recv()
                fwd = copy(a, 4 + j, (*chip, c), sibling)
                fwd.start()
                started.append(fwd)
        for a in range(n):
            copy(a, 0, sibling, me).wait_recv()
            for j, chip in enumerate(chips):
                copy(a, 4 + j, (*chip, 1 - c), me).wait_recv()
        for cp in started:
            cp.wait_send()
        for loc in mine:
            loc.wait()

    return pl.pallas_call(
        body, name=name,
        in_specs=[HBM_SPEC] * n, out_specs=[HBM_SPEC] * n,
        out_shape=[jax.ShapeDtypeStruct((N_DEV,) + b.shape, b.dtype) for b in blocks],
        scratch_shapes=[pltpu.SemaphoreType.DMA((n, 7)), pltpu.SemaphoreType.DMA((n, 7)),
                        pltpu.SemaphoreType.DMA((n,))],
    )(*blocks)


def _exchange_sibling(grads, *, name):
    n = len(grads)

    def body(*refs):
        ins, outs = refs[:n], refs[n:2 * n]
        send_sems, recv_sems = refs[2 * n:]
        x, y, c, _ = _place()
        sibling = (x, y, 1 - c)
        copies = []
        for a in range(n):
            for q in range(4):
                cp = pltpu.make_async_remote_copy(
                    src_ref=ins[a].at[2 * q + (1 - c)], dst_ref=outs[a].at[q],
                    send_sem=send_sems.at[a, q], recv_sem=recv_sems.at[a, q],
                    device_id=sibling, device_id_type=MESH)
                cp.start()
                copies.append(cp)
        for cp in copies:
            cp.wait()

    return pl.pallas_call(
        body, name=name,
        in_specs=[HBM_SPEC] * n, out_specs=[HBM_SPEC] * n,
        out_shape=[jax.ShapeDtypeStruct((4,) + g.shape[1:], g.dtype) for g in grads],
        scratch_shapes=[pltpu.SemaphoreType.DMA((n, 4)), pltpu.SemaphoreType.DMA((n, 4))],
    )(*grads)


def _chip_partials(grad, got, *, name):
    _, r, cdim = grad.shape
    tr = _row_tile(r, 512)

    def body(g_ref, s_ref, o_ref):
        o_ref[...] = (g_ref[...] + s_ref[...]).astype(BF16)

    return pl.pallas_call(
        body, name=name, grid=(4, r // tr),
        in_specs=[pl.BlockSpec((None, tr, cdim), lambda q, i: (2 * q + lax.axis_index("c"), i, 0)),
                  pl.BlockSpec((None, tr, cdim), lambda q, i: (q, i, 0))],
        out_specs=pl.BlockSpec((None, tr, cdim), lambda q, i: (q, i, 0)),
        out_shape=jax.ShapeDtypeStruct((4, r, cdim), BF16),
        compiler_params=_params("parallel", "parallel"),
    )(grad, got)


def _exchange_chips(parts, *, name):
    n = len(parts)

    def body(*refs):
        ins, outs = refs[:n], refs[n:2 * n]
        send_sems, recv_sems = refs[2 * n:]
        x, y, c, chips = _place()
        copies = []
        for a in range(n):
            for j, (px, py) in enumerate(chips):
                cp = pltpu.make_async_remote_copy(
                    src_ref=ins[a].at[2 * px + py], dst_ref=outs[a].at[j],
                    send_sem=send_sems.at[a, j], recv_sem=recv_sems.at[a, j],
                    device_id=(px, py, c), device_id_type=MESH)
                cp.start()
                copies.append(cp)
        for cp in copies:
            cp.wait()

    return pl.pallas_call(
        body, name=name,
        in_specs=[HBM_SPEC] * n, out_specs=[HBM_SPEC] * n,
        out_shape=[jax.ShapeDtypeStruct((3,) + p.shape[1:], p.dtype) for p in parts],
        scratch_shapes=[pltpu.SemaphoreType.DMA((n, 3)), pltpu.SemaphoreType.DMA((n, 3))],
    )(*parts)


def _all_reduce_small(v, *, name):
    r, w = v.shape

    def body(v_ref, o_ref, buf, send_sems, recv_sems):
        x, y, c, _ = _place()
        me = _block_index(x, y, c)
        buf[me] = v_ref[...]
        copies = []
        for k in range(1, N_DEV):
            fx, fy, fc = (k >> 2) & 1, (k >> 1) & 1, k & 1
            peer = (x ^ fx, y ^ fy, c ^ fc)
            cp = pltpu.make_async_remote_copy(
                src_ref=v_ref, dst_ref=buf.at[me],
                send_sem=send_sems.at[k - 1], recv_sem=recv_sems.at[k - 1],
                device_id=peer, device_id_type=MESH)
            cp.start()
            copies.append(cp)
        for cp in copies:
            cp.wait()
        acc = buf[0]
        for p in range(1, N_DEV):
            acc = acc + buf[p]
        o_ref[...] = acc

    vmem = pl.BlockSpec(memory_space=pltpu.VMEM)
    return pl.pallas_call(
        body, name=name, in_specs=[vmem], out_specs=vmem,
        out_shape=jax.ShapeDtypeStruct((r, w), F32),
        scratch_shapes=[pltpu.VMEM((N_DEV, r, w), F32), pltpu.SemaphoreType.DMA((N_DEV - 1,)),
                        pltpu.SemaphoreType.DMA((N_DEV - 1,))],
    )(v)


def _adamw_math(w, g, m, v):
    m = ADAM_B1 * m + (1.0 - ADAM_B1) * g
    v = ADAM_B2 * v + (1.0 - ADAM_B2) * (g * g)
    m_hat = m / (1.0 - ADAM_B1 ** ADAM_STEP)
    v_hat = v / (1.0 - ADAM_B2 ** ADAM_STEP)
    delta = -ADAM_LR * (m_hat / (jnp.sqrt(v_hat) + ADAM_EPS) + ADAM_WD * w)
    return delta, m, v


def _adamw_sharded(w, m, v, part, got, *, name):
    n_l, r, cdim = w.shape
    tr = _row_tile(r, 256)
    mine = lambda: 2 * lax.axis_index("x") + lax.axis_index("y")

    def body(*refs):
        w_ref, m_ref, v_ref = refs[:3]
        p_refs, g_refs = refs[3:3 + n_l], refs[3 + n_l:3 + 2 * n_l]
        g_out, d_out, m_out, v_out = refs[3 + 2 * n_l:]
        for l in range(n_l):
            @pl.when(pl.program_id(0) == l)
            def _(l=l):
                g = p_refs[l][...].astype(F32)
                for j in range(3):
                    g = g + g_refs[l][j].astype(F32)
                delta, m_new, v_new = _adamw_math(w_ref[...], g, m_ref[...], v_ref[...])
                g_out[...] = g
                d_out[...] = delta
                m_out[...] = m_new
                v_out[...] = v_new

    shard = pl.BlockSpec((None, tr, cdim), lambda l, i: (l, i, 0))
    p_specs = [pl.BlockSpec((None, tr, cdim), lambda l, i: (mine(), i, 0))] * n_l
    g_specs = [pl.BlockSpec((3, tr, cdim), lambda l, i: (0, i, 0))] * n_l
    return pl.pallas_call(
        body, name=name, grid=(n_l, r // tr),
        in_specs=[shard] * 3 + p_specs + g_specs, out_specs=[shard] * 4,
        out_shape=[jax.ShapeDtypeStruct(w.shape, F32)] * 4,
        compiler_params=_params("arbitrary", "parallel"),
    )(w, m, v, *part, *got)


def _adamw_small(w, g, m, v, *, name):
    def body(w_ref, g_ref, m_ref, v_ref, d_out, m_out, v_out):
        delta, m_new, v_new = _adamw_math(w_ref[...], g_ref[...], m_ref[...], v_ref[...])
        d_out[...] = delta
        m_out[...] = m_new
        v_out[...] = v_new

    vmem = pl.BlockSpec(memory_space=pltpu.VMEM)
    return pl.pallas_call(
        body, name=name, in_specs=[vmem] * 4, out_specs=[vmem] * 3,
        out_shape=[jax.ShapeDtypeStruct(w.shape, F32)] * 3,
    )(w, g, m, v)


def _pad_lanes(a, width=128):
    return jnp.pad(a, ((0, 0), (0, width - a.shape[1])))


def _local_step(x, target, norm1_g, forget_b, rel_bias, outnorm_a_g, outnorm_b_g, norm2_g, final_norm_g,
                w_qkv, w_f, w_out, w_mlp_in, w_mlp_out):
    depth = len(w_qkv)
    n_heads = w_qkv[0].shape[1] // (6 * HEAD_DIM)
    dh = n_heads * HEAD_DIM
    row = lambda a: a.reshape(1, -1)
    saved = []
    for l in range(depth):
        h1 = _rmsnorm_fwd(x, row(norm1_g[l]), name=f"norm1_fwd_{l}")
        qkv = _matmul(h1, w_qkv[l], out_dtype=BF16, name=f"qkv_proj_{l}")
        f = _matmul(h1, w_f[l], name=f"gate_proj_{l}")
        fb = _pad_lanes(row(forget_b[l]))
        c = _gates_fwd(f, fb, name=f"gates_fwd_{l}")
        c_heads = c[:, :n_heads].T
        cq, ck = c_heads[:, :, None], c_heads[:, None, :]
        ya, lse_a = _fox_fwd(qkv, cq, ck, n_heads, name=f"fox_fwd_{l}")
        yb, lse_b = _dil_fwd(qkv, rel_bias, n_heads, 3 * n_heads, name=f"dil_fwd_{l}")
        mixed = _outnorm_fwd(ya, yb, row(outnorm_a_g[l]), row(outnorm_b_g[l]), name=f"outnorm_fwd_{l}")
        x1 = _matmul(mixed, w_out[l], addend=x, name=f"out_proj_{l}")
        h2 = _rmsnorm_fwd(x1, row(norm2_g[l]), name=f"norm2_fwd_{l}")
        relu, act = _matmul(h2, w_mlp_in[l], out_dtype=BF16, mode="relu_sq", name=f"mlp_in_{l}")
        x2 = _matmul(act, w_mlp_out[l], addend=x1, name=f"mlp_out_{l}")
        saved.append((x, h1, qkv, f, fb, cq, ck, ya, lse_a, yb, lse_b, mixed, x1, h2, relu, act))
        x = x2
    dx, loss_tile, d_final = _loss_head(x, row(final_norm_g), target, name="loss_head")

    grads = []
    for l in reversed(range(depth)):
        x0, h1, qkv, f, fb, cq, ck, ya, lse_a, yb, lse_b, mixed, x1, h2, relu, act = saved[l]
        du = _matmul(dx, w_mlp_out[l], tb=True, out_dtype=BF16, mode="mul2", aux=relu, name=f"mlp_out_dx_{l}")
        dw_mlp_out = _matmul(act, dx, ta=True, tk=1024, name=f"mlp_out_dw_{l}")
        dh2 = _matmul(du, w_mlp_in[l], tb=True, name=f"mlp_in_dx_{l}")
        dw_mlp_in = _matmul(h2, du, ta=True, tk=1024, out_chunks=N_DEV, name=f"mlp_in_dw_{l}")
        dx1, d_norm2 = _rmsnorm_bwd(x1, row(norm2_g[l]), dh2, dx, name=f"norm2_bwd_{l}")
        dmixed = _matmul(dx1, w_out[l], tb=True, name=f"out_proj_dx_{l}")
        dw_out = _matmul(mixed, dx1, ta=True, tk=1024, name=f"out_proj_dw_{l}")
        dya, dyb, d_ga, d_gb = _outnorm_bwd(ya, yb, row(outnorm_a_g[l]), row(outnorm_b_g[l]), dmixed,
                                            name=f"outnorm_bwd_{l}")
        dqa, dka, dva, dcq, dck = _fox_bwd(qkv, cq, ck, lse_a, ya, dya, n_heads, name=f"fox_bwd_{l}")
        dqb, dkb, dvb, d_rb = _dil_bwd(qkv, rel_bias, lse_b, yb, dyb, n_heads, 3 * n_heads, name=f"dil_bwd_{l}")
        dc = _pad_lanes((dcq[:, :, 0] + dck[:, 0, :]).T)
        df, dfb = _gates_bwd(f, fb, dc, name=f"gates_bwd_{l}")
        dqkv = jnp.concatenate([dqa, dka, dva, dqb, dkb, dvb], axis=1)
        dh1 = _matmul(df, w_f[l], tb=True, name=f"gate_proj_dx_{l}")
        dh1 = _matmul(dqkv, w_qkv[l], tb=True, addend=dh1, name=f"qkv_proj_dx_{l}")
        dw_qkv = _matmul(h1, dqkv, ta=True, tk=1024, name=f"qkv_proj_dw_{l}")
        dw_f = _matmul(h1, df, ta=True, tk=1024, name=f"gate_proj_dw_{l}")
        dx, d_norm1 = _rmsnorm_bwd(x0, row(norm1_g[l]), dh1, dx1, name=f"norm1_bwd_{l}")
        grads.append(dict(norm1_g=d_norm1[0], forget_b=dfb[0, :n_heads], rel_bias=d_rb[:, 0, :REL_BUCKETS].T,
                          outnorm_a_g=d_ga[0], outnorm_b_g=d_gb[0], norm2_g=d_norm2[0],
                          w_qkv=dw_qkv, w_f=dw_f[:, :n_heads], w_out=dw_out, w_mlp_in=dw_mlp_in,
                          w_mlp_out=dw_mlp_out))
    grads.reverse()
    return loss_tile[0, 0], dx, d_final[0], grads


def _pack_small(parts, rows):
    flat = jnp.concatenate([p.reshape(-1).astype(F32) for p in parts])
    return jnp.pad(flat, (0, rows * 128 - flat.shape[0])).reshape(rows, 128)


def _unpack_small(packed, shapes):
    flat = packed.reshape(-1)
    out, pos = [], 0
    for shp in shapes:
        size = int(np.prod(shp))
        out.append(flat[pos:pos + size].reshape(shp))
        pos += size
    return out


def kernel(x, norm1_g, w_in, forget_b, rel_bias, outnorm_a_g, outnorm_b_g, w_out, norm2_g, w_mlp_in, w_mlp_out, final_norm_g, loss_target, m_norm1_g, m_w_in, m_forget_b, m_rel_bias, m_outnorm_a_g, m_outnorm_b_g, m_w_out, m_norm2_g, m_w_mlp_in, m_w_mlp_out, m_final_norm_g, v_norm1_g, v_w_in, v_forget_b, v_rel_bias, v_outnorm_a_g, v_outnorm_b_g, v_w_out, v_norm2_g, v_w_mlp_in, v_w_mlp_out, v_final_norm_g):
    depth, d_model, in_shard = w_in.shape
    n_in = in_shard * N_DEV
    n_heads = forget_b.shape[1]
    dh = n_heads * HEAD_DIM
    assert n_in == 6 * dh + n_heads and x.shape[0] == 1
    d_ff = w_mlp_in.shape[2] * N_DEV

    blocks = []
    for l in range(depth):
        blocks += [w_in[l].astype(BF16), w_out[l].astype(BF16), w_mlp_in[l].astype(BF16), w_mlp_out[l].astype(BF16)]
    gathered = _all_gather(blocks, name="gather_weights")
    w_qkv, w_f, w_out_full, w_mlp_in_full, w_mlp_out_full = [], [], [], [], []
    for l in range(depth):
        g_in, g_out, g_mi, g_mo = gathered[4 * l:4 * l + 4]
        full_in = g_in.transpose(1, 0, 2).reshape(d_model, n_in)
        w_qkv.append(jnp.concatenate([full_in[:, :3 * dh], full_in[:, 3 * dh + n_heads:]], axis=1))
        w_f.append(_pad_lanes(full_in[:, 3 * dh:3 * dh + n_heads]))
        w_out_full.append(g_out.reshape(2 * dh, d_model))
        w_mlp_in_full.append(g_mi.transpose(1, 0, 2).reshape(d_model, d_ff))
        w_mlp_out_full.append(g_mo.reshape(d_ff, d_model))

    loss_part, dx, d_final, grads = _local_step(
        x[0], loss_target[0], norm1_g, forget_b, rel_bias, outnorm_a_g, outnorm_b_g, norm2_g, final_norm_g,
        w_qkv, w_f, w_out_full, w_mlp_in_full, w_mlp_out_full)

    small_names = ["norm1_g", "forget_b", "rel_bias", "outnorm_a_g", "outnorm_b_g", "norm2_g", "final_norm_g"]
    small_w = dict(norm1_g=norm1_g, forget_b=forget_b, rel_bias=rel_bias, outnorm_a_g=outnorm_a_g,
                   outnorm_b_g=outnorm_b_g, norm2_g=norm2_g, final_norm_g=final_norm_g)
    small_m = dict(norm1_g=m_norm1_g, forget_b=m_forget_b, rel_bias=m_rel_bias, outnorm_a_g=m_outnorm_a_g,
                   outnorm_b_g=m_outnorm_b_g, norm2_g=m_norm2_g, final_norm_g=m_final_norm_g)
    small_v = dict(norm1_g=v_norm1_g, forget_b=v_forget_b, rel_bias=v_rel_bias, outnorm_a_g=v_outnorm_a_g,
                   outnorm_b_g=v_outnorm_b_g, norm2_g=v_norm2_g, final_norm_g=v_final_norm_g)
    small_g = {k: jnp.stack([g[k] for g in grads]) for k in small_names if k not in ("rel_bias", "final_norm_g")}
    small_g["rel_bias"] = functools.reduce(jnp.add, [g["rel_bias"] for g in grads])
    small_g["final_norm_g"] = d_final
    shapes = [small_w[k].shape for k in small_names]
    total = sum(int(np.prod(s)) for s in shapes) + 1
    rows = -(-total // (8 * 128)) * 8
    packed_g = _pack_small([small_g[k] for k in small_names] + [loss_part], rows)
    packed_g = _all_reduce_small(packed_g, name="reduce_small")
    zero = jnp.zeros((1,), F32)
    packed_w = _pack_small([small_w[k] for k in small_names] + [zero], rows)
    packed_m = _pack_small([small_m[k] for k in small_names] + [zero], rows)
    packed_v = _pack_small([small_v[k] for k in small_names] + [zero + 1.0], rows)
    packed_d, packed_nm, packed_nv = _adamw_small(packed_w, packed_g, packed_m, packed_v, name="adamw_small")
    g_small = dict(zip(small_names, _unpack_small(packed_g, shapes)))
    d_small = dict(zip(small_names, _unpack_small(packed_d, shapes)))
    nm_small = dict(zip(small_names, _unpack_small(packed_nm, shapes)))
    nv_small = dict(zip(small_names, _unpack_small(packed_nv, shapes)))
    loss = packed_g.reshape(-1)[total - 1]

    big = []
    for l in range(depth):
        g = grads[l]
        dw_in = jnp.concatenate([g["w_qkv"][:, :3 * dh], g["w_f"], g["w_qkv"][:, 3 * dh:]], axis=1)
        big += [dw_in.reshape(d_model, N_DEV, in_shard).transpose(1, 0, 2),
                g["w_out"].reshape(N_DEV, 2 * dh // N_DEV, d_model),
                g["w_mlp_in"],
                g["w_mlp_out"].reshape(N_DEV, d_ff // N_DEV, d_model)]
    from_sibling = _exchange_sibling(big, name="reduce_sibling")
    parts = [_chip_partials(g, s, name=f"chip_partials_{i}") for i, (g, s) in enumerate(zip(big, from_sibling))]
    from_chips = _exchange_chips(parts, name="reduce_chips")
    big_out = {}
    for i, (nm, w, m, v) in enumerate([("w_in", w_in, m_w_in, v_w_in), ("w_out", w_out, m_w_out, v_w_out),
                                       ("w_mlp_in", w_mlp_in, m_w_mlp_in, v_w_mlp_in),
                                       ("w_mlp_out", w_mlp_out, m_w_mlp_out, v_w_mlp_out)]):
        big_out[nm] = _adamw_sharded(w, m, v, [parts[4 * l + i] for l in range(depth)],
                                     [from_chips[4 * l + i] for l in range(depth)], name=f"adamw_{nm}")

    order = ["norm1_g", "w_in", "forget_b", "rel_bias", "outnorm_a_g", "outnorm_b_g", "w_out", "norm2_g",
             "w_mlp_in", "w_mlp_out", "final_norm_g"]
    pick = lambda k, idx, small: big_out[k][idx] if k in big_out else small[k]
    outs = [loss, dx[None]]
    outs += [pick(k, 0, g_small) for k in order]
    outs += [pick(k, 1, d_small) for k in order]
    outs += [pick(k, 2, nm_small) for k in order]
    outs += [pick(k, 3, nv_small) for k in order]
    return tuple(outs)
```

```python
import functools
import math

import numpy as np
import jax
import jax.numpy as jnp
from jax import lax
from jax.experimental import pallas as pl
from jax.experimental.pallas import tpu as pltpu

F32 = jnp.float32
BF16 = jnp.bfloat16

HEAD_DIM = 128
DIL_PATTERNS = ((128, 1), (512, 4), (2048, 16))
DIL_BLOCK = 128
REL_BUCKETS = 32
REL_MAX_DISTANCE = 2048
NORM_EPS = 1e-6
NEG_INF = -1e30
ADAM_LR = 0.001
ADAM_B1 = 0.9
ADAM_B2 = 0.999
ADAM_EPS = 1e-08
ADAM_WD = 0.01
ADAM_STEP = 10

N_DEV = 8
V7X_VMEM_LIMIT_BYTES = 56 * 1024 * 1024
MESH = pl.DeviceIdType.MESH


def _params(*semantics):
    return pltpu.CompilerParams(dimension_semantics=semantics, vmem_limit_bytes=V7X_VMEM_LIMIT_BYTES)


HBM_SPEC = pl.BlockSpec(memory_space=pltpu.HBM)


class _Carried:
    def __init__(self, operands, out_shape, scratch, start, forward, finish):
        self.operands, self.out_shape, self.scratch = list(operands), list(out_shape), list(scratch)
        self.start, self.forward, self.finish = start, forward, finish


FORWARD_AT = 0.8


def _call(body, *, name, grid, in_specs, out_specs, out_shape, operands, scratch=(), semantics, carry=None):
    in_specs, out_specs, out_shape, scratch = list(in_specs), list(out_specs), list(out_shape), list(scratch)
    if carry is None:
        return pl.pallas_call(
            body, name=name, grid=grid, in_specs=in_specs, out_specs=out_specs, out_shape=out_shape,
            scratch_shapes=scratch, compiler_params=_params(*semantics))(*operands)
    n_in, n_out = len(in_specs), len(out_specs)
    ci, co, cs = len(carry.operands), len(carry.out_shape), len(carry.scratch)
    steps = int(np.prod(grid))
    forward_step = min(int(steps * FORWARD_AT), steps - 1)

    def carrying(*refs):
        main_in, c_in = refs[:n_in], refs[n_in:n_in + ci]
        main_out = refs[n_in + ci:n_in + ci + n_out]
        c_out = refs[n_in + ci + n_out:n_in + ci + n_out + co]
        rest = refs[n_in + ci + n_out + co:]
        main_scr, c_scr = rest[:len(rest) - cs], rest[len(rest) - cs:]
        step = 0
        for axis, extent in enumerate(grid):
            step = step * extent + pl.program_id(axis)

        @pl.when(step == 0)
        def _():
            carry.start(c_in, c_out, c_scr)

        body(*main_in, *main_out, *main_scr)

        @pl.when(step == forward_step)
        def _():
            carry.forward(c_in, c_out, c_scr)

        @pl.when(step == steps - 1)
        def _():
            carry.finish(c_in, c_out, c_scr)

    res = pl.pallas_call(
        carrying, name=name, grid=grid, in_specs=in_specs + [HBM_SPEC] * ci,
        out_specs=out_specs + [HBM_SPEC] * co, out_shape=out_shape + carry.out_shape,
        scratch_shapes=scratch + carry.scratch,
        compiler_params=_params(*(["arbitrary"] * len(grid))))(*operands, *carry.operands)
    return res[:n_out], res[n_out:]


def _run_carried(carry, *, name):
    ci, co = len(carry.operands), len(carry.out_shape)

    def body(*refs):
        parts = (refs[:ci], refs[ci:ci + co], refs[ci + co:])
        carry.start(*parts)
        carry.forward(*parts)
        carry.finish(*parts)

    return pl.pallas_call(
        body, name=name, in_specs=[HBM_SPEC] * ci, out_specs=[HBM_SPEC] * co,
        out_shape=carry.out_shape, scratch_shapes=carry.scratch)(*carry.operands)


def _fit(dim, want, unit=128):
    if dim <= want:
        return dim
    t = want - want % unit
    while dim % t:
        t -= unit
    return t


def _matmul(a, b, *, name, ta=False, tb=False, out_dtype=F32, tm=1024, tn=1024, tk=2048,
            addend=None, mode=None, aux=None, out_chunks=None, carry=None):
    m_dim, k_dim = (a.shape[1], a.shape[0]) if ta else a.shape
    n_dim = b.shape[0] if tb else b.shape[1]
    assert (b.shape[1] if tb else b.shape[0]) == k_dim
    tm, tn, tk = _fit(m_dim, tm), _fit(n_dim // (out_chunks or 1), tn), _fit(k_dim, tk)
    assert m_dim % tm == 0 and n_dim % tn == 0 and k_dim % tk == 0, (name, a.shape, b.shape)
    nk = k_dim // tk
    a_spec = (pl.BlockSpec((tk, tm), lambda i, j, k: (k, i)) if ta
              else pl.BlockSpec((tm, tk), lambda i, j, k: (i, k)))
    b_spec = (pl.BlockSpec((tn, tk), lambda i, j, k: (j, k)) if tb
              else pl.BlockSpec((tk, tn), lambda i, j, k: (k, j)))
    mn_spec = pl.BlockSpec((tm, tn), lambda i, j, k: (i, j))
    if out_chunks is None:
        o_spec, o_shape = mn_spec, (m_dim, n_dim)
    else:
        per = n_dim // out_chunks // tn
        assert per * tn * out_chunks == n_dim
        o_spec = pl.BlockSpec((None, tm, tn), lambda i, j, k: (j // per, i, j % per))
        o_shape = (out_chunks, m_dim, n_dim // out_chunks)
    dims = (((0 if ta else 1,), (1 if tb else 0,)), ((), ()))
    n_out = 2 if mode == "relu_sq" else 1
    in_specs, operands = [a_spec, b_spec], [a, b]
    if addend is not None:
        in_specs.append(mn_spec)
        operands.append(addend)
    if mode == "mul2":
        in_specs.append(mn_spec)
        operands.append(aux)

    def body(*refs):
        a_ref, b_ref = refs[0], refs[1]
        pos = 2
        add_ref = aux_ref = None
        if addend is not None:
            add_ref, pos = refs[pos], pos + 1
        if mode == "mul2":
            aux_ref, pos = refs[pos], pos + 1
        outs = refs[pos:pos + n_out]

        def finish(acc):
            if add_ref is not None:
                acc = acc + add_ref[...].astype(F32)
            if mode == "relu_sq":
                r = jnp.maximum(acc, 0.0)
                outs[0][...] = r.astype(outs[0].dtype)
                outs[1][...] = (r * r).astype(outs[1].dtype)
            elif mode == "mul2":
                outs[0][...] = (acc * (2.0 * aux_ref[...].astype(F32))).astype(outs[0].dtype)
            else:
                outs[0][...] = acc.astype(outs[0].dtype)

        part = lax.dot_general(a_ref[...].astype(BF16), b_ref[...].astype(BF16), dims,
                               preferred_element_type=F32)
        if nk == 1:
            finish(part)
        else:
            acc_ref = refs[-1]
            k = pl.program_id(2)

            @pl.when(k == 0)
            def _():
                acc_ref[...] = part

            @pl.when(k > 0)
            def _():
                acc_ref[...] += part

            @pl.when(k == nk - 1)
            def _():
                finish(acc_ref[...])

    out_shape = [jax.ShapeDtypeStruct(o_shape, out_dtype)] * n_out
    res = _call(
        body, name=name, grid=(m_dim // tm, n_dim // tn, nk),
        in_specs=in_specs, out_specs=[o_spec] * n_out, out_shape=out_shape, operands=operands,
        scratch=[pltpu.VMEM((tm, tn), F32)] if nk > 1 else [],
        semantics=("parallel", "parallel", "arbitrary"), carry=carry)
    if carry is not None:
        res, carried = res
        return (res if n_out > 1 else res[0]), carried
    return res if n_out > 1 else res[0]


def _norm_fwd_math(x, g):
    r = lax.rsqrt(jnp.mean(x * x, axis=-1, keepdims=True) + NORM_EPS)
    return (x * r) * g


def _norm_bwd_math(x, g, dy):
    r = lax.rsqrt(jnp.mean(x * x, axis=-1, keepdims=True) + NORM_EPS)
    xh = x * r
    dxh = dy * g
    dx = r * (dxh - xh * jnp.mean(dxh * xh, axis=-1, keepdims=True))
    return dx, jnp.sum(dy * xh, axis=0, keepdims=True)


def _row_tile(rows, want=256):
    t = min(rows, want)
    assert rows % t == 0
    return t


def _rmsnorm_fwd(x, g, *, name):
    s, d = x.shape
    tr = _row_tile(s)

    def body(x_ref, g_ref, h_ref):
        h_ref[...] = _norm_fwd_math(x_ref[...], g_ref[...]).astype(BF16)

    return pl.pallas_call(
        body, name=name, grid=(s // tr,),
        in_specs=[pl.BlockSpec((tr, d), lambda i: (i, 0)), pl.BlockSpec((1, d), lambda i: (0, 0))],
        out_specs=pl.BlockSpec((tr, d), lambda i: (i, 0)),
        out_shape=jax.ShapeDtypeStruct((s, d), BF16),
        compiler_params=_params("parallel"),
    )(x, g)


def _rmsnorm_bwd(x, g, dh, dres, *, name, carry=None):
    s, d = x.shape
    tr = _row_tile(s)

    def body(x_ref, g_ref, dh_ref, dres_ref, dx_ref, dg_ref):
        dx, dg = _norm_bwd_math(x_ref[...], g_ref[...], dh_ref[...])
        dx_ref[...] = dres_ref[...] + dx

        @pl.when(pl.program_id(0) == 0)
        def _():
            dg_ref[...] = dg

        @pl.when(pl.program_id(0) > 0)
        def _():
            dg_ref[...] += dg

    row = pl.BlockSpec((tr, d), lambda i: (i, 0))
    vec = pl.BlockSpec((1, d), lambda i: (0, 0))
    return _call(
        body, name=name, grid=(s // tr,),
        in_specs=[row, vec, row, row], out_specs=[row, vec],
        out_shape=[jax.ShapeDtypeStruct((s, d), F32), jax.ShapeDtypeStruct((1, d), F32)],
        operands=(x, g, dh, dres), semantics=("arbitrary",), carry=carry)


def _outnorm_fwd(ya, yb, ga, gb, *, name):
    s, da = ya.shape
    db = yb.shape[1]
    tr = _row_tile(s)

    def body(ya_ref, yb_ref, ga_ref, gb_ref, o_ref):
        o_ref[:, :da] = _norm_fwd_math(ya_ref[...], ga_ref[...]).astype(BF16)
        o_ref[:, da:] = _norm_fwd_math(yb_ref[...], gb_ref[...]).astype(BF16)

    return pl.pallas_call(
        body, name=name, grid=(s // tr,),
        in_specs=[pl.BlockSpec((tr, da), lambda i: (i, 0)), pl.BlockSpec((tr, db), lambda i: (i, 0)),
                  pl.BlockSpec((1, da), lambda i: (0, 0)), pl.BlockSpec((1, db), lambda i: (0, 0))],
        out_specs=pl.BlockSpec((tr, da + db), lambda i: (i, 0)),
        out_shape=jax.ShapeDtypeStruct((s, da + db), BF16),
        compiler_params=_params("parallel"),
    )(ya, yb, ga, gb)


def _outnorm_bwd(ya, yb, ga, gb, dmixed, *, name):
    s, da = ya.shape
    db = yb.shape[1]
    tr = _row_tile(s)

    def body(ya_ref, yb_ref, ga_ref, gb_ref, dm_ref, dya_ref, dyb_ref, dga_ref, dgb_ref):
        dxa, dga = _norm_bwd_math(ya_ref[...], ga_ref[...], dm_ref[:, :da])
        dxb, dgb = _norm_bwd_math(yb_ref[...], gb_ref[...], dm_ref[:, da:])
        dya_ref[...] = dxa
        dyb_ref[...] = dxb

        @pl.when(pl.program_id(0) == 0)
        def _():
            dga_ref[...] = dga
            dgb_ref[...] = dgb

        @pl.when(pl.program_id(0) > 0)
        def _():
            dga_ref[...] += dga
            dgb_ref[...] += dgb

    ra = pl.BlockSpec((tr, da), lambda i: (i, 0))
    rb = pl.BlockSpec((tr, db), lambda i: (i, 0))
    va = pl.BlockSpec((1, da), lambda i: (0, 0))
    vb = pl.BlockSpec((1, db), lambda i: (0, 0))
    return pl.pallas_call(
        body, name=name, grid=(s // tr,),
        in_specs=[ra, rb, va, vb, pl.BlockSpec((tr, da + db), lambda i: (i, 0))],
        out_specs=[ra, rb, va, vb],
        out_shape=[jax.ShapeDtypeStruct((s, da), F32), jax.ShapeDtypeStruct((s, db), F32),
                   jax.ShapeDtypeStruct((1, da), F32), jax.ShapeDtypeStruct((1, db), F32)],
        compiler_params=_params("arbitrary"),
    )(ya, yb, ga, gb, dmixed)


def _loss_head(x, g, target, *, name):
    s, d = x.shape
    tr = _row_tile(s)

    def body(x_ref, g_ref, t_ref, dx_ref, loss_ref, dg_ref):
        xv, gv = x_ref[...], g_ref[...]
        err = _norm_fwd_math(xv, gv) - t_ref[...]
        part = 0.5 * jnp.sum(jnp.mean(err * err, axis=-1, keepdims=True), axis=0, keepdims=True)
        dx, dg = _norm_bwd_math(xv, gv, err * (1.0 / d))
        dx_ref[...] = dx
        part = jnp.broadcast_to(part, (8, 128))

        @pl.when(pl.program_id(0) == 0)
        def _():
            dg_ref[...] = dg
            loss_ref[...] = part

        @pl.when(pl.program_id(0) > 0)
        def _():
            dg_ref[...] += dg
            loss_ref[...] += part

    row = pl.BlockSpec((tr, d), lambda i: (i, 0))
    vec = pl.BlockSpec((1, d), lambda i: (0, 0))
    return pl.pallas_call(
        body, name=name, grid=(s // tr,),
        in_specs=[row, vec, row],
        out_specs=[row, pl.BlockSpec((8, 128), lambda i: (0, 0)), vec],
        out_shape=[jax.ShapeDtypeStruct((s, d), F32), jax.ShapeDtypeStruct((8, 128), F32),
                   jax.ShapeDtypeStruct((1, d), F32)],
        compiler_params=_params("arbitrary"),
    )(x, g, target)


def _split3(x):
    hi = x.astype(BF16)
    rem = x - hi.astype(F32)
    mid = rem.astype(BF16)
    lo = (rem - mid.astype(F32)).astype(BF16)
    return hi, mid, lo


def _tri_sum(tri, x):
    hi, mid, lo = _split3(x)
    dot = functools.partial(jnp.dot, preferred_element_type=F32)
    return dot(tri, hi) + dot(tri, mid) + dot(tri, lo)


def _gates_fwd(f, fb, *, name):
    s, w = f.shape
    tb = 128
    nb = s // tb

    def body(f_ref, fb_ref, c_ref, carry):
        @pl.when(pl.program_id(0) == 0)
        def _():
            carry[...] = jnp.zeros_like(carry)

        logf = jax.nn.log_sigmoid(f_ref[...] + fb_ref[...])
        row = lax.broadcasted_iota(jnp.int32, (tb, tb), 0)
        col = lax.broadcasted_iota(jnp.int32, (tb, tb), 1)
        tri = (row >= col).astype(BF16)
        c = _tri_sum(tri, logf) + carry[...]
        c_ref[...] = c
        carry[...] = c[tb - 1:tb, :]

    return pl.pallas_call(
        body, name=name, grid=(nb,),
        in_specs=[pl.BlockSpec((tb, w), lambda i: (i, 0)), pl.BlockSpec((1, w), lambda i: (0, 0))],
        out_specs=pl.BlockSpec((tb, w), lambda i: (i, 0)),
        out_shape=jax.ShapeDtypeStruct((s, w), F32),
        scratch_shapes=[pltpu.VMEM((1, w), F32)],
        compiler_params=_params("arbitrary"),
    )(f, fb)


def _gates_bwd(f, fb, dc, *, name):
    s, w = f.shape
    tb = 128
    nb = s // tb

    def body(f_ref, fb_ref, dc_ref, df_ref, dfb_ref, carry):
        @pl.when(pl.program_id(0) == 0)
        def _():
            carry[...] = jnp.zeros_like(carry)
            dfb_ref[...] = jnp.zeros_like(dfb_ref)

        row = lax.broadcasted_iota(jnp.int32, (tb, tb), 0)
        col = lax.broadcasted_iota(jnp.int32, (tb, tb), 1)
        tri = (row <= col).astype(BF16)
        dlogf = _tri_sum(tri, dc_ref[...]) + carry[...]
        carry[...] = dlogf[0:1, :]
        df = dlogf * jax.nn.sigmoid(-(f_ref[...] + fb_ref[...]))
        df_ref[...] = df
        dfb_ref[...] += jnp.sum(df, axis=0, keepdims=True)

    rev = pl.BlockSpec((tb, w), lambda i: (nb - 1 - i, 0))
    vec = pl.BlockSpec((1, w), lambda i: (0, 0))
    return pl.pallas_call(
        body, name=name, grid=(nb,),
        in_specs=[rev, vec, rev], out_specs=[rev, vec],
        out_shape=[jax.ShapeDtypeStruct((s, w), F32), jax.ShapeDtypeStruct((1, w), F32)],
        scratch_shapes=[pltpu.VMEM((1, w), F32)],
        compiler_params=_params("arbitrary"),
    )(f, fb, dc)


def _nt(a, b):
    return lax.dot_general(a, b, (((1,), (1,)), ((), ())), preferred_element_type=F32)


def _tn(a, b):
    return lax.dot_general(a, b, (((0,), (0,)), ((), ())), preferred_element_type=F32)


def _nn(a, b):
    return jnp.dot(a, b, preferred_element_type=F32)


def _fox_logits(q, k, ck, scale, diagonal):
    s = _nt(q, k) * scale - ck
    if diagonal:
        row = lax.broadcasted_iota(jnp.int32, s.shape, 0)
        col = lax.broadcasted_iota(jnp.int32, s.shape, 1)
        s = jnp.where(col <= row, s, NEG_INF)
    return s


def _fox_fwd(qkv, cq, ck, n_heads, *, name, carry=None):
    s = qkv.shape[0]
    e = HEAD_DIM
    t = min(512, s)
    nt = s // t
    scale = e ** -0.5

    def body(q_ref, k_ref, v_ref, cq_ref, ck_ref, o_ref, lse_ref, m_scr, l_scr, acc_scr):
        qi, ki = pl.program_id(1), pl.program_id(2)

        @pl.when(ki == 0)
        def _():
            m_scr[...] = jnp.full_like(m_scr, NEG_INF)
            l_scr[...] = jnp.zeros_like(l_scr)
            acc_scr[...] = jnp.zeros_like(acc_scr)

        def update(diagonal):
            sc = _fox_logits(q_ref[...], k_ref[...], ck_ref[...], scale, diagonal)
            m_new = jnp.maximum(m_scr[...], jnp.max(sc, axis=-1, keepdims=True))
            alpha = jnp.exp(m_scr[...] - m_new)
            p = jnp.exp(sc - m_new)
            l_scr[...] = alpha * l_scr[...] + jnp.sum(p, axis=-1, keepdims=True)
            acc_scr[...] = alpha * acc_scr[...] + _nn(p.astype(BF16), v_ref[...])
            m_scr[...] = m_new

        @pl.when(ki < qi)
        def _():
            update(False)

        @pl.when(ki == qi)
        def _():
            update(True)
            o_ref[...] = acc_scr[...] / l_scr[...]
            lse_ref[...] = m_scr[...] + jnp.log(l_scr[...]) + cq_ref[...]

    h_ = n_heads
    return _call(
        body, name=name, grid=(h_, nt, nt),
        in_specs=[
            pl.BlockSpec((t, e), lambda h, qi, ki: (qi, h)),
            pl.BlockSpec((t, e), lambda h, qi, ki: (jnp.minimum(ki, qi), h_ + h)),
            pl.BlockSpec((t, e), lambda h, qi, ki: (jnp.minimum(ki, qi), 2 * h_ + h)),
            pl.BlockSpec((None, t, 1), lambda h, qi, ki: (h, qi, 0)),
            pl.BlockSpec((None, 1, t), lambda h, qi, ki: (h, 0, jnp.minimum(ki, qi))),
        ],
        out_specs=[pl.BlockSpec((t, e), lambda h, qi, ki: (qi, h)),
                   pl.BlockSpec((None, t, 1), lambda h, qi, ki: (h, qi, 0))],
        out_shape=[jax.ShapeDtypeStruct((s, h_ * e), F32), jax.ShapeDtypeStruct((h_, s, 1), F32)],
        operands=(qkv, qkv, qkv, cq, ck),
        scratch=[pltpu.VMEM((t, 1), F32), pltpu.VMEM((t, 1), F32), pltpu.VMEM((t, e), F32)],
        semantics=("parallel", "parallel", "arbitrary"), carry=carry)


def _fox_bwd(qkv, cq, ck, lse, y, dy, n_heads, *, name, carry=None):
    s = qkv.shape[0]
    e = HEAD_DIM
    t = min(512, s)
    nt = s // t
    scale = e ** -0.5

    def body(q_ref, k_ref, v_ref, cq_ref, ck_ref, lse_ref, y_ref, dy_ref,
             dq_ref, dk_ref, dv_ref, dcq_ref, dck_ref, dq_scr, dk_scr, dv_scr, dck_scr):
        ki, qi = pl.program_id(1), pl.program_id(2)

        @pl.when(jnp.logical_and(ki == 0, qi == 0))
        def _():
            dq_scr[...] = jnp.zeros_like(dq_scr)
            dcq_ref[...] = jnp.zeros_like(dcq_ref)

        @pl.when(qi == 0)
        def _():
            dk_scr[...] = jnp.zeros_like(dk_scr)
            dv_scr[...] = jnp.zeros_like(dv_scr)
            dck_scr[...] = jnp.zeros_like(dck_scr)

        def update(diagonal):
            q, k, v = q_ref[...], k_ref[...], v_ref[...]
            sc = _fox_logits(q, k, ck_ref[...], scale, diagonal)
            p = jnp.exp(sc - (lse_ref[...] - cq_ref[...]))
            do = dy_ref[...]
            delta = jnp.sum(do * y_ref[...], axis=-1, keepdims=True)
            do = do.astype(BF16)
            dv_scr[...] += _tn(p.astype(BF16), do)
            ds = p * (_nt(do, v) - delta)
            dck_scr[...] -= jnp.sum(ds, axis=0, keepdims=True)
            rows = pl.ds(pl.multiple_of(qi * t, t), t)
            dcq_ref[rows, :] += jnp.sum(ds, axis=-1, keepdims=True)
            ds = ds.astype(BF16)
            dq_scr[rows, :] += scale * _nn(ds, k)
            dk_scr[...] += scale * _tn(ds, q)

        @pl.when(qi > ki)
        def _():
            update(False)

        @pl.when(qi == ki)
        def _():
            update(True)

        @pl.when(qi == nt - 1)
        def _():
            dk_ref[...] = dk_scr[...].astype(BF16)
            dv_ref[...] = dv_scr[...].astype(BF16)
            dck_ref[...] = dck_scr[...]

        @pl.when(jnp.logical_and(ki == nt - 1, qi == nt - 1))
        def _():
            dq_ref[...] = dq_scr[...].astype(BF16)

    h_ = n_heads
    qrow = lambda ki, qi: jnp.maximum(qi, ki)
    return _call(
        body, name=name, grid=(h_, nt, nt),
        in_specs=[
            pl.BlockSpec((t, e), lambda h, ki, qi: (qrow(ki, qi), h)),
            pl.BlockSpec((t, e), lambda h, ki, qi: (ki, h_ + h)),
            pl.BlockSpec((t, e), lambda h, ki, qi: (ki, 2 * h_ + h)),
            pl.BlockSpec((None, t, 1), lambda h, ki, qi: (h, qrow(ki, qi), 0)),
            pl.BlockSpec((None, 1, t), lambda h, ki, qi: (h, 0, ki)),
            pl.BlockSpec((None, t, 1), lambda h, ki, qi: (h, qrow(ki, qi), 0)),
            pl.BlockSpec((t, e), lambda h, ki, qi: (qrow(ki, qi), h)),
            pl.BlockSpec((t, e), lambda h, ki, qi: (qrow(ki, qi), h)),
        ],
        out_specs=[pl.BlockSpec((s, e), lambda h, ki, qi: (0, h)),
                   pl.BlockSpec((t, e), lambda h, ki, qi: (ki, h)),
                   pl.BlockSpec((t, e), lambda h, ki, qi: (ki, h)),
                   pl.BlockSpec((None, s, 1), lambda h, ki, qi: (h, 0, 0)),
                   pl.BlockSpec((None, 1, t), lambda h, ki, qi: (h, 0, ki))],
        out_shape=[jax.ShapeDtypeStruct((s, h_ * e), BF16)] * 3
        + [jax.ShapeDtypeStruct((h_, s, 1), F32), jax.ShapeDtypeStruct((h_, 1, s), F32)],
        operands=(qkv, qkv, qkv, cq, ck, lse, y, dy),
        scratch=[pltpu.VMEM((s, e), F32), pltpu.VMEM((t, e), F32), pltpu.VMEM((t, e), F32),
                 pltpu.VMEM((1, t), F32)],
        semantics=("parallel", "arbitrary", "arbitrary"), carry=carry)


def _rel_bucket_table(dilation, span):
    dist = np.arange(span + 1, dtype=np.int64) * dilation
    max_exact = REL_BUCKETS // 2
    d = np.maximum(dist.astype(np.float32), np.float32(1.0))
    large = max_exact + (np.log(d / np.float32(max_exact)) / np.float32(math.log(REL_MAX_DISTANCE / max_exact))
                         * np.float32(REL_BUCKETS - max_exact)).astype(np.int32)
    large = np.minimum(large, REL_BUCKETS - 1)
    return np.where(dist < max_exact, dist, large)


def _bucket_bands(dilation, span):
    table = _rel_bucket_table(dilation, span)
    bands = []
    for n, b in enumerate(table):
        if bands and bands[-1][0] == int(b):
            bands[-1][2] = n
        else:
            assert not any(bb[0] == int(b) for bb in bands)
            bands.append([int(b), n, n])
    return [tuple(b) for b in bands]


def _steps_back():
    i = lax.broadcasted_iota(jnp.int32, (DIL_BLOCK, DIL_BLOCK), 0)
    j = lax.broadcasted_iota(jnp.int32, (DIL_BLOCK, DIL_BLOCK), 1)
    return i - j, DIL_BLOCK + i - j


def _bias_tiles(rb_ref, h, bands, span):
    n_cur, n_prev = _steps_back()
    t_cur = jnp.zeros((DIL_BLOCK, DIL_BLOCK), F32)
    t_prev = jnp.zeros((DIL_BLOCK, DIL_BLOCK), F32)
    for b, lo, hi in bands:
        val = rb_ref[b, h]
        t_cur = jnp.where((n_cur >= lo) & (n_cur <= hi), val, t_cur)
        t_prev = jnp.where((n_prev >= lo) & (n_prev <= hi), val, t_prev)
    t_cur = jnp.where(n_cur >= 0, t_cur, NEG_INF)
    t_prev = jnp.where(n_prev <= span, t_prev, NEG_INF)
    return t_cur, t_prev


def _dil_rows(start, dilation):
    return pl.ds(start, DIL_BLOCK, stride=dilation) if dilation > 1 else pl.ds(start, DIL_BLOCK)


def _dil_fwd(qkv, rel_bias, n_heads, col0, *, name, carry=None):
    s = qkv.shape[0]
    e = HEAD_DIM
    scale = e ** -0.5
    n_pat = len(DIL_PATTERNS)
    for window, d in DIL_PATTERNS:
        assert window // d == DIL_BLOCK and s % (d * DIL_BLOCK) == 0
    bands = [_bucket_bands(d, w // d) for w, d in DIL_PATTERNS]

    def body(rb_ref, q_ref, k_ref, v_ref, y_ref, lse_ref, qf, kf, vf, *scr):
        o_scr, l_scr = scr[:n_pat], scr[n_pat:]
        h = pl.program_id(0)
        qf[...] = q_ref[...].astype(F32)
        kf[...] = k_ref[...].astype(F32)
        vf[...] = v_ref[...].astype(F32)
        for pi, (window, d) in enumerate(DIL_PATTERNS):
            t_cur, t_prev = _bias_tiles(rb_ref, h, bands[pi], window // d)
            group = d * DIL_BLOCK
            for r in range(d):
                def blk(n, carry, r=r, d=d, pi=pi, group=group, t_cur=t_cur, t_prev=t_prev):
                    q0 = pl.multiple_of(n * group, DIL_BLOCK) + r
                    p0 = pl.multiple_of(jnp.maximum(n - 1, 0) * group, DIL_BLOCK) + r
                    cur, prev = _dil_rows(q0, d), _dil_rows(p0, d)
                    qb = qf[cur, :].astype(BF16)
                    s_c = _nt(qb, kf[cur, :].astype(BF16)) * scale + t_cur
                    s_p = _nt(qb, kf[prev, :].astype(BF16)) * scale + (t_prev + jnp.where(n == 0, NEG_INF, 0.0))
                    m = jnp.maximum(jnp.max(s_c, axis=-1, keepdims=True), jnp.max(s_p, axis=-1, keepdims=True))
                    p_c, p_p = jnp.exp(s_c - m), jnp.exp(s_p - m)
                    l = jnp.sum(p_c, axis=-1, keepdims=True) + jnp.sum(p_p, axis=-1, keepdims=True)
                    o = _nn(p_c.astype(BF16), vf[cur, :].astype(BF16)) + _nn(p_p.astype(BF16), vf[prev, :].astype(BF16))
                    o_scr[pi][cur, :] = o / l
                    l_scr[pi][cur, :] = m + jnp.log(l)
                    return carry
                lax.fori_loop(0, s // group, blk, 0)
        lses = [l_scr[pi][...] for pi in range(n_pat)]
        m = functools.reduce(jnp.maximum, lses)
        ws = [jnp.exp(l - m) for l in lses]
        tot = functools.reduce(jnp.add, ws)
        y = functools.reduce(jnp.add, [w * o_scr[pi][...] for pi, w in enumerate(ws)])
        y_ref[...] = y / tot
        lse_ref[...] = m + jnp.log(tot)

    h_ = n_heads
    return _call(
        body, name=name, grid=(h_,),
        in_specs=[pl.BlockSpec(memory_space=pltpu.SMEM),
                  pl.BlockSpec((s, e), lambda h: (0, col0 + h)),
                  pl.BlockSpec((s, e), lambda h: (0, col0 + h_ + h)),
                  pl.BlockSpec((s, e), lambda h: (0, col0 + 2 * h_ + h))],
        out_specs=[pl.BlockSpec((s, e), lambda h: (0, h)), pl.BlockSpec((None, s, 1), lambda h: (h, 0, 0))],
        out_shape=[jax.ShapeDtypeStruct((s, h_ * e), F32), jax.ShapeDtypeStruct((h_, s, 1), F32)],
        operands=(rel_bias, qkv, qkv, qkv),
        scratch=[pltpu.VMEM((s, e), F32)] * 3 + [pltpu.VMEM((s, e), F32)] * n_pat
        + [pltpu.VMEM((s, 1), F32)] * n_pat,
        semantics=("parallel",), carry=carry)


def _dil_bwd(qkv, rel_bias, lse, y, dy, n_heads, col0, *, name, carry=None):
    s = qkv.shape[0]
    e = HEAD_DIM
    scale = e ** -0.5
    bands = [_bucket_bands(d, w // d) for w, d in DIL_PATTERNS]

    def body(rb_ref, q_ref, k_ref, v_ref, lse_ref, y_ref, dy_ref, dq_ref, dk_ref, dv_ref, drb_ref,
             qf, kf, vf, dqf, dkf, dvf, delta, dt_cur, dt_prev):
        h = pl.program_id(0)
        qf[...] = q_ref[...].astype(F32)
        kf[...] = k_ref[...].astype(F32)
        vf[...] = v_ref[...].astype(F32)
        dqf[...] = jnp.zeros_like(dqf)
        dkf[...] = jnp.zeros_like(dkf)
        dvf[...] = jnp.zeros_like(dvf)
        delta[...] = jnp.sum(dy_ref[...] * y_ref[...], axis=-1, keepdims=True)
        lane = lax.broadcasted_iota(jnp.int32, (1, 128), 1)
        drb = jnp.zeros((1, 128), F32)
        n_cur, n_prev = _steps_back()
        for pi, (window, d) in enumerate(DIL_PATTERNS):
            t_cur, t_prev = _bias_tiles(rb_ref, h, bands[pi], window // d)
            group = d * DIL_BLOCK
            dt_cur[...] = jnp.zeros_like(dt_cur)
            dt_prev[...] = jnp.zeros_like(dt_prev)
            for r in range(d):
                def blk(n, carry, r=r, d=d, group=group, t_cur=t_cur, t_prev=t_prev):
                    q0 = pl.multiple_of(n * group, DIL_BLOCK) + r
                    p0 = pl.multiple_of(jnp.maximum(n - 1, 0) * group, DIL_BLOCK) + r
                    cur, prev = _dil_rows(q0, d), _dil_rows(p0, d)
                    qb = qf[cur, :].astype(BF16)
                    kc, kp = kf[cur, :].astype(BF16), kf[prev, :].astype(BF16)
                    vc, vp = vf[cur, :].astype(BF16), vf[prev, :].astype(BF16)
                    lse_b, delta_b = lse_ref[cur, :], delta[cur, :]
                    do = dy_ref[cur, :].astype(BF16)
                    s_c = _nt(qb, kc) * scale + t_cur
                    s_p = _nt(qb, kp) * scale + (t_prev + jnp.where(n == 0, NEG_INF, 0.0))
                    p_c, p_p = jnp.exp(s_c - lse_b), jnp.exp(s_p - lse_b)
                    ds_c = p_c * (_nt(do, vc) - delta_b)
                    ds_p = p_p * (_nt(do, vp) - delta_b)
                    dt_cur[...] += ds_c
                    dt_prev[...] += ds_p
                    p_c, p_p = p_c.astype(BF16), p_p.astype(BF16)
                    ds_c, ds_p = ds_c.astype(BF16), ds_p.astype(BF16)
                    dqf[cur, :] += scale * (_nn(ds_c, kc) + _nn(ds_p, kp))
                    dkf[cur, :] += scale * _tn(ds_c, qb)
                    dvf[cur, :] += _tn(p_c, do)
                    dkf[prev, :] += scale * _tn(ds_p, qb)
                    dvf[prev, :] += _tn(p_p, do)
                    return carry
                lax.fori_loop(0, s // group, blk, 0)
            dtc, dtp = dt_cur[...], dt_prev[...]
            for b, lo, hi in bands[pi]:
                tot = (jnp.sum(jnp.where((n_cur >= lo) & (n_cur <= hi), dtc, 0.0))
                       + jnp.sum(jnp.where((n_prev >= lo) & (n_prev <= hi), dtp, 0.0)))
                drb = drb + jnp.where(lane == b, tot, 0.0)
        dq_ref[...] = dqf[...].astype(BF16)
        dk_ref[...] = dkf[...].astype(BF16)
        dv_ref[...] = dvf[...].astype(BF16)
        drb_ref[...] = drb

    h_ = n_heads
    col = pl.BlockSpec((s, e), lambda h: (0, h))
    return _call(
        body, name=name, grid=(h_,),
        in_specs=[pl.BlockSpec(memory_space=pltpu.SMEM),
                  pl.BlockSpec((s, e), lambda h: (0, col0 + h)),
                  pl.BlockSpec((s, e), lambda h: (0, col0 + h_ + h)),
                  pl.BlockSpec((s, e), lambda h: (0, col0 + 2 * h_ + h)),
                  pl.BlockSpec((None, s, 1), lambda h: (h, 0, 0)), col, col],
        out_specs=[col, col, col, pl.BlockSpec((None, 1, 128), lambda h: (h, 0, 0))],
        out_shape=[jax.ShapeDtypeStruct((s, h_ * e), BF16)] * 3 + [jax.ShapeDtypeStruct((h_, 1, 128), F32)],
        operands=(rel_bias, qkv, qkv, qkv, lse, y, dy),
        scratch=[pltpu.VMEM((s, e), F32)] * 6 + [pltpu.VMEM((s, 1), F32)]
        + [pltpu.VMEM((DIL_BLOCK, DIL_BLOCK), F32)] * 2,
        semantics=("parallel",), carry=carry)


def _place():
    x, y, c = lax.axis_index("x"), lax.axis_index("y"), lax.axis_index("c")
    chips = [(1 - x, y), (x, 1 - y), (1 - x, 1 - y)]
    return x, y, c, chips


def _block_index(px, py, pc):
    return 4 * px + 2 * py + pc


def _gather_task(blocks):
    n = len(blocks)

    def copies(ins, outs, sems):
        send_sems, recv_sems, local_sems = sems
        x, y, c, chips = _place()
        me, sibling = (x, y, c), (x, y, 1 - c)

        def copy(a, k, block, to, src=None):
            slot = outs[a].at[_block_index(*block)]
            return pltpu.make_async_remote_copy(
                src_ref=slot if src is None else src, dst_ref=slot,
                send_sem=send_sems.at[a, k], recv_sem=recv_sems.at[a, k],
                device_id=to, device_id_type=MESH)

        local = lambda a: pltpu.make_async_copy(ins[a], outs[a].at[_block_index(*me)], local_sems.at[a])
        return copy, local, me, sibling, c, chips

    def start(ins, outs, sems):
        copy, local, me, sibling, c, chips = copies(ins, outs, sems)
        for a in range(n):
            local(a).start()
            copy(a, 0, me, sibling, src=ins[a]).start()
            for j, chip in enumerate(chips):
                copy(a, 1 + j, me, (*chip, c), src=ins[a]).start()

    def forward(ins, outs, sems):
        copy, local, me, sibling, c, chips = copies(ins, outs, sems)
        for a in range(n):
            for j, chip in enumerate(chips):
                copy(a, 1 + j, (*chip, c), me).wait_recv()
                copy(a, 4 + j, (*chip, c), sibling).start()

    def finish(ins, outs, sems):
        copy, local, me, sibling, c, chips = copies(ins, outs, sems)
        for a in range(n):
            copy(a, 0, sibling, me).wait_recv()
            for j, chip in enumerate(chips):
                copy(a, 4 + j, (*chip, 1 - c), me).wait_recv()
        for a in range(n):
            for k in range(7):
                copy(a, k, me, sibling, src=ins[a]).wait_send()
            local(a).wait()

    return _Carried(
        blocks, [jax.ShapeDtypeStruct((N_DEV,) + b.shape, b.dtype) for b in blocks],
        [pltpu.SemaphoreType.DMA((n, 7)), pltpu.SemaphoreType.DMA((n, 7)), pltpu.SemaphoreType.DMA((n,))],
        start, forward, finish)


def _exchange_task(arrays, n_slots, route):
    n = len(arrays)

    def copies(ins, outs, sems):
        send_sems, recv_sems = sems
        out = []
        for a in range(n):
            for j in range(n_slots):
                src, to = route(j)
                out.append(pltpu.make_async_remote_copy(
                    src_ref=ins[a].at[src], dst_ref=outs[a].at[j],
                    send_sem=send_sems.at[a, j], recv_sem=recv_sems.at[a, j],
                    device_id=to, device_id_type=MESH))
        return out

    def start(ins, outs, sems):
        for cp in copies(ins, outs, sems):
            cp.start()

    def finish(ins, outs, sems):
        for cp in copies(ins, outs, sems):
            cp.wait()

    return _Carried(
        arrays, [jax.ShapeDtypeStruct((n_slots,) + g.shape[1:], g.dtype) for g in arrays],
        [pltpu.SemaphoreType.DMA((n, n_slots)), pltpu.SemaphoreType.DMA((n, n_slots))],
        start, lambda ins, outs, sems: None, finish)


def _sibling_task(grads):
    def route(q):
        x, y, c, _ = _place()
        return 2 * q + (1 - c), (x, y, 1 - c)
    return _exchange_task(grads, 4, route)


def _chips_task(parts):
    def route(j):
        x, y, c, chips = _place()
        px, py = chips[j]
        return 2 * px + py, (px, py, c)
    return _exchange_task(parts, 3, route)


def _chip_partials(grad, got, place, *, name):
    _, r, cdim = grad.shape
    tr = _row_tile(r, 512)

    def body(place_ref, g_ref, s_ref, o_ref):
        o_ref[...] = (g_ref[...] + s_ref[...]).astype(BF16)

    return pl.pallas_call(
        body, name=name,
        grid_spec=pltpu.PrefetchScalarGridSpec(
            num_scalar_prefetch=1, grid=(4, r // tr),
            in_specs=[pl.BlockSpec((None, tr, cdim), lambda q, i, pos: (2 * q + pos[2], i, 0)),
                      pl.BlockSpec((None, tr, cdim), lambda q, i, pos: (q, i, 0))],
            out_specs=pl.BlockSpec((None, tr, cdim), lambda q, i, pos: (q, i, 0))),
        out_shape=jax.ShapeDtypeStruct((4, r, cdim), BF16),
        compiler_params=_params("parallel", "parallel"),
    )(place, grad, got)


def _all_reduce_small(v, *, name):
    r, w = v.shape

    def body(v_ref, o_ref, buf, send_sems, recv_sems):
        x, y, c, _ = _place()
        me = _block_index(x, y, c)
        buf[me] = v_ref[...]
        copies = []
        for k in range(1, N_DEV):
            fx, fy, fc = (k >> 2) & 1, (k >> 1) & 1, k & 1
            peer = (x ^ fx, y ^ fy, c ^ fc)
            cp = pltpu.make_async_remote_copy(
                src_ref=v_ref, dst_ref=buf.at[me],
                send_sem=send_sems.at[k - 1], recv_sem=recv_sems.at[k - 1],
                device_id=peer, device_id_type=MESH)
            cp.start()
            copies.append(cp)
        for cp in copies:
            cp.wait()
        acc = buf[0]
        for p in range(1, N_DEV):
            acc = acc + buf[p]
        o_ref[...] = acc

    vmem = pl.BlockSpec(memory_space=pltpu.VMEM)
    return pl.pallas_call(
        body, name=name, in_specs=[vmem], out_specs=vmem,
        out_shape=jax.ShapeDtypeStruct((r, w), F32),
        scratch_shapes=[pltpu.VMEM((N_DEV, r, w), F32), pltpu.SemaphoreType.DMA((N_DEV - 1,)),
                        pltpu.SemaphoreType.DMA((N_DEV - 1,))],
    )(v)


def _adamw_math(w, g, m, v):
    m = ADAM_B1 * m + (1.0 - ADAM_B1) * g
    v = ADAM_B2 * v + (1.0 - ADAM_B2) * (g * g)
    m_hat = m / (1.0 - ADAM_B1 ** ADAM_STEP)
    v_hat = v / (1.0 - ADAM_B2 ** ADAM_STEP)
    delta = -ADAM_LR * (m_hat / (jnp.sqrt(v_hat) + ADAM_EPS) + ADAM_WD * w)
    return delta, m, v


def _adamw_sharded(w, m, v, part, got, place, *, name):
    n_l, r, cdim = w.shape
    tr = _row_tile(r, 256)

    def body(place_ref, *refs):
        w_ref, m_ref, v_ref = refs[:3]
        p_refs, g_refs = refs[3:3 + n_l], refs[3 + n_l:3 + 2 * n_l]
        g_out, d_out, m_out, v_out = refs[3 + 2 * n_l:]
        for l in range(n_l):
            @pl.when(pl.program_id(0) == l)
            def _(l=l):
                g = p_refs[l][...].astype(F32)
                for j in range(3):
                    g = g + g_refs[l][j].astype(F32)
                delta, m_new, v_new = _adamw_math(w_ref[...], g, m_ref[...], v_ref[...])
                g_out[...] = g
                d_out[...] = delta
                m_out[...] = m_new
                v_out[...] = v_new

    shard = pl.BlockSpec((None, tr, cdim), lambda l, i, pos: (l, i, 0))
    p_specs = [pl.BlockSpec((None, tr, cdim), lambda l, i, pos: (2 * pos[0] + pos[1], i, 0))] * n_l
    g_specs = [pl.BlockSpec((3, tr, cdim), lambda l, i, pos: (0, i, 0))] * n_l
    return pl.pallas_call(
        body, name=name,
        grid_spec=pltpu.PrefetchScalarGridSpec(
            num_scalar_prefetch=1, grid=(n_l, r // tr),
            in_specs=[shard] * 3 + p_specs + g_specs, out_specs=[shard] * 4),
        out_shape=[jax.ShapeDtypeStruct(w.shape, F32)] * 4,
        compiler_params=_params("arbitrary", "parallel"),
    )(place, w, m, v, *part, *got)


def _adamw_small(w, g, m, v, *, name):
    def body(w_ref, g_ref, m_ref, v_ref, d_out, m_out, v_out):
        delta, m_new, v_new = _adamw_math(w_ref[...], g_ref[...], m_ref[...], v_ref[...])
        d_out[...] = delta
        m_out[...] = m_new
        v_out[...] = v_new

    vmem = pl.BlockSpec(memory_space=pltpu.VMEM)
    return pl.pallas_call(
        body, name=name, in_specs=[vmem] * 4, out_specs=[vmem] * 3,
        out_shape=[jax.ShapeDtypeStruct(w.shape, F32)] * 3,
    )(w, g, m, v)


def _pad_lanes(a, width=128):
    return jnp.pad(a, ((0, 0), (0, width - a.shape[1])))


def _train_step(x, target, small, shards, place):
    w_in, w_out, w_mlp_in, w_mlp_out = shards
    depth, d_model, in_shard = w_in.shape
    n_heads = small["forget_b"].shape[1]
    dh = n_heads * HEAD_DIM
    d_ff = w_mlp_in.shape[2] * N_DEV
    row = lambda a: a.reshape(1, -1)
    block = {"in": w_in.astype(BF16), "out": w_out.astype(BF16), "mi": w_mlp_in.astype(BF16),
             "mo": w_mlp_out.astype(BF16)}
    whole = {}

    def unpack(kind, l, g):
        if kind == "in":
            full = g.transpose(1, 0, 2).reshape(d_model, N_DEV * in_shard)
            whole["qkv", l] = jnp.concatenate([full[:, :3 * dh], full[:, 3 * dh + n_heads:]], axis=1)
            whole["f", l] = _pad_lanes(full[:, 3 * dh:3 * dh + n_heads])
        elif kind == "out":
            whole["out", l] = g.reshape(2 * dh, d_model)
        elif kind == "mi":
            whole["mi", l] = g.transpose(1, 0, 2).reshape(d_model, d_ff)
        else:
            whole["mo", l] = g.reshape(d_ff, d_model)

    def gather(kinds, l):
        return _gather_task([block[k][l] for k in kinds])

    (g,) = _run_carried(gather(["in"], 0), name="gather_w_in_0")
    unpack("in", 0, g)

    saved = []
    for l in range(depth):
        h1 = _rmsnorm_fwd(x, row(small["norm1_g"][l]), name=f"norm1_fwd_{l}")
        qkv = _matmul(h1, whole["qkv", l], out_dtype=BF16, name=f"qkv_proj_{l}")
        f = _matmul(h1, whole["f", l], name=f"gate_proj_{l}")
        fb = _pad_lanes(row(small["forget_b"][l]))
        c = _gates_fwd(f, fb, name=f"gates_fwd_{l}")
        c_heads = c[:, :n_heads].T
        cq, ck = c_heads[:, :, None], c_heads[:, None, :]
        kinds = (["out"] if l == 0 else []) + ["mi", "mo"]
        (ya, lse_a), got = _fox_fwd(qkv, cq, ck, n_heads, name=f"fox_fwd_{l}", carry=gather(kinds, l))
        for k, g in zip(kinds, got):
            unpack(k, l, g)
        if l + 1 < depth:
            (yb, lse_b), got = _dil_fwd(qkv, small["rel_bias"], n_heads, 3 * n_heads, name=f"dil_fwd_{l}",
                                        carry=gather(["in", "out"], l + 1))
            unpack("in", l + 1, got[0])
            unpack("out", l + 1, got[1])
        else:
            yb, lse_b = _dil_fwd(qkv, small["rel_bias"], n_heads, 3 * n_heads, name=f"dil_fwd_{l}")
        mixed = _outnorm_fwd(ya, yb, row(small["outnorm_a_g"][l]), row(small["outnorm_b_g"][l]),
                             name=f"outnorm_fwd_{l}")
        x1 = _matmul(mixed, whole["out", l], addend=x, name=f"out_proj_{l}")
        h2 = _rmsnorm_fwd(x1, row(small["norm2_g"][l]), name=f"norm2_fwd_{l}")
        relu, act = _matmul(h2, whole["mi", l], out_dtype=BF16, mode="relu_sq", name=f"mlp_in_{l}")
        x2 = _matmul(act, whole["mo", l], addend=x1, name=f"mlp_out_{l}")
        saved.append((x, h1, qkv, f, fb, cq, ck, ya, lse_a, yb, lse_b, mixed, x1, h2, relu, act))
        x = x2
    dx, loss_tile, d_final = _loss_head(x, row(small["final_norm_g"]), target, name="loss_head")

    parts = {k: [None] * depth for k in ("in", "out", "mi", "mo")}
    got = {k: [None] * depth for k in ("in", "out", "mi", "mo")}
    small_grads = [None] * depth
    waiting = None
    for l in reversed(range(depth)):
        x0, h1, qkv, f, fb, cq, ck, ya, lse_a, yb, lse_b, mixed, x1, h2, relu, act = saved[l]
        mlp_out_dx = functools.partial(_matmul, dx, whole["mo", l], tb=True, out_dtype=BF16, mode="mul2", aux=relu,
                                       name=f"mlp_out_dx_{l}")
        mlp_in_dw = functools.partial(_matmul, h2, tk=1024, ta=True, out_chunks=N_DEV, name=f"mlp_in_dw_{l}")
        if waiting is None:
            du = mlp_out_dx()
        else:
            du, (from_sibling,) = mlp_out_dx(carry=_sibling_task([waiting]))
            parts["in"][l + 1] = _chip_partials(waiting, from_sibling, place, name=f"chip_partials_in_{l + 1}")
        dw_mo = _matmul(act, dx, ta=True, tk=1024, name=f"mlp_out_dw_{l}")
        dh2 = _matmul(du, whole["mi", l], tb=True, name=f"mlp_in_dx_{l}")
        if waiting is None:
            dw_mi = mlp_in_dw(du)
        else:
            dw_mi, (got["in"][l + 1],) = mlp_in_dw(du, carry=_chips_task([parts["in"][l + 1]]))
        dx1, d_norm2 = _rmsnorm_bwd(x1, row(small["norm2_g"][l]), dh2, dx, name=f"norm2_bwd_{l}")
        dmixed = _matmul(dx1, whole["out", l], tb=True, name=f"out_proj_dx_{l}")
        dw_out = _matmul(mixed, dx1, ta=True, tk=1024, name=f"out_proj_dw_{l}")
        dya, dyb, d_ga, d_gb = _outnorm_bwd(ya, yb, row(small["outnorm_a_g"][l]), row(small["outnorm_b_g"][l]),
                                            dmixed, name=f"outnorm_bwd_{l}")
        group = {"mo": dw_mo.reshape(N_DEV, d_ff // N_DEV, d_model), "mi": dw_mi,
                 "out": dw_out.reshape(N_DEV, 2 * dh // N_DEV, d_model)}
        kinds = list(group)
        (dqa, dka, dva, dcq, dck), from_sibling = _fox_bwd(
            qkv, cq, ck, lse_a, ya, dya, n_heads, name=f"fox_bwd_{l}",
            carry=_sibling_task([group[k] for k in kinds]))
        for k, s_ in zip(kinds, from_sibling):
            parts[k][l] = _chip_partials(group[k], s_, place, name=f"chip_partials_{k}_{l}")
        (dqb, dkb, dvb, d_rb), from_chips = _dil_bwd(
            qkv, small["rel_bias"], lse_b, yb, dyb, n_heads, 3 * n_heads, name=f"dil_bwd_{l}",
            carry=_chips_task([parts[k][l] for k in kinds]))
        for k, c_ in zip(kinds, from_chips):
            got[k][l] = c_
        dc = _pad_lanes((dcq[:, :, 0] + dck[:, 0, :]).T)
        df, dfb = _gates_bwd(f, fb, dc, name=f"gates_bwd_{l}")
        dqkv = jnp.concatenate([dqa, dka, dva, dqb, dkb, dvb], axis=1)
        dw_qkv = _matmul(h1, dqkv, ta=True, tk=1024, name=f"qkv_proj_dw_{l}")
        dw_f = _matmul(h1, df, ta=True, tk=1024, name=f"gate_proj_dw_{l}")
        dw_in = jnp.concatenate([dw_qkv[:, :3 * dh], dw_f[:, :n_heads], dw_qkv[:, 3 * dh:]], axis=1)
        dw_in = dw_in.reshape(d_model, N_DEV, in_shard).transpose(1, 0, 2)
        dh1 = _matmul(df, whole["f", l], tb=True, name=f"gate_proj_dx_{l}")
        qkv_dx = functools.partial(_matmul, dqkv, whole["qkv", l], tb=True, addend=dh1, name=f"qkv_proj_dx_{l}")
        norm1_bwd = functools.partial(_rmsnorm_bwd, x0, row(small["norm1_g"][l]), name=f"norm1_bwd_{l}")
        if l > 0:
            dx, d_norm1 = norm1_bwd(qkv_dx(), dx1)
            waiting = dw_in
        else:
            dh1, (from_sibling,) = qkv_dx(carry=_sibling_task([dw_in]))
            parts["in"][0] = _chip_partials(dw_in, from_sibling, place, name="chip_partials_in_0")
            (dx, d_norm1), (got["in"][0],) = norm1_bwd(dh1, dx1, carry=_chips_task([parts["in"][0]]))
        small_grads[l] = dict(norm1_g=d_norm1[0], forget_b=dfb[0, :n_heads], rel_bias=d_rb[:, 0, :REL_BUCKETS].T,
                              outnorm_a_g=d_ga[0], outnorm_b_g=d_gb[0], norm2_g=d_norm2[0])
    return loss_tile[0, 0], dx, d_final[0], small_grads, parts, got


def _pack_small(parts, rows):
    flat = jnp.concatenate([p.reshape(-1).astype(F32) for p in parts])
    return jnp.pad(flat, (0, rows * 128 - flat.shape[0])).reshape(rows, 128)


def _unpack_small(packed, shapes):
    flat = packed.reshape(-1)
    out, pos = [], 0
    for shp in shapes:
        size = int(np.prod(shp))
        out.append(flat[pos:pos + size].reshape(shp))
        pos += size
    return out


def kernel(x, norm1_g, w_in, forget_b, rel_bias, outnorm_a_g, outnorm_b_g, w_out, norm2_g, w_mlp_in, w_mlp_out, final_norm_g, loss_target, m_norm1_g, m_w_in, m_forget_b, m_rel_bias, m_outnorm_a_g, m_outnorm_b_g, m_w_out, m_norm2_g, m_w_mlp_in, m_w_mlp_out, m_final_norm_g, v_norm1_g, v_w_in, v_forget_b, v_rel_bias, v_outnorm_a_g, v_outnorm_b_g, v_w_out, v_norm2_g, v_w_mlp_in, v_w_mlp_out, v_final_norm_g):
    depth, d_model, in_shard = w_in.shape
    n_heads = forget_b.shape[1]
    assert in_shard * N_DEV == 6 * n_heads * HEAD_DIM + n_heads and x.shape[0] == 1
    place = jnp.stack([lax.axis_index("x"), lax.axis_index("y"), lax.axis_index("c")]).astype(jnp.int32)

    small_names = ["norm1_g", "forget_b", "rel_bias", "outnorm_a_g", "outnorm_b_g", "norm2_g", "final_norm_g"]
    small_w = dict(norm1_g=norm1_g, forget_b=forget_b, rel_bias=rel_bias, outnorm_a_g=outnorm_a_g,
                   outnorm_b_g=outnorm_b_g, norm2_g=norm2_g, final_norm_g=final_norm_g)
    loss_part, dx, d_final, grads, parts, got = _train_step(
        x[0], loss_target[0], small_w, (w_in, w_out, w_mlp_in, w_mlp_out), place)

    small_m = dict(norm1_g=m_norm1_g, forget_b=m_forget_b, rel_bias=m_rel_bias, outnorm_a_g=m_outnorm_a_g,
                   outnorm_b_g=m_outnorm_b_g, norm2_g=m_norm2_g, final_norm_g=m_final_norm_g)
    small_v = dict(norm1_g=v_norm1_g, forget_b=v_forget_b, rel_bias=v_rel_bias, outnorm_a_g=v_outnorm_a_g,
                   outnorm_b_g=v_outnorm_b_g, norm2_g=v_norm2_g, final_norm_g=v_final_norm_g)
    small_g = {k: jnp.stack([g[k] for g in grads]) for k in small_names if k not in ("rel_bias", "final_norm_g")}
    small_g["rel_bias"] = functools.reduce(jnp.add, [g["rel_bias"] for g in grads])
    small_g["final_norm_g"] = d_final
    shapes = [small_w[k].shape for k in small_names]
    total = sum(int(np.prod(s)) for s in shapes) + 1
    rows = -(-total // (8 * 128)) * 8
    packed_g = _pack_small([small_g[k] for k in small_names] + [loss_part], rows)
    packed_g = _all_reduce_small(packed_g, name="reduce_small")
    zero = jnp.zeros((1,), F32)
    packed_w = _pack_small([small_w[k] for k in small_names] + [zero], rows)
    packed_m = _pack_small([small_m[k] for k in small_names] + [zero], rows)
    packed_v = _pack_small([small_v[k] for k in small_names] + [zero + 1.0], rows)
    packed_d, packed_nm, packed_nv = _adamw_small(packed_w, packed_g, packed_m, packed_v, name="adamw_small")
    g_small = dict(zip(small_names, _unpack_small(packed_g, shapes)))
    d_small = dict(zip(small_names, _unpack_small(packed_d, shapes)))
    nm_small = dict(zip(small_names, _unpack_small(packed_nm, shapes)))
    nv_small = dict(zip(small_names, _unpack_small(packed_nv, shapes)))
    loss = packed_g.reshape(-1)[total - 1]

    big_out = {}
    for kind, nm, w, m, v in [("in", "w_in", w_in, m_w_in, v_w_in), ("out", "w_out", w_out, m_w_out, v_w_out),
                              ("mi", "w_mlp_in", w_mlp_in, m_w_mlp_in, v_w_mlp_in),
                              ("mo", "w_mlp_out", w_mlp_out, m_w_mlp_out, v_w_mlp_out)]:
        big_out[nm] = _adamw_sharded(w, m, v, parts[kind], got[kind], place, name=f"adamw_{nm}")

    order = ["norm1_g", "w_in", "forget_b", "rel_bias", "outnorm_a_g", "outnorm_b_g", "w_out", "norm2_g",
             "w_mlp_in", "w_mlp_out", "final_norm_g"]
    pick = lambda k, idx, small: big_out[k][idx] if k in big_out else small[k]
    outs = [loss, dx[None]]
    outs += [pick(k, 0, g_small) for k in order]
    outs += [pick(k, 1, d_small) for k in order]
    outs += [pick(k, 2, nm_small) for k in order]
    outs += [pick(k, 3, nv_small) for k in order]
    return tuple(outs)
```

```python
import functools
import math

import numpy as np
import jax
import jax.numpy as jnp
from jax import lax
from jax.experimental import pallas as pl
from jax.experimental.pallas import tpu as pltpu

F32 = jnp.float32
BF16 = jnp.bfloat16

HEAD_DIM = 128
DIL_PATTERNS = ((128, 1), (512, 4), (2048, 16))
DIL_BLOCK = 128
REL_BUCKETS = 32
REL_MAX_DISTANCE = 2048
NORM_EPS = 1e-6
NEG_INF = -1e30
ADAM_LR = 0.001
ADAM_B1 = 0.9
ADAM_B2 = 0.999
ADAM_EPS = 1e-08
ADAM_WD = 0.01
ADAM_STEP = 10

N_DEV = 8
V7X_VMEM_LIMIT_BYTES = 56 * 1024 * 1024
MESH = pl.DeviceIdType.MESH


def _params(*semantics):
    return pltpu.CompilerParams(dimension_semantics=semantics, vmem_limit_bytes=V7X_VMEM_LIMIT_BYTES)


HBM_SPEC = pl.BlockSpec(memory_space=pltpu.HBM)


class _Carried:
    def __init__(self, operands, out_shape, scratch, start, forward, finish):
        self.operands, self.out_shape, self.scratch = list(operands), list(out_shape), list(scratch)
        self.start, self.forward, self.finish = start, forward, finish


FORWARD_AT = 0.8


def _call(body, *, name, grid, in_specs, out_specs, out_shape, operands, scratch=(), semantics, carry=None,
          tables=()):
    in_specs, out_specs, out_shape, scratch = list(in_specs), list(out_specs), list(out_shape), list(scratch)
    n_tab = len(tables)

    def run(fn, in_specs, out_specs, out_shape, scratch, operands, semantics):
        return pl.pallas_call(
            fn, name=name,
            grid_spec=pltpu.PrefetchScalarGridSpec(
                num_scalar_prefetch=n_tab, grid=grid, in_specs=in_specs, out_specs=out_specs,
                scratch_shapes=scratch),
            out_shape=out_shape, compiler_params=_params(*semantics))(*tables, *operands)

    if carry is None:
        return run(body, in_specs, out_specs, out_shape, scratch, operands, semantics)
    n_in, n_out = len(in_specs), len(out_specs)
    ci, co, cs = len(carry.operands), len(carry.out_shape), len(carry.scratch)
    steps = int(np.prod(grid))
    forward_step = min(int(steps * FORWARD_AT), steps - 1)

    def carrying(*refs):
        tabs, refs = refs[:n_tab], refs[n_tab:]
        main_in, c_in = refs[:n_in], refs[n_in:n_in + ci]
        main_out = refs[n_in + ci:n_in + ci + n_out]
        c_out = refs[n_in + ci + n_out:n_in + ci + n_out + co]
        rest = refs[n_in + ci + n_out + co:]
        main_scr, c_scr = rest[:len(rest) - cs], rest[len(rest) - cs:]
        step = 0
        for axis, extent in enumerate(grid):
            step = step * extent + pl.program_id(axis)

        @pl.when(step == 0)
        def _():
            carry.start(c_in, c_out, c_scr)

        body(*tabs, *main_in, *main_out, *main_scr)

        @pl.when(step == forward_step)
        def _():
            carry.forward(c_in, c_out, c_scr)

        @pl.when(step == steps - 1)
        def _():
            carry.finish(c_in, c_out, c_scr)

    res = run(carrying, in_specs + [HBM_SPEC] * ci, out_specs + [HBM_SPEC] * co, out_shape + carry.out_shape,
              scratch + carry.scratch, (*operands, *carry.operands), ["arbitrary"] * len(grid))
    return res[:n_out], res[n_out:]


def _run_carried(carry, *, name):
    ci, co = len(carry.operands), len(carry.out_shape)

    def body(*refs):
        parts = (refs[:ci], refs[ci:ci + co], refs[ci + co:])
        carry.start(*parts)
        carry.forward(*parts)
        carry.finish(*parts)

    return pl.pallas_call(
        body, name=name, in_specs=[HBM_SPEC] * ci, out_specs=[HBM_SPEC] * co,
        out_shape=carry.out_shape, scratch_shapes=carry.scratch)(*carry.operands)


def _fit(dim, want, unit=128):
    if dim <= want:
        return dim
    t = want - want % unit
    while dim % t:
        t -= unit
    return t


def _matmul(a, b, *, name, ta=False, tb=False, out_dtype=F32, tm=1024, tn=1024, tk=2048,
            addend=None, mode=None, aux=None, out_chunks=None, carry=None):
    m_dim, k_dim = (a.shape[1], a.shape[0]) if ta else a.shape
    n_dim = b.shape[0] if tb else b.shape[1]
    assert (b.shape[1] if tb else b.shape[0]) == k_dim
    tm, tn, tk = _fit(m_dim, tm), _fit(n_dim // (out_chunks or 1), tn), _fit(k_dim, tk)
    assert m_dim % tm == 0 and n_dim % tn == 0 and k_dim % tk == 0, (name, a.shape, b.shape)
    nk = k_dim // tk
    a_spec = (pl.BlockSpec((tk, tm), lambda i, j, k: (k, i)) if ta
              else pl.BlockSpec((tm, tk), lambda i, j, k: (i, k)))
    b_spec = (pl.BlockSpec((tn, tk), lambda i, j, k: (j, k)) if tb
              else pl.BlockSpec((tk, tn), lambda i, j, k: (k, j)))
    mn_spec = pl.BlockSpec((tm, tn), lambda i, j, k: (i, j))
    if out_chunks is None:
        o_spec, o_shape = mn_spec, (m_dim, n_dim)
    else:
        per = n_dim // out_chunks // tn
        assert per * tn * out_chunks == n_dim
        o_spec = pl.BlockSpec((None, tm, tn), lambda i, j, k: (j // per, i, j % per))
        o_shape = (out_chunks, m_dim, n_dim // out_chunks)
    dims = (((0 if ta else 1,), (1 if tb else 0,)), ((), ()))
    n_out = 2 if mode == "relu_sq" else 1
    in_specs, operands = [a_spec, b_spec], [a, b]
    if addend is not None:
        in_specs.append(mn_spec)
        operands.append(addend)
    if mode == "mul2":
        in_specs.append(mn_spec)
        operands.append(aux)

    def body(*refs):
        a_ref, b_ref = refs[0], refs[1]
        pos = 2
        add_ref = aux_ref = None
        if addend is not None:
            add_ref, pos = refs[pos], pos + 1
        if mode == "mul2":
            aux_ref, pos = refs[pos], pos + 1
        outs = refs[pos:pos + n_out]

        def finish(acc):
            if add_ref is not None:
                acc = acc + add_ref[...].astype(F32)
            if mode == "relu_sq":
                r = jnp.maximum(acc, 0.0)
                outs[0][...] = r.astype(outs[0].dtype)
                outs[1][...] = (r * r).astype(outs[1].dtype)
            elif mode == "mul2":
                outs[0][...] = (acc * (2.0 * aux_ref[...].astype(F32))).astype(outs[0].dtype)
            else:
                outs[0][...] = acc.astype(outs[0].dtype)

        part = lax.dot_general(a_ref[...].astype(BF16), b_ref[...].astype(BF16), dims,
                               preferred_element_type=F32)
        if nk == 1:
            finish(part)
        else:
            acc_ref = refs[-1]
            k = pl.program_id(2)

            @pl.when(k == 0)
            def _():
                acc_ref[...] = part

            @pl.when(k > 0)
            def _():
                acc_ref[...] += part

            @pl.when(k == nk - 1)
            def _():
                finish(acc_ref[...])

    out_shape = [jax.ShapeDtypeStruct(o_shape, out_dtype)] * n_out
    res = _call(
        body, name=name, grid=(m_dim // tm, n_dim // tn, nk),
        in_specs=in_specs, out_specs=[o_spec] * n_out, out_shape=out_shape, operands=operands,
        scratch=[pltpu.VMEM((tm, tn), F32)] if nk > 1 else [],
        semantics=("parallel", "parallel", "arbitrary"), carry=carry)
    if carry is not None:
        res, carried = res
        return (res if n_out > 1 else res[0]), carried
    return res if n_out > 1 else res[0]


def _norm_fwd_math(x, g):
    r = lax.rsqrt(jnp.mean(x * x, axis=-1, keepdims=True) + NORM_EPS)
    return (x * r) * g


def _norm_bwd_math(x, g, dy):
    r = lax.rsqrt(jnp.mean(x * x, axis=-1, keepdims=True) + NORM_EPS)
    xh = x * r
    dxh = dy * g
    dx = r * (dxh - xh * jnp.mean(dxh * xh, axis=-1, keepdims=True))
    return dx, jnp.sum(dy * xh, axis=0, keepdims=True)


def _row_tile(rows, want=256):
    t = min(rows, want)
    assert rows % t == 0
    return t


def _rmsnorm_fwd(x, g, *, name):
    s, d = x.shape
    tr = _row_tile(s)

    def body(x_ref, g_ref, h_ref):
        h_ref[...] = _norm_fwd_math(x_ref[...], g_ref[...]).astype(BF16)

    return pl.pallas_call(
        body, name=name, grid=(s // tr,),
        in_specs=[pl.BlockSpec((tr, d), lambda i: (i, 0)), pl.BlockSpec((1, d), lambda i: (0, 0))],
        out_specs=pl.BlockSpec((tr, d), lambda i: (i, 0)),
        out_shape=jax.ShapeDtypeStruct((s, d), BF16),
        compiler_params=_params("parallel"),
    )(x, g)


def _rmsnorm_bwd(x, g, dh, dres, *, name, carry=None):
    s, d = x.shape
    tr = _row_tile(s)

    def body(x_ref, g_ref, dh_ref, dres_ref, dx_ref, dg_ref):
        dx, dg = _norm_bwd_math(x_ref[...], g_ref[...], dh_ref[...])
        dx_ref[...] = dres_ref[...] + dx

        @pl.when(pl.program_id(0) == 0)
        def _():
            dg_ref[...] = dg

        @pl.when(pl.program_id(0) > 0)
        def _():
            dg_ref[...] += dg

    row = pl.BlockSpec((tr, d), lambda i: (i, 0))
    vec = pl.BlockSpec((1, d), lambda i: (0, 0))
    return _call(
        body, name=name, grid=(s // tr,),
        in_specs=[row, vec, row, row], out_specs=[row, vec],
        out_shape=[jax.ShapeDtypeStruct((s, d), F32), jax.ShapeDtypeStruct((1, d), F32)],
        operands=(x, g, dh, dres), semantics=("arbitrary",), carry=carry)


def _outnorm_fwd(ya, yb, ga, gb, *, name):
    s, da = ya.shape
    db = yb.shape[1]
    tr = _row_tile(s)

    def body(ya_ref, yb_ref, ga_ref, gb_ref, o_ref):
        o_ref[:, :da] = _norm_fwd_math(ya_ref[...], ga_ref[...]).astype(BF16)
        o_ref[:, da:] = _norm_fwd_math(yb_ref[...], gb_ref[...]).astype(BF16)

    return pl.pallas_call(
        body, name=name, grid=(s // tr,),
        in_specs=[pl.BlockSpec((tr, da), lambda i: (i, 0)), pl.BlockSpec((tr, db), lambda i: (i, 0)),
                  pl.BlockSpec((1, da), lambda i: (0, 0)), pl.BlockSpec((1, db), lambda i: (0, 0))],
        out_specs=pl.BlockSpec((tr, da + db), lambda i: (i, 0)),
        out_shape=jax.ShapeDtypeStruct((s, da + db), BF16),
        compiler_params=_params("parallel"),
    )(ya, yb, ga, gb)


def _outnorm_bwd(ya, yb, ga, gb, dmixed, *, name):
    s, da = ya.shape
    db = yb.shape[1]
    tr = _row_tile(s)

    def body(ya_ref, yb_ref, ga_ref, gb_ref, dm_ref, dya_ref, dyb_ref, dga_ref, dgb_ref):
        dxa, dga = _norm_bwd_math(ya_ref[...], ga_ref[...], dm_ref[:, :da])
        dxb, dgb = _norm_bwd_math(yb_ref[...], gb_ref[...], dm_ref[:, da:])
        dya_ref[...] = dxa
        dyb_ref[...] = dxb

        @pl.when(pl.program_id(0) == 0)
        def _():
            dga_ref[...] = dga
            dgb_ref[...] = dgb

        @pl.when(pl.program_id(0) > 0)
        def _():
            dga_ref[...] += dga
            dgb_ref[...] += dgb

    ra = pl.BlockSpec((tr, da), lambda i: (i, 0))
    rb = pl.BlockSpec((tr, db), lambda i: (i, 0))
    va = pl.BlockSpec((1, da), lambda i: (0, 0))
    vb = pl.BlockSpec((1, db), lambda i: (0, 0))
    return pl.pallas_call(
        body, name=name, grid=(s // tr,),
        in_specs=[ra, rb, va, vb, pl.BlockSpec((tr, da + db), lambda i: (i, 0))],
        out_specs=[ra, rb, va, vb],
        out_shape=[jax.ShapeDtypeStruct((s, da), F32), jax.ShapeDtypeStruct((s, db), F32),
                   jax.ShapeDtypeStruct((1, da), F32), jax.ShapeDtypeStruct((1, db), F32)],
        compiler_params=_params("arbitrary"),
    )(ya, yb, ga, gb, dmixed)


def _loss_head(x, g, target, *, name):
    s, d = x.shape
    tr = _row_tile(s)

    def body(x_ref, g_ref, t_ref, dx_ref, loss_ref, dg_ref):
        xv, gv = x_ref[...], g_ref[...]
        err = _norm_fwd_math(xv, gv) - t_ref[...]
        part = 0.5 * jnp.sum(jnp.mean(err * err, axis=-1, keepdims=True), axis=0, keepdims=True)
        dx, dg = _norm_bwd_math(xv, gv, err * (1.0 / d))
        dx_ref[...] = dx
        part = jnp.broadcast_to(part, (8, 128))

        @pl.when(pl.program_id(0) == 0)
        def _():
            dg_ref[...] = dg
            loss_ref[...] = part

        @pl.when(pl.program_id(0) > 0)
        def _():
            dg_ref[...] += dg
            loss_ref[...] += part

    row = pl.BlockSpec((tr, d), lambda i: (i, 0))
    vec = pl.BlockSpec((1, d), lambda i: (0, 0))
    return pl.pallas_call(
        body, name=name, grid=(s // tr,),
        in_specs=[row, vec, row],
        out_specs=[row, pl.BlockSpec((8, 128), lambda i: (0, 0)), vec],
        out_shape=[jax.ShapeDtypeStruct((s, d), F32), jax.ShapeDtypeStruct((8, 128), F32),
                   jax.ShapeDtypeStruct((1, d), F32)],
        compiler_params=_params("arbitrary"),
    )(x, g, target)


def _split3(x):
    hi = x.astype(BF16)
    rem = x - hi.astype(F32)
    mid = rem.astype(BF16)
    lo = (rem - mid.astype(F32)).astype(BF16)
    return hi, mid, lo


def _tri_sum(tri, x):
    hi, mid, lo = _split3(x)
    dot = functools.partial(jnp.dot, preferred_element_type=F32)
    return dot(tri, hi) + dot(tri, mid) + dot(tri, lo)


def _gates_fwd(f, fb, *, name):
    s, w = f.shape
    tb = 128
    nb = s // tb

    def body(f_ref, fb_ref, c_ref, carry):
        @pl.when(pl.program_id(0) == 0)
        def _():
            carry[...] = jnp.zeros_like(carry)

        logf = jax.nn.log_sigmoid(f_ref[...] + fb_ref[...])
        row = lax.broadcasted_iota(jnp.int32, (tb, tb), 0)
        col = lax.broadcasted_iota(jnp.int32, (tb, tb), 1)
        tri = (row >= col).astype(BF16)
        c = _tri_sum(tri, logf) + carry[...]
        c_ref[...] = c
        carry[...] = c[tb - 1:tb, :]

    return pl.pallas_call(
        body, name=name, grid=(nb,),
        in_specs=[pl.BlockSpec((tb, w), lambda i: (i, 0)), pl.BlockSpec((1, w), lambda i: (0, 0))],
        out_specs=pl.BlockSpec((tb, w), lambda i: (i, 0)),
        out_shape=jax.ShapeDtypeStruct((s, w), F32),
        scratch_shapes=[pltpu.VMEM((1, w), F32)],
        compiler_params=_params("arbitrary"),
    )(f, fb)


def _gates_bwd(f, fb, dc, *, name):
    s, w = f.shape
    tb = 128
    nb = s // tb

    def body(f_ref, fb_ref, dc_ref, df_ref, dfb_ref, carry):
        @pl.when(pl.program_id(0) == 0)
        def _():
            carry[...] = jnp.zeros_like(carry)
            dfb_ref[...] = jnp.zeros_like(dfb_ref)

        row = lax.broadcasted_iota(jnp.int32, (tb, tb), 0)
        col = lax.broadcasted_iota(jnp.int32, (tb, tb), 1)
        tri = (row <= col).astype(BF16)
        dlogf = _tri_sum(tri, dc_ref[...]) + carry[...]
        carry[...] = dlogf[0:1, :]
        df = dlogf * jax.nn.sigmoid(-(f_ref[...] + fb_ref[...]))
        df_ref[...] = df
        dfb_ref[...] += jnp.sum(df, axis=0, keepdims=True)

    rev = pl.BlockSpec((tb, w), lambda i: (nb - 1 - i, 0))
    vec = pl.BlockSpec((1, w), lambda i: (0, 0))
    return pl.pallas_call(
        body, name=name, grid=(nb,),
        in_specs=[rev, vec, rev], out_specs=[rev, vec],
        out_shape=[jax.ShapeDtypeStruct((s, w), F32), jax.ShapeDtypeStruct((1, w), F32)],
        scratch_shapes=[pltpu.VMEM((1, w), F32)],
        compiler_params=_params("arbitrary"),
    )(f, fb, dc)


def _nt(a, b):
    return lax.dot_general(a, b, (((1,), (1,)), ((), ())), preferred_element_type=F32)


def _tn(a, b):
    return lax.dot_general(a, b, (((0,), (0,)), ((), ())), preferred_element_type=F32)


def _nn(a, b):
    return jnp.dot(a, b, preferred_element_type=F32)


DIL_GROUP = 4
FOX_TILE = 512
LOG2E = 1.4426950408889634


def _causal_pairs(nt, key_major):
    if key_major:
        pairs = [(q, k) for k in range(nt) for q in range(k, nt)]
    else:
        pairs = [(q, k) for q in range(nt) for k in range(q + 1)]
    return (jnp.asarray([p[0] for p in pairs], jnp.int32), jnp.asarray([p[1] for p in pairs], jnp.int32))


def _fox_fwd(qkv, c_row, ck_lanes, n_heads, *, name, carry=None):
    s = qkv.shape[0]
    e = HEAD_DIM
    t = min(FOX_TILE, s)
    nt = s // t
    scale2 = e ** -0.5 * LOG2E
    lanes = 128
    q_tab, k_tab = _causal_pairs(nt, key_major=False)

    def body(q_tab, k_tab, q_ref, k_ref, v_ref, cq_ref, ck_ref, o_ref, lse_ref,
             m_scr, l_scr, acc_scr, s_scr, p_scr):
        pair = pl.program_id(1)
        qi, ki = q_tab[pair], k_tab[pair]

        @pl.when(ki == 0)
        def _():
            m_scr[...] = jnp.full_like(m_scr, NEG_INF)
            l_scr[...] = jnp.zeros_like(l_scr)
            acc_scr[...] = jnp.zeros_like(acc_scr)

        def update(diagonal):
            s_scr[...] = _nt(k_ref[...], q_ref[...])
            ck2 = ck_ref[...]
            for c0 in range(0, t, lanes):
                cols = pl.ds(c0, lanes)

                def logits2():
                    x = s_scr[:, cols] * scale2 - ck2
                    if diagonal:
                        key = lax.broadcasted_iota(jnp.int32, x.shape, 0)
                        qry = c0 + lax.broadcasted_iota(jnp.int32, x.shape, 1)
                        x = jnp.where(key <= qry, x, NEG_INF)
                    return x

                m_old = m_scr[:, cols]
                m_new = jnp.maximum(m_old, jnp.max(logits2(), axis=0, keepdims=True))
                p = jnp.exp2(logits2() - m_new)
                alpha = jnp.exp2(m_old - m_new)
                l_scr[:, cols] = alpha * l_scr[:, cols] + jnp.sum(p, axis=0, keepdims=True)
                m_scr[:, cols] = m_new
                acc_scr[:, cols] = alpha * acc_scr[:, cols]
                p_scr[:, cols] = p.astype(BF16)
            acc_scr[...] += _tn(v_ref[...], p_scr[...])

        @pl.when(ki < qi)
        def _():
            update(False)

        @pl.when(ki == qi)
        def _():
            update(True)
            o_ref[...] = (acc_scr[...] / l_scr[...]).T
            lse_ref[...] = (m_scr[...] + jnp.log2(l_scr[...])) * (1.0 / LOG2E) + cq_ref[...]

    h_ = n_heads
    return _call(
        body, name=name, grid=(h_, int(q_tab.shape[0])), tables=(q_tab, k_tab),
        in_specs=[
            pl.BlockSpec((t, e), lambda h, p, qt, kt: (qt[p], h)),
            pl.BlockSpec((t, e), lambda h, p, qt, kt: (kt[p], h_ + h)),
            pl.BlockSpec((t, e), lambda h, p, qt, kt: (kt[p], 2 * h_ + h)),
            pl.BlockSpec((None, 1, t), lambda h, p, qt, kt: (h, 0, qt[p])),
            pl.BlockSpec((None, t, lanes), lambda h, p, qt, kt: (h, kt[p], 0)),
        ],
        out_specs=[pl.BlockSpec((t, e), lambda h, p, qt, kt: (qt[p], h)),
                   pl.BlockSpec((None, 1, t), lambda h, p, qt, kt: (h, 0, qt[p]))],
        out_shape=[jax.ShapeDtypeStruct((s, h_ * e), F32), jax.ShapeDtypeStruct((h_, 1, s), F32)],
        operands=(qkv, qkv, qkv, c_row, ck_lanes),
        scratch=[pltpu.VMEM((1, t), F32), pltpu.VMEM((1, t), F32), pltpu.VMEM((e, t), F32),
                 pltpu.VMEM((t, t), F32), pltpu.VMEM((t, t), BF16)],
        semantics=("parallel", "arbitrary"), carry=carry)


def _fox_bwd(qkv, c_row, ck_lanes, lse, y, dy, n_heads, *, name, carry=None):
    s = qkv.shape[0]
    e = HEAD_DIM
    t = min(FOX_TILE, s)
    nt = s // t
    scale = e ** -0.5
    scale2 = scale * LOG2E
    lanes = 128
    q_tab, k_tab = _causal_pairs(nt, key_major=False)
    n_pairs = int(q_tab.shape[0])

    def body(q_tab, k_tab, q_ref, k_ref, v_ref, c_ref, ck_ref, lse_ref, y_ref, dy_ref,
             dq_ref, dk_ref, dv_ref, dcq_ref, dck_ref,
             dq_scr, dk_scr, dv_scr, dck_scr, s_scr, dp_scr, p_scr, ds_scr, do_scr, delta_scr, shift_scr):
        pair = pl.program_id(1)
        qi, ki = q_tab[pair], k_tab[pair]

        @pl.when(pair == 0)
        def _():
            dk_scr[...] = jnp.zeros_like(dk_scr)
            dv_scr[...] = jnp.zeros_like(dv_scr)
            dck_scr[...] = jnp.zeros_like(dck_scr)

        @pl.when(ki == 0)
        def _():
            do = dy_ref[...]
            delta_scr[...] = lax.dot_general(jnp.ones((8, e), F32), do * y_ref[...], (((1,), (1,)), ((), ())),
                                             precision=lax.Precision.HIGHEST, preferred_element_type=F32)
            shift_scr[...] = (lse_ref[...] - c_ref[...]) * LOG2E
            do_scr[...] = do.astype(BF16)
            dq_scr[...] = jnp.zeros_like(dq_scr)
            dcq_ref[...] = jnp.zeros_like(dcq_ref)

        def update(diagonal):
            s_scr[...] = _nt(k_ref[...], q_ref[...])
            dp_scr[...] = _nt(v_ref[...], do_scr[...])
            ck2 = ck_ref[...]
            k_rows = pl.ds(pl.multiple_of(ki * t, t), t)
            for c0 in range(0, t, lanes):
                cols = pl.ds(c0, lanes)
                x = s_scr[:, cols] * scale2 - ck2
                if diagonal:
                    key = lax.broadcasted_iota(jnp.int32, x.shape, 0)
                    qry = c0 + lax.broadcasted_iota(jnp.int32, x.shape, 1)
                    x = jnp.where(key <= qry, x, NEG_INF)
                p = jnp.exp2(x - shift_scr[:, cols])
                ds = p * (dp_scr[:, cols] - delta_scr[0:1, cols])
                dcq_ref[:, cols] += jnp.sum(ds, axis=0, keepdims=True)
                dck_scr[k_rows, :] += ds
                p_scr[:, cols] = p.astype(BF16)
                ds_scr[:, cols] = ds.astype(BF16)
            dv_scr[k_rows, :] += _nn(p_scr[...], do_scr[...])
            dk_scr[k_rows, :] += scale * _nn(ds_scr[...], q_ref[...])
            dq_scr[...] += scale * _tn(ds_scr[...], k_ref[...])

        @pl.when(ki < qi)
        def _():
            update(False)

        @pl.when(ki == qi)
        def _():
            update(True)
            dq_ref[...] = dq_scr[...].astype(BF16)

        @pl.when(pair == n_pairs - 1)
        def _():
            dk_ref[...] = dk_scr[...].astype(BF16)
            dv_ref[...] = dv_scr[...].astype(BF16)
            dck_ref[...] = -jnp.sum(dck_scr[...], axis=-1, keepdims=True)

    h_ = n_heads
    q_row = pl.BlockSpec((None, 1, t), lambda h, p, qt, kt: (h, 0, qt[p]))
    q_blk = pl.BlockSpec((t, e), lambda h, p, qt, kt: (qt[p], h))
    whole = pl.BlockSpec((s, e), lambda h, p, qt, kt: (0, h))
    return _call(
        body, name=name, grid=(h_, n_pairs), tables=(q_tab, k_tab),
        in_specs=[
            q_blk,
            pl.BlockSpec((t, e), lambda h, p, qt, kt: (kt[p], h_ + h)),
            pl.BlockSpec((t, e), lambda h, p, qt, kt: (kt[p], 2 * h_ + h)),
            q_row,
            pl.BlockSpec((None, t, lanes), lambda h, p, qt, kt: (h, kt[p], 0)),
            q_row, q_blk, q_blk,
        ],
        out_specs=[q_blk, whole, whole, q_row,
                   pl.BlockSpec((None, s, 1), lambda h, p, qt, kt: (h, 0, 0))],
        out_shape=[jax.ShapeDtypeStruct((s, h_ * e), BF16)] * 3
        + [jax.ShapeDtypeStruct((h_, 1, s), F32), jax.ShapeDtypeStruct((h_, s, 1), F32)],
        operands=(qkv, qkv, qkv, c_row, ck_lanes, lse, y, dy),
        scratch=[pltpu.VMEM((t, e), F32), pltpu.VMEM((s, e), F32), pltpu.VMEM((s, e), F32),
                 pltpu.VMEM((s, lanes), F32), pltpu.VMEM((t, t), F32), pltpu.VMEM((t, t), F32), pltpu.VMEM((t, t), BF16),
                 pltpu.VMEM((t, t), BF16), pltpu.VMEM((t, e), BF16), pltpu.VMEM((8, t), F32),
                 pltpu.VMEM((1, t), F32)],
        semantics=("parallel", "arbitrary"), carry=carry)


def _rel_bucket_table(dilation, span):
    dist = np.arange(span + 1, dtype=np.int64) * dilation
    max_exact = REL_BUCKETS // 2
    d = np.maximum(dist.astype(np.float32), np.float32(1.0))
    large = max_exact + (np.log(d / np.float32(max_exact)) / np.float32(math.log(REL_MAX_DISTANCE / max_exact))
                         * np.float32(REL_BUCKETS - max_exact)).astype(np.int32)
    large = np.minimum(large, REL_BUCKETS - 1)
    return np.where(dist < max_exact, dist, large)


def _bucket_bands(dilation, span):
    table = _rel_bucket_table(dilation, span)
    bands = []
    for n, b in enumerate(table):
        if bands and bands[-1][0] == int(b):
            bands[-1][2] = n
        else:
            assert not any(bb[0] == int(b) for bb in bands)
            bands.append([int(b), n, n])
    return [tuple(b) for b in bands]


def _steps_back():
    i = lax.broadcasted_iota(jnp.int32, (DIL_BLOCK, DIL_BLOCK), 0)
    j = lax.broadcasted_iota(jnp.int32, (DIL_BLOCK, DIL_BLOCK), 1)
    return i - j, DIL_BLOCK + i - j


def _bias_tiles(rb_ref, h, bands, span):
    n_cur, n_prev = _steps_back()
    t_cur = jnp.zeros((DIL_BLOCK, DIL_BLOCK), F32)
    t_prev = jnp.zeros((DIL_BLOCK, DIL_BLOCK), F32)
    for b, lo, hi in bands:
        val = rb_ref[b, h]
        t_cur = jnp.where((n_cur >= lo) & (n_cur <= hi), val, t_cur)
        t_prev = jnp.where((n_prev >= lo) & (n_prev <= hi), val, t_prev)
    t_cur = jnp.where(n_cur >= 0, t_cur, NEG_INF)
    t_prev = jnp.where(n_prev <= span, t_prev, NEG_INF)
    return t_cur, t_prev


def _dil_rows(start, dilation):
    return pl.ds(start, DIL_BLOCK, stride=dilation) if dilation > 1 else pl.ds(start, DIL_BLOCK)


def _dil_block_groups(s, dilation, run_group):
    group = dilation * DIL_BLOCK
    per_residue = s // group

    def block(r, n):
        if isinstance(n, int):
            return n * group + r, max(n - 1, 0) * group + r, n == 0
        return (pl.multiple_of(n * group, DIL_BLOCK) + r,
                pl.multiple_of(jnp.maximum(n - 1, 0) * group, DIL_BLOCK) + r, n == 0)

    if per_residue >= DIL_GROUP:
        assert per_residue % DIL_GROUP == 0
        for r in range(dilation):
            def trip(it, carry, r=r):
                run_group([block(r, it * DIL_GROUP + j) for j in range(DIL_GROUP)])
                return carry
            lax.fori_loop(0, per_residue // DIL_GROUP, trip, 0)
    else:
        residues = DIL_GROUP // per_residue
        assert residues * per_residue == DIL_GROUP and dilation % residues == 0
        for r0 in range(0, dilation, residues):
            run_group([block(r, n) for r in range(r0, r0 + residues) for n in range(per_residue)])


def _dil_fwd(qkv, rel_bias, n_heads, col0, *, name, carry=None):
    s = qkv.shape[0]
    e = HEAD_DIM
    scale = e ** -0.5
    n_pat = len(DIL_PATTERNS)
    for window, d in DIL_PATTERNS:
        assert window // d == DIL_BLOCK and s % (d * DIL_BLOCK) == 0
    bands = [_bucket_bands(d, w // d) for w, d in DIL_PATTERNS]

    def body(rb_ref, q_ref, k_ref, v_ref, y_ref, lse_ref, qf, kf, vf, *scr):
        o_scr, l_scr = scr[:n_pat], scr[n_pat:]
        h = pl.program_id(0)
        qf[...] = q_ref[...].astype(F32)
        kf[...] = k_ref[...].astype(F32)
        vf[...] = v_ref[...].astype(F32)
        for pi, (window, d) in enumerate(DIL_PATTERNS):
            t_cur, t_prev = _bias_tiles(rb_ref, h, bands[pi], window // d)

            def run_group(blocks, d=d, pi=pi, t_cur=t_cur, t_prev=t_prev):
                rows = [(_dil_rows(q0, d), _dil_rows(p0, d)) for q0, p0, _ in blocks]
                qb = [qf[cur, :].astype(BF16) for cur, _ in rows]
                s_c = [_nt(q, kf[cur, :].astype(BF16)) for q, (cur, _) in zip(qb, rows)]
                s_p = [_nt(q, kf[prev, :].astype(BF16)) for q, (_, prev) in zip(qb, rows)]
                s_c = [x * scale + t_cur for x in s_c]
                s_p = [x * scale + (t_prev + jnp.where(first, NEG_INF, 0.0)) for x, (_, _, first) in zip(s_p, blocks)]
                m = [jnp.maximum(jnp.max(a, axis=-1, keepdims=True), jnp.max(b, axis=-1, keepdims=True))
                     for a, b in zip(s_c, s_p)]
                p_c = [jnp.exp(a - mm) for a, mm in zip(s_c, m)]
                p_p = [jnp.exp(b - mm) for b, mm in zip(s_p, m)]
                l = [jnp.sum(a, axis=-1, keepdims=True) + jnp.sum(b, axis=-1, keepdims=True)
                     for a, b in zip(p_c, p_p)]
                o = [_nn(a.astype(BF16), vf[cur, :].astype(BF16)) + _nn(b.astype(BF16), vf[prev, :].astype(BF16))
                     for a, b, (cur, prev) in zip(p_c, p_p, rows)]
                for (cur, _), oo, ll, mm in zip(rows, o, l, m):
                    o_scr[pi][cur, :] = oo / ll
                    l_scr[pi][cur, :] = mm + jnp.log(ll)

            _dil_block_groups(s, d, run_group)
        lses = [l_scr[pi][...] for pi in range(n_pat)]
        m = functools.reduce(jnp.maximum, lses)
        ws = [jnp.exp(l - m) for l in lses]
        tot = functools.reduce(jnp.add, ws)
        y = functools.reduce(jnp.add, [w * o_scr[pi][...] for pi, w in enumerate(ws)])
        y_ref[...] = y / tot
        lse_ref[...] = m + jnp.log(tot)

    h_ = n_heads
    return _call(
        body, name=name, grid=(h_,),
        in_specs=[pl.BlockSpec(memory_space=pltpu.SMEM),
                  pl.BlockSpec((s, e), lambda h: (0, col0 + h)),
                  pl.BlockSpec((s, e), lambda h: (0, col0 + h_ + h)),
                  pl.BlockSpec((s, e), lambda h: (0, col0 + 2 * h_ + h))],
        out_specs=[pl.BlockSpec((s, e), lambda h: (0, h)), pl.BlockSpec((None, s, 1), lambda h: (h, 0, 0))],
        out_shape=[jax.ShapeDtypeStruct((s, h_ * e), F32), jax.ShapeDtypeStruct((h_, s, 1), F32)],
        operands=(rel_bias, qkv, qkv, qkv),
        scratch=[pltpu.VMEM((s, e), F32)] * 3 + [pltpu.VMEM((s, e), F32)] * n_pat
        + [pltpu.VMEM((s, 1), F32)] * n_pat,
        semantics=("parallel",), carry=carry)


def _dil_bwd(qkv, rel_bias, lse, y, dy, n_heads, col0, *, name, carry=None):
    s = qkv.shape[0]
    e = HEAD_DIM
    scale = e ** -0.5
    bands = [_bucket_bands(d, w // d) for w, d in DIL_PATTERNS]

    def body(rb_ref, q_ref, k_ref, v_ref, lse_ref, y_ref, dy_ref, dq_ref, dk_ref, dv_ref, drb_ref,
             qf, kf, vf, dqf, dkf, dvf, delta, dt_cur, dt_prev):
        h = pl.program_id(0)
        qf[...] = q_ref[...].astype(F32)
        kf[...] = k_ref[...].astype(F32)
        vf[...] = v_ref[...].astype(F32)
        dqf[...] = jnp.zeros_like(dqf)
        dkf[...] = jnp.zeros_like(dkf)
        dvf[...] = jnp.zeros_like(dvf)
        delta[...] = jnp.sum(dy_ref[...] * y_ref[...], axis=-1, keepdims=True)
        lane = lax.broadcasted_iota(jnp.int32, (1, 128), 1)
        drb = jnp.zeros((1, 128), F32)
        n_cur, n_prev = _steps_back()
        for pi, (window, d) in enumerate(DIL_PATTERNS):
            t_cur, t_prev = _bias_tiles(rb_ref, h, bands[pi], window // d)
            dt_cur[...] = jnp.zeros_like(dt_cur)
            dt_prev[...] = jnp.zeros_like(dt_prev)

            def run_group(blocks, d=d, t_cur=t_cur, t_prev=t_prev):
                rows = [(_dil_rows(q0, d), _dil_rows(p0, d)) for q0, p0, _ in blocks]
                qb = [qf[cur, :].astype(BF16) for cur, _ in rows]
                kc = [kf[cur, :].astype(BF16) for cur, _ in rows]
                kp = [kf[prev, :].astype(BF16) for _, prev in rows]
                vc = [vf[cur, :].astype(BF16) for cur, _ in rows]
                vp = [vf[prev, :].astype(BF16) for _, prev in rows]
                do = [dy_ref[cur, :].astype(BF16) for cur, _ in rows]
                lse_b = [lse_ref[cur, :] for cur, _ in rows]
                delta_b = [delta[cur, :] for cur, _ in rows]
                s_c = [_nt(q, k) for q, k in zip(qb, kc)]
                s_p = [_nt(q, k) for q, k in zip(qb, kp)]
                dp_c = [_nt(g, v) for g, v in zip(do, vc)]
                dp_p = [_nt(g, v) for g, v in zip(do, vp)]
                p_c = [jnp.exp(x * scale + t_cur - ls) for x, ls in zip(s_c, lse_b)]
                p_p = [jnp.exp(x * scale + (t_prev + jnp.where(first, NEG_INF, 0.0)) - ls)
                       for x, ls, (_, _, first) in zip(s_p, lse_b, blocks)]
                ds_c = [p * (g - dl) for p, g, dl in zip(p_c, dp_c, delta_b)]
                ds_p = [p * (g - dl) for p, g, dl in zip(p_p, dp_p, delta_b)]
                dt_cur[...] += functools.reduce(jnp.add, ds_c)
                dt_prev[...] += functools.reduce(jnp.add, ds_p)
                p_c, p_p = [p.astype(BF16) for p in p_c], [p.astype(BF16) for p in p_p]
                ds_c, ds_p = [x.astype(BF16) for x in ds_c], [x.astype(BF16) for x in ds_p]
                dq = [scale * (_nn(a, k1) + _nn(b, k2)) for a, b, k1, k2 in zip(ds_c, ds_p, kc, kp)]
                dk_c = [scale * _tn(a, q) for a, q in zip(ds_c, qb)]
                dv_c = [_tn(p, g) for p, g in zip(p_c, do)]
                dk_p = [scale * _tn(b, q) for b, q in zip(ds_p, qb)]
                dv_p = [_tn(p, g) for p, g in zip(p_p, do)]
                for i, (cur, prev) in enumerate(rows):
                    dqf[cur, :] += dq[i]
                    dkf[cur, :] += dk_c[i]
                    dvf[cur, :] += dv_c[i]
                    dkf[prev, :] += dk_p[i]
                    dvf[prev, :] += dv_p[i]

            _dil_block_groups(s, d, run_group)
            dtc, dtp = dt_cur[...], dt_prev[...]
            for b, lo, hi in bands[pi]:
                tot = (jnp.sum(jnp.where((n_cur >= lo) & (n_cur <= hi), dtc, 0.0))
                       + jnp.sum(jnp.where((n_prev >= lo) & (n_prev <= hi), dtp, 0.0)))
                drb = drb + jnp.where(lane == b, tot, 0.0)
        dq_ref[...] = dqf[...].astype(BF16)
        dk_ref[...] = dkf[...].astype(BF16)
        dv_ref[...] = dvf[...].astype(BF16)
        drb_ref[...] = drb

    h_ = n_heads
    col = pl.BlockSpec((s, e), lambda h: (0, h))
    return _call(
        body, name=name, grid=(h_,),
        in_specs=[pl.BlockSpec(memory_space=pltpu.SMEM),
                  pl.BlockSpec((s, e), lambda h: (0, col0 + h)),
                  pl.BlockSpec((s, e), lambda h: (0, col0 + h_ + h)),
                  pl.BlockSpec((s, e), lambda h: (0, col0 + 2 * h_ + h)),
                  pl.BlockSpec((None, s, 1), lambda h: (h, 0, 0)), col, col],
        out_specs=[col, col, col, pl.BlockSpec((None, 1, 128), lambda h: (h, 0, 0))],
        out_shape=[jax.ShapeDtypeStruct((s, h_ * e), BF16)] * 3 + [jax.ShapeDtypeStruct((h_, 1, 128), F32)],
        operands=(rel_bias, qkv, qkv, qkv, lse, y, dy),
        scratch=[pltpu.VMEM((s, e), F32)] * 6 + [pltpu.VMEM((s, 1), F32)]
        + [pltpu.VMEM((DIL_BLOCK, DIL_BLOCK), F32)] * 2,
        semantics=("parallel",), carry=carry)


def _place():
    x, y, c = lax.axis_index("x"), lax.axis_index("y"), lax.axis_index("c")
    chips = [(1 - x, y), (x, 1 - y), (1 - x, 1 - y)]
    return x, y, c, chips


def _block_index(px, py, pc):
    return 4 * px + 2 * py + pc


def _gather_task(blocks):
    n = len(blocks)

    def copies(ins, outs, sems):
        send_sems, recv_sems, local_sems = sems
        x, y, c, chips = _place()
        me, sibling = (x, y, c), (x, y, 1 - c)

        def copy(a, k, block, to, src=None):
            slot = outs[a].at[_block_index(*block)]
            return pltpu.make_async_remote_copy(
                src_ref=slot if src is None else src, dst_ref=slot,
                send_sem=send_sems.at[a, k], recv_sem=recv_sems.at[a, k],
                device_id=to, device_id_type=MESH)

        local = lambda a: pltpu.make_async_copy(ins[a], outs[a].at[_block_index(*me)], local_sems.at[a])
        return copy, local, me, sibling, c, chips

    def start(ins, outs, sems):
        copy, local, me, sibling, c, chips = copies(ins, outs, sems)
        for a in range(n):
            local(a).start()
            copy(a, 0, me, sibling, src=ins[a]).start()
            for j, chip in enumerate(chips):
                copy(a, 1 + j, me, (*chip, c), src=ins[a]).start()

    def forward(ins, outs, sems):
        copy, local, me, sibling, c, chips = copies(ins, outs, sems)
        for a in range(n):
            for j, chip in enumerate(chips):
                copy(a, 1 + j, (*chip, c), me).wait_recv()
                copy(a, 4 + j, (*chip, c), sibling).start()

    def finish(ins, outs, sems):
        copy, local, me, sibling, c, chips = copies(ins, outs, sems)
        for a in range(n):
            copy(a, 0, sibling, me).wait_recv()
            for j, chip in enumerate(chips):
                copy(a, 4 + j, (*chip, 1 - c), me).wait_recv()
        for a in range(n):
            for k in range(7):
                copy(a, k, me, sibling, src=ins[a]).wait_send()
            local(a).wait()

    return _Carried(
        blocks, [jax.ShapeDtypeStruct((N_DEV,) + b.shape, b.dtype) for b in blocks],
        [pltpu.SemaphoreType.DMA((n, 7)), pltpu.SemaphoreType.DMA((n, 7)), pltpu.SemaphoreType.DMA((n,))],
        start, forward, finish)


def _exchange_task(arrays, n_slots, route):
    n = len(arrays)

    def copies(ins, outs, sems):
        send_sems, recv_sems = sems
        out = []
        for a in range(n):
            for j in range(n_slots):
                src, to = route(j)
                out.append(pltpu.make_async_remote_copy(
                    src_ref=ins[a].at[src], dst_ref=outs[a].at[j],
                    send_sem=send_sems.at[a, j], recv_sem=recv_sems.at[a, j],
                    device_id=to, device_id_type=MESH))
        return out

    def start(ins, outs, sems):
        for cp in copies(ins, outs, sems):
            cp.start()

    def finish(ins, outs, sems):
        for cp in copies(ins, outs, sems):
            cp.wait()

    return _Carried(
        arrays, [jax.ShapeDtypeStruct((n_slots,) + g.shape[1:], g.dtype) for g in arrays],
        [pltpu.SemaphoreType.DMA((n, n_slots)), pltpu.SemaphoreType.DMA((n, n_slots))],
        start, lambda ins, outs, sems: None, finish)


def _sibling_task(grads):
    def route(q):
        x, y, c, _ = _place()
        return 2 * q + (1 - c), (x, y, 1 - c)
    return _exchange_task(grads, 4, route)


def _chips_task(parts):
    def route(j):
        x, y, c, chips = _place()
        px, py = chips[j]
        return 2 * px + py, (px, py, c)
    return _exchange_task(parts, 3, route)


def _chip_partials(grad, got, place, *, name):
    _, r, cdim = grad.shape
    tr = _row_tile(r, 512)

    def body(place_ref, g_ref, s_ref, o_ref):
        o_ref[...] = (g_ref[...] + s_ref[...]).astype(BF16)

    return pl.pallas_call(
        body, name=name,
        grid_spec=pltpu.PrefetchScalarGridSpec(
            num_scalar_prefetch=1, grid=(4, r // tr),
            in_specs=[pl.BlockSpec((None, tr, cdim), lambda q, i, pos: (2 * q + pos[2], i, 0)),
                      pl.BlockSpec((None, tr, cdim), lambda q, i, pos: (q, i, 0))],
            out_specs=pl.BlockSpec((None, tr, cdim), lambda q, i, pos: (q, i, 0))),
        out_shape=jax.ShapeDtypeStruct((4, r, cdim), BF16),
        compiler_params=_params("parallel", "parallel"),
    )(place, grad, got)


def _all_reduce_small(v, *, name):
    r, w = v.shape

    def body(v_ref, o_ref, buf, send_sems, recv_sems):
        x, y, c, _ = _place()
        me = _block_index(x, y, c)
        buf[me] = v_ref[...]
        copies = []
        for k in range(1, N_DEV):
            fx, fy, fc = (k >> 2) & 1, (k >> 1) & 1, k & 1
            peer = (x ^ fx, y ^ fy, c ^ fc)
            cp = pltpu.make_async_remote_copy(
                src_ref=v_ref, dst_ref=buf.at[me],
                send_sem=send_sems.at[k - 1], recv_sem=recv_sems.at[k - 1],
                device_id=peer, device_id_type=MESH)
            cp.start()
            copies.append(cp)
        for cp in copies:
            cp.wait()
        acc = buf[0]
        for p in range(1, N_DEV):
            acc = acc + buf[p]
        o_ref[...] = acc

    vmem = pl.BlockSpec(memory_space=pltpu.VMEM)
    return pl.pallas_call(
        body, name=name, in_specs=[vmem], out_specs=vmem,
        out_shape=jax.ShapeDtypeStruct((r, w), F32),
        scratch_shapes=[pltpu.VMEM((N_DEV, r, w), F32), pltpu.SemaphoreType.DMA((N_DEV - 1,)),
                        pltpu.SemaphoreType.DMA((N_DEV - 1,))],
    )(v)


def _adamw_math(w, g, m, v):
    m = ADAM_B1 * m + (1.0 - ADAM_B1) * g
    v = ADAM_B2 * v + (1.0 - ADAM_B2) * (g * g)
    m_hat = m / (1.0 - ADAM_B1 ** ADAM_STEP)
    v_hat = v / (1.0 - ADAM_B2 ** ADAM_STEP)
    delta = -ADAM_LR * (m_hat / (jnp.sqrt(v_hat) + ADAM_EPS) + ADAM_WD * w)
    return delta, m, v


def _adamw_sharded(w, m, v, part, got, place, *, name):
    n_l, r, cdim = w.shape
    tr = _row_tile(r, 256)

    def body(place_ref, *refs):
        w_ref, m_ref, v_ref = refs[:3]
        p_refs, g_refs = refs[3:3 + n_l], refs[3 + n_l:3 + 2 * n_l]
        g_out, d_out, m_out, v_out = refs[3 + 2 * n_l:]
        for l in range(n_l):
            @pl.when(pl.program_id(0) == l)
            def _(l=l):
                g = p_refs[l][...].astype(F32)
                for j in range(3):
                    g = g + g_refs[l][j].astype(F32)
                delta, m_new, v_new = _adamw_math(w_ref[...], g, m_ref[...], v_ref[...])
                g_out[...] = g
                d_out[...] = delta
                m_out[...] = m_new
                v_out[...] = v_new

    shard = pl.BlockSpec((None, tr, cdim), lambda l, i, pos: (l, i, 0))
    p_specs = [pl.BlockSpec((None, tr, cdim), lambda l, i, pos: (2 * pos[0] + pos[1], i, 0))] * n_l
    g_specs = [pl.BlockSpec((3, tr, cdim), lambda l, i, pos: (0, i, 0))] * n_l
    return pl.pallas_call(
        body, name=name,
        grid_spec=pltpu.PrefetchScalarGridSpec(
            num_scalar_prefetch=1, grid=(n_l, r // tr),
            in_specs=[shard] * 3 + p_specs + g_specs, out_specs=[shard] * 4),
        out_shape=[jax.ShapeDtypeStruct(w.shape, F32)] * 4,
        compiler_params=_params("arbitrary", "parallel"),
    )(place, w, m, v, *part, *got)


def _adamw_small(w, g, m, v, *, name):
    def body(w_ref, g_ref, m_ref, v_ref, d_out, m_out, v_out):
        delta, m_new, v_new = _adamw_math(w_ref[...], g_ref[...], m_ref[...], v_ref[...])
        d_out[...] = delta
        m_out[...] = m_new
        v_out[...] = v_new

    vmem = pl.BlockSpec(memory_space=pltpu.VMEM)
    return pl.pallas_call(
        body, name=name, in_specs=[vmem] * 4, out_specs=[vmem] * 3,
        out_shape=[jax.ShapeDtypeStruct(w.shape, F32)] * 3,
    )(w, g, m, v)


def _pad_lanes(a, width=128):
    return jnp.pad(a, ((0, 0), (0, width - a.shape[1])))


def _train_step(x, target, small, shards, place):
    w_in, w_out, w_mlp_in, w_mlp_out = shards
    depth, d_model, in_shard = w_in.shape
    n_heads = small["forget_b"].shape[1]
    dh = n_heads * HEAD_DIM
    d_ff = w_mlp_in.shape[2] * N_DEV
    row = lambda a: a.reshape(1, -1)
    block = {"in": w_in.astype(BF16), "out": w_out.astype(BF16), "mi": w_mlp_in.astype(BF16),
             "mo": w_mlp_out.astype(BF16)}
    whole = {}

    def unpack(kind, l, g):
        if kind == "in":
            full = g.transpose(1, 0, 2).reshape(d_model, N_DEV * in_shard)
            whole["qkv", l] = jnp.concatenate([full[:, :3 * dh], full[:, 3 * dh + n_heads:]], axis=1)
            whole["f", l] = _pad_lanes(full[:, 3 * dh:3 * dh + n_heads])
        elif kind == "out":
            whole["out", l] = g.reshape(2 * dh, d_model)
        elif kind == "mi":
            whole["mi", l] = g.transpose(1, 0, 2).reshape(d_model, d_ff)
        else:
            whole["mo", l] = g.reshape(d_ff, d_model)

    def gather(kinds, l):
        return _gather_task([block[k][l] for k in kinds])

    (g,) = _run_carried(gather(["in"], 0), name="gather_w_in_0")
    unpack("in", 0, g)

    saved = []
    for l in range(depth):
        h1 = _rmsnorm_fwd(x, row(small["norm1_g"][l]), name=f"norm1_fwd_{l}")
        qkv = _matmul(h1, whole["qkv", l], out_dtype=BF16, name=f"qkv_proj_{l}")
        f = _matmul(h1, whole["f", l], name=f"gate_proj_{l}")
        fb = _pad_lanes(row(small["forget_b"][l]))
        c = _gates_fwd(f, fb, name=f"gates_fwd_{l}")
        c_heads = c[:, :n_heads].T
        cq = c_heads[:, None, :]
        ck = jnp.broadcast_to(c_heads[:, :, None] * LOG2E, c_heads.shape + (128,))
        kinds = (["out"] if l == 0 else []) + ["mi", "mo"]
        (ya, lse_a), got = _fox_fwd(qkv, cq, ck, n_heads, name=f"fox_fwd_{l}", carry=gather(kinds, l))
        for k, g in zip(kinds, got):
            unpack(k, l, g)
        if l + 1 < depth:
            (yb, lse_b), got = _dil_fwd(qkv, small["rel_bias"], n_heads, 3 * n_heads, name=f"dil_fwd_{l}",
                                        carry=gather(["in", "out"], l + 1))
            unpack("in", l + 1, got[0])
            unpack("out", l + 1, got[1])
        else:
            yb, lse_b = _dil_fwd(qkv, small["rel_bias"], n_heads, 3 * n_heads, name=f"dil_fwd_{l}")
        mixed = _outnorm_fwd(ya, yb, row(small["outnorm_a_g"][l]), row(small["outnorm_b_g"][l]),
                             name=f"outnorm_fwd_{l}")
        x1 = _matmul(mixed, whole["out", l], addend=x, name=f"out_proj_{l}")
        h2 = _rmsnorm_fwd(x1, row(small["norm2_g"][l]), name=f"norm2_fwd_{l}")
        relu, act = _matmul(h2, whole["mi", l], out_dtype=BF16, mode="relu_sq", name=f"mlp_in_{l}")
        x2 = _matmul(act, whole["mo", l], addend=x1, name=f"mlp_out_{l}")
        saved.append((x, h1, qkv, f, fb, cq, ck, ya, lse_a, yb, lse_b, mixed, x1, h2, relu, act))
        x = x2
    dx, loss_tile, d_final = _loss_head(x, row(small["final_norm_g"]), target, name="loss_head")

    parts = {k: [None] * depth for k in ("in", "out", "mi", "mo")}
    got = {k: [None] * depth for k in ("in", "out", "mi", "mo")}
    small_grads = [None] * depth
    waiting = None
    for l in reversed(range(depth)):
        x0, h1, qkv, f, fb, cq, ck, ya, lse_a, yb, lse_b, mixed, x1, h2, relu, act = saved[l]
        mlp_out_dx = functools.partial(_matmul, dx, whole["mo", l], tb=True, out_dtype=BF16, mode="mul2", aux=relu,
                                       name=f"mlp_out_dx_{l}")
        mlp_in_dw = functools.partial(_matmul, h2, tk=1024, ta=True, out_chunks=N_DEV, name=f"mlp_in_dw_{l}")
        if waiting is None:
            du = mlp_out_dx()
        else:
            du, (from_sibling,) = mlp_out_dx(carry=_sibling_task([waiting]))
            parts["in"][l + 1] = _chip_partials(waiting, from_sibling, place, name=f"chip_partials_in_{l + 1}")
        dw_mo = _matmul(act, dx, ta=True, tk=1024, name=f"mlp_out_dw_{l}")
        dh2 = _matmul(du, whole["mi", l], tb=True, name=f"mlp_in_dx_{l}")
        if waiting is None:
            dw_mi = mlp_in_dw(du)
        else:
            dw_mi, (got["in"][l + 1],) = mlp_in_dw(du, carry=_chips_task([parts["in"][l + 1]]))
        dx1, d_norm2 = _rmsnorm_bwd(x1, row(small["norm2_g"][l]), dh2, dx, name=f"norm2_bwd_{l}")
        dmixed = _matmul(dx1, whole["out", l], tb=True, name=f"out_proj_dx_{l}")
        dw_out = _matmul(mixed, dx1, ta=True, tk=1024, name=f"out_proj_dw_{l}")
        dya, dyb, d_ga, d_gb = _outnorm_bwd(ya, yb, row(small["outnorm_a_g"][l]), row(small["outnorm_b_g"][l]),
                                            dmixed, name=f"outnorm_bwd_{l}")
        group = {"mo": dw_mo.reshape(N_DEV, d_ff // N_DEV, d_model), "mi": dw_mi,
                 "out": dw_out.reshape(N_DEV, 2 * dh // N_DEV, d_model)}
        kinds = list(group)
        (dqa, dka, dva, dcq, dck), from_sibling = _fox_bwd(
            qkv, cq, ck, lse_a, ya, dya, n_heads, name=f"fox_bwd_{l}",
            carry=_sibling_task([group[k] for k in kinds]))
        for k, s_ in zip(kinds, from_sibling):
            parts[k][l] = _chip_partials(group[k], s_, place, name=f"chip_partials_{k}_{l}")
        (dqb, dkb, dvb, d_rb), from_chips = _dil_bwd(
            qkv, small["rel_bias"], lse_b, yb, dyb, n_heads, 3 * n_heads, name=f"dil_bwd_{l}",
            carry=_chips_task([parts[k][l] for k in kinds]))
        for k, c_ in zip(kinds, from_chips):
            got[k][l] = c_
        dc = _pad_lanes((dcq[:, 0, :] + dck[:, :, 0]).T)
        df, dfb = _gates_bwd(f, fb, dc, name=f"gates_bwd_{l}")
        dqkv = jnp.concatenate([dqa, dka, dva, dqb, dkb, dvb], axis=1)
        dw_qkv = _matmul(h1, dqkv, ta=True, tk=1024, name=f"qkv_proj_dw_{l}")
        dw_f = _matmul(h1, df, ta=True, tk=1024, name=f"gate_proj_dw_{l}")
        dw_in = jnp.concatenate([dw_qkv[:, :3 * dh], dw_f[:, :n_heads], dw_qkv[:, 3 * dh:]], axis=1)
        dw_in = dw_in.reshape(d_model, N_DEV, in_shard).transpose(1, 0, 2)
        dh1 = _matmul(df, whole["f", l], tb=True, name=f"gate_proj_dx_{l}")
        qkv_dx = functools.partial(_matmul, dqkv, whole["qkv", l], tb=True, addend=dh1, name=f"qkv_proj_dx_{l}")
        norm1_bwd = functools.partial(_rmsnorm_bwd, x0, row(small["norm1_g"][l]), name=f"norm1_bwd_{l}")
        if l > 0:
            dx, d_norm1 = norm1_bwd(qkv_dx(), dx1)
            waiting = dw_in
        else:
            dh1, (from_sibling,) = qkv_dx(carry=_sibling_task([dw_in]))
            parts["in"][0] = _chip_partials(dw_in, from_sibling, place, name="chip_partials_in_0")
            (dx, d_norm1), (got["in"][0],) = norm1_bwd(dh1, dx1, carry=_chips_task([parts["in"][0]]))
        small_grads[l] = dict(norm1_g=d_norm1[0], forget_b=dfb[0, :n_heads], rel_bias=d_rb[:, 0, :REL_BUCKETS].T,
                              outnorm_a_g=d_ga[0], outnorm_b_g=d_gb[0], norm2_g=d_norm2[0])
    return loss_tile[0, 0], dx, d_final[0], small_grads, parts, got


def _pack_small(parts, rows):
    flat = jnp.concatenate([p.reshape(-1).astype(F32) for p in parts])
    return jnp.pad(flat, (0, rows * 128 - flat.shape[0])).reshape(rows, 128)


def _unpack_small(packed, shapes):
    flat = packed.reshape(-1)
    out, pos = [], 0
    for shp in shapes:
        size = int(np.prod(shp))
        out.append(flat[pos:pos + size].reshape(shp))
        pos += size
    return out


def kernel(x, norm1_g, w_in, forget_b, rel_bias, outnorm_a_g, outnorm_b_g, w_out, norm2_g, w_mlp_in, w_mlp_out, final_norm_g, loss_target, m_norm1_g, m_w_in, m_forget_b, m_rel_bias, m_outnorm_a_g, m_outnorm_b_g, m_w_out, m_norm2_g, m_w_mlp_in, m_w_mlp_out, m_final_norm_g, v_norm1_g, v_w_in, v_forget_b, v_rel_bias, v_outnorm_a_g, v_outnorm_b_g, v_w_out, v_norm2_g, v_w_mlp_in, v_w_mlp_out, v_final_norm_g):
    depth, d_model, in_shard = w_in.shape
    n_heads = forget_b.shape[1]
    assert in_shard * N_DEV == 6 * n_heads * HEAD_DIM + n_heads and x.shape[0] == 1
    place = jnp.stack([lax.axis_index("x"), lax.axis_index("y"), lax.axis_index("c")]).astype(jnp.int32)

    small_names = ["norm1_g", "forget_b", "rel_bias", "outnorm_a_g", "outnorm_b_g", "norm2_g", "final_norm_g"]
    small_w = dict(norm1_g=norm1_g, forget_b=forget_b, rel_bias=rel_bias, outnorm_a_g=outnorm_a_g,
                   outnorm_b_g=outnorm_b_g, norm2_g=norm2_g, final_norm_g=final_norm_g)
    loss_part, dx, d_final, grads, parts, got = _train_step(
        x[0], loss_target[0], small_w, (w_in, w_out, w_mlp_in, w_mlp_out), place)

    small_m = dict(norm1_g=m_norm1_g, forget_b=m_forget_b, rel_bias=m_rel_bias, outnorm_a_g=m_outnorm_a_g,
                   outnorm_b_g=m_outnorm_b_g, norm2_g=m_norm2_g, final_norm_g=m_final_norm_g)
    small_v = dict(norm1_g=v_norm1_g, forget_b=v_forget_b, rel_bias=v_rel_bias, outnorm_a_g=v_outnorm_a_g,
                   outnorm_b_g=v_outnorm_b_g, norm2_g=v_norm2_g, final_norm_g=v_final_norm_g)
    small_g = {k: jnp.stack([g[k] for g in grads]) for k in small_names if k not in ("rel_bias", "final_norm_g")}
    small_g["rel_bias"] = functools.reduce(jnp.add, [g["rel_bias"] for g in grads])
    small_g["final_norm_g"] = d_final
    shapes = [small_w[k].shape for k in small_names]
    total = sum(int(np.prod(s)) for s in shapes) + 1
    rows = -(-total // (8 * 128)) * 8
    packed_g = _pack_small([small_g[k] for k in small_names] + [loss_part], rows)
    packed_g = _all_reduce_small(packed_g, name="reduce_small")
    zero = jnp.zeros((1,), F32)
    packed_w = _pack_small([small_w[k] for k in small_names] + [zero], rows)
    packed_m = _pack_small([small_m[k] for k in small_names] + [zero], rows)
    packed_v = _pack_small([small_v[k] for k in small_names] + [zero + 1.0], rows)
    packed_d, packed_nm, packed_nv = _adamw_small(packed_w, packed_g, packed_m, packed_v, name="adamw_small")
    g_small = dict(zip(small_names, _unpack_small(packed_g, shapes)))
    d_small = dict(zip(small_names, _unpack_small(packed_d, shapes)))
    nm_small = dict(zip(small_names, _unpack_small(packed_nm, shapes)))
    nv_small = dict(zip(small_names, _unpack_small(packed_nv, shapes)))
    loss = packed_g.reshape(-1)[total - 1]

    big_out = {}
    for kind, nm, w, m, v in [("in", "w_in", w_in, m_w_in, v_w_in), ("out", "w_out", w_out, m_w_out, v_w_out),
                              ("mi", "w_mlp_in", w_mlp_in, m_w_mlp_in, v_w_mlp_in),
                              ("mo", "w_mlp_out", w_mlp_out, m_w_mlp_out, v_w_mlp_out)]:
        big_out[nm] = _adamw_sharded(w, m, v, parts[kind], got[kind], place, name=f"adamw_{nm}")

    order = ["norm1_g", "w_in", "forget_b", "rel_bias", "outnorm_a_g", "outnorm_b_g", "w_out", "norm2_g",
             "w_mlp_in", "w_mlp_out", "final_norm_g"]
    pick = lambda k, idx, small: big_out[k][idx] if k in big_out else small[k]
    outs = [loss, dx[None]]
    outs += [pick(k, 0, g_small) for k in order]
    outs += [pick(k, 1, d_small) for k in order]
    outs += [pick(k, 2, nm_small) for k in order]
    outs += [pick(k, 3, nv_small) for k in order]
    return tuple(outs)
```

```python
import functools
import math

import numpy as np
import jax
import jax.numpy as jnp
from jax import lax
from jax.experimental import pallas as pl
from jax.experimental.pallas import tpu as pltpu

F32 = jnp.float32
BF16 = jnp.bfloat16

HEAD_DIM = 128
DIL_PATTERNS = ((128, 1), (512, 4), (2048, 16))
DIL_BLOCK = 128
REL_BUCKETS = 32
REL_MAX_DISTANCE = 2048
NORM_EPS = 1e-6
NEG_INF = -1e30
ADAM_LR = 0.001
ADAM_B1 = 0.9
ADAM_B2 = 0.999
ADAM_EPS = 1e-08
ADAM_WD = 0.01
ADAM_STEP = 10

N_DEV = 8
V7X_VMEM_LIMIT_BYTES = 56 * 1024 * 1024
MESH = pl.DeviceIdType.MESH


def _params(*semantics):
    return pltpu.CompilerParams(dimension_semantics=semantics, vmem_limit_bytes=V7X_VMEM_LIMIT_BYTES)


HBM_SPEC = pl.BlockSpec(memory_space=pltpu.HBM)


class _Carried:
    def __init__(self, operands, out_shape, scratch, start, forward, finish):
        self.operands, self.out_shape, self.scratch = list(operands), list(out_shape), list(scratch)
        self.start, self.forward, self.finish = start, forward, finish


FORWARD_AT = 0.8


def _call(body, *, name, grid, in_specs, out_specs, out_shape, operands, scratch=(), semantics, carry=None,
          tables=()):
    in_specs, out_specs, out_shape, scratch = list(in_specs), list(out_specs), list(out_shape), list(scratch)
    n_tab = len(tables)

    def run(fn, in_specs, out_specs, out_shape, scratch, operands, semantics):
        return pl.pallas_call(
            fn, name=name,
            grid_spec=pltpu.PrefetchScalarGridSpec(
                num_scalar_prefetch=n_tab, grid=grid, in_specs=in_specs, out_specs=out_specs,
                scratch_shapes=scratch),
            out_shape=out_shape, compiler_params=_params(*semantics))(*tables, *operands)

    if carry is None:
        return run(body, in_specs, out_specs, out_shape, scratch, operands, semantics)
    n_in, n_out = len(in_specs), len(out_specs)
    ci, co, cs = len(carry.operands), len(carry.out_shape), len(carry.scratch)
    steps = int(np.prod(grid))
    forward_step = min(int(steps * FORWARD_AT), steps - 1)

    def carrying(*refs):
        tabs, refs = refs[:n_tab], refs[n_tab:]
        main_in, c_in = refs[:n_in], refs[n_in:n_in + ci]
        main_out = refs[n_in + ci:n_in + ci + n_out]
        c_out = refs[n_in + ci + n_out:n_in + ci + n_out + co]
        rest = refs[n_in + ci + n_out + co:]
        main_scr, c_scr = rest[:len(rest) - cs], rest[len(rest) - cs:]
        step = 0
        for axis, extent in enumerate(grid):
            step = step * extent + pl.program_id(axis)

        @pl.when(step == 0)
        def _():
            carry.start(c_in, c_out, c_scr)

        body(*tabs, *main_in, *main_out, *main_scr)

        @pl.when(step == forward_step)
        def _():
            carry.forward(c_in, c_out, c_scr)

        @pl.when(step == steps - 1)
        def _():
            carry.finish(c_in, c_out, c_scr)

    res = run(carrying, in_specs + [HBM_SPEC] * ci, out_specs + [HBM_SPEC] * co, out_shape + carry.out_shape,
              scratch + carry.scratch, (*operands, *carry.operands), ["arbitrary"] * len(grid))
    return res[:n_out], res[n_out:]


def _run_carried(carry, *, name):
    ci, co = len(carry.operands), len(carry.out_shape)

    def body(*refs):
        parts = (refs[:ci], refs[ci:ci + co], refs[ci + co:])
        carry.start(*parts)
        carry.forward(*parts)
        carry.finish(*parts)

    return pl.pallas_call(
        body, name=name, in_specs=[HBM_SPEC] * ci, out_specs=[HBM_SPEC] * co,
        out_shape=carry.out_shape, scratch_shapes=carry.scratch)(*carry.operands)


def _fit(dim, want, unit=128):
    if dim <= want:
        return dim
    t = want - want % unit
    while dim % t:
        t -= unit
    return t


def _matmul(a, b, *, name, ta=False, tb=False, out_dtype=F32, tm=1024, tn=1024, tk=2048,
            addend=None, mode=None, aux=None, out_chunks=None, carry=None):
    m_dim, k_dim = (a.shape[1], a.shape[0]) if ta else a.shape
    n_dim = b.shape[0] if tb else b.shape[1]
    assert (b.shape[1] if tb else b.shape[0]) == k_dim
    tm, tn, tk = _fit(m_dim, tm), _fit(n_dim // (out_chunks or 1), tn), _fit(k_dim, tk)
    assert m_dim % tm == 0 and n_dim % tn == 0 and k_dim % tk == 0, (name, a.shape, b.shape)
    nk = k_dim // tk
    a_spec = (pl.BlockSpec((tk, tm), lambda i, j, k: (k, i)) if ta
              else pl.BlockSpec((tm, tk), lambda i, j, k: (i, k)))
    b_spec = (pl.BlockSpec((tn, tk), lambda i, j, k: (j, k)) if tb
              else pl.BlockSpec((tk, tn), lambda i, j, k: (k, j)))
    mn_spec = pl.BlockSpec((tm, tn), lambda i, j, k: (i, j))
    if out_chunks is None:
        o_spec, o_shape = mn_spec, (m_dim, n_dim)
    else:
        per = n_dim // out_chunks // tn
        assert per * tn * out_chunks == n_dim
        o_spec = pl.BlockSpec((None, tm, tn), lambda i, j, k: (j // per, i, j % per))
        o_shape = (out_chunks, m_dim, n_dim // out_chunks)
    dims = (((0 if ta else 1,), (1 if tb else 0,)), ((), ()))
    n_out = 2 if mode == "relu_sq" else 1
    in_specs, operands = [a_spec, b_spec], [a, b]
    if addend is not None:
        in_specs.append(mn_spec)
        operands.append(addend)
    if mode == "mul2":
        in_specs.append(mn_spec)
        operands.append(aux)

    def body(*refs):
        a_ref, b_ref = refs[0], refs[1]
        pos = 2
        add_ref = aux_ref = None
        if addend is not None:
            add_ref, pos = refs[pos], pos + 1
        if mode == "mul2":
            aux_ref, pos = refs[pos], pos + 1
        outs = refs[pos:pos + n_out]

        def finish(acc):
            if add_ref is not None:
                acc = acc + add_ref[...].astype(F32)
            if mode == "relu_sq":
                r = jnp.maximum(acc, 0.0)
                outs[0][...] = r.astype(outs[0].dtype)
                outs[1][...] = (r * r).astype(outs[1].dtype)
            elif mode == "mul2":
                outs[0][...] = (acc * (2.0 * aux_ref[...].astype(F32))).astype(outs[0].dtype)
            else:
                outs[0][...] = acc.astype(outs[0].dtype)

        part = lax.dot_general(a_ref[...].astype(BF16), b_ref[...].astype(BF16), dims,
                               preferred_element_type=F32)
        if nk == 1:
            finish(part)
        else:
            acc_ref = refs[-1]
            k = pl.program_id(2)

            @pl.when(k == 0)
            def _():
                acc_ref[...] = part

            @pl.when(k > 0)
            def _():
                acc_ref[...] += part

            @pl.when(k == nk - 1)
            def _():
                finish(acc_ref[...])

    out_shape = [jax.ShapeDtypeStruct(o_shape, out_dtype)] * n_out
    res = _call(
        body, name=name, grid=(m_dim // tm, n_dim // tn, nk),
        in_specs=in_specs, out_specs=[o_spec] * n_out, out_shape=out_shape, operands=operands,
        scratch=[pltpu.VMEM((tm, tn), F32)] if nk > 1 else [],
        semantics=("parallel", "parallel", "arbitrary"), carry=carry)
    if carry is not None:
        res, carried = res
        return (res if n_out > 1 else res[0]), carried
    return res if n_out > 1 else res[0]


def _norm_fwd_math(x, g):
    r = lax.rsqrt(jnp.mean(x * x, axis=-1, keepdims=True) + NORM_EPS)
    return (x * r) * g


def _norm_bwd_math(x, g, dy):
    r = lax.rsqrt(jnp.mean(x * x, axis=-1, keepdims=True) + NORM_EPS)
    xh = x * r
    dxh = dy * g
    dx = r * (dxh - xh * jnp.mean(dxh * xh, axis=-1, keepdims=True))
    return dx, jnp.sum(dy * xh, axis=0, keepdims=True)


def _row_tile(rows, want=256):
    t = min(rows, want)
    assert rows % t == 0
    return t


def _rmsnorm_fwd(x, g, *, name):
    s, d = x.shape
    tr = _row_tile(s)

    def body(x_ref, g_ref, h_ref):
        h_ref[...] = _norm_fwd_math(x_ref[...], g_ref[...]).astype(BF16)

    return pl.pallas_call(
        body, name=name, grid=(s // tr,),
        in_specs=[pl.BlockSpec((tr, d), lambda i: (i, 0)), pl.BlockSpec((1, d), lambda i: (0, 0))],
        out_specs=pl.BlockSpec((tr, d), lambda i: (i, 0)),
        out_shape=jax.ShapeDtypeStruct((s, d), BF16),
        compiler_params=_params("parallel"),
    )(x, g)


def _rmsnorm_bwd(x, g, dh, dres, *, name, carry=None):
    s, d = x.shape
    tr = _row_tile(s)

    def body(x_ref, g_ref, dh_ref, dres_ref, dx_ref, dg_ref):
        dx, dg = _norm_bwd_math(x_ref[...], g_ref[...], dh_ref[...])
        dx_ref[...] = dres_ref[...] + dx

        @pl.when(pl.program_id(0) == 0)
        def _():
            dg_ref[...] = dg

        @pl.when(pl.program_id(0) > 0)
        def _():
            dg_ref[...] += dg

    row = pl.BlockSpec((tr, d), lambda i: (i, 0))
    vec = pl.BlockSpec((1, d), lambda i: (0, 0))
    return _call(
        body, name=name, grid=(s // tr,),
        in_specs=[row, vec, row, row], out_specs=[row, vec],
        out_shape=[jax.ShapeDtypeStruct((s, d), F32), jax.ShapeDtypeStruct((1, d), F32)],
        operands=(x, g, dh, dres), semantics=("arbitrary",), carry=carry)


def _outnorm_fwd(ya, yb, ga, gb, *, name):
    s, da = ya.shape
    db = yb.shape[1]
    tr = _row_tile(s)

    def body(ya_ref, yb_ref, ga_ref, gb_ref, o_ref):
        o_ref[:, :da] = _norm_fwd_math(ya_ref[...], ga_ref[...]).astype(BF16)
        o_ref[:, da:] = _norm_fwd_math(yb_ref[...], gb_ref[...]).astype(BF16)

    return pl.pallas_call(
        body, name=name, grid=(s // tr,),
        in_specs=[pl.BlockSpec((tr, da), lambda i: (i, 0)), pl.BlockSpec((tr, db), lambda i: (i, 0)),
                  pl.BlockSpec((1, da), lambda i: (0, 0)), pl.BlockSpec((1, db), lambda i: (0, 0))],
        out_specs=pl.BlockSpec((tr, da + db), lambda i: (i, 0)),
        out_shape=jax.ShapeDtypeStruct((s, da + db), BF16),
        compiler_params=_params("parallel"),
    )(ya, yb, ga, gb)


def _outnorm_bwd(ya, yb, ga, gb, dmixed, *, name):
    s, da = ya.shape
    db = yb.shape[1]
    tr = _row_tile(s)

    def body(ya_ref, yb_ref, ga_ref, gb_ref, dm_ref, dya_ref, dyb_ref, dga_ref, dgb_ref):
        dxa, dga = _norm_bwd_math(ya_ref[...], ga_ref[...], dm_ref[:, :da])
        dxb, dgb = _norm_bwd_math(yb_ref[...], gb_ref[...], dm_ref[:, da:])
        dya_ref[...] = dxa
        dyb_ref[...] = dxb

        @pl.when(pl.program_id(0) == 0)
        def _():
            dga_ref[...] = dga
            dgb_ref[...] = dgb

        @pl.when(pl.program_id(0) > 0)
        def _():
            dga_ref[...] += dga
            dgb_ref[...] += dgb

    ra = pl.BlockSpec((tr, da), lambda i: (i, 0))
    rb = pl.BlockSpec((tr, db), lambda i: (i, 0))
    va = pl.BlockSpec((1, da), lambda i: (0, 0))
    vb = pl.BlockSpec((1, db), lambda i: (0, 0))
    return pl.pallas_call(
        body, name=name, grid=(s // tr,),
        in_specs=[ra, rb, va, vb, pl.BlockSpec((tr, da + db), lambda i: (i, 0))],
        out_specs=[ra, rb, va, vb],
        out_shape=[jax.ShapeDtypeStruct((s, da), F32), jax.ShapeDtypeStruct((s, db), F32),
                   jax.ShapeDtypeStruct((1, da), F32), jax.ShapeDtypeStruct((1, db), F32)],
        compiler_params=_params("arbitrary"),
    )(ya, yb, ga, gb, dmixed)


def _loss_head(x, g, target, *, name):
    s, d = x.shape
    tr = _row_tile(s)

    def body(x_ref, g_ref, t_ref, dx_ref, loss_ref, dg_ref):
        xv, gv = x_ref[...], g_ref[...]
        err = _norm_fwd_math(xv, gv) - t_ref[...]
        part = 0.5 * jnp.sum(jnp.mean(err * err, axis=-1, keepdims=True), axis=0, keepdims=True)
        dx, dg = _norm_bwd_math(xv, gv, err * (1.0 / d))
        dx_ref[...] = dx
        part = jnp.broadcast_to(part, (8, 128))

        @pl.when(pl.program_id(0) == 0)
        def _():
            dg_ref[...] = dg
            loss_ref[...] = part

        @pl.when(pl.program_id(0) > 0)
        def _():
            dg_ref[...] += dg
            loss_ref[...] += part

    row = pl.BlockSpec((tr, d), lambda i: (i, 0))
    vec = pl.BlockSpec((1, d), lambda i: (0, 0))
    return pl.pallas_call(
        body, name=name, grid=(s // tr,),
        in_specs=[row, vec, row],
        out_specs=[row, pl.BlockSpec((8, 128), lambda i: (0, 0)), vec],
        out_shape=[jax.ShapeDtypeStruct((s, d), F32), jax.ShapeDtypeStruct((8, 128), F32),
                   jax.ShapeDtypeStruct((1, d), F32)],
        compiler_params=_params("arbitrary"),
    )(x, g, target)


def _split3(x):
    hi = x.astype(BF16)
    rem = x - hi.astype(F32)
    mid = rem.astype(BF16)
    lo = (rem - mid.astype(F32)).astype(BF16)
    return hi, mid, lo


def _tri_sum(tri, x):
    hi, mid, lo = _split3(x)
    dot = functools.partial(jnp.dot, preferred_element_type=F32)
    return dot(tri, hi) + dot(tri, mid) + dot(tri, lo)


def _gates_fwd(f, fb, *, name):
    s, w = f.shape
    tb = 128
    nb = s // tb

    def body(f_ref, fb_ref, c_ref, carry):
        @pl.when(pl.program_id(0) == 0)
        def _():
            carry[...] = jnp.zeros_like(carry)

        logf = jax.nn.log_sigmoid(f_ref[...] + fb_ref[...])
        row = lax.broadcasted_iota(jnp.int32, (tb, tb), 0)
        col = lax.broadcasted_iota(jnp.int32, (tb, tb), 1)
        tri = (row >= col).astype(BF16)
        c = _tri_sum(tri, logf) + carry[...]
        c_ref[...] = c
        carry[...] = c[tb - 1:tb, :]

    return pl.pallas_call(
        body, name=name, grid=(nb,),
        in_specs=[pl.BlockSpec((tb, w), lambda i: (i, 0)), pl.BlockSpec((1, w), lambda i: (0, 0))],
        out_specs=pl.BlockSpec((tb, w), lambda i: (i, 0)),
        out_shape=jax.ShapeDtypeStruct((s, w), F32),
        scratch_shapes=[pltpu.VMEM((1, w), F32)],
        compiler_params=_params("arbitrary"),
    )(f, fb)


def _gates_bwd(f, fb, dc, *, name):
    s, w = f.shape
    tb = 128
    nb = s // tb

    def body(f_ref, fb_ref, dc_ref, df_ref, dfb_ref, carry):
        @pl.when(pl.program_id(0) == 0)
        def _():
            carry[...] = jnp.zeros_like(carry)
            dfb_ref[...] = jnp.zeros_like(dfb_ref)

        row = lax.broadcasted_iota(jnp.int32, (tb, tb), 0)
        col = lax.broadcasted_iota(jnp.int32, (tb, tb), 1)
        tri = (row <= col).astype(BF16)
        dlogf = _tri_sum(tri, dc_ref[...]) + carry[...]
        carry[...] = dlogf[0:1, :]
        df = dlogf * jax.nn.sigmoid(-(f_ref[...] + fb_ref[...]))
        df_ref[...] = df
        dfb_ref[...] += jnp.sum(df, axis=0, keepdims=True)

    rev = pl.BlockSpec((tb, w), lambda i: (nb - 1 - i, 0))
    vec = pl.BlockSpec((1, w), lambda i: (0, 0))
    return pl.pallas_call(
        body, name=name, grid=(nb,),
        in_specs=[rev, vec, rev], out_specs=[rev, vec],
        out_shape=[jax.ShapeDtypeStruct((s, w), F32), jax.ShapeDtypeStruct((1, w), F32)],
        scratch_shapes=[pltpu.VMEM((1, w), F32)],
        compiler_params=_params("arbitrary"),
    )(f, fb, dc)


def _nt(a, b):
    return lax.dot_general(a, b, (((1,), (1,)), ((), ())), preferred_element_type=F32)


def _tn(a, b):
    return lax.dot_general(a, b, (((0,), (0,)), ((), ())), preferred_element_type=F32)


def _nn(a, b):
    return jnp.dot(a, b, preferred_element_type=F32)


DIL_GROUP = 4
FOX_TILE = 1024
LOG2E = 1.4426950408889634


def _causal_pairs(nt, key_major):
    if key_major:
        pairs = [(q, k) for k in range(nt) for q in range(k, nt)]
    else:
        pairs = [(q, k) for q in range(nt) for k in range(q + 1)]
    return (jnp.asarray([p[0] for p in pairs], jnp.int32), jnp.asarray([p[1] for p in pairs], jnp.int32))


def _fox_fwd(qkv, c_row, ck_lanes, n_heads, *, name, carry=None):
    s = qkv.shape[0]
    e = HEAD_DIM
    t = min(FOX_TILE, s)
    nt = s // t
    scale2 = e ** -0.5 * LOG2E
    lanes = 128
    q_tab, k_tab = _causal_pairs(nt, key_major=False)

    def body(q_tab, k_tab, q_ref, k_ref, v_ref, cq_ref, ck_ref, o_ref, lse_ref,
             m_scr, l_scr, acc_scr, s_scr, p_scr):
        pair = pl.program_id(1)
        qi, ki = q_tab[pair], k_tab[pair]

        @pl.when(ki == 0)
        def _():
            m_scr[...] = jnp.full_like(m_scr, NEG_INF)
            l_scr[...] = jnp.zeros_like(l_scr)
            acc_scr[...] = jnp.zeros_like(acc_scr)

        def update(diagonal):
            s_scr[...] = _nt(k_ref[...], q_ref[...])
            ck2 = ck_ref[...]
            for c0 in range(0, t, lanes):
                cols = pl.ds(c0, lanes)

                def logits2():
                    x = s_scr[:, cols] * scale2 - ck2
                    if diagonal:
                        key = lax.broadcasted_iota(jnp.int32, x.shape, 0)
                        qry = c0 + lax.broadcasted_iota(jnp.int32, x.shape, 1)
                        x = jnp.where(key <= qry, x, NEG_INF)
                    return x

                m_old = m_scr[:, cols]
                m_new = jnp.maximum(m_old, jnp.max(logits2(), axis=0, keepdims=True))
                p = jnp.exp2(logits2() - m_new)
                alpha = jnp.exp2(m_old - m_new)
                l_scr[:, cols] = alpha * l_scr[:, cols] + jnp.sum(p, axis=0, keepdims=True)
                m_scr[:, cols] = m_new
                acc_scr[:, cols] = alpha * acc_scr[:, cols]
                p_scr[:, cols] = p.astype(BF16)
            acc_scr[...] += _tn(v_ref[...], p_scr[...])

        @pl.when(ki < qi)
        def _():
            update(False)

        @pl.when(ki == qi)
        def _():
            update(True)
            o_ref[...] = (acc_scr[...] / l_scr[...]).T
            lse_ref[...] = (m_scr[...] + jnp.log2(l_scr[...])) * (1.0 / LOG2E) + cq_ref[...]

    h_ = n_heads
    return _call(
        body, name=name, grid=(h_, int(q_tab.shape[0])), tables=(q_tab, k_tab),
        in_specs=[
            pl.BlockSpec((t, e), lambda h, p, qt, kt: (qt[p], h)),
            pl.BlockSpec((t, e), lambda h, p, qt, kt: (kt[p], h_ + h)),
            pl.BlockSpec((t, e), lambda h, p, qt, kt: (kt[p], 2 * h_ + h)),
            pl.BlockSpec((None, 1, t), lambda h, p, qt, kt: (h, 0, qt[p])),
            pl.BlockSpec((None, t, lanes), lambda h, p, qt, kt: (h, kt[p], 0)),
        ],
        out_specs=[pl.BlockSpec((t, e), lambda h, p, qt, kt: (qt[p], h)),
                   pl.BlockSpec((None, 1, t), lambda h, p, qt, kt: (h, 0, qt[p]))],
        out_shape=[jax.ShapeDtypeStruct((s, h_ * e), F32), jax.ShapeDtypeStruct((h_, 1, s), F32)],
        operands=(qkv, qkv, qkv, c_row, ck_lanes),
        scratch=[pltpu.VMEM((1, t), F32), pltpu.VMEM((1, t), F32), pltpu.VMEM((e, t), F32),
                 pltpu.VMEM((t, t), F32), pltpu.VMEM((t, t), BF16)],
        semantics=("parallel", "arbitrary"), carry=carry)


def _fox_bwd(qkv, c_row, ck_lanes, lse, y, dy, n_heads, *, name, carry=None):
    s = qkv.shape[0]
    e = HEAD_DIM
    t = min(FOX_TILE, s)
    nt = s // t
    scale = e ** -0.5
    scale2 = scale * LOG2E
    lanes = 128
    q_tab, k_tab = _causal_pairs(nt, key_major=False)
    n_pairs = int(q_tab.shape[0])

    def body(q_tab, k_tab, q_ref, k_ref, v_ref, c_ref, ck_ref, lse_ref, y_ref, dy_ref,
             dq_ref, dk_ref, dv_ref, dcq_ref, dck_ref,
             dq_scr, dk_scr, dv_scr, dck_scr, s_scr, dp_scr, p_scr, ds_scr, do_scr, delta_scr, shift_scr):
        pair = pl.program_id(1)
        qi, ki = q_tab[pair], k_tab[pair]

        @pl.when(pair == 0)
        def _():
            dk_scr[...] = jnp.zeros_like(dk_scr)
            dv_scr[...] = jnp.zeros_like(dv_scr)
            dck_scr[...] = jnp.zeros_like(dck_scr)

        @pl.when(ki == 0)
        def _():
            do = dy_ref[...]
            delta_scr[...] = lax.dot_general(jnp.ones((8, e), F32), do * y_ref[...], (((1,), (1,)), ((), ())),
                                             precision=lax.Precision.HIGHEST, preferred_element_type=F32)
            shift_scr[...] = (lse_ref[...] - c_ref[...]) * LOG2E
            do_scr[...] = do.astype(BF16)
            dq_scr[...] = jnp.zeros_like(dq_scr)
            dcq_ref[...] = jnp.zeros_like(dcq_ref)

        def update(diagonal):
            s_scr[...] = _nt(k_ref[...], q_ref[...])
            dp_scr[...] = _nt(v_ref[...], do_scr[...])
            ck2 = ck_ref[...]
            k_rows = pl.ds(pl.multiple_of(ki * t, t), t)
            for c0 in range(0, t, lanes):
                cols = pl.ds(c0, lanes)
                x = s_scr[:, cols] * scale2 - ck2
                if diagonal:
                    key = lax.broadcasted_iota(jnp.int32, x.shape, 0)
                    qry = c0 + lax.broadcasted_iota(jnp.int32, x.shape, 1)
                    x = jnp.where(key <= qry, x, NEG_INF)
                p = jnp.exp2(x - shift_scr[:, cols])
                ds = p * (dp_scr[:, cols] - delta_scr[0:1, cols])
                dcq_ref[:, cols] += jnp.sum(ds, axis=0, keepdims=True)
                dck_scr[k_rows, :] += ds
                p_scr[:, cols] = p.astype(BF16)
                ds_scr[:, cols] = ds.astype(BF16)
            dv_scr[k_rows, :] += _nn(p_scr[...], do_scr[...])
            dk_scr[k_rows, :] += scale * _nn(ds_scr[...], q_ref[...])
            dq_scr[...] += scale * _tn(ds_scr[...], k_ref[...])

        @pl.when(ki < qi)
        def _():
            update(False)

        @pl.when(ki == qi)
        def _():
            update(True)
            dq_ref[...] = dq_scr[...].astype(BF16)

        @pl.when(pair == n_pairs - 1)
        def _():
            dk_ref[...] = dk_scr[...].astype(BF16)
            dv_ref[...] = dv_scr[...].astype(BF16)
            dck_ref[...] = -jnp.sum(dck_scr[...], axis=-1, keepdims=True)

    h_ = n_heads
    q_row = pl.BlockSpec((None, 1, t), lambda h, p, qt, kt: (h, 0, qt[p]))
    q_blk = pl.BlockSpec((t, e), lambda h, p, qt, kt: (qt[p], h))
    whole = pl.BlockSpec((s, e), lambda h, p, qt, kt: (0, h))
    return _call(
        body, name=name, grid=(h_, n_pairs), tables=(q_tab, k_tab),
        in_specs=[
            q_blk,
            pl.BlockSpec((t, e), lambda h, p, qt, kt: (kt[p], h_ + h)),
            pl.BlockSpec((t, e), lambda h, p, qt, kt: (kt[p], 2 * h_ + h)),
            q_row,
            pl.BlockSpec((None, t, lanes), lambda h, p, qt, kt: (h, kt[p], 0)),
            q_row, q_blk, q_blk,
        ],
        out_specs=[q_blk, whole, whole, q_row,
                   pl.BlockSpec((None, s, 1), lambda h, p, qt, kt: (h, 0, 0))],
        out_shape=[jax.ShapeDtypeStruct((s, h_ * e), BF16)] * 3
        + [jax.ShapeDtypeStruct((h_, 1, s), F32), jax.ShapeDtypeStruct((h_, s, 1), F32)],
        operands=(qkv, qkv, qkv, c_row, ck_lanes, lse, y, dy),
        scratch=[pltpu.VMEM((t, e), F32), pltpu.VMEM((s, e), F32), pltpu.VMEM((s, e), F32),
                 pltpu.VMEM((s, lanes), F32), pltpu.VMEM((t, t), F32), pltpu.VMEM((t, t), F32), pltpu.VMEM((t, t), BF16),
                 pltpu.VMEM((t, t), BF16), pltpu.VMEM((t, e), BF16), pltpu.VMEM((8, t), F32),
                 pltpu.VMEM((1, t), F32)],
        semantics=("parallel", "arbitrary"), carry=carry)


def _rel_bucket_table(dilation, span):
    dist = np.arange(span + 1, dtype=np.int64) * dilation
    max_exact = REL_BUCKETS // 2
    d = np.maximum(dist.astype(np.float32), np.float32(1.0))
    large = max_exact + (np.log(d / np.float32(max_exact)) / np.float32(math.log(REL_MAX_DISTANCE / max_exact))
                         * np.float32(REL_BUCKETS - max_exact)).astype(np.int32)
    large = np.minimum(large, REL_BUCKETS - 1)
    return np.where(dist < max_exact, dist, large)


def _bucket_bands(dilation, span):
    table = _rel_bucket_table(dilation, span)
    bands = []
    for n, b in enumerate(table):
        if bands and bands[-1][0] == int(b):
            bands[-1][2] = n
        else:
            assert not any(bb[0] == int(b) for bb in bands)
            bands.append([int(b), n, n])
    return [tuple(b) for b in bands]


def _steps_back():
    i = lax.broadcasted_iota(jnp.int32, (DIL_BLOCK, DIL_BLOCK), 0)
    j = lax.broadcasted_iota(jnp.int32, (DIL_BLOCK, DIL_BLOCK), 1)
    return i - j, DIL_BLOCK + i - j


def _bias_tiles(rb_ref, h, bands, span):
    n_cur, n_prev = _steps_back()
    t_cur = jnp.zeros((DIL_BLOCK, DIL_BLOCK), F32)
    t_prev = jnp.zeros((DIL_BLOCK, DIL_BLOCK), F32)
    for b, lo, hi in bands:
        val = rb_ref[b, h]
        t_cur = jnp.where((n_cur >= lo) & (n_cur <= hi), val, t_cur)
        t_prev = jnp.where((n_prev >= lo) & (n_prev <= hi), val, t_prev)
    t_cur = jnp.where(n_cur >= 0, t_cur, NEG_INF)
    t_prev = jnp.where(n_prev <= span, t_prev, NEG_INF)
    return t_cur, t_prev


def _dil_rows(start, dilation):
    return pl.ds(start, DIL_BLOCK, stride=dilation) if dilation > 1 else pl.ds(start, DIL_BLOCK)


def _dil_block_groups(s, dilation, run_group):
    group = dilation * DIL_BLOCK
    per_residue = s // group

    def block(r, n):
        if isinstance(n, int):
            return n * group + r, max(n - 1, 0) * group + r, n == 0
        return (pl.multiple_of(n * group, DIL_BLOCK) + r,
                pl.multiple_of(jnp.maximum(n - 1, 0) * group, DIL_BLOCK) + r, n == 0)

    if per_residue >= DIL_GROUP:
        assert per_residue % DIL_GROUP == 0
        for r in range(dilation):
            def trip(it, carry, r=r):
                run_group([block(r, it * DIL_GROUP + j) for j in range(DIL_GROUP)])
                return carry
            lax.fori_loop(0, per_residue // DIL_GROUP, trip, 0)
    else:
        residues = DIL_GROUP // per_residue
        assert residues * per_residue == DIL_GROUP and dilation % residues == 0
        for r0 in range(0, dilation, residues):
            run_group([block(r, n) for r in range(r0, r0 + residues) for n in range(per_residue)])


def _dil_fwd(qkv, rel_bias, n_heads, col0, *, name, carry=None):
    s = qkv.shape[0]
    e = HEAD_DIM
    scale = e ** -0.5
    n_pat = len(DIL_PATTERNS)
    for window, d in DIL_PATTERNS:
        assert window // d == DIL_BLOCK and s % (d * DIL_BLOCK) == 0
    bands = [_bucket_bands(d, w // d) for w, d in DIL_PATTERNS]

    def body(rb_ref, q_ref, k_ref, v_ref, y_ref, lse_ref, qf, kf, vf, *scr):
        o_scr, l_scr = scr[:n_pat], scr[n_pat:]
        h = pl.program_id(0)
        qf[...] = q_ref[...].astype(F32)
        kf[...] = k_ref[...].astype(F32)
        vf[...] = v_ref[...].astype(F32)
        for pi, (window, d) in enumerate(DIL_PATTERNS):
            t_cur, t_prev = _bias_tiles(rb_ref, h, bands[pi], window // d)

            def run_group(blocks, d=d, pi=pi, t_cur=t_cur, t_prev=t_prev):
                rows = [(_dil_rows(q0, d), _dil_rows(p0, d)) for q0, p0, _ in blocks]
                qb = [qf[cur, :].astype(BF16) for cur, _ in rows]
                s_c = [_nt(q, kf[cur, :].astype(BF16)) for q, (cur, _) in zip(qb, rows)]
                s_p = [_nt(q, kf[prev, :].astype(BF16)) for q, (_, prev) in zip(qb, rows)]
                s_c = [x * scale + t_cur for x in s_c]
                s_p = [x * scale + (t_prev + jnp.where(first, NEG_INF, 0.0)) for x, (_, _, first) in zip(s_p, blocks)]
                m = [jnp.maximum(jnp.max(a, axis=-1, keepdims=True), jnp.max(b, axis=-1, keepdims=True))
                     for a, b in zip(s_c, s_p)]
                p_c = [jnp.exp(a - mm) for a, mm in zip(s_c, m)]
                p_p = [jnp.exp(b - mm) for b, mm in zip(s_p, m)]
                l = [jnp.sum(a, axis=-1, keepdims=True) + jnp.sum(b, axis=-1, keepdims=True)
                     for a, b in zip(p_c, p_p)]
                o = [_nn(a.astype(BF16), vf[cur, :].astype(BF16)) + _nn(b.astype(BF16), vf[prev, :].astype(BF16))
                     for a, b, (cur, prev) in zip(p_c, p_p, rows)]
                for (cur, _), oo, ll, mm in zip(rows, o, l, m):
                    o_scr[pi][cur, :] = oo / ll
                    l_scr[pi][cur, :] = mm + jnp.log(ll)

            _dil_block_groups(s, d, run_group)
        lses = [l_scr[pi][...] for pi in range(n_pat)]
        m = functools.reduce(jnp.maximum, lses)
        ws = [jnp.exp(l - m) for l in lses]
        tot = functools.reduce(jnp.add, ws)
        y = functools.reduce(jnp.add, [w * o_scr[pi][...] for pi, w in enumerate(ws)])
        y_ref[...] = y / tot
        lse_ref[...] = m + jnp.log(tot)

    h_ = n_heads
    return _call(
        body, name=name, grid=(h_,),
        in_specs=[pl.BlockSpec(memory_space=pltpu.SMEM),
                  pl.BlockSpec((s, e), lambda h: (0, col0 + h)),
                  pl.BlockSpec((s, e), lambda h: (0, col0 + h_ + h)),
                  pl.BlockSpec((s, e), lambda h: (0, col0 + 2 * h_ + h))],
        out_specs=[pl.BlockSpec((s, e), lambda h: (0, h)), pl.BlockSpec((None, s, 1), lambda h: (h, 0, 0))],
        out_shape=[jax.ShapeDtypeStruct((s, h_ * e), F32), jax.ShapeDtypeStruct((h_, s, 1), F32)],
        operands=(rel_bias, qkv, qkv, qkv),
        scratch=[pltpu.VMEM((s, e), F32)] * 3 + [pltpu.VMEM((s, e), F32)] * n_pat
        + [pltpu.VMEM((s, 1), F32)] * n_pat,
        semantics=("parallel",), carry=carry)


def _dil_bwd(qkv, rel_bias, lse, y, dy, n_heads, col0, *, name, carry=None):
    s = qkv.shape[0]
    e = HEAD_DIM
    scale = e ** -0.5
    bands = [_bucket_bands(d, w // d) for w, d in DIL_PATTERNS]

    def body(rb_ref, q_ref, k_ref, v_ref, lse_ref, y_ref, dy_ref, dq_ref, dk_ref, dv_ref, drb_ref,
             qf, kf, vf, dqf, dkf, dvf, delta, dt_cur, dt_prev):
        h = pl.program_id(0)
        qf[...] = q_ref[...].astype(F32)
        kf[...] = k_ref[...].astype(F32)
        vf[...] = v_ref[...].astype(F32)
        dqf[...] = jnp.zeros_like(dqf)
        dkf[...] = jnp.zeros_like(dkf)
        dvf[...] = jnp.zeros_like(dvf)
        delta[...] = jnp.sum(dy_ref[...] * y_ref[...], axis=-1, keepdims=True)
        lane = lax.broadcasted_iota(jnp.int32, (1, 128), 1)
        drb = jnp.zeros((1, 128), F32)
        n_cur, n_prev = _steps_back()
        for pi, (window, d) in enumerate(DIL_PATTERNS):
            t_cur, t_prev = _bias_tiles(rb_ref, h, bands[pi], window // d)
            dt_cur[...] = jnp.zeros_like(dt_cur)
            dt_prev[...] = jnp.zeros_like(dt_prev)

            def run_group(blocks, d=d, t_cur=t_cur, t_prev=t_prev):
                rows = [(_dil_rows(q0, d), _dil_rows(p0, d)) for q0, p0, _ in blocks]
                qb = [qf[cur, :].astype(BF16) for cur, _ in rows]
                kc = [kf[cur, :].astype(BF16) for cur, _ in rows]
                kp = [kf[prev, :].astype(BF16) for _, prev in rows]
                vc = [vf[cur, :].astype(BF16) for cur, _ in rows]
                vp = [vf[prev, :].astype(BF16) for _, prev in rows]
                do = [dy_ref[cur, :].astype(BF16) for cur, _ in rows]
                lse_b = [lse_ref[cur, :] for cur, _ in rows]
                delta_b = [delta[cur, :] for cur, _ in rows]
                s_c = [_nt(q, k) for q, k in zip(qb, kc)]
                s_p = [_nt(q, k) for q, k in zip(qb, kp)]
                dp_c = [_nt(g, v) for g, v in zip(do, vc)]
                dp_p = [_nt(g, v) for g, v in zip(do, vp)]
                p_c = [jnp.exp(x * scale + t_cur - ls) for x, ls in zip(s_c, lse_b)]
                p_p = [jnp.exp(x * scale + (t_prev + jnp.where(first, NEG_INF, 0.0)) - ls)
                       for x, ls, (_, _, first) in zip(s_p, lse_b, blocks)]
                ds_c = [p * (g - dl) for p, g, dl in zip(p_c, dp_c, delta_b)]
                ds_p = [p * (g - dl) for p, g, dl in zip(p_p, dp_p, delta_b)]
                dt_cur[...] += functools.reduce(jnp.add, ds_c)
                dt_prev[...] += functools.reduce(jnp.add, ds_p)
                p_c, p_p = [p.astype(BF16) for p in p_c], [p.astype(BF16) for p in p_p]
                ds_c, ds_p = [x.astype(BF16) for x in ds_c], [x.astype(BF16) for x in ds_p]
                dq = [scale * (_nn(a, k1) + _nn(b, k2)) for a, b, k1, k2 in zip(ds_c, ds_p, kc, kp)]
                dk_c = [scale * _tn(a, q) for a, q in zip(ds_c, qb)]
                dv_c = [_tn(p, g) for p, g in zip(p_c, do)]
                dk_p = [scale * _tn(b, q) for b, q in zip(ds_p, qb)]
                dv_p = [_tn(p, g) for p, g in zip(p_p, do)]
                for i, (cur, prev) in enumerate(rows):
                    dqf[cur, :] += dq[i]
                    dkf[cur, :] += dk_c[i]
                    dvf[cur, :] += dv_c[i]
                    dkf[prev, :] += dk_p[i]
                    dvf[prev, :] += dv_p[i]

            _dil_block_groups(s, d, run_group)
            dtc, dtp = dt_cur[...], dt_prev[...]
            for b, lo, hi in bands[pi]:
                tot = (jnp.sum(jnp.where((n_cur >= lo) & (n_cur <= hi), dtc, 0.0))
                       + jnp.sum(jnp.where((n_prev >= lo) & (n_prev <= hi), dtp, 0.0)))
                drb = drb + jnp.where(lane == b, tot, 0.0)
        dq_ref[...] = dqf[...].astype(BF16)
        dk_ref[...] = dkf[...].astype(BF16)
        dv_ref[...] = dvf[...].astype(BF16)
        drb_ref[...] = drb

    h_ = n_heads
    col = pl.BlockSpec((s, e), lambda h: (0, h))
    return _call(
        body, name=name, grid=(h_,),
        in_specs=[pl.BlockSpec(memory_space=pltpu.SMEM),
                  pl.BlockSpec((s, e), lambda h: (0, col0 + h)),
                  pl.BlockSpec((s, e), lambda h: (0, col0 + h_ + h)),
                  pl.BlockSpec((s, e), lambda h: (0, col0 + 2 * h_ + h)),
                  pl.BlockSpec((None, s, 1), lambda h: (h, 0, 0)), col, col],
        out_specs=[col, col, col, pl.BlockSpec((None, 1, 128), lambda h: (h, 0, 0))],
        out_shape=[jax.ShapeDtypeStruct((s, h_ * e), BF16)] * 3 + [jax.ShapeDtypeStruct((h_, 1, 128), F32)],
        operands=(rel_bias, qkv, qkv, qkv, lse, y, dy),
        scratch=[pltpu.VMEM((s, e), F32)] * 6 + [pltpu.VMEM((s, 1), F32)]
        + [pltpu.VMEM((DIL_BLOCK, DIL_BLOCK), F32)] * 2,
        semantics=("parallel",), carry=carry)


def _place():
    x, y, c = lax.axis_index("x"), lax.axis_index("y"), lax.axis_index("c")
    chips = [(1 - x, y), (x, 1 - y), (1 - x, 1 - y)]
    return x, y, c, chips


def _block_index(px, py, pc):
    return 4 * px + 2 * py + pc


def _gather_task(blocks):
    n = len(blocks)

    def copies(ins, outs, sems):
        send_sems, recv_sems, local_sems = sems
        x, y, c, chips = _place()
        me, sibling = (x, y, c), (x, y, 1 - c)

        def copy(a, k, block, to, src=None):
            slot = outs[a].at[_block_index(*block)]
            return pltpu.make_async_remote_copy(
                src_ref=slot if src is None else src, dst_ref=slot,
                send_sem=send_sems.at[a, k], recv_sem=recv_sems.at[a, k],
                device_id=to, device_id_type=MESH)

        local = lambda a: pltpu.make_async_copy(ins[a], outs[a].at[_block_index(*me)], local_sems.at[a])
        return copy, local, me, sibling, c, chips

    def start(ins, outs, sems):
        copy, local, me, sibling, c, chips = copies(ins, outs, sems)
        for a in range(n):
            local(a).start()
            copy(a, 0, me, sibling, src=ins[a]).start()
            for j, chip in enumerate(chips):
                copy(a, 1 + j, me, (*chip, c), src=ins[a]).start()

    def forward(ins, outs, sems):
        copy, local, me, sibling, c, chips = copies(ins, outs, sems)
        for a in range(n):
            for j, chip in enumerate(chips):
                copy(a, 1 + j, (*chip, c), me).wait_recv()
                copy(a, 4 + j, (*chip, c), sibling).start()

    def finish(ins, outs, sems):
        copy, local, me, sibling, c, chips = copies(ins, outs, sems)
        for a in range(n):
            copy(a, 0, sibling, me).wait_recv()
            for j, chip in enumerate(chips):
                copy(a, 4 + j, (*chip, 1 - c), me).wait_recv()
        for a in range(n):
            for k in range(7):
                copy(a, k, me, sibling, src=ins[a]).wait_send()
            local(a).wait()

    return _Carried(
        blocks, [jax.ShapeDtypeStruct((N_DEV,) + b.shape, b.dtype) for b in blocks],
        [pltpu.SemaphoreType.DMA((n, 7)), pltpu.SemaphoreType.DMA((n, 7)), pltpu.SemaphoreType.DMA((n,))],
        start, forward, finish)


def _exchange_task(arrays, n_slots, route):
    n = len(arrays)

    def copies(ins, outs, sems):
        send_sems, recv_sems = sems
        out = []
        for a in range(n):
            for j in range(n_slots):
                src, to = route(j)
                out.append(pltpu.make_async_remote_copy(
                    src_ref=ins[a].at[src], dst_ref=outs[a].at[j],
                    send_sem=send_sems.at[a, j], recv_sem=recv_sems.at[a, j],
                    device_id=to, device_id_type=MESH))
        return out

    def start(ins, outs, sems):
        for cp in copies(ins, outs, sems):
            cp.start()

    def finish(ins, outs, sems):
        for cp in copies(ins, outs, sems):
            cp.wait()

    return _Carried(
        arrays, [jax.ShapeDtypeStruct((n_slots,) + g.shape[1:], g.dtype) for g in arrays],
        [pltpu.SemaphoreType.DMA((n, n_slots)), pltpu.SemaphoreType.DMA((n, n_slots))],
        start, lambda ins, outs, sems: None, finish)


def _sibling_task(grads):
    def route(q):
        x, y, c, _ = _place()
        return 2 * q + (1 - c), (x, y, 1 - c)
    return _exchange_task(grads, 4, route)


def _chips_task(parts):
    def route(j):
        x, y, c, chips = _place()
        px, py = chips[j]
        return 2 * px + py, (px, py, c)
    return _exchange_task(parts, 3, route)


def _chip_partials(grad, got, place, *, name):
    _, r, cdim = grad.shape
    tr = _row_tile(r, 512)

    def body(place_ref, g_ref, s_ref, o_ref):
        o_ref[...] = (g_ref[...] + s_ref[...]).astype(BF16)

    return pl.pallas_call(
        body, name=name,
        grid_spec=pltpu.PrefetchScalarGridSpec(
            num_scalar_prefetch=1, grid=(4, r // tr),
            in_specs=[pl.BlockSpec((None, tr, cdim), lambda q, i, pos: (2 * q + pos[2], i, 0)),
                      pl.BlockSpec((None, tr, cdim), lambda q, i, pos: (q, i, 0))],
            out_specs=pl.BlockSpec((None, tr, cdim), lambda q, i, pos: (q, i, 0))),
        out_shape=jax.ShapeDtypeStruct((4, r, cdim), BF16),
        compiler_params=_params("parallel", "parallel"),
    )(place, grad, got)


def _all_reduce_small(v, *, name):
    r, w = v.shape

    def body(v_ref, o_ref, buf, send_sems, recv_sems):
        x, y, c, _ = _place()
        me = _block_index(x, y, c)
        buf[me] = v_ref[...]
        copies = []
        for k in range(1, N_DEV):
            fx, fy, fc = (k >> 2) & 1, (k >> 1) & 1, k & 1
            peer = (x ^ fx, y ^ fy, c ^ fc)
            cp = pltpu.make_async_remote_copy(
                src_ref=v_ref, dst_ref=buf.at[me],
                send_sem=send_sems.at[k - 1], recv_sem=recv_sems.at[k - 1],
                device_id=peer, device_id_type=MESH)
            cp.start()
            copies.append(cp)
        for cp in copies:
            cp.wait()
        acc = buf[0]
        for p in range(1, N_DEV):
            acc = acc + buf[p]
        o_ref[...] = acc

    vmem = pl.BlockSpec(memory_space=pltpu.VMEM)
    return pl.pallas_call(
        body, name=name, in_specs=[vmem], out_specs=vmem,
        out_shape=jax.ShapeDtypeStruct((r, w), F32),
        scratch_shapes=[pltpu.VMEM((N_DEV, r, w), F32), pltpu.SemaphoreType.DMA((N_DEV - 1,)),
                        pltpu.SemaphoreType.DMA((N_DEV - 1,))],
    )(v)


def _adamw_math(w, g, m, v):
    m = ADAM_B1 * m + (1.0 - ADAM_B1) * g
    v = ADAM_B2 * v + (1.0 - ADAM_B2) * (g * g)
    m_hat = m / (1.0 - ADAM_B1 ** ADAM_STEP)
    v_hat = v / (1.0 - ADAM_B2 ** ADAM_STEP)
    delta = -ADAM_LR * (m_hat / (jnp.sqrt(v_hat) + ADAM_EPS) + ADAM_WD * w)
    return delta, m, v


def _adamw_sharded(w, m, v, part, got, place, *, name, carry=None):
    n_l, r, cdim = w.shape
    tr = _row_tile(r, 256)

    def body(place_ref, *refs):
        w_ref, m_ref, v_ref = refs[:3]
        p_refs, g_refs = refs[3:3 + n_l], refs[3 + n_l:3 + 2 * n_l]
        g_out, d_out, m_out, v_out = refs[3 + 2 * n_l:]
        for l in range(n_l):
            @pl.when(pl.program_id(0) == l)
            def _(l=l):
                g = p_refs[l][...].astype(F32)
                for j in range(3):
                    g = g + g_refs[l][j].astype(F32)
                delta, m_new, v_new = _adamw_math(w_ref[...], g, m_ref[...], v_ref[...])
                g_out[...] = g
                d_out[...] = delta
                m_out[...] = m_new
                v_out[...] = v_new

    shard = pl.BlockSpec((None, tr, cdim), lambda l, i, pos: (l, i, 0))
    p_specs = [pl.BlockSpec((None, tr, cdim), lambda l, i, pos: (2 * pos[0] + pos[1], i, 0))] * n_l
    g_specs = [pl.BlockSpec((3, tr, cdim), lambda l, i, pos: (0, i, 0))] * n_l
    return _call(
        body, name=name, grid=(n_l, r // tr), tables=(place,),
        in_specs=[shard] * 3 + p_specs + g_specs, out_specs=[shard] * 4,
        out_shape=[jax.ShapeDtypeStruct(w.shape, F32)] * 4,
        operands=(w, m, v, *part, *got), semantics=("arbitrary", "parallel"), carry=carry)


def _adamw_small(w, g, m, v, *, name):
    def body(w_ref, g_ref, m_ref, v_ref, d_out, m_out, v_out):
        delta, m_new, v_new = _adamw_math(w_ref[...], g_ref[...], m_ref[...], v_ref[...])
        d_out[...] = delta
        m_out[...] = m_new
        v_out[...] = v_new

    vmem = pl.BlockSpec(memory_space=pltpu.VMEM)
    return pl.pallas_call(
        body, name=name, in_specs=[vmem] * 4, out_specs=[vmem] * 3,
        out_shape=[jax.ShapeDtypeStruct(w.shape, F32)] * 3,
    )(w, g, m, v)


def _pad_lanes(a, width=128):
    return jnp.pad(a, ((0, 0), (0, width - a.shape[1])))


def _train_step(x, target, small, shards, place):
    w_in, w_out, w_mlp_in, w_mlp_out = shards
    depth, d_model, in_shard = w_in.shape
    n_heads = small["forget_b"].shape[1]
    dh = n_heads * HEAD_DIM
    d_ff = w_mlp_in.shape[2] * N_DEV
    row = lambda a: a.reshape(1, -1)
    block = {"in": w_in.astype(BF16), "out": w_out.astype(BF16), "mi": w_mlp_in.astype(BF16),
             "mo": w_mlp_out.astype(BF16)}
    whole = {}

    def unpack(kind, l, g):
        if kind == "in":
            full = g.transpose(1, 0, 2).reshape(d_model, N_DEV * in_shard)
            whole["qkv", l] = jnp.concatenate([full[:, :3 * dh], full[:, 3 * dh + n_heads:]], axis=1)
            whole["f", l] = _pad_lanes(full[:, 3 * dh:3 * dh + n_heads])
        elif kind == "out":
            whole["out", l] = g.reshape(2 * dh, d_model)
        elif kind == "mi":
            whole["mi", l] = g.transpose(1, 0, 2).reshape(d_model, d_ff)
        else:
            whole["mo", l] = g.reshape(d_ff, d_model)

    def gathering(kind, l):
        return _gather_task([block[kind][l]]), lambda got: unpack(kind, l, got[0])

    def carried(call, kind, l):
        if l >= depth:
            return call()
        task, done = gathering(kind, l)
        res, got = call(carry=task)
        done(got)
        return res

    (g,) = _run_carried(_gather_task([block["in"][0]]), name="gather_w_in_0")
    unpack("in", 0, g)

    saved = []
    for l in range(depth):
        h1 = _rmsnorm_fwd(x, row(small["norm1_g"][l]), name=f"norm1_fwd_{l}")
        qkv_proj = functools.partial(_matmul, h1, whole["qkv", l], out_dtype=BF16, name=f"qkv_proj_{l}")
        qkv = carried(qkv_proj, "out", 0) if l == 0 else qkv_proj()
        f = _matmul(h1, whole["f", l], name=f"gate_proj_{l}")
        fb = _pad_lanes(row(small["forget_b"][l]))
        c = _gates_fwd(f, fb, name=f"gates_fwd_{l}")
        c_heads = c[:, :n_heads].T
        cq = c_heads[:, None, :]
        ck = jnp.broadcast_to(c_heads[:, :, None] * LOG2E, c_heads.shape + (128,))
        ya, lse_a = carried(functools.partial(_fox_fwd, qkv, cq, ck, n_heads, name=f"fox_fwd_{l}"), "mi", l)
        yb, lse_b = carried(functools.partial(_dil_fwd, qkv, small["rel_bias"], n_heads, 3 * n_heads,
                                              name=f"dil_fwd_{l}"), "mo", l)
        mixed = _outnorm_fwd(ya, yb, row(small["outnorm_a_g"][l]), row(small["outnorm_b_g"][l]),
                             name=f"outnorm_fwd_{l}")
        x1 = _matmul(mixed, whole["out", l], addend=x, name=f"out_proj_{l}")
        h2 = _rmsnorm_fwd(x1, row(small["norm2_g"][l]), name=f"norm2_fwd_{l}")
        relu, act = carried(functools.partial(_matmul, h2, whole["mi", l], out_dtype=BF16, mode="relu_sq",
                                              name=f"mlp_in_{l}"), "in", l + 1)
        x2 = carried(functools.partial(_matmul, act, whole["mo", l], addend=x1, name=f"mlp_out_{l}"), "out", l + 1)
        saved.append((x, h1, qkv, f, fb, cq, ck, ya, lse_a, yb, lse_b, mixed, x1, h2, relu, act))
        x = x2
    dx, loss_tile, d_final = _loss_head(x, row(small["final_norm_g"]), target, name="loss_head")

    parts = {k: [None] * depth for k in ("in", "out", "mi", "mo")}
    got = {k: [None] * depth for k in ("in", "out", "mi", "mo")}
    small_grads = [None] * depth

    def to_sibling(call, kind, l, grad):
        res, (from_sibling,) = call(carry=_sibling_task([grad]))
        parts[kind][l] = _chip_partials(grad, from_sibling, place, name=f"chip_partials_{kind}_{l}")
        return res

    def to_chips(call, kind, l):
        if l >= depth:
            return call()
        res, (got[kind][l],) = call(carry=_chips_task([parts[kind][l]]))
        return res

    for l in reversed(range(depth)):
        x0, h1, qkv, f, fb, cq, ck, ya, lse_a, yb, lse_b, mixed, x1, h2, relu, act = saved[l]
        du = to_chips(functools.partial(_matmul, dx, whole["mo", l], tb=True, out_dtype=BF16, mode="mul2",
                                        aux=relu, name=f"mlp_out_dx_{l}"), "in", l + 1)
        dw_mo = _matmul(act, dx, ta=True, tk=1024, name=f"mlp_out_dw_{l}")
        dw_mo = dw_mo.reshape(N_DEV, d_ff // N_DEV, d_model)
        dh2 = to_sibling(functools.partial(_matmul, du, whole["mi", l], tb=True, name=f"mlp_in_dx_{l}"),
                         "mo", l, dw_mo)
        dw_mi = to_chips(functools.partial(_matmul, h2, du, tk=1024, ta=True, out_chunks=N_DEV,
                                           name=f"mlp_in_dw_{l}"), "mo", l)
        dx1, d_norm2 = _rmsnorm_bwd(x1, row(small["norm2_g"][l]), dh2, dx, name=f"norm2_bwd_{l}")
        dmixed = to_sibling(functools.partial(_matmul, dx1, whole["out", l], tb=True, name=f"out_proj_dx_{l}"),
                            "mi", l, dw_mi)
        dw_out = _matmul(mixed, dx1, ta=True, tk=1024, name=f"out_proj_dw_{l}")
        dw_out = dw_out.reshape(N_DEV, 2 * dh // N_DEV, d_model)
        dya, dyb, d_ga, d_gb = _outnorm_bwd(ya, yb, row(small["outnorm_a_g"][l]), row(small["outnorm_b_g"][l]),
                                            dmixed, name=f"outnorm_bwd_{l}")
        dqa, dka, dva, dcq, dck = to_chips(functools.partial(
            _fox_bwd, qkv, cq, ck, lse_a, ya, dya, n_heads, name=f"fox_bwd_{l}"), "mi", l)
        dqb, dkb, dvb, d_rb = to_sibling(functools.partial(
            _dil_bwd, qkv, small["rel_bias"], lse_b, yb, dyb, n_heads, 3 * n_heads, name=f"dil_bwd_{l}"),
            "out", l, dw_out)
        dc = _pad_lanes((dcq[:, 0, :] + dck[:, :, 0]).T)
        df, dfb = _gates_bwd(f, fb, dc, name=f"gates_bwd_{l}")
        dqkv = jnp.concatenate([dqa, dka, dva, dqb, dkb, dvb], axis=1)
        dw_qkv = _matmul(h1, dqkv, ta=True, tk=1024, name=f"qkv_proj_dw_{l}")
        dw_f = _matmul(h1, df, ta=True, tk=1024, name=f"gate_proj_dw_{l}")
        dw_in = jnp.concatenate([dw_qkv[:, :3 * dh], dw_f[:, :n_heads], dw_qkv[:, 3 * dh:]], axis=1)
        dw_in = dw_in.reshape(d_model, N_DEV, in_shard).transpose(1, 0, 2)
        dh1 = _matmul(df, whole["f", l], tb=True, name=f"gate_proj_dx_{l}")
        dh1 = to_chips(functools.partial(_matmul, dqkv, whole["qkv", l], tb=True, addend=dh1,
                                         name=f"qkv_proj_dx_{l}"), "out", l)
        dx, d_norm1 = to_sibling(functools.partial(_rmsnorm_bwd, x0, row(small["norm1_g"][l]), dh1, dx1,
                                                   name=f"norm1_bwd_{l}"), "in", l, dw_in)
        small_grads[l] = dict(norm1_g=d_norm1[0], forget_b=dfb[0, :n_heads], rel_bias=d_rb[:, 0, :REL_BUCKETS].T,
                              outnorm_a_g=d_ga[0], outnorm_b_g=d_gb[0], norm2_g=d_norm2[0])
    return loss_tile[0, 0], dx, d_final[0], small_grads, parts, got


def _pack_small(parts, rows):
    flat = jnp.concatenate([p.reshape(-1).astype(F32) for p in parts])
    return jnp.pad(flat, (0, rows * 128 - flat.shape[0])).reshape(rows, 128)


def _unpack_small(packed, shapes):
    flat = packed.reshape(-1)
    out, pos = [], 0
    for shp in shapes:
        size = int(np.prod(shp))
        out.append(flat[pos:pos + size].reshape(shp))
        pos += size
    return out


def kernel(x, norm1_g, w_in, forget_b, rel_bias, outnorm_a_g, outnorm_b_g, w_out, norm2_g, w_mlp_in, w_mlp_out, final_norm_g, loss_target, m_norm1_g, m_w_in, m_forget_b, m_rel_bias, m_outnorm_a_g, m_outnorm_b_g, m_w_out, m_norm2_g, m_w_mlp_in, m_w_mlp_out, m_final_norm_g, v_norm1_g, v_w_in, v_forget_b, v_rel_bias, v_outnorm_a_g, v_outnorm_b_g, v_w_out, v_norm2_g, v_w_mlp_in, v_w_mlp_out, v_final_norm_g):
    depth, d_model, in_shard = w_in.shape
    n_heads = forget_b.shape[1]
    assert in_shard * N_DEV == 6 * n_heads * HEAD_DIM + n_heads and x.shape[0] == 1
    place = jnp.stack([lax.axis_index("x"), lax.axis_index("y"), lax.axis_index("c")]).astype(jnp.int32)

    small_names = ["norm1_g", "forget_b", "rel_bias", "outnorm_a_g", "outnorm_b_g", "norm2_g", "final_norm_g"]
    small_w = dict(norm1_g=norm1_g, forget_b=forget_b, rel_bias=rel_bias, outnorm_a_g=outnorm_a_g,
                   outnorm_b_g=outnorm_b_g, norm2_g=norm2_g, final_norm_g=final_norm_g)
    loss_part, dx, d_final, grads, parts, got = _train_step(
        x[0], loss_target[0], small_w, (w_in, w_out, w_mlp_in, w_mlp_out), place)

    small_m = dict(norm1_g=m_norm1_g, forget_b=m_forget_b, rel_bias=m_rel_bias, outnorm_a_g=m_outnorm_a_g,
                   outnorm_b_g=m_outnorm_b_g, norm2_g=m_norm2_g, final_norm_g=m_final_norm_g)
    small_v = dict(norm1_g=v_norm1_g, forget_b=v_forget_b, rel_bias=v_rel_bias, outnorm_a_g=v_outnorm_a_g,
                   outnorm_b_g=v_outnorm_b_g, norm2_g=v_norm2_g, final_norm_g=v_final_norm_g)
    small_g = {k: jnp.stack([g[k] for g in grads]) for k in small_names if k not in ("rel_bias", "final_norm_g")}
    small_g["rel_bias"] = functools.reduce(jnp.add, [g["rel_bias"] for g in grads])
    small_g["final_norm_g"] = d_final
    shapes = [small_w[k].shape for k in small_names]
    total = sum(int(np.prod(s)) for s in shapes) + 1
    rows = -(-total // (8 * 128)) * 8
    packed_g = _pack_small([small_g[k] for k in small_names] + [loss_part], rows)
    packed_g = _all_reduce_small(packed_g, name="reduce_small")
    zero = jnp.zeros((1,), F32)
    packed_w = _pack_small([small_w[k] for k in small_names] + [zero], rows)
    packed_m = _pack_small([small_m[k] for k in small_names] + [zero], rows)
    packed_v = _pack_small([small_v[k] for k in small_names] + [zero + 1.0], rows)
    packed_d, packed_nm, packed_nv = _adamw_small(packed_w, packed_g, packed_m, packed_v, name="adamw_small")
    g_small = dict(zip(small_names, _unpack_small(packed_g, shapes)))
    d_small = dict(zip(small_names, _unpack_small(packed_d, shapes)))
    nm_small = dict(zip(small_names, _unpack_small(packed_nm, shapes)))
    nv_small = dict(zip(small_names, _unpack_small(packed_nv, shapes)))
    loss = packed_g.reshape(-1)[total - 1]

    big_out = {}
    for kind, nm, w, m, v in [("mi", "w_mlp_in", w_mlp_in, m_w_mlp_in, v_w_mlp_in),
                              ("mo", "w_mlp_out", w_mlp_out, m_w_mlp_out, v_w_mlp_out),
                              ("out", "w_out", w_out, m_w_out, v_w_out), ("in", "w_in", w_in, m_w_in, v_w_in)]:
        adamw = functools.partial(_adamw_sharded, w, m, v, parts[kind], got[kind], place, name=f"adamw_{nm}")
        if got["in"][0] is None:
            big_out[nm], (got["in"][0],) = adamw(carry=_chips_task([parts["in"][0]]))
        else:
            big_out[nm] = adamw()

    order = ["norm1_g", "w_in", "forget_b", "rel_bias", "outnorm_a_g", "outnorm_b_g", "w_out", "norm2_g",
             "w_mlp_in", "w_mlp_out", "final_norm_g"]
    pick = lambda k, idx, small: big_out[k][idx] if k in big_out else small[k]
    outs = [loss, dx[None]]
    outs += [pick(k, 0, g_small) for k in order]
    outs += [pick(k, 1, d_small) for k in order]
    outs += [pick(k, 2, nm_small) for k in order]
    outs += [pick(k, 3, nv_small) for k in order]
    return tuple(outs)
```

```python
import functools
import math

import numpy as np
import jax
import jax.numpy as jnp
from jax import lax
from jax.experimental import pallas as pl
from jax.experimental.pallas import tpu as pltpu

F32 = jnp.float32
BF16 = jnp.bfloat16

HEAD_DIM = 128
DIL_PATTERNS = ((128, 1), (512, 4), (2048, 16))
DIL_BLOCK = 128
REL_BUCKETS = 32
REL_MAX_DISTANCE = 2048
NORM_EPS = 1e-6
NEG_INF = -1e30
ADAM_LR = 0.001
ADAM_B1 = 0.9
ADAM_B2 = 0.999
ADAM_EPS = 1e-08
ADAM_WD = 0.01
ADAM_STEP = 10

N_DEV = 8
V7X_VMEM_LIMIT_BYTES = 56 * 1024 * 1024
MESH = pl.DeviceIdType.MESH


def _params(*semantics):
    return pltpu.CompilerParams(dimension_semantics=semantics, vmem_limit_bytes=V7X_VMEM_LIMIT_BYTES)


HBM_SPEC = pl.BlockSpec(memory_space=pltpu.HBM)


class _Carried:
    def __init__(self, operands, out_shape, scratch, start, forward, finish):
        self.operands, self.out_shape, self.scratch = list(operands), list(out_shape), list(scratch)
        self.start, self.forward, self.finish = start, forward, finish


FORWARD_AT = 0.8


def _call(body, *, name, grid, in_specs, out_specs, out_shape, operands, scratch=(), semantics, carry=None,
          tables=()):
    in_specs, out_specs, out_shape, scratch = list(in_specs), list(out_specs), list(out_shape), list(scratch)
    n_tab = len(tables)

    def run(fn, in_specs, out_specs, out_shape, scratch, operands, semantics):
        return pl.pallas_call(
            fn, name=name,
            grid_spec=pltpu.PrefetchScalarGridSpec(
                num_scalar_prefetch=n_tab, grid=grid, in_specs=in_specs, out_specs=out_specs,
                scratch_shapes=scratch),
            out_shape=out_shape, compiler_params=_params(*semantics))(*tables, *operands)

    if carry is None:
        return run(body, in_specs, out_specs, out_shape, scratch, operands, semantics)
    n_in, n_out = len(in_specs), len(out_specs)
    ci, co, cs = len(carry.operands), len(carry.out_shape), len(carry.scratch)
    steps = int(np.prod(grid))
    forward_step = min(int(steps * FORWARD_AT), steps - 1)

    def carrying(*refs):
        tabs, refs = refs[:n_tab], refs[n_tab:]
        main_in, c_in = refs[:n_in], refs[n_in:n_in + ci]
        main_out = refs[n_in + ci:n_in + ci + n_out]
        c_out = refs[n_in + ci + n_out:n_in + ci + n_out + co]
        rest = refs[n_in + ci + n_out + co:]
        main_scr, c_scr = rest[:len(rest) - cs], rest[len(rest) - cs:]
        step = 0
        for axis, extent in enumerate(grid):
            step = step * extent + pl.program_id(axis)

        @pl.when(step == 0)
        def _():
            carry.start(c_in, c_out, c_scr)

        body(*tabs, *main_in, *main_out, *main_scr)

        @pl.when(step == forward_step)
        def _():
            carry.forward(c_in, c_out, c_scr)

        @pl.when(step == steps - 1)
        def _():
            carry.finish(c_in, c_out, c_scr)

    res = run(carrying, in_specs + [HBM_SPEC] * ci, out_specs + [HBM_SPEC] * co, out_shape + carry.out_shape,
              scratch + carry.scratch, (*operands, *carry.operands), ["arbitrary"] * len(grid))
    return res[:n_out], res[n_out:]


def _run_carried(carry, *, name):
    ci, co = len(carry.operands), len(carry.out_shape)

    def body(*refs):
        parts = (refs[:ci], refs[ci:ci + co], refs[ci + co:])
        carry.start(*parts)
        carry.forward(*parts)
        carry.finish(*parts)

    return pl.pallas_call(
        body, name=name, in_specs=[HBM_SPEC] * ci, out_specs=[HBM_SPEC] * co,
        out_shape=carry.out_shape, scratch_shapes=carry.scratch)(*carry.operands)


def _fit(dim, want, unit=128):
    if dim <= want:
        return dim
    t = want - want % unit
    while dim % t:
        t -= unit
    return t


def _matmul(a, b, *, name, ta=False, tb=False, out_dtype=F32, tm=1024, tn=1024, tk=2048,
            addend=None, mode=None, aux=None, out_chunks=None, carry=None):
    m_dim, k_dim = (a.shape[1], a.shape[0]) if ta else a.shape
    n_dim = b.shape[0] if tb else b.shape[1]
    assert (b.shape[1] if tb else b.shape[0]) == k_dim
    tm, tn, tk = _fit(m_dim, tm), _fit(n_dim // (out_chunks or 1), tn), _fit(k_dim, tk)
    assert m_dim % tm == 0 and n_dim % tn == 0 and k_dim % tk == 0, (name, a.shape, b.shape)
    nk = k_dim // tk
    a_spec = (pl.BlockSpec((tk, tm), lambda i, j, k: (k, i)) if ta
              else pl.BlockSpec((tm, tk), lambda i, j, k: (i, k)))
    b_spec = (pl.BlockSpec((tn, tk), lambda i, j, k: (j, k)) if tb
              else pl.BlockSpec((tk, tn), lambda i, j, k: (k, j)))
    mn_spec = pl.BlockSpec((tm, tn), lambda i, j, k: (i, j))
    if out_chunks is None:
        o_spec, o_shape = mn_spec, (m_dim, n_dim)
    else:
        per = n_dim // out_chunks // tn
        assert per * tn * out_chunks == n_dim
        o_spec = pl.BlockSpec((None, tm, tn), lambda i, j, k: (j // per, i, j % per))
        o_shape = (out_chunks, m_dim, n_dim // out_chunks)
    dims = (((0 if ta else 1,), (1 if tb else 0,)), ((), ()))
    n_out = 2 if mode == "relu_sq" else 1
    in_specs, operands = [a_spec, b_spec], [a, b]
    if addend is not None:
        in_specs.append(mn_spec)
        operands.append(addend)
    if mode == "mul2":
        in_specs.append(mn_spec)
        operands.append(aux)

    def body(*refs):
        a_ref, b_ref = refs[0], refs[1]
        pos = 2
        add_ref = aux_ref = None
        if addend is not None:
            add_ref, pos = refs[pos], pos + 1
        if mode == "mul2":
            aux_ref, pos = refs[pos], pos + 1
        outs = refs[pos:pos + n_out]

        def finish(acc):
            if add_ref is not None:
                acc = acc + add_ref[...].astype(F32)
            if mode == "relu_sq":
                r = jnp.maximum(acc, 0.0)
                outs[0][...] = r.astype(outs[0].dtype)
                outs[1][...] = (r * r).astype(outs[1].dtype)
            elif mode == "mul2":
                outs[0][...] = (acc * (2.0 * aux_ref[...].astype(F32))).astype(outs[0].dtype)
            else:
                outs[0][...] = acc.astype(outs[0].dtype)

        part = lax.dot_general(a_ref[...].astype(BF16), b_ref[...].astype(BF16), dims,
                               preferred_element_type=F32)
        if nk == 1:
            finish(part)
        else:
            acc_ref = refs[-1]
            k = pl.program_id(2)

            @pl.when(k == 0)
            def _():
                acc_ref[...] = part

            @pl.when(k > 0)
            def _():
                acc_ref[...] += part

            @pl.when(k == nk - 1)
            def _():
                finish(acc_ref[...])

    out_shape = [jax.ShapeDtypeStruct(o_shape, out_dtype)] * n_out
    res = _call(
        body, name=name, grid=(m_dim // tm, n_dim // tn, nk),
        in_specs=in_specs, out_specs=[o_spec] * n_out, out_shape=out_shape, operands=operands,
        scratch=[pltpu.VMEM((tm, tn), F32)] if nk > 1 else [],
        semantics=("parallel", "parallel", "arbitrary"), carry=carry)
    if carry is not None:
        res, carried = res
        return (res if n_out > 1 else res[0]), carried
    return res if n_out > 1 else res[0]


def _norm_fwd_math(x, g):
    r = lax.rsqrt(jnp.mean(x * x, axis=-1, keepdims=True) + NORM_EPS)
    return (x * r) * g


def _norm_bwd_math(x, g, dy):
    r = lax.rsqrt(jnp.mean(x * x, axis=-1, keepdims=True) + NORM_EPS)
    xh = x * r
    dxh = dy * g
    dx = r * (dxh - xh * jnp.mean(dxh * xh, axis=-1, keepdims=True))
    return dx, jnp.sum(dy * xh, axis=0, keepdims=True)


def _row_tile(rows, want=256):
    t = min(rows, want)
    assert rows % t == 0
    return t


def _rmsnorm_fwd(x, g, *, name):
    s, d = x.shape
    tr = _row_tile(s)

    def body(x_ref, g_ref, h_ref):
        h_ref[...] = _norm_fwd_math(x_ref[...], g_ref[...]).astype(BF16)

    return pl.pallas_call(
        body, name=name, grid=(s // tr,),
        in_specs=[pl.BlockSpec((tr, d), lambda i: (i, 0)), pl.BlockSpec((1, d), lambda i: (0, 0))],
        out_specs=pl.BlockSpec((tr, d), lambda i: (i, 0)),
        out_shape=jax.ShapeDtypeStruct((s, d), BF16),
        compiler_params=_params("parallel"),
    )(x, g)


def _rmsnorm_bwd(x, g, dh, dres, *, name, carry=None):
    s, d = x.shape
    tr = _row_tile(s)

    def body(x_ref, g_ref, dh_ref, dres_ref, dx_ref, dx16_ref, dg_ref):
        dx, dg = _norm_bwd_math(x_ref[...], g_ref[...], dh_ref[...])
        dx = dres_ref[...] + dx
        dx_ref[...] = dx
        dx16_ref[...] = dx.astype(BF16)

        @pl.when(pl.program_id(0) == 0)
        def _():
            dg_ref[...] = dg

        @pl.when(pl.program_id(0) > 0)
        def _():
            dg_ref[...] += dg

    row = pl.BlockSpec((tr, d), lambda i: (i, 0))
    vec = pl.BlockSpec((1, d), lambda i: (0, 0))
    return _call(
        body, name=name, grid=(s // tr,),
        in_specs=[row, vec, row, row], out_specs=[row, row, vec],
        out_shape=[jax.ShapeDtypeStruct((s, d), F32), jax.ShapeDtypeStruct((s, d), BF16),
                   jax.ShapeDtypeStruct((1, d), F32)],
        operands=(x, g, dh, dres), semantics=("arbitrary",), carry=carry)


def _outnorm_fwd(ya, yb, ga, gb, *, name):
    s, da = ya.shape
    db = yb.shape[1]
    tr = _row_tile(s)

    def body(ya_ref, yb_ref, ga_ref, gb_ref, o_ref):
        o_ref[:, :da] = _norm_fwd_math(ya_ref[...], ga_ref[...]).astype(BF16)
        o_ref[:, da:] = _norm_fwd_math(yb_ref[...], gb_ref[...]).astype(BF16)

    return pl.pallas_call(
        body, name=name, grid=(s // tr,),
        in_specs=[pl.BlockSpec((tr, da), lambda i: (i, 0)), pl.BlockSpec((tr, db), lambda i: (i, 0)),
                  pl.BlockSpec((1, da), lambda i: (0, 0)), pl.BlockSpec((1, db), lambda i: (0, 0))],
        out_specs=pl.BlockSpec((tr, da + db), lambda i: (i, 0)),
        out_shape=jax.ShapeDtypeStruct((s, da + db), BF16),
        compiler_params=_params("parallel"),
    )(ya, yb, ga, gb)


def _outnorm_bwd(ya, yb, ga, gb, dmixed, *, name):
    s, da = ya.shape
    db = yb.shape[1]
    tr = _row_tile(s)

    def body(ya_ref, yb_ref, ga_ref, gb_ref, dm_ref, dya_ref, dyb_ref, dga_ref, dgb_ref):
        dxa, dga = _norm_bwd_math(ya_ref[...], ga_ref[...], dm_ref[:, :da])
        dxb, dgb = _norm_bwd_math(yb_ref[...], gb_ref[...], dm_ref[:, da:])
        dya_ref[...] = dxa
        dyb_ref[...] = dxb

        @pl.when(pl.program_id(0) == 0)
        def _():
            dga_ref[...] = dga
            dgb_ref[...] = dgb

        @pl.when(pl.program_id(0) > 0)
        def _():
            dga_ref[...] += dga
            dgb_ref[...] += dgb

    ra = pl.BlockSpec((tr, da), lambda i: (i, 0))
    rb = pl.BlockSpec((tr, db), lambda i: (i, 0))
    va = pl.BlockSpec((1, da), lambda i: (0, 0))
    vb = pl.BlockSpec((1, db), lambda i: (0, 0))
    return pl.pallas_call(
        body, name=name, grid=(s // tr,),
        in_specs=[ra, rb, va, vb, pl.BlockSpec((tr, da + db), lambda i: (i, 0))],
        out_specs=[ra, rb, va, vb],
        out_shape=[jax.ShapeDtypeStruct((s, da), F32), jax.ShapeDtypeStruct((s, db), F32),
                   jax.ShapeDtypeStruct((1, da), F32), jax.ShapeDtypeStruct((1, db), F32)],
        compiler_params=_params("arbitrary"),
    )(ya, yb, ga, gb, dmixed)


def _loss_head(x, g, target, *, name):
    s, d = x.shape
    tr = _row_tile(s)

    def body(x_ref, g_ref, t_ref, dx_ref, dx16_ref, loss_ref, dg_ref):
        xv, gv = x_ref[...], g_ref[...]
        err = _norm_fwd_math(xv, gv) - t_ref[...]
        part = 0.5 * jnp.sum(jnp.mean(err * err, axis=-1, keepdims=True), axis=0, keepdims=True)
        dx, dg = _norm_bwd_math(xv, gv, err * (1.0 / d))
        dx_ref[...] = dx
        dx16_ref[...] = dx.astype(BF16)
        part = jnp.broadcast_to(part, (8, 128))

        @pl.when(pl.program_id(0) == 0)
        def _():
            dg_ref[...] = dg
            loss_ref[...] = part

        @pl.when(pl.program_id(0) > 0)
        def _():
            dg_ref[...] += dg
            loss_ref[...] += part

    row = pl.BlockSpec((tr, d), lambda i: (i, 0))
    vec = pl.BlockSpec((1, d), lambda i: (0, 0))
    return pl.pallas_call(
        body, name=name, grid=(s // tr,),
        in_specs=[row, vec, row],
        out_specs=[row, row, pl.BlockSpec((8, 128), lambda i: (0, 0)), vec],
        out_shape=[jax.ShapeDtypeStruct((s, d), F32), jax.ShapeDtypeStruct((s, d), BF16),
                   jax.ShapeDtypeStruct((8, 128), F32), jax.ShapeDtypeStruct((1, d), F32)],
        compiler_params=_params("arbitrary"),
    )(x, g, target)


def _split3(x):
    hi = x.astype(BF16)
    rem = x - hi.astype(F32)
    mid = rem.astype(BF16)
    lo = (rem - mid.astype(F32)).astype(BF16)
    return hi, mid, lo


def _tri_sum(tri, x):
    hi, mid, lo = _split3(x)
    dot = functools.partial(jnp.dot, preferred_element_type=F32)
    return dot(tri, hi) + dot(tri, mid) + dot(tri, lo)


def _gates_fwd(f, fb, *, name):
    s, w = f.shape
    tb = 128
    nb = s // tb

    def body(f_ref, fb_ref, c_ref, carry):
        @pl.when(pl.program_id(0) == 0)
        def _():
            carry[...] = jnp.zeros_like(carry)

        logf = jax.nn.log_sigmoid(f_ref[...] + fb_ref[...])
        row = lax.broadcasted_iota(jnp.int32, (tb, tb), 0)
        col = lax.broadcasted_iota(jnp.int32, (tb, tb), 1)
        tri = (row >= col).astype(BF16)
        c = _tri_sum(tri, logf) + carry[...]
        c_ref[...] = c
        carry[...] = c[tb - 1:tb, :]

    return pl.pallas_call(
        body, name=name, grid=(nb,),
        in_specs=[pl.BlockSpec((tb, w), lambda i: (i, 0)), pl.BlockSpec((1, w), lambda i: (0, 0))],
        out_specs=pl.BlockSpec((tb, w), lambda i: (i, 0)),
        out_shape=jax.ShapeDtypeStruct((s, w), F32),
        scratch_shapes=[pltpu.VMEM((1, w), F32)],
        compiler_params=_params("arbitrary"),
    )(f, fb)


def _gates_bwd(f, fb, dc, *, name):
    s, w = f.shape
    tb = 128
    nb = s // tb

    def body(f_ref, fb_ref, dc_ref, df_ref, dfb_ref, carry):
        @pl.when(pl.program_id(0) == 0)
        def _():
            carry[...] = jnp.zeros_like(carry)
            dfb_ref[...] = jnp.zeros_like(dfb_ref)

        row = lax.broadcasted_iota(jnp.int32, (tb, tb), 0)
        col = lax.broadcasted_iota(jnp.int32, (tb, tb), 1)
        tri = (row <= col).astype(BF16)
        dlogf = _tri_sum(tri, dc_ref[...]) + carry[...]
        carry[...] = dlogf[0:1, :]
        df = dlogf * jax.nn.sigmoid(-(f_ref[...] + fb_ref[...]))
        df_ref[...] = df
        dfb_ref[...] += jnp.sum(df, axis=0, keepdims=True)

    rev = pl.BlockSpec((tb, w), lambda i: (nb - 1 - i, 0))
    vec = pl.BlockSpec((1, w), lambda i: (0, 0))
    return pl.pallas_call(
        body, name=name, grid=(nb,),
        in_specs=[rev, vec, rev], out_specs=[rev, vec],
        out_shape=[jax.ShapeDtypeStruct((s, w), F32), jax.ShapeDtypeStruct((1, w), F32)],
        scratch_shapes=[pltpu.VMEM((1, w), F32)],
        compiler_params=_params("arbitrary"),
    )(f, fb, dc)


def _nt(a, b):
    return lax.dot_general(a, b, (((1,), (1,)), ((), ())), preferred_element_type=F32)


def _tn(a, b):
    return lax.dot_general(a, b, (((0,), (0,)), ((), ())), preferred_element_type=F32)


def _nn(a, b):
    return jnp.dot(a, b, preferred_element_type=F32)


DIL_GROUP = 4
FOX_TILE = 1024
LOG2E = 1.4426950408889634


def _causal_pairs(nt, key_major):
    if key_major:
        pairs = [(q, k) for k in range(nt) for q in range(k, nt)]
    else:
        pairs = [(q, k) for q in range(nt) for k in range(q + 1)]
    return (jnp.asarray([p[0] for p in pairs], jnp.int32), jnp.asarray([p[1] for p in pairs], jnp.int32))


def _fox_fwd(qkv, c_row, ck_lanes, n_heads, *, name, carry=None):
    s = qkv.shape[0]
    e = HEAD_DIM
    t = min(FOX_TILE, s)
    nt = s // t
    scale2 = e ** -0.5 * LOG2E
    lanes = 128
    q_tab, k_tab = _causal_pairs(nt, key_major=False)

    def body(q_tab, k_tab, q_ref, k_ref, v_ref, cq_ref, ck_ref, o_ref, lse_ref,
             m_scr, l_scr, acc_scr, s_scr, p_scr):
        pair = pl.program_id(1)
        qi, ki = q_tab[pair], k_tab[pair]

        @pl.when(ki == 0)
        def _():
            m_scr[...] = jnp.full_like(m_scr, NEG_INF)
            l_scr[...] = jnp.zeros_like(l_scr)
            acc_scr[...] = jnp.zeros_like(acc_scr)

        def update(diagonal):
            s_scr[...] = _nt(k_ref[...], q_ref[...])
            ck2 = ck_ref[...]
            for c0 in range(0, t, lanes):
                cols = pl.ds(c0, lanes)

                def logits2():
                    x = s_scr[:, cols] * scale2 - ck2
                    if diagonal:
                        key = lax.broadcasted_iota(jnp.int32, x.shape, 0)
                        qry = c0 + lax.broadcasted_iota(jnp.int32, x.shape, 1)
                        x = jnp.where(key <= qry, x, NEG_INF)
                    return x

                m_old = m_scr[:, cols]
                m_new = jnp.maximum(m_old, jnp.max(logits2(), axis=0, keepdims=True))
                p = jnp.exp2(logits2() - m_new)
                alpha = jnp.exp2(m_old - m_new)
                l_scr[:, cols] = alpha * l_scr[:, cols] + jnp.sum(p, axis=0, keepdims=True)
                m_scr[:, cols] = m_new
                acc_scr[:, cols] = alpha * acc_scr[:, cols]
                p_scr[:, cols] = p.astype(BF16)
            acc_scr[...] += _tn(v_ref[...], p_scr[...])

        @pl.when(ki < qi)
        def _():
            update(False)

        @pl.when(ki == qi)
        def _():
            update(True)
            o_ref[...] = (acc_scr[...] / l_scr[...]).T
            lse_ref[...] = (m_scr[...] + jnp.log2(l_scr[...])) * (1.0 / LOG2E) + cq_ref[...]

    h_ = n_heads
    return _call(
        body, name=name, grid=(h_, int(q_tab.shape[0])), tables=(q_tab, k_tab),
        in_specs=[
            pl.BlockSpec((t, e), lambda h, p, qt, kt: (qt[p], h)),
            pl.BlockSpec((t, e), lambda h, p, qt, kt: (kt[p], h_ + h)),
            pl.BlockSpec((t, e), lambda h, p, qt, kt: (kt[p], 2 * h_ + h)),
            pl.BlockSpec((None, 1, t), lambda h, p, qt, kt: (h, 0, qt[p])),
            pl.BlockSpec((None, t, lanes), lambda h, p, qt, kt: (h, kt[p], 0)),
        ],
        out_specs=[pl.BlockSpec((t, e), lambda h, p, qt, kt: (qt[p], h)),
                   pl.BlockSpec((None, 1, t), lambda h, p, qt, kt: (h, 0, qt[p]))],
        out_shape=[jax.ShapeDtypeStruct((s, h_ * e), F32), jax.ShapeDtypeStruct((h_, 1, s), F32)],
        operands=(qkv, qkv, qkv, c_row, ck_lanes),
        scratch=[pltpu.VMEM((1, t), F32), pltpu.VMEM((1, t), F32), pltpu.VMEM((e, t), F32),
                 pltpu.VMEM((t, t), F32), pltpu.VMEM((t, t), BF16)],
        semantics=("parallel", "arbitrary"), carry=carry)


def _fox_bwd(qkv, c_row, ck_lanes, lse, y, dy, n_heads, *, name, carry=None):
    s = qkv.shape[0]
    e = HEAD_DIM
    t = min(FOX_TILE, s)
    nt = s // t
    scale = e ** -0.5
    scale2 = scale * LOG2E
    lanes = 128
    q_tab, k_tab = _causal_pairs(nt, key_major=False)
    n_pairs = int(q_tab.shape[0])

    def body(q_tab, k_tab, q_ref, k_ref, v_ref, c_ref, ck_ref, lse_ref, y_ref, dy_ref,
             dq_ref, dk_ref, dv_ref, dcq_ref, dck_ref,
             dq_scr, dk_scr, dv_scr, dck_scr, s_scr, dp_scr, p_scr, ds_scr, do_scr, delta_scr, shift_scr):
        pair = pl.program_id(1)
        qi, ki = q_tab[pair], k_tab[pair]

        @pl.when(pair == 0)
        def _():
            dk_scr[...] = jnp.zeros_like(dk_scr)
            dv_scr[...] = jnp.zeros_like(dv_scr)
            dck_scr[...] = jnp.zeros_like(dck_scr)

        @pl.when(ki == 0)
        def _():
            do = dy_ref[...]
            delta_scr[...] = lax.dot_general(jnp.ones((8, e), F32), do * y_ref[...], (((1,), (1,)), ((), ())),
                                             precision=lax.Precision.HIGHEST, preferred_element_type=F32)
            shift_scr[...] = (lse_ref[...] - c_ref[...]) * LOG2E
            do_scr[...] = do.astype(BF16)
            dq_scr[...] = jnp.zeros_like(dq_scr)
            dcq_ref[...] = jnp.zeros_like(dcq_ref)

        def update(diagonal):
            s_scr[...] = _nt(k_ref[...], q_ref[...])
            dp_scr[...] = _nt(v_ref[...], do_scr[...])
            ck2 = ck_ref[...]
            k_rows = pl.ds(pl.multiple_of(ki * t, t), t)
            for c0 in range(0, t, lanes):
                cols = pl.ds(c0, lanes)
                x = s_scr[:, cols] * scale2 - ck2
                if diagonal:
                    key = lax.broadcasted_iota(jnp.int32, x.shape, 0)
                    qry = c0 + lax.broadcasted_iota(jnp.int32, x.shape, 1)
                    x = jnp.where(key <= qry, x, NEG_INF)
                p = jnp.exp2(x - shift_scr[:, cols])
                ds = p * (dp_scr[:, cols] - delta_scr[0:1, cols])
                dcq_ref[:, cols] += jnp.sum(ds, axis=0, keepdims=True)
                dck_scr[k_rows, :] += ds
                p_scr[:, cols] = p.astype(BF16)
                ds_scr[:, cols] = ds.astype(BF16)
            dv_scr[k_rows, :] += _nn(p_scr[...], do_scr[...])
            dk_scr[k_rows, :] += scale * _nn(ds_scr[...], q_ref[...])
            dq_scr[...] += scale * _tn(ds_scr[...], k_ref[...])

        @pl.when(ki < qi)
        def _():
            update(False)

        @pl.when(ki == qi)
        def _():
            update(True)
            dq_ref[...] = dq_scr[...].astype(BF16)

        @pl.when(pair == n_pairs - 1)
        def _():
            dk_ref[...] = dk_scr[...].astype(BF16)
            dv_ref[...] = dv_scr[...].astype(BF16)
            dck_ref[...] = -jnp.sum(dck_scr[...], axis=-1, keepdims=True)

    h_ = n_heads
    q_row = pl.BlockSpec((None, 1, t), lambda h, p, qt, kt: (h, 0, qt[p]))
    q_blk = pl.BlockSpec((t, e), lambda h, p, qt, kt: (qt[p], h))
    whole = pl.BlockSpec((s, e), lambda h, p, qt, kt: (0, h))
    return _call(
        body, name=name, grid=(h_, n_pairs), tables=(q_tab, k_tab),
        in_specs=[
            q_blk,
            pl.BlockSpec((t, e), lambda h, p, qt, kt: (kt[p], h_ + h)),
            pl.BlockSpec((t, e), lambda h, p, qt, kt: (kt[p], 2 * h_ + h)),
            q_row,
            pl.BlockSpec((None, t, lanes), lambda h, p, qt, kt: (h, kt[p], 0)),
            q_row, q_blk, q_blk,
        ],
        out_specs=[q_blk, whole, whole, q_row,
                   pl.BlockSpec((None, s, 1), lambda h, p, qt, kt: (h, 0, 0))],
        out_shape=[jax.ShapeDtypeStruct((s, h_ * e), BF16)] * 3
        + [jax.ShapeDtypeStruct((h_, 1, s), F32), jax.ShapeDtypeStruct((h_, s, 1), F32)],
        operands=(qkv, qkv, qkv, c_row, ck_lanes, lse, y, dy),
        scratch=[pltpu.VMEM((t, e), F32), pltpu.VMEM((s, e), F32), pltpu.VMEM((s, e), F32),
                 pltpu.VMEM((s, lanes), F32), pltpu.VMEM((t, t), F32), pltpu.VMEM((t, t), F32), pltpu.VMEM((t, t), BF16),
                 pltpu.VMEM((t, t), BF16), pltpu.VMEM((t, e), BF16), pltpu.VMEM((8, t), F32),
                 pltpu.VMEM((1, t), F32)],
        semantics=("parallel", "arbitrary"), carry=carry)


def _rel_bucket_table(dilation, span):
    dist = np.arange(span + 1, dtype=np.int64) * dilation
    max_exact = REL_BUCKETS // 2
    d = np.maximum(dist.astype(np.float32), np.float32(1.0))
    large = max_exact + (np.log(d / np.float32(max_exact)) / np.float32(math.log(REL_MAX_DISTANCE / max_exact))
                         * np.float32(REL_BUCKETS - max_exact)).astype(np.int32)
    large = np.minimum(large, REL_BUCKETS - 1)
    return np.where(dist < max_exact, dist, large)


def _bucket_bands(dilation, span):
    table = _rel_bucket_table(dilation, span)
    bands = []
    for n, b in enumerate(table):
        if bands and bands[-1][0] == int(b):
            bands[-1][2] = n
        else:
            assert not any(bb[0] == int(b) for bb in bands)
            bands.append([int(b), n, n])
    return [tuple(b) for b in bands]


def _steps_back():
    i = lax.broadcasted_iota(jnp.int32, (DIL_BLOCK, DIL_BLOCK), 0)
    j = lax.broadcasted_iota(jnp.int32, (DIL_BLOCK, DIL_BLOCK), 1)
    return i - j, DIL_BLOCK + i - j


def _bias_tiles(rb_ref, h, bands, span):
    n_cur, n_prev = _steps_back()
    t_cur = jnp.zeros((DIL_BLOCK, DIL_BLOCK), F32)
    t_prev = jnp.zeros((DIL_BLOCK, DIL_BLOCK), F32)
    for b, lo, hi in bands:
        val = rb_ref[b, h]
        t_cur = jnp.where((n_cur >= lo) & (n_cur <= hi), val, t_cur)
        t_prev = jnp.where((n_prev >= lo) & (n_prev <= hi), val, t_prev)
    t_cur = jnp.where(n_cur >= 0, t_cur, NEG_INF)
    t_prev = jnp.where(n_prev <= span, t_prev, NEG_INF)
    return t_cur, t_prev


def _dil_rows(start, dilation):
    return pl.ds(start, DIL_BLOCK, stride=dilation) if dilation > 1 else pl.ds(start, DIL_BLOCK)


def _dil_block_groups(s, dilation, run_group):
    group = dilation * DIL_BLOCK
    per_residue = s // group

    def block(r, n):
        if isinstance(n, int):
            return n * group + r, max(n - 1, 0) * group + r, n == 0
        return (pl.multiple_of(n * group, DIL_BLOCK) + r,
                pl.multiple_of(jnp.maximum(n - 1, 0) * group, DIL_BLOCK) + r, n == 0)

    if per_residue >= DIL_GROUP:
        assert per_residue % DIL_GROUP == 0
        for r in range(dilation):
            def trip(it, carry, r=r):
                run_group([block(r, it * DIL_GROUP + j) for j in range(DIL_GROUP)])
                return carry
            lax.fori_loop(0, per_residue // DIL_GROUP, trip, 0)
    else:
        residues = DIL_GROUP // per_residue
        assert residues * per_residue == DIL_GROUP and dilation % residues == 0
        for r0 in range(0, dilation, residues):
            run_group([block(r, n) for r in range(r0, r0 + residues) for n in range(per_residue)])


def _dil_fwd(qkv, rel_bias, n_heads, col0, *, name, carry=None):
    s = qkv.shape[0]
    e = HEAD_DIM
    scale = e ** -0.5
    n_pat = len(DIL_PATTERNS)
    for window, d in DIL_PATTERNS:
        assert window // d == DIL_BLOCK and s % (d * DIL_BLOCK) == 0
    bands = [_bucket_bands(d, w // d) for w, d in DIL_PATTERNS]

    def body(rb_ref, q_ref, k_ref, v_ref, y_ref, lse_ref, qf, kf, vf, *scr):
        o_scr, l_scr = scr[:n_pat], scr[n_pat:]
        h = pl.program_id(0)
        qf[...] = q_ref[...].astype(F32)
        kf[...] = k_ref[...].astype(F32)
        vf[...] = v_ref[...].astype(F32)
        for pi, (window, d) in enumerate(DIL_PATTERNS):
            t_cur, t_prev = _bias_tiles(rb_ref, h, bands[pi], window // d)

            def run_group(blocks, d=d, pi=pi, t_cur=t_cur, t_prev=t_prev):
                rows = [(_dil_rows(q0, d), _dil_rows(p0, d)) for q0, p0, _ in blocks]
                qb = [qf[cur, :].astype(BF16) for cur, _ in rows]
                s_c = [_nt(q, kf[cur, :].astype(BF16)) for q, (cur, _) in zip(qb, rows)]
                s_p = [_nt(q, kf[prev, :].astype(BF16)) for q, (_, prev) in zip(qb, rows)]
                s_c = [x * scale + t_cur for x in s_c]
                s_p = [x * scale + (t_prev + jnp.where(first, NEG_INF, 0.0)) for x, (_, _, first) in zip(s_p, blocks)]
                m = [jnp.maximum(jnp.max(a, axis=-1, keepdims=True), jnp.max(b, axis=-1, keepdims=True))
                     for a, b in zip(s_c, s_p)]
                p_c = [jnp.exp(a - mm) for a, mm in zip(s_c, m)]
                p_p = [jnp.exp(b - mm) for b, mm in zip(s_p, m)]
                l = [jnp.sum(a, axis=-1, keepdims=True) + jnp.sum(b, axis=-1, keepdims=True)
                     for a, b in zip(p_c, p_p)]
                o = [_nn(a.astype(BF16), vf[cur, :].astype(BF16)) + _nn(b.astype(BF16), vf[prev, :].astype(BF16))
                     for a, b, (cur, prev) in zip(p_c, p_p, rows)]
                for (cur, _), oo, ll, mm in zip(rows, o, l, m):
                    o_scr[pi][cur, :] = oo / ll
                    l_scr[pi][cur, :] = mm + jnp.log(ll)

            _dil_block_groups(s, d, run_group)
        lses = [l_scr[pi][...] for pi in range(n_pat)]
        m = functools.reduce(jnp.maximum, lses)
        ws = [jnp.exp(l - m) for l in lses]
        tot = functools.reduce(jnp.add, ws)
        y = functools.reduce(jnp.add, [w * o_scr[pi][...] for pi, w in enumerate(ws)])
        y_ref[...] = y / tot
        lse_ref[...] = m + jnp.log(tot)

    h_ = n_heads
    return _call(
        body, name=name, grid=(h_,),
        in_specs=[pl.BlockSpec(memory_space=pltpu.SMEM),
                  pl.BlockSpec((s, e), lambda h: (0, col0 + h)),
                  pl.BlockSpec((s, e), lambda h: (0, col0 + h_ + h)),
                  pl.BlockSpec((s, e), lambda h: (0, col0 + 2 * h_ + h))],
        out_specs=[pl.BlockSpec((s, e), lambda h: (0, h)), pl.BlockSpec((None, s, 1), lambda h: (h, 0, 0))],
        out_shape=[jax.ShapeDtypeStruct((s, h_ * e), F32), jax.ShapeDtypeStruct((h_, s, 1), F32)],
        operands=(rel_bias, qkv, qkv, qkv),
        scratch=[pltpu.VMEM((s, e), F32)] * 3 + [pltpu.VMEM((s, e), F32)] * n_pat
        + [pltpu.VMEM((s, 1), F32)] * n_pat,
        semantics=("parallel",), carry=carry)


def _dil_bwd(qkv, rel_bias, lse, y, dy, n_heads, col0, *, name, carry=None):
    s = qkv.shape[0]
    e = HEAD_DIM
    scale = e ** -0.5
    bands = [_bucket_bands(d, w // d) for w, d in DIL_PATTERNS]

    def body(rb_ref, q_ref, k_ref, v_ref, lse_ref, y_ref, dy_ref, dq_ref, dk_ref, dv_ref, drb_ref,
             qf, kf, vf, dqf, dkf, dvf, delta, dt_cur, dt_prev):
        h = pl.program_id(0)
        qf[...] = q_ref[...].astype(F32)
        kf[...] = k_ref[...].astype(F32)
        vf[...] = v_ref[...].astype(F32)
        dqf[...] = jnp.zeros_like(dqf)
        dkf[...] = jnp.zeros_like(dkf)
        dvf[...] = jnp.zeros_like(dvf)
        delta[...] = jnp.sum(dy_ref[...] * y_ref[...], axis=-1, keepdims=True)
        lane = lax.broadcasted_iota(jnp.int32, (1, 128), 1)
        drb = jnp.zeros((1, 128), F32)
        n_cur, n_prev = _steps_back()
        for pi, (window, d) in enumerate(DIL_PATTERNS):
            t_cur, t_prev = _bias_tiles(rb_ref, h, bands[pi], window // d)
            dt_cur[...] = jnp.zeros_like(dt_cur)
            dt_prev[...] = jnp.zeros_like(dt_prev)

            def run_group(blocks, d=d, t_cur=t_cur, t_prev=t_prev):
                rows = [(_dil_rows(q0, d), _dil_rows(p0, d)) for q0, p0, _ in blocks]
                qb = [qf[cur, :].astype(BF16) for cur, _ in rows]
                kc = [kf[cur, :].astype(BF16) for cur, _ in rows]
                kp = [kf[prev, :].astype(BF16) for _, prev in rows]
                vc = [vf[cur, :].astype(BF16) for cur, _ in rows]
                vp = [vf[prev, :].astype(BF16) for _, prev in rows]
                do = [dy_ref[cur, :].astype(BF16) for cur, _ in rows]
                lse_b = [lse_ref[cur, :] for cur, _ in rows]
                delta_b = [delta[cur, :] for cur, _ in rows]
                s_c = [_nt(q, k) for q, k in zip(qb, kc)]
                s_p = [_nt(q, k) for q, k in zip(qb, kp)]
                dp_c = [_nt(g, v) for g, v in zip(do, vc)]
                dp_p = [_nt(g, v) for g, v in zip(do, vp)]
                p_c = [jnp.exp(x * scale + t_cur - ls) for x, ls in zip(s_c, lse_b)]
                p_p = [jnp.exp(x * scale + (t_prev + jnp.where(first, NEG_INF, 0.0)) - ls)
                       for x, ls, (_, _, first) in zip(s_p, lse_b, blocks)]
                ds_c = [p * (g - dl) for p, g, dl in zip(p_c, dp_c, delta_b)]
                ds_p = [p * (g - dl) for p, g, dl in zip(p_p, dp_p, delta_b)]
                dt_cur[...] += functools.reduce(jnp.add, ds_c)
                dt_prev[...] += functools.reduce(jnp.add, ds_p)
                p_c, p_p = [p.astype(BF16) for p in p_c], [p.astype(BF16) for p in p_p]
                ds_c, ds_p = [x.astype(BF16) for x in ds_c], [x.astype(BF16) for x in ds_p]
                dq = [scale * (_nn(a, k1) + _nn(b, k2)) for a, b, k1, k2 in zip(ds_c, ds_p, kc, kp)]
                dk_c = [scale * _tn(a, q) for a, q in zip(ds_c, qb)]
                dv_c = [_tn(p, g) for p, g in zip(p_c, do)]
                dk_p = [scale * _tn(b, q) for b, q in zip(ds_p, qb)]
                dv_p = [_tn(p, g) for p, g in zip(p_p, do)]
                for i, (cur, prev) in enumerate(rows):
                    dqf[cur, :] += dq[i]
                    dkf[cur, :] += dk_c[i]
                    dvf[cur, :] += dv_c[i]
                    dkf[prev, :] += dk_p[i]
                    dvf[prev, :] += dv_p[i]

            _dil_block_groups(s, d, run_group)
            dtc, dtp = dt_cur[...], dt_prev[...]
            for b, lo, hi in bands[pi]:
                tot = (jnp.sum(jnp.where((n_cur >= lo) & (n_cur <= hi), dtc, 0.0))
                       + jnp.sum(jnp.where((n_prev >= lo) & (n_prev <= hi), dtp, 0.0)))
                drb = drb + jnp.where(lane == b, tot, 0.0)
        dq_ref[...] = dqf[...].astype(BF16)
        dk_ref[...] = dkf[...].astype(BF16)
        dv_ref[...] = dvf[...].astype(BF16)
        drb_ref[...] = drb

    h_ = n_heads
    col = pl.BlockSpec((s, e), lambda h: (0, h))
    return _call(
        body, name=name, grid=(h_,),
        in_specs=[pl.BlockSpec(memory_space=pltpu.SMEM),
                  pl.BlockSpec((s, e), lambda h: (0, col0 + h)),
                  pl.BlockSpec((s, e), lambda h: (0, col0 + h_ + h)),
                  pl.BlockSpec((s, e), lambda h: (0, col0 + 2 * h_ + h)),
                  pl.BlockSpec((None, s, 1), lambda h: (h, 0, 0)), col, col],
        out_specs=[col, col, col, pl.BlockSpec((None, 1, 128), lambda h: (h, 0, 0))],
        out_shape=[jax.ShapeDtypeStruct((s, h_ * e), BF16)] * 3 + [jax.ShapeDtypeStruct((h_, 1, 128), F32)],
        operands=(rel_bias, qkv, qkv, qkv, lse, y, dy),
        scratch=[pltpu.VMEM((s, e), F32)] * 6 + [pltpu.VMEM((s, 1), F32)]
        + [pltpu.VMEM((DIL_BLOCK, DIL_BLOCK), F32)] * 2,
        semantics=("parallel",), carry=carry)


def _place():
    x, y, c = lax.axis_index("x"), lax.axis_index("y"), lax.axis_index("c")
    chips = [(1 - x, y), (x, 1 - y), (1 - x, 1 - y)]
    return x, y, c, chips


def _block_index(px, py, pc):
    return 4 * px + 2 * py + pc


def _gather_task(blocks):
    n = len(blocks)

    def copies(ins, outs, sems):
        send_sems, recv_sems, local_sems = sems
        x, y, c, chips = _place()
        me, sibling = (x, y, c), (x, y, 1 - c)

        def copy(a, k, block, to, src=None):
            slot = outs[a].at[_block_index(*block)]
            return pltpu.make_async_remote_copy(
                src_ref=slot if src is None else src, dst_ref=slot,
                send_sem=send_sems.at[a, k], recv_sem=recv_sems.at[a, k],
                device_id=to, device_id_type=MESH)

        local = lambda a: pltpu.make_async_copy(ins[a], outs[a].at[_block_index(*me)], local_sems.at[a])
        return copy, local, me, sibling, c, chips

    def start(ins, outs, sems):
        copy, local, me, sibling, c, chips = copies(ins, outs, sems)
        for a in range(n):
            local(a).start()
            copy(a, 0, me, sibling, src=ins[a]).start()
            for j, chip in enumerate(chips):
                copy(a, 1 + j, me, (*chip, c), src=ins[a]).start()

    def forward(ins, outs, sems):
        copy, local, me, sibling, c, chips = copies(ins, outs, sems)
        for a in range(n):
            for j, chip in enumerate(chips):
                copy(a, 1 + j, (*chip, c), me).wait_recv()
                copy(a, 4 + j, (*chip, c), sibling).start()

    def finish(ins, outs, sems):
        copy, local, me, sibling, c, chips = copies(ins, outs, sems)
        for a in range(n):
            copy(a, 0, sibling, me).wait_recv()
            for j, chip in enumerate(chips):
                copy(a, 4 + j, (*chip, 1 - c), me).wait_recv()
        for a in range(n):
            for k in range(7):
                copy(a, k, me, sibling, src=ins[a]).wait_send()
            local(a).wait()

    return _Carried(
        blocks, [jax.ShapeDtypeStruct((N_DEV,) + b.shape, b.dtype) for b in blocks],
        [pltpu.SemaphoreType.DMA((n, 7)), pltpu.SemaphoreType.DMA((n, 7)), pltpu.SemaphoreType.DMA((n,))],
        start, forward, finish)


def _exchange_task(arrays, n_slots, route):
    n = len(arrays)

    def copies(ins, outs, sems):
        send_sems, recv_sems = sems
        out = []
        for a in range(n):
            for j in range(n_slots):
                src, to = route(j)
                out.append(pltpu.make_async_remote_copy(
                    src_ref=ins[a].at[src], dst_ref=outs[a].at[j],
                    send_sem=send_sems.at[a, j], recv_sem=recv_sems.at[a, j],
                    device_id=to, device_id_type=MESH))
        return out

    def start(ins, outs, sems):
        for cp in copies(ins, outs, sems):
            cp.start()

    def finish(ins, outs, sems):
        for cp in copies(ins, outs, sems):
            cp.wait()

    return _Carried(
        arrays, [jax.ShapeDtypeStruct((n_slots,) + g.shape[1:], g.dtype) for g in arrays],
        [pltpu.SemaphoreType.DMA((n, n_slots)), pltpu.SemaphoreType.DMA((n, n_slots))],
        start, lambda ins, outs, sems: None, finish)


def _sibling_task(grads):
    def route(q):
        x, y, c, _ = _place()
        return 2 * q + (1 - c), (x, y, 1 - c)
    return _exchange_task(grads, 4, route)


def _chips_task(parts):
    def route(j):
        x, y, c, chips = _place()
        px, py = chips[j]
        return 2 * px + py, (px, py, c)
    return _exchange_task(parts, 3, route)


def _chip_partials(grad, got, place, *, name):
    _, r, cdim = grad.shape
    tr = _row_tile(r, 512)

    def body(place_ref, g_ref, s_ref, o_ref):
        o_ref[...] = (g_ref[...] + s_ref[...]).astype(BF16)

    return pl.pallas_call(
        body, name=name,
        grid_spec=pltpu.PrefetchScalarGridSpec(
            num_scalar_prefetch=1, grid=(4, r // tr),
            in_specs=[pl.BlockSpec((None, tr, cdim), lambda q, i, pos: (2 * q + pos[2], i, 0)),
                      pl.BlockSpec((None, tr, cdim), lambda q, i, pos: (q, i, 0))],
            out_specs=pl.BlockSpec((None, tr, cdim), lambda q, i, pos: (q, i, 0))),
        out_shape=jax.ShapeDtypeStruct((4, r, cdim), BF16),
        compiler_params=_params("parallel", "parallel"),
    )(place, grad, got)


def _all_reduce_small(v, *, name):
    r, w = v.shape

    def body(v_ref, o_ref, buf, send_sems, recv_sems):
        x, y, c, _ = _place()
        me = _block_index(x, y, c)
        buf[me] = v_ref[...]
        copies = []
        for k in range(1, N_DEV):
            fx, fy, fc = (k >> 2) & 1, (k >> 1) & 1, k & 1
            peer = (x ^ fx, y ^ fy, c ^ fc)
            cp = pltpu.make_async_remote_copy(
                src_ref=v_ref, dst_ref=buf.at[me],
                send_sem=send_sems.at[k - 1], recv_sem=recv_sems.at[k - 1],
                device_id=peer, device_id_type=MESH)
            cp.start()
            copies.append(cp)
        for cp in copies:
            cp.wait()
        acc = buf[0]
        for p in range(1, N_DEV):
            acc = acc + buf[p]
        o_ref[...] = acc

    vmem = pl.BlockSpec(memory_space=pltpu.VMEM)
    return pl.pallas_call(
        body, name=name, in_specs=[vmem], out_specs=vmem,
        out_shape=jax.ShapeDtypeStruct((r, w), F32),
        scratch_shapes=[pltpu.VMEM((N_DEV, r, w), F32), pltpu.SemaphoreType.DMA((N_DEV - 1,)),
                        pltpu.SemaphoreType.DMA((N_DEV - 1,))],
    )(v)


def _adamw_math(w, g, m, v):
    m = ADAM_B1 * m + (1.0 - ADAM_B1) * g
    v = ADAM_B2 * v + (1.0 - ADAM_B2) * (g * g)
    m_hat = m / (1.0 - ADAM_B1 ** ADAM_STEP)
    v_hat = v / (1.0 - ADAM_B2 ** ADAM_STEP)
    delta = -ADAM_LR * (m_hat / (jnp.sqrt(v_hat) + ADAM_EPS) + ADAM_WD * w)
    return delta, m, v


def _adamw_sharded(w, m, v, part, got, place, *, name, carry=None):
    n_l, r, cdim = w.shape
    tr = _row_tile(r, 256)

    def body(place_ref, *refs):
        w_ref, m_ref, v_ref = refs[:3]
        p_refs, g_refs = refs[3:3 + n_l], refs[3 + n_l:3 + 2 * n_l]
        g_out, d_out, m_out, v_out = refs[3 + 2 * n_l:]
        for l in range(n_l):
            @pl.when(pl.program_id(0) == l)
            def _(l=l):
                g = p_refs[l][...].astype(F32)
                for j in range(3):
                    g = g + g_refs[l][j].astype(F32)
                delta, m_new, v_new = _adamw_math(w_ref[...], g, m_ref[...], v_ref[...])
                g_out[...] = g
                d_out[...] = delta
                m_out[...] = m_new
                v_out[...] = v_new

    shard = pl.BlockSpec((None, tr, cdim), lambda l, i, pos: (l, i, 0))
    p_specs = [pl.BlockSpec((None, tr, cdim), lambda l, i, pos: (2 * pos[0] + pos[1], i, 0))] * n_l
    g_specs = [pl.BlockSpec((3, tr, cdim), lambda l, i, pos: (0, i, 0))] * n_l
    return _call(
        body, name=name, grid=(n_l, r // tr), tables=(place,),
        in_specs=[shard] * 3 + p_specs + g_specs, out_specs=[shard] * 4,
        out_shape=[jax.ShapeDtypeStruct(w.shape, F32)] * 4,
        operands=(w, m, v, *part, *got), semantics=("arbitrary", "parallel"), carry=carry)


def _adamw_small(w, g, m, v, *, name):
    def body(w_ref, g_ref, m_ref, v_ref, d_out, m_out, v_out):
        delta, m_new, v_new = _adamw_math(w_ref[...], g_ref[...], m_ref[...], v_ref[...])
        d_out[...] = delta
        m_out[...] = m_new
        v_out[...] = v_new

    vmem = pl.BlockSpec(memory_space=pltpu.VMEM)
    return pl.pallas_call(
        body, name=name, in_specs=[vmem] * 4, out_specs=[vmem] * 3,
        out_shape=[jax.ShapeDtypeStruct(w.shape, F32)] * 3,
    )(w, g, m, v)


def _pad_lanes(a, width=128):
    return jnp.pad(a, ((0, 0), (0, width - a.shape[1])))


def _train_step(x, target, small, shards, place):
    w_in, w_out, w_mlp_in, w_mlp_out = shards
    depth, d_model, in_shard = w_in.shape
    n_heads = small["forget_b"].shape[1]
    dh = n_heads * HEAD_DIM
    d_ff = w_mlp_in.shape[2] * N_DEV
    row = lambda a: a.reshape(1, -1)
    block = {"in": w_in.astype(BF16), "out": w_out.astype(BF16), "mi": w_mlp_in.astype(BF16),
             "mo": w_mlp_out.astype(BF16)}
    whole = {}

    gate_lo, gate_hi = 3 * dh, 3 * dh + n_heads

    def shard_columns(g, lo, hi):
        pieces = []
        while lo < hi:
            p, a = divmod(lo, in_shard)
            b = min(in_shard, a + hi - lo)
            pieces.append(g[p][:, a:b])
            lo += b - a
        return pieces

    def in_columns(d_qkv, d_gate, lo, hi):
        pieces = []
        if lo < gate_lo:
            pieces.append(d_qkv[:, lo:min(hi, gate_lo)])
        if lo < gate_hi and hi > gate_lo:
            pieces.append(d_gate[:, max(lo, gate_lo) - gate_lo:min(hi, gate_hi) - gate_lo])
        if hi > gate_hi:
            pieces.append(d_qkv[:, max(lo, gate_hi) - n_heads:hi - n_heads])
        return pieces

    def unpack(kind, l, g):
        if kind == "in":
            whole["qkv", l] = jnp.concatenate(
                shard_columns(g, 0, gate_lo) + shard_columns(g, gate_hi, N_DEV * in_shard), axis=1)
            whole["f", l] = _pad_lanes(jnp.concatenate(shard_columns(g, gate_lo, gate_hi), axis=1))
        elif kind == "out":
            whole["out", l] = g.reshape(2 * dh, d_model)
        elif kind == "mi":
            whole["mi", l] = g.transpose(1, 0, 2).reshape(d_model, d_ff)
        else:
            whole["mo", l] = g.reshape(d_ff, d_model)

    def gathering(kind, l):
        return _gather_task([block[kind][l]]), lambda got: unpack(kind, l, got[0])

    def carried(call, kind, l):
        if l >= depth:
            return call()
        task, done = gathering(kind, l)
        res, got = call(carry=task)
        done(got)
        return res

    (g,) = _run_carried(_gather_task([block["in"][0]]), name="gather_w_in_0")
    unpack("in", 0, g)

    saved = []
    for l in range(depth):
        h1 = _rmsnorm_fwd(x, row(small["norm1_g"][l]), name=f"norm1_fwd_{l}")
        qkv_proj = functools.partial(_matmul, h1, whole["qkv", l], out_dtype=BF16, name=f"qkv_proj_{l}")
        qkv = carried(qkv_proj, "out", 0) if l == 0 else qkv_proj()
        f = _matmul(h1, whole["f", l], name=f"gate_proj_{l}")
        fb = _pad_lanes(row(small["forget_b"][l]))
        c = _gates_fwd(f, fb, name=f"gates_fwd_{l}")
        c_heads = c[:, :n_heads].T
        cq = c_heads[:, None, :]
        ck = jnp.broadcast_to(c_heads[:, :, None] * LOG2E, c_heads.shape + (128,))
        ya, lse_a = carried(functools.partial(_fox_fwd, qkv, cq, ck, n_heads, name=f"fox_fwd_{l}"), "mi", l)
        yb, lse_b = carried(functools.partial(_dil_fwd, qkv, small["rel_bias"], n_heads, 3 * n_heads,
                                              name=f"dil_fwd_{l}"), "mo", l)
        mixed = _outnorm_fwd(ya, yb, row(small["outnorm_a_g"][l]), row(small["outnorm_b_g"][l]),
                             name=f"outnorm_fwd_{l}")
        x1 = _matmul(mixed, whole["out", l], addend=x, name=f"out_proj_{l}")
        h2 = _rmsnorm_fwd(x1, row(small["norm2_g"][l]), name=f"norm2_fwd_{l}")
        relu, act = carried(functools.partial(_matmul, h2, whole["mi", l], out_dtype=BF16, mode="relu_sq",
                                              name=f"mlp_in_{l}"), "in", l + 1)
        x2 = carried(functools.partial(_matmul, act, whole["mo", l], addend=x1, name=f"mlp_out_{l}"), "out", l + 1)
        saved.append((x, h1, qkv, f, fb, cq, ck, ya, lse_a, yb, lse_b, mixed, x1, h2, relu, act))
        x = x2
    dx, dx16, loss_tile, d_final = _loss_head(x, row(small["final_norm_g"]), target, name="loss_head")

    parts = {k: [None] * depth for k in ("in", "out", "mi", "mo")}
    got = {k: [None] * depth for k in ("in", "out", "mi", "mo")}
    small_grads = [None] * depth

    def to_sibling(call, kind, l, grad):
        res, (from_sibling,) = call(carry=_sibling_task([grad]))
        parts[kind][l] = _chip_partials(grad, from_sibling, place, name=f"chip_partials_{kind}_{l}")
        return res

    def to_chips(call, kind, l):
        if l >= depth:
            return call()
        res, (got[kind][l],) = call(carry=_chips_task([parts[kind][l]]))
        return res

    for l in reversed(range(depth)):
        x0, h1, qkv, f, fb, cq, ck, ya, lse_a, yb, lse_b, mixed, x1, h2, relu, act = saved[l]
        du = to_chips(functools.partial(_matmul, dx16, whole["mo", l], tb=True, out_dtype=BF16, mode="mul2",
                                        aux=relu, name=f"mlp_out_dx_{l}"), "in", l + 1)
        dw_mo = _matmul(act, dx16, ta=True, name=f"mlp_out_dw_{l}")
        dw_mo = dw_mo.reshape(N_DEV, d_ff // N_DEV, d_model)
        dh2 = to_sibling(functools.partial(_matmul, du, whole["mi", l], tb=True, name=f"mlp_in_dx_{l}"),
                         "mo", l, dw_mo)
        dw_mi = to_chips(functools.partial(_matmul, h2, du, ta=True, out_chunks=N_DEV,
                                           name=f"mlp_in_dw_{l}"), "mo", l)
        dx1, dx1_16, d_norm2 = _rmsnorm_bwd(x1, row(small["norm2_g"][l]), dh2, dx, name=f"norm2_bwd_{l}")
        dmixed = to_sibling(functools.partial(_matmul, dx1_16, whole["out", l], tb=True, name=f"out_proj_dx_{l}"),
                            "mi", l, dw_mi)
        dw_out = _matmul(mixed, dx1_16, ta=True, name=f"out_proj_dw_{l}")
        dw_out = dw_out.reshape(N_DEV, 2 * dh // N_DEV, d_model)
        dya, dyb, d_ga, d_gb = _outnorm_bwd(ya, yb, row(small["outnorm_a_g"][l]), row(small["outnorm_b_g"][l]),
                                            dmixed, name=f"outnorm_bwd_{l}")
        dqa, dka, dva, dcq, dck = to_chips(functools.partial(
            _fox_bwd, qkv, cq, ck, lse_a, ya, dya, n_heads, name=f"fox_bwd_{l}"), "mi", l)
        dqb, dkb, dvb, d_rb = to_sibling(functools.partial(
            _dil_bwd, qkv, small["rel_bias"], lse_b, yb, dyb, n_heads, 3 * n_heads, name=f"dil_bwd_{l}"),
            "out", l, dw_out)
        dc = _pad_lanes((dcq[:, 0, :] + dck[:, :, 0]).T)
        df, dfb = _gates_bwd(f, fb, dc, name=f"gates_bwd_{l}")
        dqkv = jnp.concatenate([dqa, dka, dva, dqb, dkb, dvb], axis=1)
        dw_qkv = _matmul(h1, dqkv, ta=True, name=f"qkv_proj_dw_{l}")
        dw_f = _matmul(h1, df, ta=True, name=f"gate_proj_dw_{l}")
        dw_in = jnp.stack([jnp.concatenate(in_columns(dw_qkv, dw_f, p * in_shard, (p + 1) * in_shard), axis=1)
                           for p in range(N_DEV)])
        dh1 = _matmul(df, whole["f", l], tb=True, name=f"gate_proj_dx_{l}")
        dh1 = to_chips(functools.partial(_matmul, dqkv, whole["qkv", l], tb=True, addend=dh1,
                                         name=f"qkv_proj_dx_{l}"), "out", l)
        dx, dx16, d_norm1 = to_sibling(functools.partial(_rmsnorm_bwd, x0, row(small["norm1_g"][l]), dh1, dx1,
                                                         name=f"norm1_bwd_{l}"), "in", l, dw_in)
        small_grads[l] = dict(norm1_g=d_norm1[0], forget_b=dfb[0, :n_heads], rel_bias=d_rb[:, 0, :REL_BUCKETS].T,
                              outnorm_a_g=d_ga[0], outnorm_b_g=d_gb[0], norm2_g=d_norm2[0])
    return loss_tile[0, 0], dx, d_final[0], small_grads, parts, got


def _pack_small(parts, rows):
    flat = jnp.concatenate([p.reshape(-1).astype(F32) for p in parts])
    return jnp.pad(flat, (0, rows * 128 - flat.shape[0])).reshape(rows, 128)


def _unpack_small(packed, shapes):
    flat = packed.reshape(-1)
    out, pos = [], 0
    for shp in shapes:
        size = int(np.prod(shp))
        out.append(flat[pos:pos + size].reshape(shp))
        pos += size
    return out


def kernel(x, norm1_g, w_in, forget_b, rel_bias, outnorm_a_g, outnorm_b_g, w_out, norm2_g, w_mlp_in, w_mlp_out, final_norm_g, loss_target, m_norm1_g, m_w_in, m_forget_b, m_rel_bias, m_outnorm_a_g, m_outnorm_b_g, m_w_out, m_norm2_g, m_w_mlp_in, m_w_mlp_out, m_final_norm_g, v_norm1_g, v_w_in, v_forget_b, v_rel_bias, v_outnorm_a_g, v_outnorm_b_g, v_w_out, v_norm2_g, v_w_mlp_in, v_w_mlp_out, v_final_norm_g):
    depth, d_model, in_shard = w_in.shape
    n_heads = forget_b.shape[1]
    assert in_shard * N_DEV == 6 * n_heads * HEAD_DIM + n_heads and x.shape[0] == 1
    place = jnp.stack([lax.axis_index("x"), lax.axis_index("y"), lax.axis_index("c")]).astype(jnp.int32)

    small_names = ["norm1_g", "forget_b", "rel_bias", "outnorm_a_g", "outnorm_b_g", "norm2_g", "final_norm_g"]
    small_w = dict(norm1_g=norm1_g, forget_b=forget_b, rel_bias=rel_bias, outnorm_a_g=outnorm_a_g,
                   outnorm_b_g=outnorm_b_g, norm2_g=norm2_g, final_norm_g=final_norm_g)
    loss_part, dx, d_final, grads, parts, got = _train_step(
        x[0], loss_target[0], small_w, (w_in, w_out, w_mlp_in, w_mlp_out), place)

    small_m = dict(norm1_g=m_norm1_g, forget_b=m_forget_b, rel_bias=m_rel_bias, outnorm_a_g=m_outnorm_a_g,
                   outnorm_b_g=m_outnorm_b_g, norm2_g=m_norm2_g, final_norm_g=m_final_norm_g)
    small_v = dict(norm1_g=v_norm1_g, forget_b=v_forget_b, rel_bias=v_rel_bias, outnorm_a_g=v_outnorm_a_g,
                   outnorm_b_g=v_outnorm_b_g, norm2_g=v_norm2_g, final_norm_g=v_final_norm_g)
    small_g = {k: jnp.stack([g[k] for g in grads]) for k in small_names if k not in ("rel_bias", "final_norm_g")}
    small_g["rel_bias"] = functools.reduce(jnp.add, [g["rel_bias"] for g in grads])
    small_g["final_norm_g"] = d_final
    shapes = [small_w[k].shape for k in small_names]
    total = sum(int(np.prod(s)) for s in shapes) + 1
    rows = -(-total // (8 * 128)) * 8
    packed_g = _pack_small([small_g[k] for k in small_names] + [loss_part], rows)
    packed_g = _all_reduce_small(packed_g, name="reduce_small")
    zero = jnp.zeros((1,), F32)
    packed_w = _pack_small([small_w[k] for k in small_names] + [zero], rows)
    packed_m = _pack_small([small_m[k] for k in small_names] + [zero], rows)
    packed_v = _pack_small([small_v[k] for k in small_names] + [zero + 1.0], rows)
    packed_d, packed_nm, packed_nv = _adamw_small(packed_w, packed_g, packed_m, packed_v, name="adamw_small")
    g_small = dict(zip(small_names, _unpack_small(packed_g, shapes)))
    d_small = dict(zip(small_names, _unpack_small(packed_d, shapes)))
    nm_small = dict(zip(small_names, _unpack_small(packed_nm, shapes)))
    nv_small = dict(zip(small_names, _unpack_small(packed_nv, shapes)))
    loss = packed_g.reshape(-1)[total - 1]

    big_out = {}
    for kind, nm, w, m, v in [("mi", "w_mlp_in", w_mlp_in, m_w_mlp_in, v_w_mlp_in),
                              ("mo", "w_mlp_out", w_mlp_out, m_w_mlp_out, v_w_mlp_out),
                              ("out", "w_out", w_out, m_w_out, v_w_out), ("in", "w_in", w_in, m_w_in, v_w_in)]:
        adamw = functools.partial(_adamw_sharded, w, m, v, parts[kind], got[kind], place, name=f"adamw_{nm}")
        if got["in"][0] is None:
            big_out[nm], (got["in"][0],) = adamw(carry=_chips_task([parts["in"][0]]))
        else:
            big_out[nm] = adamw()

    order = ["norm1_g", "w_in", "forget_b", "rel_bias", "outnorm_a_g", "outnorm_b_g", "w_out", "norm2_g",
             "w_mlp_in", "w_mlp_out", "final_norm_g"]
    pick = lambda k, idx, small: big_out[k][idx] if k in big_out else small[k]
    outs = [loss, dx[None]]
    outs += [pick(k, 0, g_small) for k in order]
    outs += [pick(k, 1, d_small) for k in order]
    outs += [pick(k, 2, nm_small) for k in order]
    outs += [pick(k, 3, nv_small) for k in order]
    return tuple(outs)
```

```python
import functools
import math

import numpy as np
import jax
import jax.numpy as jnp
from jax import lax
from jax.experimental import pallas as pl
from jax.experimental.pallas import tpu as pltpu

F32 = jnp.float32
BF16 = jnp.bfloat16

HEAD_DIM = 128
DIL_PATTERNS = ((128, 1), (512, 4), (2048, 16))
DIL_BLOCK = 128
REL_BUCKETS = 32
REL_MAX_DISTANCE = 2048
NORM_EPS = 1e-6
NEG_INF = -1e30
ADAM_LR = 0.001
ADAM_B1 = 0.9
ADAM_B2 = 0.999
ADAM_EPS = 1e-08
ADAM_WD = 0.01
ADAM_STEP = 10

N_DEV = 8
V7X_VMEM_LIMIT_BYTES = 56 * 1024 * 1024
MESH = pl.DeviceIdType.MESH


def _params(*semantics):
    return pltpu.CompilerParams(dimension_semantics=semantics, vmem_limit_bytes=V7X_VMEM_LIMIT_BYTES)


HBM_SPEC = pl.BlockSpec(memory_space=pltpu.HBM)


class _Carried:
    def __init__(self, operands, out_shape, scratch, start, forward, finish, aliases=None):
        self.operands, self.out_shape, self.scratch = list(operands), list(out_shape), list(scratch)
        self.start, self.forward, self.finish = start, forward, finish
        self.aliases = dict(aliases or {})


def _join(tasks):
    if len(tasks) == 1:
        return tasks[0]
    spans, aliases = [], {}
    i0 = o0 = s0 = 0
    for t in tasks:
        i1, o1, s1 = i0 + len(t.operands), o0 + len(t.out_shape), s0 + len(t.scratch)
        spans.append((slice(i0, i1), slice(o0, o1), slice(s0, s1)))
        aliases.update({i0 + i: o0 + o for i, o in t.aliases.items()})
        i0, o0, s0 = i1, o1, s1

    def phase(which):
        def run(ins, outs, sems):
            for t, (si, so, ss) in zip(tasks, spans):
                getattr(t, which)(ins[si], outs[so], sems[ss])
        return run

    return _Carried(sum((t.operands for t in tasks), []), sum((t.out_shape for t in tasks), []),
                    sum((t.scratch for t in tasks), []), phase("start"), phase("forward"), phase("finish"),
                    aliases)


FORWARD_AT = 0.8


def _call(body, *, name, grid, in_specs, out_specs, out_shape, operands, scratch=(), semantics, carry=None,
          tables=()):
    in_specs, out_specs, out_shape, scratch = list(in_specs), list(out_specs), list(out_shape), list(scratch)
    n_tab = len(tables)

    def run(fn, in_specs, out_specs, out_shape, scratch, operands, semantics, aliases):
        return pl.pallas_call(
            fn, name=name,
            grid_spec=pltpu.PrefetchScalarGridSpec(
                num_scalar_prefetch=n_tab, grid=grid, in_specs=in_specs, out_specs=out_specs,
                scratch_shapes=scratch),
            out_shape=out_shape, input_output_aliases=aliases,
            compiler_params=_params(*semantics))(*tables, *operands)

    if carry is None:
        return run(body, in_specs, out_specs, out_shape, scratch, operands, semantics, {})
    n_in, n_out = len(in_specs), len(out_specs)
    ci, co, cs = len(carry.operands), len(carry.out_shape), len(carry.scratch)
    steps = int(np.prod(grid))
    forward_step = min(int(steps * FORWARD_AT), steps - 1)

    def carrying(*refs):
        tabs, refs = refs[:n_tab], refs[n_tab:]
        main_in, c_in = refs[:n_in], refs[n_in:n_in + ci]
        main_out = refs[n_in + ci:n_in + ci + n_out]
        c_out = refs[n_in + ci + n_out:n_in + ci + n_out + co]
        rest = refs[n_in + ci + n_out + co:]
        main_scr, c_scr = rest[:len(rest) - cs], rest[len(rest) - cs:]
        step = 0
        for axis, extent in enumerate(grid):
            step = step * extent + pl.program_id(axis)

        @pl.when(step == 0)
        def _():
            carry.start(c_in, c_out, c_scr)

        body(*tabs, *main_in, *main_out, *main_scr)

        @pl.when(step == forward_step)
        def _():
            carry.forward(c_in, c_out, c_scr)

        @pl.when(step == steps - 1)
        def _():
            carry.finish(c_in, c_out, c_scr)

    aliases = {n_tab + n_in + i: n_out + o for i, o in carry.aliases.items()}
    res = run(carrying, in_specs + [HBM_SPEC] * ci, out_specs + [HBM_SPEC] * co, out_shape + carry.out_shape,
              scratch + carry.scratch, (*operands, *carry.operands), ["arbitrary"] * len(grid), aliases)
    return res[:n_out], res[n_out:]


def _run_carried(carry, *, name):
    ci, co = len(carry.operands), len(carry.out_shape)

    def body(*refs):
        parts = (refs[:ci], refs[ci:ci + co], refs[ci + co:])
        carry.start(*parts)
        carry.forward(*parts)
        carry.finish(*parts)

    return pl.pallas_call(
        body, name=name, in_specs=[HBM_SPEC] * ci, out_specs=[HBM_SPEC] * co,
        out_shape=carry.out_shape, scratch_shapes=carry.scratch)(*carry.operands)


def _fit(dim, want, unit=128):
    if dim <= want:
        return dim
    t = want - want % unit
    while dim % t:
        t -= unit
    return t


def _matmul(a, b, *, name, ta=False, tb=False, out_dtype=F32, tm=1024, tn=1024, tk=2048,
            addend=None, mode=None, aux=None, out_chunks=None, b_chunked=False, carry=None):
    m_dim, k_dim = (a.shape[1], a.shape[0]) if ta else a.shape
    b_rows, b_cols = (b.shape[1], b.shape[0] * b.shape[2]) if b_chunked else b.shape
    n_dim = b_rows if tb else b_cols
    assert (b_cols if tb else b_rows) == k_dim
    b_chunk = b.shape[2] if b_chunked else b_cols
    tm, tn, tk = _fit(m_dim, tm), _fit(n_dim // (out_chunks or 1), tn), _fit(k_dim, tk)
    if b_chunked:
        tn, tk = (tn, _fit(b_chunk, tk)) if tb else (_fit(b_chunk, tn), tk)
    assert m_dim % tm == 0 and n_dim % tn == 0 and k_dim % tk == 0, (name, a.shape, b.shape)
    nk = k_dim // tk
    a_spec = (pl.BlockSpec((tk, tm), lambda i, j, k: (k, i)) if ta
              else pl.BlockSpec((tm, tk), lambda i, j, k: (i, k)))
    if b_chunked:
        per_b = b_chunk // (tk if tb else tn)
        b_spec = (pl.BlockSpec((None, tn, tk), lambda i, j, k: (k // per_b, j, k % per_b)) if tb
                  else pl.BlockSpec((None, tk, tn), lambda i, j, k: (j // per_b, k, j % per_b)))
    else:
        b_spec = (pl.BlockSpec((tn, tk), lambda i, j, k: (j, k)) if tb
                  else pl.BlockSpec((tk, tn), lambda i, j, k: (k, j)))
    mn_spec = pl.BlockSpec((tm, tn), lambda i, j, k: (i, j))
    if out_chunks is None:
        o_spec, o_shape = mn_spec, (m_dim, n_dim)
    else:
        per = n_dim // out_chunks // tn
        assert per * tn * out_chunks == n_dim
        o_spec = pl.BlockSpec((None, tm, tn), lambda i, j, k: (j // per, i, j % per))
        o_shape = (out_chunks, m_dim, n_dim // out_chunks)
    dims = (((0 if ta else 1,), (1 if tb else 0,)), ((), ()))
    n_out = 2 if mode == "relu_sq" else 1
    in_specs, operands = [a_spec, b_spec], [a, b]
    if addend is not None:
        in_specs.append(mn_spec)
        operands.append(addend)
    if mode == "mul2":
        in_specs.append(mn_spec)
        operands.append(aux)

    def body(*refs):
        a_ref, b_ref = refs[0], refs[1]
        pos = 2
        add_ref = aux_ref = None
        if addend is not None:
            add_ref, pos = refs[pos], pos + 1
        if mode == "mul2":
            aux_ref, pos = refs[pos], pos + 1
        outs = refs[pos:pos + n_out]

        def finish(acc):
            if add_ref is not None:
                acc = acc + add_ref[...].astype(F32)
            if mode == "relu_sq":
                r = jnp.maximum(acc, 0.0)
                outs[0][...] = r.astype(outs[0].dtype)
                outs[1][...] = (r * r).astype(outs[1].dtype)
            elif mode == "mul2":
                outs[0][...] = (acc * (2.0 * aux_ref[...].astype(F32))).astype(outs[0].dtype)
            else:
                outs[0][...] = acc.astype(outs[0].dtype)

        part = lax.dot_general(a_ref[...].astype(BF16), b_ref[...].astype(BF16), dims,
                               preferred_element_type=F32)
        if nk == 1:
            finish(part)
        else:
            acc_ref = refs[-1]
            k = pl.program_id(2)

            @pl.when(k == 0)
            def _():
                acc_ref[...] = part

            @pl.when(k > 0)
            def _():
                acc_ref[...] += part

            @pl.when(k == nk - 1)
            def _():
                finish(acc_ref[...])

    out_shape = [jax.ShapeDtypeStruct(o_shape, out_dtype)] * n_out
    res = _call(
        body, name=name, grid=(m_dim // tm, n_dim // tn, nk),
        in_specs=in_specs, out_specs=[o_spec] * n_out, out_shape=out_shape, operands=operands,
        scratch=[pltpu.VMEM((tm, tn), F32)] if nk > 1 else [],
        semantics=("parallel", "parallel", "arbitrary"), carry=carry)
    if carry is not None:
        res, carried = res
        return (res if n_out > 1 else res[0]), carried
    return res if n_out > 1 else res[0]


def _norm_fwd_math(x, g):
    r = lax.rsqrt(jnp.mean(x * x, axis=-1, keepdims=True) + NORM_EPS)
    return (x * r) * g


def _norm_bwd_math(x, g, dy):
    r = lax.rsqrt(jnp.mean(x * x, axis=-1, keepdims=True) + NORM_EPS)
    xh = x * r
    dxh = dy * g
    dx = r * (dxh - xh * jnp.mean(dxh * xh, axis=-1, keepdims=True))
    return dx, jnp.sum(dy * xh, axis=0, keepdims=True)


def _row_tile(rows, want=256):
    t = min(rows, want)
    assert rows % t == 0
    return t


def _rmsnorm_fwd(x, g, *, name):
    s, d = x.shape
    tr = _row_tile(s)

    def body(x_ref, g_ref, h_ref):
        h_ref[...] = _norm_fwd_math(x_ref[...], g_ref[...]).astype(BF16)

    return pl.pallas_call(
        body, name=name, grid=(s // tr,),
        in_specs=[pl.BlockSpec((tr, d), lambda i: (i, 0)), pl.BlockSpec((1, d), lambda i: (0, 0))],
        out_specs=pl.BlockSpec((tr, d), lambda i: (i, 0)),
        out_shape=jax.ShapeDtypeStruct((s, d), BF16),
        compiler_params=_params("parallel"),
    )(x, g)


def _rmsnorm_bwd(x, g, dh, dres, *, name, carry=None):
    s, d = x.shape
    tr = _row_tile(s)

    def body(x_ref, g_ref, dh_ref, dres_ref, dx_ref, dx16_ref, dg_ref):
        dx, dg = _norm_bwd_math(x_ref[...], g_ref[...], dh_ref[...])
        dx = dres_ref[...] + dx
        dx_ref[...] = dx
        dx16_ref[...] = dx.astype(BF16)

        @pl.when(pl.program_id(0) == 0)
        def _():
            dg_ref[...] = dg

        @pl.when(pl.program_id(0) > 0)
        def _():
            dg_ref[...] += dg

    row = pl.BlockSpec((tr, d), lambda i: (i, 0))
    vec = pl.BlockSpec((1, d), lambda i: (0, 0))
    return _call(
        body, name=name, grid=(s // tr,),
        in_specs=[row, vec, row, row], out_specs=[row, row, vec],
        out_shape=[jax.ShapeDtypeStruct((s, d), F32), jax.ShapeDtypeStruct((s, d), BF16),
                   jax.ShapeDtypeStruct((1, d), F32)],
        operands=(x, g, dh, dres), semantics=("arbitrary",), carry=carry)


def _outnorm_fwd(ya, yb, ga, gb, *, name):
    s, da = ya.shape
    db = yb.shape[1]
    tr = _row_tile(s)

    def body(ya_ref, yb_ref, ga_ref, gb_ref, o_ref):
        o_ref[:, :da] = _norm_fwd_math(ya_ref[...], ga_ref[...]).astype(BF16)
        o_ref[:, da:] = _norm_fwd_math(yb_ref[...], gb_ref[...]).astype(BF16)

    return pl.pallas_call(
        body, name=name, grid=(s // tr,),
        in_specs=[pl.BlockSpec((tr, da), lambda i: (i, 0)), pl.BlockSpec((tr, db), lambda i: (i, 0)),
                  pl.BlockSpec((1, da), lambda i: (0, 0)), pl.BlockSpec((1, db), lambda i: (0, 0))],
        out_specs=pl.BlockSpec((tr, da + db), lambda i: (i, 0)),
        out_shape=jax.ShapeDtypeStruct((s, da + db), BF16),
        compiler_params=_params("parallel"),
    )(ya, yb, ga, gb)


def _outnorm_bwd(ya, yb, ga, gb, dmixed, *, name):
    s, da = ya.shape
    db = yb.shape[1]
    tr = _row_tile(s)

    def body(ya_ref, yb_ref, ga_ref, gb_ref, dm_ref, dya_ref, dyb_ref, dga_ref, dgb_ref):
        dxa, dga = _norm_bwd_math(ya_ref[...], ga_ref[...], dm_ref[:, :da])
        dxb, dgb = _norm_bwd_math(yb_ref[...], gb_ref[...], dm_ref[:, da:])
        dya_ref[...] = dxa
        dyb_ref[...] = dxb

        @pl.when(pl.program_id(0) == 0)
        def _():
            dga_ref[...] = dga
            dgb_ref[...] = dgb

        @pl.when(pl.program_id(0) > 0)
        def _():
            dga_ref[...] += dga
            dgb_ref[...] += dgb

    ra = pl.BlockSpec((tr, da), lambda i: (i, 0))
    rb = pl.BlockSpec((tr, db), lambda i: (i, 0))
    va = pl.BlockSpec((1, da), lambda i: (0, 0))
    vb = pl.BlockSpec((1, db), lambda i: (0, 0))
    return pl.pallas_call(
        body, name=name, grid=(s // tr,),
        in_specs=[ra, rb, va, vb, pl.BlockSpec((tr, da + db), lambda i: (i, 0))],
        out_specs=[ra, rb, va, vb],
        out_shape=[jax.ShapeDtypeStruct((s, da), F32), jax.ShapeDtypeStruct((s, db), F32),
                   jax.ShapeDtypeStruct((1, da), F32), jax.ShapeDtypeStruct((1, db), F32)],
        compiler_params=_params("arbitrary"),
    )(ya, yb, ga, gb, dmixed)


def _loss_head(x, g, target, *, name):
    s, d = x.shape
    tr = _row_tile(s)

    def body(x_ref, g_ref, t_ref, dx_ref, dx16_ref, loss_ref, dg_ref):
        xv, gv = x_ref[...], g_ref[...]
        err = _norm_fwd_math(xv, gv) - t_ref[...]
        part = 0.5 * jnp.sum(jnp.mean(err * err, axis=-1, keepdims=True), axis=0, keepdims=True)
        dx, dg = _norm_bwd_math(xv, gv, err * (1.0 / d))
        dx_ref[...] = dx
        dx16_ref[...] = dx.astype(BF16)
        part = jnp.broadcast_to(part, (8, 128))

        @pl.when(pl.program_id(0) == 0)
        def _():
            dg_ref[...] = dg
            loss_ref[...] = part

        @pl.when(pl.program_id(0) > 0)
        def _():
            dg_ref[...] += dg
            loss_ref[...] += part

    row = pl.BlockSpec((tr, d), lambda i: (i, 0))
    vec = pl.BlockSpec((1, d), lambda i: (0, 0))
    return pl.pallas_call(
        body, name=name, grid=(s // tr,),
        in_specs=[row, vec, row],
        out_specs=[row, row, pl.BlockSpec((8, 128), lambda i: (0, 0)), vec],
        out_shape=[jax.ShapeDtypeStruct((s, d), F32), jax.ShapeDtypeStruct((s, d), BF16),
                   jax.ShapeDtypeStruct((8, 128), F32), jax.ShapeDtypeStruct((1, d), F32)],
        compiler_params=_params("arbitrary"),
    )(x, g, target)


def _split3(x):
    hi = x.astype(BF16)
    rem = x - hi.astype(F32)
    mid = rem.astype(BF16)
    lo = (rem - mid.astype(F32)).astype(BF16)
    return hi, mid, lo


def _tri_sum(tri, x):
    hi, mid, lo = _split3(x)
    dot = functools.partial(jnp.dot, preferred_element_type=F32)
    return dot(tri, hi) + dot(tri, mid) + dot(tri, lo)


def _gates_fwd(f, fb, *, name):
    s, w = f.shape
    tb = 128
    nb = s // tb

    def body(f_ref, fb_ref, c_ref, carry):
        @pl.when(pl.program_id(0) == 0)
        def _():
            carry[...] = jnp.zeros_like(carry)

        logf = jax.nn.log_sigmoid(f_ref[...] + fb_ref[...])
        row = lax.broadcasted_iota(jnp.int32, (tb, tb), 0)
        col = lax.broadcasted_iota(jnp.int32, (tb, tb), 1)
        tri = (row >= col).astype(BF16)
        c = _tri_sum(tri, logf) + carry[...]
        c_ref[...] = c
        carry[...] = c[tb - 1:tb, :]

    return pl.pallas_call(
        body, name=name, grid=(nb,),
        in_specs=[pl.BlockSpec((tb, w), lambda i: (i, 0)), pl.BlockSpec((1, w), lambda i: (0, 0))],
        out_specs=pl.BlockSpec((tb, w), lambda i: (i, 0)),
        out_shape=jax.ShapeDtypeStruct((s, w), F32),
        scratch_shapes=[pltpu.VMEM((1, w), F32)],
        compiler_params=_params("arbitrary"),
    )(f, fb)


def _gates_bwd(f, fb, dc, *, name):
    s, w = f.shape
    tb = 128
    nb = s // tb

    def body(f_ref, fb_ref, dc_ref, df_ref, dfb_ref, carry):
        @pl.when(pl.program_id(0) == 0)
        def _():
            carry[...] = jnp.zeros_like(carry)
            dfb_ref[...] = jnp.zeros_like(dfb_ref)

        row = lax.broadcasted_iota(jnp.int32, (tb, tb), 0)
        col = lax.broadcasted_iota(jnp.int32, (tb, tb), 1)
        tri = (row <= col).astype(BF16)
        dlogf = _tri_sum(tri, dc_ref[...]) + carry[...]
        carry[...] = dlogf[0:1, :]
        df = dlogf * jax.nn.sigmoid(-(f_ref[...] + fb_ref[...]))
        df_ref[...] = df
        dfb_ref[...] += jnp.sum(df, axis=0, keepdims=True)

    rev = pl.BlockSpec((tb, w), lambda i: (nb - 1 - i, 0))
    vec = pl.BlockSpec((1, w), lambda i: (0, 0))
    return pl.pallas_call(
        body, name=name, grid=(nb,),
        in_specs=[rev, vec, rev], out_specs=[rev, vec],
        out_shape=[jax.ShapeDtypeStruct((s, w), F32), jax.ShapeDtypeStruct((1, w), F32)],
        scratch_shapes=[pltpu.VMEM((1, w), F32)],
        compiler_params=_params("arbitrary"),
    )(f, fb, dc)


def _nt(a, b):
    return lax.dot_general(a, b, (((1,), (1,)), ((), ())), preferred_element_type=F32)


def _tn(a, b):
    return lax.dot_general(a, b, (((0,), (0,)), ((), ())), preferred_element_type=F32)


def _nn(a, b):
    return jnp.dot(a, b, preferred_element_type=F32)


DIL_GROUP = 4
FOX_TILE = 1024
LOG2E = 1.4426950408889634


def _causal_pairs(nt, key_major):
    if key_major:
        pairs = [(q, k) for k in range(nt) for q in range(k, nt)]
    else:
        pairs = [(q, k) for q in range(nt) for k in range(q + 1)]
    return (jnp.asarray([p[0] for p in pairs], jnp.int32), jnp.asarray([p[1] for p in pairs], jnp.int32))


def _fox_fwd(qkv, c_row, ck_lanes, n_heads, *, name, carry=None):
    s = qkv.shape[0]
    e = HEAD_DIM
    t = min(FOX_TILE, s)
    nt = s // t
    scale2 = e ** -0.5 * LOG2E
    lanes = 128
    q_tab, k_tab = _causal_pairs(nt, key_major=False)

    def body(q_tab, k_tab, q_ref, k_ref, v_ref, cq_ref, ck_ref, o_ref, lse_ref,
             m_scr, l_scr, acc_scr, s_scr, p_scr):
        pair = pl.program_id(1)
        qi, ki = q_tab[pair], k_tab[pair]

        @pl.when(ki == 0)
        def _():
            m_scr[...] = jnp.full_like(m_scr, NEG_INF)
            l_scr[...] = jnp.zeros_like(l_scr)
            acc_scr[...] = jnp.zeros_like(acc_scr)

        def update(diagonal):
            s_scr[...] = _nt(k_ref[...], q_ref[...])
            ck2 = ck_ref[...]
            for c0 in range(0, t, lanes):
                cols = pl.ds(c0, lanes)

                def logits2():
                    x = s_scr[:, cols] * scale2 - ck2
                    if diagonal:
                        key = lax.broadcasted_iota(jnp.int32, x.shape, 0)
                        qry = c0 + lax.broadcasted_iota(jnp.int32, x.shape, 1)
                        x = jnp.where(key <= qry, x, NEG_INF)
                    return x

                m_old = m_scr[:, cols]
                m_new = jnp.maximum(m_old, jnp.max(logits2(), axis=0, keepdims=True))
                p = jnp.exp2(logits2() - m_new)
                alpha = jnp.exp2(m_old - m_new)
                l_scr[:, cols] = alpha * l_scr[:, cols] + jnp.sum(p, axis=0, keepdims=True)
                m_scr[:, cols] = m_new
                acc_scr[:, cols] = alpha * acc_scr[:, cols]
                p_scr[:, cols] = p.astype(BF16)
            acc_scr[...] += _tn(v_ref[...], p_scr[...])

        @pl.when(ki < qi)
        def _():
            update(False)

        @pl.when(ki == qi)
        def _():
            update(True)
            o_ref[...] = (acc_scr[...] / l_scr[...]).T
            lse_ref[...] = (m_scr[...] + jnp.log2(l_scr[...])) * (1.0 / LOG2E) + cq_ref[...]

    h_ = n_heads
    return _call(
        body, name=name, grid=(h_, int(q_tab.shape[0])), tables=(q_tab, k_tab),
        in_specs=[
            pl.BlockSpec((t, e), lambda h, p, qt, kt: (qt[p], h)),
            pl.BlockSpec((t, e), lambda h, p, qt, kt: (kt[p], h_ + h)),
            pl.BlockSpec((t, e), lambda h, p, qt, kt: (kt[p], 2 * h_ + h)),
            pl.BlockSpec((None, 1, t), lambda h, p, qt, kt: (h, 0, qt[p])),
            pl.BlockSpec((None, t, lanes), lambda h, p, qt, kt: (h, kt[p], 0)),
        ],
        out_specs=[pl.BlockSpec((t, e), lambda h, p, qt, kt: (qt[p], h)),
                   pl.BlockSpec((None, 1, t), lambda h, p, qt, kt: (h, 0, qt[p]))],
        out_shape=[jax.ShapeDtypeStruct((s, h_ * e), F32), jax.ShapeDtypeStruct((h_, 1, s), F32)],
        operands=(qkv, qkv, qkv, c_row, ck_lanes),
        scratch=[pltpu.VMEM((1, t), F32), pltpu.VMEM((1, t), F32), pltpu.VMEM((e, t), F32),
                 pltpu.VMEM((t, t), F32), pltpu.VMEM((t, t), BF16)],
        semantics=("parallel", "arbitrary"), carry=carry)


def _fox_bwd(qkv, c_row, ck_lanes, lse, y, dy, n_heads, *, name, carry=None):
    s = qkv.shape[0]
    e = HEAD_DIM
    t = min(FOX_TILE, s)
    nt = s // t
    scale = e ** -0.5
    scale2 = scale * LOG2E
    lanes = 128
    q_tab, k_tab = _causal_pairs(nt, key_major=False)
    n_pairs = int(q_tab.shape[0])

    def body(q_tab, k_tab, q_ref, k_ref, v_ref, c_ref, ck_ref, lse_ref, y_ref, dy_ref,
             dq_ref, dk_ref, dv_ref, dcq_ref, dck_ref,
             dq_scr, dk_scr, dv_scr, dck_scr, s_scr, dp_scr, p_scr, ds_scr, do_scr, delta_scr, shift_scr):
        pair = pl.program_id(1)
        qi, ki = q_tab[pair], k_tab[pair]

        @pl.when(pair == 0)
        def _():
            dk_scr[...] = jnp.zeros_like(dk_scr)
            dv_scr[...] = jnp.zeros_like(dv_scr)
            dck_scr[...] = jnp.zeros_like(dck_scr)

        @pl.when(ki == 0)
        def _():
            do = dy_ref[...]
            delta_scr[...] = lax.dot_general(jnp.ones((8, e), F32), do * y_ref[...], (((1,), (1,)), ((), ())),
                                             precision=lax.Precision.HIGHEST, preferred_element_type=F32)
            shift_scr[...] = (lse_ref[...] - c_ref[...]) * LOG2E
            do_scr[...] = do.astype(BF16)
            dq_scr[...] = jnp.zeros_like(dq_scr)
            dcq_ref[...] = jnp.zeros_like(dcq_ref)

        def update(diagonal):
            s_scr[...] = _nt(k_ref[...], q_ref[...])
            dp_scr[...] = _nt(v_ref[...], do_scr[...])
            ck2 = ck_ref[...]
            k_rows = pl.ds(pl.multiple_of(ki * t, t), t)
            for c0 in range(0, t, lanes):
                cols = pl.ds(c0, lanes)
                x = s_scr[:, cols] * scale2 - ck2
                if diagonal:
                    key = lax.broadcasted_iota(jnp.int32, x.shape, 0)
                    qry = c0 + lax.broadcasted_iota(jnp.int32, x.shape, 1)
                    x = jnp.where(key <= qry, x, NEG_INF)
                p = jnp.exp2(x - shift_scr[:, cols])
                ds = p * (dp_scr[:, cols] - delta_scr[0:1, cols])
                dcq_ref[:, cols] += jnp.sum(ds, axis=0, keepdims=True)
                dck_scr[k_rows, :] += ds
                p_scr[:, cols] = p.astype(BF16)
                ds_scr[:, cols] = ds.astype(BF16)
            dv_scr[k_rows, :] += _nn(p_scr[...], do_scr[...])
            dk_scr[k_rows, :] += scale * _nn(ds_scr[...], q_ref[...])
            dq_scr[...] += scale * _tn(ds_scr[...], k_ref[...])

        @pl.when(ki < qi)
        def _():
            update(False)

        @pl.when(ki == qi)
        def _():
            update(True)
            dq_ref[...] = dq_scr[...].astype(BF16)

        @pl.when(pair == n_pairs - 1)
        def _():
            dk_ref[...] = dk_scr[...].astype(BF16)
            dv_ref[...] = dv_scr[...].astype(BF16)
            dck_ref[...] = -jnp.sum(dck_scr[...], axis=-1, keepdims=True)

    h_ = n_heads
    q_row = pl.BlockSpec((None, 1, t), lambda h, p, qt, kt: (h, 0, qt[p]))
    q_blk = pl.BlockSpec((t, e), lambda h, p, qt, kt: (qt[p], h))
    whole = pl.BlockSpec((s, e), lambda h, p, qt, kt: (0, h))
    return _call(
        body, name=name, grid=(h_, n_pairs), tables=(q_tab, k_tab),
        in_specs=[
            q_blk,
            pl.BlockSpec((t, e), lambda h, p, qt, kt: (kt[p], h_ + h)),
            pl.BlockSpec((t, e), lambda h, p, qt, kt: (kt[p], 2 * h_ + h)),
            q_row,
            pl.BlockSpec((None, t, lanes), lambda h, p, qt, kt: (h, kt[p], 0)),
            q_row, q_blk, q_blk,
        ],
        out_specs=[q_blk, whole, whole, q_row,
                   pl.BlockSpec((None, s, 1), lambda h, p, qt, kt: (h, 0, 0))],
        out_shape=[jax.ShapeDtypeStruct((s, h_ * e), BF16)] * 3
        + [jax.ShapeDtypeStruct((h_, 1, s), F32), jax.ShapeDtypeStruct((h_, s, 1), F32)],
        operands=(qkv, qkv, qkv, c_row, ck_lanes, lse, y, dy),
        scratch=[pltpu.VMEM((t, e), F32), pltpu.VMEM((s, e), F32), pltpu.VMEM((s, e), F32),
                 pltpu.VMEM((s, lanes), F32), pltpu.VMEM((t, t), F32), pltpu.VMEM((t, t), F32), pltpu.VMEM((t, t), BF16),
                 pltpu.VMEM((t, t), BF16), pltpu.VMEM((t, e), BF16), pltpu.VMEM((8, t), F32),
                 pltpu.VMEM((1, t), F32)],
        semantics=("parallel", "arbitrary"), carry=carry)


def _rel_bucket_table(dilation, span):
    dist = np.arange(span + 1, dtype=np.int64) * dilation
    max_exact = REL_BUCKETS // 2
    d = np.maximum(dist.astype(np.float32), np.float32(1.0))
    large = max_exact + (np.log(d / np.float32(max_exact)) / np.float32(math.log(REL_MAX_DISTANCE / max_exact))
                         * np.float32(REL_BUCKETS - max_exact)).astype(np.int32)
    large = np.minimum(large, REL_BUCKETS - 1)
    return np.where(dist < max_exact, dist, large)


def _bucket_bands(dilation, span):
    table = _rel_bucket_table(dilation, span)
    bands = []
    for n, b in enumerate(table):
        if bands and bands[-1][0] == int(b):
            bands[-1][2] = n
        else:
            assert not any(bb[0] == int(b) for bb in bands)
            bands.append([int(b), n, n])
    return [tuple(b) for b in bands]


def _steps_back():
    i = lax.broadcasted_iota(jnp.int32, (DIL_BLOCK, DIL_BLOCK), 0)
    j = lax.broadcasted_iota(jnp.int32, (DIL_BLOCK, DIL_BLOCK), 1)
    return i - j, DIL_BLOCK + i - j


def _bias_tiles(rb_ref, h, bands, span):
    n_cur, n_prev = _steps_back()
    t_cur = jnp.zeros((DIL_BLOCK, DIL_BLOCK), F32)
    t_prev = jnp.zeros((DIL_BLOCK, DIL_BLOCK), F32)
    for b, lo, hi in bands:
        val = rb_ref[b, h]
        t_cur = jnp.where((n_cur >= lo) & (n_cur <= hi), val, t_cur)
        t_prev = jnp.where((n_prev >= lo) & (n_prev <= hi), val, t_prev)
    t_cur = jnp.where(n_cur >= 0, t_cur, NEG_INF)
    t_prev = jnp.where(n_prev <= span, t_prev, NEG_INF)
    return t_cur, t_prev


def _dil_rows(start, dilation):
    return pl.ds(start, DIL_BLOCK, stride=dilation) if dilation > 1 else pl.ds(start, DIL_BLOCK)


def _dil_block_groups(s, dilation, run_group):
    group = dilation * DIL_BLOCK
    per_residue = s // group

    def block(r, n):
        if isinstance(n, int):
            return n * group + r, max(n - 1, 0) * group + r, n == 0
        return (pl.multiple_of(n * group, DIL_BLOCK) + r,
                pl.multiple_of(jnp.maximum(n - 1, 0) * group, DIL_BLOCK) + r, n == 0)

    if per_residue >= DIL_GROUP:
        assert per_residue % DIL_GROUP == 0
        for r in range(dilation):
            def trip(it, carry, r=r):
                run_group([block(r, it * DIL_GROUP + j) for j in range(DIL_GROUP)])
                return carry
            lax.fori_loop(0, per_residue // DIL_GROUP, trip, 0)
    else:
        residues = DIL_GROUP // per_residue
        assert residues * per_residue == DIL_GROUP and dilation % residues == 0
        for r0 in range(0, dilation, residues):
            run_group([block(r, n) for r in range(r0, r0 + residues) for n in range(per_residue)])


def _dil_fwd(qkv, rel_bias, n_heads, col0, *, name, carry=None):
    s = qkv.shape[0]
    e = HEAD_DIM
    scale = e ** -0.5
    n_pat = len(DIL_PATTERNS)
    for window, d in DIL_PATTERNS:
        assert window // d == DIL_BLOCK and s % (d * DIL_BLOCK) == 0
    bands = [_bucket_bands(d, w // d) for w, d in DIL_PATTERNS]

    def body(rb_ref, q_ref, k_ref, v_ref, y_ref, lse_ref, qf, kf, vf, *scr):
        o_scr, l_scr = scr[:n_pat], scr[n_pat:]
        h = pl.program_id(0)
        qf[...] = q_ref[...].astype(F32)
        kf[...] = k_ref[...].astype(F32)
        vf[...] = v_ref[...].astype(F32)
        for pi, (window, d) in enumerate(DIL_PATTERNS):
            t_cur, t_prev = _bias_tiles(rb_ref, h, bands[pi], window // d)

            def run_group(blocks, d=d, pi=pi, t_cur=t_cur, t_prev=t_prev):
                rows = [(_dil_rows(q0, d), _dil_rows(p0, d)) for q0, p0, _ in blocks]
                qb = [qf[cur, :].astype(BF16) for cur, _ in rows]
                s_c = [_nt(q, kf[cur, :].astype(BF16)) for q, (cur, _) in zip(qb, rows)]
                s_p = [_nt(q, kf[prev, :].astype(BF16)) for q, (_, prev) in zip(qb, rows)]
                s_c = [x * scale + t_cur for x in s_c]
                s_p = [x * scale + (t_prev + jnp.where(first, NEG_INF, 0.0)) for x, (_, _, first) in zip(s_p, blocks)]
                m = [jnp.maximum(jnp.max(a, axis=-1, keepdims=True), jnp.max(b, axis=-1, keepdims=True))
                     for a, b in zip(s_c, s_p)]
                p_c = [jnp.exp(a - mm) for a, mm in zip(s_c, m)]
                p_p = [jnp.exp(b - mm) for b, mm in zip(s_p, m)]
                l = [jnp.sum(a, axis=-1, keepdims=True) + jnp.sum(b, axis=-1, keepdims=True)
                     for a, b in zip(p_c, p_p)]
                o = [_nn(a.astype(BF16), vf[cur, :].astype(BF16)) + _nn(b.astype(BF16), vf[prev, :].astype(BF16))
                     for a, b, (cur, prev) in zip(p_c, p_p, rows)]
                for (cur, _), oo, ll, mm in zip(rows, o, l, m):
                    o_scr[pi][cur, :] = oo / ll
                    l_scr[pi][cur, :] = mm + jnp.log(ll)

            _dil_block_groups(s, d, run_group)
        lses = [l_scr[pi][...] for pi in range(n_pat)]
        m = functools.reduce(jnp.maximum, lses)
        ws = [jnp.exp(l - m) for l in lses]
        tot = functools.reduce(jnp.add, ws)
        y = functools.reduce(jnp.add, [w * o_scr[pi][...] for pi, w in enumerate(ws)])
        y_ref[...] = y / tot
        lse_ref[...] = m + jnp.log(tot)

    h_ = n_heads
    return _call(
        body, name=name, grid=(h_,),
        in_specs=[pl.BlockSpec(memory_space=pltpu.SMEM),
                  pl.BlockSpec((s, e), lambda h: (0, col0 + h)),
                  pl.BlockSpec((s, e), lambda h: (0, col0 + h_ + h)),
                  pl.BlockSpec((s, e), lambda h: (0, col0 + 2 * h_ + h))],
        out_specs=[pl.BlockSpec((s, e), lambda h: (0, h)), pl.BlockSpec((None, s, 1), lambda h: (h, 0, 0))],
        out_shape=[jax.ShapeDtypeStruct((s, h_ * e), F32), jax.ShapeDtypeStruct((h_, s, 1), F32)],
        operands=(rel_bias, qkv, qkv, qkv),
        scratch=[pltpu.VMEM((s, e), F32)] * 3 + [pltpu.VMEM((s, e), F32)] * n_pat
        + [pltpu.VMEM((s, 1), F32)] * n_pat,
        semantics=("parallel",), carry=carry)


def _dil_bwd(qkv, rel_bias, lse, y, dy, n_heads, col0, *, name, carry=None):
    s = qkv.shape[0]
    e = HEAD_DIM
    scale = e ** -0.5
    bands = [_bucket_bands(d, w // d) for w, d in DIL_PATTERNS]

    def body(rb_ref, q_ref, k_ref, v_ref, lse_ref, y_ref, dy_ref, dq_ref, dk_ref, dv_ref, drb_ref,
             qf, kf, vf, dqf, dkf, dvf, delta, dt_cur, dt_prev):
        h = pl.program_id(0)
        qf[...] = q_ref[...].astype(F32)
        kf[...] = k_ref[...].astype(F32)
        vf[...] = v_ref[...].astype(F32)
        dqf[...] = jnp.zeros_like(dqf)
        dkf[...] = jnp.zeros_like(dkf)
        dvf[...] = jnp.zeros_like(dvf)
        delta[...] = jnp.sum(dy_ref[...] * y_ref[...], axis=-1, keepdims=True)
        lane = lax.broadcasted_iota(jnp.int32, (1, 128), 1)
        drb = jnp.zeros((1, 128), F32)
        n_cur, n_prev = _steps_back()
        for pi, (window, d) in enumerate(DIL_PATTERNS):
            t_cur, t_prev = _bias_tiles(rb_ref, h, bands[pi], window // d)
            dt_cur[...] = jnp.zeros_like(dt_cur)
            dt_prev[...] = jnp.zeros_like(dt_prev)

            def run_group(blocks, d=d, t_cur=t_cur, t_prev=t_prev):
                rows = [(_dil_rows(q0, d), _dil_rows(p0, d)) for q0, p0, _ in blocks]
                qb = [qf[cur, :].astype(BF16) for cur, _ in rows]
                kc = [kf[cur, :].astype(BF16) for cur, _ in rows]
                kp = [kf[prev, :].astype(BF16) for _, prev in rows]
                vc = [vf[cur, :].astype(BF16) for cur, _ in rows]
                vp = [vf[prev, :].astype(BF16) for _, prev in rows]
                do = [dy_ref[cur, :].astype(BF16) for cur, _ in rows]
                lse_b = [lse_ref[cur, :] for cur, _ in rows]
                delta_b = [delta[cur, :] for cur, _ in rows]
                s_c = [_nt(q, k) for q, k in zip(qb, kc)]
                s_p = [_nt(q, k) for q, k in zip(qb, kp)]
                dp_c = [_nt(g, v) for g, v in zip(do, vc)]
                dp_p = [_nt(g, v) for g, v in zip(do, vp)]
                p_c = [jnp.exp(x * scale + t_cur - ls) for x, ls in zip(s_c, lse_b)]
                p_p = [jnp.exp(x * scale + (t_prev + jnp.where(first, NEG_INF, 0.0)) - ls)
                       for x, ls, (_, _, first) in zip(s_p, lse_b, blocks)]
                ds_c = [p * (g - dl) for p, g, dl in zip(p_c, dp_c, delta_b)]
                ds_p = [p * (g - dl) for p, g, dl in zip(p_p, dp_p, delta_b)]
                dt_cur[...] += functools.reduce(jnp.add, ds_c)
                dt_prev[...] += functools.reduce(jnp.add, ds_p)
                p_c, p_p = [p.astype(BF16) for p in p_c], [p.astype(BF16) for p in p_p]
                ds_c, ds_p = [x.astype(BF16) for x in ds_c], [x.astype(BF16) for x in ds_p]
                dq = [scale * (_nn(a, k1) + _nn(b, k2)) for a, b, k1, k2 in zip(ds_c, ds_p, kc, kp)]
                dk_c = [scale * _tn(a, q) for a, q in zip(ds_c, qb)]
                dv_c = [_tn(p, g) for p, g in zip(p_c, do)]
                dk_p = [scale * _tn(b, q) for b, q in zip(ds_p, qb)]
                dv_p = [_tn(p, g) for p, g in zip(p_p, do)]
                for i, (cur, prev) in enumerate(rows):
                    dqf[cur, :] += dq[i]
                    dkf[cur, :] += dk_c[i]
                    dvf[cur, :] += dv_c[i]
                    dkf[prev, :] += dk_p[i]
                    dvf[prev, :] += dv_p[i]

            _dil_block_groups(s, d, run_group)
            dtc, dtp = dt_cur[...], dt_prev[...]
            for b, lo, hi in bands[pi]:
                tot = (jnp.sum(jnp.where((n_cur >= lo) & (n_cur <= hi), dtc, 0.0))
                       + jnp.sum(jnp.where((n_prev >= lo) & (n_prev <= hi), dtp, 0.0)))
                drb = drb + jnp.where(lane == b, tot, 0.0)
        dq_ref[...] = dqf[...].astype(BF16)
        dk_ref[...] = dkf[...].astype(BF16)
        dv_ref[...] = dvf[...].astype(BF16)
        drb_ref[...] = drb

    h_ = n_heads
    col = pl.BlockSpec((s, e), lambda h: (0, h))
    return _call(
        body, name=name, grid=(h_,),
        in_specs=[pl.BlockSpec(memory_space=pltpu.SMEM),
                  pl.BlockSpec((s, e), lambda h: (0, col0 + h)),
                  pl.BlockSpec((s, e), lambda h: (0, col0 + h_ + h)),
                  pl.BlockSpec((s, e), lambda h: (0, col0 + 2 * h_ + h)),
                  pl.BlockSpec((None, s, 1), lambda h: (h, 0, 0)), col, col],
        out_specs=[col, col, col, pl.BlockSpec((None, 1, 128), lambda h: (h, 0, 0))],
        out_shape=[jax.ShapeDtypeStruct((s, h_ * e), BF16)] * 3 + [jax.ShapeDtypeStruct((h_, 1, 128), F32)],
        operands=(rel_bias, qkv, qkv, qkv, lse, y, dy),
        scratch=[pltpu.VMEM((s, e), F32)] * 6 + [pltpu.VMEM((s, 1), F32)]
        + [pltpu.VMEM((DIL_BLOCK, DIL_BLOCK), F32)] * 2,
        semantics=("parallel",), carry=carry)


def _place():
    x, y, c = lax.axis_index("x"), lax.axis_index("y"), lax.axis_index("c")
    chips = [(1 - x, y), (x, 1 - y), (1 - x, 1 - y)]
    return x, y, c, chips


def _block_index(px, py, pc):
    return 4 * px + 2 * py + pc


def _gather_task(blocks, *, row0=0, rows=None, into=None):
    n = len(blocks)
    part = blocks[0].shape[0]
    rows = rows or part

    def copies(ins, outs, sems):
        send_sems, recv_sems, local_sems = sems
        x, y, c, chips = _place()
        me, sibling = (x, y, c), (x, y, 1 - c)

        def place_of(a, block):
            slot = outs[a].at[_block_index(*block)]
            return slot if part == rows else slot.at[pl.ds(row0, part)]

        def copy(a, k, block, to, src=None):
            slot = place_of(a, block)
            return pltpu.make_async_remote_copy(
                src_ref=slot if src is None else src, dst_ref=slot,
                send_sem=send_sems.at[a, k], recv_sem=recv_sems.at[a, k],
                device_id=to, device_id_type=MESH)

        local = lambda a: pltpu.make_async_copy(ins[a], place_of(a, me), local_sems.at[a])
        return copy, local, me, sibling, c, chips

    def start(ins, outs, sems):
        copy, local, me, sibling, c, chips = copies(ins, outs, sems)
        for a in range(n):
            local(a).start()
            copy(a, 0, me, sibling, src=ins[a]).start()
            for j, chip in enumerate(chips):
                copy(a, 1 + j, me, (*chip, c), src=ins[a]).start()

    def forward(ins, outs, sems):
        copy, local, me, sibling, c, chips = copies(ins, outs, sems)
        for a in range(n):
            for j, chip in enumerate(chips):
                copy(a, 1 + j, (*chip, c), me).wait_recv()
                copy(a, 4 + j, (*chip, c), sibling).start()

    def finish(ins, outs, sems):
        copy, local, me, sibling, c, chips = copies(ins, outs, sems)
        for a in range(n):
            copy(a, 0, sibling, me).wait_recv()
            for j, chip in enumerate(chips):
                copy(a, 4 + j, (*chip, 1 - c), me).wait_recv()
        for a in range(n):
            for k in range(7):
                copy(a, k, me, sibling, src=ins[a]).wait_send()
            local(a).wait()

    return _Carried(
        list(blocks) + list(into or []),
        [jax.ShapeDtypeStruct((N_DEV, rows) + b.shape[1:], b.dtype) for b in blocks],
        [pltpu.SemaphoreType.DMA((n, 7)), pltpu.SemaphoreType.DMA((n, 7)), pltpu.SemaphoreType.DMA((n,))],
        start, forward, finish, aliases={n + a: a for a in range(n)} if into else None)


def _exchange_task(arrays, n_slots, route):
    n = len(arrays)

    def copies(ins, outs, sems):
        send_sems, recv_sems = sems
        out = []
        for a in range(n):
            for j in range(n_slots):
                src, to = route(j)
                out.append(pltpu.make_async_remote_copy(
                    src_ref=ins[a].at[src], dst_ref=outs[a].at[j],
                    send_sem=send_sems.at[a, j], recv_sem=recv_sems.at[a, j],
                    device_id=to, device_id_type=MESH))
        return out

    def start(ins, outs, sems):
        for cp in copies(ins, outs, sems):
            cp.start()

    def finish(ins, outs, sems):
        for cp in copies(ins, outs, sems):
            cp.wait()

    return _Carried(
        arrays, [jax.ShapeDtypeStruct((n_slots,) + g.shape[1:], g.dtype) for g in arrays],
        [pltpu.SemaphoreType.DMA((n, n_slots)), pltpu.SemaphoreType.DMA((n, n_slots))],
        start, lambda ins, outs, sems: None, finish)


def _sibling_task(grads):
    def route(q):
        x, y, c, _ = _place()
        return 2 * q + (1 - c), (x, y, 1 - c)
    return _exchange_task(grads, 4, route)


def _chips_task(parts):
    def route(j):
        x, y, c, chips = _place()
        px, py = chips[j]
        return 2 * px + py, (px, py, c)
    return _exchange_task(parts, 3, route)


def _chip_partials(grad, got, place, *, name):
    _, r, cdim = grad.shape
    tr = _row_tile(r, 512)

    def body(place_ref, g_ref, s_ref, o_ref):
        o_ref[...] = (g_ref[...] + s_ref[...]).astype(BF16)

    return pl.pallas_call(
        body, name=name,
        grid_spec=pltpu.PrefetchScalarGridSpec(
            num_scalar_prefetch=1, grid=(4, r // tr),
            in_specs=[pl.BlockSpec((None, tr, cdim), lambda q, i, pos: (2 * q + pos[2], i, 0)),
                      pl.BlockSpec((None, tr, cdim), lambda q, i, pos: (q, i, 0))],
            out_specs=pl.BlockSpec((None, tr, cdim), lambda q, i, pos: (q, i, 0))),
        out_shape=jax.ShapeDtypeStruct((4, r, cdim), BF16),
        compiler_params=_params("parallel", "parallel"),
    )(place, grad, got)


def _all_reduce_small(v, *, name):
    r, w = v.shape

    def body(v_ref, o_ref, buf, send_sems, recv_sems):
        x, y, c, _ = _place()
        me = _block_index(x, y, c)
        buf[me] = v_ref[...]
        copies = []
        for k in range(1, N_DEV):
            fx, fy, fc = (k >> 2) & 1, (k >> 1) & 1, k & 1
            peer = (x ^ fx, y ^ fy, c ^ fc)
            cp = pltpu.make_async_remote_copy(
                src_ref=v_ref, dst_ref=buf.at[me],
                send_sem=send_sems.at[k - 1], recv_sem=recv_sems.at[k - 1],
                device_id=peer, device_id_type=MESH)
            cp.start()
            copies.append(cp)
        for cp in copies:
            cp.wait()
        acc = buf[0]
        for p in range(1, N_DEV):
            acc = acc + buf[p]
        o_ref[...] = acc

    vmem = pl.BlockSpec(memory_space=pltpu.VMEM)
    return pl.pallas_call(
        body, name=name, in_specs=[vmem], out_specs=vmem,
        out_shape=jax.ShapeDtypeStruct((r, w), F32),
        scratch_shapes=[pltpu.VMEM((N_DEV, r, w), F32), pltpu.SemaphoreType.DMA((N_DEV - 1,)),
                        pltpu.SemaphoreType.DMA((N_DEV - 1,))],
    )(v)


def _adamw_math(w, g, m, v):
    m = ADAM_B1 * m + (1.0 - ADAM_B1) * g
    v = ADAM_B2 * v + (1.0 - ADAM_B2) * (g * g)
    m_hat = m / (1.0 - ADAM_B1 ** ADAM_STEP)
    v_hat = v / (1.0 - ADAM_B2 ** ADAM_STEP)
    delta = -ADAM_LR * (m_hat / (jnp.sqrt(v_hat) + ADAM_EPS) + ADAM_WD * w)
    return delta, m, v


def _adamw_sharded(w, m, v, part, got, place, *, name, carry=None):
    n_l, r, cdim = w.shape
    tr = _row_tile(r, 256)

    def body(place_ref, *refs):
        w_ref, m_ref, v_ref = refs[:3]
        p_refs, g_refs = refs[3:3 + n_l], refs[3 + n_l:3 + 2 * n_l]
        g_out, d_out, m_out, v_out = refs[3 + 2 * n_l:]
        for l in range(n_l):
            @pl.when(pl.program_id(0) == l)
            def _(l=l):
                g = p_refs[l][...].astype(F32)
                for j in range(3):
                    g = g + g_refs[l][j].astype(F32)
                delta, m_new, v_new = _adamw_math(w_ref[...], g, m_ref[...], v_ref[...])
                g_out[...] = g
                d_out[...] = delta
                m_out[...] = m_new
                v_out[...] = v_new

    shard = pl.BlockSpec((None, tr, cdim), lambda l, i, pos: (l, i, 0))
    p_specs = [pl.BlockSpec((None, tr, cdim), lambda l, i, pos: (2 * pos[0] + pos[1], i, 0))] * n_l
    g_specs = [pl.BlockSpec((3, tr, cdim), lambda l, i, pos: (0, i, 0))] * n_l
    return _call(
        body, name=name, grid=(n_l, r // tr), tables=(place,),
        in_specs=[shard] * 3 + p_specs + g_specs, out_specs=[shard] * 4,
        out_shape=[jax.ShapeDtypeStruct(w.shape, F32)] * 4,
        operands=(w, m, v, *part, *got), semantics=("arbitrary", "parallel"), carry=carry)


def _adamw_small(w, g, m, v, *, name):
    def body(w_ref, g_ref, m_ref, v_ref, d_out, m_out, v_out):
        delta, m_new, v_new = _adamw_math(w_ref[...], g_ref[...], m_ref[...], v_ref[...])
        d_out[...] = delta
        m_out[...] = m_new
        v_out[...] = v_new

    vmem = pl.BlockSpec(memory_space=pltpu.VMEM)
    return pl.pallas_call(
        body, name=name, in_specs=[vmem] * 4, out_specs=[vmem] * 3,
        out_shape=[jax.ShapeDtypeStruct(w.shape, F32)] * 3,
    )(w, g, m, v)


def _pad_lanes(a, width=128):
    return jnp.pad(a, ((0, 0), (0, width - a.shape[1])))


def _train_step(x, target, small, shards, place):
    w_in, w_out, w_mlp_in, w_mlp_out = shards
    depth, d_model, in_shard = w_in.shape
    n_heads = small["forget_b"].shape[1]
    dh = n_heads * HEAD_DIM
    d_ff = w_mlp_in.shape[2] * N_DEV
    row = lambda a: a.reshape(1, -1)
    block = {"in": w_in.astype(BF16), "out": w_out.astype(BF16), "mi": w_mlp_in.astype(BF16),
             "mo": w_mlp_out.astype(BF16)}
    whole = {}

    gate_lo, gate_hi = 3 * dh, 3 * dh + n_heads

    def shard_columns(g, lo, hi):
        pieces = []
        while lo < hi:
            p, a = divmod(lo, in_shard)
            b = min(in_shard, a + hi - lo)
            pieces.append(g[p][:, a:b])
            lo += b - a
        return pieces

    def in_columns(d_qkv, d_gate, lo, hi):
        pieces = []
        if lo < gate_lo:
            pieces.append(d_qkv[:, lo:min(hi, gate_lo)])
        if lo < gate_hi and hi > gate_lo:
            pieces.append(d_gate[:, max(lo, gate_lo) - gate_lo:min(hi, gate_hi) - gate_lo])
        if hi > gate_hi:
            pieces.append(d_qkv[:, max(lo, gate_hi) - n_heads:hi - n_heads])
        return pieces

    def unpack(kind, l, g):
        if kind == "in":
            whole["qkv", l] = jnp.concatenate(
                shard_columns(g, 0, gate_lo) + shard_columns(g, gate_hi, N_DEV * in_shard), axis=1)
            whole["f", l] = _pad_lanes(jnp.concatenate(shard_columns(g, gate_lo, gate_hi), axis=1))
        elif kind == "out":
            whole["out", l] = g.reshape(2 * dh, d_model)
        elif kind == "mi":
            whole["mi", l] = g
        else:
            whole["mo", l] = g.reshape(d_ff, d_model)

    half_done = {}

    def gathering(kind, l, part):
        blk = block[kind][l]
        rows = blk.shape[0]
        if part == "all":
            return _gather_task([blk]), lambda got: unpack(kind, l, got[0])
        if part == "first":
            return (_gather_task([blk[:rows // 2]], rows=rows),
                    lambda got: half_done.__setitem__((kind, l), got[0]))
        return (_gather_task([blk[rows // 2:]], row0=rows // 2, rows=rows, into=[half_done.pop((kind, l))]),
                lambda got: unpack(kind, l, got[0]))

    def carried(call, *wanted):
        wanted = [gathering(*w) for w in wanted if w[1] < depth]
        if not wanted:
            return call()
        res, got = call(carry=_join([task for task, _ in wanted]))
        for (_, done), g in zip(wanted, got):
            done([g])
        return res

    (g,) = _run_carried(_gather_task([block["in"][0]]), name="gather_w_in_0")
    unpack("in", 0, g)

    saved = []
    for l in range(depth):
        h1 = _rmsnorm_fwd(x, row(small["norm1_g"][l]), name=f"norm1_fwd_{l}")
        qkv = carried(functools.partial(_matmul, h1, whole["qkv", l], out_dtype=BF16, name=f"qkv_proj_{l}"),
                      *([("out", 0, "all")] if l == 0 else []), ("mi", l, "first"))
        f = _matmul(h1, whole["f", l], name=f"gate_proj_{l}")
        fb = _pad_lanes(row(small["forget_b"][l]))
        c = _gates_fwd(f, fb, name=f"gates_fwd_{l}")
        c_heads = c[:, :n_heads].T
        cq = c_heads[:, None, :]
        ck = jnp.broadcast_to(c_heads[:, :, None] * LOG2E, c_heads.shape + (128,))
        ya, lse_a = carried(functools.partial(_fox_fwd, qkv, cq, ck, n_heads, name=f"fox_fwd_{l}"),
                            ("mi", l, "second"))
        yb, lse_b = carried(functools.partial(_dil_fwd, qkv, small["rel_bias"], n_heads, 3 * n_heads,
                                              name=f"dil_fwd_{l}"), ("mo", l, "first"))
        mixed = _outnorm_fwd(ya, yb, row(small["outnorm_a_g"][l]), row(small["outnorm_b_g"][l]),
                             name=f"outnorm_fwd_{l}")
        x1 = _matmul(mixed, whole["out", l], addend=x, name=f"out_proj_{l}")
        h2 = _rmsnorm_fwd(x1, row(small["norm2_g"][l]), name=f"norm2_fwd_{l}")
        relu, act = carried(functools.partial(_matmul, h2, whole["mi", l], b_chunked=True, out_dtype=BF16,
                                              mode="relu_sq", name=f"mlp_in_{l}"),
                            ("mo", l, "second"), ("out", l + 1, "all"))
        x2 = carried(functools.partial(_matmul, act, whole["mo", l], addend=x1, name=f"mlp_out_{l}"),
                     ("in", l + 1, "all"))
        saved.append((x, h1, qkv, f, fb, cq, ck, ya, lse_a, yb, lse_b, mixed, x1, h2, relu, act))
        x = x2
    dx, dx16, loss_tile, d_final = _loss_head(x, row(small["final_norm_g"]), target, name="loss_head")

    parts = {k: [None] * depth for k in ("in", "out", "mi", "mo")}
    got = {k: [None] * depth for k in ("in", "out", "mi", "mo")}
    small_grads = [None] * depth

    def to_sibling(call, kind, l, grad):
        res, (from_sibling,) = call(carry=_sibling_task([grad]))
        parts[kind][l] = _chip_partials(grad, from_sibling, place, name=f"chip_partials_{kind}_{l}")
        return res

    def to_chips(call, kind, l):
        if l >= depth:
            return call()
        res, (got[kind][l],) = call(carry=_chips_task([parts[kind][l]]))
        return res

    for l in reversed(range(depth)):
        x0, h1, qkv, f, fb, cq, ck, ya, lse_a, yb, lse_b, mixed, x1, h2, relu, act = saved[l]
        du = to_chips(functools.partial(_matmul, dx16, whole["mo", l], tb=True, out_dtype=BF16, mode="mul2",
                                        aux=relu, name=f"mlp_out_dx_{l}"), "in", l + 1)
        dw_mo = _matmul(act, dx16, ta=True, name=f"mlp_out_dw_{l}")
        dw_mo = dw_mo.reshape(N_DEV, d_ff // N_DEV, d_model)
        dh2 = to_sibling(functools.partial(_matmul, du, whole["mi", l], tb=True, b_chunked=True,
                                           name=f"mlp_in_dx_{l}"),
                         "mo", l, dw_mo)
        dw_mi = to_chips(functools.partial(_matmul, h2, du, ta=True, out_chunks=N_DEV,
                                           name=f"mlp_in_dw_{l}"), "mo", l)
        dx1, dx1_16, d_norm2 = _rmsnorm_bwd(x1, row(small["norm2_g"][l]), dh2, dx, name=f"norm2_bwd_{l}")
        dmixed = to_sibling(functools.partial(_matmul, dx1_16, whole["out", l], tb=True, name=f"out_proj_dx_{l}"),
                            "mi", l, dw_mi)
        dw_out = _matmul(mixed, dx1_16, ta=True, name=f"out_proj_dw_{l}")
        dw_out = dw_out.reshape(N_DEV, 2 * dh // N_DEV, d_model)
        dya, dyb, d_ga, d_gb = _outnorm_bwd(ya, yb, row(small["outnorm_a_g"][l]), row(small["outnorm_b_g"][l]),
                                            dmixed, name=f"outnorm_bwd_{l}")
        dqa, dka, dva, dcq, dck = to_chips(functools.partial(
            _fox_bwd, qkv, cq, ck, lse_a, ya, dya, n_heads, name=f"fox_bwd_{l}"), "mi", l)
        dqb, dkb, dvb, d_rb = to_sibling(functools.partial(
            _dil_bwd, qkv, small["rel_bias"], lse_b, yb, dyb, n_heads, 3 * n_heads, name=f"dil_bwd_{l}"),
            "out", l, dw_out)
        dc = _pad_lanes((dcq[:, 0, :] + dck[:, :, 0]).T)
        df, dfb = _gates_bwd(f, fb, dc, name=f"gates_bwd_{l}")
        dqkv = jnp.concatenate([dqa, dka, dva, dqb, dkb, dvb], axis=1)
        dw_qkv = _matmul(h1, dqkv, ta=True, name=f"qkv_proj_dw_{l}")
        dw_f = _matmul(h1, df, ta=True, name=f"gate_proj_dw_{l}")
        dw_in = jnp.stack([jnp.concatenate(in_columns(dw_qkv, dw_f, p * in_shard, (p + 1) * in_shard), axis=1)
                           for p in range(N_DEV)])
        dh1 = _matmul(df, whole["f", l], tb=True, name=f"gate_proj_dx_{l}")
        dh1 = to_chips(functools.partial(_matmul, dqkv, whole["qkv", l], tb=True, addend=dh1,
                                         name=f"qkv_proj_dx_{l}"), "out", l)
        dx, dx16, d_norm1 = to_sibling(functools.partial(_rmsnorm_bwd, x0, row(small["norm1_g"][l]), dh1, dx1,
                                                         name=f"norm1_bwd_{l}"), "in", l, dw_in)
        small_grads[l] = dict(norm1_g=d_norm1[0], forget_b=dfb[0, :n_heads], rel_bias=d_rb[:, 0, :REL_BUCKETS].T,
                              outnorm_a_g=d_ga[0], outnorm_b_g=d_gb[0], norm2_g=d_norm2[0])
    return loss_tile[0, 0], dx, d_final[0], small_grads, parts, got


def _pack_small(parts, rows):
    flat = jnp.concatenate([p.reshape(-1).astype(F32) for p in parts])
    return jnp.pad(flat, (0, rows * 128 - flat.shape[0])).reshape(rows, 128)


def _unpack_small(packed, shapes):
    flat = packed.reshape(-1)
    out, pos = [], 0
    for shp in shapes:
        size = int(np.prod(shp))
        out.append(flat[pos:pos + size].reshape(shp))
        pos += size
    return out


def kernel(x, norm1_g, w_in, forget_b, rel_bias, outnorm_a_g, outnorm_b_g, w_out, norm2_g, w_mlp_in, w_mlp_out, final_norm_g, loss_target, m_norm1_g, m_w_in, m_forget_b, m_rel_bias, m_outnorm_a_g, m_outnorm_b_g, m_w_out, m_norm2_g, m_w_mlp_in, m_w_mlp_out, m_final_norm_g, v_norm1_g, v_w_in, v_forget_b, v_rel_bias, v_outnorm_a_g, v_outnorm_b_g, v_w_out, v_norm2_g, v_w_mlp_in, v_w_mlp_out, v_final_norm_g):
    depth, d_model, in_shard = w_in.shape
    n_heads = forget_b.shape[1]
    assert in_shard * N_DEV == 6 * n_heads * HEAD_DIM + n_heads and x.shape[0] == 1
    place = jnp.stack([lax.axis_index("x"), lax.axis_index("y"), lax.axis_index("c")]).astype(jnp.int32)

    small_names = ["norm1_g", "forget_b", "rel_bias", "outnorm_a_g", "outnorm_b_g", "norm2_g", "final_norm_g"]
    small_w = dict(norm1_g=norm1_g, forget_b=forget_b, rel_bias=rel_bias, outnorm_a_g=outnorm_a_g,
                   outnorm_b_g=outnorm_b_g, norm2_g=norm2_g, final_norm_g=final_norm_g)
    loss_part, dx, d_final, grads, parts, got = _train_step(
        x[0], loss_target[0], small_w, (w_in, w_out, w_mlp_in, w_mlp_out), place)

    small_m = dict(norm1_g=m_norm1_g, forget_b=m_forget_b, rel_bias=m_rel_bias, outnorm_a_g=m_outnorm_a_g,
                   outnorm_b_g=m_outnorm_b_g, norm2_g=m_norm2_g, final_norm_g=m_final_norm_g)
    small_v = dict(norm1_g=v_norm1_g, forget_b=v_forget_b, rel_bias=v_rel_bias, outnorm_a_g=v_outnorm_a_g,
                   outnorm_b_g=v_outnorm_b_g, norm2_g=v_norm2_g, final_norm_g=v_final_norm_g)
    small_g = {k: jnp.stack([g[k] for g in grads]) for k in small_names if k not in ("rel_bias", "final_norm_g")}
    small_g["rel_bias"] = functools.reduce(jnp.add, [g["rel_bias"] for g in grads])
    small_g["final_norm_g"] = d_final
    shapes = [small_w[k].shape for k in small_names]
    total = sum(int(np.prod(s)) for s in shapes) + 1
    rows = -(-total // (8 * 128)) * 8
    packed_g = _pack_small([small_g[k] for k in small_names] + [loss_part], rows)
    packed_g = _all_reduce_small(packed_g, name="reduce_small")
    zero = jnp.zeros((1,), F32)
    packed_w = _pack_small([small_w[k] for k in small_names] + [zero], rows)
    packed_m = _pack_small([small_m[k] for k in small_names] + [zero], rows)
    packed_v = _pack_small([small_v[k] for k in small_names] + [zero + 1.0], rows)
    packed_d, packed_nm, packed_nv = _adamw_small(packed_w, packed_g, packed_m, packed_v, name="adamw_small")
    g_small = dict(zip(small_names, _unpack_small(packed_g, shapes)))
    d_small = dict(zip(small_names, _unpack_small(packed_d, shapes)))
    nm_small = dict(zip(small_names, _unpack_small(packed_nm, shapes)))
    nv_small = dict(zip(small_names, _unpack_small(packed_nv, shapes)))
    loss = packed_g.reshape(-1)[total - 1]

    big_out = {}
    last = parts["in"][0]
    last_halves = [last[:, :last.shape[1] // 2], last[:, last.shape[1] // 2:]]
    last_got = []
    for kind, nm, w, m, v in [("mi", "w_mlp_in", w_mlp_in, m_w_mlp_in, v_w_mlp_in),
                              ("mo", "w_mlp_out", w_mlp_out, m_w_mlp_out, v_w_mlp_out),
                              ("out", "w_out", w_out, m_w_out, v_w_out), ("in", "w_in", w_in, m_w_in, v_w_in)]:
        adamw = functools.partial(_adamw_sharded, w, m, v, parts[kind], got[kind], place, name=f"adamw_{nm}")
        if last_halves:
            big_out[nm], (half,) = adamw(carry=_chips_task([last_halves.pop(0)]))
            last_got.append(half)
            if not last_halves:
                got["in"][0] = jnp.concatenate(last_got, axis=1)
        else:
            big_out[nm] = adamw()

    order = ["norm1_g", "w_in", "forget_b", "rel_bias", "outnorm_a_g", "outnorm_b_g", "w_out", "norm2_g",
             "w_mlp_in", "w_mlp_out", "final_norm_g"]
    pick = lambda k, idx, small: big_out[k][idx] if k in big_out else small[k]
    outs = [loss, dx[None]]
    outs += [pick(k, 0, g_small) for k in order]
    outs += [pick(k, 1, d_small) for k in order]
    outs += [pick(k, 2, nm_small) for k in order]
    outs += [pick(k, 3, nv_small) for k in order]
    return tuple(outs)
```

```python
import functools
import math

import numpy as np
import jax
import jax.numpy as jnp
from jax import lax
from jax.experimental import pallas as pl
from jax.experimental.pallas import tpu as pltpu

F32 = jnp.float32
BF16 = jnp.bfloat16

HEAD_DIM = 128
DIL_PATTERNS = ((128, 1), (512, 4), (2048, 16))
DIL_BLOCK = 128
REL_BUCKETS = 32
REL_MAX_DISTANCE = 2048
NORM_EPS = 1e-6
NEG_INF = -1e30
ADAM_LR = 0.001
ADAM_B1 = 0.9
ADAM_B2 = 0.999
ADAM_EPS = 1e-08
ADAM_WD = 0.01
ADAM_STEP = 10

N_DEV = 8
V7X_VMEM_LIMIT_BYTES = 56 * 1024 * 1024
MESH = pl.DeviceIdType.MESH


def _params(*semantics):
    return pltpu.CompilerParams(dimension_semantics=semantics, vmem_limit_bytes=V7X_VMEM_LIMIT_BYTES)


HBM_SPEC = pl.BlockSpec(memory_space=pltpu.HBM)


class _Carried:
    def __init__(self, operands, out_shape, scratch, start, forward, finish, aliases=None):
        self.operands, self.out_shape, self.scratch = list(operands), list(out_shape), list(scratch)
        self.start, self.forward, self.finish = start, forward, finish
        self.aliases = dict(aliases or {})


def _join(tasks):
    if len(tasks) == 1:
        return tasks[0]
    spans, aliases = [], {}
    i0 = o0 = s0 = 0
    for t in tasks:
        i1, o1, s1 = i0 + len(t.operands), o0 + len(t.out_shape), s0 + len(t.scratch)
        spans.append((slice(i0, i1), slice(o0, o1), slice(s0, s1)))
        aliases.update({i0 + i: o0 + o for i, o in t.aliases.items()})
        i0, o0, s0 = i1, o1, s1

    def phase(which):
        def run(ins, outs, sems):
            for t, (si, so, ss) in zip(tasks, spans):
                getattr(t, which)(ins[si], outs[so], sems[ss])
        return run

    return _Carried(sum((t.operands for t in tasks), []), sum((t.out_shape for t in tasks), []),
                    sum((t.scratch for t in tasks), []), phase("start"), phase("forward"), phase("finish"),
                    aliases)


FORWARD_AT = 0.8


def _call(body, *, name, grid, in_specs, out_specs, out_shape, operands, scratch=(), semantics, carry=None,
          tables=()):
    in_specs, out_specs, out_shape, scratch = list(in_specs), list(out_specs), list(out_shape), list(scratch)
    n_tab = len(tables)

    def run(fn, in_specs, out_specs, out_shape, scratch, operands, semantics, aliases):
        return pl.pallas_call(
            fn, name=name,
            grid_spec=pltpu.PrefetchScalarGridSpec(
                num_scalar_prefetch=n_tab, grid=grid, in_specs=in_specs, out_specs=out_specs,
                scratch_shapes=scratch),
            out_shape=out_shape, input_output_aliases=aliases,
            compiler_params=_params(*semantics))(*tables, *operands)

    if carry is None:
        return run(body, in_specs, out_specs, out_shape, scratch, operands, semantics, {})
    n_in, n_out = len(in_specs), len(out_specs)
    ci, co, cs = len(carry.operands), len(carry.out_shape), len(carry.scratch)
    steps = int(np.prod(grid))
    forward_step = min(int(steps * FORWARD_AT), steps - 1)

    def carrying(*refs):
        tabs, refs = refs[:n_tab], refs[n_tab:]
        main_in, c_in = refs[:n_in], refs[n_in:n_in + ci]
        main_out = refs[n_in + ci:n_in + ci + n_out]
        c_out = refs[n_in + ci + n_out:n_in + ci + n_out + co]
        rest = refs[n_in + ci + n_out + co:]
        main_scr, c_scr = rest[:len(rest) - cs], rest[len(rest) - cs:]
        step = 0
        for axis, extent in enumerate(grid):
            step = step * extent + pl.program_id(axis)

        @pl.when(step == 0)
        def _():
            carry.start(c_in, c_out, c_scr)

        body(*tabs, *main_in, *main_out, *main_scr)

        @pl.when(step == forward_step)
        def _():
            carry.forward(c_in, c_out, c_scr)

        @pl.when(step == steps - 1)
        def _():
            carry.finish(c_in, c_out, c_scr)

    aliases = {n_tab + n_in + i: n_out + o for i, o in carry.aliases.items()}
    res = run(carrying, in_specs + [HBM_SPEC] * ci, out_specs + [HBM_SPEC] * co, out_shape + carry.out_shape,
              scratch + carry.scratch, (*operands, *carry.operands), ["arbitrary"] * len(grid), aliases)
    return res[:n_out], res[n_out:]


def _run_carried(carry, *, name):
    ci, co = len(carry.operands), len(carry.out_shape)

    def body(*refs):
        parts = (refs[:ci], refs[ci:ci + co], refs[ci + co:])
        carry.start(*parts)
        carry.forward(*parts)
        carry.finish(*parts)

    return pl.pallas_call(
        body, name=name, in_specs=[HBM_SPEC] * ci, out_specs=[HBM_SPEC] * co,
        out_shape=carry.out_shape, scratch_shapes=carry.scratch)(*carry.operands)


def _fit(dim, want, unit=128):
    if dim <= want:
        return dim
    t = want - want % unit
    while dim % t:
        t -= unit
    return t


def _matmul(a, b, *, name, ta=False, tb=False, out_dtype=F32, tm=1024, tn=1024, tk=2048,
            addend=None, mode=None, aux=None, out_chunks=None, b_chunked=False, carry=None):
    m_dim, k_dim = (a.shape[1], a.shape[0]) if ta else a.shape
    b_rows, b_cols = (b.shape[1], b.shape[0] * b.shape[2]) if b_chunked else b.shape
    n_dim = b_rows if tb else b_cols
    assert (b_cols if tb else b_rows) == k_dim
    b_chunk = b.shape[2] if b_chunked else b_cols
    tm, tn, tk = _fit(m_dim, tm), _fit(n_dim // (out_chunks or 1), tn), _fit(k_dim, tk)
    k_chunks = 1
    if b_chunked and tb:
        assert tk % b_chunk == 0 and not ta
        k_chunks = tk // b_chunk
    elif b_chunked:
        tn = _fit(b_chunk, tn)
    assert m_dim % tm == 0 and n_dim % tn == 0 and k_dim % tk == 0, (name, a.shape, b.shape)
    nk = k_dim // tk
    a_spec = (pl.BlockSpec((tk, tm), lambda i, j, k: (k, i)) if ta
              else pl.BlockSpec((tm, tk), lambda i, j, k: (i, k)))
    if b_chunked and tb:
        b_spec = pl.BlockSpec((k_chunks, tn, b_chunk), lambda i, j, k: (k, j, 0))
    elif b_chunked:
        per_b = b_chunk // tn
        b_spec = pl.BlockSpec((None, tk, tn), lambda i, j, k: (j // per_b, k, j % per_b))
    else:
        b_spec = (pl.BlockSpec((tn, tk), lambda i, j, k: (j, k)) if tb
                  else pl.BlockSpec((tk, tn), lambda i, j, k: (k, j)))
    mn_spec = pl.BlockSpec((tm, tn), lambda i, j, k: (i, j))
    if out_chunks is None:
        o_spec, o_shape = mn_spec, (m_dim, n_dim)
    else:
        per = n_dim // out_chunks // tn
        assert per * tn * out_chunks == n_dim
        o_spec = pl.BlockSpec((None, tm, tn), lambda i, j, k: (j // per, i, j % per))
        o_shape = (out_chunks, m_dim, n_dim // out_chunks)
    dims = (((0 if ta else 1,), (1 if tb else 0,)), ((), ()))
    n_out = 2 if mode == "relu_sq" else 1
    in_place = mode is None and out_dtype == F32
    in_specs, operands = [a_spec, b_spec], [a, b]
    if addend is not None:
        in_specs.append(mn_spec)
        operands.append(addend)
    if mode == "mul2":
        in_specs.append(mn_spec)
        operands.append(aux)

    def body(*refs):
        a_ref, b_ref = refs[0], refs[1]
        pos = 2
        add_ref = aux_ref = None
        if addend is not None:
            add_ref, pos = refs[pos], pos + 1
        if mode == "mul2":
            aux_ref, pos = refs[pos], pos + 1
        outs = refs[pos:pos + n_out]

        def finish(acc):
            if add_ref is not None:
                acc = acc + add_ref[...].astype(F32)
            if mode == "relu_sq":
                r = jnp.maximum(acc, 0.0)
                outs[0][...] = r.astype(outs[0].dtype)
                outs[1][...] = (r * r).astype(outs[1].dtype)
            elif mode == "mul2":
                outs[0][...] = (acc * (2.0 * aux_ref[...].astype(F32))).astype(outs[0].dtype)
            else:
                outs[0][...] = acc.astype(outs[0].dtype)

        if k_chunks > 1:
            part = functools.reduce(jnp.add, [
                lax.dot_general(a_ref[:, c * b_chunk:(c + 1) * b_chunk].astype(BF16), b_ref[c].astype(BF16), dims,
                                preferred_element_type=F32) for c in range(k_chunks)])
        else:
            b_tile = b_ref[0] if (b_chunked and tb) else b_ref[...]
            part = lax.dot_general(a_ref[...].astype(BF16), b_tile.astype(BF16), dims,
                                   preferred_element_type=F32)
        k = pl.program_id(2)
        if nk == 1:
            finish(part)
        elif in_place:
            @pl.when(k == 0)
            def _():
                finish(part)

            @pl.when(k > 0)
            def _():
                outs[0][...] += part
        else:
            acc_ref = refs[-1]

            @pl.when(k == 0)
            def _():
                acc_ref[...] = part

            @pl.when(k > 0)
            def _():
                acc_ref[...] += part

            @pl.when(k == nk - 1)
            def _():
                finish(acc_ref[...])

    out_shape = [jax.ShapeDtypeStruct(o_shape, out_dtype)] * n_out
    res = _call(
        body, name=name, grid=(m_dim // tm, n_dim // tn, nk),
        in_specs=in_specs, out_specs=[o_spec] * n_out, out_shape=out_shape, operands=operands,
        scratch=[pltpu.VMEM((tm, tn), F32)] if nk > 1 and not in_place else [],
        semantics=("parallel", "parallel", "arbitrary"), carry=carry)
    if carry is not None:
        res, carried = res
        return (res if n_out > 1 else res[0]), carried
    return res if n_out > 1 else res[0]


def _norm_fwd_math(x, g):
    r = lax.rsqrt(jnp.mean(x * x, axis=-1, keepdims=True) + NORM_EPS)
    return (x * r) * g


def _norm_bwd_math(x, g, dy):
    r = lax.rsqrt(jnp.mean(x * x, axis=-1, keepdims=True) + NORM_EPS)
    xh = x * r
    dxh = dy * g
    dx = r * (dxh - xh * jnp.mean(dxh * xh, axis=-1, keepdims=True))
    return dx, jnp.sum(dy * xh, axis=0, keepdims=True)


def _row_tile(rows, want=256):
    t = min(rows, want)
    assert rows % t == 0
    return t


def _rmsnorm_fwd(x, g, *, name):
    s, d = x.shape
    tr = _row_tile(s)

    def body(x_ref, g_ref, h_ref):
        h_ref[...] = _norm_fwd_math(x_ref[...], g_ref[...]).astype(BF16)

    return pl.pallas_call(
        body, name=name, grid=(s // tr,),
        in_specs=[pl.BlockSpec((tr, d), lambda i: (i, 0)), pl.BlockSpec((1, d), lambda i: (0, 0))],
        out_specs=pl.BlockSpec((tr, d), lambda i: (i, 0)),
        out_shape=jax.ShapeDtypeStruct((s, d), BF16),
        compiler_params=_params("parallel"),
    )(x, g)


def _rmsnorm_bwd(x, g, dh, dres, *, name, carry=None):
    s, d = x.shape
    tr = _row_tile(s)

    def body(x_ref, g_ref, dh_ref, dres_ref, dx_ref, dx16_ref, dg_ref):
        dx, dg = _norm_bwd_math(x_ref[...], g_ref[...], dh_ref[...])
        dx = dres_ref[...] + dx
        dx_ref[...] = dx
        dx16_ref[...] = dx.astype(BF16)

        @pl.when(pl.program_id(0) == 0)
        def _():
            dg_ref[...] = dg

        @pl.when(pl.program_id(0) > 0)
        def _():
            dg_ref[...] += dg

    row = pl.BlockSpec((tr, d), lambda i: (i, 0))
    vec = pl.BlockSpec((1, d), lambda i: (0, 0))
    return _call(
        body, name=name, grid=(s // tr,),
        in_specs=[row, vec, row, row], out_specs=[row, row, vec],
        out_shape=[jax.ShapeDtypeStruct((s, d), F32), jax.ShapeDtypeStruct((s, d), BF16),
                   jax.ShapeDtypeStruct((1, d), F32)],
        operands=(x, g, dh, dres), semantics=("arbitrary",), carry=carry)


def _outnorm_fwd(ya, yb, ga, gb, *, name):
    s, da = ya.shape
    db = yb.shape[1]
    tr = _row_tile(s)

    def body(ya_ref, yb_ref, ga_ref, gb_ref, o_ref):
        o_ref[:, :da] = _norm_fwd_math(ya_ref[...], ga_ref[...]).astype(BF16)
        o_ref[:, da:] = _norm_fwd_math(yb_ref[...], gb_ref[...]).astype(BF16)

    return pl.pallas_call(
        body, name=name, grid=(s // tr,),
        in_specs=[pl.BlockSpec((tr, da), lambda i: (i, 0)), pl.BlockSpec((tr, db), lambda i: (i, 0)),
                  pl.BlockSpec((1, da), lambda i: (0, 0)), pl.BlockSpec((1, db), lambda i: (0, 0))],
        out_specs=pl.BlockSpec((tr, da + db), lambda i: (i, 0)),
        out_shape=jax.ShapeDtypeStruct((s, da + db), BF16),
        compiler_params=_params("parallel"),
    )(ya, yb, ga, gb)


def _outnorm_bwd(ya, yb, ga, gb, dmixed, *, name):
    s, da = ya.shape
    db = yb.shape[1]
    tr = _row_tile(s)

    def body(ya_ref, yb_ref, ga_ref, gb_ref, dm_ref, dya_ref, dyb_ref, dga_ref, dgb_ref):
        dxa, dga = _norm_bwd_math(ya_ref[...], ga_ref[...], dm_ref[:, :da])
        dxb, dgb = _norm_bwd_math(yb_ref[...], gb_ref[...], dm_ref[:, da:])
        dya_ref[...] = dxa
        dyb_ref[...] = dxb

        @pl.when(pl.program_id(0) == 0)
        def _():
            dga_ref[...] = dga
            dgb_ref[...] = dgb

        @pl.when(pl.program_id(0) > 0)
        def _():
            dga_ref[...] += dga
            dgb_ref[...] += dgb

    ra = pl.BlockSpec((tr, da), lambda i: (i, 0))
    rb = pl.BlockSpec((tr, db), lambda i: (i, 0))
    va = pl.BlockSpec((1, da), lambda i: (0, 0))
    vb = pl.BlockSpec((1, db), lambda i: (0, 0))
    return pl.pallas_call(
        body, name=name, grid=(s // tr,),
        in_specs=[ra, rb, va, vb, pl.BlockSpec((tr, da + db), lambda i: (i, 0))],
        out_specs=[ra, rb, va, vb],
        out_shape=[jax.ShapeDtypeStruct((s, da), F32), jax.ShapeDtypeStruct((s, db), F32),
                   jax.ShapeDtypeStruct((1, da), F32), jax.ShapeDtypeStruct((1, db), F32)],
        compiler_params=_params("arbitrary"),
    )(ya, yb, ga, gb, dmixed)


def _loss_head(x, g, target, *, name):
    s, d = x.shape
    tr = _row_tile(s)

    def body(x_ref, g_ref, t_ref, dx_ref, dx16_ref, loss_ref, dg_ref):
        xv, gv = x_ref[...], g_ref[...]
        err = _norm_fwd_math(xv, gv) - t_ref[...]
        part = 0.5 * jnp.sum(jnp.mean(err * err, axis=-1, keepdims=True), axis=0, keepdims=True)
        dx, dg = _norm_bwd_math(xv, gv, err * (1.0 / d))
        dx_ref[...] = dx
        dx16_ref[...] = dx.astype(BF16)
        part = jnp.broadcast_to(part, (8, 128))

        @pl.when(pl.program_id(0) == 0)
        def _():
            dg_ref[...] = dg
            loss_ref[...] = part

        @pl.when(pl.program_id(0) > 0)
        def _():
            dg_ref[...] += dg
            loss_ref[...] += part

    row = pl.BlockSpec((tr, d), lambda i: (i, 0))
    vec = pl.BlockSpec((1, d), lambda i: (0, 0))
    return pl.pallas_call(
        body, name=name, grid=(s // tr,),
        in_specs=[row, vec, row],
        out_specs=[row, row, pl.BlockSpec((8, 128), lambda i: (0, 0)), vec],
        out_shape=[jax.ShapeDtypeStruct((s, d), F32), jax.ShapeDtypeStruct((s, d), BF16),
                   jax.ShapeDtypeStruct((8, 128), F32), jax.ShapeDtypeStruct((1, d), F32)],
        compiler_params=_params("arbitrary"),
    )(x, g, target)


def _split3(x):
    hi = x.astype(BF16)
    rem = x - hi.astype(F32)
    mid = rem.astype(BF16)
    lo = (rem - mid.astype(F32)).astype(BF16)
    return hi, mid, lo


def _tri_sum(tri, x):
    hi, mid, lo = _split3(x)
    dot = functools.partial(jnp.dot, preferred_element_type=F32)
    return dot(tri, hi) + dot(tri, mid) + dot(tri, lo)


def _gates_fwd(f, fb, *, name):
    s, w = f.shape
    tb = 128
    nb = s // tb

    def body(f_ref, fb_ref, c_ref, carry):
        @pl.when(pl.program_id(0) == 0)
        def _():
            carry[...] = jnp.zeros_like(carry)

        logf = jax.nn.log_sigmoid(f_ref[...] + fb_ref[...])
        row = lax.broadcasted_iota(jnp.int32, (tb, tb), 0)
        col = lax.broadcasted_iota(jnp.int32, (tb, tb), 1)
        tri = (row >= col).astype(BF16)
        c = _tri_sum(tri, logf) + carry[...]
        c_ref[...] = c
        carry[...] = c[tb - 1:tb, :]

    return pl.pallas_call(
        body, name=name, grid=(nb,),
        in_specs=[pl.BlockSpec((tb, w), lambda i: (i, 0)), pl.BlockSpec((1, w), lambda i: (0, 0))],
        out_specs=pl.BlockSpec((tb, w), lambda i: (i, 0)),
        out_shape=jax.ShapeDtypeStruct((s, w), F32),
        scratch_shapes=[pltpu.VMEM((1, w), F32)],
        compiler_params=_params("arbitrary"),
    )(f, fb)


def _gates_bwd(f, fb, dc, *, name):
    s, w = f.shape
    tb = 128
    nb = s // tb

    def body(f_ref, fb_ref, dc_ref, df_ref, dfb_ref, carry):
        @pl.when(pl.program_id(0) == 0)
        def _():
            carry[...] = jnp.zeros_like(carry)
            dfb_ref[...] = jnp.zeros_like(dfb_ref)

        row = lax.broadcasted_iota(jnp.int32, (tb, tb), 0)
        col = lax.broadcasted_iota(jnp.int32, (tb, tb), 1)
        tri = (row <= col).astype(BF16)
        dlogf = _tri_sum(tri, dc_ref[...]) + carry[...]
        carry[...] = dlogf[0:1, :]
        df = dlogf * jax.nn.sigmoid(-(f_ref[...] + fb_ref[...]))
        df_ref[...] = df
        dfb_ref[...] += jnp.sum(df, axis=0, keepdims=True)

    rev = pl.BlockSpec((tb, w), lambda i: (nb - 1 - i, 0))
    vec = pl.BlockSpec((1, w), lambda i: (0, 0))
    return pl.pallas_call(
        body, name=name, grid=(nb,),
        in_specs=[rev, vec, rev], out_specs=[rev, vec],
        out_shape=[jax.ShapeDtypeStruct((s, w), F32), jax.ShapeDtypeStruct((1, w), F32)],
        scratch_shapes=[pltpu.VMEM((1, w), F32)],
        compiler_params=_params("arbitrary"),
    )(f, fb, dc)


def _nt(a, b):
    return lax.dot_general(a, b, (((1,), (1,)), ((), ())), preferred_element_type=F32)


def _tn(a, b):
    return lax.dot_general(a, b, (((0,), (0,)), ((), ())), preferred_element_type=F32)


def _nn(a, b):
    return jnp.dot(a, b, preferred_element_type=F32)


DIL_GROUP = 4
FOX_TILE = 1024
LOG2E = 1.4426950408889634


def _causal_pairs(nt, key_major):
    if key_major:
        pairs = [(q, k) for k in range(nt) for q in range(k, nt)]
    else:
        pairs = [(q, k) for q in range(nt) for k in range(q + 1)]
    return (jnp.asarray([p[0] for p in pairs], jnp.int32), jnp.asarray([p[1] for p in pairs], jnp.int32))


def _fox_fwd(qkv, c_row, ck_lanes, n_heads, *, name, carry=None):
    s = qkv.shape[0]
    e = HEAD_DIM
    t = min(FOX_TILE, s)
    nt = s // t
    scale2 = e ** -0.5 * LOG2E
    lanes = 128
    q_tab, k_tab = _causal_pairs(nt, key_major=False)

    def body(q_tab, k_tab, q_ref, k_ref, v_ref, cq_ref, ck_ref, o_ref, lse_ref,
             m_scr, l_scr, acc_scr, s_scr, p_scr):
        pair = pl.program_id(1)
        qi, ki = q_tab[pair], k_tab[pair]

        @pl.when(ki == 0)
        def _():
            m_scr[...] = jnp.full_like(m_scr, NEG_INF)
            l_scr[...] = jnp.zeros_like(l_scr)
            acc_scr[...] = jnp.zeros_like(acc_scr)

        def update(diagonal):
            s_scr[...] = _nt(k_ref[...], q_ref[...])
            ck2 = ck_ref[...]
            for c0 in range(0, t, lanes):
                cols = pl.ds(c0, lanes)

                def logits2():
                    x = s_scr[:, cols] * scale2 - ck2
                    if diagonal:
                        key = lax.broadcasted_iota(jnp.int32, x.shape, 0)
                        qry = c0 + lax.broadcasted_iota(jnp.int32, x.shape, 1)
                        x = jnp.where(key <= qry, x, NEG_INF)
                    return x

                m_old = m_scr[:, cols]
                m_new = jnp.maximum(m_old, jnp.max(logits2(), axis=0, keepdims=True))
                p = jnp.exp2(logits2() - m_new)
                alpha = jnp.exp2(m_old - m_new)
                l_scr[:, cols] = alpha * l_scr[:, cols] + jnp.sum(p, axis=0, keepdims=True)
                m_scr[:, cols] = m_new
                acc_scr[:, cols] = alpha * acc_scr[:, cols]
                p_scr[:, cols] = p.astype(BF16)
            acc_scr[...] += _tn(v_ref[...], p_scr[...])

        @pl.when(ki < qi)
        def _():
            update(False)

        @pl.when(ki == qi)
        def _():
            update(True)
            o_ref[...] = (acc_scr[...] / l_scr[...]).T
            lse_ref[...] = (m_scr[...] + jnp.log2(l_scr[...])) * (1.0 / LOG2E) + cq_ref[...]

    h_ = n_heads
    return _call(
        body, name=name, grid=(h_, int(q_tab.shape[0])), tables=(q_tab, k_tab),
        in_specs=[
            pl.BlockSpec((t, e), lambda h, p, qt, kt: (qt[p], h)),
            pl.BlockSpec((t, e), lambda h, p, qt, kt: (kt[p], h_ + h)),
            pl.BlockSpec((t, e), lambda h, p, qt, kt: (kt[p], 2 * h_ + h)),
            pl.BlockSpec((None, 1, t), lambda h, p, qt, kt: (h, 0, qt[p])),
            pl.BlockSpec((None, t, lanes), lambda h, p, qt, kt: (h, kt[p], 0)),
        ],
        out_specs=[pl.BlockSpec((t, e), lambda h, p, qt, kt: (qt[p], h)),
                   pl.BlockSpec((None, 1, t), lambda h, p, qt, kt: (h, 0, qt[p]))],
        out_shape=[jax.ShapeDtypeStruct((s, h_ * e), F32), jax.ShapeDtypeStruct((h_, 1, s), F32)],
        operands=(qkv, qkv, qkv, c_row, ck_lanes),
        scratch=[pltpu.VMEM((1, t), F32), pltpu.VMEM((1, t), F32), pltpu.VMEM((e, t), F32),
                 pltpu.VMEM((t, t), F32), pltpu.VMEM((t, t), BF16)],
        semantics=("parallel", "arbitrary"), carry=carry)


def _fox_bwd(qkv, c_row, ck_lanes, lse, y, dy, n_heads, *, name, carry=None):
    s = qkv.shape[0]
    e = HEAD_DIM
    t = min(FOX_TILE, s)
    nt = s // t
    scale = e ** -0.5
    scale2 = scale * LOG2E
    lanes = 128
    q_tab, k_tab = _causal_pairs(nt, key_major=False)
    n_pairs = int(q_tab.shape[0])

    def body(q_tab, k_tab, q_ref, k_ref, v_ref, c_ref, ck_ref, lse_ref, y_ref, dy_ref,
             dq_ref, dk_ref, dv_ref, dcq_ref, dck_ref,
             dq_scr, dk_scr, dv_scr, dck_scr, s_scr, dp_scr, p_scr, ds_scr, do_scr, delta_scr, shift_scr):
        pair = pl.program_id(1)
        qi, ki = q_tab[pair], k_tab[pair]

        @pl.when(pair == 0)
        def _():
            dk_scr[...] = jnp.zeros_like(dk_scr)
            dv_scr[...] = jnp.zeros_like(dv_scr)
            dck_scr[...] = jnp.zeros_like(dck_scr)

        @pl.when(ki == 0)
        def _():
            do = dy_ref[...]
            delta_scr[...] = lax.dot_general(jnp.ones((8, e), F32), do * y_ref[...], (((1,), (1,)), ((), ())),
                                             precision=lax.Precision.HIGHEST, preferred_element_type=F32)
            shift_scr[...] = (lse_ref[...] - c_ref[...]) * LOG2E
            do_scr[...] = do.astype(BF16)
            dq_scr[...] = jnp.zeros_like(dq_scr)
            dcq_ref[...] = jnp.zeros_like(dcq_ref)

        def update(diagonal):
            s_scr[...] = _nt(k_ref[...], q_ref[...])
            dp_scr[...] = _nt(v_ref[...], do_scr[...])
            ck2 = ck_ref[...]
            k_rows = pl.ds(pl.multiple_of(ki * t, t), t)
            for c0 in range(0, t, lanes):
                cols = pl.ds(c0, lanes)
                x = s_scr[:, cols] * scale2 - ck2
                if diagonal:
                    key = lax.broadcasted_iota(jnp.int32, x.shape, 0)
                    qry = c0 + lax.broadcasted_iota(jnp.int32, x.shape, 1)
                    x = jnp.where(key <= qry, x, NEG_INF)
                p = jnp.exp2(x - shift_scr[:, cols])
                ds = p * (dp_scr[:, cols] - delta_scr[0:1, cols])
                dcq_ref[:, cols] += jnp.sum(ds, axis=0, keepdims=True)
                dck_scr[k_rows, :] += ds
                p_scr[:, cols] = p.astype(BF16)
                ds_scr[:, cols] = ds.astype(BF16)
            dv_scr[k_rows, :] += _nn(p_scr[...], do_scr[...])
            dk_scr[k_rows, :] += scale * _nn(ds_scr[...], q_ref[...])
            dq_scr[...] += scale * _tn(ds_scr[...], k_ref[...])

        @pl.when(ki < qi)
        def _():
            update(False)

        @pl.when(ki == qi)
        def _():
            update(True)
            dq_ref[...] = dq_scr[...].astype(BF16)

        @pl.when(pair == n_pairs - 1)
        def _():
            dk_ref[...] = dk_scr[...].astype(BF16)
            dv_ref[...] = dv_scr[...].astype(BF16)
            dck_ref[...] = -lax.dot_general(jnp.ones((8, lanes), F32), dck_scr[...], (((1,), (1,)), ((), ())),
                                            precision=lax.Precision.HIGHEST, preferred_element_type=F32)[0:1]

    h_ = n_heads
    q_row = pl.BlockSpec((None, 1, t), lambda h, p, qt, kt: (h, 0, qt[p]))
    q_blk = pl.BlockSpec((t, e), lambda h, p, qt, kt: (qt[p], h))
    whole = pl.BlockSpec((s, e), lambda h, p, qt, kt: (0, h))
    return _call(
        body, name=name, grid=(h_, n_pairs), tables=(q_tab, k_tab),
        in_specs=[
            q_blk,
            pl.BlockSpec((t, e), lambda h, p, qt, kt: (kt[p], h_ + h)),
            pl.BlockSpec((t, e), lambda h, p, qt, kt: (kt[p], 2 * h_ + h)),
            q_row,
            pl.BlockSpec((None, t, lanes), lambda h, p, qt, kt: (h, kt[p], 0)),
            q_row, q_blk, q_blk,
        ],
        out_specs=[q_blk, whole, whole, q_row,
                   pl.BlockSpec((None, 1, s), lambda h, p, qt, kt: (h, 0, 0))],
        out_shape=[jax.ShapeDtypeStruct((s, h_ * e), BF16)] * 3
        + [jax.ShapeDtypeStruct((h_, 1, s), F32), jax.ShapeDtypeStruct((h_, 1, s), F32)],
        operands=(qkv, qkv, qkv, c_row, ck_lanes, lse, y, dy),
        scratch=[pltpu.VMEM((t, e), F32), pltpu.VMEM((s, e), F32), pltpu.VMEM((s, e), F32),
                 pltpu.VMEM((s, lanes), F32), pltpu.VMEM((t, t), F32), pltpu.VMEM((t, t), F32), pltpu.VMEM((t, t), BF16),
                 pltpu.VMEM((t, t), BF16), pltpu.VMEM((t, e), BF16), pltpu.VMEM((8, t), F32),
                 pltpu.VMEM((1, t), F32)],
        semantics=("parallel", "arbitrary"), carry=carry)


def _rel_bucket_table(dilation, span):
    dist = np.arange(span + 1, dtype=np.int64) * dilation
    max_exact = REL_BUCKETS // 2
    d = np.maximum(dist.astype(np.float32), np.float32(1.0))
    large = max_exact + (np.log(d / np.float32(max_exact)) / np.float32(math.log(REL_MAX_DISTANCE / max_exact))
                         * np.float32(REL_BUCKETS - max_exact)).astype(np.int32)
    large = np.minimum(large, REL_BUCKETS - 1)
    return np.where(dist < max_exact, dist, large)


def _bucket_bands(dilation, span):
    table = _rel_bucket_table(dilation, span)
    bands = []
    for n, b in enumerate(table):
        if bands and bands[-1][0] == int(b):
            bands[-1][2] = n
        else:
            assert not any(bb[0] == int(b) for bb in bands)
            bands.append([int(b), n, n])
    return [tuple(b) for b in bands]


def _steps_back():
    i = lax.broadcasted_iota(jnp.int32, (DIL_BLOCK, DIL_BLOCK), 0)
    j = lax.broadcasted_iota(jnp.int32, (DIL_BLOCK, DIL_BLOCK), 1)
    return i - j, DIL_BLOCK + i - j


def _bias_tiles(rb_ref, h, bands, span):
    n_cur, n_prev = _steps_back()
    t_cur = jnp.zeros((DIL_BLOCK, DIL_BLOCK), F32)
    t_prev = jnp.zeros((DIL_BLOCK, DIL_BLOCK), F32)
    for b, lo, hi in bands:
        val = rb_ref[b, h]
        t_cur = jnp.where((n_cur >= lo) & (n_cur <= hi), val, t_cur)
        t_prev = jnp.where((n_prev >= lo) & (n_prev <= hi), val, t_prev)
    t_cur = jnp.where(n_cur >= 0, t_cur, NEG_INF)
    t_prev = jnp.where(n_prev <= span, t_prev, NEG_INF)
    return t_cur, t_prev


def _dil_rows(start, dilation):
    return pl.ds(start, DIL_BLOCK, stride=dilation) if dilation > 1 else pl.ds(start, DIL_BLOCK)


def _dil_block_groups(s, dilation, run_group):
    group = dilation * DIL_BLOCK
    per_residue = s // group

    def block(r, n):
        if isinstance(n, int):
            return n * group + r, max(n - 1, 0) * group + r, n == 0
        return (pl.multiple_of(n * group, DIL_BLOCK) + r,
                pl.multiple_of(jnp.maximum(n - 1, 0) * group, DIL_BLOCK) + r, n == 0)

    if per_residue >= DIL_GROUP:
        assert per_residue % DIL_GROUP == 0
        for r in range(dilation):
            def trip(it, carry, r=r):
                run_group([block(r, it * DIL_GROUP + j) for j in range(DIL_GROUP)])
                return carry
            lax.fori_loop(0, per_residue // DIL_GROUP, trip, 0)
    else:
        residues = DIL_GROUP // per_residue
        assert residues * per_residue == DIL_GROUP and dilation % residues == 0
        for r0 in range(0, dilation, residues):
            run_group([block(r, n) for r in range(r0, r0 + residues) for n in range(per_residue)])


def _dil_fwd(qkv, rel_bias, n_heads, col0, *, name, carry=None):
    s = qkv.shape[0]
    e = HEAD_DIM
    scale = e ** -0.5
    n_pat = len(DIL_PATTERNS)
    for window, d in DIL_PATTERNS:
        assert window // d == DIL_BLOCK and s % (d * DIL_BLOCK) == 0
    bands = [_bucket_bands(d, w // d) for w, d in DIL_PATTERNS]

    def body(rb_ref, q_ref, k_ref, v_ref, y_ref, lse_ref, qf, kf, vf, *scr):
        o_scr, l_scr = scr[:n_pat], scr[n_pat:]
        h = pl.program_id(0)
        qf[...] = q_ref[...].astype(F32)
        kf[...] = k_ref[...].astype(F32)
        vf[...] = v_ref[...].astype(F32)
        for pi, (window, d) in enumerate(DIL_PATTERNS):
            t_cur, t_prev = _bias_tiles(rb_ref, h, bands[pi], window // d)

            def run_group(blocks, d=d, pi=pi, t_cur=t_cur, t_prev=t_prev):
                rows = [(_dil_rows(q0, d), _dil_rows(p0, d)) for q0, p0, _ in blocks]
                qb = [qf[cur, :].astype(BF16) for cur, _ in rows]
                s_c = [_nt(q, kf[cur, :].astype(BF16)) for q, (cur, _) in zip(qb, rows)]
                s_p = [_nt(q, kf[prev, :].astype(BF16)) for q, (_, prev) in zip(qb, rows)]
                s_c = [x * scale + t_cur for x in s_c]
                s_p = [x * scale + (t_prev + jnp.where(first, NEG_INF, 0.0)) for x, (_, _, first) in zip(s_p, blocks)]
                m = [jnp.maximum(jnp.max(a, axis=-1, keepdims=True), jnp.max(b, axis=-1, keepdims=True))
                     for a, b in zip(s_c, s_p)]
                p_c = [jnp.exp(a - mm) for a, mm in zip(s_c, m)]
                p_p = [jnp.exp(b - mm) for b, mm in zip(s_p, m)]
                l = [jnp.sum(a, axis=-1, keepdims=True) + jnp.sum(b, axis=-1, keepdims=True)
                     for a, b in zip(p_c, p_p)]
                o = [_nn(a.astype(BF16), vf[cur, :].astype(BF16)) + _nn(b.astype(BF16), vf[prev, :].astype(BF16))
                     for a, b, (cur, prev) in zip(p_c, p_p, rows)]
                for (cur, _), oo, ll, mm in zip(rows, o, l, m):
                    o_scr[pi][cur, :] = oo / ll
                    l_scr[pi][cur, :] = mm + jnp.log(ll)

            _dil_block_groups(s, d, run_group)
        lses = [l_scr[pi][...] for pi in range(n_pat)]
        m = functools.reduce(jnp.maximum, lses)
        ws = [jnp.exp(l - m) for l in lses]
        tot = functools.reduce(jnp.add, ws)
        y = functools.reduce(jnp.add, [w * o_scr[pi][...] for pi, w in enumerate(ws)])
        y_ref[...] = y / tot
        lse_ref[...] = m + jnp.log(tot)

    h_ = n_heads
    return _call(
        body, name=name, grid=(h_,),
        in_specs=[pl.BlockSpec(memory_space=pltpu.SMEM),
                  pl.BlockSpec((s, e), lambda h: (0, col0 + h)),
                  pl.BlockSpec((s, e), lambda h: (0, col0 + h_ + h)),
                  pl.BlockSpec((s, e), lambda h: (0, col0 + 2 * h_ + h))],
        out_specs=[pl.BlockSpec((s, e), lambda h: (0, h)), pl.BlockSpec((None, s, 1), lambda h: (h, 0, 0))],
        out_shape=[jax.ShapeDtypeStruct((s, h_ * e), F32), jax.ShapeDtypeStruct((h_, s, 1), F32)],
        operands=(rel_bias, qkv, qkv, qkv),
        scratch=[pltpu.VMEM((s, e), F32)] * 3 + [pltpu.VMEM((s, e), F32)] * n_pat
        + [pltpu.VMEM((s, 1), F32)] * n_pat,
        semantics=("parallel",), carry=carry)


def _dil_bwd(qkv, rel_bias, lse, y, dy, n_heads, col0, *, name, carry=None):
    s = qkv.shape[0]
    e = HEAD_DIM
    scale = e ** -0.5
    bands = [_bucket_bands(d, w // d) for w, d in DIL_PATTERNS]

    def body(rb_ref, q_ref, k_ref, v_ref, lse_ref, y_ref, dy_ref, dq_ref, dk_ref, dv_ref, drb_ref,
             qf, kf, vf, dqf, dkf, dvf, delta, dt_cur, dt_prev):
        h = pl.program_id(0)
        qf[...] = q_ref[...].astype(F32)
        kf[...] = k_ref[...].astype(F32)
        vf[...] = v_ref[...].astype(F32)
        dqf[...] = jnp.zeros_like(dqf)
        dkf[...] = jnp.zeros_like(dkf)
        dvf[...] = jnp.zeros_like(dvf)
        delta[...] = jnp.sum(dy_ref[...] * y_ref[...], axis=-1, keepdims=True)
        lane = lax.broadcasted_iota(jnp.int32, (1, 128), 1)
        drb = jnp.zeros((1, 128), F32)
        n_cur, n_prev = _steps_back()
        for pi, (window, d) in enumerate(DIL_PATTERNS):
            t_cur, t_prev = _bias_tiles(rb_ref, h, bands[pi], window // d)
            dt_cur[...] = jnp.zeros_like(dt_cur)
            dt_prev[...] = jnp.zeros_like(dt_prev)

            def run_group(blocks, d=d, t_cur=t_cur, t_prev=t_prev):
                rows = [(_dil_rows(q0, d), _dil_rows(p0, d)) for q0, p0, _ in blocks]
                qb = [qf[cur, :].astype(BF16) for cur, _ in rows]
                kc = [kf[cur, :].astype(BF16) for cur, _ in rows]
                kp = [kf[prev, :].astype(BF16) for _, prev in rows]
                vc = [vf[cur, :].astype(BF16) for cur, _ in rows]
                vp = [vf[prev, :].astype(BF16) for _, prev in rows]
                do = [dy_ref[cur, :].astype(BF16) for cur, _ in rows]
                lse_b = [lse_ref[cur, :] for cur, _ in rows]
                delta_b = [delta[cur, :] for cur, _ in rows]
                s_c = [_nt(q, k) for q, k in zip(qb, kc)]
                s_p = [_nt(q, k) for q, k in zip(qb, kp)]
                dp_c = [_nt(g, v) for g, v in zip(do, vc)]
                dp_p = [_nt(g, v) for g, v in zip(do, vp)]
                p_c = [jnp.exp(x * scale + t_cur - ls) for x, ls in zip(s_c, lse_b)]
                p_p = [jnp.exp(x * scale + (t_prev + jnp.where(first, NEG_INF, 0.0)) - ls)
                       for x, ls, (_, _, first) in zip(s_p, lse_b, blocks)]
                ds_c = [p * (g - dl) for p, g, dl in zip(p_c, dp_c, delta_b)]
                ds_p = [p * (g - dl) for p, g, dl in zip(p_p, dp_p, delta_b)]
                dt_cur[...] += functools.reduce(jnp.add, ds_c)
                dt_prev[...] += functools.reduce(jnp.add, ds_p)
                p_c, p_p = [p.astype(BF16) for p in p_c], [p.astype(BF16) for p in p_p]
                ds_c, ds_p = [x.astype(BF16) for x in ds_c], [x.astype(BF16) for x in ds_p]
                dq = [scale * (_nn(a, k1) + _nn(b, k2)) for a, b, k1, k2 in zip(ds_c, ds_p, kc, kp)]
                dk_c = [scale * _tn(a, q) for a, q in zip(ds_c, qb)]
                dv_c = [_tn(p, g) for p, g in zip(p_c, do)]
                dk_p = [scale * _tn(b, q) for b, q in zip(ds_p, qb)]
                dv_p = [_tn(p, g) for p, g in zip(p_p, do)]
                for i, (cur, prev) in enumerate(rows):
                    dqf[cur, :] += dq[i]
                    dkf[cur, :] += dk_c[i]
                    dvf[cur, :] += dv_c[i]
                    dkf[prev, :] += dk_p[i]
                    dvf[prev, :] += dv_p[i]

            _dil_block_groups(s, d, run_group)
            dtc, dtp = dt_cur[...], dt_prev[...]
            for b, lo, hi in bands[pi]:
                tot = (jnp.sum(jnp.where((n_cur >= lo) & (n_cur <= hi), dtc, 0.0))
                       + jnp.sum(jnp.where((n_prev >= lo) & (n_prev <= hi), dtp, 0.0)))
                drb = drb + jnp.where(lane == b, tot, 0.0)
        dq_ref[...] = dqf[...].astype(BF16)
        dk_ref[...] = dkf[...].astype(BF16)
        dv_ref[...] = dvf[...].astype(BF16)
        drb_ref[...] = drb

    h_ = n_heads
    col = pl.BlockSpec((s, e), lambda h: (0, h))
    return _call(
        body, name=name, grid=(h_,),
        in_specs=[pl.BlockSpec(memory_space=pltpu.SMEM),
                  pl.BlockSpec((s, e), lambda h: (0, col0 + h)),
                  pl.BlockSpec((s, e), lambda h: (0, col0 + h_ + h)),
                  pl.BlockSpec((s, e), lambda h: (0, col0 + 2 * h_ + h)),
                  pl.BlockSpec((None, s, 1), lambda h: (h, 0, 0)), col, col],
        out_specs=[col, col, col, pl.BlockSpec((None, 1, 128), lambda h: (h, 0, 0))],
        out_shape=[jax.ShapeDtypeStruct((s, h_ * e), BF16)] * 3 + [jax.ShapeDtypeStruct((h_, 1, 128), F32)],
        operands=(rel_bias, qkv, qkv, qkv, lse, y, dy),
        scratch=[pltpu.VMEM((s, e), F32)] * 6 + [pltpu.VMEM((s, 1), F32)]
        + [pltpu.VMEM((DIL_BLOCK, DIL_BLOCK), F32)] * 2,
        semantics=("parallel",), carry=carry)


def _place():
    x, y, c = lax.axis_index("x"), lax.axis_index("y"), lax.axis_index("c")
    chips = [(1 - x, y), (x, 1 - y), (1 - x, 1 - y)]
    return x, y, c, chips


def _block_index(px, py, pc):
    return 4 * px + 2 * py + pc


def _gather_task(blocks, *, row0=0, rows=None, into=None):
    n = len(blocks)
    part = blocks[0].shape[0]
    rows = rows or part

    def copies(ins, outs, sems):
        send_sems, recv_sems, local_sems = sems
        x, y, c, chips = _place()
        me, sibling = (x, y, c), (x, y, 1 - c)

        def place_of(a, block):
            slot = outs[a].at[_block_index(*block)]
            return slot if part == rows else slot.at[pl.ds(row0, part)]

        def copy(a, k, block, to, src=None):
            slot = place_of(a, block)
            return pltpu.make_async_remote_copy(
                src_ref=slot if src is None else src, dst_ref=slot,
                send_sem=send_sems.at[a, k], recv_sem=recv_sems.at[a, k],
                device_id=to, device_id_type=MESH)

        local = lambda a: pltpu.make_async_copy(ins[a], place_of(a, me), local_sems.at[a])
        return copy, local, me, sibling, c, chips

    def start(ins, outs, sems):
        copy, local, me, sibling, c, chips = copies(ins, outs, sems)
        for a in range(n):
            local(a).start()
            copy(a, 0, me, sibling, src=ins[a]).start()
            for j, chip in enumerate(chips):
                copy(a, 1 + j, me, (*chip, c), src=ins[a]).start()

    def forward(ins, outs, sems):
        copy, local, me, sibling, c, chips = copies(ins, outs, sems)
        for a in range(n):
            for j, chip in enumerate(chips):
                copy(a, 1 + j, (*chip, c), me).wait_recv()
                copy(a, 4 + j, (*chip, c), sibling).start()

    def finish(ins, outs, sems):
        copy, local, me, sibling, c, chips = copies(ins, outs, sems)
        for a in range(n):
            copy(a, 0, sibling, me).wait_recv()
            for j, chip in enumerate(chips):
                copy(a, 4 + j, (*chip, 1 - c), me).wait_recv()
        for a in range(n):
            for k in range(7):
                copy(a, k, me, sibling, src=ins[a]).wait_send()
            local(a).wait()

    return _Carried(
        list(blocks) + list(into or []),
        [jax.ShapeDtypeStruct((N_DEV, rows) + b.shape[1:], b.dtype) for b in blocks],
        [pltpu.SemaphoreType.DMA((n, 7)), pltpu.SemaphoreType.DMA((n, 7)), pltpu.SemaphoreType.DMA((n,))],
        start, forward, finish, aliases={n + a: a for a in range(n)} if into else None)


def _exchange_task(arrays, n_slots, route):
    n = len(arrays)

    def copies(ins, outs, sems):
        send_sems, recv_sems = sems
        out = []
        for a in range(n):
            for j in range(n_slots):
                src, to = route(j)
                out.append(pltpu.make_async_remote_copy(
                    src_ref=ins[a].at[src], dst_ref=outs[a].at[j],
                    send_sem=send_sems.at[a, j], recv_sem=recv_sems.at[a, j],
                    device_id=to, device_id_type=MESH))
        return out

    def start(ins, outs, sems):
        for cp in copies(ins, outs, sems):
            cp.start()

    def finish(ins, outs, sems):
        for cp in copies(ins, outs, sems):
            cp.wait()

    return _Carried(
        arrays, [jax.ShapeDtypeStruct((n_slots,) + g.shape[1:], g.dtype) for g in arrays],
        [pltpu.SemaphoreType.DMA((n, n_slots)), pltpu.SemaphoreType.DMA((n, n_slots))],
        start, lambda ins, outs, sems: None, finish)


def _sibling_task(grads):
    def route(q):
        x, y, c, _ = _place()
        return 2 * q + (1 - c), (x, y, 1 - c)
    return _exchange_task(grads, 4, route)


def _chips_task(parts):
    def route(j):
        x, y, c, chips = _place()
        px, py = chips[j]
        return 2 * px + py, (px, py, c)
    return _exchange_task(parts, 3, route)


def _chip_partials(grad, got, place, *, name):
    _, r, cdim = grad.shape
    tr = _row_tile(r, 512)

    def body(place_ref, g_ref, s_ref, o_ref):
        o_ref[...] = (g_ref[...] + s_ref[...]).astype(BF16)

    return pl.pallas_call(
        body, name=name,
        grid_spec=pltpu.PrefetchScalarGridSpec(
            num_scalar_prefetch=1, grid=(4, r // tr),
            in_specs=[pl.BlockSpec((None, tr, cdim), lambda q, i, pos: (2 * q + pos[2], i, 0)),
                      pl.BlockSpec((None, tr, cdim), lambda q, i, pos: (q, i, 0))],
            out_specs=pl.BlockSpec((None, tr, cdim), lambda q, i, pos: (q, i, 0))),
        out_shape=jax.ShapeDtypeStruct((4, r, cdim), BF16),
        compiler_params=_params("parallel", "parallel"),
    )(place, grad, got)


def _all_reduce_small(v, *, name):
    r, w = v.shape

    def body(v_ref, o_ref, buf, send_sems, recv_sems):
        x, y, c, _ = _place()
        me = _block_index(x, y, c)
        buf[me] = v_ref[...]
        copies = []
        for k in range(1, N_DEV):
            fx, fy, fc = (k >> 2) & 1, (k >> 1) & 1, k & 1
            peer = (x ^ fx, y ^ fy, c ^ fc)
            cp = pltpu.make_async_remote_copy(
                src_ref=v_ref, dst_ref=buf.at[me],
                send_sem=send_sems.at[k - 1], recv_sem=recv_sems.at[k - 1],
                device_id=peer, device_id_type=MESH)
            cp.start()
            copies.append(cp)
        for cp in copies:
            cp.wait()
        acc = buf[0]
        for p in range(1, N_DEV):
            acc = acc + buf[p]
        o_ref[...] = acc

    vmem = pl.BlockSpec(memory_space=pltpu.VMEM)
    return pl.pallas_call(
        body, name=name, in_specs=[vmem], out_specs=vmem,
        out_shape=jax.ShapeDtypeStruct((r, w), F32),
        scratch_shapes=[pltpu.VMEM((N_DEV, r, w), F32), pltpu.SemaphoreType.DMA((N_DEV - 1,)),
                        pltpu.SemaphoreType.DMA((N_DEV - 1,))],
    )(v)


def _adamw_math(w, g, m, v):
    m = ADAM_B1 * m + (1.0 - ADAM_B1) * g
    v = ADAM_B2 * v + (1.0 - ADAM_B2) * (g * g)
    m_hat = m / (1.0 - ADAM_B1 ** ADAM_STEP)
    v_hat = v / (1.0 - ADAM_B2 ** ADAM_STEP)
    delta = -ADAM_LR * (m_hat / (jnp.sqrt(v_hat) + ADAM_EPS) + ADAM_WD * w)
    return delta, m, v


def _adamw_sharded(w, m, v, part, got, place, *, name, carry=None):
    n_l, r, cdim = w.shape
    tr = _row_tile(r, 256)

    def body(place_ref, *refs):
        w_ref, m_ref, v_ref = refs[:3]
        p_refs, g_refs = refs[3:3 + n_l], refs[3 + n_l:3 + 2 * n_l]
        g_out, d_out, m_out, v_out = refs[3 + 2 * n_l:]
        for l in range(n_l):
            @pl.when(pl.program_id(0) == l)
            def _(l=l):
                g = p_refs[l][...].astype(F32)
                for j in range(3):
                    g = g + g_refs[l][j].astype(F32)
                delta, m_new, v_new = _adamw_math(w_ref[...], g, m_ref[...], v_ref[...])
                g_out[...] = g
                d_out[...] = delta
                m_out[...] = m_new
                v_out[...] = v_new

    shard = pl.BlockSpec((None, tr, cdim), lambda l, i, pos: (l, i, 0))
    p_specs = [pl.BlockSpec((None, tr, cdim), lambda l, i, pos: (2 * pos[0] + pos[1], i, 0))] * n_l
    g_specs = [pl.BlockSpec((3, tr, cdim), lambda l, i, pos: (0, i, 0))] * n_l
    return _call(
        body, name=name, grid=(n_l, r // tr), tables=(place,),
        in_specs=[shard] * 3 + p_specs + g_specs, out_specs=[shard] * 4,
        out_shape=[jax.ShapeDtypeStruct(w.shape, F32)] * 4,
        operands=(w, m, v, *part, *got), semantics=("arbitrary", "parallel"), carry=carry)


def _adamw_small(w, g, m, v, *, name):
    def body(w_ref, g_ref, m_ref, v_ref, d_out, m_out, v_out):
        delta, m_new, v_new = _adamw_math(w_ref[...], g_ref[...], m_ref[...], v_ref[...])
        d_out[...] = delta
        m_out[...] = m_new
        v_out[...] = v_new

    vmem = pl.BlockSpec(memory_space=pltpu.VMEM)
    return pl.pallas_call(
        body, name=name, in_specs=[vmem] * 4, out_specs=[vmem] * 3,
        out_shape=[jax.ShapeDtypeStruct(w.shape, F32)] * 3,
    )(w, g, m, v)


def _pad_lanes(a, width=128):
    return jnp.pad(a, ((0, 0), (0, width - a.shape[1])))


def _train_step(x, target, small, shards, place):
    w_in, w_out, w_mlp_in, w_mlp_out = shards
    depth, d_model, in_shard = w_in.shape
    n_heads = small["forget_b"].shape[1]
    dh = n_heads * HEAD_DIM
    d_ff = w_mlp_in.shape[2] * N_DEV
    row = lambda a: a.reshape(1, -1)
    block = {"in": w_in.astype(BF16), "out": w_out.astype(BF16), "mi": w_mlp_in.astype(BF16),
             "mo": w_mlp_out.astype(BF16)}
    whole = {}

    gate_lo, gate_hi = 3 * dh, 3 * dh + n_heads

    def shard_columns(g, lo, hi):
        pieces = []
        while lo < hi:
            p, a = divmod(lo, in_shard)
            b = min(in_shard, a + hi - lo)
            pieces.append(g[p][:, a:b])
            lo += b - a
        return pieces

    def in_columns(d_qkv, d_gate, lo, hi):
        pieces = []
        if lo < gate_lo:
            pieces.append(d_qkv[:, lo:min(hi, gate_lo)])
        if lo < gate_hi and hi > gate_lo:
            pieces.append(d_gate[:, max(lo, gate_lo) - gate_lo:min(hi, gate_hi) - gate_lo])
        if hi > gate_hi:
            pieces.append(d_qkv[:, max(lo, gate_hi) - n_heads:hi - n_heads])
        return pieces

    def unpack(kind, l, g):
        if kind == "in":
            whole["qkv", l] = jnp.concatenate(
                shard_columns(g, 0, gate_lo) + shard_columns(g, gate_hi, N_DEV * in_shard), axis=1)
            whole["f", l] = _pad_lanes(jnp.concatenate(shard_columns(g, gate_lo, gate_hi), axis=1))
        elif kind == "out":
            whole["out", l] = g.reshape(2 * dh, d_model)
        elif kind == "mi":
            whole["mi", l] = g
        else:
            whole["mo", l] = g.reshape(d_ff, d_model)

    half_done = {}

    def gathering(kind, l, part):
        blk = block[kind][l]
        rows = blk.shape[0]
        if part == "all":
            return _gather_task([blk]), lambda got: unpack(kind, l, got[0])
        if part == "first":
            return (_gather_task([blk[:rows // 2]], rows=rows),
                    lambda got: half_done.__setitem__((kind, l), got[0]))
        return (_gather_task([blk[rows // 2:]], row0=rows // 2, rows=rows, into=[half_done.pop((kind, l))]),
                lambda got: unpack(kind, l, got[0]))

    def carried(call, *wanted):
        wanted = [gathering(*w) for w in wanted if w[1] < depth]
        if not wanted:
            return call()
        res, got = call(carry=_join([task for task, _ in wanted]))
        for (_, done), g in zip(wanted, got):
            done([g])
        return res

    (g,) = _run_carried(_gather_task([block["in"][0]]), name="gather_w_in_0")
    unpack("in", 0, g)

    saved = []
    for l in range(depth):
        h1 = _rmsnorm_fwd(x, row(small["norm1_g"][l]), name=f"norm1_fwd_{l}")
        qkv = carried(functools.partial(_matmul, h1, whole["qkv", l], out_dtype=BF16, name=f"qkv_proj_{l}"),
                      *([("out", 0, "all")] if l == 0 else []), ("mi", l, "first"))
        f = _matmul(h1, whole["f", l], name=f"gate_proj_{l}")
        fb = _pad_lanes(row(small["forget_b"][l]))
        c = _gates_fwd(f, fb, name=f"gates_fwd_{l}")
        c_heads = c[:, :n_heads].T
        cq = c_heads[:, None, :]
        ck = jnp.broadcast_to(c_heads[:, :, None] * LOG2E, c_heads.shape + (128,))
        ya, lse_a = carried(functools.partial(_fox_fwd, qkv, cq, ck, n_heads, name=f"fox_fwd_{l}"),
                            ("mi", l, "second"))
        yb, lse_b = carried(functools.partial(_dil_fwd, qkv, small["rel_bias"], n_heads, 3 * n_heads,
                                              name=f"dil_fwd_{l}"), ("mo", l, "first"))
        mixed = _outnorm_fwd(ya, yb, row(small["outnorm_a_g"][l]), row(small["outnorm_b_g"][l]),
                             name=f"outnorm_fwd_{l}")
        x1 = _matmul(mixed, whole["out", l], addend=x, name=f"out_proj_{l}")
        h2 = _rmsnorm_fwd(x1, row(small["norm2_g"][l]), name=f"norm2_fwd_{l}")
        relu, act = carried(functools.partial(_matmul, h2, whole["mi", l], b_chunked=True, out_dtype=BF16,
                                              mode="relu_sq", name=f"mlp_in_{l}"),
                            ("mo", l, "second"), ("out", l + 1, "all"))
        x2 = carried(functools.partial(_matmul, act, whole["mo", l], addend=x1, name=f"mlp_out_{l}"),
                     ("in", l + 1, "all"))
        saved.append((x, h1, qkv, f, fb, cq, ck, ya, lse_a, yb, lse_b, mixed, x1, h2, relu, act))
        x = x2
    dx, dx16, loss_tile, d_final = _loss_head(x, row(small["final_norm_g"]), target, name="loss_head")

    parts = {k: [None] * depth for k in ("in", "out", "mi", "mo")}
    got = {k: [None] * depth for k in ("in", "out", "mi", "mo")}
    small_grads = [None] * depth

    def to_sibling(call, kind, l, grad):
        res, (from_sibling,) = call(carry=_sibling_task([grad]))
        parts[kind][l] = _chip_partials(grad, from_sibling, place, name=f"chip_partials_{kind}_{l}")
        return res

    def to_chips(call, kind, l):
        if l >= depth:
            return call()
        res, (got[kind][l],) = call(carry=_chips_task([parts[kind][l]]))
        return res

    for l in reversed(range(depth)):
        x0, h1, qkv, f, fb, cq, ck, ya, lse_a, yb, lse_b, mixed, x1, h2, relu, act = saved[l]
        du = to_chips(functools.partial(_matmul, dx16, whole["mo", l], tb=True, out_dtype=BF16, mode="mul2",
                                        aux=relu, name=f"mlp_out_dx_{l}"), "in", l + 1)
        dw_mo = _matmul(act, dx16, ta=True, name=f"mlp_out_dw_{l}")
        dw_mo = dw_mo.reshape(N_DEV, d_ff // N_DEV, d_model)
        dh2 = to_sibling(functools.partial(_matmul, du, whole["mi", l], tb=True, b_chunked=True,
                                           name=f"mlp_in_dx_{l}"),
                         "mo", l, dw_mo)
        dw_mi = to_chips(functools.partial(_matmul, h2, du, ta=True, out_chunks=N_DEV,
                                           name=f"mlp_in_dw_{l}"), "mo", l)
        dx1, dx1_16, d_norm2 = _rmsnorm_bwd(x1, row(small["norm2_g"][l]), dh2, dx, name=f"norm2_bwd_{l}")
        dmixed = to_sibling(functools.partial(_matmul, dx1_16, whole["out", l], tb=True, name=f"out_proj_dx_{l}"),
                            "mi", l, dw_mi)
        dw_out = _matmul(mixed, dx1_16, ta=True, name=f"out_proj_dw_{l}")
        dw_out = dw_out.reshape(N_DEV, 2 * dh // N_DEV, d_model)
        dya, dyb, d_ga, d_gb = _outnorm_bwd(ya, yb, row(small["outnorm_a_g"][l]), row(small["outnorm_b_g"][l]),
                                            dmixed, name=f"outnorm_bwd_{l}")
        dqa, dka, dva, dcq, dck = to_chips(functools.partial(
            _fox_bwd, qkv, cq, ck, lse_a, ya, dya, n_heads, name=f"fox_bwd_{l}"), "mi", l)
        dqb, dkb, dvb, d_rb = to_sibling(functools.partial(
            _dil_bwd, qkv, small["rel_bias"], lse_b, yb, dyb, n_heads, 3 * n_heads, name=f"dil_bwd_{l}"),
            "out", l, dw_out)
        dc = _pad_lanes((dcq[:, 0, :] + dck[:, 0, :]).T)
        df, dfb = _gates_bwd(f, fb, dc, name=f"gates_bwd_{l}")
        dqkv = jnp.concatenate([dqa, dka, dva, dqb, dkb, dvb], axis=1)
        dw_qkv = to_chips(functools.partial(_matmul, h1, dqkv, ta=True, name=f"qkv_proj_dw_{l}"), "out", l)
        dw_f = _matmul(h1, df, ta=True, name=f"gate_proj_dw_{l}")
        dw_in = jnp.stack([jnp.concatenate(in_columns(dw_qkv, dw_f, p * in_shard, (p + 1) * in_shard), axis=1)
                           for p in range(N_DEV)])
        dh1 = _matmul(df, whole["f", l], tb=True, name=f"gate_proj_dx_{l}")
        dh1 = to_sibling(functools.partial(_matmul, dqkv, whole["qkv", l], tb=True, addend=dh1,
                                           name=f"qkv_proj_dx_{l}"), "in", l, dw_in)
        dx, dx16, d_norm1 = _rmsnorm_bwd(x0, row(small["norm1_g"][l]), dh1, dx1, name=f"norm1_bwd_{l}")
        small_grads[l] = dict(norm1_g=d_norm1[0], forget_b=dfb[0, :n_heads], rel_bias=d_rb[:, 0, :REL_BUCKETS].T,
                              outnorm_a_g=d_ga[0], outnorm_b_g=d_gb[0], norm2_g=d_norm2[0])
    return loss_tile[0, 0], dx, d_final[0], small_grads, parts, got


def _pack_small(parts, rows):
    flat = jnp.concatenate([p.reshape(-1).astype(F32) for p in parts])
    return jnp.pad(flat, (0, rows * 128 - flat.shape[0])).reshape(rows, 128)


def _unpack_small(packed, shapes):
    flat = packed.reshape(-1)
    out, pos = [], 0
    for shp in shapes:
        size = int(np.prod(shp))
        out.append(flat[pos:pos + size].reshape(shp))
        pos += size
    return out


def kernel(x, norm1_g, w_in, forget_b, rel_bias, outnorm_a_g, outnorm_b_g, w_out, norm2_g, w_mlp_in, w_mlp_out, final_norm_g, loss_target, m_norm1_g, m_w_in, m_forget_b, m_rel_bias, m_outnorm_a_g, m_outnorm_b_g, m_w_out, m_norm2_g, m_w_mlp_in, m_w_mlp_out, m_final_norm_g, v_norm1_g, v_w_in, v_forget_b, v_rel_bias, v_outnorm_a_g, v_outnorm_b_g, v_w_out, v_norm2_g, v_w_mlp_in, v_w_mlp_out, v_final_norm_g):
    depth, d_model, in_shard = w_in.shape
    n_heads = forget_b.shape[1]
    assert in_shard * N_DEV == 6 * n_heads * HEAD_DIM + n_heads and x.shape[0] == 1
    place = jnp.stack([lax.axis_index("x"), lax.axis_index("y"), lax.axis_index("c")]).astype(jnp.int32)

    small_names = ["norm1_g", "forget_b", "rel_bias", "outnorm_a_g", "outnorm_b_g", "norm2_g", "final_norm_g"]
    small_w = dict(norm1_g=norm1_g, forget_b=forget_b, rel_bias=rel_bias, outnorm_a_g=outnorm_a_g,
                   outnorm_b_g=outnorm_b_g, norm2_g=norm2_g, final_norm_g=final_norm_g)
    loss_part, dx, d_final, grads, parts, got = _train_step(
        x[0], loss_target[0], small_w, (w_in, w_out, w_mlp_in, w_mlp_out), place)

    small_m = dict(norm1_g=m_norm1_g, forget_b=m_forget_b, rel_bias=m_rel_bias, outnorm_a_g=m_outnorm_a_g,
                   outnorm_b_g=m_outnorm_b_g, norm2_g=m_norm2_g, final_norm_g=m_final_norm_g)
    small_v = dict(norm1_g=v_norm1_g, forget_b=v_forget_b, rel_bias=v_rel_bias, outnorm_a_g=v_outnorm_a_g,
                   outnorm_b_g=v_outnorm_b_g, norm2_g=v_norm2_g, final_norm_g=v_final_norm_g)
    small_g = {k: jnp.stack([g[k] for g in grads]) for k in small_names if k not in ("rel_bias", "final_norm_g")}
    small_g["rel_bias"] = functools.reduce(jnp.add, [g["rel_bias"] for g in grads])
    small_g["final_norm_g"] = d_final
    shapes = [small_w[k].shape for k in small_names]
    total = sum(int(np.prod(s)) for s in shapes) + 1
    rows = -(-total // (8 * 128)) * 8
    packed_g = _pack_small([small_g[k] for k in small_names] + [loss_part], rows)
    packed_g = _all_reduce_small(packed_g, name="reduce_small")
    zero = jnp.zeros((1,), F32)
    packed_w = _pack_small([small_w[k] for k in small_names] + [zero], rows)
    packed_m = _pack_small([small_m[k] for k in small_names] + [zero], rows)
    packed_v = _pack_small([small_v[k] for k in small_names] + [zero + 1.0], rows)
    packed_d, packed_nm, packed_nv = _adamw_small(packed_w, packed_g, packed_m, packed_v, name="adamw_small")
    g_small = dict(zip(small_names, _unpack_small(packed_g, shapes)))
    d_small = dict(zip(small_names, _unpack_small(packed_d, shapes)))
    nm_small = dict(zip(small_names, _unpack_small(packed_nm, shapes)))
    nv_small = dict(zip(small_names, _unpack_small(packed_nv, shapes)))
    loss = packed_g.reshape(-1)[total - 1]

    big_out = {}
    last = parts["in"][0]
    last_halves = [last[:, :last.shape[1] // 2], last[:, last.shape[1] // 2:]]
    last_got = []
    for kind, nm, w, m, v in [("mi", "w_mlp_in", w_mlp_in, m_w_mlp_in, v_w_mlp_in),
                              ("mo", "w_mlp_out", w_mlp_out, m_w_mlp_out, v_w_mlp_out),
                              ("out", "w_out", w_out, m_w_out, v_w_out), ("in", "w_in", w_in, m_w_in, v_w_in)]:
        adamw = functools.partial(_adamw_sharded, w, m, v, parts[kind], got[kind], place, name=f"adamw_{nm}")
        if last_halves:
            big_out[nm], (half,) = adamw(carry=_chips_task([last_halves.pop(0)]))
            last_got.append(half)
            if not last_halves:
                got["in"][0] = jnp.concatenate(last_got, axis=1)
        else:
            big_out[nm] = adamw()

    order = ["norm1_g", "w_in", "forget_b", "rel_bias", "outnorm_a_g", "outnorm_b_g", "w_out", "norm2_g",
             "w_mlp_in", "w_mlp_out", "final_norm_g"]
    pick = lambda k, idx, small: big_out[k][idx] if k in big_out else small[k]
    outs = [loss, dx[None]]
    outs += [pick(k, 0, g_small) for k in order]
    outs += [pick(k, 1, d_small) for k in order]
    outs += [pick(k, 2, nm_small) for k in order]
    outs += [pick(k, 3, nv_small) for k in order]
    return tuple(outs)
```

```python
import functools
import math

import numpy as np
import jax
import jax.numpy as jnp
from jax import lax
from jax.experimental import pallas as pl
from jax.experimental.pallas import tpu as pltpu

F32 = jnp.float32
BF16 = jnp.bfloat16

HEAD_DIM = 128
DIL_PATTERNS = ((128, 1), (512, 4), (2048, 16))
DIL_BLOCK = 128
REL_BUCKETS = 32
REL_MAX_DISTANCE = 2048
NORM_EPS = 1e-6
NEG_INF = -1e30
ADAM_LR = 0.001
ADAM_B1 = 0.9
ADAM_B2 = 0.999
ADAM_EPS = 1e-08
ADAM_WD = 0.01
ADAM_STEP = 10

N_DEV = 8
V7X_VMEM_LIMIT_BYTES = 56 * 1024 * 1024
MESH = pl.DeviceIdType.MESH


def _params(*semantics):
    return pltpu.CompilerParams(dimension_semantics=semantics, vmem_limit_bytes=V7X_VMEM_LIMIT_BYTES)


HBM_SPEC = pl.BlockSpec(memory_space=pltpu.HBM)


class _Carried:
    def __init__(self, operands, out_shape, scratch, start, forward, finish, aliases=None):
        self.operands, self.out_shape, self.scratch = list(operands), list(out_shape), list(scratch)
        self.start, self.forward, self.finish = start, forward, finish
        self.aliases = dict(aliases or {})


def _join(tasks):
    if len(tasks) == 1:
        return tasks[0]
    spans, aliases = [], {}
    i0 = o0 = s0 = 0
    for t in tasks:
        i1, o1, s1 = i0 + len(t.operands), o0 + len(t.out_shape), s0 + len(t.scratch)
        spans.append((slice(i0, i1), slice(o0, o1), slice(s0, s1)))
        aliases.update({i0 + i: o0 + o for i, o in t.aliases.items()})
        i0, o0, s0 = i1, o1, s1

    def phase(which):
        def run(ins, outs, sems):
            for t, (si, so, ss) in zip(tasks, spans):
                getattr(t, which)(ins[si], outs[so], sems[ss])
        return run

    return _Carried(sum((t.operands for t in tasks), []), sum((t.out_shape for t in tasks), []),
                    sum((t.scratch for t in tasks), []), phase("start"), phase("forward"), phase("finish"),
                    aliases)


FORWARD_AT = 0.8


def _call(body, *, name, grid, in_specs, out_specs, out_shape, operands, scratch=(), semantics, carry=None,
          tables=()):
    in_specs, out_specs, out_shape, scratch = list(in_specs), list(out_specs), list(out_shape), list(scratch)
    n_tab = len(tables)

    def run(fn, in_specs, out_specs, out_shape, scratch, operands, semantics, aliases):
        return pl.pallas_call(
            fn, name=name,
            grid_spec=pltpu.PrefetchScalarGridSpec(
                num_scalar_prefetch=n_tab, grid=grid, in_specs=in_specs, out_specs=out_specs,
                scratch_shapes=scratch),
            out_shape=out_shape, input_output_aliases=aliases,
            compiler_params=_params(*semantics))(*tables, *operands)

    if carry is None:
        return run(body, in_specs, out_specs, out_shape, scratch, operands, semantics, {})
    n_in, n_out = len(in_specs), len(out_specs)
    ci, co, cs = len(carry.operands), len(carry.out_shape), len(carry.scratch)
    steps = int(np.prod(grid))
    forward_step = min(int(steps * FORWARD_AT), steps - 1)

    def carrying(*refs):
        tabs, refs = refs[:n_tab], refs[n_tab:]
        main_in, c_in = refs[:n_in], refs[n_in:n_in + ci]
        main_out = refs[n_in + ci:n_in + ci + n_out]
        c_out = refs[n_in + ci + n_out:n_in + ci + n_out + co]
        rest = refs[n_in + ci + n_out + co:]
        main_scr, c_scr = rest[:len(rest) - cs], rest[len(rest) - cs:]
        step = 0
        for axis, extent in enumerate(grid):
            step = step * extent + pl.program_id(axis)

        @pl.when(step == 0)
        def _():
            carry.start(c_in, c_out, c_scr)

        body(*tabs, *main_in, *main_out, *main_scr)

        @pl.when(step == forward_step)
        def _():
            carry.forward(c_in, c_out, c_scr)

        @pl.when(step == steps - 1)
        def _():
            carry.finish(c_in, c_out, c_scr)

    aliases = {n_tab + n_in + i: n_out + o for i, o in carry.aliases.items()}
    res = run(carrying, in_specs + [HBM_SPEC] * ci, out_specs + [HBM_SPEC] * co, out_shape + carry.out_shape,
              scratch + carry.scratch, (*operands, *carry.operands), ["arbitrary"] * len(grid), aliases)
    return res[:n_out], res[n_out:]


def _run_carried(carry, *, name):
    ci, co = len(carry.operands), len(carry.out_shape)

    def body(*refs):
        parts = (refs[:ci], refs[ci:ci + co], refs[ci + co:])
        carry.start(*parts)
        carry.forward(*parts)
        carry.finish(*parts)

    return pl.pallas_call(
        body, name=name, in_specs=[HBM_SPEC] * ci, out_specs=[HBM_SPEC] * co,
        out_shape=carry.out_shape, scratch_shapes=carry.scratch)(*carry.operands)


def _fit(dim, want, unit=128):
    if dim <= want:
        return dim
    t = want - want % unit
    while dim % t:
        t -= unit
    return t


def _matmul(a, b, *, name, ta=False, tb=False, out_dtype=F32, tm=1024, tn=1024, tk=2048,
            addend=None, mode=None, aux=None, out_chunks=None, b_chunked=False, carry=None):
    m_dim, k_dim = (a.shape[1], a.shape[0]) if ta else a.shape
    b_rows, b_cols = (b.shape[1], b.shape[0] * b.shape[2]) if b_chunked else b.shape
    n_dim = b_rows if tb else b_cols
    assert (b_cols if tb else b_rows) == k_dim
    b_chunk = b.shape[2] if b_chunked else b_cols
    tm, tn, tk = _fit(m_dim, tm), _fit(n_dim // (out_chunks or 1), tn), _fit(k_dim, tk)
    k_chunks = 1
    if b_chunked and tb:
        assert tk % b_chunk == 0 and not ta
        k_chunks = tk // b_chunk
    elif b_chunked:
        tn = _fit(b_chunk, tn)
    assert m_dim % tm == 0 and n_dim % tn == 0 and k_dim % tk == 0, (name, a.shape, b.shape)
    nk = k_dim // tk
    a_spec = (pl.BlockSpec((tk, tm), lambda i, j, k: (k, i)) if ta
              else pl.BlockSpec((tm, tk), lambda i, j, k: (i, k)))
    if b_chunked and tb:
        b_spec = pl.BlockSpec((k_chunks, tn, b_chunk), lambda i, j, k: (k, j, 0))
    elif b_chunked:
        per_b = b_chunk // tn
        b_spec = pl.BlockSpec((None, tk, tn), lambda i, j, k: (j // per_b, k, j % per_b))
    else:
        b_spec = (pl.BlockSpec((tn, tk), lambda i, j, k: (j, k)) if tb
                  else pl.BlockSpec((tk, tn), lambda i, j, k: (k, j)))
    mn_spec = pl.BlockSpec((tm, tn), lambda i, j, k: (i, j))
    if out_chunks is None:
        o_spec, o_shape = mn_spec, (m_dim, n_dim)
    else:
        per = n_dim // out_chunks // tn
        assert per * tn * out_chunks == n_dim
        o_spec = pl.BlockSpec((None, tm, tn), lambda i, j, k: (j // per, i, j % per))
        o_shape = (out_chunks, m_dim, n_dim // out_chunks)
    dims = (((0 if ta else 1,), (1 if tb else 0,)), ((), ()))
    n_out = 2 if mode == "relu_sq" else 1
    in_place = mode is None and out_dtype == F32
    in_specs, operands = [a_spec, b_spec], [a, b]
    if addend is not None:
        in_specs.append(mn_spec)
        operands.append(addend)
    if mode == "mul2":
        in_specs.append(mn_spec)
        operands.append(aux)

    def body(*refs):
        a_ref, b_ref = refs[0], refs[1]
        pos = 2
        add_ref = aux_ref = None
        if addend is not None:
            add_ref, pos = refs[pos], pos + 1
        if mode == "mul2":
            aux_ref, pos = refs[pos], pos + 1
        outs = refs[pos:pos + n_out]

        def finish(acc):
            if add_ref is not None:
                acc = acc + add_ref[...].astype(F32)
            if mode == "relu_sq":
                r = jnp.maximum(acc, 0.0)
                outs[0][...] = r.astype(outs[0].dtype)
                outs[1][...] = (r * r).astype(outs[1].dtype)
            elif mode == "mul2":
                outs[0][...] = (acc * (2.0 * aux_ref[...].astype(F32))).astype(outs[0].dtype)
            else:
                outs[0][...] = acc.astype(outs[0].dtype)

        if k_chunks > 1:
            part = functools.reduce(jnp.add, [
                lax.dot_general(a_ref[:, c * b_chunk:(c + 1) * b_chunk].astype(BF16), b_ref[c].astype(BF16), dims,
                                preferred_element_type=F32) for c in range(k_chunks)])
        else:
            b_tile = b_ref[0] if (b_chunked and tb) else b_ref[...]
            part = lax.dot_general(a_ref[...].astype(BF16), b_tile.astype(BF16), dims,
                                   preferred_element_type=F32)
        k = pl.program_id(2)
        if nk == 1:
            finish(part)
        elif in_place:
            @pl.when(k == 0)
            def _():
                finish(part)

            @pl.when(k > 0)
            def _():
                outs[0][...] += part
        else:
            acc_ref = refs[-1]

            @pl.when(k == 0)
            def _():
                acc_ref[...] = part

            @pl.when(k > 0)
            def _():
                acc_ref[...] += part

            @pl.when(k == nk - 1)
            def _():
                finish(acc_ref[...])

    out_shape = [jax.ShapeDtypeStruct(o_shape, out_dtype)] * n_out
    res = _call(
        body, name=name, grid=(m_dim // tm, n_dim // tn, nk),
        in_specs=in_specs, out_specs=[o_spec] * n_out, out_shape=out_shape, operands=operands,
        scratch=[pltpu.VMEM((tm, tn), F32)] if nk > 1 and not in_place else [],
        semantics=("parallel", "parallel", "arbitrary"), carry=carry)
    if carry is not None:
        res, carried = res
        return (res if n_out > 1 else res[0]), carried
    return res if n_out > 1 else res[0]


def _norm_fwd_math(x, g):
    r = lax.rsqrt(jnp.mean(x * x, axis=-1, keepdims=True) + NORM_EPS)
    return (x * r) * g


def _norm_bwd_math(x, g, dy):
    r = lax.rsqrt(jnp.mean(x * x, axis=-1, keepdims=True) + NORM_EPS)
    xh = x * r
    dxh = dy * g
    dx = r * (dxh - xh * jnp.mean(dxh * xh, axis=-1, keepdims=True))
    return dx, jnp.sum(dy * xh, axis=0, keepdims=True)


def _row_tile(rows, want=256):
    t = min(rows, want)
    assert rows % t == 0
    return t


def _rmsnorm_fwd(x, g, *, name):
    s, d = x.shape
    tr = _row_tile(s)

    def body(x_ref, g_ref, h_ref):
        h_ref[...] = _norm_fwd_math(x_ref[...], g_ref[...]).astype(BF16)

    return pl.pallas_call(
        body, name=name, grid=(s // tr,),
        in_specs=[pl.BlockSpec((tr, d), lambda i: (i, 0)), pl.BlockSpec((1, d), lambda i: (0, 0))],
        out_specs=pl.BlockSpec((tr, d), lambda i: (i, 0)),
        out_shape=jax.ShapeDtypeStruct((s, d), BF16),
        compiler_params=_params("parallel"),
    )(x, g)


def _rmsnorm_bwd(x, g, dh, dres, *, name, carry=None):
    s, d = x.shape
    tr = _row_tile(s)

    def body(x_ref, g_ref, dh_ref, dres_ref, dx_ref, dx16_ref, dg_ref):
        dx, dg = _norm_bwd_math(x_ref[...], g_ref[...], dh_ref[...])
        dx = dres_ref[...] + dx
        dx_ref[...] = dx
        dx16_ref[...] = dx.astype(BF16)

        @pl.when(pl.program_id(0) == 0)
        def _():
            dg_ref[...] = dg

        @pl.when(pl.program_id(0) > 0)
        def _():
            dg_ref[...] += dg

    row = pl.BlockSpec((tr, d), lambda i: (i, 0))
    vec = pl.BlockSpec((1, d), lambda i: (0, 0))
    return _call(
        body, name=name, grid=(s // tr,),
        in_specs=[row, vec, row, row], out_specs=[row, row, vec],
        out_shape=[jax.ShapeDtypeStruct((s, d), F32), jax.ShapeDtypeStruct((s, d), BF16),
                   jax.ShapeDtypeStruct((1, d), F32)],
        operands=(x, g, dh, dres), semantics=("arbitrary",), carry=carry)


def _outnorm_fwd(ya, yb, ga, gb, *, name):
    s, da = ya.shape
    db = yb.shape[1]
    tr = _row_tile(s)

    def body(ya_ref, yb_ref, ga_ref, gb_ref, o_ref):
        o_ref[:, :da] = _norm_fwd_math(ya_ref[...], ga_ref[...]).astype(BF16)
        o_ref[:, da:] = _norm_fwd_math(yb_ref[...], gb_ref[...]).astype(BF16)

    return pl.pallas_call(
        body, name=name, grid=(s // tr,),
        in_specs=[pl.BlockSpec((tr, da), lambda i: (i, 0)), pl.BlockSpec((tr, db), lambda i: (i, 0)),
                  pl.BlockSpec((1, da), lambda i: (0, 0)), pl.BlockSpec((1, db), lambda i: (0, 0))],
        out_specs=pl.BlockSpec((tr, da + db), lambda i: (i, 0)),
        out_shape=jax.ShapeDtypeStruct((s, da + db), BF16),
        compiler_params=_params("parallel"),
    )(ya, yb, ga, gb)


def _outnorm_bwd(ya, yb, ga, gb, dmixed, *, name):
    s, da = ya.shape
    db = yb.shape[1]
    tr = _row_tile(s)

    def body(ya_ref, yb_ref, ga_ref, gb_ref, dm_ref, dya_ref, dyb_ref, dga_ref, dgb_ref):
        dxa, dga = _norm_bwd_math(ya_ref[...], ga_ref[...], dm_ref[:, :da])
        dxb, dgb = _norm_bwd_math(yb_ref[...], gb_ref[...], dm_ref[:, da:])
        dya_ref[...] = dxa
        dyb_ref[...] = dxb

        @pl.when(pl.program_id(0) == 0)
        def _():
            dga_ref[...] = dga
            dgb_ref[...] = dgb

        @pl.when(pl.program_id(0) > 0)
        def _():
            dga_ref[...] += dga
            dgb_ref[...] += dgb

    ra = pl.BlockSpec((tr, da), lambda i: (i, 0))
    rb = pl.BlockSpec((tr, db), lambda i: (i, 0))
    va = pl.BlockSpec((1, da), lambda i: (0, 0))
    vb = pl.BlockSpec((1, db), lambda i: (0, 0))
    return pl.pallas_call(
        body, name=name, grid=(s // tr,),
        in_specs=[ra, rb, va, vb, pl.BlockSpec((tr, da + db), lambda i: (i, 0))],
        out_specs=[ra, rb, va, vb],
        out_shape=[jax.ShapeDtypeStruct((s, da), F32), jax.ShapeDtypeStruct((s, db), F32),
                   jax.ShapeDtypeStruct((1, da), F32), jax.ShapeDtypeStruct((1, db), F32)],
        compiler_params=_params("arbitrary"),
    )(ya, yb, ga, gb, dmixed)


def _loss_head(x, g, target, *, name):
    s, d = x.shape
    tr = _row_tile(s)

    def body(x_ref, g_ref, t_ref, dx_ref, dx16_ref, loss_ref, dg_ref):
        xv, gv = x_ref[...], g_ref[...]
        err = _norm_fwd_math(xv, gv) - t_ref[...]
        part = 0.5 * jnp.sum(jnp.mean(err * err, axis=-1, keepdims=True), axis=0, keepdims=True)
        dx, dg = _norm_bwd_math(xv, gv, err * (1.0 / d))
        dx_ref[...] = dx
        dx16_ref[...] = dx.astype(BF16)
        part = jnp.broadcast_to(part, (8, 128))

        @pl.when(pl.program_id(0) == 0)
        def _():
            dg_ref[...] = dg
            loss_ref[...] = part

        @pl.when(pl.program_id(0) > 0)
        def _():
            dg_ref[...] += dg
            loss_ref[...] += part

    row = pl.BlockSpec((tr, d), lambda i: (i, 0))
    vec = pl.BlockSpec((1, d), lambda i: (0, 0))
    return pl.pallas_call(
        body, name=name, grid=(s // tr,),
        in_specs=[row, vec, row],
        out_specs=[row, row, pl.BlockSpec((8, 128), lambda i: (0, 0)), vec],
        out_shape=[jax.ShapeDtypeStruct((s, d), F32), jax.ShapeDtypeStruct((s, d), BF16),
                   jax.ShapeDtypeStruct((8, 128), F32), jax.ShapeDtypeStruct((1, d), F32)],
        compiler_params=_params("arbitrary"),
    )(x, g, target)


def _split3(x):
    hi = x.astype(BF16)
    rem = x - hi.astype(F32)
    mid = rem.astype(BF16)
    lo = (rem - mid.astype(F32)).astype(BF16)
    return hi, mid, lo


def _tri_sum(tri, x):
    hi, mid, lo = _split3(x)
    dot = functools.partial(jnp.dot, preferred_element_type=F32)
    return dot(tri, hi) + dot(tri, mid) + dot(tri, lo)


def _gates_fwd(f, fb, *, name):
    s, w = f.shape
    tb = 128
    nb = s // tb

    def body(f_ref, fb_ref, c_ref, carry):
        @pl.when(pl.program_id(0) == 0)
        def _():
            carry[...] = jnp.zeros_like(carry)

        logf = jax.nn.log_sigmoid(f_ref[...] + fb_ref[...])
        row = lax.broadcasted_iota(jnp.int32, (tb, tb), 0)
        col = lax.broadcasted_iota(jnp.int32, (tb, tb), 1)
        tri = (row >= col).astype(BF16)
        c = _tri_sum(tri, logf) + carry[...]
        c_ref[...] = c
        carry[...] = c[tb - 1:tb, :]

    return pl.pallas_call(
        body, name=name, grid=(nb,),
        in_specs=[pl.BlockSpec((tb, w), lambda i: (i, 0)), pl.BlockSpec((1, w), lambda i: (0, 0))],
        out_specs=pl.BlockSpec((tb, w), lambda i: (i, 0)),
        out_shape=jax.ShapeDtypeStruct((s, w), F32),
        scratch_shapes=[pltpu.VMEM((1, w), F32)],
        compiler_params=_params("arbitrary"),
    )(f, fb)


def _gates_bwd(f, fb, dc, *, name):
    s, w = f.shape
    tb = 128
    nb = s // tb

    def body(f_ref, fb_ref, dc_ref, df_ref, dfb_ref, carry):
        @pl.when(pl.program_id(0) == 0)
        def _():
            carry[...] = jnp.zeros_like(carry)
            dfb_ref[...] = jnp.zeros_like(dfb_ref)

        row = lax.broadcasted_iota(jnp.int32, (tb, tb), 0)
        col = lax.broadcasted_iota(jnp.int32, (tb, tb), 1)
        tri = (row <= col).astype(BF16)
        dlogf = _tri_sum(tri, dc_ref[...]) + carry[...]
        carry[...] = dlogf[0:1, :]
        df = dlogf * jax.nn.sigmoid(-(f_ref[...] + fb_ref[...]))
        df_ref[...] = df
        dfb_ref[...] += jnp.sum(df, axis=0, keepdims=True)

    rev = pl.BlockSpec((tb, w), lambda i: (nb - 1 - i, 0))
    vec = pl.BlockSpec((1, w), lambda i: (0, 0))
    return pl.pallas_call(
        body, name=name, grid=(nb,),
        in_specs=[rev, vec, rev], out_specs=[rev, vec],
        out_shape=[jax.ShapeDtypeStruct((s, w), F32), jax.ShapeDtypeStruct((1, w), F32)],
        scratch_shapes=[pltpu.VMEM((1, w), F32)],
        compiler_params=_params("arbitrary"),
    )(f, fb, dc)


def _nt(a, b):
    return lax.dot_general(a, b, (((1,), (1,)), ((), ())), preferred_element_type=F32)


def _tn(a, b):
    return lax.dot_general(a, b, (((0,), (0,)), ((), ())), preferred_element_type=F32)


def _nn(a, b):
    return jnp.dot(a, b, preferred_element_type=F32)


DIL_GROUP = 8
FOX_TILE = 1024
LOG2E = 1.4426950408889634


def _causal_pairs(nt, key_major):
    if key_major:
        pairs = [(q, k) for k in range(nt) for q in range(k, nt)]
    else:
        pairs = [(q, k) for q in range(nt) for k in range(q + 1)]
    return (jnp.asarray([p[0] for p in pairs], jnp.int32), jnp.asarray([p[1] for p in pairs], jnp.int32))


def _fox_fwd(qkv, c_row, ck_lanes, n_heads, *, name, carry=None):
    s = qkv.shape[0]
    e = HEAD_DIM
    t = min(FOX_TILE, s)
    nt = s // t
    scale2 = e ** -0.5 * LOG2E
    lanes = 128
    q_tab, k_tab = _causal_pairs(nt, key_major=False)

    def body(q_tab, k_tab, q_ref, k_ref, v_ref, cq_ref, ck_ref, o_ref, lse_ref,
             m_scr, l_scr, acc_scr, s_scr, p_scr):
        pair = pl.program_id(1)
        qi, ki = q_tab[pair], k_tab[pair]

        @pl.when(ki == 0)
        def _():
            m_scr[...] = jnp.full_like(m_scr, NEG_INF)
            l_scr[...] = jnp.zeros_like(l_scr)
            acc_scr[...] = jnp.zeros_like(acc_scr)

        def update(diagonal):
            s_scr[...] = _nt(k_ref[...], q_ref[...])
            ck2 = ck_ref[...]
            for c0 in range(0, t, lanes):
                cols = pl.ds(c0, lanes)

                def logits2():
                    x = s_scr[:, cols] * scale2 - ck2
                    if diagonal:
                        key = lax.broadcasted_iota(jnp.int32, x.shape, 0)
                        qry = c0 + lax.broadcasted_iota(jnp.int32, x.shape, 1)
                        x = jnp.where(key <= qry, x, NEG_INF)
                    return x

                m_old = m_scr[:, cols]
                m_new = jnp.maximum(m_old, jnp.max(logits2(), axis=0, keepdims=True))
                p = jnp.exp2(logits2() - m_new)
                alpha = jnp.exp2(m_old - m_new)
                l_scr[:, cols] = alpha * l_scr[:, cols] + jnp.sum(p, axis=0, keepdims=True)
                m_scr[:, cols] = m_new
                acc_scr[:, cols] = alpha * acc_scr[:, cols]
                p_scr[:, cols] = p.astype(BF16)
            acc_scr[...] += _tn(v_ref[...], p_scr[...])

        @pl.when(ki < qi)
        def _():
            update(False)

        @pl.when(ki == qi)
        def _():
            update(True)
            o_ref[...] = (acc_scr[...] / l_scr[...]).T
            lse_ref[...] = (m_scr[...] + jnp.log2(l_scr[...])) * (1.0 / LOG2E) + cq_ref[...]

    h_ = n_heads
    return _call(
        body, name=name, grid=(h_, int(q_tab.shape[0])), tables=(q_tab, k_tab),
        in_specs=[
            pl.BlockSpec((t, e), lambda h, p, qt, kt: (qt[p], h)),
            pl.BlockSpec((t, e), lambda h, p, qt, kt: (kt[p], h_ + h)),
            pl.BlockSpec((t, e), lambda h, p, qt, kt: (kt[p], 2 * h_ + h)),
            pl.BlockSpec((None, 1, t), lambda h, p, qt, kt: (h, 0, qt[p])),
            pl.BlockSpec((None, t, lanes), lambda h, p, qt, kt: (h, kt[p], 0)),
        ],
        out_specs=[pl.BlockSpec((t, e), lambda h, p, qt, kt: (qt[p], h)),
                   pl.BlockSpec((None, 1, t), lambda h, p, qt, kt: (h, 0, qt[p]))],
        out_shape=[jax.ShapeDtypeStruct((s, h_ * e), F32), jax.ShapeDtypeStruct((h_, 1, s), F32)],
        operands=(qkv, qkv, qkv, c_row, ck_lanes),
        scratch=[pltpu.VMEM((1, t), F32), pltpu.VMEM((1, t), F32), pltpu.VMEM((e, t), F32),
                 pltpu.VMEM((t, t), F32), pltpu.VMEM((t, t), BF16)],
        semantics=("parallel", "arbitrary"), carry=carry)


def _fox_bwd(qkv, c_row, ck_lanes, lse, y, dy, n_heads, *, name, carry=None):
    s = qkv.shape[0]
    e = HEAD_DIM
    t = min(FOX_TILE, s)
    nt = s // t
    scale = e ** -0.5
    scale2 = scale * LOG2E
    lanes = 128
    q_tab, k_tab = _causal_pairs(nt, key_major=False)
    n_pairs = int(q_tab.shape[0])

    def body(q_tab, k_tab, q_ref, k_ref, v_ref, c_ref, ck_ref, lse_ref, y_ref, dy_ref,
             dq_ref, dk_ref, dv_ref, dcq_ref, dck_ref,
             dq_scr, dk_scr, dv_scr, dck_scr, s_scr, dp_scr, p_scr, ds_scr, do_scr, delta_scr, shift_scr):
        pair = pl.program_id(1)
        qi, ki = q_tab[pair], k_tab[pair]

        @pl.when(pair == 0)
        def _():
            dk_scr[...] = jnp.zeros_like(dk_scr)
            dv_scr[...] = jnp.zeros_like(dv_scr)
            dck_scr[...] = jnp.zeros_like(dck_scr)

        @pl.when(ki == 0)
        def _():
            do = dy_ref[...]
            delta_scr[...] = lax.dot_general(jnp.ones((8, e), F32), do * y_ref[...], (((1,), (1,)), ((), ())),
                                             precision=lax.Precision.HIGHEST, preferred_element_type=F32)
            shift_scr[...] = (lse_ref[...] - c_ref[...]) * LOG2E
            do_scr[...] = do.astype(BF16)
            dq_scr[...] = jnp.zeros_like(dq_scr)
            dcq_ref[...] = jnp.zeros_like(dcq_ref)

        def update(diagonal):
            s_scr[...] = _nt(k_ref[...], q_ref[...])
            dp_scr[...] = _nt(v_ref[...], do_scr[...])
            ck2 = ck_ref[...]
            k_rows = pl.ds(pl.multiple_of(ki * t, t), t)
            for c0 in range(0, t, lanes):
                cols = pl.ds(c0, lanes)
                x = s_scr[:, cols] * scale2 - ck2
                if diagonal:
                    key = lax.broadcasted_iota(jnp.int32, x.shape, 0)
                    qry = c0 + lax.broadcasted_iota(jnp.int32, x.shape, 1)
                    x = jnp.where(key <= qry, x, NEG_INF)
                p = jnp.exp2(x - shift_scr[:, cols])
                ds = p * (dp_scr[:, cols] - delta_scr[0:1, cols])
                dcq_ref[:, cols] += jnp.sum(ds, axis=0, keepdims=True)
                dck_scr[k_rows, :] += ds
                p_scr[:, cols] = p.astype(BF16)
                ds_scr[:, cols] = ds.astype(BF16)
            dv_scr[k_rows, :] += _nn(p_scr[...], do_scr[...])
            dk_scr[k_rows, :] += scale * _nn(ds_scr[...], q_ref[...])
            dq_scr[...] += scale * _tn(ds_scr[...], k_ref[...])

        @pl.when(ki < qi)
        def _():
            update(False)

        @pl.when(ki == qi)
        def _():
            update(True)
            dq_ref[...] = dq_scr[...].astype(BF16)

        @pl.when(pair == n_pairs - 1)
        def _():
            dk_ref[...] = dk_scr[...].astype(BF16)
            dv_ref[...] = dv_scr[...].astype(BF16)
            dck_ref[...] = -lax.dot_general(jnp.ones((8, lanes), F32), dck_scr[...], (((1,), (1,)), ((), ())),
                                            precision=lax.Precision.HIGHEST, preferred_element_type=F32)[0:1]

    h_ = n_heads
    q_row = pl.BlockSpec((None, 1, t), lambda h, p, qt, kt: (h, 0, qt[p]))
    q_blk = pl.BlockSpec((t, e), lambda h, p, qt, kt: (qt[p], h))
    whole = pl.BlockSpec((s, e), lambda h, p, qt, kt: (0, h))
    return _call(
        body, name=name, grid=(h_, n_pairs), tables=(q_tab, k_tab),
        in_specs=[
            q_blk,
            pl.BlockSpec((t, e), lambda h, p, qt, kt: (kt[p], h_ + h)),
            pl.BlockSpec((t, e), lambda h, p, qt, kt: (kt[p], 2 * h_ + h)),
            q_row,
            pl.BlockSpec((None, t, lanes), lambda h, p, qt, kt: (h, kt[p], 0)),
            q_row, q_blk, q_blk,
        ],
        out_specs=[q_blk, whole, whole, q_row,
                   pl.BlockSpec((None, 1, s), lambda h, p, qt, kt: (h, 0, 0))],
        out_shape=[jax.ShapeDtypeStruct((s, h_ * e), BF16)] * 3
        + [jax.ShapeDtypeStruct((h_, 1, s), F32), jax.ShapeDtypeStruct((h_, 1, s), F32)],
        operands=(qkv, qkv, qkv, c_row, ck_lanes, lse, y, dy),
        scratch=[pltpu.VMEM((t, e), F32), pltpu.VMEM((s, e), F32), pltpu.VMEM((s, e), F32),
                 pltpu.VMEM((s, lanes), F32), pltpu.VMEM((t, t), F32), pltpu.VMEM((t, t), F32), pltpu.VMEM((t, t), BF16),
                 pltpu.VMEM((t, t), BF16), pltpu.VMEM((t, e), BF16), pltpu.VMEM((8, t), F32),
                 pltpu.VMEM((1, t), F32)],
        semantics=("parallel", "arbitrary"), carry=carry)


def _rel_bucket_table(dilation, span):
    dist = np.arange(span + 1, dtype=np.int64) * dilation
    max_exact = REL_BUCKETS // 2
    d = np.maximum(dist.astype(np.float32), np.float32(1.0))
    large = max_exact + (np.log(d / np.float32(max_exact)) / np.float32(math.log(REL_MAX_DISTANCE / max_exact))
                         * np.float32(REL_BUCKETS - max_exact)).astype(np.int32)
    large = np.minimum(large, REL_BUCKETS - 1)
    return np.where(dist < max_exact, dist, large)


def _bucket_bands(dilation, span):
    table = _rel_bucket_table(dilation, span)
    bands = []
    for n, b in enumerate(table):
        if bands and bands[-1][0] == int(b):
            bands[-1][2] = n
        else:
            assert not any(bb[0] == int(b) for bb in bands)
            bands.append([int(b), n, n])
    return [tuple(b) for b in bands]


def _steps_back():
    i = lax.broadcasted_iota(jnp.int32, (DIL_BLOCK, DIL_BLOCK), 0)
    j = lax.broadcasted_iota(jnp.int32, (DIL_BLOCK, DIL_BLOCK), 1)
    return i - j, DIL_BLOCK + i - j


def _bias_tiles(rb_ref, h, bands, span):
    n_cur, n_prev = _steps_back()
    t_cur = jnp.zeros((DIL_BLOCK, DIL_BLOCK), F32)
    t_prev = jnp.zeros((DIL_BLOCK, DIL_BLOCK), F32)
    for b, lo, hi in bands:
        val = rb_ref[b, h]
        t_cur = jnp.where((n_cur >= lo) & (n_cur <= hi), val, t_cur)
        t_prev = jnp.where((n_prev >= lo) & (n_prev <= hi), val, t_prev)
    t_cur = jnp.where(n_cur >= 0, t_cur, NEG_INF)
    t_prev = jnp.where(n_prev <= span, t_prev, NEG_INF)
    return t_cur, t_prev


def _dil_rows(start, dilation):
    return pl.ds(start, DIL_BLOCK, stride=dilation) if dilation > 1 else pl.ds(start, DIL_BLOCK)


def _dil_block_groups(s, dilation, run_group):
    group = dilation * DIL_BLOCK
    per_residue = s // group

    def block(r, n):
        if isinstance(n, int):
            return n * group + r, max(n - 1, 0) * group + r, n == 0
        return (pl.multiple_of(n * group, DIL_BLOCK) + r,
                pl.multiple_of(jnp.maximum(n - 1, 0) * group, DIL_BLOCK) + r, n == 0)

    if per_residue >= DIL_GROUP:
        assert per_residue % DIL_GROUP == 0
        for r in range(dilation):
            def trip(it, carry, r=r):
                run_group([block(r, it * DIL_GROUP + j) for j in range(DIL_GROUP)])
                return carry
            lax.fori_loop(0, per_residue // DIL_GROUP, trip, 0)
    else:
        residues = DIL_GROUP // per_residue
        assert residues * per_residue == DIL_GROUP and dilation % residues == 0
        for r0 in range(0, dilation, residues):
            run_group([block(r, n) for r in range(r0, r0 + residues) for n in range(per_residue)])


def _dil_fwd(qkv, rel_bias, n_heads, col0, *, name, carry=None):
    s = qkv.shape[0]
    e = HEAD_DIM
    scale = e ** -0.5
    n_pat = len(DIL_PATTERNS)
    for window, d in DIL_PATTERNS:
        assert window // d == DIL_BLOCK and s % (d * DIL_BLOCK) == 0
    bands = [_bucket_bands(d, w // d) for w, d in DIL_PATTERNS]

    def body(rb_ref, q_ref, k_ref, v_ref, y_ref, lse_ref, qf, kf, vf, *scr):
        o_scr, l_scr = scr[:n_pat], scr[n_pat:]
        h = pl.program_id(0)
        qf[...] = q_ref[...].astype(F32)
        kf[...] = k_ref[...].astype(F32)
        vf[...] = v_ref[...].astype(F32)
        for pi, (window, d) in enumerate(DIL_PATTERNS):
            t_cur, t_prev = _bias_tiles(rb_ref, h, bands[pi], window // d)

            def run_group(blocks, d=d, pi=pi, t_cur=t_cur, t_prev=t_prev):
                rows = [(_dil_rows(q0, d), _dil_rows(p0, d)) for q0, p0, _ in blocks]
                qb = [qf[cur, :].astype(BF16) for cur, _ in rows]
                s_c = [_nt(q, kf[cur, :].astype(BF16)) for q, (cur, _) in zip(qb, rows)]
                s_p = [_nt(q, kf[prev, :].astype(BF16)) for q, (_, prev) in zip(qb, rows)]
                s_c = [x * scale + t_cur for x in s_c]
                s_p = [x * scale + (t_prev + jnp.where(first, NEG_INF, 0.0)) for x, (_, _, first) in zip(s_p, blocks)]
                m = [jnp.max(jnp.maximum(a, b), axis=-1, keepdims=True) for a, b in zip(s_c, s_p)]
                p_c = [jnp.exp(a - mm) for a, mm in zip(s_c, m)]
                p_p = [jnp.exp(b - mm) for b, mm in zip(s_p, m)]
                l = [jnp.sum(a + b, axis=-1, keepdims=True) for a, b in zip(p_c, p_p)]
                o = [_nn(a.astype(BF16), vf[cur, :].astype(BF16)) + _nn(b.astype(BF16), vf[prev, :].astype(BF16))
                     for a, b, (cur, prev) in zip(p_c, p_p, rows)]
                for (cur, _), oo, ll, mm in zip(rows, o, l, m):
                    o_scr[pi][cur, :] = oo / ll
                    l_scr[pi][cur, :] = mm + jnp.log(ll)

            _dil_block_groups(s, d, run_group)
        lses = [l_scr[pi][...] for pi in range(n_pat)]
        m = functools.reduce(jnp.maximum, lses)
        ws = [jnp.exp(l - m) for l in lses]
        tot = functools.reduce(jnp.add, ws)
        y = functools.reduce(jnp.add, [w * o_scr[pi][...] for pi, w in enumerate(ws)])
        y_ref[...] = y / tot
        lse_ref[...] = m + jnp.log(tot)

    h_ = n_heads
    return _call(
        body, name=name, grid=(h_,),
        in_specs=[pl.BlockSpec(memory_space=pltpu.SMEM),
                  pl.BlockSpec((s, e), lambda h: (0, col0 + h)),
                  pl.BlockSpec((s, e), lambda h: (0, col0 + h_ + h)),
                  pl.BlockSpec((s, e), lambda h: (0, col0 + 2 * h_ + h))],
        out_specs=[pl.BlockSpec((s, e), lambda h: (0, h)), pl.BlockSpec((None, s, 1), lambda h: (h, 0, 0))],
        out_shape=[jax.ShapeDtypeStruct((s, h_ * e), F32), jax.ShapeDtypeStruct((h_, s, 1), F32)],
        operands=(rel_bias, qkv, qkv, qkv),
        scratch=[pltpu.VMEM((s, e), F32)] * 3 + [pltpu.VMEM((s, e), F32)] * n_pat
        + [pltpu.VMEM((s, 1), F32)] * n_pat,
        semantics=("parallel",), carry=carry)


def _dil_bwd(qkv, rel_bias, lse, y, dy, n_heads, col0, *, name, carry=None):
    s = qkv.shape[0]
    e = HEAD_DIM
    scale = e ** -0.5
    bands = [_bucket_bands(d, w // d) for w, d in DIL_PATTERNS]

    def body(rb_ref, q_ref, k_ref, v_ref, lse_ref, y_ref, dy_ref, dq_ref, dk_ref, dv_ref, drb_ref,
             qf, kf, vf, dqf, dkf, dvf, delta, dt_cur, dt_prev):
        h = pl.program_id(0)
        qf[...] = q_ref[...].astype(F32)
        kf[...] = k_ref[...].astype(F32)
        vf[...] = v_ref[...].astype(F32)
        dqf[...] = jnp.zeros_like(dqf)
        dkf[...] = jnp.zeros_like(dkf)
        dvf[...] = jnp.zeros_like(dvf)
        delta[...] = jnp.sum(dy_ref[...] * y_ref[...], axis=-1, keepdims=True)
        lane = lax.broadcasted_iota(jnp.int32, (1, 128), 1)
        drb = jnp.zeros((1, 128), F32)
        n_cur, n_prev = _steps_back()
        for pi, (window, d) in enumerate(DIL_PATTERNS):
            t_cur, t_prev = _bias_tiles(rb_ref, h, bands[pi], window // d)
            dt_cur[...] = jnp.zeros_like(dt_cur)
            dt_prev[...] = jnp.zeros_like(dt_prev)

            def run_group(blocks, d=d, t_cur=t_cur, t_prev=t_prev):
                rows = [(_dil_rows(q0, d), _dil_rows(p0, d)) for q0, p0, _ in blocks]
                qb = [qf[cur, :].astype(BF16) for cur, _ in rows]
                kc = [kf[cur, :].astype(BF16) for cur, _ in rows]
                kp = [kf[prev, :].astype(BF16) for _, prev in rows]
                vc = [vf[cur, :].astype(BF16) for cur, _ in rows]
                vp = [vf[prev, :].astype(BF16) for _, prev in rows]
                do = [dy_ref[cur, :].astype(BF16) for cur, _ in rows]
                lse_b = [lse_ref[cur, :] for cur, _ in rows]
                delta_b = [delta[cur, :] for cur, _ in rows]
                s_c = [_nt(q, k) for q, k in zip(qb, kc)]
                s_p = [_nt(q, k) for q, k in zip(qb, kp)]
                dp_c = [_nt(g, v) for g, v in zip(do, vc)]
                dp_p = [_nt(g, v) for g, v in zip(do, vp)]
                p_c = [jnp.exp(x * scale + t_cur - ls) for x, ls in zip(s_c, lse_b)]
                p_p = [jnp.exp(x * scale + (t_prev + jnp.where(first, NEG_INF, 0.0)) - ls)
                       for x, ls, (_, _, first) in zip(s_p, lse_b, blocks)]
                ds_c = [p * (g - dl) for p, g, dl in zip(p_c, dp_c, delta_b)]
                ds_p = [p * (g - dl) for p, g, dl in zip(p_p, dp_p, delta_b)]
                dt_cur[...] += functools.reduce(jnp.add, ds_c)
                dt_prev[...] += functools.reduce(jnp.add, ds_p)
                p_c, p_p = [p.astype(BF16) for p in p_c], [p.astype(BF16) for p in p_p]
                ds_c, ds_p = [x.astype(BF16) for x in ds_c], [x.astype(BF16) for x in ds_p]
                dq = [scale * (_nn(a, k1) + _nn(b, k2)) for a, b, k1, k2 in zip(ds_c, ds_p, kc, kp)]
                dk_c = [scale * _tn(a, q) for a, q in zip(ds_c, qb)]
                dv_c = [_tn(p, g) for p, g in zip(p_c, do)]
                dk_p = [scale * _tn(b, q) for b, q in zip(ds_p, qb)]
                dv_p = [_tn(p, g) for p, g in zip(p_p, do)]
                for i, (cur, prev) in enumerate(rows):
                    dqf[cur, :] += dq[i]
                    dkf[cur, :] += dk_c[i]
                    dvf[cur, :] += dv_c[i]
                    dkf[prev, :] += dk_p[i]
                    dvf[prev, :] += dv_p[i]

            _dil_block_groups(s, d, run_group)
            dtc, dtp = dt_cur[...], dt_prev[...]
            for b, lo, hi in bands[pi]:
                tot = (jnp.sum(jnp.where((n_cur >= lo) & (n_cur <= hi), dtc, 0.0))
                       + jnp.sum(jnp.where((n_prev >= lo) & (n_prev <= hi), dtp, 0.0)))
                drb = drb + jnp.where(lane == b, tot, 0.0)
        dq_ref[...] = dqf[...].astype(BF16)
        dk_ref[...] = dkf[...].astype(BF16)
        dv_ref[...] = dvf[...].astype(BF16)
        drb_ref[...] = drb

    h_ = n_heads
    col = pl.BlockSpec((s, e), lambda h: (0, h))
    return _call(
        body, name=name, grid=(h_,),
        in_specs=[pl.BlockSpec(memory_space=pltpu.SMEM),
                  pl.BlockSpec((s, e), lambda h: (0, col0 + h)),
                  pl.BlockSpec((s, e), lambda h: (0, col0 + h_ + h)),
                  pl.BlockSpec((s, e), lambda h: (0, col0 + 2 * h_ + h)),
                  pl.BlockSpec((None, s, 1), lambda h: (h, 0, 0)), col, col],
        out_specs=[col, col, col, pl.BlockSpec((None, 1, 128), lambda h: (h, 0, 0))],
        out_shape=[jax.ShapeDtypeStruct((s, h_ * e), BF16)] * 3 + [jax.ShapeDtypeStruct((h_, 1, 128), F32)],
        operands=(rel_bias, qkv, qkv, qkv, lse, y, dy),
        scratch=[pltpu.VMEM((s, e), F32)] * 6 + [pltpu.VMEM((s, 1), F32)]
        + [pltpu.VMEM((DIL_BLOCK, DIL_BLOCK), F32)] * 2,
        semantics=("parallel",), carry=carry)


def _place():
    x, y, c = lax.axis_index("x"), lax.axis_index("y"), lax.axis_index("c")
    chips = [(1 - x, y), (x, 1 - y), (1 - x, 1 - y)]
    return x, y, c, chips


def _block_index(px, py, pc):
    return 4 * px + 2 * py + pc


def _gather_task(blocks, *, row0=0, rows=None, into=None):
    n = len(blocks)
    part = blocks[0].shape[0]
    rows = rows or part

    def copies(ins, outs, sems):
        send_sems, recv_sems, local_sems = sems
        x, y, c, chips = _place()
        me, sibling = (x, y, c), (x, y, 1 - c)

        def place_of(a, block):
            slot = outs[a].at[_block_index(*block)]
            return slot if part == rows else slot.at[pl.ds(row0, part)]

        def copy(a, k, block, to, src=None):
            slot = place_of(a, block)
            return pltpu.make_async_remote_copy(
                src_ref=slot if src is None else src, dst_ref=slot,
                send_sem=send_sems.at[a, k], recv_sem=recv_sems.at[a, k],
                device_id=to, device_id_type=MESH)

        local = lambda a: pltpu.make_async_copy(ins[a], place_of(a, me), local_sems.at[a])
        return copy, local, me, sibling, c, chips

    def start(ins, outs, sems):
        copy, local, me, sibling, c, chips = copies(ins, outs, sems)
        for a in range(n):
            local(a).start()
            copy(a, 0, me, sibling, src=ins[a]).start()
            for j, chip in enumerate(chips):
                copy(a, 1 + j, me, (*chip, c), src=ins[a]).start()

    def forward(ins, outs, sems):
        copy, local, me, sibling, c, chips = copies(ins, outs, sems)
        for a in range(n):
            for j, chip in enumerate(chips):
                copy(a, 1 + j, (*chip, c), me).wait_recv()
                copy(a, 4 + j, (*chip, c), sibling).start()

    def finish(ins, outs, sems):
        copy, local, me, sibling, c, chips = copies(ins, outs, sems)
        for a in range(n):
            copy(a, 0, sibling, me).wait_recv()
            for j, chip in enumerate(chips):
                copy(a, 4 + j, (*chip, 1 - c), me).wait_recv()
        for a in range(n):
            for k in range(7):
                copy(a, k, me, sibling, src=ins[a]).wait_send()
            local(a).wait()

    return _Carried(
        list(blocks) + list(into or []),
        [jax.ShapeDtypeStruct((N_DEV, rows) + b.shape[1:], b.dtype) for b in blocks],
        [pltpu.SemaphoreType.DMA((n, 7)), pltpu.SemaphoreType.DMA((n, 7)), pltpu.SemaphoreType.DMA((n,))],
        start, forward, finish, aliases={n + a: a for a in range(n)} if into else None)


def _exchange_task(arrays, n_slots, route):
    n = len(arrays)

    def copies(ins, outs, sems):
        send_sems, recv_sems = sems
        out = []
        for a in range(n):
            for j in range(n_slots):
                src, to = route(j)
                out.append(pltpu.make_async_remote_copy(
                    src_ref=ins[a].at[src], dst_ref=outs[a].at[j],
                    send_sem=send_sems.at[a, j], recv_sem=recv_sems.at[a, j],
                    device_id=to, device_id_type=MESH))
        return out

    def start(ins, outs, sems):
        for cp in copies(ins, outs, sems):
            cp.start()

    def finish(ins, outs, sems):
        for cp in copies(ins, outs, sems):
            cp.wait()

    return _Carried(
        arrays, [jax.ShapeDtypeStruct((n_slots,) + g.shape[1:], g.dtype) for g in arrays],
        [pltpu.SemaphoreType.DMA((n, n_slots)), pltpu.SemaphoreType.DMA((n, n_slots))],
        start, lambda ins, outs, sems: None, finish)


def _sibling_task(grads):
    def route(q):
        x, y, c, _ = _place()
        return 2 * q + (1 - c), (x, y, 1 - c)
    return _exchange_task(grads, 4, route)


def _chips_task(parts):
    def route(j):
        x, y, c, chips = _place()
        px, py = chips[j]
        return 2 * px + py, (px, py, c)
    return _exchange_task(parts, 3, route)


def _chip_partials(grad, got, place, *, name):
    _, r, cdim = grad.shape
    tr = _row_tile(r, 512)

    def body(place_ref, g_ref, s_ref, o_ref):
        o_ref[...] = (g_ref[...] + s_ref[...]).astype(BF16)

    return pl.pallas_call(
        body, name=name,
        grid_spec=pltpu.PrefetchScalarGridSpec(
            num_scalar_prefetch=1, grid=(4, r // tr),
            in_specs=[pl.BlockSpec((None, tr, cdim), lambda q, i, pos: (2 * q + pos[2], i, 0)),
                      pl.BlockSpec((None, tr, cdim), lambda q, i, pos: (q, i, 0))],
            out_specs=pl.BlockSpec((None, tr, cdim), lambda q, i, pos: (q, i, 0))),
        out_shape=jax.ShapeDtypeStruct((4, r, cdim), BF16),
        compiler_params=_params("parallel", "parallel"),
    )(place, grad, got)


def _all_reduce_small(v, *, name):
    r, w = v.shape

    def body(v_ref, o_ref, buf, send_sems, recv_sems):
        x, y, c, _ = _place()
        me = _block_index(x, y, c)
        buf[me] = v_ref[...]
        copies = []
        for k in range(1, N_DEV):
            fx, fy, fc = (k >> 2) & 1, (k >> 1) & 1, k & 1
            peer = (x ^ fx, y ^ fy, c ^ fc)
            cp = pltpu.make_async_remote_copy(
                src_ref=v_ref, dst_ref=buf.at[me],
                send_sem=send_sems.at[k - 1], recv_sem=recv_sems.at[k - 1],
                device_id=peer, device_id_type=MESH)
            cp.start()
            copies.append(cp)
        for cp in copies:
            cp.wait()
        acc = buf[0]
        for p in range(1, N_DEV):
            acc = acc + buf[p]
        o_ref[...] = acc

    vmem = pl.BlockSpec(memory_space=pltpu.VMEM)
    return pl.pallas_call(
        body, name=name, in_specs=[vmem], out_specs=vmem,
        out_shape=jax.ShapeDtypeStruct((r, w), F32),
        scratch_shapes=[pltpu.VMEM((N_DEV, r, w), F32), pltpu.SemaphoreType.DMA((N_DEV - 1,)),
                        pltpu.SemaphoreType.DMA((N_DEV - 1,))],
    )(v)


def _adamw_math(w, g, m, v):
    m = ADAM_B1 * m + (1.0 - ADAM_B1) * g
    v = ADAM_B2 * v + (1.0 - ADAM_B2) * (g * g)
    m_hat = m / (1.0 - ADAM_B1 ** ADAM_STEP)
    v_hat = v / (1.0 - ADAM_B2 ** ADAM_STEP)
    delta = -ADAM_LR * (m_hat / (jnp.sqrt(v_hat) + ADAM_EPS) + ADAM_WD * w)
    return delta, m, v


def _adamw_sharded(w, m, v, part, got, place, *, name, carry=None):
    n_l, r, cdim = w.shape
    tr = _row_tile(r, 256)

    def body(place_ref, *refs):
        w_ref, m_ref, v_ref = refs[:3]
        p_refs, g_refs = refs[3:3 + n_l], refs[3 + n_l:3 + 2 * n_l]
        g_out, d_out, m_out, v_out = refs[3 + 2 * n_l:]
        for l in range(n_l):
            @pl.when(pl.program_id(0) == l)
            def _(l=l):
                g = p_refs[l][...].astype(F32)
                for j in range(3):
                    g = g + g_refs[l][j].astype(F32)
                delta, m_new, v_new = _adamw_math(w_ref[...], g, m_ref[...], v_ref[...])
                g_out[...] = g
                d_out[...] = delta
                m_out[...] = m_new
                v_out[...] = v_new

    shard = pl.BlockSpec((None, tr, cdim), lambda l, i, pos: (l, i, 0))
    p_specs = [pl.BlockSpec((None, tr, cdim), lambda l, i, pos: (2 * pos[0] + pos[1], i, 0))] * n_l
    g_specs = [pl.BlockSpec((3, tr, cdim), lambda l, i, pos: (0, i, 0))] * n_l
    return _call(
        body, name=name, grid=(n_l, r // tr), tables=(place,),
        in_specs=[shard] * 3 + p_specs + g_specs, out_specs=[shard] * 4,
        out_shape=[jax.ShapeDtypeStruct(w.shape, F32)] * 4,
        operands=(w, m, v, *part, *got), semantics=("arbitrary", "parallel"), carry=carry)


def _adamw_small(w, g, m, v, *, name):
    def body(w_ref, g_ref, m_ref, v_ref, d_out, m_out, v_out):
        delta, m_new, v_new = _adamw_math(w_ref[...], g_ref[...], m_ref[...], v_ref[...])
        d_out[...] = delta
        m_out[...] = m_new
        v_out[...] = v_new

    vmem = pl.BlockSpec(memory_space=pltpu.VMEM)
    return pl.pallas_call(
        body, name=name, in_specs=[vmem] * 4, out_specs=[vmem] * 3,
        out_shape=[jax.ShapeDtypeStruct(w.shape, F32)] * 3,
    )(w, g, m, v)


def _pad_lanes(a, width=128):
    return jnp.pad(a, ((0, 0), (0, width - a.shape[1])))


def _train_step(x, target, small, shards, place):
    w_in, w_out, w_mlp_in, w_mlp_out = shards
    depth, d_model, in_shard = w_in.shape
    n_heads = small["forget_b"].shape[1]
    dh = n_heads * HEAD_DIM
    d_ff = w_mlp_in.shape[2] * N_DEV
    row = lambda a: a.reshape(1, -1)
    block = {"in": w_in.astype(BF16), "out": w_out.astype(BF16), "mi": w_mlp_in.astype(BF16),
             "mo": w_mlp_out.astype(BF16)}
    whole = {}

    gate_lo, gate_hi = 3 * dh, 3 * dh + n_heads

    def shard_columns(g, lo, hi):
        pieces = []
        while lo < hi:
            p, a = divmod(lo, in_shard)
            b = min(in_shard, a + hi - lo)
            pieces.append(g[p][:, a:b])
            lo += b - a
        return pieces

    def in_columns(d_qkv, d_gate, lo, hi):
        pieces = []
        if lo < gate_lo:
            pieces.append(d_qkv[:, lo:min(hi, gate_lo)])
        if lo < gate_hi and hi > gate_lo:
            pieces.append(d_gate[:, max(lo, gate_lo) - gate_lo:min(hi, gate_hi) - gate_lo])
        if hi > gate_hi:
            pieces.append(d_qkv[:, max(lo, gate_hi) - n_heads:hi - n_heads])
        return pieces

    def unpack(kind, l, g):
        if kind == "in":
            whole["qkv", l] = jnp.concatenate(
                shard_columns(g, 0, gate_lo) + shard_columns(g, gate_hi, N_DEV * in_shard), axis=1)
            whole["f", l] = _pad_lanes(jnp.concatenate(shard_columns(g, gate_lo, gate_hi), axis=1))
        elif kind == "out":
            whole["out", l] = g.reshape(2 * dh, d_model)
        elif kind == "mi":
            whole["mi", l] = g
        else:
            whole["mo", l] = g.reshape(d_ff, d_model)

    half_done = {}

    def gathering(kind, l, part):
        blk = block[kind][l]
        rows = blk.shape[0]
        if part == "all":
            return _gather_task([blk]), lambda got: unpack(kind, l, got[0])
        if part == "first":
            return (_gather_task([blk[:rows // 2]], rows=rows),
                    lambda got: half_done.__setitem__((kind, l), got[0]))
        return (_gather_task([blk[rows // 2:]], row0=rows // 2, rows=rows, into=[half_done.pop((kind, l))]),
                lambda got: unpack(kind, l, got[0]))

    def carried(call, *wanted):
        wanted = [gathering(*w) for w in wanted if w[1] < depth]
        if not wanted:
            return call()
        res, got = call(carry=_join([task for task, _ in wanted]))
        for (_, done), g in zip(wanted, got):
            done([g])
        return res

    (g,) = _run_carried(_gather_task([block["in"][0]]), name="gather_w_in_0")
    unpack("in", 0, g)

    saved = []
    for l in range(depth):
        h1 = _rmsnorm_fwd(x, row(small["norm1_g"][l]), name=f"norm1_fwd_{l}")
        qkv = carried(functools.partial(_matmul, h1, whole["qkv", l], out_dtype=BF16, name=f"qkv_proj_{l}"),
                      ("mi", l, "first"))
        f = _matmul(h1, whole["f", l], name=f"gate_proj_{l}")
        fb = _pad_lanes(row(small["forget_b"][l]))
        c = _gates_fwd(f, fb, name=f"gates_fwd_{l}")
        c_heads = c[:, :n_heads].T
        cq = c_heads[:, None, :]
        ck = jnp.broadcast_to(c_heads[:, :, None] * LOG2E, c_heads.shape + (128,))
        ya, lse_a = carried(functools.partial(_fox_fwd, qkv, cq, ck, n_heads, name=f"fox_fwd_{l}"),
                            ("mi", l, "second"), ("out", l, "all"))
        yb, lse_b = carried(functools.partial(_dil_fwd, qkv, small["rel_bias"], n_heads, 3 * n_heads,
                                              name=f"dil_fwd_{l}"), ("mo", l, "first"))
        mixed = _outnorm_fwd(ya, yb, row(small["outnorm_a_g"][l]), row(small["outnorm_b_g"][l]),
                             name=f"outnorm_fwd_{l}")
        x1 = _matmul(mixed, whole["out", l], addend=x, name=f"out_proj_{l}")
        h2 = _rmsnorm_fwd(x1, row(small["norm2_g"][l]), name=f"norm2_fwd_{l}")
        relu, act = carried(functools.partial(_matmul, h2, whole["mi", l], b_chunked=True, out_dtype=BF16,
                                              mode="relu_sq", name=f"mlp_in_{l}"),
                            ("mo", l, "second"))
        x2 = carried(functools.partial(_matmul, act, whole["mo", l], addend=x1, name=f"mlp_out_{l}"),
                     ("in", l + 1, "all"))
        saved.append((x, h1, qkv, f, fb, cq, ck, ya, lse_a, yb, lse_b, mixed, x1, h2, relu, act))
        x = x2
    dx, dx16, loss_tile, d_final = _loss_head(x, row(small["final_norm_g"]), target, name="loss_head")

    parts = {k: [None] * depth for k in ("in", "out", "mi", "mo")}
    got = {k: [None] * depth for k in ("in", "out", "mi", "mo")}
    small_grads = [None] * depth

    def to_sibling(call, kind, l, grad):
        res, (from_sibling,) = call(carry=_sibling_task([grad]))
        parts[kind][l] = _chip_partials(grad, from_sibling, place, name=f"chip_partials_{kind}_{l}")
        return res

    def to_chips(call, kind, l):
        if l >= depth:
            return call()
        res, (got[kind][l],) = call(carry=_chips_task([parts[kind][l]]))
        return res

    for l in reversed(range(depth)):
        x0, h1, qkv, f, fb, cq, ck, ya, lse_a, yb, lse_b, mixed, x1, h2, relu, act = saved[l]
        du = to_chips(functools.partial(_matmul, dx16, whole["mo", l], tb=True, out_dtype=BF16, mode="mul2",
                                        aux=relu, name=f"mlp_out_dx_{l}"), "in", l + 1)
        dw_mo = _matmul(act, dx16, ta=True, name=f"mlp_out_dw_{l}")
        dw_mo = dw_mo.reshape(N_DEV, d_ff // N_DEV, d_model)
        dh2 = to_sibling(functools.partial(_matmul, du, whole["mi", l], tb=True, b_chunked=True,
                                           name=f"mlp_in_dx_{l}"),
                         "mo", l, dw_mo)
        dw_mi = to_chips(functools.partial(_matmul, h2, du, ta=True, out_chunks=N_DEV,
                                           name=f"mlp_in_dw_{l}"), "mo", l)
        dx1, dx1_16, d_norm2 = _rmsnorm_bwd(x1, row(small["norm2_g"][l]), dh2, dx, name=f"norm2_bwd_{l}")
        dmixed = to_sibling(functools.partial(_matmul, dx1_16, whole["out", l], tb=True, name=f"out_proj_dx_{l}"),
                            "mi", l, dw_mi)
        dw_out = _matmul(mixed, dx1_16, ta=True, name=f"out_proj_dw_{l}")
        dw_out = dw_out.reshape(N_DEV, 2 * dh // N_DEV, d_model)
        dya, dyb, d_ga, d_gb = _outnorm_bwd(ya, yb, row(small["outnorm_a_g"][l]), row(small["outnorm_b_g"][l]),
                                            dmixed, name=f"outnorm_bwd_{l}")
        dqa, dka, dva, dcq, dck = to_chips(functools.partial(
            _fox_bwd, qkv, cq, ck, lse_a, ya, dya, n_heads, name=f"fox_bwd_{l}"), "mi", l)
        dqb, dkb, dvb, d_rb = to_sibling(functools.partial(
            _dil_bwd, qkv, small["rel_bias"], lse_b, yb, dyb, n_heads, 3 * n_heads, name=f"dil_bwd_{l}"),
            "out", l, dw_out)
        dc = _pad_lanes((dcq[:, 0, :] + dck[:, 0, :]).T)
        df, dfb = _gates_bwd(f, fb, dc, name=f"gates_bwd_{l}")
        dqkv = jnp.concatenate([dqa, dka, dva, dqb, dkb, dvb], axis=1)
        dw_qkv = to_chips(functools.partial(_matmul, h1, dqkv, ta=True, name=f"qkv_proj_dw_{l}"), "out", l)
        dw_f = _matmul(h1, df, ta=True, name=f"gate_proj_dw_{l}")
        dw_in = jnp.stack([jnp.concatenate(in_columns(dw_qkv, dw_f, p * in_shard, (p + 1) * in_shard), axis=1)
                           for p in range(N_DEV)])
        dh1 = _matmul(df, whole["f", l], tb=True, name=f"gate_proj_dx_{l}")
        dh1 = to_sibling(functools.partial(_matmul, dqkv, whole["qkv", l], tb=True, addend=dh1,
                                           name=f"qkv_proj_dx_{l}"), "in", l, dw_in)
        dx, dx16, d_norm1 = _rmsnorm_bwd(x0, row(small["norm1_g"][l]), dh1, dx1, name=f"norm1_bwd_{l}")
        small_grads[l] = dict(norm1_g=d_norm1[0], forget_b=dfb[0, :n_heads], rel_bias=d_rb[:, 0, :REL_BUCKETS].T,
                              outnorm_a_g=d_ga[0], outnorm_b_g=d_gb[0], norm2_g=d_norm2[0])
    return loss_tile[0, 0], dx, d_final[0], small_grads, parts, got


def _pack_small(parts, rows):
    flat = jnp.concatenate([p.reshape(-1).astype(F32) for p in parts])
    return jnp.pad(flat, (0, rows * 128 - flat.shape[0])).reshape(rows, 128)


def _unpack_small(packed, shapes):
    flat = packed.reshape(-1)
    out, pos = [], 0
    for shp in shapes:
        size = int(np.prod(shp))
        out.append(flat[pos:pos + size].reshape(shp))
        pos += size
    return out


def kernel(x, norm1_g, w_in, forget_b, rel_bias, outnorm_a_g, outnorm_b_g, w_out, norm2_g, w_mlp_in, w_mlp_out, final_norm_g, loss_target, m_norm1_g, m_w_in, m_forget_b, m_rel_bias, m_outnorm_a_g, m_outnorm_b_g, m_w_out, m_norm2_g, m_w_mlp_in, m_w_mlp_out, m_final_norm_g, v_norm1_g, v_w_in, v_forget_b, v_rel_bias, v_outnorm_a_g, v_outnorm_b_g, v_w_out, v_norm2_g, v_w_mlp_in, v_w_mlp_out, v_final_norm_g):
    depth, d_model, in_shard = w_in.shape
    n_heads = forget_b.shape[1]
    assert in_shard * N_DEV == 6 * n_heads * HEAD_DIM + n_heads and x.shape[0] == 1
    place = jnp.stack([lax.axis_index("x"), lax.axis_index("y"), lax.axis_index("c")]).astype(jnp.int32)

    small_names = ["norm1_g", "forget_b", "rel_bias", "outnorm_a_g", "outnorm_b_g", "norm2_g", "final_norm_g"]
    small_w = dict(norm1_g=norm1_g, forget_b=forget_b, rel_bias=rel_bias, outnorm_a_g=outnorm_a_g,
                   outnorm_b_g=outnorm_b_g, norm2_g=norm2_g, final_norm_g=final_norm_g)
    loss_part, dx, d_final, grads, parts, got = _train_step(
        x[0], loss_target[0], small_w, (w_in, w_out, w_mlp_in, w_mlp_out), place)

    small_m = dict(norm1_g=m_norm1_g, forget_b=m_forget_b, rel_bias=m_rel_bias, outnorm_a_g=m_outnorm_a_g,
                   outnorm_b_g=m_outnorm_b_g, norm2_g=m_norm2_g, final_norm_g=m_final_norm_g)
    small_v = dict(norm1_g=v_norm1_g, forget_b=v_forget_b, rel_bias=v_rel_bias, outnorm_a_g=v_outnorm_a_g,
                   outnorm_b_g=v_outnorm_b_g, norm2_g=v_norm2_g, final_norm_g=v_final_norm_g)
    small_g = {k: jnp.stack([g[k] for g in grads]) for k in small_names if k not in ("rel_bias", "final_norm_g")}
    small_g["rel_bias"] = functools.reduce(jnp.add, [g["rel_bias"] for g in grads])
    small_g["final_norm_g"] = d_final
    shapes = [small_w[k].shape for k in small_names]
    total = sum(int(np.prod(s)) for s in shapes) + 1
    rows = -(-total // (8 * 128)) * 8
    packed_g = _pack_small([small_g[k] for k in small_names] + [loss_part], rows)
    packed_g = _all_reduce_small(packed_g, name="reduce_small")
    zero = jnp.zeros((1,), F32)
    packed_w = _pack_small([small_w[k] for k in small_names] + [zero], rows)
    packed_m = _pack_small([small_m[k] for k in small_names] + [zero], rows)
    packed_v = _pack_small([small_v[k] for k in small_names] + [zero + 1.0], rows)
    packed_d, packed_nm, packed_nv = _adamw_small(packed_w, packed_g, packed_m, packed_v, name="adamw_small")
    g_small = dict(zip(small_names, _unpack_small(packed_g, shapes)))
    d_small = dict(zip(small_names, _unpack_small(packed_d, shapes)))
    nm_small = dict(zip(small_names, _unpack_small(packed_nm, shapes)))
    nv_small = dict(zip(small_names, _unpack_small(packed_nv, shapes)))
    loss = packed_g.reshape(-1)[total - 1]

    big_out = {}
    last = parts["in"][0]
    last_halves = [last[:, :last.shape[1] // 2], last[:, last.shape[1] // 2:]]
    last_got = []
    for kind, nm, w, m, v in [("mi", "w_mlp_in", w_mlp_in, m_w_mlp_in, v_w_mlp_in),
                              ("mo", "w_mlp_out", w_mlp_out, m_w_mlp_out, v_w_mlp_out),
                              ("out", "w_out", w_out, m_w_out, v_w_out), ("in", "w_in", w_in, m_w_in, v_w_in)]:
        adamw = functools.partial(_adamw_sharded, w, m, v, parts[kind], got[kind], place, name=f"adamw_{nm}")
        if last_halves:
            big_out[nm], (half,) = adamw(carry=_chips_task([last_halves.pop(0)]))
            last_got.append(half)
            if not last_halves:
                got["in"][0] = jnp.concatenate(last_got, axis=1)
        else:
            big_out[nm] = adamw()

    order = ["norm1_g", "w_in", "forget_b", "rel_bias", "outnorm_a_g", "outnorm_b_g", "w_out", "norm2_g",
             "w_mlp_in", "w_mlp_out", "final_norm_g"]
    pick = lambda k, idx, small: big_out[k][idx] if k in big_out else small[k]
    outs = [loss, dx[None]]
    outs += [pick(k, 0, g_small) for k in order]
    outs += [pick(k, 1, d_small) for k in order]
    outs += [pick(k, 2, nm_small) for k in order]
    outs += [pick(k, 3, nv_small) for k in order]
    return tuple(outs)
```

```python
import functools
import math

import numpy as np
import jax
import jax.numpy as jnp
from jax import lax
from jax.experimental import pallas as pl
from jax.experimental.pallas import tpu as pltpu

F32 = jnp.float32
BF16 = jnp.bfloat16

HEAD_DIM = 128
DIL_PATTERNS = ((128, 1), (512, 4), (2048, 16))
DIL_BLOCK = 128
REL_BUCKETS = 32
REL_MAX_DISTANCE = 2048
NORM_EPS = 1e-6
NEG_INF = -1e30
ADAM_LR = 0.001
ADAM_B1 = 0.9
ADAM_B2 = 0.999
ADAM_EPS = 1e-08
ADAM_WD = 0.01
ADAM_STEP = 10

N_DEV = 8
V7X_VMEM_LIMIT_BYTES = 56 * 1024 * 1024
MESH = pl.DeviceIdType.MESH


def _params(*semantics):
    return pltpu.CompilerParams(dimension_semantics=semantics, vmem_limit_bytes=V7X_VMEM_LIMIT_BYTES)


HBM_SPEC = pl.BlockSpec(memory_space=pltpu.HBM)


class _Carried:
    def __init__(self, operands, out_shape, scratch, start, forward, finish, aliases=None):
        self.operands, self.out_shape, self.scratch = list(operands), list(out_shape), list(scratch)
        self.start, self.forward, self.finish = start, forward, finish
        self.aliases = dict(aliases or {})


def _join(tasks):
    if len(tasks) == 1:
        return tasks[0]
    spans, aliases = [], {}
    i0 = o0 = s0 = 0
    for t in tasks:
        i1, o1, s1 = i0 + len(t.operands), o0 + len(t.out_shape), s0 + len(t.scratch)
        spans.append((slice(i0, i1), slice(o0, o1), slice(s0, s1)))
        aliases.update({i0 + i: o0 + o for i, o in t.aliases.items()})
        i0, o0, s0 = i1, o1, s1

    def phase(which):
        def run(ins, outs, sems):
            for t, (si, so, ss) in zip(tasks, spans):
                getattr(t, which)(ins[si], outs[so], sems[ss])
        return run

    return _Carried(sum((t.operands for t in tasks), []), sum((t.out_shape for t in tasks), []),
                    sum((t.scratch for t in tasks), []), phase("start"), phase("forward"), phase("finish"),
                    aliases)


FORWARD_AT = 0.8


def _call(body, *, name, grid, in_specs, out_specs, out_shape, operands, scratch=(), semantics, carry=None,
          tables=()):
    in_specs, out_specs, out_shape, scratch = list(in_specs), list(out_specs), list(out_shape), list(scratch)
    n_tab = len(tables)

    def run(fn, in_specs, out_specs, out_shape, scratch, operands, semantics, aliases):
        return pl.pallas_call(
            fn, name=name,
            grid_spec=pltpu.PrefetchScalarGridSpec(
                num_scalar_prefetch=n_tab, grid=grid, in_specs=in_specs, out_specs=out_specs,
                scratch_shapes=scratch),
            out_shape=out_shape, input_output_aliases=aliases,
            compiler_params=_params(*semantics))(*tables, *operands)

    if carry is None:
        return run(body, in_specs, out_specs, out_shape, scratch, operands, semantics, {})
    n_in, n_out = len(in_specs), len(out_specs)
    ci, co, cs = len(carry.operands), len(carry.out_shape), len(carry.scratch)
    steps = int(np.prod(grid))
    forward_step = min(int(steps * FORWARD_AT), steps - 1)

    def carrying(*refs):
        tabs, refs = refs[:n_tab], refs[n_tab:]
        main_in, c_in = refs[:n_in], refs[n_in:n_in + ci]
        main_out = refs[n_in + ci:n_in + ci + n_out]
        c_out = refs[n_in + ci + n_out:n_in + ci + n_out + co]
        rest = refs[n_in + ci + n_out + co:]
        main_scr, c_scr = rest[:len(rest) - cs], rest[len(rest) - cs:]
        step = 0
        for axis, extent in enumerate(grid):
            step = step * extent + pl.program_id(axis)

        @pl.when(step == 0)
        def _():
            carry.start(c_in, c_out, c_scr)

        body(*tabs, *main_in, *main_out, *main_scr)

        @pl.when(step == forward_step)
        def _():
            carry.forward(c_in, c_out, c_scr)

        @pl.when(step == steps - 1)
        def _():
            carry.finish(c_in, c_out, c_scr)

    aliases = {n_tab + n_in + i: n_out + o for i, o in carry.aliases.items()}
    res = run(carrying, in_specs + [HBM_SPEC] * ci, out_specs + [HBM_SPEC] * co, out_shape + carry.out_shape,
              scratch + carry.scratch, (*operands, *carry.operands), ["arbitrary"] * len(grid), aliases)
    return res[:n_out], res[n_out:]


def _run_carried(carry, *, name):
    ci, co = len(carry.operands), len(carry.out_shape)

    def body(*refs):
        parts = (refs[:ci], refs[ci:ci + co], refs[ci + co:])
        carry.start(*parts)
        carry.forward(*parts)
        carry.finish(*parts)

    return pl.pallas_call(
        body, name=name, in_specs=[HBM_SPEC] * ci, out_specs=[HBM_SPEC] * co,
        out_shape=carry.out_shape, scratch_shapes=carry.scratch)(*carry.operands)


def _fit(dim, want, unit=128):
    if dim <= want:
        return dim
    t = want - want % unit
    while dim % t:
        t -= unit
    return t


def _matmul(a, b, *, name, ta=False, tb=False, out_dtype=F32, tm=1024, tn=1024, tk=2048,
            addend=None, mode=None, aux=None, out_chunks=None, b_chunked=False, carry=None):
    m_dim, k_dim = (a.shape[1], a.shape[0]) if ta else a.shape
    b_rows, b_cols = (b.shape[1], b.shape[0] * b.shape[2]) if b_chunked else b.shape
    n_dim = b_rows if tb else b_cols
    assert (b_cols if tb else b_rows) == k_dim
    b_chunk = b.shape[2] if b_chunked else b_cols
    tm, tn, tk = _fit(m_dim, tm), _fit(n_dim // (out_chunks or 1), tn), _fit(k_dim, tk)
    k_chunks = 1
    if b_chunked and tb:
        assert tk % b_chunk == 0 and not ta
        k_chunks = tk // b_chunk
    elif b_chunked:
        tn = _fit(b_chunk, tn)
    assert m_dim % tm == 0 and n_dim % tn == 0 and k_dim % tk == 0, (name, a.shape, b.shape)
    nk = k_dim // tk
    a_spec = (pl.BlockSpec((tk, tm), lambda i, j, k: (k, i)) if ta
              else pl.BlockSpec((tm, tk), lambda i, j, k: (i, k)))
    if b_chunked and tb:
        b_spec = pl.BlockSpec((k_chunks, tn, b_chunk), lambda i, j, k: (k, j, 0))
    elif b_chunked:
        per_b = b_chunk // tn
        b_spec = pl.BlockSpec((None, tk, tn), lambda i, j, k: (j // per_b, k, j % per_b))
    else:
        b_spec = (pl.BlockSpec((tn, tk), lambda i, j, k: (j, k)) if tb
                  else pl.BlockSpec((tk, tn), lambda i, j, k: (k, j)))
    mn_spec = pl.BlockSpec((tm, tn), lambda i, j, k: (i, j))
    if out_chunks is None:
        o_spec, o_shape = mn_spec, (m_dim, n_dim)
    else:
        per = n_dim // out_chunks // tn
        assert per * tn * out_chunks == n_dim
        o_spec = pl.BlockSpec((None, tm, tn), lambda i, j, k: (j // per, i, j % per))
        o_shape = (out_chunks, m_dim, n_dim // out_chunks)
    dims = (((0 if ta else 1,), (1 if tb else 0,)), ((), ()))
    n_out = 2 if mode == "relu_sq" else 1
    in_place = mode is None and out_dtype == F32
    in_specs, operands = [a_spec, b_spec], [a, b]
    if addend is not None:
        in_specs.append(mn_spec)
        operands.append(addend)
    if mode == "mul2":
        in_specs.append(mn_spec)
        operands.append(aux)

    def body(*refs):
        a_ref, b_ref = refs[0], refs[1]
        pos = 2
        add_ref = aux_ref = None
        if addend is not None:
            add_ref, pos = refs[pos], pos + 1
        if mode == "mul2":
            aux_ref, pos = refs[pos], pos + 1
        outs = refs[pos:pos + n_out]

        def finish(acc):
            if add_ref is not None:
                acc = acc + add_ref[...].astype(F32)
            if mode == "relu_sq":
                r = jnp.maximum(acc, 0.0)
                outs[0][...] = r.astype(outs[0].dtype)
                outs[1][...] = (r * r).astype(outs[1].dtype)
            elif mode == "mul2":
                outs[0][...] = (acc * (2.0 * aux_ref[...].astype(F32))).astype(outs[0].dtype)
            else:
                outs[0][...] = acc.astype(outs[0].dtype)

        if k_chunks > 1:
            part = functools.reduce(jnp.add, [
                lax.dot_general(a_ref[:, c * b_chunk:(c + 1) * b_chunk].astype(BF16), b_ref[c].astype(BF16), dims,
                                preferred_element_type=F32) for c in range(k_chunks)])
        else:
            b_tile = b_ref[0] if (b_chunked and tb) else b_ref[...]
            part = lax.dot_general(a_ref[...].astype(BF16), b_tile.astype(BF16), dims,
                                   preferred_element_type=F32)
        k = pl.program_id(2)
        if nk == 1:
            finish(part)
        elif in_place:
            @pl.when(k == 0)
            def _():
                finish(part)

            @pl.when(k > 0)
            def _():
                outs[0][...] += part
        else:
            acc_ref = refs[-1]

            @pl.when(k == 0)
            def _():
                acc_ref[...] = part

            @pl.when(k > 0)
            def _():
                acc_ref[...] += part

            @pl.when(k == nk - 1)
            def _():
                finish(acc_ref[...])

    out_shape = [jax.ShapeDtypeStruct(o_shape, out_dtype)] * n_out
    res = _call(
        body, name=name, grid=(m_dim // tm, n_dim // tn, nk),
        in_specs=in_specs, out_specs=[o_spec] * n_out, out_shape=out_shape, operands=operands,
        scratch=[pltpu.VMEM((tm, tn), F32)] if nk > 1 and not in_place else [],
        semantics=("parallel", "parallel", "arbitrary"), carry=carry)
    if carry is not None:
        res, carried = res
        return (res if n_out > 1 else res[0]), carried
    return res if n_out > 1 else res[0]


def _norm_fwd_math(x, g):
    r = lax.rsqrt(jnp.mean(x * x, axis=-1, keepdims=True) + NORM_EPS)
    return (x * r) * g


def _norm_bwd_math(x, g, dy):
    r = lax.rsqrt(jnp.mean(x * x, axis=-1, keepdims=True) + NORM_EPS)
    xh = x * r
    dxh = dy * g
    dx = r * (dxh - xh * jnp.mean(dxh * xh, axis=-1, keepdims=True))
    return dx, jnp.sum(dy * xh, axis=0, keepdims=True)


def _row_tile(rows, want=256):
    t = min(rows, want)
    assert rows % t == 0
    return t


def _rmsnorm_fwd(x, g, *, name):
    s, d = x.shape
    tr = _row_tile(s)

    def body(x_ref, g_ref, h_ref, ht_ref):
        h = _norm_fwd_math(x_ref[...], g_ref[...])
        h_ref[...] = h.astype(BF16)
        ht_ref[...] = h.T.astype(BF16)

    return pl.pallas_call(
        body, name=name, grid=(s // tr,),
        in_specs=[pl.BlockSpec((tr, d), lambda i: (i, 0)), pl.BlockSpec((1, d), lambda i: (0, 0))],
        out_specs=[pl.BlockSpec((tr, d), lambda i: (i, 0)), pl.BlockSpec((d, tr), lambda i: (0, i))],
        out_shape=[jax.ShapeDtypeStruct((s, d), BF16), jax.ShapeDtypeStruct((d, s), BF16)],
        compiler_params=_params("parallel"),
    )(x, g)


def _rmsnorm_bwd(x, g, dh, dres, *, name, carry=None):
    s, d = x.shape
    tr = _row_tile(s)

    def body(x_ref, g_ref, dh_ref, dres_ref, dx_ref, dx16_ref, dg_ref):
        dx, dg = _norm_bwd_math(x_ref[...], g_ref[...], dh_ref[...])
        dx = dres_ref[...] + dx
        dx_ref[...] = dx
        dx16_ref[...] = dx.astype(BF16)

        @pl.when(pl.program_id(0) == 0)
        def _():
            dg_ref[...] = dg

        @pl.when(pl.program_id(0) > 0)
        def _():
            dg_ref[...] += dg

    row = pl.BlockSpec((tr, d), lambda i: (i, 0))
    vec = pl.BlockSpec((1, d), lambda i: (0, 0))
    return _call(
        body, name=name, grid=(s // tr,),
        in_specs=[row, vec, row, row], out_specs=[row, row, vec],
        out_shape=[jax.ShapeDtypeStruct((s, d), F32), jax.ShapeDtypeStruct((s, d), BF16),
                   jax.ShapeDtypeStruct((1, d), F32)],
        operands=(x, g, dh, dres), semantics=("arbitrary",), carry=carry)


def _outnorm_fwd(ya, yb, ga, gb, *, name):
    s, da = ya.shape
    db = yb.shape[1]
    tr = _row_tile(s)

    def body(ya_ref, yb_ref, ga_ref, gb_ref, o_ref, ot_ref):
        na = _norm_fwd_math(ya_ref[...], ga_ref[...])
        nb = _norm_fwd_math(yb_ref[...], gb_ref[...])
        o_ref[:, :da] = na.astype(BF16)
        o_ref[:, da:] = nb.astype(BF16)
        ot_ref[:da, :] = na.T.astype(BF16)
        ot_ref[da:, :] = nb.T.astype(BF16)

    return pl.pallas_call(
        body, name=name, grid=(s // tr,),
        in_specs=[pl.BlockSpec((tr, da), lambda i: (i, 0)), pl.BlockSpec((tr, db), lambda i: (i, 0)),
                  pl.BlockSpec((1, da), lambda i: (0, 0)), pl.BlockSpec((1, db), lambda i: (0, 0))],
        out_specs=[pl.BlockSpec((tr, da + db), lambda i: (i, 0)), pl.BlockSpec((da + db, tr), lambda i: (0, i))],
        out_shape=[jax.ShapeDtypeStruct((s, da + db), BF16), jax.ShapeDtypeStruct((da + db, s), BF16)],
        compiler_params=_params("parallel"),
    )(ya, yb, ga, gb)


def _outnorm_bwd(ya, yb, ga, gb, dmixed, *, name):
    s, da = ya.shape
    db = yb.shape[1]
    tr = _row_tile(s)

    def body(ya_ref, yb_ref, ga_ref, gb_ref, dm_ref, dya_ref, dyb_ref, dga_ref, dgb_ref):
        dxa, dga = _norm_bwd_math(ya_ref[...], ga_ref[...], dm_ref[:, :da])
        dxb, dgb = _norm_bwd_math(yb_ref[...], gb_ref[...], dm_ref[:, da:])
        dya_ref[...] = dxa
        dyb_ref[...] = dxb

        @pl.when(pl.program_id(0) == 0)
        def _():
            dga_ref[...] = dga
            dgb_ref[...] = dgb

        @pl.when(pl.program_id(0) > 0)
        def _():
            dga_ref[...] += dga
            dgb_ref[...] += dgb

    ra = pl.BlockSpec((tr, da), lambda i: (i, 0))
    rb = pl.BlockSpec((tr, db), lambda i: (i, 0))
    va = pl.BlockSpec((1, da), lambda i: (0, 0))
    vb = pl.BlockSpec((1, db), lambda i: (0, 0))
    return pl.pallas_call(
        body, name=name, grid=(s // tr,),
        in_specs=[ra, rb, va, vb, pl.BlockSpec((tr, da + db), lambda i: (i, 0))],
        out_specs=[ra, rb, va, vb],
        out_shape=[jax.ShapeDtypeStruct((s, da), F32), jax.ShapeDtypeStruct((s, db), F32),
                   jax.ShapeDtypeStruct((1, da), F32), jax.ShapeDtypeStruct((1, db), F32)],
        compiler_params=_params("arbitrary"),
    )(ya, yb, ga, gb, dmixed)


def _loss_head(x, g, target, *, name):
    s, d = x.shape
    tr = _row_tile(s)

    def body(x_ref, g_ref, t_ref, dx_ref, dx16_ref, loss_ref, dg_ref):
        xv, gv = x_ref[...], g_ref[...]
        err = _norm_fwd_math(xv, gv) - t_ref[...]
        part = 0.5 * jnp.sum(jnp.mean(err * err, axis=-1, keepdims=True), axis=0, keepdims=True)
        dx, dg = _norm_bwd_math(xv, gv, err * (1.0 / d))
        dx_ref[...] = dx
        dx16_ref[...] = dx.astype(BF16)
        part = jnp.broadcast_to(part, (8, 128))

        @pl.when(pl.program_id(0) == 0)
        def _():
            dg_ref[...] = dg
            loss_ref[...] = part

        @pl.when(pl.program_id(0) > 0)
        def _():
            dg_ref[...] += dg
            loss_ref[...] += part

    row = pl.BlockSpec((tr, d), lambda i: (i, 0))
    vec = pl.BlockSpec((1, d), lambda i: (0, 0))
    return pl.pallas_call(
        body, name=name, grid=(s // tr,),
        in_specs=[row, vec, row],
        out_specs=[row, row, pl.BlockSpec((8, 128), lambda i: (0, 0)), vec],
        out_shape=[jax.ShapeDtypeStruct((s, d), F32), jax.ShapeDtypeStruct((s, d), BF16),
                   jax.ShapeDtypeStruct((8, 128), F32), jax.ShapeDtypeStruct((1, d), F32)],
        compiler_params=_params("arbitrary"),
    )(x, g, target)


def _split3(x):
    hi = x.astype(BF16)
    rem = x - hi.astype(F32)
    mid = rem.astype(BF16)
    lo = (rem - mid.astype(F32)).astype(BF16)
    return hi, mid, lo


def _tri_sum(tri, x):
    hi, mid, lo = _split3(x)
    dot = functools.partial(jnp.dot, preferred_element_type=F32)
    return dot(tri, hi) + dot(tri, mid) + dot(tri, lo)


def _gates_fwd(f, fb, *, name):
    s, w = f.shape
    tb = 128
    nb = s // tb

    def body(f_ref, fb_ref, c_ref, carry):
        @pl.when(pl.program_id(0) == 0)
        def _():
            carry[...] = jnp.zeros_like(carry)

        logf = jax.nn.log_sigmoid(f_ref[...] + fb_ref[...])
        row = lax.broadcasted_iota(jnp.int32, (tb, tb), 0)
        col = lax.broadcasted_iota(jnp.int32, (tb, tb), 1)
        tri = (row >= col).astype(BF16)
        c = _tri_sum(tri, logf) + carry[...]
        c_ref[...] = c
        carry[...] = c[tb - 1:tb, :]

    return pl.pallas_call(
        body, name=name, grid=(nb,),
        in_specs=[pl.BlockSpec((tb, w), lambda i: (i, 0)), pl.BlockSpec((1, w), lambda i: (0, 0))],
        out_specs=pl.BlockSpec((tb, w), lambda i: (i, 0)),
        out_shape=jax.ShapeDtypeStruct((s, w), F32),
        scratch_shapes=[pltpu.VMEM((1, w), F32)],
        compiler_params=_params("arbitrary"),
    )(f, fb)


def _gates_bwd(f, fb, dc, *, name):
    s, w = f.shape
    tb = 128
    nb = s // tb

    def body(f_ref, fb_ref, dc_ref, df_ref, dfb_ref, carry):
        @pl.when(pl.program_id(0) == 0)
        def _():
            carry[...] = jnp.zeros_like(carry)
            dfb_ref[...] = jnp.zeros_like(dfb_ref)

        row = lax.broadcasted_iota(jnp.int32, (tb, tb), 0)
        col = lax.broadcasted_iota(jnp.int32, (tb, tb), 1)
        tri = (row <= col).astype(BF16)
        dlogf = _tri_sum(tri, dc_ref[...]) + carry[...]
        carry[...] = dlogf[0:1, :]
        df = dlogf * jax.nn.sigmoid(-(f_ref[...] + fb_ref[...]))
        df_ref[...] = df
        dfb_ref[...] += jnp.sum(df, axis=0, keepdims=True)

    rev = pl.BlockSpec((tb, w), lambda i: (nb - 1 - i, 0))
    vec = pl.BlockSpec((1, w), lambda i: (0, 0))
    return pl.pallas_call(
        body, name=name, grid=(nb,),
        in_specs=[rev, vec, rev], out_specs=[rev, vec],
        out_shape=[jax.ShapeDtypeStruct((s, w), F32), jax.ShapeDtypeStruct((1, w), F32)],
        scratch_shapes=[pltpu.VMEM((1, w), F32)],
        compiler_params=_params("arbitrary"),
    )(f, fb, dc)


def _nt(a, b):
    return lax.dot_general(a, b, (((1,), (1,)), ((), ())), preferred_element_type=F32)


def _tn(a, b):
    return lax.dot_general(a, b, (((0,), (0,)), ((), ())), preferred_element_type=F32)


def _nn(a, b):
    return jnp.dot(a, b, preferred_element_type=F32)


DIL_GROUP = 8
FOX_TILE = 1024
LOG2E = 1.4426950408889634


def _causal_pairs(nt, key_major):
    if key_major:
        pairs = [(q, k) for k in range(nt) for q in range(k, nt)]
    else:
        pairs = [(q, k) for q in range(nt) for k in range(q + 1)]
    return (jnp.asarray([p[0] for p in pairs], jnp.int32), jnp.asarray([p[1] for p in pairs], jnp.int32))


def _fox_fwd(qkv, c_row, ck_lanes, n_heads, *, name, carry=None):
    s = qkv.shape[0]
    e = HEAD_DIM
    t = min(FOX_TILE, s)
    nt = s // t
    scale2 = e ** -0.5 * LOG2E
    lanes = 128
    q_tab, k_tab = _causal_pairs(nt, key_major=False)

    def body(q_tab, k_tab, q_ref, k_ref, v_ref, cq_ref, ck_ref, o_ref, lse_ref,
             m_scr, l_scr, acc_scr, s_scr, p_scr):
        pair = pl.program_id(1)
        qi, ki = q_tab[pair], k_tab[pair]

        @pl.when(ki == 0)
        def _():
            m_scr[...] = jnp.full_like(m_scr, NEG_INF)
            l_scr[...] = jnp.zeros_like(l_scr)
            acc_scr[...] = jnp.zeros_like(acc_scr)

        def update(diagonal):
            s_scr[...] = _nt(k_ref[...], q_ref[...])
            ck2 = ck_ref[...]
            for c0 in range(0, t, lanes):
                cols = pl.ds(c0, lanes)

                def logits2():
                    x = s_scr[:, cols] * scale2 - ck2
                    if diagonal:
                        key = lax.broadcasted_iota(jnp.int32, x.shape, 0)
                        qry = c0 + lax.broadcasted_iota(jnp.int32, x.shape, 1)
                        x = jnp.where(key <= qry, x, NEG_INF)
                    return x

                m_old = m_scr[:, cols]
                m_new = jnp.maximum(m_old, jnp.max(logits2(), axis=0, keepdims=True))
                p = jnp.exp2(logits2() - m_new)
                alpha = jnp.exp2(m_old - m_new)
                l_scr[:, cols] = alpha * l_scr[:, cols] + jnp.sum(p, axis=0, keepdims=True)
                m_scr[:, cols] = m_new
                acc_scr[:, cols] = alpha * acc_scr[:, cols]
                p_scr[:, cols] = p.astype(BF16)
            acc_scr[...] += _tn(v_ref[...], p_scr[...])

        @pl.when(ki < qi)
        def _():
            update(False)

        @pl.when(ki == qi)
        def _():
            update(True)
            o_ref[...] = (acc_scr[...] / l_scr[...]).T
            lse_ref[...] = (m_scr[...] + jnp.log2(l_scr[...])) * (1.0 / LOG2E) + cq_ref[...]

    h_ = n_heads
    return _call(
        body, name=name, grid=(h_, int(q_tab.shape[0])), tables=(q_tab, k_tab),
        in_specs=[
            pl.BlockSpec((t, e), lambda h, p, qt, kt: (qt[p], h)),
            pl.BlockSpec((t, e), lambda h, p, qt, kt: (kt[p], h_ + h)),
            pl.BlockSpec((t, e), lambda h, p, qt, kt: (kt[p], 2 * h_ + h)),
            pl.BlockSpec((None, 1, t), lambda h, p, qt, kt: (h, 0, qt[p])),
            pl.BlockSpec((None, t, lanes), lambda h, p, qt, kt: (h, kt[p], 0)),
        ],
        out_specs=[pl.BlockSpec((t, e), lambda h, p, qt, kt: (qt[p], h)),
                   pl.BlockSpec((None, 1, t), lambda h, p, qt, kt: (h, 0, qt[p]))],
        out_shape=[jax.ShapeDtypeStruct((s, h_ * e), F32), jax.ShapeDtypeStruct((h_, 1, s), F32)],
        operands=(qkv, qkv, qkv, c_row, ck_lanes),
        scratch=[pltpu.VMEM((1, t), F32), pltpu.VMEM((1, t), F32), pltpu.VMEM((e, t), F32),
                 pltpu.VMEM((t, t), F32), pltpu.VMEM((t, t), BF16)],
        semantics=("parallel", "arbitrary"), carry=carry)


def _fox_bwd(qkv, c_row, ck_lanes, lse, y, dy, n_heads, *, name, carry=None):
    s = qkv.shape[0]
    e = HEAD_DIM
    t = min(FOX_TILE, s)
    nt = s // t
    scale = e ** -0.5
    scale2 = scale * LOG2E
    lanes = 128
    q_tab, k_tab = _causal_pairs(nt, key_major=False)
    n_pairs = int(q_tab.shape[0])

    def body(q_tab, k_tab, q_ref, k_ref, v_ref, c_ref, ck_ref, lse_ref, y_ref, dy_ref,
             dq_ref, dk_ref, dv_ref, dcq_ref, dck_ref,
             dq_scr, dk_scr, dv_scr, dck_scr, s_scr, dp_scr, p_scr, ds_scr, do_scr, delta_scr, shift_scr):
        pair = pl.program_id(1)
        qi, ki = q_tab[pair], k_tab[pair]

        @pl.when(pair == 0)
        def _():
            dk_scr[...] = jnp.zeros_like(dk_scr)
            dv_scr[...] = jnp.zeros_like(dv_scr)
            dck_scr[...] = jnp.zeros_like(dck_scr)

        @pl.when(ki == 0)
        def _():
            do = dy_ref[...]
            delta_scr[...] = lax.dot_general(jnp.ones((8, e), F32), do * y_ref[...], (((1,), (1,)), ((), ())),
                                             precision=lax.Precision.HIGHEST, preferred_element_type=F32)
            shift_scr[...] = (lse_ref[...] - c_ref[...]) * LOG2E
            do_scr[...] = do.astype(BF16)
            dq_scr[...] = jnp.zeros_like(dq_scr)
            dcq_ref[...] = jnp.zeros_like(dcq_ref)

        def update(diagonal):
            s_scr[...] = _nt(k_ref[...], q_ref[...])
            dp_scr[...] = _nt(v_ref[...], do_scr[...])
            ck2 = ck_ref[...]
            k_rows = pl.ds(pl.multiple_of(ki * t, t), t)
            for c0 in range(0, t, lanes):
                cols = pl.ds(c0, lanes)
                x = s_scr[:, cols] * scale2 - ck2
                if diagonal:
                    key = lax.broadcasted_iota(jnp.int32, x.shape, 0)
                    qry = c0 + lax.broadcasted_iota(jnp.int32, x.shape, 1)
                    x = jnp.where(key <= qry, x, NEG_INF)
                p = jnp.exp2(x - shift_scr[:, cols])
                ds = p * (dp_scr[:, cols] - delta_scr[0:1, cols])
                dcq_ref[:, cols] += jnp.sum(ds, axis=0, keepdims=True)
                dck_scr[k_rows, :] += ds
                p_scr[:, cols] = p.astype(BF16)
                ds_scr[:, cols] = ds.astype(BF16)
            dv_scr[k_rows, :] += _nn(p_scr[...], do_scr[...])
            dk_scr[k_rows, :] += scale * _nn(ds_scr[...], q_ref[...])
            dq_scr[...] += scale * _tn(ds_scr[...], k_ref[...])

        @pl.when(ki < qi)
        def _():
            update(False)

        @pl.when(ki == qi)
        def _():
            update(True)
            dq_ref[...] = dq_scr[...].astype(BF16)

        @pl.when(pair == n_pairs - 1)
        def _():
            dk_ref[...] = dk_scr[...].astype(BF16)
            dv_ref[...] = dv_scr[...].astype(BF16)
            dck_ref[...] = -lax.dot_general(jnp.ones((8, lanes), F32), dck_scr[...], (((1,), (1,)), ((), ())),
                                            precision=lax.Precision.HIGHEST, preferred_element_type=F32)[0:1]

    h_ = n_heads
    q_row = pl.BlockSpec((None, 1, t), lambda h, p, qt, kt: (h, 0, qt[p]))
    q_blk = pl.BlockSpec((t, e), lambda h, p, qt, kt: (qt[p], h))
    whole = pl.BlockSpec((s, e), lambda h, p, qt, kt: (0, h))
    return _call(
        body, name=name, grid=(h_, n_pairs), tables=(q_tab, k_tab),
        in_specs=[
            q_blk,
            pl.BlockSpec((t, e), lambda h, p, qt, kt: (kt[p], h_ + h)),
            pl.BlockSpec((t, e), lambda h, p, qt, kt: (kt[p], 2 * h_ + h)),
            q_row,
            pl.BlockSpec((None, t, lanes), lambda h, p, qt, kt: (h, kt[p], 0)),
            q_row, q_blk, q_blk,
        ],
        out_specs=[q_blk, whole, whole, q_row,
                   pl.BlockSpec((None, 1, s), lambda h, p, qt, kt: (h, 0, 0))],
        out_shape=[jax.ShapeDtypeStruct((s, h_ * e), BF16)] * 3
        + [jax.ShapeDtypeStruct((h_, 1, s), F32), jax.ShapeDtypeStruct((h_, 1, s), F32)],
        operands=(qkv, qkv, qkv, c_row, ck_lanes, lse, y, dy),
        scratch=[pltpu.VMEM((t, e), F32), pltpu.VMEM((s, e), F32), pltpu.VMEM((s, e), F32),
                 pltpu.VMEM((s, lanes), F32), pltpu.VMEM((t, t), F32), pltpu.VMEM((t, t), F32), pltpu.VMEM((t, t), BF16),
                 pltpu.VMEM((t, t), BF16), pltpu.VMEM((t, e), BF16), pltpu.VMEM((8, t), F32),
                 pltpu.VMEM((1, t), F32)],
        semantics=("parallel", "arbitrary"), carry=carry)


def _rel_bucket_table(dilation, span):
    dist = np.arange(span + 1, dtype=np.int64) * dilation
    max_exact = REL_BUCKETS // 2
    d = np.maximum(dist.astype(np.float32), np.float32(1.0))
    large = max_exact + (np.log(d / np.float32(max_exact)) / np.float32(math.log(REL_MAX_DISTANCE / max_exact))
                         * np.float32(REL_BUCKETS - max_exact)).astype(np.int32)
    large = np.minimum(large, REL_BUCKETS - 1)
    return np.where(dist < max_exact, dist, large)


def _bucket_bands(dilation, span):
    table = _rel_bucket_table(dilation, span)
    bands = []
    for n, b in enumerate(table):
        if bands and bands[-1][0] == int(b):
            bands[-1][2] = n
        else:
            assert not any(bb[0] == int(b) for bb in bands)
            bands.append([int(b), n, n])
    return [tuple(b) for b in bands]


def _steps_back():
    i = lax.broadcasted_iota(jnp.int32, (DIL_BLOCK, DIL_BLOCK), 0)
    j = lax.broadcasted_iota(jnp.int32, (DIL_BLOCK, DIL_BLOCK), 1)
    return i - j, DIL_BLOCK + i - j


def _bias_tiles(rb_ref, h, bands, span):
    n_cur, n_prev = _steps_back()
    t_cur = jnp.zeros((DIL_BLOCK, DIL_BLOCK), F32)
    t_prev = jnp.zeros((DIL_BLOCK, DIL_BLOCK), F32)
    for b, lo, hi in bands:
        val = rb_ref[b, h]
        t_cur = jnp.where((n_cur >= lo) & (n_cur <= hi), val, t_cur)
        t_prev = jnp.where((n_prev >= lo) & (n_prev <= hi), val, t_prev)
    t_cur = jnp.where(n_cur >= 0, t_cur, NEG_INF)
    t_prev = jnp.where(n_prev <= span, t_prev, NEG_INF)
    return t_cur, t_prev


def _dil_rows(start, dilation):
    return pl.ds(start, DIL_BLOCK, stride=dilation) if dilation > 1 else pl.ds(start, DIL_BLOCK)


def _dil_block_groups(s, dilation, run_group):
    group = dilation * DIL_BLOCK
    per_residue = s // group

    def block(r, n):
        if isinstance(n, int):
            return n * group + r, max(n - 1, 0) * group + r, n == 0
        return (pl.multiple_of(n * group, DIL_BLOCK) + r,
                pl.multiple_of(jnp.maximum(n - 1, 0) * group, DIL_BLOCK) + r, n == 0)

    if per_residue >= DIL_GROUP:
        assert per_residue % DIL_GROUP == 0
        for r in range(dilation):
            def trip(it, carry, r=r):
                run_group([block(r, it * DIL_GROUP + j) for j in range(DIL_GROUP)])
                return carry
            lax.fori_loop(0, per_residue // DIL_GROUP, trip, 0)
    else:
        residues = DIL_GROUP // per_residue
        assert residues * per_residue == DIL_GROUP and dilation % residues == 0
        for r0 in range(0, dilation, residues):
            run_group([block(r, n) for r in range(r0, r0 + residues) for n in range(per_residue)])


def _dil_fwd(qkv, rel_bias, n_heads, col0, *, name, carry=None):
    s = qkv.shape[0]
    e = HEAD_DIM
    scale = e ** -0.5
    n_pat = len(DIL_PATTERNS)
    for window, d in DIL_PATTERNS:
        assert window // d == DIL_BLOCK and s % (d * DIL_BLOCK) == 0
    bands = [_bucket_bands(d, w // d) for w, d in DIL_PATTERNS]

    def body(rb_ref, q_ref, k_ref, v_ref, y_ref, lse_ref, qf, kf, vf, *scr):
        o_scr, l_scr = scr[:n_pat], scr[n_pat:]
        h = pl.program_id(0)
        qf[...] = q_ref[...].astype(F32)
        kf[...] = k_ref[...].astype(F32)
        vf[...] = v_ref[...].astype(F32)
        for pi, (window, d) in enumerate(DIL_PATTERNS):
            t_cur, t_prev = _bias_tiles(rb_ref, h, bands[pi], window // d)

            def run_group(blocks, d=d, pi=pi, t_cur=t_cur, t_prev=t_prev):
                rows = [(_dil_rows(q0, d), _dil_rows(p0, d)) for q0, p0, _ in blocks]
                qb = [qf[cur, :].astype(BF16) for cur, _ in rows]
                s_c = [_nt(q, kf[cur, :].astype(BF16)) for q, (cur, _) in zip(qb, rows)]
                s_p = [_nt(q, kf[prev, :].astype(BF16)) for q, (_, prev) in zip(qb, rows)]
                s_c = [x * scale + t_cur for x in s_c]
                s_p = [x * scale + (t_prev + jnp.where(first, NEG_INF, 0.0)) for x, (_, _, first) in zip(s_p, blocks)]
                m = [jnp.max(jnp.maximum(a, b), axis=-1, keepdims=True) for a, b in zip(s_c, s_p)]
                p_c = [jnp.exp(a - mm) for a, mm in zip(s_c, m)]
                p_p = [jnp.exp(b - mm) for b, mm in zip(s_p, m)]
                l = [jnp.sum(a + b, axis=-1, keepdims=True) for a, b in zip(p_c, p_p)]
                o = [_nn(a.astype(BF16), vf[cur, :].astype(BF16)) + _nn(b.astype(BF16), vf[prev, :].astype(BF16))
                     for a, b, (cur, prev) in zip(p_c, p_p, rows)]
                for (cur, _), oo, ll, mm in zip(rows, o, l, m):
                    o_scr[pi][cur, :] = oo / ll
                    l_scr[pi][cur, :] = mm + jnp.log(ll)

            _dil_block_groups(s, d, run_group)
        lses = [l_scr[pi][...] for pi in range(n_pat)]
        m = functools.reduce(jnp.maximum, lses)
        ws = [jnp.exp(l - m) for l in lses]
        tot = functools.reduce(jnp.add, ws)
        y = functools.reduce(jnp.add, [w * o_scr[pi][...] for pi, w in enumerate(ws)])
        y_ref[...] = y / tot
        lse_ref[...] = m + jnp.log(tot)

    h_ = n_heads
    return _call(
        body, name=name, grid=(h_,),
        in_specs=[pl.BlockSpec(memory_space=pltpu.SMEM),
                  pl.BlockSpec((s, e), lambda h: (0, col0 + h)),
                  pl.BlockSpec((s, e), lambda h: (0, col0 + h_ + h)),
                  pl.BlockSpec((s, e), lambda h: (0, col0 + 2 * h_ + h))],
        out_specs=[pl.BlockSpec((s, e), lambda h: (0, h)), pl.BlockSpec((None, s, 1), lambda h: (h, 0, 0))],
        out_shape=[jax.ShapeDtypeStruct((s, h_ * e), F32), jax.ShapeDtypeStruct((h_, s, 1), F32)],
        operands=(rel_bias, qkv, qkv, qkv),
        scratch=[pltpu.VMEM((s, e), F32)] * 3 + [pltpu.VMEM((s, e), F32)] * n_pat
        + [pltpu.VMEM((s, 1), F32)] * n_pat,
        semantics=("parallel",), carry=carry)


def _dil_bwd(qkv, rel_bias, lse, y, dy, n_heads, col0, *, name, carry=None):
    s = qkv.shape[0]
    e = HEAD_DIM
    scale = e ** -0.5
    bands = [_bucket_bands(d, w // d) for w, d in DIL_PATTERNS]

    def body(rb_ref, q_ref, k_ref, v_ref, lse_ref, y_ref, dy_ref, dq_ref, dk_ref, dv_ref, drb_ref,
             qf, kf, vf, dqf, dkf, dvf, delta, dt_cur, dt_prev):
        h = pl.program_id(0)
        qf[...] = q_ref[...].astype(F32)
        kf[...] = k_ref[...].astype(F32)
        vf[...] = v_ref[...].astype(F32)
        dqf[...] = jnp.zeros_like(dqf)
        dkf[...] = jnp.zeros_like(dkf)
        dvf[...] = jnp.zeros_like(dvf)
        delta[...] = jnp.sum(dy_ref[...] * y_ref[...], axis=-1, keepdims=True)
        lane = lax.broadcasted_iota(jnp.int32, (1, 128), 1)
        drb = jnp.zeros((1, 128), F32)
        n_cur, n_prev = _steps_back()
        for pi, (window, d) in enumerate(DIL_PATTERNS):
            t_cur, t_prev = _bias_tiles(rb_ref, h, bands[pi], window // d)
            dt_cur[...] = jnp.zeros_like(dt_cur)
            dt_prev[...] = jnp.zeros_like(dt_prev)

            def run_group(blocks, d=d, t_cur=t_cur, t_prev=t_prev):
                rows = [(_dil_rows(q0, d), _dil_rows(p0, d)) for q0, p0, _ in blocks]
                qb = [qf[cur, :].astype(BF16) for cur, _ in rows]
                kc = [kf[cur, :].astype(BF16) for cur, _ in rows]
                kp = [kf[prev, :].astype(BF16) for _, prev in rows]
                vc = [vf[cur, :].astype(BF16) for cur, _ in rows]
                vp = [vf[prev, :].astype(BF16) for _, prev in rows]
                do = [dy_ref[cur, :].astype(BF16) for cur, _ in rows]
                lse_b = [lse_ref[cur, :] for cur, _ in rows]
                delta_b = [delta[cur, :] for cur, _ in rows]
                s_c = [_nt(q, k) for q, k in zip(qb, kc)]
                s_p = [_nt(q, k) for q, k in zip(qb, kp)]
                dp_c = [_nt(g, v) for g, v in zip(do, vc)]
                dp_p = [_nt(g, v) for g, v in zip(do, vp)]
                p_c = [jnp.exp(x * scale + t_cur - ls) for x, ls in zip(s_c, lse_b)]
                p_p = [jnp.exp(x * scale + (t_prev + jnp.where(first, NEG_INF, 0.0)) - ls)
                       for x, ls, (_, _, first) in zip(s_p, lse_b, blocks)]
                ds_c = [p * (g - dl) for p, g, dl in zip(p_c, dp_c, delta_b)]
                ds_p = [p * (g - dl) for p, g, dl in zip(p_p, dp_p, delta_b)]
                dt_cur[...] += functools.reduce(jnp.add, ds_c)
                dt_prev[...] += functools.reduce(jnp.add, ds_p)
                p_c, p_p = [p.astype(BF16) for p in p_c], [p.astype(BF16) for p in p_p]
                ds_c, ds_p = [x.astype(BF16) for x in ds_c], [x.astype(BF16) for x in ds_p]
                dq = [scale * (_nn(a, k1) + _nn(b, k2)) for a, b, k1, k2 in zip(ds_c, ds_p, kc, kp)]
                dk_c = [scale * _tn(a, q) for a, q in zip(ds_c, qb)]
                dv_c = [_tn(p, g) for p, g in zip(p_c, do)]
                dk_p = [scale * _tn(b, q) for b, q in zip(ds_p, qb)]
                dv_p = [_tn(p, g) for p, g in zip(p_p, do)]
                for i, (cur, prev) in enumerate(rows):
                    dqf[cur, :] += dq[i]
                    dkf[cur, :] += dk_c[i]
                    dvf[cur, :] += dv_c[i]
                    dkf[prev, :] += dk_p[i]
                    dvf[prev, :] += dv_p[i]

            _dil_block_groups(s, d, run_group)
            dtc, dtp = dt_cur[...], dt_prev[...]
            for b, lo, hi in bands[pi]:
                tot = (jnp.sum(jnp.where((n_cur >= lo) & (n_cur <= hi), dtc, 0.0))
                       + jnp.sum(jnp.where((n_prev >= lo) & (n_prev <= hi), dtp, 0.0)))
                drb = drb + jnp.where(lane == b, tot, 0.0)
        dq_ref[...] = dqf[...].astype(BF16)
        dk_ref[...] = dkf[...].astype(BF16)
        dv_ref[...] = dvf[...].astype(BF16)
        drb_ref[...] = drb

    h_ = n_heads
    col = pl.BlockSpec((s, e), lambda h: (0, h))
    return _call(
        body, name=name, grid=(h_,),
        in_specs=[pl.BlockSpec(memory_space=pltpu.SMEM),
                  pl.BlockSpec((s, e), lambda h: (0, col0 + h)),
                  pl.BlockSpec((s, e), lambda h: (0, col0 + h_ + h)),
                  pl.BlockSpec((s, e), lambda h: (0, col0 + 2 * h_ + h)),
                  pl.BlockSpec((None, s, 1), lambda h: (h, 0, 0)), col, col],
        out_specs=[col, col, col, pl.BlockSpec((None, 1, 128), lambda h: (h, 0, 0))],
        out_shape=[jax.ShapeDtypeStruct((s, h_ * e), BF16)] * 3 + [jax.ShapeDtypeStruct((h_, 1, 128), F32)],
        operands=(rel_bias, qkv, qkv, qkv, lse, y, dy),
        scratch=[pltpu.VMEM((s, e), F32)] * 6 + [pltpu.VMEM((s, 1), F32)]
        + [pltpu.VMEM((DIL_BLOCK, DIL_BLOCK), F32)] * 2,
        semantics=("parallel",), carry=carry)


def _place():
    x, y, c = lax.axis_index("x"), lax.axis_index("y"), lax.axis_index("c")
    chips = [(1 - x, y), (x, 1 - y), (1 - x, 1 - y)]
    return x, y, c, chips


def _block_index(px, py, pc):
    return 4 * px + 2 * py + pc


def _gather_task(blocks, *, row0=0, rows=None, into=None):
    n = len(blocks)
    part = blocks[0].shape[0]
    rows = rows or part

    def copies(ins, outs, sems):
        send_sems, recv_sems, local_sems = sems
        x, y, c, chips = _place()
        me, sibling = (x, y, c), (x, y, 1 - c)

        def place_of(a, block):
            slot = outs[a].at[_block_index(*block)]
            return slot if part == rows else slot.at[pl.ds(row0, part)]

        def copy(a, k, block, to, src=None):
            slot = place_of(a, block)
            return pltpu.make_async_remote_copy(
                src_ref=slot if src is None else src, dst_ref=slot,
                send_sem=send_sems.at[a, k], recv_sem=recv_sems.at[a, k],
                device_id=to, device_id_type=MESH)

        local = lambda a: pltpu.make_async_copy(ins[a], place_of(a, me), local_sems.at[a])
        return copy, local, me, sibling, c, chips

    def start(ins, outs, sems):
        copy, local, me, sibling, c, chips = copies(ins, outs, sems)
        for a in range(n):
            local(a).start()
            copy(a, 0, me, sibling, src=ins[a]).start()
            for j, chip in enumerate(chips):
                copy(a, 1 + j, me, (*chip, c), src=ins[a]).start()

    def forward(ins, outs, sems):
        copy, local, me, sibling, c, chips = copies(ins, outs, sems)
        for a in range(n):
            for j, chip in enumerate(chips):
                copy(a, 1 + j, (*chip, c), me).wait_recv()
                copy(a, 4 + j, (*chip, c), sibling).start()

    def finish(ins, outs, sems):
        copy, local, me, sibling, c, chips = copies(ins, outs, sems)
        for a in range(n):
            copy(a, 0, sibling, me).wait_recv()
            for j, chip in enumerate(chips):
                copy(a, 4 + j, (*chip, 1 - c), me).wait_recv()
        for a in range(n):
            for k in range(7):
                copy(a, k, me, sibling, src=ins[a]).wait_send()
            local(a).wait()

    return _Carried(
        list(blocks) + list(into or []),
        [jax.ShapeDtypeStruct((N_DEV, rows) + b.shape[1:], b.dtype) for b in blocks],
        [pltpu.SemaphoreType.DMA((n, 7)), pltpu.SemaphoreType.DMA((n, 7)), pltpu.SemaphoreType.DMA((n,))],
        start, forward, finish, aliases={n + a: a for a in range(n)} if into else None)


def _exchange_task(arrays, n_slots, route):
    n = len(arrays)

    def copies(ins, outs, sems):
        send_sems, recv_sems = sems
        out = []
        for a in range(n):
            for j in range(n_slots):
                src, to = route(j)
                out.append(pltpu.make_async_remote_copy(
                    src_ref=ins[a].at[src], dst_ref=outs[a].at[j],
                    send_sem=send_sems.at[a, j], recv_sem=recv_sems.at[a, j],
                    device_id=to, device_id_type=MESH))
        return out

    def start(ins, outs, sems):
        for cp in copies(ins, outs, sems):
            cp.start()

    def finish(ins, outs, sems):
        for cp in copies(ins, outs, sems):
            cp.wait()

    return _Carried(
        arrays, [jax.ShapeDtypeStruct((n_slots,) + g.shape[1:], g.dtype) for g in arrays],
        [pltpu.SemaphoreType.DMA((n, n_slots)), pltpu.SemaphoreType.DMA((n, n_slots))],
        start, lambda ins, outs, sems: None, finish)


def _sibling_task(grads):
    def route(q):
        x, y, c, _ = _place()
        return 2 * q + (1 - c), (x, y, 1 - c)
    return _exchange_task(grads, 4, route)


def _chips_task(parts):
    def route(j):
        x, y, c, chips = _place()
        px, py = chips[j]
        return 2 * px + py, (px, py, c)
    return _exchange_task(parts, 3, route)


def _chip_partials(grad, got, place, *, name):
    _, r, cdim = grad.shape
    tr = _row_tile(r, 512)

    def body(place_ref, g_ref, s_ref, o_ref):
        o_ref[...] = (g_ref[...] + s_ref[...]).astype(BF16)

    return pl.pallas_call(
        body, name=name,
        grid_spec=pltpu.PrefetchScalarGridSpec(
            num_scalar_prefetch=1, grid=(4, r // tr),
            in_specs=[pl.BlockSpec((None, tr, cdim), lambda q, i, pos: (2 * q + pos[2], i, 0)),
                      pl.BlockSpec((None, tr, cdim), lambda q, i, pos: (q, i, 0))],
            out_specs=pl.BlockSpec((None, tr, cdim), lambda q, i, pos: (q, i, 0))),
        out_shape=jax.ShapeDtypeStruct((4, r, cdim), BF16),
        compiler_params=_params("parallel", "parallel"),
    )(place, grad, got)


def _all_reduce_small(v, *, name):
    r, w = v.shape

    def body(v_ref, o_ref, buf, send_sems, recv_sems):
        x, y, c, _ = _place()
        me = _block_index(x, y, c)
        buf[me] = v_ref[...]
        copies = []
        for k in range(1, N_DEV):
            fx, fy, fc = (k >> 2) & 1, (k >> 1) & 1, k & 1
            peer = (x ^ fx, y ^ fy, c ^ fc)
            cp = pltpu.make_async_remote_copy(
                src_ref=v_ref, dst_ref=buf.at[me],
                send_sem=send_sems.at[k - 1], recv_sem=recv_sems.at[k - 1],
                device_id=peer, device_id_type=MESH)
            cp.start()
            copies.append(cp)
        for cp in copies:
            cp.wait()
        acc = buf[0]
        for p in range(1, N_DEV):
            acc = acc + buf[p]
        o_ref[...] = acc

    vmem = pl.BlockSpec(memory_space=pltpu.VMEM)
    return pl.pallas_call(
        body, name=name, in_specs=[vmem], out_specs=vmem,
        out_shape=jax.ShapeDtypeStruct((r, w), F32),
        scratch_shapes=[pltpu.VMEM((N_DEV, r, w), F32), pltpu.SemaphoreType.DMA((N_DEV - 1,)),
                        pltpu.SemaphoreType.DMA((N_DEV - 1,))],
    )(v)


def _adamw_math(w, g, m, v):
    m = ADAM_B1 * m + (1.0 - ADAM_B1) * g
    v = ADAM_B2 * v + (1.0 - ADAM_B2) * (g * g)
    m_hat = m / (1.0 - ADAM_B1 ** ADAM_STEP)
    v_hat = v / (1.0 - ADAM_B2 ** ADAM_STEP)
    delta = -ADAM_LR * (m_hat / (jnp.sqrt(v_hat) + ADAM_EPS) + ADAM_WD * w)
    return delta, m, v


def _adamw_sharded(w, m, v, part, got, place, *, name, carry=None):
    n_l, r, cdim = w.shape
    tr = _row_tile(r, 256)

    def body(place_ref, *refs):
        w_ref, m_ref, v_ref = refs[:3]
        p_refs, g_refs = refs[3:3 + n_l], refs[3 + n_l:3 + 2 * n_l]
        g_out, d_out, m_out, v_out = refs[3 + 2 * n_l:]
        for l in range(n_l):
            @pl.when(pl.program_id(0) == l)
            def _(l=l):
                g = p_refs[l][...].astype(F32)
                for j in range(3):
                    g = g + g_refs[l][j].astype(F32)
                delta, m_new, v_new = _adamw_math(w_ref[...], g, m_ref[...], v_ref[...])
                g_out[...] = g
                d_out[...] = delta
                m_out[...] = m_new
                v_out[...] = v_new

    shard = pl.BlockSpec((None, tr, cdim), lambda l, i, pos: (l, i, 0))
    p_specs = [pl.BlockSpec((None, tr, cdim), lambda l, i, pos: (2 * pos[0] + pos[1], i, 0))] * n_l
    g_specs = [pl.BlockSpec((3, tr, cdim), lambda l, i, pos: (0, i, 0))] * n_l
    return _call(
        body, name=name, grid=(n_l, r // tr), tables=(place,),
        in_specs=[shard] * 3 + p_specs + g_specs, out_specs=[shard] * 4,
        out_shape=[jax.ShapeDtypeStruct(w.shape, F32)] * 4,
        operands=(w, m, v, *part, *got), semantics=("arbitrary", "parallel"), carry=carry)


def _adamw_small(w, g, m, v, *, name):
    def body(w_ref, g_ref, m_ref, v_ref, d_out, m_out, v_out):
        delta, m_new, v_new = _adamw_math(w_ref[...], g_ref[...], m_ref[...], v_ref[...])
        d_out[...] = delta
        m_out[...] = m_new
        v_out[...] = v_new

    vmem = pl.BlockSpec(memory_space=pltpu.VMEM)
    return pl.pallas_call(
        body, name=name, in_specs=[vmem] * 4, out_specs=[vmem] * 3,
        out_shape=[jax.ShapeDtypeStruct(w.shape, F32)] * 3,
    )(w, g, m, v)


def _pad_lanes(a, width=128):
    return jnp.pad(a, ((0, 0), (0, width - a.shape[1])))


def _train_step(x, target, small, shards, place):
    w_in, w_out, w_mlp_in, w_mlp_out = shards
    depth, d_model, in_shard = w_in.shape
    n_heads = small["forget_b"].shape[1]
    dh = n_heads * HEAD_DIM
    d_ff = w_mlp_in.shape[2] * N_DEV
    row = lambda a: a.reshape(1, -1)
    block = {"in": w_in.astype(BF16), "out": w_out.astype(BF16), "mi": w_mlp_in.astype(BF16),
             "mo": w_mlp_out.astype(BF16)}
    whole = {}

    gate_lo, gate_hi = 3 * dh, 3 * dh + n_heads

    def shard_columns(g, lo, hi):
        pieces = []
        while lo < hi:
            p, a = divmod(lo, in_shard)
            b = min(in_shard, a + hi - lo)
            pieces.append(g[p][:, a:b])
            lo += b - a
        return pieces

    def in_columns(d_qkv, d_gate, lo, hi):
        pieces = []
        if lo < gate_lo:
            pieces.append(d_qkv[:, lo:min(hi, gate_lo)])
        if lo < gate_hi and hi > gate_lo:
            pieces.append(d_gate[:, max(lo, gate_lo) - gate_lo:min(hi, gate_hi) - gate_lo])
        if hi > gate_hi:
            pieces.append(d_qkv[:, max(lo, gate_hi) - n_heads:hi - n_heads])
        return pieces

    def unpack(kind, l, g):
        if kind == "in":
            whole["qkv", l] = jnp.concatenate(
                shard_columns(g, 0, gate_lo) + shard_columns(g, gate_hi, N_DEV * in_shard), axis=1)
            whole["f", l] = _pad_lanes(jnp.concatenate(shard_columns(g, gate_lo, gate_hi), axis=1))
        elif kind == "out":
            whole["out", l] = g.reshape(2 * dh, d_model)
        elif kind == "mi":
            whole["mi", l] = g
        else:
            whole["mo", l] = g.reshape(d_ff, d_model)

    half_done = {}

    def gathering(kind, l, part):
        blk = block[kind][l]
        rows = blk.shape[0]
        if part == "all":
            return _gather_task([blk]), lambda got: unpack(kind, l, got[0])
        if part == "first":
            return (_gather_task([blk[:rows // 2]], rows=rows),
                    lambda got: half_done.__setitem__((kind, l), got[0]))
        return (_gather_task([blk[rows // 2:]], row0=rows // 2, rows=rows, into=[half_done.pop((kind, l))]),
                lambda got: unpack(kind, l, got[0]))

    def carried(call, *wanted):
        wanted = [gathering(*w) for w in wanted if w[1] < depth]
        if not wanted:
            return call()
        res, got = call(carry=_join([task for task, _ in wanted]))
        for (_, done), g in zip(wanted, got):
            done([g])
        return res

    (g,) = _run_carried(_gather_task([block["in"][0]]), name="gather_w_in_0")
    unpack("in", 0, g)

    saved = []
    for l in range(depth):
        h1, h1_t = _rmsnorm_fwd(x, row(small["norm1_g"][l]), name=f"norm1_fwd_{l}")
        qkv = carried(functools.partial(_matmul, h1, whole["qkv", l], out_dtype=BF16, name=f"qkv_proj_{l}"),
                      ("mi", l, "first"))
        f = _matmul(h1, whole["f", l], name=f"gate_proj_{l}")
        fb = _pad_lanes(row(small["forget_b"][l]))
        c = _gates_fwd(f, fb, name=f"gates_fwd_{l}")
        c_heads = c[:, :n_heads].T
        cq = c_heads[:, None, :]
        ck = jnp.broadcast_to(c_heads[:, :, None] * LOG2E, c_heads.shape + (128,))
        ya, lse_a = carried(functools.partial(_fox_fwd, qkv, cq, ck, n_heads, name=f"fox_fwd_{l}"),
                            ("mi", l, "second"), ("out", l, "all"))
        yb, lse_b = carried(functools.partial(_dil_fwd, qkv, small["rel_bias"], n_heads, 3 * n_heads,
                                              name=f"dil_fwd_{l}"), ("mo", l, "first"))
        mixed, mixed_t = _outnorm_fwd(ya, yb, row(small["outnorm_a_g"][l]), row(small["outnorm_b_g"][l]),
                                      name=f"outnorm_fwd_{l}")
        x1 = _matmul(mixed, whole["out", l], addend=x, name=f"out_proj_{l}")
        h2, h2_t = _rmsnorm_fwd(x1, row(small["norm2_g"][l]), name=f"norm2_fwd_{l}")
        relu, act = carried(functools.partial(_matmul, h2, whole["mi", l], b_chunked=True, out_dtype=BF16,
                                              mode="relu_sq", name=f"mlp_in_{l}"),
                            ("mo", l, "second"))
        x2 = carried(functools.partial(_matmul, act, whole["mo", l], addend=x1, name=f"mlp_out_{l}"),
                     ("in", l + 1, "all"))
        saved.append((x, h1_t, qkv, f, fb, cq, ck, ya, lse_a, yb, lse_b, mixed_t, x1, h2_t, relu, act))
        x = x2
    dx, dx16, loss_tile, d_final = _loss_head(x, row(small["final_norm_g"]), target, name="loss_head")

    parts = {k: [None] * depth for k in ("in", "out", "mi", "mo")}
    got = {k: [None] * depth for k in ("in", "out", "mi", "mo")}
    small_grads = [None] * depth

    def to_sibling(call, kind, l, grad):
        res, (from_sibling,) = call(carry=_sibling_task([grad]))
        parts[kind][l] = _chip_partials(grad, from_sibling, place, name=f"chip_partials_{kind}_{l}")
        return res

    def to_chips(call, kind, l):
        if l >= depth:
            return call()
        res, (got[kind][l],) = call(carry=_chips_task([parts[kind][l]]))
        return res

    for l in reversed(range(depth)):
        x0, h1_t, qkv, f, fb, cq, ck, ya, lse_a, yb, lse_b, mixed_t, x1, h2_t, relu, act = saved[l]
        du = to_chips(functools.partial(_matmul, dx16, whole["mo", l], tb=True, out_dtype=BF16, mode="mul2",
                                        aux=relu, name=f"mlp_out_dx_{l}"), "in", l + 1)
        dw_mo = _matmul(act, dx16, ta=True, name=f"mlp_out_dw_{l}")
        dw_mo = dw_mo.reshape(N_DEV, d_ff // N_DEV, d_model)
        dh2 = to_sibling(functools.partial(_matmul, du, whole["mi", l], tb=True, b_chunked=True,
                                           name=f"mlp_in_dx_{l}"),
                         "mo", l, dw_mo)
        dw_mi = to_chips(functools.partial(_matmul, h2_t, du, out_chunks=N_DEV, name=f"mlp_in_dw_{l}"), "mo", l)
        dx1, dx1_16, d_norm2 = _rmsnorm_bwd(x1, row(small["norm2_g"][l]), dh2, dx, name=f"norm2_bwd_{l}")
        dmixed = to_sibling(functools.partial(_matmul, dx1_16, whole["out", l], tb=True, name=f"out_proj_dx_{l}"),
                            "mi", l, dw_mi)
        dw_out = _matmul(mixed_t, dx1_16, name=f"out_proj_dw_{l}")
        dw_out = dw_out.reshape(N_DEV, 2 * dh // N_DEV, d_model)
        dya, dyb, d_ga, d_gb = _outnorm_bwd(ya, yb, row(small["outnorm_a_g"][l]), row(small["outnorm_b_g"][l]),
                                            dmixed, name=f"outnorm_bwd_{l}")
        dqa, dka, dva, dcq, dck = to_chips(functools.partial(
            _fox_bwd, qkv, cq, ck, lse_a, ya, dya, n_heads, name=f"fox_bwd_{l}"), "mi", l)
        dqb, dkb, dvb, d_rb = to_sibling(functools.partial(
            _dil_bwd, qkv, small["rel_bias"], lse_b, yb, dyb, n_heads, 3 * n_heads, name=f"dil_bwd_{l}"),
            "out", l, dw_out)
        dc = _pad_lanes((dcq[:, 0, :] + dck[:, 0, :]).T)
        df, dfb = _gates_bwd(f, fb, dc, name=f"gates_bwd_{l}")
        dqkv = jnp.concatenate([dqa, dka, dva, dqb, dkb, dvb], axis=1)
        dw_qkv = to_chips(functools.partial(_matmul, h1_t, dqkv, name=f"qkv_proj_dw_{l}"), "out", l)
        dw_f = _matmul(h1_t, df, name=f"gate_proj_dw_{l}")
        dw_in = jnp.stack([jnp.concatenate(in_columns(dw_qkv, dw_f, p * in_shard, (p + 1) * in_shard), axis=1)
                           for p in range(N_DEV)])
        dh1 = _matmul(df, whole["f", l], tb=True, name=f"gate_proj_dx_{l}")
        dh1 = to_sibling(functools.partial(_matmul, dqkv, whole["qkv", l], tb=True, addend=dh1,
                                           name=f"qkv_proj_dx_{l}"), "in", l, dw_in)
        dx, dx16, d_norm1 = _rmsnorm_bwd(x0, row(small["norm1_g"][l]), dh1, dx1, name=f"norm1_bwd_{l}")
        small_grads[l] = dict(norm1_g=d_norm1[0], forget_b=dfb[0, :n_heads], rel_bias=d_rb[:, 0, :REL_BUCKETS].T,
                              outnorm_a_g=d_ga[0], outnorm_b_g=d_gb[0], norm2_g=d_norm2[0])
    return loss_tile[0, 0], dx, d_final[0], small_grads, parts, got


def _pack_small(parts, rows):
    flat = jnp.concatenate([p.reshape(-1).astype(F32) for p in parts])
    return jnp.pad(flat, (0, rows * 128 - flat.shape[0])).reshape(rows, 128)


def _unpack_small(packed, shapes):
    flat = packed.reshape(-1)
    out, pos = [], 0
    for shp in shapes:
        size = int(np.prod(shp))
        out.append(flat[pos:pos + size].reshape(shp))
        pos += size
    return out


def kernel(x, norm1_g, w_in, forget_b, rel_bias, outnorm_a_g, outnorm_b_g, w_out, norm2_g, w_mlp_in, w_mlp_out, final_norm_g, loss_target, m_norm1_g, m_w_in, m_forget_b, m_rel_bias, m_outnorm_a_g, m_outnorm_b_g, m_w_out, m_norm2_g, m_w_mlp_in, m_w_mlp_out, m_final_norm_g, v_norm1_g, v_w_in, v_forget_b, v_rel_bias, v_outnorm_a_g, v_outnorm_b_g, v_w_out, v_norm2_g, v_w_mlp_in, v_w_mlp_out, v_final_norm_g):
    depth, d_model, in_shard = w_in.shape
    n_heads = forget_b.shape[1]
    assert in_shard * N_DEV == 6 * n_heads * HEAD_DIM + n_heads and x.shape[0] == 1
    place = jnp.stack([lax.axis_index("x"), lax.axis_index("y"), lax.axis_index("c")]).astype(jnp.int32)

    small_names = ["norm1_g", "forget_b", "rel_bias", "outnorm_a_g", "outnorm_b_g", "norm2_g", "final_norm_g"]
    small_w = dict(norm1_g=norm1_g, forget_b=forget_b, rel_bias=rel_bias, outnorm_a_g=outnorm_a_g,
                   outnorm_b_g=outnorm_b_g, norm2_g=norm2_g, final_norm_g=final_norm_g)
    loss_part, dx, d_final, grads, parts, got = _train_step(
        x[0], loss_target[0], small_w, (w_in, w_out, w_mlp_in, w_mlp_out), place)

    small_m = dict(norm1_g=m_norm1_g, forget_b=m_forget_b, rel_bias=m_rel_bias, outnorm_a_g=m_outnorm_a_g,
                   outnorm_b_g=m_outnorm_b_g, norm2_g=m_norm2_g, final_norm_g=m_final_norm_g)
    small_v = dict(norm1_g=v_norm1_g, forget_b=v_forget_b, rel_bias=v_rel_bias, outnorm_a_g=v_outnorm_a_g,
                   outnorm_b_g=v_outnorm_b_g, norm2_g=v_norm2_g, final_norm_g=v_final_norm_g)
    small_g = {k: jnp.stack([g[k] for g in grads]) for k in small_names if k not in ("rel_bias", "final_norm_g")}
    small_g["rel_bias"] = functools.reduce(jnp.add, [g["rel_bias"] for g in grads])
    small_g["final_norm_g"] = d_final
    shapes = [small_w[k].shape for k in small_names]
    total = sum(int(np.prod(s)) for s in shapes) + 1
    rows = -(-total // (8 * 128)) * 8
    packed_g = _pack_small([small_g[k] for k in small_names] + [loss_part], rows)
    packed_g = _all_reduce_small(packed_g, name="reduce_small")
    zero = jnp.zeros((1,), F32)
    packed_w = _pack_small([small_w[k] for k in small_names] + [zero], rows)
    packed_m = _pack_small([small_m[k] for k in small_names] + [zero], rows)
    packed_v = _pack_small([small_v[k] for k in small_names] + [zero + 1.0], rows)
    packed_d, packed_nm, packed_nv = _adamw_small(packed_w, packed_g, packed_m, packed_v, name="adamw_small")
    g_small = dict(zip(small_names, _unpack_small(packed_g, shapes)))
    d_small = dict(zip(small_names, _unpack_small(packed_d, shapes)))
    nm_small = dict(zip(small_names, _unpack_small(packed_nm, shapes)))
    nv_small = dict(zip(small_names, _unpack_small(packed_nv, shapes)))
    loss = packed_g.reshape(-1)[total - 1]

    big_out = {}
    last = parts["in"][0]
    last_halves = [last[:, :last.shape[1] // 2], last[:, last.shape[1] // 2:]]
    last_got = []
    for kind, nm, w, m, v in [("mi", "w_mlp_in", w_mlp_in, m_w_mlp_in, v_w_mlp_in),
                              ("mo", "w_mlp_out", w_mlp_out, m_w_mlp_out, v_w_mlp_out),
                              ("out", "w_out", w_out, m_w_out, v_w_out), ("in", "w_in", w_in, m_w_in, v_w_in)]:
        adamw = functools.partial(_adamw_sharded, w, m, v, parts[kind], got[kind], place, name=f"adamw_{nm}")
        if last_halves:
            big_out[nm], (half,) = adamw(carry=_chips_task([last_halves.pop(0)]))
            last_got.append(half)
            if not last_halves:
                got["in"][0] = jnp.concatenate(last_got, axis=1)
        else:
            big_out[nm] = adamw()

    order = ["norm1_g", "w_in", "forget_b", "rel_bias", "outnorm_a_g", "outnorm_b_g", "w_out", "norm2_g",
             "w_mlp_in", "w_mlp_out", "final_norm_g"]
    pick = lambda k, idx, small: big_out[k][idx] if k in big_out else small[k]
    outs = [loss, dx[None]]
    outs += [pick(k, 0, g_small) for k in order]
    outs += [pick(k, 1, d_small) for k in order]
    outs += [pick(k, 2, nm_small) for k in order]
    outs += [pick(k, 3, nv_small) for k in order]
    return tuple(outs)
```

```python
import functools
import math

import numpy as np
import jax
import jax.numpy as jnp
from jax import lax
from jax.experimental import pallas as pl
from jax.experimental.pallas import tpu as pltpu

F32 = jnp.float32
BF16 = jnp.bfloat16

HEAD_DIM = 128
DIL_PATTERNS = ((128, 1), (512, 4), (2048, 16))
DIL_BLOCK = 128
REL_BUCKETS = 32
REL_MAX_DISTANCE = 2048
NORM_EPS = 1e-6
NEG_INF = -1e30
ADAM_LR = 0.001
ADAM_B1 = 0.9
ADAM_B2 = 0.999
ADAM_EPS = 1e-08
ADAM_WD = 0.01
ADAM_STEP = 10

N_DEV = 8
V7X_VMEM_LIMIT_BYTES = 56 * 1024 * 1024
MESH = pl.DeviceIdType.MESH


def _params(*semantics):
    return pltpu.CompilerParams(dimension_semantics=semantics, vmem_limit_bytes=V7X_VMEM_LIMIT_BYTES)


HBM_SPEC = pl.BlockSpec(memory_space=pltpu.HBM)


class _Carried:
    def __init__(self, operands, out_shape, scratch, start, forward, finish, aliases=None):
        self.operands, self.out_shape, self.scratch = list(operands), list(out_shape), list(scratch)
        self.start, self.forward, self.finish = start, forward, finish
        self.aliases = dict(aliases or {})


def _join(tasks):
    if len(tasks) == 1:
        return tasks[0]
    spans, aliases = [], {}
    i0 = o0 = s0 = 0
    for t in tasks:
        i1, o1, s1 = i0 + len(t.operands), o0 + len(t.out_shape), s0 + len(t.scratch)
        spans.append((slice(i0, i1), slice(o0, o1), slice(s0, s1)))
        aliases.update({i0 + i: o0 + o for i, o in t.aliases.items()})
        i0, o0, s0 = i1, o1, s1

    def phase(which):
        def run(ins, outs, sems):
            for t, (si, so, ss) in zip(tasks, spans):
                getattr(t, which)(ins[si], outs[so], sems[ss])
        return run

    return _Carried(sum((t.operands for t in tasks), []), sum((t.out_shape for t in tasks), []),
                    sum((t.scratch for t in tasks), []), phase("start"), phase("forward"), phase("finish"),
                    aliases)


FORWARD_AT = 0.8


def _call(body, *, name, grid, in_specs, out_specs, out_shape, operands, scratch=(), semantics, carry=None,
          tables=()):
    in_specs, out_specs, out_shape, scratch = list(in_specs), list(out_specs), list(out_shape), list(scratch)
    n_tab = len(tables)

    def run(fn, in_specs, out_specs, out_shape, scratch, operands, semantics, aliases):
        return pl.pallas_call(
            fn, name=name,
            grid_spec=pltpu.PrefetchScalarGridSpec(
                num_scalar_prefetch=n_tab, grid=grid, in_specs=in_specs, out_specs=out_specs,
                scratch_shapes=scratch),
            out_shape=out_shape, input_output_aliases=aliases,
            compiler_params=_params(*semantics))(*tables, *operands)

    if carry is None:
        return run(body, in_specs, out_specs, out_shape, scratch, operands, semantics, {})
    n_in, n_out = len(in_specs), len(out_specs)
    ci, co, cs = len(carry.operands), len(carry.out_shape), len(carry.scratch)
    steps = int(np.prod(grid))
    forward_step = min(int(steps * FORWARD_AT), steps - 1)

    def carrying(*refs):
        tabs, refs = refs[:n_tab], refs[n_tab:]
        main_in, c_in = refs[:n_in], refs[n_in:n_in + ci]
        main_out = refs[n_in + ci:n_in + ci + n_out]
        c_out = refs[n_in + ci + n_out:n_in + ci + n_out + co]
        rest = refs[n_in + ci + n_out + co:]
        main_scr, c_scr = rest[:len(rest) - cs], rest[len(rest) - cs:]
        step = 0
        for axis, extent in enumerate(grid):
            step = step * extent + pl.program_id(axis)

        @pl.when(step == 0)
        def _():
            carry.start(c_in, c_out, c_scr)

        body(*tabs, *main_in, *main_out, *main_scr)

        @pl.when(step == forward_step)
        def _():
            carry.forward(c_in, c_out, c_scr)

        @pl.when(step == steps - 1)
        def _():
            carry.finish(c_in, c_out, c_scr)

    aliases = {n_tab + n_in + i: n_out + o for i, o in carry.aliases.items()}
    res = run(carrying, in_specs + [HBM_SPEC] * ci, out_specs + [HBM_SPEC] * co, out_shape + carry.out_shape,
              scratch + carry.scratch, (*operands, *carry.operands), ["arbitrary"] * len(grid), aliases)
    return res[:n_out], res[n_out:]


def _run_carried(carry, *, name):
    ci, co = len(carry.operands), len(carry.out_shape)

    def body(*refs):
        parts = (refs[:ci], refs[ci:ci + co], refs[ci + co:])
        carry.start(*parts)
        carry.forward(*parts)
        carry.finish(*parts)

    return pl.pallas_call(
        body, name=name, in_specs=[HBM_SPEC] * ci, out_specs=[HBM_SPEC] * co,
        out_shape=carry.out_shape, scratch_shapes=carry.scratch)(*carry.operands)


def _fit(dim, want, unit=128):
    if dim <= want:
        return dim
    t = want - want % unit
    while dim % t:
        t -= unit
    return t


def _matmul(a, b, *, name, ta=False, tb=False, out_dtype=F32, tm=1024, tn=1024, tk=2048,
            addend=None, mode=None, aux=None, out_chunks=None, b_chunked=False, norm_gain=None, carry=None):
    m_dim, k_dim = (a.shape[1], a.shape[0]) if ta else a.shape
    b_rows, b_cols = (b.shape[1], b.shape[0] * b.shape[2]) if b_chunked else b.shape
    n_dim = b_rows if tb else b_cols
    assert (b_cols if tb else b_rows) == k_dim
    b_chunk = b.shape[2] if b_chunked else b_cols
    tm, tn, tk = _fit(m_dim, tm), _fit(n_dim // (out_chunks or 1), tn), _fit(k_dim, tk)
    k_chunks = 1
    if b_chunked and tb:
        assert tk % b_chunk == 0 and not ta
        k_chunks = tk // b_chunk
    elif b_chunked:
        tn = _fit(b_chunk, tn)
    assert m_dim % tm == 0 and n_dim % tn == 0 and k_dim % tk == 0, (name, a.shape, b.shape)
    nk = k_dim // tk
    a_spec = (pl.BlockSpec((tk, tm), lambda i, j, k: (k, i)) if ta
              else pl.BlockSpec((tm, tk), lambda i, j, k: (i, k)))
    if b_chunked and tb:
        b_spec = pl.BlockSpec((k_chunks, tn, b_chunk), lambda i, j, k: (k, j, 0))
    elif b_chunked:
        per_b = b_chunk // tn
        b_spec = pl.BlockSpec((None, tk, tn), lambda i, j, k: (j // per_b, k, j % per_b))
    else:
        b_spec = (pl.BlockSpec((tn, tk), lambda i, j, k: (j, k)) if tb
                  else pl.BlockSpec((tk, tn), lambda i, j, k: (k, j)))
    mn_spec = pl.BlockSpec((tm, tn), lambda i, j, k: (i, j))
    if out_chunks is None:
        o_spec, o_shape = mn_spec, (m_dim, n_dim)
    else:
        per = n_dim // out_chunks // tn
        assert per * tn * out_chunks == n_dim
        o_spec = pl.BlockSpec((None, tm, tn), lambda i, j, k: (j // per, i, j % per))
        o_shape = (out_chunks, m_dim, n_dim // out_chunks)
    dims = (((0 if ta else 1,), (1 if tb else 0,)), ((), ()))
    n_out = 2 if (mode == "relu_sq" or norm_gain is not None) else 1
    in_place = mode is None and out_dtype == F32
    in_specs, operands = [a_spec, b_spec], [a, b]
    if addend is not None:
        in_specs.append(mn_spec)
        operands.append(addend)
    if mode == "mul2":
        in_specs.append(mn_spec)
        operands.append(aux)
    if norm_gain is not None:
        assert in_place and tn == n_dim and out_chunks is None
        in_specs.append(pl.BlockSpec((1, tn), lambda i, j, k: (0, 0)))
        operands.append(norm_gain)

    def body(*refs):
        a_ref, b_ref = refs[0], refs[1]
        pos = 2
        add_ref = aux_ref = gain_ref = None
        if addend is not None:
            add_ref, pos = refs[pos], pos + 1
        if mode == "mul2":
            aux_ref, pos = refs[pos], pos + 1
        if norm_gain is not None:
            gain_ref, pos = refs[pos], pos + 1
        outs = refs[pos:pos + n_out]

        def normed(c):
            if gain_ref is not None:
                outs[1][...] = _norm_fwd_math(c, gain_ref[...]).astype(BF16)

        def finish(acc):
            if add_ref is not None:
                acc = acc + add_ref[...].astype(F32)
            if mode == "relu_sq":
                r = jnp.maximum(acc, 0.0)
                outs[0][...] = r.astype(outs[0].dtype)
                outs[1][...] = (r * r).astype(outs[1].dtype)
            elif mode == "mul2":
                outs[0][...] = (acc * (2.0 * aux_ref[...].astype(F32))).astype(outs[0].dtype)
            else:
                outs[0][...] = acc.astype(outs[0].dtype)
                if nk == 1:
                    normed(acc)

        if k_chunks > 1:
            part = functools.reduce(jnp.add, [
                lax.dot_general(a_ref[:, c * b_chunk:(c + 1) * b_chunk].astype(BF16), b_ref[c].astype(BF16), dims,
                                preferred_element_type=F32) for c in range(k_chunks)])
        else:
            b_tile = b_ref[0] if (b_chunked and tb) else b_ref[...]
            part = lax.dot_general(a_ref[...].astype(BF16), b_tile.astype(BF16), dims,
                                   preferred_element_type=F32)
        k = pl.program_id(2)
        if nk == 1:
            finish(part)
        elif in_place:
            @pl.when(k == 0)
            def _():
                finish(part)

            @pl.when(k > 0)
            def _():
                outs[0][...] += part

            if gain_ref is not None:
                @pl.when(k == nk - 1)
                def _():
                    normed(outs[0][...])
        else:
            acc_ref = refs[-1]

            @pl.when(k == 0)
            def _():
                acc_ref[...] = part

            @pl.when(k > 0)
            def _():
                acc_ref[...] += part

            @pl.when(k == nk - 1)
            def _():
                finish(acc_ref[...])

    out_shape = [jax.ShapeDtypeStruct(o_shape, out_dtype)] * n_out
    if norm_gain is not None:
        out_shape[1] = jax.ShapeDtypeStruct(o_shape, BF16)
    res = _call(
        body, name=name, grid=(m_dim // tm, n_dim // tn, nk),
        in_specs=in_specs, out_specs=[o_spec] * n_out, out_shape=out_shape, operands=operands,
        scratch=[pltpu.VMEM((tm, tn), F32)] if nk > 1 and not in_place else [],
        semantics=("parallel", "parallel", "arbitrary"), carry=carry)
    if carry is not None:
        res, carried = res
        return (res if n_out > 1 else res[0]), carried
    return res if n_out > 1 else res[0]


def _norm_fwd_math(x, g):
    r = lax.rsqrt(jnp.mean(x * x, axis=-1, keepdims=True) + NORM_EPS)
    return (x * r) * g


def _norm_bwd_math(x, g, dy):
    r = lax.rsqrt(jnp.mean(x * x, axis=-1, keepdims=True) + NORM_EPS)
    xh = x * r
    dxh = dy * g
    dx = r * (dxh - xh * jnp.mean(dxh * xh, axis=-1, keepdims=True))
    return dx, jnp.sum(dy * xh, axis=0, keepdims=True)


def _row_tile(rows, want=256):
    t = min(rows, want)
    assert rows % t == 0
    return t


def _rmsnorm_fwd(x, g, *, name):
    s, d = x.shape
    tr = _row_tile(s)

    def body(x_ref, g_ref, h_ref):
        h_ref[...] = _norm_fwd_math(x_ref[...], g_ref[...]).astype(BF16)

    return pl.pallas_call(
        body, name=name, grid=(s // tr,),
        in_specs=[pl.BlockSpec((tr, d), lambda i: (i, 0)), pl.BlockSpec((1, d), lambda i: (0, 0))],
        out_specs=pl.BlockSpec((tr, d), lambda i: (i, 0)),
        out_shape=jax.ShapeDtypeStruct((s, d), BF16),
        compiler_params=_params("parallel"),
    )(x, g)


def _rmsnorm_bwd(x, g, dh, dres, *, name, carry=None):
    s, d = x.shape
    tr = _row_tile(s)

    def body(x_ref, g_ref, dh_ref, dres_ref, dx_ref, dx16_ref, dg_ref):
        dx, dg = _norm_bwd_math(x_ref[...], g_ref[...], dh_ref[...])
        dx = dres_ref[...] + dx
        dx_ref[...] = dx
        dx16_ref[...] = dx.astype(BF16)

        @pl.when(pl.program_id(0) == 0)
        def _():
            dg_ref[...] = dg

        @pl.when(pl.program_id(0) > 0)
        def _():
            dg_ref[...] += dg

    row = pl.BlockSpec((tr, d), lambda i: (i, 0))
    vec = pl.BlockSpec((1, d), lambda i: (0, 0))
    return _call(
        body, name=name, grid=(s // tr,),
        in_specs=[row, vec, row, row], out_specs=[row, row, vec],
        out_shape=[jax.ShapeDtypeStruct((s, d), F32), jax.ShapeDtypeStruct((s, d), BF16),
                   jax.ShapeDtypeStruct((1, d), F32)],
        operands=(x, g, dh, dres), semantics=("arbitrary",), carry=carry)


def _outnorm_fwd(ya, yb, ga, gb, *, name):
    s, da = ya.shape
    db = yb.shape[1]
    tr = _row_tile(s)

    def body(ya_ref, yb_ref, ga_ref, gb_ref, o_ref):
        o_ref[:, :da] = _norm_fwd_math(ya_ref[...], ga_ref[...]).astype(BF16)
        o_ref[:, da:] = _norm_fwd_math(yb_ref[...], gb_ref[...]).astype(BF16)

    return pl.pallas_call(
        body, name=name, grid=(s // tr,),
        in_specs=[pl.BlockSpec((tr, da), lambda i: (i, 0)), pl.BlockSpec((tr, db), lambda i: (i, 0)),
                  pl.BlockSpec((1, da), lambda i: (0, 0)), pl.BlockSpec((1, db), lambda i: (0, 0))],
        out_specs=pl.BlockSpec((tr, da + db), lambda i: (i, 0)),
        out_shape=jax.ShapeDtypeStruct((s, da + db), BF16),
        compiler_params=_params("parallel"),
    )(ya, yb, ga, gb)


def _outnorm_bwd(ya, yb, ga, gb, dmixed, *, name):
    s, da = ya.shape
    db = yb.shape[1]
    tr = _row_tile(s)

    def body(ya_ref, yb_ref, ga_ref, gb_ref, dm_ref, dya_ref, dyb_ref, dga_ref, dgb_ref):
        dxa, dga = _norm_bwd_math(ya_ref[...], ga_ref[...], dm_ref[:, :da])
        dxb, dgb = _norm_bwd_math(yb_ref[...], gb_ref[...], dm_ref[:, da:])
        dya_ref[...] = dxa
        dyb_ref[...] = dxb

        @pl.when(pl.program_id(0) == 0)
        def _():
            dga_ref[...] = dga
            dgb_ref[...] = dgb

        @pl.when(pl.program_id(0) > 0)
        def _():
            dga_ref[...] += dga
            dgb_ref[...] += dgb

    ra = pl.BlockSpec((tr, da), lambda i: (i, 0))
    rb = pl.BlockSpec((tr, db), lambda i: (i, 0))
    va = pl.BlockSpec((1, da), lambda i: (0, 0))
    vb = pl.BlockSpec((1, db), lambda i: (0, 0))
    return pl.pallas_call(
        body, name=name, grid=(s // tr,),
        in_specs=[ra, rb, va, vb, pl.BlockSpec((tr, da + db), lambda i: (i, 0))],
        out_specs=[ra, rb, va, vb],
        out_shape=[jax.ShapeDtypeStruct((s, da), F32), jax.ShapeDtypeStruct((s, db), F32),
                   jax.ShapeDtypeStruct((1, da), F32), jax.ShapeDtypeStruct((1, db), F32)],
        compiler_params=_params("arbitrary"),
    )(ya, yb, ga, gb, dmixed)


def _loss_head(x, g, target, *, name):
    s, d = x.shape
    tr = _row_tile(s)

    def body(x_ref, g_ref, t_ref, dx_ref, dx16_ref, loss_ref, dg_ref):
        xv, gv = x_ref[...], g_ref[...]
        err = _norm_fwd_math(xv, gv) - t_ref[...]
        part = 0.5 * jnp.sum(jnp.mean(err * err, axis=-1, keepdims=True), axis=0, keepdims=True)
        dx, dg = _norm_bwd_math(xv, gv, err * (1.0 / d))
        dx_ref[...] = dx
        dx16_ref[...] = dx.astype(BF16)
        part = jnp.broadcast_to(part, (8, 128))

        @pl.when(pl.program_id(0) == 0)
        def _():
            dg_ref[...] = dg
            loss_ref[...] = part

        @pl.when(pl.program_id(0) > 0)
        def _():
            dg_ref[...] += dg
            loss_ref[...] += part

    row = pl.BlockSpec((tr, d), lambda i: (i, 0))
    vec = pl.BlockSpec((1, d), lambda i: (0, 0))
    return pl.pallas_call(
        body, name=name, grid=(s // tr,),
        in_specs=[row, vec, row],
        out_specs=[row, row, pl.BlockSpec((8, 128), lambda i: (0, 0)), vec],
        out_shape=[jax.ShapeDtypeStruct((s, d), F32), jax.ShapeDtypeStruct((s, d), BF16),
                   jax.ShapeDtypeStruct((8, 128), F32), jax.ShapeDtypeStruct((1, d), F32)],
        compiler_params=_params("arbitrary"),
    )(x, g, target)


def _split3(x):
    hi = x.astype(BF16)
    rem = x - hi.astype(F32)
    mid = rem.astype(BF16)
    lo = (rem - mid.astype(F32)).astype(BF16)
    return hi, mid, lo


def _tri_sum(tri, x):
    hi, mid, lo = _split3(x)
    dot = functools.partial(jnp.dot, preferred_element_type=F32)
    return dot(tri, hi) + dot(tri, mid) + dot(tri, lo)


def _gates_fwd(f, fb, *, name):
    s, w = f.shape
    tb = 128
    nb = s // tb

    def body(f_ref, fb_ref, c_ref, carry):
        @pl.when(pl.program_id(0) == 0)
        def _():
            carry[...] = jnp.zeros_like(carry)

        logf = jax.nn.log_sigmoid(f_ref[...] + fb_ref[...])
        row = lax.broadcasted_iota(jnp.int32, (tb, tb), 0)
        col = lax.broadcasted_iota(jnp.int32, (tb, tb), 1)
        tri = (row >= col).astype(BF16)
        c = _tri_sum(tri, logf) + carry[...]
        c_ref[...] = c
        carry[...] = c[tb - 1:tb, :]

    return pl.pallas_call(
        body, name=name, grid=(nb,),
        in_specs=[pl.BlockSpec((tb, w), lambda i: (i, 0)), pl.BlockSpec((1, w), lambda i: (0, 0))],
        out_specs=pl.BlockSpec((tb, w), lambda i: (i, 0)),
        out_shape=jax.ShapeDtypeStruct((s, w), F32),
        scratch_shapes=[pltpu.VMEM((1, w), F32)],
        compiler_params=_params("arbitrary"),
    )(f, fb)


def _gates_bwd(f, fb, dc, *, name):
    s, w = f.shape
    tb = 128
    nb = s // tb

    def body(f_ref, fb_ref, dc_ref, df_ref, dfb_ref, carry):
        @pl.when(pl.program_id(0) == 0)
        def _():
            carry[...] = jnp.zeros_like(carry)
            dfb_ref[...] = jnp.zeros_like(dfb_ref)

        row = lax.broadcasted_iota(jnp.int32, (tb, tb), 0)
        col = lax.broadcasted_iota(jnp.int32, (tb, tb), 1)
        tri = (row <= col).astype(BF16)
        dlogf = _tri_sum(tri, dc_ref[...]) + carry[...]
        carry[...] = dlogf[0:1, :]
        df = dlogf * jax.nn.sigmoid(-(f_ref[...] + fb_ref[...]))
        df_ref[...] = df
        dfb_ref[...] += jnp.sum(df, axis=0, keepdims=True)

    rev = pl.BlockSpec((tb, w), lambda i: (nb - 1 - i, 0))
    vec = pl.BlockSpec((1, w), lambda i: (0, 0))
    return pl.pallas_call(
        body, name=name, grid=(nb,),
        in_specs=[rev, vec, rev], out_specs=[rev, vec],
        out_shape=[jax.ShapeDtypeStruct((s, w), F32), jax.ShapeDtypeStruct((1, w), F32)],
        scratch_shapes=[pltpu.VMEM((1, w), F32)],
        compiler_params=_params("arbitrary"),
    )(f, fb, dc)


def _nt(a, b):
    return lax.dot_general(a, b, (((1,), (1,)), ((), ())), preferred_element_type=F32)


def _tn(a, b):
    return lax.dot_general(a, b, (((0,), (0,)), ((), ())), preferred_element_type=F32)


def _nn(a, b):
    return jnp.dot(a, b, preferred_element_type=F32)


DIL_GROUP = 8
FOX_TILE = 1024
LOG2E = 1.4426950408889634


def _causal_pairs(nt, key_major):
    if key_major:
        pairs = [(q, k) for k in range(nt) for q in range(k, nt)]
    else:
        pairs = [(q, k) for q in range(nt) for k in range(q + 1)]
    return (jnp.asarray([p[0] for p in pairs], jnp.int32), jnp.asarray([p[1] for p in pairs], jnp.int32))


def _fox_fwd(qkv, c_row, ck_lanes, n_heads, *, name, carry=None):
    s = qkv.shape[0]
    e = HEAD_DIM
    t = min(FOX_TILE, s)
    nt = s // t
    scale2 = e ** -0.5 * LOG2E
    lanes = 128
    q_tab, k_tab = _causal_pairs(nt, key_major=False)

    def body(q_tab, k_tab, q_ref, k_ref, v_ref, cq_ref, ck_ref, o_ref, lse_ref,
             m_scr, l_scr, acc_scr, s_scr, p_scr):
        pair = pl.program_id(1)
        qi, ki = q_tab[pair], k_tab[pair]

        @pl.when(ki == 0)
        def _():
            m_scr[...] = jnp.full_like(m_scr, NEG_INF)
            l_scr[...] = jnp.zeros_like(l_scr)
            acc_scr[...] = jnp.zeros_like(acc_scr)

        def update(diagonal):
            s_scr[...] = _nt(k_ref[...], q_ref[...])
            ck2 = ck_ref[...]
            for c0 in range(0, t, lanes):
                cols = pl.ds(c0, lanes)

                def logits2():
                    x = s_scr[:, cols] * scale2 - ck2
                    if diagonal:
                        key = lax.broadcasted_iota(jnp.int32, x.shape, 0)
                        qry = c0 + lax.broadcasted_iota(jnp.int32, x.shape, 1)
                        x = jnp.where(key <= qry, x, NEG_INF)
                    return x

                m_old = m_scr[:, cols]
                m_new = jnp.maximum(m_old, jnp.max(logits2(), axis=0, keepdims=True))
                p = jnp.exp2(logits2() - m_new)
                alpha = jnp.exp2(m_old - m_new)
                l_scr[:, cols] = alpha * l_scr[:, cols] + jnp.sum(p, axis=0, keepdims=True)
                m_scr[:, cols] = m_new
                acc_scr[:, cols] = alpha * acc_scr[:, cols]
                p_scr[:, cols] = p.astype(BF16)
            acc_scr[...] += _tn(v_ref[...], p_scr[...])

        @pl.when(ki < qi)
        def _():
            update(False)

        @pl.when(ki == qi)
        def _():
            update(True)
            o_ref[...] = (acc_scr[...] / l_scr[...]).T
            lse_ref[...] = (m_scr[...] + jnp.log2(l_scr[...])) * (1.0 / LOG2E) + cq_ref[...]

    h_ = n_heads
    return _call(
        body, name=name, grid=(h_, int(q_tab.shape[0])), tables=(q_tab, k_tab),
        in_specs=[
            pl.BlockSpec((t, e), lambda h, p, qt, kt: (qt[p], h)),
            pl.BlockSpec((t, e), lambda h, p, qt, kt: (kt[p], h_ + h)),
            pl.BlockSpec((t, e), lambda h, p, qt, kt: (kt[p], 2 * h_ + h)),
            pl.BlockSpec((None, 1, t), lambda h, p, qt, kt: (h, 0, qt[p])),
            pl.BlockSpec((None, t, lanes), lambda h, p, qt, kt: (h, kt[p], 0)),
        ],
        out_specs=[pl.BlockSpec((t, e), lambda h, p, qt, kt: (qt[p], h)),
                   pl.BlockSpec((None, 1, t), lambda h, p, qt, kt: (h, 0, qt[p]))],
        out_shape=[jax.ShapeDtypeStruct((s, h_ * e), F32), jax.ShapeDtypeStruct((h_, 1, s), F32)],
        operands=(qkv, qkv, qkv, c_row, ck_lanes),
        scratch=[pltpu.VMEM((1, t), F32), pltpu.VMEM((1, t), F32), pltpu.VMEM((e, t), F32),
                 pltpu.VMEM((t, t), F32), pltpu.VMEM((t, t), BF16)],
        semantics=("parallel", "arbitrary"), carry=carry)


def _fox_bwd(qkv, c_row, ck_lanes, lse, y, dy, n_heads, *, name, carry=None):
    s = qkv.shape[0]
    e = HEAD_DIM
    t = min(FOX_TILE, s)
    nt = s // t
    scale = e ** -0.5
    scale2 = scale * LOG2E
    lanes = 128
    q_tab, k_tab = _causal_pairs(nt, key_major=False)
    n_pairs = int(q_tab.shape[0])

    def body(q_tab, k_tab, q_ref, k_ref, v_ref, c_ref, ck_ref, lse_ref, y_ref, dy_ref,
             dq_ref, dk_ref, dv_ref, dcq_ref, dck_ref,
             dq_scr, dk_scr, dv_scr, dck_scr, s_scr, dp_scr, p_scr, ds_scr, do_scr, delta_scr, shift_scr):
        pair = pl.program_id(1)
        qi, ki = q_tab[pair], k_tab[pair]

        @pl.when(pair == 0)
        def _():
            dk_scr[...] = jnp.zeros_like(dk_scr)
            dv_scr[...] = jnp.zeros_like(dv_scr)
            dck_scr[...] = jnp.zeros_like(dck_scr)

        @pl.when(ki == 0)
        def _():
            do = dy_ref[...]
            delta_scr[...] = lax.dot_general(jnp.ones((8, e), F32), do * y_ref[...], (((1,), (1,)), ((), ())),
                                             precision=lax.Precision.HIGHEST, preferred_element_type=F32)
            shift_scr[...] = (lse_ref[...] - c_ref[...]) * LOG2E
            do_scr[...] = do.astype(BF16)
            dq_scr[...] = jnp.zeros_like(dq_scr)
            dcq_ref[...] = jnp.zeros_like(dcq_ref)

        def update(diagonal):
            s_scr[...] = _nt(k_ref[...], q_ref[...])
            dp_scr[...] = _nt(v_ref[...], do_scr[...])
            ck2 = ck_ref[...]
            k_rows = pl.ds(pl.multiple_of(ki * t, t), t)
            for c0 in range(0, t, lanes):
                cols = pl.ds(c0, lanes)
                x = s_scr[:, cols] * scale2 - ck2
                if diagonal:
                    key = lax.broadcasted_iota(jnp.int32, x.shape, 0)
                    qry = c0 + lax.broadcasted_iota(jnp.int32, x.shape, 1)
                    x = jnp.where(key <= qry, x, NEG_INF)
                p = jnp.exp2(x - shift_scr[:, cols])
                ds = p * (dp_scr[:, cols] - delta_scr[0:1, cols])
                dcq_ref[:, cols] += jnp.sum(ds, axis=0, keepdims=True)
                dck_scr[k_rows, :] += ds
                p_scr[:, cols] = p.astype(BF16)
                ds_scr[:, cols] = ds.astype(BF16)
            dv_scr[k_rows, :] += _nn(p_scr[...], do_scr[...])
            dk_scr[k_rows, :] += scale * _nn(ds_scr[...], q_ref[...])
            dq_scr[...] += scale * _tn(ds_scr[...], k_ref[...])

        @pl.when(ki < qi)
        def _():
            update(False)

        @pl.when(ki == qi)
        def _():
            update(True)
            dq_ref[...] = dq_scr[...].astype(BF16)

        @pl.when(pair == n_pairs - 1)
        def _():
            dk_ref[...] = dk_scr[...].astype(BF16)
            dv_ref[...] = dv_scr[...].astype(BF16)
            dck_ref[...] = -lax.dot_general(jnp.ones((8, lanes), F32), dck_scr[...], (((1,), (1,)), ((), ())),
                                            precision=lax.Precision.HIGHEST, preferred_element_type=F32)[0:1]

    h_ = n_heads
    q_row = pl.BlockSpec((None, 1, t), lambda h, p, qt, kt: (h, 0, qt[p]))
    q_blk = pl.BlockSpec((t, e), lambda h, p, qt, kt: (qt[p], h))
    whole = pl.BlockSpec((s, e), lambda h, p, qt, kt: (0, h))
    return _call(
        body, name=name, grid=(h_, n_pairs), tables=(q_tab, k_tab),
        in_specs=[
            q_blk,
            pl.BlockSpec((t, e), lambda h, p, qt, kt: (kt[p], h_ + h)),
            pl.BlockSpec((t, e), lambda h, p, qt, kt: (kt[p], 2 * h_ + h)),
            q_row,
            pl.BlockSpec((None, t, lanes), lambda h, p, qt, kt: (h, kt[p], 0)),
            q_row, q_blk, q_blk,
        ],
        out_specs=[q_blk, whole, whole, q_row,
                   pl.BlockSpec((None, 1, s), lambda h, p, qt, kt: (h, 0, 0))],
        out_shape=[jax.ShapeDtypeStruct((s, h_ * e), BF16)] * 3
        + [jax.ShapeDtypeStruct((h_, 1, s), F32), jax.ShapeDtypeStruct((h_, 1, s), F32)],
        operands=(qkv, qkv, qkv, c_row, ck_lanes, lse, y, dy),
        scratch=[pltpu.VMEM((t, e), F32), pltpu.VMEM((s, e), F32), pltpu.VMEM((s, e), F32),
                 pltpu.VMEM((s, lanes), F32), pltpu.VMEM((t, t), F32), pltpu.VMEM((t, t), F32), pltpu.VMEM((t, t), BF16),
                 pltpu.VMEM((t, t), BF16), pltpu.VMEM((t, e), BF16), pltpu.VMEM((8, t), F32),
                 pltpu.VMEM((1, t), F32)],
        semantics=("parallel", "arbitrary"), carry=carry)


def _rel_bucket_table(dilation, span):
    dist = np.arange(span + 1, dtype=np.int64) * dilation
    max_exact = REL_BUCKETS // 2
    d = np.maximum(dist.astype(np.float32), np.float32(1.0))
    large = max_exact + (np.log(d / np.float32(max_exact)) / np.float32(math.log(REL_MAX_DISTANCE / max_exact))
                         * np.float32(REL_BUCKETS - max_exact)).astype(np.int32)
    large = np.minimum(large, REL_BUCKETS - 1)
    return np.where(dist < max_exact, dist, large)


def _bucket_bands(dilation, span):
    table = _rel_bucket_table(dilation, span)
    bands = []
    for n, b in enumerate(table):
        if bands and bands[-1][0] == int(b):
            bands[-1][2] = n
        else:
            assert not any(bb[0] == int(b) for bb in bands)
            bands.append([int(b), n, n])
    return [tuple(b) for b in bands]


def _steps_back():
    i = lax.broadcasted_iota(jnp.int32, (DIL_BLOCK, DIL_BLOCK), 0)
    j = lax.broadcasted_iota(jnp.int32, (DIL_BLOCK, DIL_BLOCK), 1)
    return i - j, DIL_BLOCK + i - j


def _bias_tiles(rb_ref, h, bands, span):
    n_cur, n_prev = _steps_back()
    t_cur = jnp.zeros((DIL_BLOCK, DIL_BLOCK), F32)
    t_prev = jnp.zeros((DIL_BLOCK, DIL_BLOCK), F32)
    for b, lo, hi in bands:
        val = rb_ref[b, h]
        t_cur = jnp.where((n_cur >= lo) & (n_cur <= hi), val, t_cur)
        t_prev = jnp.where((n_prev >= lo) & (n_prev <= hi), val, t_prev)
    t_cur = jnp.where(n_cur >= 0, t_cur, NEG_INF)
    t_prev = jnp.where(n_prev <= span, t_prev, NEG_INF)
    return t_cur, t_prev


def _dil_rows(start, dilation):
    return pl.ds(start, DIL_BLOCK, stride=dilation) if dilation > 1 else pl.ds(start, DIL_BLOCK)


def _dil_block_groups(s, dilation, run_group):
    group = dilation * DIL_BLOCK
    per_residue = s // group

    def block(r, n):
        if isinstance(n, int):
            return n * group + r, max(n - 1, 0) * group + r, n == 0
        return (pl.multiple_of(n * group, DIL_BLOCK) + r,
                pl.multiple_of(jnp.maximum(n - 1, 0) * group, DIL_BLOCK) + r, n == 0)

    if per_residue >= DIL_GROUP:
        assert per_residue % DIL_GROUP == 0
        for r in range(dilation):
            def trip(it, carry, r=r):
                run_group([block(r, it * DIL_GROUP + j) for j in range(DIL_GROUP)])
                return carry
            lax.fori_loop(0, per_residue // DIL_GROUP, trip, 0)
    else:
        residues = DIL_GROUP // per_residue
        assert residues * per_residue == DIL_GROUP and dilation % residues == 0
        for r0 in range(0, dilation, residues):
            run_group([block(r, n) for r in range(r0, r0 + residues) for n in range(per_residue)])


def _dil_fwd(qkv, rel_bias, n_heads, col0, *, name, carry=None):
    s = qkv.shape[0]
    e = HEAD_DIM
    scale = e ** -0.5
    n_pat = len(DIL_PATTERNS)
    for window, d in DIL_PATTERNS:
        assert window // d == DIL_BLOCK and s % (d * DIL_BLOCK) == 0
    bands = [_bucket_bands(d, w // d) for w, d in DIL_PATTERNS]

    def body(rb_ref, q_ref, k_ref, v_ref, y_ref, lse_ref, qf, kf, vf, *scr):
        o_scr, l_scr = scr[:n_pat], scr[n_pat:]
        h = pl.program_id(0)
        qf[...] = q_ref[...].astype(F32)
        kf[...] = k_ref[...].astype(F32)
        vf[...] = v_ref[...].astype(F32)
        for pi, (window, d) in enumerate(DIL_PATTERNS):
            t_cur, t_prev = _bias_tiles(rb_ref, h, bands[pi], window // d)

            def run_group(blocks, d=d, pi=pi, t_cur=t_cur, t_prev=t_prev):
                rows = [(_dil_rows(q0, d), _dil_rows(p0, d)) for q0, p0, _ in blocks]
                qb = [qf[cur, :].astype(BF16) for cur, _ in rows]
                s_c = [_nt(q, kf[cur, :].astype(BF16)) for q, (cur, _) in zip(qb, rows)]
                s_p = [_nt(q, kf[prev, :].astype(BF16)) for q, (_, prev) in zip(qb, rows)]
                s_c = [x * scale + t_cur for x in s_c]
                s_p = [x * scale + (t_prev + jnp.where(first, NEG_INF, 0.0)) for x, (_, _, first) in zip(s_p, blocks)]
                m = [jnp.max(jnp.maximum(a, b), axis=-1, keepdims=True) for a, b in zip(s_c, s_p)]
                p_c = [jnp.exp(a - mm) for a, mm in zip(s_c, m)]
                p_p = [jnp.exp(b - mm) for b, mm in zip(s_p, m)]
                l = [jnp.sum(a + b, axis=-1, keepdims=True) for a, b in zip(p_c, p_p)]
                o = [_nn(a.astype(BF16), vf[cur, :].astype(BF16)) + _nn(b.astype(BF16), vf[prev, :].astype(BF16))
                     for a, b, (cur, prev) in zip(p_c, p_p, rows)]
                for (cur, _), oo, ll, mm in zip(rows, o, l, m):
                    o_scr[pi][cur, :] = oo / ll
                    l_scr[pi][cur, :] = mm + jnp.log(ll)

            _dil_block_groups(s, d, run_group)
        lses = [l_scr[pi][...] for pi in range(n_pat)]
        m = functools.reduce(jnp.maximum, lses)
        ws = [jnp.exp(l - m) for l in lses]
        tot = functools.reduce(jnp.add, ws)
        y = functools.reduce(jnp.add, [w * o_scr[pi][...] for pi, w in enumerate(ws)])
        y_ref[...] = y / tot
        lse_ref[...] = m + jnp.log(tot)

    h_ = n_heads
    return _call(
        body, name=name, grid=(h_,),
        in_specs=[pl.BlockSpec(memory_space=pltpu.SMEM),
                  pl.BlockSpec((s, e), lambda h: (0, col0 + h)),
                  pl.BlockSpec((s, e), lambda h: (0, col0 + h_ + h)),
                  pl.BlockSpec((s, e), lambda h: (0, col0 + 2 * h_ + h))],
        out_specs=[pl.BlockSpec((s, e), lambda h: (0, h)), pl.BlockSpec((None, s, 1), lambda h: (h, 0, 0))],
        out_shape=[jax.ShapeDtypeStruct((s, h_ * e), F32), jax.ShapeDtypeStruct((h_, s, 1), F32)],
        operands=(rel_bias, qkv, qkv, qkv),
        scratch=[pltpu.VMEM((s, e), F32)] * 3 + [pltpu.VMEM((s, e), F32)] * n_pat
        + [pltpu.VMEM((s, 1), F32)] * n_pat,
        semantics=("parallel",), carry=carry)


def _dil_bwd(qkv, rel_bias, lse, y, dy, n_heads, col0, *, name, carry=None):
    s = qkv.shape[0]
    e = HEAD_DIM
    scale = e ** -0.5
    bands = [_bucket_bands(d, w // d) for w, d in DIL_PATTERNS]

    def body(rb_ref, q_ref, k_ref, v_ref, lse_ref, y_ref, dy_ref, dq_ref, dk_ref, dv_ref, drb_ref,
             qf, kf, vf, dqf, dkf, dvf, delta, dt_cur, dt_prev):
        h = pl.program_id(0)
        qf[...] = q_ref[...].astype(F32)
        kf[...] = k_ref[...].astype(F32)
        vf[...] = v_ref[...].astype(F32)
        dqf[...] = jnp.zeros_like(dqf)
        dkf[...] = jnp.zeros_like(dkf)
        dvf[...] = jnp.zeros_like(dvf)
        delta[...] = jnp.sum(dy_ref[...] * y_ref[...], axis=-1, keepdims=True)
        lane = lax.broadcasted_iota(jnp.int32, (1, 128), 1)
        drb = jnp.zeros((1, 128), F32)
        n_cur, n_prev = _steps_back()
        for pi, (window, d) in enumerate(DIL_PATTERNS):
            t_cur, t_prev = _bias_tiles(rb_ref, h, bands[pi], window // d)
            dt_cur[...] = jnp.zeros_like(dt_cur)
            dt_prev[...] = jnp.zeros_like(dt_prev)

            def run_group(blocks, d=d, t_cur=t_cur, t_prev=t_prev):
                rows = [(_dil_rows(q0, d), _dil_rows(p0, d)) for q0, p0, _ in blocks]
                qb = [qf[cur, :].astype(BF16) for cur, _ in rows]
                kc = [kf[cur, :].astype(BF16) for cur, _ in rows]
                kp = [kf[prev, :].astype(BF16) for _, prev in rows]
                vc = [vf[cur, :].astype(BF16) for cur, _ in rows]
                vp = [vf[prev, :].astype(BF16) for _, prev in rows]
                do = [dy_ref[cur, :].astype(BF16) for cur, _ in rows]
                lse_b = [lse_ref[cur, :] for cur, _ in rows]
                delta_b = [delta[cur, :] for cur, _ in rows]
                s_c = [_nt(q, k) for q, k in zip(qb, kc)]
                s_p = [_nt(q, k) for q, k in zip(qb, kp)]
                dp_c = [_nt(g, v) for g, v in zip(do, vc)]
                dp_p = [_nt(g, v) for g, v in zip(do, vp)]
                p_c = [jnp.exp(x * scale + t_cur - ls) for x, ls in zip(s_c, lse_b)]
                p_p = [jnp.exp(x * scale + (t_prev + jnp.where(first, NEG_INF, 0.0)) - ls)
                       for x, ls, (_, _, first) in zip(s_p, lse_b, blocks)]
                ds_c = [p * (g - dl) for p, g, dl in zip(p_c, dp_c, delta_b)]
                ds_p = [p * (g - dl) for p, g, dl in zip(p_p, dp_p, delta_b)]
                dt_cur[...] += functools.reduce(jnp.add, ds_c)
                dt_prev[...] += functools.reduce(jnp.add, ds_p)
                p_c, p_p = [p.astype(BF16) for p in p_c], [p.astype(BF16) for p in p_p]
                ds_c, ds_p = [x.astype(BF16) for x in ds_c], [x.astype(BF16) for x in ds_p]
                dq = [scale * (_nn(a, k1) + _nn(b, k2)) for a, b, k1, k2 in zip(ds_c, ds_p, kc, kp)]
                dk_c = [scale * _tn(a, q) for a, q in zip(ds_c, qb)]
                dv_c = [_tn(p, g) for p, g in zip(p_c, do)]
                dk_p = [scale * _tn(b, q) for b, q in zip(ds_p, qb)]
                dv_p = [_tn(p, g) for p, g in zip(p_p, do)]
                for i, (cur, prev) in enumerate(rows):
                    dqf[cur, :] += dq[i]
                    dkf[cur, :] += dk_c[i]
                    dvf[cur, :] += dv_c[i]
                    dkf[prev, :] += dk_p[i]
                    dvf[prev, :] += dv_p[i]

            _dil_block_groups(s, d, run_group)
            dtc, dtp = dt_cur[...], dt_prev[...]
            for b, lo, hi in bands[pi]:
                tot = (jnp.sum(jnp.where((n_cur >= lo) & (n_cur <= hi), dtc, 0.0))
                       + jnp.sum(jnp.where((n_prev >= lo) & (n_prev <= hi), dtp, 0.0)))
                drb = drb + jnp.where(lane == b, tot, 0.0)
        dq_ref[...] = dqf[...].astype(BF16)
        dk_ref[...] = dkf[...].astype(BF16)
        dv_ref[...] = dvf[...].astype(BF16)
        drb_ref[...] = drb

    h_ = n_heads
    col = pl.BlockSpec((s, e), lambda h: (0, h))
    return _call(
        body, name=name, grid=(h_,),
        in_specs=[pl.BlockSpec(memory_space=pltpu.SMEM),
                  pl.BlockSpec((s, e), lambda h: (0, col0 + h)),
                  pl.BlockSpec((s, e), lambda h: (0, col0 + h_ + h)),
                  pl.BlockSpec((s, e), lambda h: (0, col0 + 2 * h_ + h)),
                  pl.BlockSpec((None, s, 1), lambda h: (h, 0, 0)), col, col],
        out_specs=[col, col, col, pl.BlockSpec((None, 1, 128), lambda h: (h, 0, 0))],
        out_shape=[jax.ShapeDtypeStruct((s, h_ * e), BF16)] * 3 + [jax.ShapeDtypeStruct((h_, 1, 128), F32)],
        operands=(rel_bias, qkv, qkv, qkv, lse, y, dy),
        scratch=[pltpu.VMEM((s, e), F32)] * 6 + [pltpu.VMEM((s, 1), F32)]
        + [pltpu.VMEM((DIL_BLOCK, DIL_BLOCK), F32)] * 2,
        semantics=("parallel",), carry=carry)


def _place():
    x, y, c = lax.axis_index("x"), lax.axis_index("y"), lax.axis_index("c")
    chips = [(1 - x, y), (x, 1 - y), (1 - x, 1 - y)]
    return x, y, c, chips


def _block_index(px, py, pc):
    return 4 * px + 2 * py + pc


def _gather_task(blocks, *, row0=0, rows=None, into=None):
    n = len(blocks)
    part = blocks[0].shape[0]
    rows = rows or part

    def copies(ins, outs, sems):
        send_sems, recv_sems, local_sems = sems
        x, y, c, chips = _place()
        me, sibling = (x, y, c), (x, y, 1 - c)

        def place_of(a, block):
            slot = outs[a].at[_block_index(*block)]
            return slot if part == rows else slot.at[pl.ds(row0, part)]

        def copy(a, k, block, to, src=None):
            slot = place_of(a, block)
            return pltpu.make_async_remote_copy(
                src_ref=slot if src is None else src, dst_ref=slot,
                send_sem=send_sems.at[a, k], recv_sem=recv_sems.at[a, k],
                device_id=to, device_id_type=MESH)

        local = lambda a: pltpu.make_async_copy(ins[a], place_of(a, me), local_sems.at[a])
        return copy, local, me, sibling, c, chips

    def start(ins, outs, sems):
        copy, local, me, sibling, c, chips = copies(ins, outs, sems)
        for a in range(n):
            local(a).start()
            copy(a, 0, me, sibling, src=ins[a]).start()
            for j, chip in enumerate(chips):
                copy(a, 1 + j, me, (*chip, c), src=ins[a]).start()

    def forward(ins, outs, sems):
        copy, local, me, sibling, c, chips = copies(ins, outs, sems)
        for a in range(n):
            for j, chip in enumerate(chips):
                copy(a, 1 + j, (*chip, c), me).wait_recv()
                copy(a, 4 + j, (*chip, c), sibling).start()

    def finish(ins, outs, sems):
        copy, local, me, sibling, c, chips = copies(ins, outs, sems)
        for a in range(n):
            copy(a, 0, sibling, me).wait_recv()
            for j, chip in enumerate(chips):
                copy(a, 4 + j, (*chip, 1 - c), me).wait_recv()
        for a in range(n):
            for k in range(7):
                copy(a, k, me, sibling, src=ins[a]).wait_send()
            local(a).wait()

    return _Carried(
        list(blocks) + list(into or []),
        [jax.ShapeDtypeStruct((N_DEV, rows) + b.shape[1:], b.dtype) for b in blocks],
        [pltpu.SemaphoreType.DMA((n, 7)), pltpu.SemaphoreType.DMA((n, 7)), pltpu.SemaphoreType.DMA((n,))],
        start, forward, finish, aliases={n + a: a for a in range(n)} if into else None)


def _exchange_task(arrays, n_slots, route):
    n = len(arrays)

    def copies(ins, outs, sems):
        send_sems, recv_sems = sems
        out = []
        for a in range(n):
            for j in range(n_slots):
                src, to = route(j)
                out.append(pltpu.make_async_remote_copy(
                    src_ref=ins[a].at[src], dst_ref=outs[a].at[j],
                    send_sem=send_sems.at[a, j], recv_sem=recv_sems.at[a, j],
                    device_id=to, device_id_type=MESH))
        return out

    def start(ins, outs, sems):
        for cp in copies(ins, outs, sems):
            cp.start()

    def finish(ins, outs, sems):
        for cp in copies(ins, outs, sems):
            cp.wait()

    return _Carried(
        arrays, [jax.ShapeDtypeStruct((n_slots,) + g.shape[1:], g.dtype) for g in arrays],
        [pltpu.SemaphoreType.DMA((n, n_slots)), pltpu.SemaphoreType.DMA((n, n_slots))],
        start, lambda ins, outs, sems: None, finish)


def _sibling_task(grads):
    def route(q):
        x, y, c, _ = _place()
        return 2 * q + (1 - c), (x, y, 1 - c)
    return _exchange_task(grads, 4, route)


def _chips_task(parts):
    def route(j):
        x, y, c, chips = _place()
        px, py = chips[j]
        return 2 * px + py, (px, py, c)
    return _exchange_task(parts, 3, route)


def _chip_partials(grad, got, place, *, name):
    _, r, cdim = grad.shape
    tr = _row_tile(r, 512)

    def body(place_ref, g_ref, s_ref, o_ref):
        o_ref[...] = (g_ref[...] + s_ref[...]).astype(BF16)

    return pl.pallas_call(
        body, name=name,
        grid_spec=pltpu.PrefetchScalarGridSpec(
            num_scalar_prefetch=1, grid=(4, r // tr),
            in_specs=[pl.BlockSpec((None, tr, cdim), lambda q, i, pos: (2 * q + pos[2], i, 0)),
                      pl.BlockSpec((None, tr, cdim), lambda q, i, pos: (q, i, 0))],
            out_specs=pl.BlockSpec((None, tr, cdim), lambda q, i, pos: (q, i, 0))),
        out_shape=jax.ShapeDtypeStruct((4, r, cdim), BF16),
        compiler_params=_params("parallel", "parallel"),
    )(place, grad, got)


def _all_reduce_small(v, *, name):
    r, w = v.shape

    def body(v_ref, o_ref, buf, send_sems, recv_sems):
        x, y, c, _ = _place()
        me = _block_index(x, y, c)
        buf[me] = v_ref[...]
        copies = []
        for k in range(1, N_DEV):
            fx, fy, fc = (k >> 2) & 1, (k >> 1) & 1, k & 1
            peer = (x ^ fx, y ^ fy, c ^ fc)
            cp = pltpu.make_async_remote_copy(
                src_ref=v_ref, dst_ref=buf.at[me],
                send_sem=send_sems.at[k - 1], recv_sem=recv_sems.at[k - 1],
                device_id=peer, device_id_type=MESH)
            cp.start()
            copies.append(cp)
        for cp in copies:
            cp.wait()
        acc = buf[0]
        for p in range(1, N_DEV):
            acc = acc + buf[p]
        o_ref[...] = acc

    vmem = pl.BlockSpec(memory_space=pltpu.VMEM)
    return pl.pallas_call(
        body, name=name, in_specs=[vmem], out_specs=vmem,
        out_shape=jax.ShapeDtypeStruct((r, w), F32),
        scratch_shapes=[pltpu.VMEM((N_DEV, r, w), F32), pltpu.SemaphoreType.DMA((N_DEV - 1,)),
                        pltpu.SemaphoreType.DMA((N_DEV - 1,))],
    )(v)


def _adamw_math(w, g, m, v):
    m = ADAM_B1 * m + (1.0 - ADAM_B1) * g
    v = ADAM_B2 * v + (1.0 - ADAM_B2) * (g * g)
    m_hat = m / (1.0 - ADAM_B1 ** ADAM_STEP)
    v_hat = v / (1.0 - ADAM_B2 ** ADAM_STEP)
    delta = -ADAM_LR * (m_hat / (jnp.sqrt(v_hat) + ADAM_EPS) + ADAM_WD * w)
    return delta, m, v


def _adamw_sharded(w, m, v, part, got, place, *, name, carry=None):
    n_l, r, cdim = w.shape
    tr = _row_tile(r, 256)

    def body(place_ref, *refs):
        w_ref, m_ref, v_ref = refs[:3]
        p_refs, g_refs = refs[3:3 + n_l], refs[3 + n_l:3 + 2 * n_l]
        g_out, d_out, m_out, v_out = refs[3 + 2 * n_l:]
        for l in range(n_l):
            @pl.when(pl.program_id(0) == l)
            def _(l=l):
                g = p_refs[l][...].astype(F32)
                for j in range(3):
                    g = g + g_refs[l][j].astype(F32)
                delta, m_new, v_new = _adamw_math(w_ref[...], g, m_ref[...], v_ref[...])
                g_out[...] = g
                d_out[...] = delta
                m_out[...] = m_new
                v_out[...] = v_new

    shard = pl.BlockSpec((None, tr, cdim), lambda l, i, pos: (l, i, 0))
    p_specs = [pl.BlockSpec((None, tr, cdim),
                            lambda l, i, pos, li=li: (2 * pos[0] + pos[1], jnp.where(l == li, i, 0), 0))
               for li in range(n_l)]
    g_specs = [pl.BlockSpec((3, tr, cdim), lambda l, i, pos, li=li: (0, jnp.where(l == li, i, 0), 0))
               for li in range(n_l)]
    return _call(
        body, name=name, grid=(n_l, r // tr), tables=(place,),
        in_specs=[shard] * 3 + p_specs + g_specs, out_specs=[shard] * 4,
        out_shape=[jax.ShapeDtypeStruct(w.shape, F32)] * 4,
        operands=(w, m, v, *part, *got), semantics=("arbitrary", "parallel"), carry=carry)


def _adamw_small(w, g, m, v, *, name):
    def body(w_ref, g_ref, m_ref, v_ref, d_out, m_out, v_out):
        delta, m_new, v_new = _adamw_math(w_ref[...], g_ref[...], m_ref[...], v_ref[...])
        d_out[...] = delta
        m_out[...] = m_new
        v_out[...] = v_new

    vmem = pl.BlockSpec(memory_space=pltpu.VMEM)
    return pl.pallas_call(
        body, name=name, in_specs=[vmem] * 4, out_specs=[vmem] * 3,
        out_shape=[jax.ShapeDtypeStruct(w.shape, F32)] * 3,
    )(w, g, m, v)


def _pad_lanes(a, width=128):
    return jnp.pad(a, ((0, 0), (0, width - a.shape[1])))


def _train_step(x, target, small, shards, place):
    w_in, w_out, w_mlp_in, w_mlp_out = shards
    depth, d_model, in_shard = w_in.shape
    n_heads = small["forget_b"].shape[1]
    dh = n_heads * HEAD_DIM
    d_ff = w_mlp_in.shape[2] * N_DEV
    row = lambda a: a.reshape(1, -1)
    block = {"in": w_in.astype(BF16), "out": w_out.astype(BF16), "mi": w_mlp_in.astype(BF16),
             "mo": w_mlp_out.astype(BF16)}
    whole = {}

    gate_lo, gate_hi = 3 * dh, 3 * dh + n_heads

    def shard_columns(g, lo, hi):
        pieces = []
        while lo < hi:
            p, a = divmod(lo, in_shard)
            b = min(in_shard, a + hi - lo)
            pieces.append(g[p][:, a:b])
            lo += b - a
        return pieces

    def in_columns(d_qkv, d_gate, lo, hi):
        pieces = []
        if lo < gate_lo:
            pieces.append(d_qkv[:, lo:min(hi, gate_lo)])
        if lo < gate_hi and hi > gate_lo:
            pieces.append(d_gate[:, max(lo, gate_lo) - gate_lo:min(hi, gate_hi) - gate_lo])
        if hi > gate_hi:
            pieces.append(d_qkv[:, max(lo, gate_hi) - n_heads:hi - n_heads])
        return pieces

    def unpack(kind, l, g):
        if kind == "in":
            whole["qkv", l] = jnp.concatenate(
                shard_columns(g, 0, gate_lo) + shard_columns(g, gate_hi, N_DEV * in_shard), axis=1)
            whole["f", l] = _pad_lanes(jnp.concatenate(shard_columns(g, gate_lo, gate_hi), axis=1))
        elif kind == "out":
            whole["out", l] = g.reshape(2 * dh, d_model)
        elif kind == "mi":
            whole["mi", l] = g
        else:
            whole["mo", l] = g.reshape(d_ff, d_model)

    half_done = {}

    def gathering(kind, l, part):
        blk = block[kind][l]
        rows = blk.shape[0]
        if part == "all":
            return _gather_task([blk]), lambda got: unpack(kind, l, got[0])
        if part == "first":
            return (_gather_task([blk[:rows // 2]], rows=rows),
                    lambda got: half_done.__setitem__((kind, l), got[0]))
        return (_gather_task([blk[rows // 2:]], row0=rows // 2, rows=rows, into=[half_done.pop((kind, l))]),
                lambda got: unpack(kind, l, got[0]))

    def carried(call, *wanted):
        wanted = [gathering(*w) for w in wanted if w[1] < depth]
        if not wanted:
            return call()
        res, got = call(carry=_join([task for task, _ in wanted]))
        for (_, done), g in zip(wanted, got):
            done([g])
        return res

    (g,) = _run_carried(_gather_task([block["in"][0]]), name="gather_w_in_0")
    unpack("in", 0, g)

    saved = []
    h1 = _rmsnorm_fwd(x, row(small["norm1_g"][0]), name="norm1_fwd_0")
    for l in range(depth):
        qkv = carried(functools.partial(_matmul, h1, whole["qkv", l], out_dtype=BF16, name=f"qkv_proj_{l}"),
                      ("mi", l, "first"))
        f = _matmul(h1, whole["f", l], name=f"gate_proj_{l}")
        fb = _pad_lanes(row(small["forget_b"][l]))
        c = _gates_fwd(f, fb, name=f"gates_fwd_{l}")
        c_heads = c[:, :n_heads].T
        cq = c_heads[:, None, :]
        ck = jnp.broadcast_to(c_heads[:, :, None] * LOG2E, c_heads.shape + (128,))
        ya, lse_a = carried(functools.partial(_fox_fwd, qkv, cq, ck, n_heads, name=f"fox_fwd_{l}"),
                            ("mi", l, "second"), ("out", l, "all"))
        yb, lse_b = carried(functools.partial(_dil_fwd, qkv, small["rel_bias"], n_heads, 3 * n_heads,
                                              name=f"dil_fwd_{l}"), ("mo", l, "first"))
        mixed = _outnorm_fwd(ya, yb, row(small["outnorm_a_g"][l]), row(small["outnorm_b_g"][l]),
                             name=f"outnorm_fwd_{l}")
        x1, h2 = _matmul(mixed, whole["out", l], addend=x, tm=512, tn=d_model, norm_gain=row(small["norm2_g"][l]),
                         name=f"out_proj_{l}")
        relu, act = carried(functools.partial(_matmul, h2, whole["mi", l], b_chunked=True, out_dtype=BF16,
                                              mode="relu_sq", name=f"mlp_in_{l}"),
                            ("mo", l, "second"))
        mlp_out = functools.partial(_matmul, act, whole["mo", l], addend=x1, name=f"mlp_out_{l}")
        saved.append((x, h1, qkv, f, fb, cq, ck, ya, lse_a, yb, lse_b, mixed, x1, h2, relu, act))
        if l + 1 < depth:
            x, h1 = carried(functools.partial(mlp_out, tm=512, tn=d_model, norm_gain=row(small["norm1_g"][l + 1])),
                            ("in", l + 1, "all"))
        else:
            x = mlp_out()
    dx, dx16, loss_tile, d_final = _loss_head(x, row(small["final_norm_g"]), target, name="loss_head")

    parts = {k: [None] * depth for k in ("in", "out", "mi", "mo")}
    got = {k: [None] * depth for k in ("in", "out", "mi", "mo")}
    small_grads = [None] * depth

    def to_sibling(call, kind, l, grad):
        res, (from_sibling,) = call(carry=_sibling_task([grad]))
        parts[kind][l] = _chip_partials(grad, from_sibling, place, name=f"chip_partials_{kind}_{l}")
        return res

    def to_chips(call, kind, l):
        if l >= depth:
            return call()
        res, (got[kind][l],) = call(carry=_chips_task([parts[kind][l]]))
        return res

    for l in reversed(range(depth)):
        x0, h1, qkv, f, fb, cq, ck, ya, lse_a, yb, lse_b, mixed, x1, h2, relu, act = saved[l]
        du = to_chips(functools.partial(_matmul, dx16, whole["mo", l], tb=True, out_dtype=BF16, mode="mul2",
                                        aux=relu, name=f"mlp_out_dx_{l}"), "in", l + 1)
        dw_mo = _matmul(act, dx16, ta=True, name=f"mlp_out_dw_{l}")
        dw_mo = dw_mo.reshape(N_DEV, d_ff // N_DEV, d_model)
        dh2 = to_sibling(functools.partial(_matmul, du, whole["mi", l], tb=True, b_chunked=True,
                                           name=f"mlp_in_dx_{l}"),
                         "mo", l, dw_mo)
        dw_mi = to_chips(functools.partial(_matmul, h2, du, ta=True, out_chunks=N_DEV,
                                           name=f"mlp_in_dw_{l}"), "mo", l)
        dx1, dx1_16, d_norm2 = _rmsnorm_bwd(x1, row(small["norm2_g"][l]), dh2, dx, name=f"norm2_bwd_{l}")
        dmixed = to_sibling(functools.partial(_matmul, dx1_16, whole["out", l], tb=True, name=f"out_proj_dx_{l}"),
                            "mi", l, dw_mi)
        dw_out = _matmul(mixed, dx1_16, ta=True, name=f"out_proj_dw_{l}")
        dw_out = dw_out.reshape(N_DEV, 2 * dh // N_DEV, d_model)
        dya, dyb, d_ga, d_gb = _outnorm_bwd(ya, yb, row(small["outnorm_a_g"][l]), row(small["outnorm_b_g"][l]),
                                            dmixed, name=f"outnorm_bwd_{l}")
        dqa, dka, dva, dcq, dck = to_chips(functools.partial(
            _fox_bwd, qkv, cq, ck, lse_a, ya, dya, n_heads, name=f"fox_bwd_{l}"), "mi", l)
        dqb, dkb, dvb, d_rb = to_sibling(functools.partial(
            _dil_bwd, qkv, small["rel_bias"], lse_b, yb, dyb, n_heads, 3 * n_heads, name=f"dil_bwd_{l}"),
            "out", l, dw_out)
        dc = _pad_lanes((dcq[:, 0, :] + dck[:, 0, :]).T)
        df, dfb = _gates_bwd(f, fb, dc, name=f"gates_bwd_{l}")
        dqkv = jnp.concatenate([dqa, dka, dva, dqb, dkb, dvb], axis=1)
        dw_qkv = to_chips(functools.partial(_matmul, h1, dqkv, ta=True, name=f"qkv_proj_dw_{l}"), "out", l)
        dw_f = _matmul(h1, df, ta=True, name=f"gate_proj_dw_{l}")
        dw_in = jnp.stack([jnp.concatenate(in_columns(dw_qkv, dw_f, p * in_shard, (p + 1) * in_shard), axis=1)
                           for p in range(N_DEV)])
        dh1 = _matmul(df, whole["f", l], tb=True, name=f"gate_proj_dx_{l}")
        dh1 = to_sibling(functools.partial(_matmul, dqkv, whole["qkv", l], tb=True, addend=dh1,
                                           name=f"qkv_proj_dx_{l}"), "in", l, dw_in)
        dx, dx16, d_norm1 = _rmsnorm_bwd(x0, row(small["norm1_g"][l]), dh1, dx1, name=f"norm1_bwd_{l}")
        small_grads[l] = dict(norm1_g=d_norm1[0], forget_b=dfb[0, :n_heads], rel_bias=d_rb[:, 0, :REL_BUCKETS].T,
                              outnorm_a_g=d_ga[0], outnorm_b_g=d_gb[0], norm2_g=d_norm2[0])
    return loss_tile[0, 0], dx, d_final[0], small_grads, parts, got


def _pack_small(parts, rows):
    flat = jnp.concatenate([p.reshape(-1).astype(F32) for p in parts])
    return jnp.pad(flat, (0, rows * 128 - flat.shape[0])).reshape(rows, 128)


def _unpack_small(packed, shapes):
    flat = packed.reshape(-1)
    out, pos = [], 0
    for shp in shapes:
        size = int(np.prod(shp))
        out.append(flat[pos:pos + size].reshape(shp))
        pos += size
    return out


def kernel(x, norm1_g, w_in, forget_b, rel_bias, outnorm_a_g, outnorm_b_g, w_out, norm2_g, w_mlp_in, w_mlp_out, final_norm_g, loss_target, m_norm1_g, m_w_in, m_forget_b, m_rel_bias, m_outnorm_a_g, m_outnorm_b_g, m_w_out, m_norm2_g, m_w_mlp_in, m_w_mlp_out, m_final_norm_g, v_norm1_g, v_w_in, v_forget_b, v_rel_bias, v_outnorm_a_g, v_outnorm_b_g, v_w_out, v_norm2_g, v_w_mlp_in, v_w_mlp_out, v_final_norm_g):
    depth, d_model, in_shard = w_in.shape
    n_heads = forget_b.shape[1]
    assert in_shard * N_DEV == 6 * n_heads * HEAD_DIM + n_heads and x.shape[0] == 1
    place = jnp.stack([lax.axis_index("x"), lax.axis_index("y"), lax.axis_index("c")]).astype(jnp.int32)

    small_names = ["norm1_g", "forget_b", "rel_bias", "outnorm_a_g", "outnorm_b_g", "norm2_g", "final_norm_g"]
    small_w = dict(norm1_g=norm1_g, forget_b=forget_b, rel_bias=rel_bias, outnorm_a_g=outnorm_a_g,
                   outnorm_b_g=outnorm_b_g, norm2_g=norm2_g, final_norm_g=final_norm_g)
    loss_part, dx, d_final, grads, parts, got = _train_step(
        x[0], loss_target[0], small_w, (w_in, w_out, w_mlp_in, w_mlp_out), place)

    small_m = dict(norm1_g=m_norm1_g, forget_b=m_forget_b, rel_bias=m_rel_bias, outnorm_a_g=m_outnorm_a_g,
                   outnorm_b_g=m_outnorm_b_g, norm2_g=m_norm2_g, final_norm_g=m_final_norm_g)
    small_v = dict(norm1_g=v_norm1_g, forget_b=v_forget_b, rel_bias=v_rel_bias, outnorm_a_g=v_outnorm_a_g,
                   outnorm_b_g=v_outnorm_b_g, norm2_g=v_norm2_g, final_norm_g=v_final_norm_g)
    small_g = {k: jnp.stack([g[k] for g in grads]) for k in small_names if k not in ("rel_bias", "final_norm_g")}
    small_g["rel_bias"] = functools.reduce(jnp.add, [g["rel_bias"] for g in grads])
    small_g["final_norm_g"] = d_final
    shapes = [small_w[k].shape for k in small_names]
    total = sum(int(np.prod(s)) for s in shapes) + 1
    rows = -(-total // (8 * 128)) * 8
    packed_g = _pack_small([small_g[k] for k in small_names] + [loss_part], rows)
    packed_g = _all_reduce_small(packed_g, name="reduce_small")
    zero = jnp.zeros((1,), F32)
    packed_w = _pack_small([small_w[k] for k in small_names] + [zero], rows)
    packed_m = _pack_small([small_m[k] for k in small_names] + [zero], rows)
    packed_v = _pack_small([small_v[k] for k in small_names] + [zero + 1.0], rows)
    packed_d, packed_nm, packed_nv = _adamw_small(packed_w, packed_g, packed_m, packed_v, name="adamw_small")
    g_small = dict(zip(small_names, _unpack_small(packed_g, shapes)))
    d_small = dict(zip(small_names, _unpack_small(packed_d, shapes)))
    nm_small = dict(zip(small_names, _unpack_small(packed_nm, shapes)))
    nv_small = dict(zip(small_names, _unpack_small(packed_nv, shapes)))
    loss = packed_g.reshape(-1)[total - 1]

    big_out = {}
    last = parts["in"][0]
    last_halves = [last[:, :last.shape[1] // 2], last[:, last.shape[1] // 2:]]
    last_got = []
    for kind, nm, w, m, v in [("mi", "w_mlp_in", w_mlp_in, m_w_mlp_in, v_w_mlp_in),
                              ("mo", "w_mlp_out", w_mlp_out, m_w_mlp_out, v_w_mlp_out),
                              ("out", "w_out", w_out, m_w_out, v_w_out), ("in", "w_in", w_in, m_w_in, v_w_in)]:
        adamw = functools.partial(_adamw_sharded, w, m, v, parts[kind], got[kind], place, name=f"adamw_{nm}")
        if last_halves:
            big_out[nm], (half,) = adamw(carry=_chips_task([last_halves.pop(0)]))
            last_got.append(half)
            if not last_halves:
                got["in"][0] = jnp.concatenate(last_got, axis=1)
        else:
            big_out[nm] = adamw()

    order = ["norm1_g", "w_in", "forget_b", "rel_bias", "outnorm_a_g", "outnorm_b_g", "w_out", "norm2_g",
             "w_mlp_in", "w_mlp_out", "final_norm_g"]
    pick = lambda k, idx, small: big_out[k][idx] if k in big_out else small[k]
    outs = [loss, dx[None]]
    outs += [pick(k, 0, g_small) for k in order]
    outs += [pick(k, 1, d_small) for k in order]
    outs += [pick(k, 2, nm_small) for k in order]
    outs += [pick(k, 3, nv_small) for k in order]
    return tuple(outs)
```

```python
import functools
import math

import numpy as np
import jax
import jax.numpy as jnp
from jax import lax
from jax.experimental import pallas as pl
from jax.experimental.pallas import tpu as pltpu

F32 = jnp.float32
BF16 = jnp.bfloat16

HEAD_DIM = 128
DIL_PATTERNS = ((128, 1), (512, 4), (2048, 16))
DIL_BLOCK = 128
REL_BUCKETS = 32
REL_MAX_DISTANCE = 2048
NORM_EPS = 1e-6
NEG_INF = -1e30
ADAM_LR = 0.001
ADAM_B1 = 0.9
ADAM_B2 = 0.999
ADAM_EPS = 1e-08
ADAM_WD = 0.01
ADAM_STEP = 10

N_DEV = 8
V7X_VMEM_LIMIT_BYTES = 56 * 1024 * 1024
MESH = pl.DeviceIdType.MESH


def _params(*semantics):
    return pltpu.CompilerParams(dimension_semantics=semantics, vmem_limit_bytes=V7X_VMEM_LIMIT_BYTES)


HBM_SPEC = pl.BlockSpec(memory_space=pltpu.HBM)


class _Carried:
    def __init__(self, operands, out_shape, scratch, start, forward, finish, aliases=None):
        self.operands, self.out_shape, self.scratch = list(operands), list(out_shape), list(scratch)
        self.start, self.forward, self.finish = start, forward, finish
        self.aliases = dict(aliases or {})


def _join(tasks):
    if len(tasks) == 1:
        return tasks[0]
    spans, aliases = [], {}
    i0 = o0 = s0 = 0
    for t in tasks:
        i1, o1, s1 = i0 + len(t.operands), o0 + len(t.out_shape), s0 + len(t.scratch)
        spans.append((slice(i0, i1), slice(o0, o1), slice(s0, s1)))
        aliases.update({i0 + i: o0 + o for i, o in t.aliases.items()})
        i0, o0, s0 = i1, o1, s1

    def phase(which):
        def run(ins, outs, sems):
            for t, (si, so, ss) in zip(tasks, spans):
                getattr(t, which)(ins[si], outs[so], sems[ss])
        return run

    return _Carried(sum((t.operands for t in tasks), []), sum((t.out_shape for t in tasks), []),
                    sum((t.scratch for t in tasks), []), phase("start"), phase("forward"), phase("finish"),
                    aliases)


FORWARD_AT = 0.8


def _call(body, *, name, grid, in_specs, out_specs, out_shape, operands, scratch=(), semantics, carry=None,
          tables=()):
    in_specs, out_specs, out_shape, scratch = list(in_specs), list(out_specs), list(out_shape), list(scratch)
    n_tab = len(tables)

    def run(fn, in_specs, out_specs, out_shape, scratch, operands, semantics, aliases):
        return pl.pallas_call(
            fn, name=name,
            grid_spec=pltpu.PrefetchScalarGridSpec(
                num_scalar_prefetch=n_tab, grid=grid, in_specs=in_specs, out_specs=out_specs,
                scratch_shapes=scratch),
            out_shape=out_shape, input_output_aliases=aliases,
            compiler_params=_params(*semantics))(*tables, *operands)

    if carry is None:
        return run(body, in_specs, out_specs, out_shape, scratch, operands, semantics, {})
    n_in, n_out = len(in_specs), len(out_specs)
    ci, co, cs = len(carry.operands), len(carry.out_shape), len(carry.scratch)
    steps = int(np.prod(grid))
    forward_step = min(int(steps * FORWARD_AT), steps - 1)

    def carrying(*refs):
        tabs, refs = refs[:n_tab], refs[n_tab:]
        main_in, c_in = refs[:n_in], refs[n_in:n_in + ci]
        main_out = refs[n_in + ci:n_in + ci + n_out]
        c_out = refs[n_in + ci + n_out:n_in + ci + n_out + co]
        rest = refs[n_in + ci + n_out + co:]
        main_scr, c_scr = rest[:len(rest) - cs], rest[len(rest) - cs:]
        step = 0
        for axis, extent in enumerate(grid):
            step = step * extent + pl.program_id(axis)

        @pl.when(step == 0)
        def _():
            carry.start(c_in, c_out, c_scr)

        body(*tabs, *main_in, *main_out, *main_scr)

        @pl.when(step == forward_step)
        def _():
            carry.forward(c_in, c_out, c_scr)

        @pl.when(step == steps - 1)
        def _():
            carry.finish(c_in, c_out, c_scr)

    aliases = {n_tab + n_in + i: n_out + o for i, o in carry.aliases.items()}
    res = run(carrying, in_specs + [HBM_SPEC] * ci, out_specs + [HBM_SPEC] * co, out_shape + carry.out_shape,
              scratch + carry.scratch, (*operands, *carry.operands), ["arbitrary"] * len(grid), aliases)
    return res[:n_out], res[n_out:]


def _run_carried(carry, *, name):
    ci, co = len(carry.operands), len(carry.out_shape)

    def body(*refs):
        parts = (refs[:ci], refs[ci:ci + co], refs[ci + co:])
        carry.start(*parts)
        carry.forward(*parts)
        carry.finish(*parts)

    return pl.pallas_call(
        body, name=name, in_specs=[HBM_SPEC] * ci, out_specs=[HBM_SPEC] * co,
        out_shape=carry.out_shape, scratch_shapes=carry.scratch)(*carry.operands)


def _fit(dim, want, unit=128):
    if dim <= want:
        return dim
    t = want - want % unit
    while dim % t:
        t -= unit
    return t


def _matmul(a, b, *, name, ta=False, tb=False, out_dtype=F32, tm=1024, tn=1024, tk=2048,
            addend=None, mode=None, aux=None, out_chunks=None, b_chunked=False, norm_gain=None, carry=None):
    m_dim, k_dim = (a.shape[1], a.shape[0]) if ta else a.shape
    b_rows, b_cols = (b.shape[1], b.shape[0] * b.shape[2]) if b_chunked else b.shape
    n_dim = b_rows if tb else b_cols
    assert (b_cols if tb else b_rows) == k_dim
    b_chunk = b.shape[2] if b_chunked else b_cols
    tm, tn, tk = _fit(m_dim, tm), _fit(n_dim // (out_chunks or 1), tn), _fit(k_dim, tk)
    k_chunks = 1
    if b_chunked and tb:
        assert tk % b_chunk == 0 and not ta
        k_chunks = tk // b_chunk
    elif b_chunked:
        tn = _fit(b_chunk, tn)
    assert m_dim % tm == 0 and n_dim % tn == 0 and k_dim % tk == 0, (name, a.shape, b.shape)
    nk = k_dim // tk
    a_spec = (pl.BlockSpec((tk, tm), lambda i, j, k: (k, i)) if ta
              else pl.BlockSpec((tm, tk), lambda i, j, k: (i, k)))
    if b_chunked and tb:
        b_spec = pl.BlockSpec((k_chunks, tn, b_chunk), lambda i, j, k: (k, j, 0))
    elif b_chunked:
        per_b = b_chunk // tn
        b_spec = pl.BlockSpec((None, tk, tn), lambda i, j, k: (j // per_b, k, j % per_b))
    else:
        b_spec = (pl.BlockSpec((tn, tk), lambda i, j, k: (j, k)) if tb
                  else pl.BlockSpec((tk, tn), lambda i, j, k: (k, j)))
    mn_spec = pl.BlockSpec((tm, tn), lambda i, j, k: (i, j))
    if out_chunks is None:
        o_spec, o_shape = mn_spec, (m_dim, n_dim)
    else:
        per = n_dim // out_chunks // tn
        assert per * tn * out_chunks == n_dim
        o_spec = pl.BlockSpec((None, tm, tn), lambda i, j, k: (j // per, i, j % per))
        o_shape = (out_chunks, m_dim, n_dim // out_chunks)
    dims = (((0 if ta else 1,), (1 if tb else 0,)), ((), ()))
    n_out = 2 if (mode == "relu_sq" or norm_gain is not None) else 1
    in_place = mode is None and out_dtype == F32
    in_specs, operands = [a_spec, b_spec], [a, b]
    if addend is not None:
        in_specs.append(mn_spec)
        operands.append(addend)
    if mode == "mul2":
        in_specs.append(mn_spec)
        operands.append(aux)
    if norm_gain is not None:
        assert in_place and tn == n_dim and out_chunks is None
        in_specs.append(pl.BlockSpec((1, tn), lambda i, j, k: (0, 0)))
        operands.append(norm_gain)

    def body(*refs):
        a_ref, b_ref = refs[0], refs[1]
        pos = 2
        add_ref = aux_ref = gain_ref = None
        if addend is not None:
            add_ref, pos = refs[pos], pos + 1
        if mode == "mul2":
            aux_ref, pos = refs[pos], pos + 1
        if norm_gain is not None:
            gain_ref, pos = refs[pos], pos + 1
        outs = refs[pos:pos + n_out]

        def normed(c):
            if gain_ref is not None:
                outs[1][...] = _norm_fwd_math(c, gain_ref[...]).astype(BF16)

        def finish(acc):
            if add_ref is not None:
                acc = acc + add_ref[...].astype(F32)
            if mode == "relu_sq":
                r = jnp.maximum(acc, 0.0)
                outs[0][...] = r.astype(outs[0].dtype)
                outs[1][...] = (r * r).astype(outs[1].dtype)
            elif mode == "mul2":
                outs[0][...] = (acc * (2.0 * aux_ref[...].astype(F32))).astype(outs[0].dtype)
            else:
                outs[0][...] = acc.astype(outs[0].dtype)
                if nk == 1:
                    normed(acc)

        if k_chunks > 1:
            part = functools.reduce(jnp.add, [
                lax.dot_general(a_ref[:, c * b_chunk:(c + 1) * b_chunk].astype(BF16), b_ref[c].astype(BF16), dims,
                                preferred_element_type=F32) for c in range(k_chunks)])
        else:
            b_tile = b_ref[0] if (b_chunked and tb) else b_ref[...]
            part = lax.dot_general(a_ref[...].astype(BF16), b_tile.astype(BF16), dims,
                                   preferred_element_type=F32)
        k = pl.program_id(2)
        if nk == 1:
            finish(part)
        elif in_place:
            @pl.when(k == 0)
            def _():
                finish(part)

            @pl.when(k > 0)
            def _():
                outs[0][...] += part

            if gain_ref is not None:
                @pl.when(k == nk - 1)
                def _():
                    normed(outs[0][...])
        else:
            acc_ref = refs[-1]

            @pl.when(k == 0)
            def _():
                acc_ref[...] = part

            @pl.when(k > 0)
            def _():
                acc_ref[...] += part

            @pl.when(k == nk - 1)
            def _():
                finish(acc_ref[...])

    out_shape = [jax.ShapeDtypeStruct(o_shape, out_dtype)] * n_out
    if norm_gain is not None:
        out_shape[1] = jax.ShapeDtypeStruct(o_shape, BF16)
    res = _call(
        body, name=name, grid=(m_dim // tm, n_dim // tn, nk),
        in_specs=in_specs, out_specs=[o_spec] * n_out, out_shape=out_shape, operands=operands,
        scratch=[pltpu.VMEM((tm, tn), F32)] if nk > 1 and not in_place else [],
        semantics=("parallel", "parallel", "arbitrary"), carry=carry)
    if carry is not None:
        res, carried = res
        return (res if n_out > 1 else res[0]), carried
    return res if n_out > 1 else res[0]


def _norm_fwd_math(x, g):
    r = lax.rsqrt(jnp.mean(x * x, axis=-1, keepdims=True) + NORM_EPS)
    return (x * r) * g


def _norm_bwd_math(x, g, dy):
    r = lax.rsqrt(jnp.mean(x * x, axis=-1, keepdims=True) + NORM_EPS)
    xh = x * r
    dxh = dy * g
    dx = r * (dxh - xh * jnp.mean(dxh * xh, axis=-1, keepdims=True))
    return dx, jnp.sum(dy * xh, axis=0, keepdims=True)


def _row_tile(rows, want=256):
    t = min(rows, want)
    assert rows % t == 0
    return t


def _rmsnorm_fwd(x, g, *, name):
    s, d = x.shape
    tr = _row_tile(s)

    def body(x_ref, g_ref, h_ref):
        h_ref[...] = _norm_fwd_math(x_ref[...], g_ref[...]).astype(BF16)

    return pl.pallas_call(
        body, name=name, grid=(s // tr,),
        in_specs=[pl.BlockSpec((tr, d), lambda i: (i, 0)), pl.BlockSpec((1, d), lambda i: (0, 0))],
        out_specs=pl.BlockSpec((tr, d), lambda i: (i, 0)),
        out_shape=jax.ShapeDtypeStruct((s, d), BF16),
        compiler_params=_params("parallel"),
    )(x, g)


def _rmsnorm_bwd(x, g, dh, dres, *, name, carry=None):
    s, d = x.shape
    tr = _row_tile(s)

    def body(x_ref, g_ref, dh_ref, dres_ref, dx_ref, dx16_ref, dg_ref):
        dx, dg = _norm_bwd_math(x_ref[...], g_ref[...], dh_ref[...])
        dx = dres_ref[...] + dx
        dx_ref[...] = dx
        dx16_ref[...] = dx.astype(BF16)

        @pl.when(pl.program_id(0) == 0)
        def _():
            dg_ref[...] = dg

        @pl.when(pl.program_id(0) > 0)
        def _():
            dg_ref[...] += dg

    row = pl.BlockSpec((tr, d), lambda i: (i, 0))
    vec = pl.BlockSpec((1, d), lambda i: (0, 0))
    return _call(
        body, name=name, grid=(s // tr,),
        in_specs=[row, vec, row, row], out_specs=[row, row, vec],
        out_shape=[jax.ShapeDtypeStruct((s, d), F32), jax.ShapeDtypeStruct((s, d), BF16),
                   jax.ShapeDtypeStruct((1, d), F32)],
        operands=(x, g, dh, dres), semantics=("arbitrary",), carry=carry)


def _outnorm_fwd(ya, yb, ga, gb, *, name):
    s, da = ya.shape
    db = yb.shape[1]
    tr = _row_tile(s)

    def body(ya_ref, yb_ref, ga_ref, gb_ref, o_ref):
        o_ref[:, :da] = _norm_fwd_math(ya_ref[...], ga_ref[...]).astype(BF16)
        o_ref[:, da:] = _norm_fwd_math(yb_ref[...], gb_ref[...]).astype(BF16)

    return pl.pallas_call(
        body, name=name, grid=(s // tr,),
        in_specs=[pl.BlockSpec((tr, da), lambda i: (i, 0)), pl.BlockSpec((tr, db), lambda i: (i, 0)),
                  pl.BlockSpec((1, da), lambda i: (0, 0)), pl.BlockSpec((1, db), lambda i: (0, 0))],
        out_specs=pl.BlockSpec((tr, da + db), lambda i: (i, 0)),
        out_shape=jax.ShapeDtypeStruct((s, da + db), BF16),
        compiler_params=_params("parallel"),
    )(ya, yb, ga, gb)


def _outnorm_bwd(ya, yb, ga, gb, dmixed, *, name):
    s, da = ya.shape
    db = yb.shape[1]
    tr = _row_tile(s)

    def body(ya_ref, yb_ref, ga_ref, gb_ref, dm_ref, dya_ref, dyb_ref, dga_ref, dgb_ref):
        dxa, dga = _norm_bwd_math(ya_ref[...], ga_ref[...], dm_ref[:, :da])
        dxb, dgb = _norm_bwd_math(yb_ref[...], gb_ref[...], dm_ref[:, da:])
        dya_ref[...] = dxa
        dyb_ref[...] = dxb

        @pl.when(pl.program_id(0) == 0)
        def _():
            dga_ref[...] = dga
            dgb_ref[...] = dgb

        @pl.when(pl.program_id(0) > 0)
        def _():
            dga_ref[...] += dga
            dgb_ref[...] += dgb

    ra = pl.BlockSpec((tr, da), lambda i: (i, 0))
    rb = pl.BlockSpec((tr, db), lambda i: (i, 0))
    va = pl.BlockSpec((1, da), lambda i: (0, 0))
    vb = pl.BlockSpec((1, db), lambda i: (0, 0))
    return pl.pallas_call(
        body, name=name, grid=(s // tr,),
        in_specs=[ra, rb, va, vb, pl.BlockSpec((tr, da + db), lambda i: (i, 0))],
        out_specs=[ra, rb, va, vb],
        out_shape=[jax.ShapeDtypeStruct((s, da), F32), jax.ShapeDtypeStruct((s, db), F32),
                   jax.ShapeDtypeStruct((1, da), F32), jax.ShapeDtypeStruct((1, db), F32)],
        compiler_params=_params("arbitrary"),
    )(ya, yb, ga, gb, dmixed)


def _loss_head(x, g, target, *, name):
    s, d = x.shape
    tr = _row_tile(s)

    def body(x_ref, g_ref, t_ref, dx_ref, dx16_ref, loss_ref, dg_ref):
        xv, gv = x_ref[...], g_ref[...]
        err = _norm_fwd_math(xv, gv) - t_ref[...]
        part = 0.5 * jnp.sum(jnp.mean(err * err, axis=-1, keepdims=True), axis=0, keepdims=True)
        dx, dg = _norm_bwd_math(xv, gv, err * (1.0 / d))
        dx_ref[...] = dx
        dx16_ref[...] = dx.astype(BF16)
        part = jnp.broadcast_to(part, (8, 128))

        @pl.when(pl.program_id(0) == 0)
        def _():
            dg_ref[...] = dg
            loss_ref[...] = part

        @pl.when(pl.program_id(0) > 0)
        def _():
            dg_ref[...] += dg
            loss_ref[...] += part

    row = pl.BlockSpec((tr, d), lambda i: (i, 0))
    vec = pl.BlockSpec((1, d), lambda i: (0, 0))
    return pl.pallas_call(
        body, name=name, grid=(s // tr,),
        in_specs=[row, vec, row],
        out_specs=[row, row, pl.BlockSpec((8, 128), lambda i: (0, 0)), vec],
        out_shape=[jax.ShapeDtypeStruct((s, d), F32), jax.ShapeDtypeStruct((s, d), BF16),
                   jax.ShapeDtypeStruct((8, 128), F32), jax.ShapeDtypeStruct((1, d), F32)],
        compiler_params=_params("arbitrary"),
    )(x, g, target)


def _split3(x):
    hi = x.astype(BF16)
    rem = x - hi.astype(F32)
    mid = rem.astype(BF16)
    lo = (rem - mid.astype(F32)).astype(BF16)
    return hi, mid, lo


def _tri_sum(tri, x):
    hi, mid, lo = _split3(x)
    dot = functools.partial(jnp.dot, preferred_element_type=F32)
    return dot(tri, hi) + dot(tri, mid) + dot(tri, lo)


def _gates_fwd(f, fb, *, name):
    s, w = f.shape
    tb = 128
    nb = s // tb

    def body(f_ref, fb_ref, c_ref, carry):
        @pl.when(pl.program_id(0) == 0)
        def _():
            carry[...] = jnp.zeros_like(carry)

        logf = jax.nn.log_sigmoid(f_ref[...] + fb_ref[...])
        row = lax.broadcasted_iota(jnp.int32, (tb, tb), 0)
        col = lax.broadcasted_iota(jnp.int32, (tb, tb), 1)
        tri = (row >= col).astype(BF16)
        c = _tri_sum(tri, logf) + carry[...]
        c_ref[...] = c
        carry[...] = c[tb - 1:tb, :]

    return pl.pallas_call(
        body, name=name, grid=(nb,),
        in_specs=[pl.BlockSpec((tb, w), lambda i: (i, 0)), pl.BlockSpec((1, w), lambda i: (0, 0))],
        out_specs=pl.BlockSpec((tb, w), lambda i: (i, 0)),
        out_shape=jax.ShapeDtypeStruct((s, w), F32),
        scratch_shapes=[pltpu.VMEM((1, w), F32)],
        compiler_params=_params("arbitrary"),
    )(f, fb)


def _gates_bwd(f, fb, dc, *, name):
    s, w = f.shape
    tb = 128
    nb = s // tb

    def body(f_ref, fb_ref, dc_ref, df_ref, dfb_ref, carry):
        @pl.when(pl.program_id(0) == 0)
        def _():
            carry[...] = jnp.zeros_like(carry)
            dfb_ref[...] = jnp.zeros_like(dfb_ref)

        row = lax.broadcasted_iota(jnp.int32, (tb, tb), 0)
        col = lax.broadcasted_iota(jnp.int32, (tb, tb), 1)
        tri = (row <= col).astype(BF16)
        dlogf = _tri_sum(tri, dc_ref[...]) + carry[...]
        carry[...] = dlogf[0:1, :]
        df = dlogf * jax.nn.sigmoid(-(f_ref[...] + fb_ref[...]))
        df_ref[...] = df
        dfb_ref[...] += jnp.sum(df, axis=0, keepdims=True)

    rev = pl.BlockSpec((tb, w), lambda i: (nb - 1 - i, 0))
    vec = pl.BlockSpec((1, w), lambda i: (0, 0))
    return pl.pallas_call(
        body, name=name, grid=(nb,),
        in_specs=[rev, vec, rev], out_specs=[rev, vec],
        out_shape=[jax.ShapeDtypeStruct((s, w), F32), jax.ShapeDtypeStruct((1, w), F32)],
        scratch_shapes=[pltpu.VMEM((1, w), F32)],
        compiler_params=_params("arbitrary"),
    )(f, fb, dc)


def _nt(a, b):
    return lax.dot_general(a, b, (((1,), (1,)), ((), ())), preferred_element_type=F32)


def _tn(a, b):
    return lax.dot_general(a, b, (((0,), (0,)), ((), ())), preferred_element_type=F32)


def _nn(a, b):
    return jnp.dot(a, b, preferred_element_type=F32)


DIL_GROUP = 8
FOX_TILE = 1024
LOG2E = 1.4426950408889634


def _causal_pairs(nt, key_major):
    if key_major:
        pairs = [(q, k) for k in range(nt) for q in range(k, nt)]
    else:
        pairs = [(q, k) for q in range(nt) for k in range(q + 1)]
    return (jnp.asarray([p[0] for p in pairs], jnp.int32), jnp.asarray([p[1] for p in pairs], jnp.int32))


def _fox_fwd(qkv, c_row, ck_lanes, n_heads, *, name, carry=None):
    s = qkv.shape[0]
    e = HEAD_DIM
    t = min(FOX_TILE, s)
    nt = s // t
    scale2 = e ** -0.5 * LOG2E
    lanes = 128
    q_tab, k_tab = _causal_pairs(nt, key_major=False)

    def body(q_tab, k_tab, q_ref, k_ref, v_ref, cq_ref, ck_ref, o_ref, lse_ref,
             m_scr, l_scr, acc_scr, s_scr, p_scr):
        pair = pl.program_id(1)
        qi, ki = q_tab[pair], k_tab[pair]

        @pl.when(ki == 0)
        def _():
            m_scr[...] = jnp.full_like(m_scr, NEG_INF)
            l_scr[...] = jnp.zeros_like(l_scr)
            acc_scr[...] = jnp.zeros_like(acc_scr)

        def update(diagonal):
            s_scr[...] = _nt(k_ref[...], q_ref[...])
            ck2 = ck_ref[...]
            for c0 in range(0, t, lanes):
                cols = pl.ds(c0, lanes)

                def logits2():
                    x = s_scr[:, cols] * scale2 - ck2
                    if diagonal:
                        key = lax.broadcasted_iota(jnp.int32, x.shape, 0)
                        qry = c0 + lax.broadcasted_iota(jnp.int32, x.shape, 1)
                        x = jnp.where(key <= qry, x, NEG_INF)
                    return x

                m_old = m_scr[:, cols]
                m_new = jnp.maximum(m_old, jnp.max(logits2(), axis=0, keepdims=True))
                p = jnp.exp2(logits2() - m_new)
                alpha = jnp.exp2(m_old - m_new)
                l_scr[:, cols] = alpha * l_scr[:, cols] + jnp.sum(p, axis=0, keepdims=True)
                m_scr[:, cols] = m_new
                acc_scr[:, cols] = alpha * acc_scr[:, cols]
                p_scr[:, cols] = p.astype(BF16)
            acc_scr[...] += _tn(v_ref[...], p_scr[...])

        @pl.when(ki < qi)
        def _():
            update(False)

        @pl.when(ki == qi)
        def _():
            update(True)
            o_ref[...] = (acc_scr[...] / l_scr[...]).T
            lse_ref[...] = (m_scr[...] + jnp.log2(l_scr[...])) * (1.0 / LOG2E) + cq_ref[...]

    h_ = n_heads
    return _call(
        body, name=name, grid=(h_, int(q_tab.shape[0])), tables=(q_tab, k_tab),
        in_specs=[
            pl.BlockSpec((t, e), lambda h, p, qt, kt: (qt[p], h)),
            pl.BlockSpec((t, e), lambda h, p, qt, kt: (kt[p], h_ + h)),
            pl.BlockSpec((t, e), lambda h, p, qt, kt: (kt[p], 2 * h_ + h)),
            pl.BlockSpec((None, 1, t), lambda h, p, qt, kt: (h, 0, qt[p])),
            pl.BlockSpec((None, t, lanes), lambda h, p, qt, kt: (h, kt[p], 0)),
        ],
        out_specs=[pl.BlockSpec((t, e), lambda h, p, qt, kt: (qt[p], h)),
                   pl.BlockSpec((None, 1, t), lambda h, p, qt, kt: (h, 0, qt[p]))],
        out_shape=[jax.ShapeDtypeStruct((s, h_ * e), F32), jax.ShapeDtypeStruct((h_, 1, s), F32)],
        operands=(qkv, qkv, qkv, c_row, ck_lanes),
        scratch=[pltpu.VMEM((1, t), F32), pltpu.VMEM((1, t), F32), pltpu.VMEM((e, t), F32),
                 pltpu.VMEM((t, t), F32), pltpu.VMEM((t, t), BF16)],
        semantics=("parallel", "arbitrary"), carry=carry)


def _fox_bwd(qkv, c_row, ck_lanes, lse, y, dy, n_heads, *, name, carry=None):
    s = qkv.shape[0]
    e = HEAD_DIM
    t = min(FOX_TILE, s)
    nt = s // t
    scale = e ** -0.5
    scale2 = scale * LOG2E
    lanes = 128
    q_tab, k_tab = _causal_pairs(nt, key_major=False)
    n_pairs = int(q_tab.shape[0])

    def body(q_tab, k_tab, q_ref, k_ref, v_ref, c_ref, ck_ref, lse_ref, y_ref, dy_ref,
             dq_ref, dk_ref, dv_ref, dcq_ref, dck_ref,
             dq_scr, dk_scr, dv_scr, dck_scr, s_scr, dp_scr, p_scr, ds_scr, do_scr, delta_scr, shift_scr):
        pair = pl.program_id(1)
        qi, ki = q_tab[pair], k_tab[pair]

        @pl.when(pair == 0)
        def _():
            dk_scr[...] = jnp.zeros_like(dk_scr)
            dv_scr[...] = jnp.zeros_like(dv_scr)
            dck_scr[...] = jnp.zeros_like(dck_scr)

        @pl.when(ki == 0)
        def _():
            do = dy_ref[...]
            delta_scr[...] = lax.dot_general(jnp.ones((8, e), F32), do * y_ref[...], (((1,), (1,)), ((), ())),
                                             precision=lax.Precision.HIGHEST, preferred_element_type=F32)
            shift_scr[...] = (lse_ref[...] - c_ref[...]) * LOG2E
            do_scr[...] = do.astype(BF16)
            dq_scr[...] = jnp.zeros_like(dq_scr)
            dcq_ref[...] = jnp.zeros_like(dcq_ref)

        def update(diagonal):
            s_scr[...] = _nt(k_ref[...], q_ref[...])
            dp_scr[...] = _nt(v_ref[...], do_scr[...])
            ck2 = ck_ref[...]
            k_rows = pl.ds(pl.multiple_of(ki * t, t), t)
            for c0 in range(0, t, lanes):
                cols = pl.ds(c0, lanes)
                x = s_scr[:, cols] * scale2 - ck2
                if diagonal:
                    key = lax.broadcasted_iota(jnp.int32, x.shape, 0)
                    qry = c0 + lax.broadcasted_iota(jnp.int32, x.shape, 1)
                    x = jnp.where(key <= qry, x, NEG_INF)
                p = jnp.exp2(x - shift_scr[:, cols])
                ds = p * (dp_scr[:, cols] - delta_scr[0:1, cols])
                dcq_ref[:, cols] += jnp.sum(ds, axis=0, keepdims=True)
                dck_scr[k_rows, :] += ds
                p_scr[:, cols] = p.astype(BF16)
                ds_scr[:, cols] = ds.astype(BF16)
            dv_scr[k_rows, :] += _nn(p_scr[...], do_scr[...])
            dk_scr[k_rows, :] += scale * _nn(ds_scr[...], q_ref[...])
            dq_scr[...] += scale * _tn(k_ref[...], ds_scr[...])

        @pl.when(ki < qi)
        def _():
            update(False)

        @pl.when(ki == qi)
        def _():
            update(True)
            dq_ref[...] = dq_scr[...].T.astype(BF16)

        @pl.when(pair == n_pairs - 1)
        def _():
            dk_ref[...] = dk_scr[...].astype(BF16)
            dv_ref[...] = dv_scr[...].astype(BF16)
            dck_ref[...] = -lax.dot_general(jnp.ones((8, lanes), F32), dck_scr[...], (((1,), (1,)), ((), ())),
                                            precision=lax.Precision.HIGHEST, preferred_element_type=F32)[0:1]

    h_ = n_heads
    q_row = pl.BlockSpec((None, 1, t), lambda h, p, qt, kt: (h, 0, qt[p]))
    q_blk = pl.BlockSpec((t, e), lambda h, p, qt, kt: (qt[p], h))
    whole = pl.BlockSpec((s, e), lambda h, p, qt, kt: (0, h))
    return _call(
        body, name=name, grid=(h_, n_pairs), tables=(q_tab, k_tab),
        in_specs=[
            q_blk,
            pl.BlockSpec((t, e), lambda h, p, qt, kt: (kt[p], h_ + h)),
            pl.BlockSpec((t, e), lambda h, p, qt, kt: (kt[p], 2 * h_ + h)),
            q_row,
            pl.BlockSpec((None, t, lanes), lambda h, p, qt, kt: (h, kt[p], 0)),
            q_row, q_blk, q_blk,
        ],
        out_specs=[q_blk, whole, whole, q_row,
                   pl.BlockSpec((None, 1, s), lambda h, p, qt, kt: (h, 0, 0))],
        out_shape=[jax.ShapeDtypeStruct((s, h_ * e), BF16)] * 3
        + [jax.ShapeDtypeStruct((h_, 1, s), F32), jax.ShapeDtypeStruct((h_, 1, s), F32)],
        operands=(qkv, qkv, qkv, c_row, ck_lanes, lse, y, dy),
        scratch=[pltpu.VMEM((e, t), F32), pltpu.VMEM((s, e), F32), pltpu.VMEM((s, e), F32),
                 pltpu.VMEM((s, lanes), F32), pltpu.VMEM((t, t), F32), pltpu.VMEM((t, t), F32), pltpu.VMEM((t, t), BF16),
                 pltpu.VMEM((t, t), BF16), pltpu.VMEM((t, e), BF16), pltpu.VMEM((8, t), F32),
                 pltpu.VMEM((1, t), F32)],
        semantics=("parallel", "arbitrary"), carry=carry)


def _rel_bucket_table(dilation, span):
    dist = np.arange(span + 1, dtype=np.int64) * dilation
    max_exact = REL_BUCKETS // 2
    d = np.maximum(dist.astype(np.float32), np.float32(1.0))
    large = max_exact + (np.log(d / np.float32(max_exact)) / np.float32(math.log(REL_MAX_DISTANCE / max_exact))
                         * np.float32(REL_BUCKETS - max_exact)).astype(np.int32)
    large = np.minimum(large, REL_BUCKETS - 1)
    return np.where(dist < max_exact, dist, large)


def _bucket_bands(dilation, span):
    table = _rel_bucket_table(dilation, span)
    bands = []
    for n, b in enumerate(table):
        if bands and bands[-1][0] == int(b):
            bands[-1][2] = n
        else:
            assert not any(bb[0] == int(b) for bb in bands)
            bands.append([int(b), n, n])
    return [tuple(b) for b in bands]


def _steps_back():
    i = lax.broadcasted_iota(jnp.int32, (DIL_BLOCK, DIL_BLOCK), 0)
    j = lax.broadcasted_iota(jnp.int32, (DIL_BLOCK, DIL_BLOCK), 1)
    return i - j, DIL_BLOCK + i - j


def _bias_tiles(rb_ref, h, bands, span):
    n_cur, n_prev = _steps_back()
    t_cur = jnp.zeros((DIL_BLOCK, DIL_BLOCK), F32)
    t_prev = jnp.zeros((DIL_BLOCK, DIL_BLOCK), F32)
    for b, lo, hi in bands:
        val = rb_ref[b, h]
        t_cur = jnp.where((n_cur >= lo) & (n_cur <= hi), val, t_cur)
        t_prev = jnp.where((n_prev >= lo) & (n_prev <= hi), val, t_prev)
    t_cur = jnp.where(n_cur >= 0, t_cur, NEG_INF)
    t_prev = jnp.where(n_prev <= span, t_prev, NEG_INF)
    return t_cur, t_prev


def _dil_rows(start, dilation):
    return pl.ds(start, DIL_BLOCK, stride=dilation) if dilation > 1 else pl.ds(start, DIL_BLOCK)


def _dil_block_groups(s, dilation, run_group):
    group = dilation * DIL_BLOCK
    per_residue = s // group

    def block(r, n):
        if isinstance(n, int):
            return n * group + r, max(n - 1, 0) * group + r, n == 0
        return (pl.multiple_of(n * group, DIL_BLOCK) + r,
                pl.multiple_of(jnp.maximum(n - 1, 0) * group, DIL_BLOCK) + r, n == 0)

    if per_residue >= DIL_GROUP:
        assert per_residue % DIL_GROUP == 0
        for r in range(dilation):
            def trip(it, carry, r=r):
                run_group([block(r, it * DIL_GROUP + j) for j in range(DIL_GROUP)])
                return carry
            lax.fori_loop(0, per_residue // DIL_GROUP, trip, 0)
    else:
        residues = DIL_GROUP // per_residue
        assert residues * per_residue == DIL_GROUP and dilation % residues == 0
        for r0 in range(0, dilation, residues):
            run_group([block(r, n) for r in range(r0, r0 + residues) for n in range(per_residue)])


def _dil_fwd(qkv, rel_bias, n_heads, col0, *, name, carry=None):
    s = qkv.shape[0]
    e = HEAD_DIM
    scale = e ** -0.5
    n_pat = len(DIL_PATTERNS)
    for window, d in DIL_PATTERNS:
        assert window // d == DIL_BLOCK and s % (d * DIL_BLOCK) == 0
    bands = [_bucket_bands(d, w // d) for w, d in DIL_PATTERNS]

    def body(rb_ref, q_ref, k_ref, v_ref, y_ref, lse_ref, qf, kf, vf, *scr):
        o_scr, l_scr = scr[:n_pat], scr[n_pat:]
        h = pl.program_id(0)
        qf[...] = q_ref[...].astype(F32)
        kf[...] = k_ref[...].astype(F32)
        vf[...] = v_ref[...].astype(F32)
        for pi, (window, d) in enumerate(DIL_PATTERNS):
            t_cur, t_prev = _bias_tiles(rb_ref, h, bands[pi], window // d)

            def run_group(blocks, d=d, pi=pi, t_cur=t_cur, t_prev=t_prev):
                rows = [(_dil_rows(q0, d), _dil_rows(p0, d)) for q0, p0, _ in blocks]
                qb = [qf[cur, :].astype(BF16) for cur, _ in rows]
                s_c = [_nt(q, kf[cur, :].astype(BF16)) for q, (cur, _) in zip(qb, rows)]
                s_p = [_nt(q, kf[prev, :].astype(BF16)) for q, (_, prev) in zip(qb, rows)]
                s_c = [x * scale + t_cur for x in s_c]
                s_p = [x * scale + (t_prev + jnp.where(first, NEG_INF, 0.0)) for x, (_, _, first) in zip(s_p, blocks)]
                m = [jnp.max(jnp.maximum(a, b), axis=-1, keepdims=True) for a, b in zip(s_c, s_p)]
                p_c = [jnp.exp(a - mm) for a, mm in zip(s_c, m)]
                p_p = [jnp.exp(b - mm) for b, mm in zip(s_p, m)]
                l = [jnp.sum(a + b, axis=-1, keepdims=True) for a, b in zip(p_c, p_p)]
                o = [_nn(a.astype(BF16), vf[cur, :].astype(BF16)) + _nn(b.astype(BF16), vf[prev, :].astype(BF16))
                     for a, b, (cur, prev) in zip(p_c, p_p, rows)]
                for (cur, _), oo, ll, mm in zip(rows, o, l, m):
                    o_scr[pi][cur, :] = oo / ll
                    l_scr[pi][cur, :] = mm + jnp.log(ll)

            _dil_block_groups(s, d, run_group)
        lses = [l_scr[pi][...] for pi in range(n_pat)]
        m = functools.reduce(jnp.maximum, lses)
        ws = [jnp.exp(l - m) for l in lses]
        tot = functools.reduce(jnp.add, ws)
        y = functools.reduce(jnp.add, [w * o_scr[pi][...] for pi, w in enumerate(ws)])
        y_ref[...] = y / tot
        lse_ref[...] = m + jnp.log(tot)

    h_ = n_heads
    return _call(
        body, name=name, grid=(h_,),
        in_specs=[pl.BlockSpec(memory_space=pltpu.SMEM),
                  pl.BlockSpec((s, e), lambda h: (0, col0 + h)),
                  pl.BlockSpec((s, e), lambda h: (0, col0 + h_ + h)),
                  pl.BlockSpec((s, e), lambda h: (0, col0 + 2 * h_ + h))],
        out_specs=[pl.BlockSpec((s, e), lambda h: (0, h)), pl.BlockSpec((None, s, 1), lambda h: (h, 0, 0))],
        out_shape=[jax.ShapeDtypeStruct((s, h_ * e), F32), jax.ShapeDtypeStruct((h_, s, 1), F32)],
        operands=(rel_bias, qkv, qkv, qkv),
        scratch=[pltpu.VMEM((s, e), F32)] * 3 + [pltpu.VMEM((s, e), F32)] * n_pat
        + [pltpu.VMEM((s, 1), F32)] * n_pat,
        semantics=("parallel",), carry=carry)


def _dil_bwd(qkv, rel_bias, lse, y, dy, n_heads, col0, *, name, carry=None):
    s = qkv.shape[0]
    e = HEAD_DIM
    scale = e ** -0.5
    bands = [_bucket_bands(d, w // d) for w, d in DIL_PATTERNS]

    def body(rb_ref, q_ref, k_ref, v_ref, lse_ref, y_ref, dy_ref, dq_ref, dk_ref, dv_ref, drb_ref,
             qf, kf, vf, dqf, dkf, dvf, delta, dt_cur, dt_prev):
        h = pl.program_id(0)
        qf[...] = q_ref[...].astype(F32)
        kf[...] = k_ref[...].astype(F32)
        vf[...] = v_ref[...].astype(F32)
        dqf[...] = jnp.zeros_like(dqf)
        dkf[...] = jnp.zeros_like(dkf)
        dvf[...] = jnp.zeros_like(dvf)
        delta[...] = jnp.sum(dy_ref[...] * y_ref[...], axis=-1, keepdims=True)
        lane = lax.broadcasted_iota(jnp.int32, (1, 128), 1)
        drb = jnp.zeros((1, 128), F32)
        n_cur, n_prev = _steps_back()
        for pi, (window, d) in enumerate(DIL_PATTERNS):
            t_cur, t_prev = _bias_tiles(rb_ref, h, bands[pi], window // d)
            dt_cur[...] = jnp.zeros_like(dt_cur)
            dt_prev[...] = jnp.zeros_like(dt_prev)

            def run_group(blocks, d=d, t_cur=t_cur, t_prev=t_prev):
                rows = [(_dil_rows(q0, d), _dil_rows(p0, d)) for q0, p0, _ in blocks]
                qb = [qf[cur, :].astype(BF16) for cur, _ in rows]
                kc = [kf[cur, :].astype(BF16) for cur, _ in rows]
                kp = [kf[prev, :].astype(BF16) for _, prev in rows]
                vc = [vf[cur, :].astype(BF16) for cur, _ in rows]
                vp = [vf[prev, :].astype(BF16) for _, prev in rows]
                do = [dy_ref[cur, :].astype(BF16) for cur, _ in rows]
                lse_b = [lse_ref[cur, :] for cur, _ in rows]
                delta_b = [delta[cur, :] for cur, _ in rows]
                s_c = [_nt(q, k) for q, k in zip(qb, kc)]
                s_p = [_nt(q, k) for q, k in zip(qb, kp)]
                dp_c = [_nt(g, v) for g, v in zip(do, vc)]
                dp_p = [_nt(g, v) for g, v in zip(do, vp)]
                p_c = [jnp.exp(x * scale + t_cur - ls) for x, ls in zip(s_c, lse_b)]
                p_p = [jnp.exp(x * scale + (t_prev + jnp.where(first, NEG_INF, 0.0)) - ls)
                       for x, ls, (_, _, first) in zip(s_p, lse_b, blocks)]
                ds_c = [p * (g - dl) for p, g, dl in zip(p_c, dp_c, delta_b)]
                ds_p = [p * (g - dl) for p, g, dl in zip(p_p, dp_p, delta_b)]
                dt_cur[...] += functools.reduce(jnp.add, ds_c)
                dt_prev[...] += functools.reduce(jnp.add, ds_p)
                p_c, p_p = [p.astype(BF16) for p in p_c], [p.astype(BF16) for p in p_p]
                ds_c, ds_p = [x.astype(BF16) for x in ds_c], [x.astype(BF16) for x in ds_p]
                dq = [scale * (_nn(a, k1) + _nn(b, k2)) for a, b, k1, k2 in zip(ds_c, ds_p, kc, kp)]
                dk_c = [scale * _tn(a, q) for a, q in zip(ds_c, qb)]
                dv_c = [_tn(p, g) for p, g in zip(p_c, do)]
                dk_p = [scale * _tn(b, q) for b, q in zip(ds_p, qb)]
                dv_p = [_tn(p, g) for p, g in zip(p_p, do)]
                for i, (cur, prev) in enumerate(rows):
                    dqf[cur, :] += dq[i]
                    dkf[cur, :] += dk_c[i]
                    dvf[cur, :] += dv_c[i]
                    dkf[prev, :] += dk_p[i]
                    dvf[prev, :] += dv_p[i]

            _dil_block_groups(s, d, run_group)
            dtc, dtp = dt_cur[...], dt_prev[...]
            for b, lo, hi in bands[pi]:
                tot = (jnp.sum(jnp.where((n_cur >= lo) & (n_cur <= hi), dtc, 0.0))
                       + jnp.sum(jnp.where((n_prev >= lo) & (n_prev <= hi), dtp, 0.0)))
                drb = drb + jnp.where(lane == b, tot, 0.0)
        dq_ref[...] = dqf[...].astype(BF16)
        dk_ref[...] = dkf[...].astype(BF16)
        dv_ref[...] = dvf[...].astype(BF16)
        drb_ref[...] = drb

    h_ = n_heads
    col = pl.BlockSpec((s, e), lambda h: (0, h))
    return _call(
        body, name=name, grid=(h_,),
        in_specs=[pl.BlockSpec(memory_space=pltpu.SMEM),
                  pl.BlockSpec((s, e), lambda h: (0, col0 + h)),
                  pl.BlockSpec((s, e), lambda h: (0, col0 + h_ + h)),
                  pl.BlockSpec((s, e), lambda h: (0, col0 + 2 * h_ + h)),
                  pl.BlockSpec((None, s, 1), lambda h: (h, 0, 0)), col, col],
        out_specs=[col, col, col, pl.BlockSpec((None, 1, 128), lambda h: (h, 0, 0))],
        out_shape=[jax.ShapeDtypeStruct((s, h_ * e), BF16)] * 3 + [jax.ShapeDtypeStruct((h_, 1, 128), F32)],
        operands=(rel_bias, qkv, qkv, qkv, lse, y, dy),
        scratch=[pltpu.VMEM((s, e), F32)] * 6 + [pltpu.VMEM((s, 1), F32)]
        + [pltpu.VMEM((DIL_BLOCK, DIL_BLOCK), F32)] * 2,
        semantics=("parallel",), carry=carry)


def _place():
    x, y, c = lax.axis_index("x"), lax.axis_index("y"), lax.axis_index("c")
    chips = [(1 - x, y), (x, 1 - y), (1 - x, 1 - y)]
    return x, y, c, chips


def _block_index(px, py, pc):
    return 4 * px + 2 * py + pc


def _gather_task(blocks, *, row0=0, rows=None, into=None):
    n = len(blocks)
    part = blocks[0].shape[0]
    rows = rows or part

    def copies(ins, outs, sems):
        send_sems, recv_sems, local_sems = sems
        x, y, c, chips = _place()
        me, sibling = (x, y, c), (x, y, 1 - c)

        def place_of(a, block):
            slot = outs[a].at[_block_index(*block)]
            return slot if part == rows else slot.at[pl.ds(row0, part)]

        def copy(a, k, block, to, src=None):
            slot = place_of(a, block)
            return pltpu.make_async_remote_copy(
                src_ref=slot if src is None else src, dst_ref=slot,
                send_sem=send_sems.at[a, k], recv_sem=recv_sems.at[a, k],
                device_id=to, device_id_type=MESH)

        local = lambda a: pltpu.make_async_copy(ins[a], place_of(a, me), local_sems.at[a])
        return copy, local, me, sibling, c, chips

    def start(ins, outs, sems):
        copy, local, me, sibling, c, chips = copies(ins, outs, sems)
        for a in range(n):
            local(a).start()
            copy(a, 0, me, sibling, src=ins[a]).start()
            for j, chip in enumerate(chips):
                copy(a, 1 + j, me, (*chip, c), src=ins[a]).start()

    def forward(ins, outs, sems):
        copy, local, me, sibling, c, chips = copies(ins, outs, sems)
        for a in range(n):
            for j, chip in enumerate(chips):
                copy(a, 1 + j, (*chip, c), me).wait_recv()
                copy(a, 4 + j, (*chip, c), sibling).start()

    def finish(ins, outs, sems):
        copy, local, me, sibling, c, chips = copies(ins, outs, sems)
        for a in range(n):
            copy(a, 0, sibling, me).wait_recv()
            for j, chip in enumerate(chips):
                copy(a, 4 + j, (*chip, 1 - c), me).wait_recv()
        for a in range(n):
            for k in range(7):
                copy(a, k, me, sibling, src=ins[a]).wait_send()
            local(a).wait()

    return _Carried(
        list(blocks) + list(into or []),
        [jax.ShapeDtypeStruct((N_DEV, rows) + b.shape[1:], b.dtype) for b in blocks],
        [pltpu.SemaphoreType.DMA((n, 7)), pltpu.SemaphoreType.DMA((n, 7)), pltpu.SemaphoreType.DMA((n,))],
        start, forward, finish, aliases={n + a: a for a in range(n)} if into else None)


def _exchange_task(arrays, n_slots, route):
    n = len(arrays)

    def copies(ins, outs, sems):
        send_sems, recv_sems = sems
        out = []
        for a in range(n):
            for j in range(n_slots):
                src, to = route(j)
                out.append(pltpu.make_async_remote_copy(
                    src_ref=ins[a].at[src], dst_ref=outs[a].at[j],
                    send_sem=send_sems.at[a, j], recv_sem=recv_sems.at[a, j],
                    device_id=to, device_id_type=MESH))
        return out

    def start(ins, outs, sems):
        for cp in copies(ins, outs, sems):
            cp.start()

    def finish(ins, outs, sems):
        for cp in copies(ins, outs, sems):
            cp.wait()

    return _Carried(
        arrays, [jax.ShapeDtypeStruct((n_slots,) + g.shape[1:], g.dtype) for g in arrays],
        [pltpu.SemaphoreType.DMA((n, n_slots)), pltpu.SemaphoreType.DMA((n, n_slots))],
        start, lambda ins, outs, sems: None, finish)


def _sibling_task(grads):
    def route(q):
        x, y, c, _ = _place()
        return 2 * q + (1 - c), (x, y, 1 - c)
    return _exchange_task(grads, 4, route)


def _chips_task(parts):
    def route(j):
        x, y, c, chips = _place()
        px, py = chips[j]
        return 2 * px + py, (px, py, c)
    return _exchange_task(parts, 3, route)


def _chip_partials(grad, got, place, *, name):
    _, r, cdim = grad.shape
    tr = _row_tile(r, 512)

    def body(place_ref, g_ref, s_ref, o_ref):
        o_ref[...] = (g_ref[...] + s_ref[...]).astype(BF16)

    return pl.pallas_call(
        body, name=name,
        grid_spec=pltpu.PrefetchScalarGridSpec(
            num_scalar_prefetch=1, grid=(4, r // tr),
            in_specs=[pl.BlockSpec((None, tr, cdim), lambda q, i, pos: (2 * q + pos[2], i, 0)),
                      pl.BlockSpec((None, tr, cdim), lambda q, i, pos: (q, i, 0))],
            out_specs=pl.BlockSpec((None, tr, cdim), lambda q, i, pos: (q, i, 0))),
        out_shape=jax.ShapeDtypeStruct((4, r, cdim), BF16),
        compiler_params=_params("parallel", "parallel"),
    )(place, grad, got)


def _all_reduce_small(v, *, name):
    r, w = v.shape

    def body(v_ref, o_ref, buf, send_sems, recv_sems):
        x, y, c, _ = _place()
        me = _block_index(x, y, c)
        buf[me] = v_ref[...]
        copies = []
        for k in range(1, N_DEV):
            fx, fy, fc = (k >> 2) & 1, (k >> 1) & 1, k & 1
            peer = (x ^ fx, y ^ fy, c ^ fc)
            cp = pltpu.make_async_remote_copy(
                src_ref=v_ref, dst_ref=buf.at[me],
                send_sem=send_sems.at[k - 1], recv_sem=recv_sems.at[k - 1],
                device_id=peer, device_id_type=MESH)
            cp.start()
            copies.append(cp)
        for cp in copies:
            cp.wait()
        acc = buf[0]
        for p in range(1, N_DEV):
            acc = acc + buf[p]
        o_ref[...] = acc

    vmem = pl.BlockSpec(memory_space=pltpu.VMEM)
    return pl.pallas_call(
        body, name=name, in_specs=[vmem], out_specs=vmem,
        out_shape=jax.ShapeDtypeStruct((r, w), F32),
        scratch_shapes=[pltpu.VMEM((N_DEV, r, w), F32), pltpu.SemaphoreType.DMA((N_DEV - 1,)),
                        pltpu.SemaphoreType.DMA((N_DEV - 1,))],
    )(v)


def _adamw_math(w, g, m, v):
    m = ADAM_B1 * m + (1.0 - ADAM_B1) * g
    v = ADAM_B2 * v + (1.0 - ADAM_B2) * (g * g)
    m_hat = m / (1.0 - ADAM_B1 ** ADAM_STEP)
    v_hat = v / (1.0 - ADAM_B2 ** ADAM_STEP)
    delta = -ADAM_LR * (m_hat / (jnp.sqrt(v_hat) + ADAM_EPS) + ADAM_WD * w)
    return delta, m, v


def _adamw_sharded(w, m, v, part, got, place, *, name, carry=None):
    n_l, r, cdim = w.shape
    tr = _row_tile(r, 256)

    def body(place_ref, *refs):
        w_ref, m_ref, v_ref = refs[:3]
        p_refs, g_refs = refs[3:3 + n_l], refs[3 + n_l:3 + 2 * n_l]
        g_out, d_out, m_out, v_out = refs[3 + 2 * n_l:]
        for l in range(n_l):
            @pl.when(pl.program_id(0) == l)
            def _(l=l):
                g = p_refs[l][...].astype(F32)
                for j in range(3):
                    g = g + g_refs[l][j].astype(F32)
                delta, m_new, v_new = _adamw_math(w_ref[...], g, m_ref[...], v_ref[...])
                g_out[...] = g
                d_out[...] = delta
                m_out[...] = m_new
                v_out[...] = v_new

    shard = pl.BlockSpec((None, tr, cdim), lambda l, i, pos: (l, i, 0))
    p_specs = [pl.BlockSpec((None, tr, cdim),
                            lambda l, i, pos, li=li: (2 * pos[0] + pos[1], jnp.where(l == li, i, 0), 0))
               for li in range(n_l)]
    g_specs = [pl.BlockSpec((3, tr, cdim), lambda l, i, pos, li=li: (0, jnp.where(l == li, i, 0), 0))
               for li in range(n_l)]
    return _call(
        body, name=name, grid=(n_l, r // tr), tables=(place,),
        in_specs=[shard] * 3 + p_specs + g_specs, out_specs=[shard] * 4,
        out_shape=[jax.ShapeDtypeStruct(w.shape, F32)] * 4,
        operands=(w, m, v, *part, *got), semantics=("arbitrary", "parallel"), carry=carry)


def _adamw_small(w, g, m, v, *, name):
    def body(w_ref, g_ref, m_ref, v_ref, d_out, m_out, v_out):
        delta, m_new, v_new = _adamw_math(w_ref[...], g_ref[...], m_ref[...], v_ref[...])
        d_out[...] = delta
        m_out[...] = m_new
        v_out[...] = v_new

    vmem = pl.BlockSpec(memory_space=pltpu.VMEM)
    return pl.pallas_call(
        body, name=name, in_specs=[vmem] * 4, out_specs=[vmem] * 3,
        out_shape=[jax.ShapeDtypeStruct(w.shape, F32)] * 3,
    )(w, g, m, v)


def _pad_lanes(a, width=128):
    return jnp.pad(a, ((0, 0), (0, width - a.shape[1])))


def _train_step(x, target, small, shards, place):
    w_in, w_out, w_mlp_in, w_mlp_out = shards
    depth, d_model, in_shard = w_in.shape
    n_heads = small["forget_b"].shape[1]
    dh = n_heads * HEAD_DIM
    d_ff = w_mlp_in.shape[2] * N_DEV
    row = lambda a: a.reshape(1, -1)
    block = {"in": w_in.astype(BF16), "out": w_out.astype(BF16), "mi": w_mlp_in.astype(BF16),
             "mo": w_mlp_out.astype(BF16)}
    whole = {}

    gate_lo, gate_hi = 3 * dh, 3 * dh + n_heads

    def shard_columns(g, lo, hi):
        pieces = []
        while lo < hi:
            p, a = divmod(lo, in_shard)
            b = min(in_shard, a + hi - lo)
            pieces.append(g[p][:, a:b])
            lo += b - a
        return pieces

    def in_columns(d_qkv, d_gate, lo, hi):
        pieces = []
        if lo < gate_lo:
            pieces.append(d_qkv[:, lo:min(hi, gate_lo)])
        if lo < gate_hi and hi > gate_lo:
            pieces.append(d_gate[:, max(lo, gate_lo) - gate_lo:min(hi, gate_hi) - gate_lo])
        if hi > gate_hi:
            pieces.append(d_qkv[:, max(lo, gate_hi) - n_heads:hi - n_heads])
        return pieces

    def unpack(kind, l, g):
        if kind == "in":
            whole["qkv", l] = jnp.concatenate(
                shard_columns(g, 0, gate_lo) + shard_columns(g, gate_hi, N_DEV * in_shard), axis=1)
            whole["f", l] = _pad_lanes(jnp.concatenate(shard_columns(g, gate_lo, gate_hi), axis=1))
        elif kind == "out":
            whole["out", l] = g.reshape(2 * dh, d_model)
        elif kind == "mi":
            whole["mi", l] = g
        else:
            whole["mo", l] = g.reshape(d_ff, d_model)

    half_done = {}

    def gathering(kind, l, part):
        blk = block[kind][l]
        rows = blk.shape[0]
        if part == "all":
            return _gather_task([blk]), lambda got: unpack(kind, l, got[0])
        if part == "first":
            return (_gather_task([blk[:rows // 2]], rows=rows),
                    lambda got: half_done.__setitem__((kind, l), got[0]))
        return (_gather_task([blk[rows // 2:]], row0=rows // 2, rows=rows, into=[half_done.pop((kind, l))]),
                lambda got: unpack(kind, l, got[0]))

    def carried(call, *wanted):
        wanted = [gathering(*w) for w in wanted if w[1] < depth]
        if not wanted:
            return call()
        res, got = call(carry=_join([task for task, _ in wanted]))
        for (_, done), g in zip(wanted, got):
            done([g])
        return res

    (g,) = _run_carried(_gather_task([block["in"][0]]), name="gather_w_in_0")
    unpack("in", 0, g)

    saved = []
    h1 = _rmsnorm_fwd(x, row(small["norm1_g"][0]), name="norm1_fwd_0")
    for l in range(depth):
        qkv = carried(functools.partial(_matmul, h1, whole["qkv", l], out_dtype=BF16, name=f"qkv_proj_{l}"),
                      ("mi", l, "first"))
        f = _matmul(h1, whole["f", l], name=f"gate_proj_{l}")
        fb = _pad_lanes(row(small["forget_b"][l]))
        c = _gates_fwd(f, fb, name=f"gates_fwd_{l}")
        c_heads = c[:, :n_heads].T
        cq = c_heads[:, None, :]
        ck = jnp.broadcast_to(c_heads[:, :, None] * LOG2E, c_heads.shape + (128,))
        ya, lse_a = carried(functools.partial(_fox_fwd, qkv, cq, ck, n_heads, name=f"fox_fwd_{l}"),
                            ("mi", l, "second"), ("out", l, "all"))
        yb, lse_b = carried(functools.partial(_dil_fwd, qkv, small["rel_bias"], n_heads, 3 * n_heads,
                                              name=f"dil_fwd_{l}"), ("mo", l, "first"))
        mixed = _outnorm_fwd(ya, yb, row(small["outnorm_a_g"][l]), row(small["outnorm_b_g"][l]),
                             name=f"outnorm_fwd_{l}")
        x1, h2 = _matmul(mixed, whole["out", l], addend=x, tm=512, tn=d_model, norm_gain=row(small["norm2_g"][l]),
                         name=f"out_proj_{l}")
        relu, act = carried(functools.partial(_matmul, h2, whole["mi", l], b_chunked=True, out_dtype=BF16,
                                              mode="relu_sq", name=f"mlp_in_{l}"),
                            ("mo", l, "second"))
        mlp_out = functools.partial(_matmul, act, whole["mo", l], addend=x1, name=f"mlp_out_{l}")
        saved.append((x, h1, qkv, f, fb, cq, ck, ya, lse_a, yb, lse_b, mixed, x1, h2, relu, act))
        if l + 1 < depth:
            x, h1 = carried(functools.partial(mlp_out, tm=512, tn=d_model, norm_gain=row(small["norm1_g"][l + 1])),
                            ("in", l + 1, "all"))
        else:
            x = mlp_out()
    dx, dx16, loss_tile, d_final = _loss_head(x, row(small["final_norm_g"]), target, name="loss_head")

    parts = {k: [None] * depth for k in ("in", "out", "mi", "mo")}
    got = {k: [None] * depth for k in ("in", "out", "mi", "mo")}
    small_grads = [None] * depth

    def to_sibling(call, kind, l, grad):
        res, (from_sibling,) = call(carry=_sibling_task([grad]))
        parts[kind][l] = _chip_partials(grad, from_sibling, place, name=f"chip_partials_{kind}_{l}")
        return res

    def to_chips(call, kind, l):
        if l >= depth:
            return call()
        res, (got[kind][l],) = call(carry=_chips_task([parts[kind][l]]))
        return res

    for l in reversed(range(depth)):
        x0, h1, qkv, f, fb, cq, ck, ya, lse_a, yb, lse_b, mixed, x1, h2, relu, act = saved[l]
        du = to_chips(functools.partial(_matmul, dx16, whole["mo", l], tb=True, out_dtype=BF16, mode="mul2",
                                        aux=relu, name=f"mlp_out_dx_{l}"), "in", l + 1)
        dw_mo = _matmul(act, dx16, ta=True, name=f"mlp_out_dw_{l}")
        dw_mo = dw_mo.reshape(N_DEV, d_ff // N_DEV, d_model)
        dh2 = to_sibling(functools.partial(_matmul, du, whole["mi", l], tb=True, b_chunked=True,
                                           name=f"mlp_in_dx_{l}"),
                         "mo", l, dw_mo)
        dw_mi = to_chips(functools.partial(_matmul, h2, du, ta=True, out_chunks=N_DEV,
                                           name=f"mlp_in_dw_{l}"), "mo", l)
        dx1, dx1_16, d_norm2 = _rmsnorm_bwd(x1, row(small["norm2_g"][l]), dh2, dx, name=f"norm2_bwd_{l}")
        dmixed = to_sibling(functools.partial(_matmul, dx1_16, whole["out", l], tb=True, name=f"out_proj_dx_{l}"),
                            "mi", l, dw_mi)
        dw_out = _matmul(mixed, dx1_16, ta=True, name=f"out_proj_dw_{l}")
        dw_out = dw_out.reshape(N_DEV, 2 * dh // N_DEV, d_model)
        dya, dyb, d_ga, d_gb = _outnorm_bwd(ya, yb, row(small["outnorm_a_g"][l]), row(small["outnorm_b_g"][l]),
                                            dmixed, name=f"outnorm_bwd_{l}")
        dqa, dka, dva, dcq, dck = to_chips(functools.partial(
            _fox_bwd, qkv, cq, ck, lse_a, ya, dya, n_heads, name=f"fox_bwd_{l}"), "mi", l)
        dqb, dkb, dvb, d_rb = to_sibling(functools.partial(
            _dil_bwd, qkv, small["rel_bias"], lse_b, yb, dyb, n_heads, 3 * n_heads, name=f"dil_bwd_{l}"),
            "out", l, dw_out)
        dc = _pad_lanes((dcq[:, 0, :] + dck[:, 0, :]).T)
        df, dfb = _gates_bwd(f, fb, dc, name=f"gates_bwd_{l}")
        dqkv = jnp.concatenate([dqa, dka, dva, dqb, dkb, dvb], axis=1)
        dw_qkv = to_chips(functools.partial(_matmul, h1, dqkv, ta=True, name=f"qkv_proj_dw_{l}"), "out", l)
        dw_f = _matmul(h1, df, ta=True, name=f"gate_proj_dw_{l}")
        dw_in = jnp.stack([jnp.concatenate(in_columns(dw_qkv, dw_f, p * in_shard, (p + 1) * in_shard), axis=1)
                           for p in range(N_DEV)])
        dh1 = _matmul(df, whole["f", l], tb=True, name=f"gate_proj_dx_{l}")
        dh1 = to_sibling(functools.partial(_matmul, dqkv, whole["qkv", l], tb=True, addend=dh1,
                                           name=f"qkv_proj_dx_{l}"), "in", l, dw_in)
        dx, dx16, d_norm1 = _rmsnorm_bwd(x0, row(small["norm1_g"][l]), dh1, dx1, name=f"norm1_bwd_{l}")
        small_grads[l] = dict(norm1_g=d_norm1[0], forget_b=dfb[0, :n_heads], rel_bias=d_rb[:, 0, :REL_BUCKETS].T,
                              outnorm_a_g=d_ga[0], outnorm_b_g=d_gb[0], norm2_g=d_norm2[0])
    return loss_tile[0, 0], dx, d_final[0], small_grads, parts, got


def _pack_small(parts, rows):
    flat = jnp.concatenate([p.reshape(-1).astype(F32) for p in parts])
    return jnp.pad(flat, (0, rows * 128 - flat.shape[0])).reshape(rows, 128)


def _unpack_small(packed, shapes):
    flat = packed.reshape(-1)
    out, pos = [], 0
    for shp in shapes:
        size = int(np.prod(shp))
        out.append(flat[pos:pos + size].reshape(shp))
        pos += size
    return out


def kernel(x, norm1_g, w_in, forget_b, rel_bias, outnorm_a_g, outnorm_b_g, w_out, norm2_g, w_mlp_in, w_mlp_out, final_norm_g, loss_target, m_norm1_g, m_w_in, m_forget_b, m_rel_bias, m_outnorm_a_g, m_outnorm_b_g, m_w_out, m_norm2_g, m_w_mlp_in, m_w_mlp_out, m_final_norm_g, v_norm1_g, v_w_in, v_forget_b, v_rel_bias, v_outnorm_a_g, v_outnorm_b_g, v_w_out, v_norm2_g, v_w_mlp_in, v_w_mlp_out, v_final_norm_g):
    depth, d_model, in_shard = w_in.shape
    n_heads = forget_b.shape[1]
    assert in_shard * N_DEV == 6 * n_heads * HEAD_DIM + n_heads and x.shape[0] == 1
    place = jnp.stack([lax.axis_index("x"), lax.axis_index("y"), lax.axis_index("c")]).astype(jnp.int32)

    small_names = ["norm1_g", "forget_b", "rel_bias", "outnorm_a_g", "outnorm_b_g", "norm2_g", "final_norm_g"]
    small_w = dict(norm1_g=norm1_g, forget_b=forget_b, rel_bias=rel_bias, outnorm_a_g=outnorm_a_g,
                   outnorm_b_g=outnorm_b_g, norm2_g=norm2_g, final_norm_g=final_norm_g)
    loss_part, dx, d_final, grads, parts, got = _train_step(
        x[0], loss_target[0], small_w, (w_in, w_out, w_mlp_in, w_mlp_out), place)

    small_m = dict(norm1_g=m_norm1_g, forget_b=m_forget_b, rel_bias=m_rel_bias, outnorm_a_g=m_outnorm_a_g,
                   outnorm_b_g=m_outnorm_b_g, norm2_g=m_norm2_g, final_norm_g=m_final_norm_g)
    small_v = dict(norm1_g=v_norm1_g, forget_b=v_forget_b, rel_bias=v_rel_bias, outnorm_a_g=v_outnorm_a_g,
                   outnorm_b_g=v_outnorm_b_g, norm2_g=v_norm2_g, final_norm_g=v_final_norm_g)
    small_g = {k: jnp.stack([g[k] for g in grads]) for k in small_names if k not in ("rel_bias", "final_norm_g")}
    small_g["rel_bias"] = functools.reduce(jnp.add, [g["rel_bias"] for g in grads])
    small_g["final_norm_g"] = d_final
    shapes = [small_w[k].shape for k in small_names]
    total = sum(int(np.prod(s)) for s in shapes) + 1
    rows = -(-total // (8 * 128)) * 8
    packed_g = _pack_small([small_g[k] for k in small_names] + [loss_part], rows)
    packed_g = _all_reduce_small(packed_g, name="reduce_small")
    zero = jnp.zeros((1,), F32)
    packed_w = _pack_small([small_w[k] for k in small_names] + [zero], rows)
    packed_m = _pack_small([small_m[k] for k in small_names] + [zero], rows)
    packed_v = _pack_small([small_v[k] for k in small_names] + [zero + 1.0], rows)
    packed_d, packed_nm, packed_nv = _adamw_small(packed_w, packed_g, packed_m, packed_v, name="adamw_small")
    g_small = dict(zip(small_names, _unpack_small(packed_g, shapes)))
    d_small = dict(zip(small_names, _unpack_small(packed_d, shapes)))
    nm_small = dict(zip(small_names, _unpack_small(packed_nm, shapes)))
    nv_small = dict(zip(small_names, _unpack_small(packed_nv, shapes)))
    loss = packed_g.reshape(-1)[total - 1]

    big_out = {}
    last = parts["in"][0]
    last_halves = [last[:, :last.shape[1] // 2], last[:, last.shape[1] // 2:]]
    last_got = []
    for kind, nm, w, m, v in [("mi", "w_mlp_in", w_mlp_in, m_w_mlp_in, v_w_mlp_in),
                              ("mo", "w_mlp_out", w_mlp_out, m_w_mlp_out, v_w_mlp_out),
                              ("out", "w_out", w_out, m_w_out, v_w_out), ("in", "w_in", w_in, m_w_in, v_w_in)]:
        adamw = functools.partial(_adamw_sharded, w, m, v, parts[kind], got[kind], place, name=f"adamw_{nm}")
        if last_halves:
            big_out[nm], (half,) = adamw(carry=_chips_task([last_halves.pop(0)]))
            last_got.append(half)
            if not last_halves:
                got["in"][0] = jnp.concatenate(last_got, axis=1)
        else:
            big_out[nm] = adamw()

    order = ["norm1_g", "w_in", "forget_b", "rel_bias", "outnorm_a_g", "outnorm_b_g", "w_out", "norm2_g",
             "w_mlp_in", "w_mlp_out", "final_norm_g"]
    pick = lambda k, idx, small: big_out[k][idx] if k in big_out else small[k]
    outs = [loss, dx[None]]
    outs += [pick(k, 0, g_small) for k in order]
    outs += [pick(k, 1, d_small) for k in order]
    outs += [pick(k, 2, nm_small) for k in order]
    outs += [pick(k, 3, nv_small) for k in order]
    return tuple(outs)
```

```python
import functools
import math

import numpy as np
import jax
import jax.numpy as jnp
from jax import lax
from jax.experimental import pallas as pl
from jax.experimental.pallas import tpu as pltpu

F32 = jnp.float32
BF16 = jnp.bfloat16

HEAD_DIM = 128
DIL_PATTERNS = ((128, 1), (512, 4), (2048, 16))
DIL_BLOCK = 128
REL_BUCKETS = 32
REL_MAX_DISTANCE = 2048
NORM_EPS = 1e-6
NEG_INF = -1e30
ADAM_LR = 0.001
ADAM_B1 = 0.9
ADAM_B2 = 0.999
ADAM_EPS = 1e-08
ADAM_WD = 0.01
ADAM_STEP = 10

N_DEV = 8
V7X_VMEM_LIMIT_BYTES = 56 * 1024 * 1024
MESH = pl.DeviceIdType.MESH


def _params(*semantics):
    return pltpu.CompilerParams(dimension_semantics=semantics, vmem_limit_bytes=V7X_VMEM_LIMIT_BYTES)


HBM_SPEC = pl.BlockSpec(memory_space=pltpu.HBM)


class _Carried:
    def __init__(self, operands, out_shape, scratch, start, forward, finish, aliases=None):
        self.operands, self.out_shape, self.scratch = list(operands), list(out_shape), list(scratch)
        self.start, self.forward, self.finish = start, forward, finish
        self.aliases = dict(aliases or {})


def _join(tasks):
    if len(tasks) == 1:
        return tasks[0]
    spans, aliases = [], {}
    i0 = o0 = s0 = 0
    for t in tasks:
        i1, o1, s1 = i0 + len(t.operands), o0 + len(t.out_shape), s0 + len(t.scratch)
        spans.append((slice(i0, i1), slice(o0, o1), slice(s0, s1)))
        aliases.update({i0 + i: o0 + o for i, o in t.aliases.items()})
        i0, o0, s0 = i1, o1, s1

    def phase(which):
        def run(ins, outs, sems):
            for t, (si, so, ss) in zip(tasks, spans):
                getattr(t, which)(ins[si], outs[so], sems[ss])
        return run

    return _Carried(sum((t.operands for t in tasks), []), sum((t.out_shape for t in tasks), []),
                    sum((t.scratch for t in tasks), []), phase("start"), phase("forward"), phase("finish"),
                    aliases)


FORWARD_AT = 0.8


def _call(body, *, name, grid, in_specs, out_specs, out_shape, operands, scratch=(), semantics, carry=None,
          tables=()):
    in_specs, out_specs, out_shape, scratch = list(in_specs), list(out_specs), list(out_shape), list(scratch)
    n_tab = len(tables)

    def run(fn, in_specs, out_specs, out_shape, scratch, operands, semantics, aliases):
        return pl.pallas_call(
            fn, name=name,
            grid_spec=pltpu.PrefetchScalarGridSpec(
                num_scalar_prefetch=n_tab, grid=grid, in_specs=in_specs, out_specs=out_specs,
                scratch_shapes=scratch),
            out_shape=out_shape, input_output_aliases=aliases,
            compiler_params=_params(*semantics))(*tables, *operands)

    if carry is None:
        return run(body, in_specs, out_specs, out_shape, scratch, operands, semantics, {})
    n_in, n_out = len(in_specs), len(out_specs)
    ci, co, cs = len(carry.operands), len(carry.out_shape), len(carry.scratch)
    steps = int(np.prod(grid))
    forward_step = min(int(steps * FORWARD_AT), steps - 1)

    def carrying(*refs):
        tabs, refs = refs[:n_tab], refs[n_tab:]
        main_in, c_in = refs[:n_in], refs[n_in:n_in + ci]
        main_out = refs[n_in + ci:n_in + ci + n_out]
        c_out = refs[n_in + ci + n_out:n_in + ci + n_out + co]
        rest = refs[n_in + ci + n_out + co:]
        main_scr, c_scr = rest[:len(rest) - cs], rest[len(rest) - cs:]
        step = 0
        for axis, extent in enumerate(grid):
            step = step * extent + pl.program_id(axis)

        @pl.when(step == 0)
        def _():
            carry.start(c_in, c_out, c_scr)

        body(*tabs, *main_in, *main_out, *main_scr)

        @pl.when(step == forward_step)
        def _():
            carry.forward(c_in, c_out, c_scr)

        @pl.when(step == steps - 1)
        def _():
            carry.finish(c_in, c_out, c_scr)

    aliases = {n_tab + n_in + i: n_out + o for i, o in carry.aliases.items()}
    res = run(carrying, in_specs + [HBM_SPEC] * ci, out_specs + [HBM_SPEC] * co, out_shape + carry.out_shape,
              scratch + carry.scratch, (*operands, *carry.operands), ["arbitrary"] * len(grid), aliases)
    return res[:n_out], res[n_out:]


def _fit(dim, want, unit=128):
    if dim <= want:
        return dim
    t = want - want % unit
    while dim % t:
        t -= unit
    return t


def _matmul(a, b, *, name, ta=False, tb=False, out_dtype=F32, tm=1024, tn=1024, tk=2048,
            addend=None, mode=None, aux=None, out_chunks=None, b_chunked=False, norm_gain=None, carry=None):
    m_dim, k_dim = (a.shape[1], a.shape[0]) if ta else a.shape
    b_rows, b_cols = (b.shape[1], b.shape[0] * b.shape[2]) if b_chunked else b.shape
    n_dim = b_rows if tb else b_cols
    assert (b_cols if tb else b_rows) == k_dim
    b_chunk = b.shape[2] if b_chunked else b_cols
    tm, tn, tk = _fit(m_dim, tm), _fit(n_dim // (out_chunks or 1), tn), _fit(k_dim, tk)
    k_chunks = 1
    if b_chunked and tb:
        assert tk % b_chunk == 0 and not ta
        k_chunks = tk // b_chunk
    elif b_chunked:
        tn = _fit(b_chunk, tn)
    assert m_dim % tm == 0 and n_dim % tn == 0 and k_dim % tk == 0, (name, a.shape, b.shape)
    nk = k_dim // tk
    a_spec = (pl.BlockSpec((tk, tm), lambda i, j, k: (k, i)) if ta
              else pl.BlockSpec((tm, tk), lambda i, j, k: (i, k)))
    if b_chunked and tb:
        b_spec = pl.BlockSpec((k_chunks, tn, b_chunk), lambda i, j, k: (k, j, 0))
    elif b_chunked:
        per_b = b_chunk // tn
        b_spec = pl.BlockSpec((None, tk, tn), lambda i, j, k: (j // per_b, k, j % per_b))
    else:
        b_spec = (pl.BlockSpec((tn, tk), lambda i, j, k: (j, k)) if tb
                  else pl.BlockSpec((tk, tn), lambda i, j, k: (k, j)))
    mn_spec = pl.BlockSpec((tm, tn), lambda i, j, k: (i, j))
    if out_chunks is None:
        o_spec, o_shape = mn_spec, (m_dim, n_dim)
    else:
        per = n_dim // out_chunks // tn
        assert per * tn * out_chunks == n_dim
        o_spec = pl.BlockSpec((None, tm, tn), lambda i, j, k: (j // per, i, j % per))
        o_shape = (out_chunks, m_dim, n_dim // out_chunks)
    dims = (((0 if ta else 1,), (1 if tb else 0,)), ((), ()))
    n_out = 2 if (mode == "relu_sq" or norm_gain is not None) else 1
    in_place = mode is None and out_dtype == F32
    in_specs, operands = [a_spec, b_spec], [a, b]
    if addend is not None:
        in_specs.append(mn_spec)
        operands.append(addend)
    if mode == "mul2":
        in_specs.append(mn_spec)
        operands.append(aux)
    if norm_gain is not None:
        assert in_place and tn == n_dim and out_chunks is None
        in_specs.append(pl.BlockSpec((1, tn), lambda i, j, k: (0, 0)))
        operands.append(norm_gain)

    def body(*refs):
        a_ref, b_ref = refs[0], refs[1]
        pos = 2
        add_ref = aux_ref = gain_ref = None
        if addend is not None:
            add_ref, pos = refs[pos], pos + 1
        if mode == "mul2":
            aux_ref, pos = refs[pos], pos + 1
        if norm_gain is not None:
            gain_ref, pos = refs[pos], pos + 1
        outs = refs[pos:pos + n_out]

        def normed(c):
            if gain_ref is not None:
                outs[1][...] = _norm_fwd_math(c, gain_ref[...]).astype(BF16)

        def finish(acc):
            if add_ref is not None:
                acc = acc + add_ref[...].astype(F32)
            if mode == "relu_sq":
                r = jnp.maximum(acc, 0.0)
                outs[0][...] = r.astype(outs[0].dtype)
                outs[1][...] = (r * r).astype(outs[1].dtype)
            elif mode == "mul2":
                outs[0][...] = (acc * (2.0 * aux_ref[...].astype(F32))).astype(outs[0].dtype)
            else:
                outs[0][...] = acc.astype(outs[0].dtype)
                if nk == 1:
                    normed(acc)

        if k_chunks > 1:
            part = functools.reduce(jnp.add, [
                lax.dot_general(a_ref[:, c * b_chunk:(c + 1) * b_chunk].astype(BF16), b_ref[c].astype(BF16), dims,
                                preferred_element_type=F32) for c in range(k_chunks)])
        else:
            b_tile = b_ref[0] if (b_chunked and tb) else b_ref[...]
            part = lax.dot_general(a_ref[...].astype(BF16), b_tile.astype(BF16), dims,
                                   preferred_element_type=F32)
        k = pl.program_id(2)
        if nk == 1:
            finish(part)
        elif in_place:
            @pl.when(k == 0)
            def _():
                finish(part)

            @pl.when(k > 0)
            def _():
                outs[0][...] += part

            if gain_ref is not None:
                @pl.when(k == nk - 1)
                def _():
                    normed(outs[0][...])
        else:
            acc_ref = refs[-1]

            @pl.when(k == 0)
            def _():
                acc_ref[...] = part

            @pl.when(k > 0)
            def _():
                acc_ref[...] += part

            @pl.when(k == nk - 1)
            def _():
                finish(acc_ref[...])

    out_shape = [jax.ShapeDtypeStruct(o_shape, out_dtype)] * n_out
    if norm_gain is not None:
        out_shape[1] = jax.ShapeDtypeStruct(o_shape, BF16)
    res = _call(
        body, name=name, grid=(m_dim // tm, n_dim // tn, nk),
        in_specs=in_specs, out_specs=[o_spec] * n_out, out_shape=out_shape, operands=operands,
        scratch=[pltpu.VMEM((tm, tn), F32)] if nk > 1 and not in_place else [],
        semantics=("parallel", "parallel", "arbitrary"), carry=carry)
    if carry is not None:
        res, carried = res
        return (res if n_out > 1 else res[0]), carried
    return res if n_out > 1 else res[0]


def _norm_fwd_math(x, g):
    r = lax.rsqrt(jnp.mean(x * x, axis=-1, keepdims=True) + NORM_EPS)
    return (x * r) * g


def _norm_bwd_math(x, g, dy):
    r = lax.rsqrt(jnp.mean(x * x, axis=-1, keepdims=True) + NORM_EPS)
    xh = x * r
    dxh = dy * g
    dx = r * (dxh - xh * jnp.mean(dxh * xh, axis=-1, keepdims=True))
    return dx, jnp.sum(dy * xh, axis=0, keepdims=True)


def _row_tile(rows, want=256):
    t = min(rows, want)
    assert rows % t == 0
    return t


def _rmsnorm_fwd(x, g, *, name, carry=None):
    s, d = x.shape
    tr = _row_tile(s)

    def body(x_ref, g_ref, h_ref):
        h_ref[...] = _norm_fwd_math(x_ref[...], g_ref[...]).astype(BF16)

    res = _call(
        body, name=name, grid=(s // tr,),
        in_specs=[pl.BlockSpec((tr, d), lambda i: (i, 0)), pl.BlockSpec((1, d), lambda i: (0, 0))],
        out_specs=[pl.BlockSpec((tr, d), lambda i: (i, 0))],
        out_shape=[jax.ShapeDtypeStruct((s, d), BF16)],
        operands=(x, g), semantics=("parallel",), carry=carry)
    return res[0] if carry is None else (res[0][0], res[1])


def _rmsnorm_bwd(x, g, dh, dres, *, name, carry=None):
    s, d = x.shape
    tr = _row_tile(s)

    def body(x_ref, g_ref, dh_ref, dres_ref, dx_ref, dx16_ref, dg_ref):
        dx, dg = _norm_bwd_math(x_ref[...], g_ref[...], dh_ref[...])
        dx = dres_ref[...] + dx
        dx_ref[...] = dx
        dx16_ref[...] = dx.astype(BF16)

        @pl.when(pl.program_id(0) == 0)
        def _():
            dg_ref[...] = dg

        @pl.when(pl.program_id(0) > 0)
        def _():
            dg_ref[...] += dg

    row = pl.BlockSpec((tr, d), lambda i: (i, 0))
    vec = pl.BlockSpec((1, d), lambda i: (0, 0))
    return _call(
        body, name=name, grid=(s // tr,),
        in_specs=[row, vec, row, row], out_specs=[row, row, vec],
        out_shape=[jax.ShapeDtypeStruct((s, d), F32), jax.ShapeDtypeStruct((s, d), BF16),
                   jax.ShapeDtypeStruct((1, d), F32)],
        operands=(x, g, dh, dres), semantics=("arbitrary",), carry=carry)


def _outnorm_fwd(ya, yb, ga, gb, *, name):
    s, da = ya.shape
    db = yb.shape[1]
    tr = _row_tile(s)

    def body(ya_ref, yb_ref, ga_ref, gb_ref, o_ref):
        o_ref[:, :da] = _norm_fwd_math(ya_ref[...], ga_ref[...]).astype(BF16)
        o_ref[:, da:] = _norm_fwd_math(yb_ref[...], gb_ref[...]).astype(BF16)

    return pl.pallas_call(
        body, name=name, grid=(s // tr,),
        in_specs=[pl.BlockSpec((tr, da), lambda i: (i, 0)), pl.BlockSpec((tr, db), lambda i: (i, 0)),
                  pl.BlockSpec((1, da), lambda i: (0, 0)), pl.BlockSpec((1, db), lambda i: (0, 0))],
        out_specs=pl.BlockSpec((tr, da + db), lambda i: (i, 0)),
        out_shape=jax.ShapeDtypeStruct((s, da + db), BF16),
        compiler_params=_params("parallel"),
    )(ya, yb, ga, gb)


def _outnorm_bwd(ya, yb, ga, gb, dmixed, *, name):
    s, da = ya.shape
    db = yb.shape[1]
    tr = _row_tile(s)

    def body(ya_ref, yb_ref, ga_ref, gb_ref, dm_ref, dya_ref, dyb_ref, dga_ref, dgb_ref):
        dxa, dga = _norm_bwd_math(ya_ref[...], ga_ref[...], dm_ref[:, :da])
        dxb, dgb = _norm_bwd_math(yb_ref[...], gb_ref[...], dm_ref[:, da:])
        dya_ref[...] = dxa
        dyb_ref[...] = dxb

        @pl.when(pl.program_id(0) == 0)
        def _():
            dga_ref[...] = dga
            dgb_ref[...] = dgb

        @pl.when(pl.program_id(0) > 0)
        def _():
            dga_ref[...] += dga
            dgb_ref[...] += dgb

    ra = pl.BlockSpec((tr, da), lambda i: (i, 0))
    rb = pl.BlockSpec((tr, db), lambda i: (i, 0))
    va = pl.BlockSpec((1, da), lambda i: (0, 0))
    vb = pl.BlockSpec((1, db), lambda i: (0, 0))
    return pl.pallas_call(
        body, name=name, grid=(s // tr,),
        in_specs=[ra, rb, va, vb, pl.BlockSpec((tr, da + db), lambda i: (i, 0))],
        out_specs=[ra, rb, va, vb],
        out_shape=[jax.ShapeDtypeStruct((s, da), F32), jax.ShapeDtypeStruct((s, db), F32),
                   jax.ShapeDtypeStruct((1, da), F32), jax.ShapeDtypeStruct((1, db), F32)],
        compiler_params=_params("arbitrary"),
    )(ya, yb, ga, gb, dmixed)


def _loss_head(x, g, target, *, name):
    s, d = x.shape
    tr = _row_tile(s)

    def body(x_ref, g_ref, t_ref, dx_ref, dx16_ref, loss_ref, dg_ref):
        xv, gv = x_ref[...], g_ref[...]
        err = _norm_fwd_math(xv, gv) - t_ref[...]
        part = 0.5 * jnp.sum(jnp.mean(err * err, axis=-1, keepdims=True), axis=0, keepdims=True)
        dx, dg = _norm_bwd_math(xv, gv, err * (1.0 / d))
        dx_ref[...] = dx
        dx16_ref[...] = dx.astype(BF16)
        part = jnp.broadcast_to(part, (8, 128))

        @pl.when(pl.program_id(0) == 0)
        def _():
            dg_ref[...] = dg
            loss_ref[...] = part

        @pl.when(pl.program_id(0) > 0)
        def _():
            dg_ref[...] += dg
            loss_ref[...] += part

    row = pl.BlockSpec((tr, d), lambda i: (i, 0))
    vec = pl.BlockSpec((1, d), lambda i: (0, 0))
    return pl.pallas_call(
        body, name=name, grid=(s // tr,),
        in_specs=[row, vec, row],
        out_specs=[row, row, pl.BlockSpec((8, 128), lambda i: (0, 0)), vec],
        out_shape=[jax.ShapeDtypeStruct((s, d), F32), jax.ShapeDtypeStruct((s, d), BF16),
                   jax.ShapeDtypeStruct((8, 128), F32), jax.ShapeDtypeStruct((1, d), F32)],
        compiler_params=_params("arbitrary"),
    )(x, g, target)


def _split3(x):
    hi = x.astype(BF16)
    rem = x - hi.astype(F32)
    mid = rem.astype(BF16)
    lo = (rem - mid.astype(F32)).astype(BF16)
    return hi, mid, lo


def _tri_sum(tri, x):
    hi, mid, lo = _split3(x)
    dot = functools.partial(jnp.dot, preferred_element_type=F32)
    return dot(tri, hi) + dot(tri, mid) + dot(tri, lo)


def _gates_fwd(f, fb, *, name):
    s, w = f.shape
    tb = 128
    nb = s // tb

    def body(f_ref, fb_ref, c_ref, carry):
        @pl.when(pl.program_id(0) == 0)
        def _():
            carry[...] = jnp.zeros_like(carry)

        logf = jax.nn.log_sigmoid(f_ref[...] + fb_ref[...])
        row = lax.broadcasted_iota(jnp.int32, (tb, tb), 0)
        col = lax.broadcasted_iota(jnp.int32, (tb, tb), 1)
        tri = (row >= col).astype(BF16)
        c = _tri_sum(tri, logf) + carry[...]
        c_ref[...] = c
        carry[...] = c[tb - 1:tb, :]

    return pl.pallas_call(
        body, name=name, grid=(nb,),
        in_specs=[pl.BlockSpec((tb, w), lambda i: (i, 0)), pl.BlockSpec((1, w), lambda i: (0, 0))],
        out_specs=pl.BlockSpec((tb, w), lambda i: (i, 0)),
        out_shape=jax.ShapeDtypeStruct((s, w), F32),
        scratch_shapes=[pltpu.VMEM((1, w), F32)],
        compiler_params=_params("arbitrary"),
    )(f, fb)


def _gates_bwd(f, fb, dc, *, name):
    s, w = f.shape
    tb = 128
    nb = s // tb

    def body(f_ref, fb_ref, dc_ref, df_ref, dfb_ref, carry):
        @pl.when(pl.program_id(0) == 0)
        def _():
            carry[...] = jnp.zeros_like(carry)
            dfb_ref[...] = jnp.zeros_like(dfb_ref)

        row = lax.broadcasted_iota(jnp.int32, (tb, tb), 0)
        col = lax.broadcasted_iota(jnp.int32, (tb, tb), 1)
        tri = (row <= col).astype(BF16)
        dlogf = _tri_sum(tri, dc_ref[...]) + carry[...]
        carry[...] = dlogf[0:1, :]
        df = dlogf * jax.nn.sigmoid(-(f_ref[...] + fb_ref[...]))
        df_ref[...] = df
        dfb_ref[...] += jnp.sum(df, axis=0, keepdims=True)

    rev = pl.BlockSpec((tb, w), lambda i: (nb - 1 - i, 0))
    vec = pl.BlockSpec((1, w), lambda i: (0, 0))
    return pl.pallas_call(
        body, name=name, grid=(nb,),
        in_specs=[rev, vec, rev], out_specs=[rev, vec],
        out_shape=[jax.ShapeDtypeStruct((s, w), F32), jax.ShapeDtypeStruct((1, w), F32)],
        scratch_shapes=[pltpu.VMEM((1, w), F32)],
        compiler_params=_params("arbitrary"),
    )(f, fb, dc)


def _nt(a, b):
    return lax.dot_general(a, b, (((1,), (1,)), ((), ())), preferred_element_type=F32)


def _tn(a, b):
    return lax.dot_general(a, b, (((0,), (0,)), ((), ())), preferred_element_type=F32)


def _nn(a, b):
    return jnp.dot(a, b, preferred_element_type=F32)


DIL_GROUP = 8
FOX_TILE = 1024
LOG2E = 1.4426950408889634


def _causal_pairs(nt, key_major):
    if key_major:
        pairs = [(q, k) for k in range(nt) for q in range(k, nt)]
    else:
        pairs = [(q, k) for q in range(nt) for k in range(q + 1)]
    return (jnp.asarray([p[0] for p in pairs], jnp.int32), jnp.asarray([p[1] for p in pairs], jnp.int32))


def _fox_fwd(qkv, c_row, ck_lanes, n_heads, *, name, carry=None):
    s = qkv.shape[0]
    e = HEAD_DIM
    t = min(FOX_TILE, s)
    nt = s // t
    scale2 = e ** -0.5 * LOG2E
    lanes = 128
    q_tab, k_tab = _causal_pairs(nt, key_major=False)

    def body(q_tab, k_tab, q_ref, k_ref, v_ref, cq_ref, ck_ref, o_ref, lse_ref,
             m_scr, l_scr, acc_scr, s_scr, p_scr):
        pair = pl.program_id(1)
        qi, ki = q_tab[pair], k_tab[pair]

        @pl.when(ki == 0)
        def _():
            m_scr[...] = jnp.full_like(m_scr, NEG_INF)
            l_scr[...] = jnp.zeros_like(l_scr)
            acc_scr[...] = jnp.zeros_like(acc_scr)

        def update(diagonal):
            s_scr[...] = _nt(k_ref[...], q_ref[...])
            ck2 = ck_ref[...]
            for c0 in range(0, t, lanes):
                cols = pl.ds(c0, lanes)

                def logits2():
                    x = s_scr[:, cols] * scale2 - ck2
                    if diagonal:
                        key = lax.broadcasted_iota(jnp.int32, x.shape, 0)
                        qry = c0 + lax.broadcasted_iota(jnp.int32, x.shape, 1)
                        x = jnp.where(key <= qry, x, NEG_INF)
                    return x

                m_old = m_scr[:, cols]
                m_new = jnp.maximum(m_old, jnp.max(logits2(), axis=0, keepdims=True))
                p = jnp.exp2(logits2() - m_new)
                alpha = jnp.exp2(m_old - m_new)
                l_scr[:, cols] = alpha * l_scr[:, cols] + jnp.sum(p, axis=0, keepdims=True)
                m_scr[:, cols] = m_new
                acc_scr[:, cols] = alpha * acc_scr[:, cols]
                p_scr[:, cols] = p.astype(BF16)
            acc_scr[...] += _tn(v_ref[...], p_scr[...])

        @pl.when(ki < qi)
        def _():
            update(False)

        @pl.when(ki == qi)
        def _():
            update(True)
            o_ref[...] = (acc_scr[...] / l_scr[...]).T
            lse_ref[...] = (m_scr[...] + jnp.log2(l_scr[...])) * (1.0 / LOG2E) + cq_ref[...]

    h_ = n_heads
    return _call(
        body, name=name, grid=(h_, int(q_tab.shape[0])), tables=(q_tab, k_tab),
        in_specs=[
            pl.BlockSpec((t, e), lambda h, p, qt, kt: (qt[p], h)),
            pl.BlockSpec((t, e), lambda h, p, qt, kt: (kt[p], h_ + h)),
            pl.BlockSpec((t, e), lambda h, p, qt, kt: (kt[p], 2 * h_ + h)),
            pl.BlockSpec((None, 1, t), lambda h, p, qt, kt: (h, 0, qt[p])),
            pl.BlockSpec((None, t, lanes), lambda h, p, qt, kt: (h, kt[p], 0)),
        ],
        out_specs=[pl.BlockSpec((t, e), lambda h, p, qt, kt: (qt[p], h)),
                   pl.BlockSpec((None, 1, t), lambda h, p, qt, kt: (h, 0, qt[p]))],
        out_shape=[jax.ShapeDtypeStruct((s, h_ * e), F32), jax.ShapeDtypeStruct((h_, 1, s), F32)],
        operands=(qkv, qkv, qkv, c_row, ck_lanes),
        scratch=[pltpu.VMEM((1, t), F32), pltpu.VMEM((1, t), F32), pltpu.VMEM((e, t), F32),
                 pltpu.VMEM((t, t), F32), pltpu.VMEM((t, t), BF16)],
        semantics=("parallel", "arbitrary"), carry=carry)


def _fox_bwd(qkv, c_row, ck_lanes, lse, y, dy, n_heads, *, name, carry=None):
    s = qkv.shape[0]
    e = HEAD_DIM
    t = min(FOX_TILE, s)
    nt = s // t
    scale = e ** -0.5
    scale2 = scale * LOG2E
    lanes = 128
    q_tab, k_tab = _causal_pairs(nt, key_major=False)
    n_pairs = int(q_tab.shape[0])

    def body(q_tab, k_tab, q_ref, k_ref, v_ref, c_ref, ck_ref, lse_ref, y_ref, dy_ref,
             dq_ref, dk_ref, dv_ref, dcq_ref, dck_ref,
             dq_scr, dk_scr, dv_scr, dck_scr, s_scr, dp_scr, p_scr, ds_scr, do_scr, delta_scr, shift_scr):
        pair = pl.program_id(1)
        qi, ki = q_tab[pair], k_tab[pair]

        @pl.when(pair == 0)
        def _():
            dk_scr[...] = jnp.zeros_like(dk_scr)
            dv_scr[...] = jnp.zeros_like(dv_scr)
            dck_scr[...] = jnp.zeros_like(dck_scr)

        @pl.when(ki == 0)
        def _():
            do = dy_ref[...]
            delta_scr[...] = lax.dot_general(jnp.ones((8, e), F32), do * y_ref[...], (((1,), (1,)), ((), ())),
                                             precision=lax.Precision.HIGHEST, preferred_element_type=F32)
            shift_scr[...] = (lse_ref[...] - c_ref[...]) * LOG2E
            do_scr[...] = do.astype(BF16)
            dq_scr[...] = jnp.zeros_like(dq_scr)
            dcq_ref[...] = jnp.zeros_like(dcq_ref)

        def update(diagonal):
            s_scr[...] = _nt(k_ref[...], q_ref[...])
            dp_scr[...] = _nt(v_ref[...], do_scr[...])
            ck2 = ck_ref[...]
            k_rows = pl.ds(pl.multiple_of(ki * t, t), t)
            for c0 in range(0, t, lanes):
                cols = pl.ds(c0, lanes)
                x = s_scr[:, cols] * scale2 - ck2
                if diagonal:
                    key = lax.broadcasted_iota(jnp.int32, x.shape, 0)
                    qry = c0 + lax.broadcasted_iota(jnp.int32, x.shape, 1)
                    x = jnp.where(key <= qry, x, NEG_INF)
                p = jnp.exp2(x - shift_scr[:, cols])
                ds = p * (dp_scr[:, cols] - delta_scr[0:1, cols])
                dcq_ref[:, cols] += jnp.sum(ds, axis=0, keepdims=True)
                dck_scr[k_rows, :] += ds
                p_scr[:, cols] = p.astype(BF16)
                ds_scr[:, cols] = ds.astype(BF16)
            dv_scr[k_rows, :] += _nn(p_scr[...], do_scr[...])
            dk_scr[k_rows, :] += scale * _nn(ds_scr[...], q_ref[...])
            dq_scr[...] += scale * _tn(k_ref[...], ds_scr[...])

        @pl.when(ki < qi)
        def _():
            update(False)

        @pl.when(ki == qi)
        def _():
            update(True)
            dq_ref[...] = dq_scr[...].T.astype(BF16)

        @pl.when(pair == n_pairs - 1)
        def _():
            dk_ref[...] = dk_scr[...].astype(BF16)
            dv_ref[...] = dv_scr[...].astype(BF16)
            dck_ref[...] = -lax.dot_general(jnp.ones((8, lanes), F32), dck_scr[...], (((1,), (1,)), ((), ())),
                                            precision=lax.Precision.HIGHEST, preferred_element_type=F32)[0:1]

    h_ = n_heads
    q_row = pl.BlockSpec((None, 1, t), lambda h, p, qt, kt: (h, 0, qt[p]))
    q_blk = pl.BlockSpec((t, e), lambda h, p, qt, kt: (qt[p], h))
    whole = pl.BlockSpec((s, e), lambda h, p, qt, kt: (0, h))
    return _call(
        body, name=name, grid=(h_, n_pairs), tables=(q_tab, k_tab),
        in_specs=[
            q_blk,
            pl.BlockSpec((t, e), lambda h, p, qt, kt: (kt[p], h_ + h)),
            pl.BlockSpec((t, e), lambda h, p, qt, kt: (kt[p], 2 * h_ + h)),
            q_row,
            pl.BlockSpec((None, t, lanes), lambda h, p, qt, kt: (h, kt[p], 0)),
            q_row, q_blk, q_blk,
        ],
        out_specs=[q_blk, whole, whole, q_row,
                   pl.BlockSpec((None, 1, s), lambda h, p, qt, kt: (h, 0, 0))],
        out_shape=[jax.ShapeDtypeStruct((s, h_ * e), BF16)] * 3
        + [jax.ShapeDtypeStruct((h_, 1, s), F32), jax.ShapeDtypeStruct((h_, 1, s), F32)],
        operands=(qkv, qkv, qkv, c_row, ck_lanes, lse, y, dy),
        scratch=[pltpu.VMEM((e, t), F32), pltpu.VMEM((s, e), F32), pltpu.VMEM((s, e), F32),
                 pltpu.VMEM((s, lanes), F32), pltpu.VMEM((t, t), F32), pltpu.VMEM((t, t), F32), pltpu.VMEM((t, t), BF16),
                 pltpu.VMEM((t, t), BF16), pltpu.VMEM((t, e), BF16), pltpu.VMEM((8, t), F32),
                 pltpu.VMEM((1, t), F32)],
        semantics=("parallel", "arbitrary"), carry=carry)


def _rel_bucket_table(dilation, span):
    dist = np.arange(span + 1, dtype=np.int64) * dilation
    max_exact = REL_BUCKETS // 2
    d = np.maximum(dist.astype(np.float32), np.float32(1.0))
    large = max_exact + (np.log(d / np.float32(max_exact)) / np.float32(math.log(REL_MAX_DISTANCE / max_exact))
                         * np.float32(REL_BUCKETS - max_exact)).astype(np.int32)
    large = np.minimum(large, REL_BUCKETS - 1)
    return np.where(dist < max_exact, dist, large)


def _bucket_bands(dilation, span):
    table = _rel_bucket_table(dilation, span)
    bands = []
    for n, b in enumerate(table):
        if bands and bands[-1][0] == int(b):
            bands[-1][2] = n
        else:
            assert not any(bb[0] == int(b) for bb in bands)
            bands.append([int(b), n, n])
    return [tuple(b) for b in bands]


def _steps_back():
    i = lax.broadcasted_iota(jnp.int32, (DIL_BLOCK, DIL_BLOCK), 0)
    j = lax.broadcasted_iota(jnp.int32, (DIL_BLOCK, DIL_BLOCK), 1)
    return i - j, DIL_BLOCK + i - j


def _bias_tiles(rb_ref, h, bands, span):
    n_cur, n_prev = _steps_back()
    t_cur = jnp.zeros((DIL_BLOCK, DIL_BLOCK), F32)
    t_prev = jnp.zeros((DIL_BLOCK, DIL_BLOCK), F32)
    for b, lo, hi in bands:
        val = rb_ref[b, h]
        t_cur = jnp.where((n_cur >= lo) & (n_cur <= hi), val, t_cur)
        t_prev = jnp.where((n_prev >= lo) & (n_prev <= hi), val, t_prev)
    t_cur = jnp.where(n_cur >= 0, t_cur, NEG_INF)
    t_prev = jnp.where(n_prev <= span, t_prev, NEG_INF)
    return t_cur, t_prev


def _dil_rows(start, dilation):
    return pl.ds(start, DIL_BLOCK, stride=dilation) if dilation > 1 else pl.ds(start, DIL_BLOCK)


def _dil_block_groups(s, dilation, run_group):
    group = dilation * DIL_BLOCK
    per_residue = s // group

    def block(r, n):
        if isinstance(n, int):
            return n * group + r, max(n - 1, 0) * group + r, n == 0
        return (pl.multiple_of(n * group, DIL_BLOCK) + r,
                pl.multiple_of(jnp.maximum(n - 1, 0) * group, DIL_BLOCK) + r, n == 0)

    if per_residue >= DIL_GROUP:
        assert per_residue % DIL_GROUP == 0
        for r in range(dilation):
            def trip(it, carry, r=r):
                run_group([block(r, it * DIL_GROUP + j) for j in range(DIL_GROUP)])
                return carry
            lax.fori_loop(0, per_residue // DIL_GROUP, trip, 0)
    else:
        residues = DIL_GROUP // per_residue
        assert residues * per_residue == DIL_GROUP and dilation % residues == 0
        for r0 in range(0, dilation, residues):
            run_group([block(r, n) for r in range(r0, r0 + residues) for n in range(per_residue)])


def _dil_fwd(qkv, rel_bias, n_heads, col0, *, name, carry=None):
    s = qkv.shape[0]
    e = HEAD_DIM
    scale = e ** -0.5
    n_pat = len(DIL_PATTERNS)
    for window, d in DIL_PATTERNS:
        assert window // d == DIL_BLOCK and s % (d * DIL_BLOCK) == 0
    bands = [_bucket_bands(d, w // d) for w, d in DIL_PATTERNS]

    def body(rb_ref, q_ref, k_ref, v_ref, y_ref, lse_ref, qf, kf, vf, *scr):
        o_scr, l_scr = scr[:n_pat], scr[n_pat:]
        h = pl.program_id(0)
        qf[...] = q_ref[...].astype(F32)
        kf[...] = k_ref[...].astype(F32)
        vf[...] = v_ref[...].astype(F32)
        for pi, (window, d) in enumerate(DIL_PATTERNS):
            t_cur, t_prev = _bias_tiles(rb_ref, h, bands[pi], window // d)

            def run_group(blocks, d=d, pi=pi, t_cur=t_cur, t_prev=t_prev):
                rows = [(_dil_rows(q0, d), _dil_rows(p0, d)) for q0, p0, _ in blocks]
                qb = [qf[cur, :].astype(BF16) for cur, _ in rows]
                s_c = [_nt(q, kf[cur, :].astype(BF16)) for q, (cur, _) in zip(qb, rows)]
                s_p = [_nt(q, kf[prev, :].astype(BF16)) for q, (_, prev) in zip(qb, rows)]
                s_c = [x * scale + t_cur for x in s_c]
                s_p = [x * scale + (t_prev + jnp.where(first, NEG_INF, 0.0)) for x, (_, _, first) in zip(s_p, blocks)]
                m = [jnp.max(jnp.maximum(a, b), axis=-1, keepdims=True) for a, b in zip(s_c, s_p)]
                p_c = [jnp.exp(a - mm) for a, mm in zip(s_c, m)]
                p_p = [jnp.exp(b - mm) for b, mm in zip(s_p, m)]
                l = [jnp.sum(a + b, axis=-1, keepdims=True) for a, b in zip(p_c, p_p)]
                o = [_nn(a.astype(BF16), vf[cur, :].astype(BF16)) + _nn(b.astype(BF16), vf[prev, :].astype(BF16))
                     for a, b, (cur, prev) in zip(p_c, p_p, rows)]
                for (cur, _), oo, ll, mm in zip(rows, o, l, m):
                    o_scr[pi][cur, :] = oo / ll
                    l_scr[pi][cur, :] = mm + jnp.log(ll)

            _dil_block_groups(s, d, run_group)
        lses = [l_scr[pi][...] for pi in range(n_pat)]
        m = functools.reduce(jnp.maximum, lses)
        ws = [jnp.exp(l - m) for l in lses]
        tot = functools.reduce(jnp.add, ws)
        y = functools.reduce(jnp.add, [w * o_scr[pi][...] for pi, w in enumerate(ws)])
        y_ref[...] = y / tot
        lse_ref[...] = m + jnp.log(tot)

    h_ = n_heads
    return _call(
        body, name=name, grid=(h_,),
        in_specs=[pl.BlockSpec(memory_space=pltpu.SMEM),
                  pl.BlockSpec((s, e), lambda h: (0, col0 + h)),
                  pl.BlockSpec((s, e), lambda h: (0, col0 + h_ + h)),
                  pl.BlockSpec((s, e), lambda h: (0, col0 + 2 * h_ + h))],
        out_specs=[pl.BlockSpec((s, e), lambda h: (0, h)), pl.BlockSpec((None, s, 1), lambda h: (h, 0, 0))],
        out_shape=[jax.ShapeDtypeStruct((s, h_ * e), F32), jax.ShapeDtypeStruct((h_, s, 1), F32)],
        operands=(rel_bias, qkv, qkv, qkv),
        scratch=[pltpu.VMEM((s, e), F32)] * 3 + [pltpu.VMEM((s, e), F32)] * n_pat
        + [pltpu.VMEM((s, 1), F32)] * n_pat,
        semantics=("parallel",), carry=carry)


def _dil_bwd(qkv, rel_bias, lse, y, dy, n_heads, col0, *, name, carry=None):
    s = qkv.shape[0]
    e = HEAD_DIM
    scale = e ** -0.5
    bands = [_bucket_bands(d, w // d) for w, d in DIL_PATTERNS]

    def body(rb_ref, q_ref, k_ref, v_ref, lse_ref, y_ref, dy_ref, dq_ref, dk_ref, dv_ref, drb_ref,
             qf, kf, vf, dqf, dkf, dvf, delta, dt_cur, dt_prev):
        h = pl.program_id(0)
        qf[...] = q_ref[...].astype(F32)
        kf[...] = k_ref[...].astype(F32)
        vf[...] = v_ref[...].astype(F32)
        dqf[...] = jnp.zeros_like(dqf)
        dkf[...] = jnp.zeros_like(dkf)
        dvf[...] = jnp.zeros_like(dvf)
        delta[...] = jnp.sum(dy_ref[...] * y_ref[...], axis=-1, keepdims=True)
        lane = lax.broadcasted_iota(jnp.int32, (1, 128), 1)
        drb = jnp.zeros((1, 128), F32)
        n_cur, n_prev = _steps_back()
        for pi, (window, d) in enumerate(DIL_PATTERNS):
            t_cur, t_prev = _bias_tiles(rb_ref, h, bands[pi], window // d)
            dt_cur[...] = jnp.zeros_like(dt_cur)
            dt_prev[...] = jnp.zeros_like(dt_prev)

            def run_group(blocks, d=d, t_cur=t_cur, t_prev=t_prev):
                rows = [(_dil_rows(q0, d), _dil_rows(p0, d)) for q0, p0, _ in blocks]
                qb = [qf[cur, :].astype(BF16) for cur, _ in rows]
                kc = [kf[cur, :].astype(BF16) for cur, _ in rows]
                kp = [kf[prev, :].astype(BF16) for _, prev in rows]
                vc = [vf[cur, :].astype(BF16) for cur, _ in rows]
                vp = [vf[prev, :].astype(BF16) for _, prev in rows]
                do = [dy_ref[cur, :].astype(BF16) for cur, _ in rows]
                lse_b = [lse_ref[cur, :] for cur, _ in rows]
                delta_b = [delta[cur, :] for cur, _ in rows]
                s_c = [_nt(q, k) for q, k in zip(qb, kc)]
                s_p = [_nt(q, k) for q, k in zip(qb, kp)]
                dp_c = [_nt(g, v) for g, v in zip(do, vc)]
                dp_p = [_nt(g, v) for g, v in zip(do, vp)]
                p_c = [jnp.exp(x * scale + t_cur - ls) for x, ls in zip(s_c, lse_b)]
                p_p = [jnp.exp(x * scale + (t_prev + jnp.where(first, NEG_INF, 0.0)) - ls)
                       for x, ls, (_, _, first) in zip(s_p, lse_b, blocks)]
                ds_c = [p * (g - dl) for p, g, dl in zip(p_c, dp_c, delta_b)]
                ds_p = [p * (g - dl) for p, g, dl in zip(p_p, dp_p, delta_b)]
                dt_cur[...] += functools.reduce(jnp.add, ds_c)
                dt_prev[...] += functools.reduce(jnp.add, ds_p)
                p_c, p_p = [p.astype(BF16) for p in p_c], [p.astype(BF16) for p in p_p]
                ds_c, ds_p = [x.astype(BF16) for x in ds_c], [x.astype(BF16) for x in ds_p]
                dq = [scale * (_nn(a, k1) + _nn(b, k2)) for a, b, k1, k2 in zip(ds_c, ds_p, kc, kp)]
                dk_c = [scale * _tn(a, q) for a, q in zip(ds_c, qb)]
                dv_c = [_tn(p, g) for p, g in zip(p_c, do)]
                dk_p = [scale * _tn(b, q) for b, q in zip(ds_p, qb)]
                dv_p = [_tn(p, g) for p, g in zip(p_p, do)]
                for i, (cur, prev) in enumerate(rows):
                    dqf[cur, :] += dq[i]
                    dkf[cur, :] += dk_c[i]
                    dvf[cur, :] += dv_c[i]
                    dkf[prev, :] += dk_p[i]
                    dvf[prev, :] += dv_p[i]

            _dil_block_groups(s, d, run_group)
            dtc, dtp = dt_cur[...], dt_prev[...]
            for b, lo, hi in bands[pi]:
                tot = (jnp.sum(jnp.where((n_cur >= lo) & (n_cur <= hi), dtc, 0.0))
                       + jnp.sum(jnp.where((n_prev >= lo) & (n_prev <= hi), dtp, 0.0)))
                drb = drb + jnp.where(lane == b, tot, 0.0)
        dq_ref[...] = dqf[...].astype(BF16)
        dk_ref[...] = dkf[...].astype(BF16)
        dv_ref[...] = dvf[...].astype(BF16)
        drb_ref[...] = drb

    h_ = n_heads
    col = pl.BlockSpec((s, e), lambda h: (0, h))
    return _call(
        body, name=name, grid=(h_,),
        in_specs=[pl.BlockSpec(memory_space=pltpu.SMEM),
                  pl.BlockSpec((s, e), lambda h: (0, col0 + h)),
                  pl.BlockSpec((s, e), lambda h: (0, col0 + h_ + h)),
                  pl.BlockSpec((s, e), lambda h: (0, col0 + 2 * h_ + h)),
                  pl.BlockSpec((None, s, 1), lambda h: (h, 0, 0)), col, col],
        out_specs=[col, col, col, pl.BlockSpec((None, 1, 128), lambda h: (h, 0, 0))],
        out_shape=[jax.ShapeDtypeStruct((s, h_ * e), BF16)] * 3 + [jax.ShapeDtypeStruct((h_, 1, 128), F32)],
        operands=(rel_bias, qkv, qkv, qkv, lse, y, dy),
        scratch=[pltpu.VMEM((s, e), F32)] * 6 + [pltpu.VMEM((s, 1), F32)]
        + [pltpu.VMEM((DIL_BLOCK, DIL_BLOCK), F32)] * 2,
        semantics=("parallel",), carry=carry)


def _place():
    x, y, c = lax.axis_index("x"), lax.axis_index("y"), lax.axis_index("c")
    chips = [(1 - x, y), (x, 1 - y), (1 - x, 1 - y)]
    return x, y, c, chips


def _block_index(px, py, pc):
    return 4 * px + 2 * py + pc


def _gather_task(blocks, *, row0=0, rows=None, into=None):
    n = len(blocks)
    part = blocks[0].shape[0]
    rows = rows or part

    def copies(ins, outs, sems):
        send_sems, recv_sems, local_sems = sems
        x, y, c, chips = _place()
        me, sibling = (x, y, c), (x, y, 1 - c)

        def place_of(a, block):
            slot = outs[a].at[_block_index(*block)]
            return slot if part == rows else slot.at[pl.ds(row0, part)]

        def copy(a, k, block, to, src=None):
            slot = place_of(a, block)
            return pltpu.make_async_remote_copy(
                src_ref=slot if src is None else src, dst_ref=slot,
                send_sem=send_sems.at[a, k], recv_sem=recv_sems.at[a, k],
                device_id=to, device_id_type=MESH)

        local = lambda a: pltpu.make_async_copy(ins[a], place_of(a, me), local_sems.at[a])
        return copy, local, me, sibling, c, chips

    def start(ins, outs, sems):
        copy, local, me, sibling, c, chips = copies(ins, outs, sems)
        for a in range(n):
            local(a).start()
            copy(a, 0, me, sibling, src=ins[a]).start()
            for j, chip in enumerate(chips):
                copy(a, 1 + j, me, (*chip, c), src=ins[a]).start()

    def forward(ins, outs, sems):
        copy, local, me, sibling, c, chips = copies(ins, outs, sems)
        for a in range(n):
            for j, chip in enumerate(chips):
                copy(a, 1 + j, (*chip, c), me).wait_recv()
                copy(a, 4 + j, (*chip, c), sibling).start()

    def finish(ins, outs, sems):
        copy, local, me, sibling, c, chips = copies(ins, outs, sems)
        for a in range(n):
            copy(a, 0, sibling, me).wait_recv()
            for j, chip in enumerate(chips):
                copy(a, 4 + j, (*chip, 1 - c), me).wait_recv()
        for a in range(n):
            for k in range(7):
                copy(a, k, me, sibling, src=ins[a]).wait_send()
            local(a).wait()

    return _Carried(
        list(blocks) + list(into or []),
        [jax.ShapeDtypeStruct((N_DEV, rows) + b.shape[1:], b.dtype) for b in blocks],
        [pltpu.SemaphoreType.DMA((n, 7)), pltpu.SemaphoreType.DMA((n, 7)), pltpu.SemaphoreType.DMA((n,))],
        start, forward, finish, aliases={n + a: a for a in range(n)} if into else None)


def _exchange_task(arrays, n_slots, route):
    n = len(arrays)

    def copies(ins, outs, sems):
        send_sems, recv_sems = sems
        out = []
        for a in range(n):
            for j in range(n_slots):
                src, to = route(j)
                out.append(pltpu.make_async_remote_copy(
                    src_ref=ins[a].at[src], dst_ref=outs[a].at[j],
                    send_sem=send_sems.at[a, j], recv_sem=recv_sems.at[a, j],
                    device_id=to, device_id_type=MESH))
        return out

    def start(ins, outs, sems):
        for cp in copies(ins, outs, sems):
            cp.start()

    def finish(ins, outs, sems):
        for cp in copies(ins, outs, sems):
            cp.wait()

    return _Carried(
        arrays, [jax.ShapeDtypeStruct((n_slots,) + g.shape[1:], g.dtype) for g in arrays],
        [pltpu.SemaphoreType.DMA((n, n_slots)), pltpu.SemaphoreType.DMA((n, n_slots))],
        start, lambda ins, outs, sems: None, finish)


def _sibling_task(grads):
    def route(q):
        x, y, c, _ = _place()
        return 2 * q + (1 - c), (x, y, 1 - c)
    return _exchange_task(grads, 4, route)


def _chips_task(parts):
    def route(j):
        x, y, c, chips = _place()
        px, py = chips[j]
        return 2 * px + py, (px, py, c)
    return _exchange_task(parts, 3, route)


def _chip_partials(grad, got, place, *, name):
    _, r, cdim = grad.shape
    tr = _row_tile(r, 512)

    def body(place_ref, g_ref, s_ref, o_ref):
        o_ref[...] = (g_ref[...] + s_ref[...]).astype(BF16)

    return pl.pallas_call(
        body, name=name,
        grid_spec=pltpu.PrefetchScalarGridSpec(
            num_scalar_prefetch=1, grid=(4, r // tr),
            in_specs=[pl.BlockSpec((None, tr, cdim), lambda q, i, pos: (2 * q + pos[2], i, 0)),
                      pl.BlockSpec((None, tr, cdim), lambda q, i, pos: (q, i, 0))],
            out_specs=pl.BlockSpec((None, tr, cdim), lambda q, i, pos: (q, i, 0))),
        out_shape=jax.ShapeDtypeStruct((4, r, cdim), BF16),
        compiler_params=_params("parallel", "parallel"),
    )(place, grad, got)


def _all_reduce_small(v, *, name):
    r, w = v.shape

    def body(v_ref, o_ref, buf, send_sems, recv_sems):
        x, y, c, _ = _place()
        me = _block_index(x, y, c)
        buf[me] = v_ref[...]
        copies = []
        for k in range(1, N_DEV):
            fx, fy, fc = (k >> 2) & 1, (k >> 1) & 1, k & 1
            peer = (x ^ fx, y ^ fy, c ^ fc)
            cp = pltpu.make_async_remote_copy(
                src_ref=v_ref, dst_ref=buf.at[me],
                send_sem=send_sems.at[k - 1], recv_sem=recv_sems.at[k - 1],
                device_id=peer, device_id_type=MESH)
            cp.start()
            copies.append(cp)
        for cp in copies:
            cp.wait()
        acc = buf[0]
        for p in range(1, N_DEV):
            acc = acc + buf[p]
        o_ref[...] = acc

    vmem = pl.BlockSpec(memory_space=pltpu.VMEM)
    return pl.pallas_call(
        body, name=name, in_specs=[vmem], out_specs=vmem,
        out_shape=jax.ShapeDtypeStruct((r, w), F32),
        scratch_shapes=[pltpu.VMEM((N_DEV, r, w), F32), pltpu.SemaphoreType.DMA((N_DEV - 1,)),
                        pltpu.SemaphoreType.DMA((N_DEV - 1,))],
    )(v)


def _adamw_math(w, g, m, v):
    m = ADAM_B1 * m + (1.0 - ADAM_B1) * g
    v = ADAM_B2 * v + (1.0 - ADAM_B2) * (g * g)
    m_hat = m / (1.0 - ADAM_B1 ** ADAM_STEP)
    v_hat = v / (1.0 - ADAM_B2 ** ADAM_STEP)
    delta = -ADAM_LR * (m_hat / (jnp.sqrt(v_hat) + ADAM_EPS) + ADAM_WD * w)
    return delta, m, v


def _adamw_sharded(w, m, v, part, got, place, *, name, carry=None):
    n_l, r, cdim = w.shape
    tr = _row_tile(r, 256)

    def body(place_ref, *refs):
        w_ref, m_ref, v_ref = refs[:3]
        p_refs, g_refs = refs[3:3 + n_l], refs[3 + n_l:3 + 2 * n_l]
        g_out, d_out, m_out, v_out = refs[3 + 2 * n_l:]
        for l in range(n_l):
            @pl.when(pl.program_id(0) == l)
            def _(l=l):
                g = p_refs[l][...].astype(F32)
                for j in range(3):
                    g = g + g_refs[l][j].astype(F32)
                delta, m_new, v_new = _adamw_math(w_ref[...], g, m_ref[...], v_ref[...])
                g_out[...] = g
                d_out[...] = delta
                m_out[...] = m_new
                v_out[...] = v_new

    shard = pl.BlockSpec((None, tr, cdim), lambda l, i, pos: (l, i, 0))
    p_specs = [pl.BlockSpec((None, tr, cdim),
                            lambda l, i, pos, li=li: (2 * pos[0] + pos[1], jnp.where(l == li, i, 0), 0))
               for li in range(n_l)]
    g_specs = [pl.BlockSpec((3, tr, cdim), lambda l, i, pos, li=li: (0, jnp.where(l == li, i, 0), 0))
               for li in range(n_l)]
    return _call(
        body, name=name, grid=(n_l, r // tr), tables=(place,),
        in_specs=[shard] * 3 + p_specs + g_specs, out_specs=[shard] * 4,
        out_shape=[jax.ShapeDtypeStruct(w.shape, F32)] * 4,
        operands=(w, m, v, *part, *got), semantics=("arbitrary", "parallel"), carry=carry)


def _adamw_small(w, g, m, v, *, name):
    def body(w_ref, g_ref, m_ref, v_ref, d_out, m_out, v_out):
        delta, m_new, v_new = _adamw_math(w_ref[...], g_ref[...], m_ref[...], v_ref[...])
        d_out[...] = delta
        m_out[...] = m_new
        v_out[...] = v_new

    vmem = pl.BlockSpec(memory_space=pltpu.VMEM)
    return pl.pallas_call(
        body, name=name, in_specs=[vmem] * 4, out_specs=[vmem] * 3,
        out_shape=[jax.ShapeDtypeStruct(w.shape, F32)] * 3,
    )(w, g, m, v)


def _pad_lanes(a, width=128):
    return jnp.pad(a, ((0, 0), (0, width - a.shape[1])))


def _train_step(x, target, small, shards, place):
    w_in, w_out, w_mlp_in, w_mlp_out = shards
    depth, d_model, in_shard = w_in.shape
    n_heads = small["forget_b"].shape[1]
    dh = n_heads * HEAD_DIM
    d_ff = w_mlp_in.shape[2] * N_DEV
    row = lambda a: a.reshape(1, -1)
    block = {"in": w_in.astype(BF16), "out": w_out.astype(BF16), "mi": w_mlp_in.astype(BF16),
             "mo": w_mlp_out.astype(BF16)}
    whole = {}

    gate_lo, gate_hi = 3 * dh, 3 * dh + n_heads

    def shard_columns(g, lo, hi):
        pieces = []
        while lo < hi:
            p, a = divmod(lo, in_shard)
            b = min(in_shard, a + hi - lo)
            pieces.append(g[p][:, a:b])
            lo += b - a
        return pieces

    def in_columns(d_qkv, d_gate, lo, hi):
        pieces = []
        if lo < gate_lo:
            pieces.append(d_qkv[:, lo:min(hi, gate_lo)])
        if lo < gate_hi and hi > gate_lo:
            pieces.append(d_gate[:, max(lo, gate_lo) - gate_lo:min(hi, gate_hi) - gate_lo])
        if hi > gate_hi:
            pieces.append(d_qkv[:, max(lo, gate_hi) - n_heads:hi - n_heads])
        return pieces

    def unpack(kind, l, g):
        if kind == "in":
            whole["qkv", l] = jnp.concatenate(
                shard_columns(g, 0, gate_lo) + shard_columns(g, gate_hi, N_DEV * in_shard), axis=1)
            whole["f", l] = _pad_lanes(jnp.concatenate(shard_columns(g, gate_lo, gate_hi), axis=1))
        elif kind == "out":
            whole["out", l] = g.reshape(2 * dh, d_model)
        elif kind == "mi":
            whole["mi", l] = g
        else:
            whole["mo", l] = g.reshape(d_ff, d_model)

    half_done = {}

    def gathering(kind, l, part):
        blk = block[kind][l]
        rows = blk.shape[0]
        if part == "all":
            return _gather_task([blk]), lambda got: unpack(kind, l, got[0])
        if part == "first":
            return (_gather_task([blk[:rows // 2]], rows=rows),
                    lambda got: half_done.__setitem__((kind, l), got[0]))
        return (_gather_task([blk[rows // 2:]], row0=rows // 2, rows=rows, into=[half_done.pop((kind, l))]),
                lambda got: unpack(kind, l, got[0]))

    def carried(call, *wanted):
        wanted = [gathering(*w) for w in wanted if w[1] < depth]
        if not wanted:
            return call()
        res, got = call(carry=_join([task for task, _ in wanted]))
        for (_, done), g in zip(wanted, got):
            done([g])
        return res

    saved = []
    h1 = carried(functools.partial(_rmsnorm_fwd, x, row(small["norm1_g"][0]), name="norm1_fwd_0"), ("in", 0, "all"))
    for l in range(depth):
        qkv = carried(functools.partial(_matmul, h1, whole["qkv", l], out_dtype=BF16, name=f"qkv_proj_{l}"),
                      ("mi", l, "first"))
        f = _matmul(h1, whole["f", l], name=f"gate_proj_{l}")
        fb = _pad_lanes(row(small["forget_b"][l]))
        c = _gates_fwd(f, fb, name=f"gates_fwd_{l}")
        c_heads = c[:, :n_heads].T
        cq = c_heads[:, None, :]
        ck = jnp.broadcast_to(c_heads[:, :, None] * LOG2E, c_heads.shape + (128,))
        ya, lse_a = carried(functools.partial(_fox_fwd, qkv, cq, ck, n_heads, name=f"fox_fwd_{l}"),
                            ("mi", l, "second"), ("out", l, "all"))
        yb, lse_b = carried(functools.partial(_dil_fwd, qkv, small["rel_bias"], n_heads, 3 * n_heads,
                                              name=f"dil_fwd_{l}"), ("mo", l, "first"))
        mixed = _outnorm_fwd(ya, yb, row(small["outnorm_a_g"][l]), row(small["outnorm_b_g"][l]),
                             name=f"outnorm_fwd_{l}")
        x1, h2 = _matmul(mixed, whole["out", l], addend=x, tm=512, tn=d_model, norm_gain=row(small["norm2_g"][l]),
                         name=f"out_proj_{l}")
        relu, act = carried(functools.partial(_matmul, h2, whole["mi", l], b_chunked=True, out_dtype=BF16,
                                              mode="relu_sq", name=f"mlp_in_{l}"),
                            ("mo", l, "second"), ("in", l + 1, "first"))
        mlp_out = functools.partial(_matmul, act, whole["mo", l], addend=x1, name=f"mlp_out_{l}")
        saved.append((x, h1, qkv, f, fb, cq, ck, ya, lse_a, yb, lse_b, mixed, x1, h2, relu, act))
        if l + 1 < depth:
            x, h1 = carried(functools.partial(mlp_out, tm=512, tn=d_model, norm_gain=row(small["norm1_g"][l + 1])),
                            ("in", l + 1, "second"))
        else:
            x = mlp_out()
    dx, dx16, loss_tile, d_final = _loss_head(x, row(small["final_norm_g"]), target, name="loss_head")

    parts = {k: [None] * depth for k in ("in", "out", "mi", "mo")}
    got = {k: [None] * depth for k in ("in", "out", "mi", "mo")}
    small_grads = [None] * depth

    def to_sibling(call, kind, l, grad):
        res, (from_sibling,) = call(carry=_sibling_task([grad]))
        parts[kind][l] = _chip_partials(grad, from_sibling, place, name=f"chip_partials_{kind}_{l}")
        return res

    def to_chips(call, kind, l):
        if l >= depth:
            return call()
        res, (got[kind][l],) = call(carry=_chips_task([parts[kind][l]]))
        return res

    for l in reversed(range(depth)):
        x0, h1, qkv, f, fb, cq, ck, ya, lse_a, yb, lse_b, mixed, x1, h2, relu, act = saved[l]
        du = to_chips(functools.partial(_matmul, dx16, whole["mo", l], tb=True, out_dtype=BF16, mode="mul2",
                                        aux=relu, name=f"mlp_out_dx_{l}"), "in", l + 1)
        dw_mo = _matmul(act, dx16, ta=True, name=f"mlp_out_dw_{l}")
        dw_mo = dw_mo.reshape(N_DEV, d_ff // N_DEV, d_model)
        dh2 = to_sibling(functools.partial(_matmul, du, whole["mi", l], tb=True, b_chunked=True,
                                           name=f"mlp_in_dx_{l}"),
                         "mo", l, dw_mo)
        dw_mi = to_chips(functools.partial(_matmul, h2, du, ta=True, out_chunks=N_DEV,
                                           name=f"mlp_in_dw_{l}"), "mo", l)
        dx1, dx1_16, d_norm2 = _rmsnorm_bwd(x1, row(small["norm2_g"][l]), dh2, dx, name=f"norm2_bwd_{l}")
        dmixed = to_sibling(functools.partial(_matmul, dx1_16, whole["out", l], tb=True, name=f"out_proj_dx_{l}"),
                            "mi", l, dw_mi)
        dw_out = _matmul(mixed, dx1_16, ta=True, name=f"out_proj_dw_{l}")
        dw_out = dw_out.reshape(N_DEV, 2 * dh // N_DEV, d_model)
        dya, dyb, d_ga, d_gb = _outnorm_bwd(ya, yb, row(small["outnorm_a_g"][l]), row(small["outnorm_b_g"][l]),
                                            dmixed, name=f"outnorm_bwd_{l}")
        dqa, dka, dva, dcq, dck = to_chips(functools.partial(
            _fox_bwd, qkv, cq, ck, lse_a, ya, dya, n_heads, name=f"fox_bwd_{l}"), "mi", l)
        dqb, dkb, dvb, d_rb = to_sibling(functools.partial(
            _dil_bwd, qkv, small["rel_bias"], lse_b, yb, dyb, n_heads, 3 * n_heads, name=f"dil_bwd_{l}"),
            "out", l, dw_out)
        dc = _pad_lanes((dcq[:, 0, :] + dck[:, 0, :]).T)
        df, dfb = _gates_bwd(f, fb, dc, name=f"gates_bwd_{l}")
        dqkv = jnp.concatenate([dqa, dka, dva, dqb, dkb, dvb], axis=1)
        dw_qkv = to_chips(functools.partial(_matmul, h1, dqkv, ta=True, name=f"qkv_proj_dw_{l}"), "out", l)
        dw_f = _matmul(h1, df, ta=True, name=f"gate_proj_dw_{l}")
        dw_in = jnp.stack([jnp.concatenate(in_columns(dw_qkv, dw_f, p * in_shard, (p + 1) * in_shard), axis=1)
                           for p in range(N_DEV)])
        dh1 = _matmul(df, whole["f", l], tb=True, name=f"gate_proj_dx_{l}")
        dh1 = to_sibling(functools.partial(_matmul, dqkv, whole["qkv", l], tb=True, addend=dh1,
                                           name=f"qkv_proj_dx_{l}"), "in", l, dw_in)
        dx, dx16, d_norm1 = _rmsnorm_bwd(x0, row(small["norm1_g"][l]), dh1, dx1, name=f"norm1_bwd_{l}")
        small_grads[l] = dict(norm1_g=d_norm1[0], forget_b=dfb[0, :n_heads], rel_bias=d_rb[:, 0, :REL_BUCKETS].T,
                              outnorm_a_g=d_ga[0], outnorm_b_g=d_gb[0], norm2_g=d_norm2[0])
    return loss_tile[0, 0], dx, d_final[0], small_grads, parts, got


def _pack_small(parts, rows):
    flat = jnp.concatenate([p.reshape(-1).astype(F32) for p in parts])
    return jnp.pad(flat, (0, rows * 128 - flat.shape[0])).reshape(rows, 128)


def _unpack_small(packed, shapes):
    flat = packed.reshape(-1)
    out, pos = [], 0
    for shp in shapes:
        size = int(np.prod(shp))
        out.append(flat[pos:pos + size].reshape(shp))
        pos += size
    return out


def kernel(x, norm1_g, w_in, forget_b, rel_bias, outnorm_a_g, outnorm_b_g, w_out, norm2_g, w_mlp_in, w_mlp_out, final_norm_g, loss_target, m_norm1_g, m_w_in, m_forget_b, m_rel_bias, m_outnorm_a_g, m_outnorm_b_g, m_w_out, m_norm2_g, m_w_mlp_in, m_w_mlp_out, m_final_norm_g, v_norm1_g, v_w_in, v_forget_b, v_rel_bias, v_outnorm_a_g, v_outnorm_b_g, v_w_out, v_norm2_g, v_w_mlp_in, v_w_mlp_out, v_final_norm_g):
    depth, d_model, in_shard = w_in.shape
    n_heads = forget_b.shape[1]
    assert in_shard * N_DEV == 6 * n_heads * HEAD_DIM + n_heads and x.shape[0] == 1
    place = jnp.stack([lax.axis_index("x"), lax.axis_index("y"), lax.axis_index("c")]).astype(jnp.int32)

    small_names = ["norm1_g", "forget_b", "rel_bias", "outnorm_a_g", "outnorm_b_g", "norm2_g", "final_norm_g"]
    small_w = dict(norm1_g=norm1_g, forget_b=forget_b, rel_bias=rel_bias, outnorm_a_g=outnorm_a_g,
                   outnorm_b_g=outnorm_b_g, norm2_g=norm2_g, final_norm_g=final_norm_g)
    loss_part, dx, d_final, grads, parts, got = _train_step(
        x[0], loss_target[0], small_w, (w_in, w_out, w_mlp_in, w_mlp_out), place)

    small_m = dict(norm1_g=m_norm1_g, forget_b=m_forget_b, rel_bias=m_rel_bias, outnorm_a_g=m_outnorm_a_g,
                   outnorm_b_g=m_outnorm_b_g, norm2_g=m_norm2_g, final_norm_g=m_final_norm_g)
    small_v = dict(norm1_g=v_norm1_g, forget_b=v_forget_b, rel_bias=v_rel_bias, outnorm_a_g=v_outnorm_a_g,
                   outnorm_b_g=v_outnorm_b_g, norm2_g=v_norm2_g, final_norm_g=v_final_norm_g)
    small_g = {k: jnp.stack([g[k] for g in grads]) for k in small_names if k not in ("rel_bias", "final_norm_g")}
    small_g["rel_bias"] = functools.reduce(jnp.add, [g["rel_bias"] for g in grads])
    small_g["final_norm_g"] = d_final
    shapes = [small_w[k].shape for k in small_names]
    total = sum(int(np.prod(s)) for s in shapes) + 1
    rows = -(-total // (8 * 128)) * 8
    packed_g = _pack_small([small_g[k] for k in small_names] + [loss_part], rows)
    packed_g = _all_reduce_small(packed_g, name="reduce_small")
    zero = jnp.zeros((1,), F32)
    packed_w = _pack_small([small_w[k] for k in small_names] + [zero], rows)
    packed_m = _pack_small([small_m[k] for k in small_names] + [zero], rows)
    packed_v = _pack_small([small_v[k] for k in small_names] + [zero + 1.0], rows)
    packed_d, packed_nm, packed_nv = _adamw_small(packed_w, packed_g, packed_m, packed_v, name="adamw_small")
    g_small = dict(zip(small_names, _unpack_small(packed_g, shapes)))
    d_small = dict(zip(small_names, _unpack_small(packed_d, shapes)))
    nm_small = dict(zip(small_names, _unpack_small(packed_nm, shapes)))
    nv_small = dict(zip(small_names, _unpack_small(packed_nv, shapes)))
    loss = packed_g.reshape(-1)[total - 1]

    big_out = {}
    last = parts["in"][0]
    last_halves = [last[:, :last.shape[1] // 2], last[:, last.shape[1] // 2:]]
    last_got = []
    for kind, nm, w, m, v in [("mi", "w_mlp_in", w_mlp_in, m_w_mlp_in, v_w_mlp_in),
                              ("mo", "w_mlp_out", w_mlp_out, m_w_mlp_out, v_w_mlp_out),
                              ("out", "w_out", w_out, m_w_out, v_w_out), ("in", "w_in", w_in, m_w_in, v_w_in)]:
        adamw = functools.partial(_adamw_sharded, w, m, v, parts[kind], got[kind], place, name=f"adamw_{nm}")
        if last_halves:
            big_out[nm], (half,) = adamw(carry=_chips_task([last_halves.pop(0)]))
            last_got.append(half)
            if not last_halves:
                got["in"][0] = jnp.concatenate(last_got, axis=1)
        else:
            big_out[nm] = adamw()

    order = ["norm1_g", "w_in", "forget_b", "rel_bias", "outnorm_a_g", "outnorm_b_g", "w_out", "norm2_g",
             "w_mlp_in", "w_mlp_out", "final_norm_g"]
    pick = lambda k, idx, small: big_out[k][idx] if k in big_out else small[k]
    outs = [loss, dx[None]]
    outs += [pick(k, 0, g_small) for k in order]
    outs += [pick(k, 1, d_small) for k in order]
    outs += [pick(k, 2, nm_small) for k in order]
    outs += [pick(k, 3, nv_small) for k in order]
    return tuple(outs)
```

```python
import functools
import math

import numpy as np
import jax
import jax.numpy as jnp
from jax import lax
from jax.experimental import pallas as pl
from jax.experimental.pallas import tpu as pltpu

F32 = jnp.float32
BF16 = jnp.bfloat16

HEAD_DIM = 128
DIL_PATTERNS = ((128, 1), (512, 4), (2048, 16))
DIL_BLOCK = 128
REL_BUCKETS = 32
REL_MAX_DISTANCE = 2048
NORM_EPS = 1e-6
NEG_INF = -1e30
ADAM_LR = 0.001
ADAM_B1 = 0.9
ADAM_B2 = 0.999
ADAM_EPS = 1e-08
ADAM_WD = 0.01
ADAM_STEP = 10

N_DEV = 8
V7X_VMEM_LIMIT_BYTES = 56 * 1024 * 1024
MESH = pl.DeviceIdType.MESH


def _params(*semantics):
    return pltpu.CompilerParams(dimension_semantics=semantics, vmem_limit_bytes=V7X_VMEM_LIMIT_BYTES)


HBM_SPEC = pl.BlockSpec(memory_space=pltpu.HBM)


class _Carried:
    def __init__(self, operands, out_shape, scratch, start, forward, finish, aliases=None):
        self.operands, self.out_shape, self.scratch = list(operands), list(out_shape), list(scratch)
        self.start, self.forward, self.finish = start, forward, finish
        self.aliases = dict(aliases or {})


def _join(tasks):
    if len(tasks) == 1:
        return tasks[0]
    spans, aliases = [], {}
    i0 = o0 = s0 = 0
    for t in tasks:
        i1, o1, s1 = i0 + len(t.operands), o0 + len(t.out_shape), s0 + len(t.scratch)
        spans.append((slice(i0, i1), slice(o0, o1), slice(s0, s1)))
        aliases.update({i0 + i: o0 + o for i, o in t.aliases.items()})
        i0, o0, s0 = i1, o1, s1

    def phase(which):
        def run(ins, outs, sems):
            for t, (si, so, ss) in zip(tasks, spans):
                getattr(t, which)(ins[si], outs[so], sems[ss])
        return run

    return _Carried(sum((t.operands for t in tasks), []), sum((t.out_shape for t in tasks), []),
                    sum((t.scratch for t in tasks), []), phase("start"), phase("forward"), phase("finish"),
                    aliases)


FORWARD_AT = 0.8


def _call(body, *, name, grid, in_specs, out_specs, out_shape, operands, scratch=(), semantics, carry=None,
          tables=()):
    in_specs, out_specs, out_shape, scratch = list(in_specs), list(out_specs), list(out_shape), list(scratch)
    n_tab = len(tables)

    def run(fn, in_specs, out_specs, out_shape, scratch, operands, semantics, aliases):
        return pl.pallas_call(
            fn, name=name,
            grid_spec=pltpu.PrefetchScalarGridSpec(
                num_scalar_prefetch=n_tab, grid=grid, in_specs=in_specs, out_specs=out_specs,
                scratch_shapes=scratch),
            out_shape=out_shape, input_output_aliases=aliases,
            compiler_params=_params(*semantics))(*tables, *operands)

    if carry is None:
        return run(body, in_specs, out_specs, out_shape, scratch, operands, semantics, {})
    n_in, n_out = len(in_specs), len(out_specs)
    ci, co, cs = len(carry.operands), len(carry.out_shape), len(carry.scratch)
    steps = int(np.prod(grid))
    forward_step = min(int(steps * FORWARD_AT), steps - 1)

    def carrying(*refs):
        tabs, refs = refs[:n_tab], refs[n_tab:]
        main_in, c_in = refs[:n_in], refs[n_in:n_in + ci]
        main_out = refs[n_in + ci:n_in + ci + n_out]
        c_out = refs[n_in + ci + n_out:n_in + ci + n_out + co]
        rest = refs[n_in + ci + n_out + co:]
        main_scr, c_scr = rest[:len(rest) - cs], rest[len(rest) - cs:]
        step = 0
        for axis, extent in enumerate(grid):
            step = step * extent + pl.program_id(axis)

        @pl.when(step == 0)
        def _():
            carry.start(c_in, c_out, c_scr)

        body(*tabs, *main_in, *main_out, *main_scr)

        @pl.when(step == forward_step)
        def _():
            carry.forward(c_in, c_out, c_scr)

        @pl.when(step == steps - 1)
        def _():
            carry.finish(c_in, c_out, c_scr)

    aliases = {n_tab + n_in + i: n_out + o for i, o in carry.aliases.items()}
    res = run(carrying, in_specs + [HBM_SPEC] * ci, out_specs + [HBM_SPEC] * co, out_shape + carry.out_shape,
              scratch + carry.scratch, (*operands, *carry.operands), ["arbitrary"] * len(grid), aliases)
    return res[:n_out], res[n_out:]


def _fit(dim, want, unit=128):
    if dim <= want:
        return dim
    t = want - want % unit
    while dim % t:
        t -= unit
    return t


def _matmul(a, b, *, name, ta=False, tb=False, out_dtype=F32, tm=1024, tn=1024, tk=2048,
            addend=None, mode=None, aux=None, out_chunks=None, b_chunked=False, norm_gain=None, carry=None):
    m_dim, k_dim = (a.shape[1], a.shape[0]) if ta else a.shape
    b_rows, b_cols = (b.shape[1], b.shape[0] * b.shape[2]) if b_chunked else b.shape
    n_dim = b_rows if tb else b_cols
    assert (b_cols if tb else b_rows) == k_dim
    b_chunk = b.shape[2] if b_chunked else b_cols
    tm, tn, tk = _fit(m_dim, tm), _fit(n_dim // (out_chunks or 1), tn), _fit(k_dim, tk)
    k_chunks = 1
    if b_chunked and tb:
        assert tk % b_chunk == 0 and not ta
        k_chunks = tk // b_chunk
    elif b_chunked:
        tn = _fit(b_chunk, tn)
    assert m_dim % tm == 0 and n_dim % tn == 0 and k_dim % tk == 0, (name, a.shape, b.shape)
    nk = k_dim // tk
    a_spec = (pl.BlockSpec((tk, tm), lambda i, j, k: (k, i)) if ta
              else pl.BlockSpec((tm, tk), lambda i, j, k: (i, k)))
    if b_chunked and tb:
        b_spec = pl.BlockSpec((k_chunks, tn, b_chunk), lambda i, j, k: (k, j, 0))
    elif b_chunked:
        per_b = b_chunk // tn
        b_spec = pl.BlockSpec((None, tk, tn), lambda i, j, k: (j // per_b, k, j % per_b))
    else:
        b_spec = (pl.BlockSpec((tn, tk), lambda i, j, k: (j, k)) if tb
                  else pl.BlockSpec((tk, tn), lambda i, j, k: (k, j)))
    mn_spec = pl.BlockSpec((tm, tn), lambda i, j, k: (i, j))
    if out_chunks is None:
        o_spec, o_shape = mn_spec, (m_dim, n_dim)
    else:
        per = n_dim // out_chunks // tn
        assert per * tn * out_chunks == n_dim
        o_spec = pl.BlockSpec((None, tm, tn), lambda i, j, k: (j // per, i, j % per))
        o_shape = (out_chunks, m_dim, n_dim // out_chunks)
    dims = (((0 if ta else 1,), (1 if tb else 0,)), ((), ()))
    n_out = 2 if (mode == "relu_sq" or norm_gain is not None) else 1
    in_place = mode is None and out_dtype == F32
    in_specs, operands = [a_spec, b_spec], [a, b]
    if addend is not None:
        in_specs.append(mn_spec)
        operands.append(addend)
    if mode == "mul2":
        in_specs.append(mn_spec)
        operands.append(aux)
    if norm_gain is not None:
        assert in_place and tn == n_dim and out_chunks is None
        in_specs.append(pl.BlockSpec((1, tn), lambda i, j, k: (0, 0)))
        operands.append(norm_gain)

    def body(*refs):
        a_ref, b_ref = refs[0], refs[1]
        pos = 2
        add_ref = aux_ref = gain_ref = None
        if addend is not None:
            add_ref, pos = refs[pos], pos + 1
        if mode == "mul2":
            aux_ref, pos = refs[pos], pos + 1
        if norm_gain is not None:
            gain_ref, pos = refs[pos], pos + 1
        outs = refs[pos:pos + n_out]

        def normed(c):
            if gain_ref is not None:
                outs[1][...] = _norm_fwd_math(c, gain_ref[...]).astype(BF16)

        def finish(acc):
            if add_ref is not None:
                acc = acc + add_ref[...].astype(F32)
            if mode == "relu_sq":
                r = jnp.maximum(acc, 0.0)
                outs[0][...] = r.astype(outs[0].dtype)
                outs[1][...] = (r * r).astype(outs[1].dtype)
            elif mode == "mul2":
                outs[0][...] = (acc * (2.0 * aux_ref[...].astype(F32))).astype(outs[0].dtype)
            else:
                outs[0][...] = acc.astype(outs[0].dtype)
                if nk == 1:
                    normed(acc)

        if k_chunks > 1:
            part = functools.reduce(jnp.add, [
                lax.dot_general(a_ref[:, c * b_chunk:(c + 1) * b_chunk].astype(BF16), b_ref[c].astype(BF16), dims,
                                preferred_element_type=F32) for c in range(k_chunks)])
        else:
            b_tile = b_ref[0] if (b_chunked and tb) else b_ref[...]
            part = lax.dot_general(a_ref[...].astype(BF16), b_tile.astype(BF16), dims,
                                   preferred_element_type=F32)
        k = pl.program_id(2)
        if nk == 1:
            finish(part)
        elif in_place:
            @pl.when(k == 0)
            def _():
                finish(part)

            @pl.when(k > 0)
            def _():
                outs[0][...] += part

            if gain_ref is not None:
                @pl.when(k == nk - 1)
                def _():
                    normed(outs[0][...])
        else:
            acc_ref = refs[-1]

            @pl.when(k == 0)
            def _():
                acc_ref[...] = part

            @pl.when(k > 0)
            def _():
                acc_ref[...] += part

            @pl.when(k == nk - 1)
            def _():
                finish(acc_ref[...])

    out_shape = [jax.ShapeDtypeStruct(o_shape, out_dtype)] * n_out
    if norm_gain is not None:
        out_shape[1] = jax.ShapeDtypeStruct(o_shape, BF16)
    res = _call(
        body, name=name, grid=(m_dim // tm, n_dim // tn, nk),
        in_specs=in_specs, out_specs=[o_spec] * n_out, out_shape=out_shape, operands=operands,
        scratch=[pltpu.VMEM((tm, tn), F32)] if nk > 1 and not in_place else [],
        semantics=("parallel", "parallel", "arbitrary"), carry=carry)
    if carry is not None:
        res, carried = res
        return (res if n_out > 1 else res[0]), carried
    return res if n_out > 1 else res[0]


def _norm_fwd_math(x, g):
    r = lax.rsqrt(jnp.mean(x * x, axis=-1, keepdims=True) + NORM_EPS)
    return (x * r) * g


def _norm_bwd_math(x, g, dy):
    r = lax.rsqrt(jnp.mean(x * x, axis=-1, keepdims=True) + NORM_EPS)
    xh = x * r
    dxh = dy * g
    dx = r * (dxh - xh * jnp.mean(dxh * xh, axis=-1, keepdims=True))
    return dx, jnp.sum(dy * xh, axis=0, keepdims=True)


def _row_tile(rows, want=256):
    t = min(rows, want)
    assert rows % t == 0
    return t


def _rmsnorm_fwd(x, g, *, name, carry=None):
    s, d = x.shape
    tr = _row_tile(s)

    def body(x_ref, g_ref, h_ref):
        h_ref[...] = _norm_fwd_math(x_ref[...], g_ref[...]).astype(BF16)

    res = _call(
        body, name=name, grid=(s // tr,),
        in_specs=[pl.BlockSpec((tr, d), lambda i: (i, 0)), pl.BlockSpec((1, d), lambda i: (0, 0))],
        out_specs=[pl.BlockSpec((tr, d), lambda i: (i, 0))],
        out_shape=[jax.ShapeDtypeStruct((s, d), BF16)],
        operands=(x, g), semantics=("parallel",), carry=carry)
    return res[0] if carry is None else (res[0][0], res[1])


def _rmsnorm_bwd(x, g, dh, dres, *, name, carry=None):
    s, d = x.shape
    tr = _row_tile(s)

    def body(x_ref, g_ref, dh_ref, dres_ref, dx_ref, dx16_ref, dg_ref):
        dx, dg = _norm_bwd_math(x_ref[...], g_ref[...], dh_ref[...])
        dx = dres_ref[...] + dx
        dx_ref[...] = dx
        dx16_ref[...] = dx.astype(BF16)

        @pl.when(pl.program_id(0) == 0)
        def _():
            dg_ref[...] = dg

        @pl.when(pl.program_id(0) > 0)
        def _():
            dg_ref[...] += dg

    row = pl.BlockSpec((tr, d), lambda i: (i, 0))
    vec = pl.BlockSpec((1, d), lambda i: (0, 0))
    return _call(
        body, name=name, grid=(s // tr,),
        in_specs=[row, vec, row, row], out_specs=[row, row, vec],
        out_shape=[jax.ShapeDtypeStruct((s, d), F32), jax.ShapeDtypeStruct((s, d), BF16),
                   jax.ShapeDtypeStruct((1, d), F32)],
        operands=(x, g, dh, dres), semantics=("arbitrary",), carry=carry)


def _outnorm_fwd(ya, yb, ga, gb, *, name):
    s, da = ya.shape
    db = yb.shape[1]
    tr = _row_tile(s)

    def body(ya_ref, yb_ref, ga_ref, gb_ref, o_ref):
        o_ref[:, :da] = _norm_fwd_math(ya_ref[...], ga_ref[...]).astype(BF16)
        o_ref[:, da:] = _norm_fwd_math(yb_ref[...], gb_ref[...]).astype(BF16)

    return pl.pallas_call(
        body, name=name, grid=(s // tr,),
        in_specs=[pl.BlockSpec((tr, da), lambda i: (i, 0)), pl.BlockSpec((tr, db), lambda i: (i, 0)),
                  pl.BlockSpec((1, da), lambda i: (0, 0)), pl.BlockSpec((1, db), lambda i: (0, 0))],
        out_specs=pl.BlockSpec((tr, da + db), lambda i: (i, 0)),
        out_shape=jax.ShapeDtypeStruct((s, da + db), BF16),
        compiler_params=_params("parallel"),
    )(ya, yb, ga, gb)


def _outnorm_bwd(ya, yb, ga, gb, dmixed, *, name):
    s, da = ya.shape
    db = yb.shape[1]
    tr = _row_tile(s)

    def body(ya_ref, yb_ref, ga_ref, gb_ref, dm_ref, dya_ref, dyb_ref, dga_ref, dgb_ref):
        dxa, dga = _norm_bwd_math(ya_ref[...], ga_ref[...], dm_ref[:, :da])
        dxb, dgb = _norm_bwd_math(yb_ref[...], gb_ref[...], dm_ref[:, da:])
        dya_ref[...] = dxa
        dyb_ref[...] = dxb

        @pl.when(pl.program_id(0) == 0)
        def _():
            dga_ref[...] = dga
            dgb_ref[...] = dgb

        @pl.when(pl.program_id(0) > 0)
        def _():
            dga_ref[...] += dga
            dgb_ref[...] += dgb

    ra = pl.BlockSpec((tr, da), lambda i: (i, 0))
    rb = pl.BlockSpec((tr, db), lambda i: (i, 0))
    va = pl.BlockSpec((1, da), lambda i: (0, 0))
    vb = pl.BlockSpec((1, db), lambda i: (0, 0))
    return pl.pallas_call(
        body, name=name, grid=(s // tr,),
        in_specs=[ra, rb, va, vb, pl.BlockSpec((tr, da + db), lambda i: (i, 0))],
        out_specs=[ra, rb, va, vb],
        out_shape=[jax.ShapeDtypeStruct((s, da), F32), jax.ShapeDtypeStruct((s, db), F32),
                   jax.ShapeDtypeStruct((1, da), F32), jax.ShapeDtypeStruct((1, db), F32)],
        compiler_params=_params("arbitrary"),
    )(ya, yb, ga, gb, dmixed)


def _loss_head(x, g, target, *, name):
    s, d = x.shape
    tr = _row_tile(s)

    def body(x_ref, g_ref, t_ref, dx_ref, dx16_ref, loss_ref, dg_ref):
        xv, gv = x_ref[...], g_ref[...]
        err = _norm_fwd_math(xv, gv) - t_ref[...]
        part = 0.5 * jnp.sum(jnp.mean(err * err, axis=-1, keepdims=True), axis=0, keepdims=True)
        dx, dg = _norm_bwd_math(xv, gv, err * (1.0 / d))
        dx_ref[...] = dx
        dx16_ref[...] = dx.astype(BF16)
        part = jnp.broadcast_to(part, (8, 128))

        @pl.when(pl.program_id(0) == 0)
        def _():
            dg_ref[...] = dg
            loss_ref[...] = part

        @pl.when(pl.program_id(0) > 0)
        def _():
            dg_ref[...] += dg
            loss_ref[...] += part

    row = pl.BlockSpec((tr, d), lambda i: (i, 0))
    vec = pl.BlockSpec((1, d), lambda i: (0, 0))
    return pl.pallas_call(
        body, name=name, grid=(s // tr,),
        in_specs=[row, vec, row],
        out_specs=[row, row, pl.BlockSpec((8, 128), lambda i: (0, 0)), vec],
        out_shape=[jax.ShapeDtypeStruct((s, d), F32), jax.ShapeDtypeStruct((s, d), BF16),
                   jax.ShapeDtypeStruct((8, 128), F32), jax.ShapeDtypeStruct((1, d), F32)],
        compiler_params=_params("arbitrary"),
    )(x, g, target)


def _split3(x):
    hi = x.astype(BF16)
    rem = x - hi.astype(F32)
    mid = rem.astype(BF16)
    lo = (rem - mid.astype(F32)).astype(BF16)
    return hi, mid, lo


def _tri_sum(tri, x):
    hi, mid, lo = _split3(x)
    dot = functools.partial(jnp.dot, preferred_element_type=F32)
    return dot(tri, hi) + dot(tri, mid) + dot(tri, lo)


def _gates_fwd(f, fb, *, name):
    s, w = f.shape
    tb = 128
    nb = s // tb

    def body(f_ref, fb_ref, c_ref, carry):
        @pl.when(pl.program_id(0) == 0)
        def _():
            carry[...] = jnp.zeros_like(carry)

        logf = jax.nn.log_sigmoid(f_ref[...] + fb_ref[...])
        row = lax.broadcasted_iota(jnp.int32, (tb, tb), 0)
        col = lax.broadcasted_iota(jnp.int32, (tb, tb), 1)
        tri = (row >= col).astype(BF16)
        c = _tri_sum(tri, logf) + carry[...]
        c_ref[...] = c
        carry[...] = c[tb - 1:tb, :]

    return pl.pallas_call(
        body, name=name, grid=(nb,),
        in_specs=[pl.BlockSpec((tb, w), lambda i: (i, 0)), pl.BlockSpec((1, w), lambda i: (0, 0))],
        out_specs=pl.BlockSpec((tb, w), lambda i: (i, 0)),
        out_shape=jax.ShapeDtypeStruct((s, w), F32),
        scratch_shapes=[pltpu.VMEM((1, w), F32)],
        compiler_params=_params("arbitrary"),
    )(f, fb)


def _gates_bwd(f, fb, dc, *, name):
    s, w = f.shape
    tb = 128
    nb = s // tb

    def body(f_ref, fb_ref, dc_ref, df_ref, dfb_ref, carry):
        @pl.when(pl.program_id(0) == 0)
        def _():
            carry[...] = jnp.zeros_like(carry)
            dfb_ref[...] = jnp.zeros_like(dfb_ref)

        row = lax.broadcasted_iota(jnp.int32, (tb, tb), 0)
        col = lax.broadcasted_iota(jnp.int32, (tb, tb), 1)
        tri = (row <= col).astype(BF16)
        dlogf = _tri_sum(tri, dc_ref[...]) + carry[...]
        carry[...] = dlogf[0:1, :]
        df = dlogf * jax.nn.sigmoid(-(f_ref[...] + fb_ref[...]))
        df_ref[...] = df
        dfb_ref[...] += jnp.sum(df, axis=0, keepdims=True)

    rev = pl.BlockSpec((tb, w), lambda i: (nb - 1 - i, 0))
    vec = pl.BlockSpec((1, w), lambda i: (0, 0))
    return pl.pallas_call(
        body, name=name, grid=(nb,),
        in_specs=[rev, vec, rev], out_specs=[rev, vec],
        out_shape=[jax.ShapeDtypeStruct((s, w), F32), jax.ShapeDtypeStruct((1, w), F32)],
        scratch_shapes=[pltpu.VMEM((1, w), F32)],
        compiler_params=_params("arbitrary"),
    )(f, fb, dc)


def _nt(a, b):
    return lax.dot_general(a, b, (((1,), (1,)), ((), ())), preferred_element_type=F32)


def _tn(a, b):
    return lax.dot_general(a, b, (((0,), (0,)), ((), ())), preferred_element_type=F32)


def _nn(a, b):
    return jnp.dot(a, b, preferred_element_type=F32)


DIL_GROUP = 8
FOX_TILE = 1024
LOG2E = 1.4426950408889634


def _causal_pairs(nt, key_major):
    if key_major:
        pairs = [(q, k) for k in range(nt) for q in range(k, nt)]
    else:
        pairs = [(q, k) for q in range(nt) for k in range(q + 1)]
    return (jnp.asarray([p[0] for p in pairs], jnp.int32), jnp.asarray([p[1] for p in pairs], jnp.int32))


def _fox_fwd(qkv, c_row, ck_lanes, n_heads, *, name, carry=None):
    s = qkv.shape[0]
    e = HEAD_DIM
    t = min(FOX_TILE, s)
    nt = s // t
    scale2 = e ** -0.5 * LOG2E
    lanes = 128
    q_tab, k_tab = _causal_pairs(nt, key_major=False)

    def body(q_tab, k_tab, q_ref, k_ref, v_ref, cq_ref, ck_ref, o_ref, lse_ref,
             m_scr, l_scr, acc_scr, s_scr, p_scr):
        pair = pl.program_id(1)
        qi, ki = q_tab[pair], k_tab[pair]

        @pl.when(ki == 0)
        def _():
            m_scr[...] = jnp.full_like(m_scr, NEG_INF)
            l_scr[...] = jnp.zeros_like(l_scr)
            acc_scr[...] = jnp.zeros_like(acc_scr)

        def update(diagonal):
            s_scr[...] = _nt(k_ref[...], q_ref[...])
            ck2 = ck_ref[...]
            for c0 in range(0, t, lanes):
                cols = pl.ds(c0, lanes)

                def logits2():
                    x = s_scr[:, cols] * scale2 - ck2
                    if diagonal:
                        key = lax.broadcasted_iota(jnp.int32, x.shape, 0)
                        qry = c0 + lax.broadcasted_iota(jnp.int32, x.shape, 1)
                        x = jnp.where(key <= qry, x, NEG_INF)
                    return x

                m_old = m_scr[:, cols]
                m_new = jnp.maximum(m_old, jnp.max(logits2(), axis=0, keepdims=True))
                p = jnp.exp2(logits2() - m_new)
                alpha = jnp.exp2(m_old - m_new)
                l_scr[:, cols] = alpha * l_scr[:, cols] + jnp.sum(p, axis=0, keepdims=True)
                m_scr[:, cols] = m_new
                acc_scr[:, cols] = alpha * acc_scr[:, cols]
                p_scr[:, cols] = p.astype(BF16)
            acc_scr[...] += _tn(v_ref[...], p_scr[...])

        @pl.when(ki < qi)
        def _():
            update(False)

        @pl.when(ki == qi)
        def _():
            update(True)
            o_ref[...] = (acc_scr[...] / l_scr[...]).T
            lse_ref[...] = (m_scr[...] + jnp.log2(l_scr[...])) * (1.0 / LOG2E) + cq_ref[...]

    h_ = n_heads
    return _call(
        body, name=name, grid=(h_, int(q_tab.shape[0])), tables=(q_tab, k_tab),
        in_specs=[
            pl.BlockSpec((t, e), lambda h, p, qt, kt: (qt[p], h)),
            pl.BlockSpec((t, e), lambda h, p, qt, kt: (kt[p], h_ + h)),
            pl.BlockSpec((t, e), lambda h, p, qt, kt: (kt[p], 2 * h_ + h)),
            pl.BlockSpec((None, 1, t), lambda h, p, qt, kt: (h, 0, qt[p])),
            pl.BlockSpec((None, t, lanes), lambda h, p, qt, kt: (h, kt[p], 0)),
        ],
        out_specs=[pl.BlockSpec((t, e), lambda h, p, qt, kt: (qt[p], h)),
                   pl.BlockSpec((None, 1, t), lambda h, p, qt, kt: (h, 0, qt[p]))],
        out_shape=[jax.ShapeDtypeStruct((s, h_ * e), F32), jax.ShapeDtypeStruct((h_, 1, s), F32)],
        operands=(qkv, qkv, qkv, c_row, ck_lanes),
        scratch=[pltpu.VMEM((1, t), F32), pltpu.VMEM((1, t), F32), pltpu.VMEM((e, t), F32),
                 pltpu.VMEM((t, t), F32), pltpu.VMEM((t, t), BF16)],
        semantics=("parallel", "arbitrary"), carry=carry)


def _fox_bwd(qkv, c_row, ck_lanes, lse, y, dy, n_heads, *, name, carry=None):
    s = qkv.shape[0]
    e = HEAD_DIM
    t = min(FOX_TILE, s)
    nt = s // t
    scale = e ** -0.5
    scale2 = scale * LOG2E
    lanes = 128
    q_tab, k_tab = _causal_pairs(nt, key_major=False)
    n_pairs = int(q_tab.shape[0])

    def body(q_tab, k_tab, q_ref, k_ref, v_ref, c_ref, ck_ref, lse_ref, y_ref, dy_ref,
             dq_ref, dk_ref, dv_ref, dcq_ref, dck_ref,
             dq_scr, dk_scr, dv_scr, dck_scr, s_scr, dp_scr, p_scr, ds_scr, do_scr, delta_scr, shift_scr):
        pair = pl.program_id(1)
        qi, ki = q_tab[pair], k_tab[pair]

        @pl.when(pair == 0)
        def _():
            dk_scr[...] = jnp.zeros_like(dk_scr)
            dv_scr[...] = jnp.zeros_like(dv_scr)
            dck_scr[...] = jnp.zeros_like(dck_scr)

        @pl.when(ki == 0)
        def _():
            do = dy_ref[...]
            delta_scr[...] = lax.dot_general(jnp.ones((8, e), F32), do * y_ref[...], (((1,), (1,)), ((), ())),
                                             precision=lax.Precision.HIGHEST, preferred_element_type=F32)
            shift_scr[...] = (lse_ref[...] - c_ref[...]) * LOG2E
            do_scr[...] = do.astype(BF16)
            dq_scr[...] = jnp.zeros_like(dq_scr)
            dcq_ref[...] = jnp.zeros_like(dcq_ref)

        def update(diagonal):
            s_scr[...] = _nt(k_ref[...], q_ref[...])
            dp_scr[...] = _nt(v_ref[...], do_scr[...])
            ck2 = ck_ref[...]
            k_rows = pl.ds(pl.multiple_of(ki * t, t), t)
            for c0 in range(0, t, lanes):
                cols = pl.ds(c0, lanes)
                x = s_scr[:, cols] * scale2 - ck2
                if diagonal:
                    key = lax.broadcasted_iota(jnp.int32, x.shape, 0)
                    qry = c0 + lax.broadcasted_iota(jnp.int32, x.shape, 1)
                    x = jnp.where(key <= qry, x, NEG_INF)
                p = jnp.exp2(x - shift_scr[:, cols])
                ds = p * (dp_scr[:, cols] - delta_scr[0:1, cols])
                dcq_ref[:, cols] += jnp.sum(ds, axis=0, keepdims=True)
                dck_scr[k_rows, :] += ds
                p_scr[:, cols] = p.astype(BF16)
                ds_scr[:, cols] = ds.astype(BF16)
            dv_scr[k_rows, :] += _nn(p_scr[...], do_scr[...])
            dk_scr[k_rows, :] += scale * _nn(ds_scr[...], q_ref[...])
            dq_scr[...] += scale * _tn(k_ref[...], ds_scr[...])

        @pl.when(ki < qi)
        def _():
            update(False)

        @pl.when(ki == qi)
        def _():
            update(True)
            dq_ref[...] = dq_scr[...].T.astype(BF16)

        @pl.when(pair == n_pairs - 1)
        def _():
            dk_ref[...] = dk_scr[...].astype(BF16)
            dv_ref[...] = dv_scr[...].astype(BF16)
            dck_ref[...] = -lax.dot_general(jnp.ones((8, lanes), F32), dck_scr[...], (((1,), (1,)), ((), ())),
                                            precision=lax.Precision.HIGHEST, preferred_element_type=F32)[0:1]

    h_ = n_heads
    q_row = pl.BlockSpec((None, 1, t), lambda h, p, qt, kt: (h, 0, qt[p]))
    q_blk = pl.BlockSpec((t, e), lambda h, p, qt, kt: (qt[p], h))
    whole = pl.BlockSpec((s, e), lambda h, p, qt, kt: (0, h))
    return _call(
        body, name=name, grid=(h_, n_pairs), tables=(q_tab, k_tab),
        in_specs=[
            q_blk,
            pl.BlockSpec((t, e), lambda h, p, qt, kt: (kt[p], h_ + h)),
            pl.BlockSpec((t, e), lambda h, p, qt, kt: (kt[p], 2 * h_ + h)),
            q_row,
            pl.BlockSpec((None, t, lanes), lambda h, p, qt, kt: (h, kt[p], 0)),
            q_row, q_blk, q_blk,
        ],
        out_specs=[q_blk, whole, whole, q_row,
                   pl.BlockSpec((None, 1, s), lambda h, p, qt, kt: (h, 0, 0))],
        out_shape=[jax.ShapeDtypeStruct((s, h_ * e), BF16)] * 3
        + [jax.ShapeDtypeStruct((h_, 1, s), F32), jax.ShapeDtypeStruct((h_, 1, s), F32)],
        operands=(qkv, qkv, qkv, c_row, ck_lanes, lse, y, dy),
        scratch=[pltpu.VMEM((e, t), F32), pltpu.VMEM((s, e), F32), pltpu.VMEM((s, e), F32),
                 pltpu.VMEM((s, lanes), F32), pltpu.VMEM((t, t), F32), pltpu.VMEM((t, t), F32), pltpu.VMEM((t, t), BF16),
                 pltpu.VMEM((t, t), BF16), pltpu.VMEM((t, e), BF16), pltpu.VMEM((8, t), F32),
                 pltpu.VMEM((1, t), F32)],
        semantics=("parallel", "arbitrary"), carry=carry)


def _rel_bucket_table(dilation, span):
    dist = np.arange(span + 1, dtype=np.int64) * dilation
    max_exact = REL_BUCKETS // 2
    d = np.maximum(dist.astype(np.float32), np.float32(1.0))
    large = max_exact + (np.log(d / np.float32(max_exact)) / np.float32(math.log(REL_MAX_DISTANCE / max_exact))
                         * np.float32(REL_BUCKETS - max_exact)).astype(np.int32)
    large = np.minimum(large, REL_BUCKETS - 1)
    return np.where(dist < max_exact, dist, large)


def _bucket_bands(dilation, span):
    table = _rel_bucket_table(dilation, span)
    bands = []
    for n, b in enumerate(table):
        if bands and bands[-1][0] == int(b):
            bands[-1][2] = n
        else:
            assert not any(bb[0] == int(b) for bb in bands)
            bands.append([int(b), n, n])
    return [tuple(b) for b in bands]


def _steps_back():
    i = lax.broadcasted_iota(jnp.int32, (DIL_BLOCK, DIL_BLOCK), 0)
    j = lax.broadcasted_iota(jnp.int32, (DIL_BLOCK, DIL_BLOCK), 1)
    return i - j, DIL_BLOCK + i - j


def _bias_tiles(rb_ref, h, bands, span):
    n_cur, n_prev = _steps_back()
    t_cur = jnp.zeros((DIL_BLOCK, DIL_BLOCK), F32)
    t_prev = jnp.zeros((DIL_BLOCK, DIL_BLOCK), F32)
    for b, lo, hi in bands:
        val = rb_ref[b, h]
        t_cur = jnp.where((n_cur >= lo) & (n_cur <= hi), val, t_cur)
        t_prev = jnp.where((n_prev >= lo) & (n_prev <= hi), val, t_prev)
    t_cur = jnp.where(n_cur >= 0, t_cur, NEG_INF)
    t_prev = jnp.where(n_prev <= span, t_prev, NEG_INF)
    return t_cur, t_prev


def _dil_rows(start, dilation):
    return pl.ds(start, DIL_BLOCK, stride=dilation) if dilation > 1 else pl.ds(start, DIL_BLOCK)


def _dil_block_groups(s, dilation, run_group):
    group = dilation * DIL_BLOCK
    per_residue = s // group

    def block(r, n):
        if isinstance(n, int):
            return n * group + r, max(n - 1, 0) * group + r, n == 0
        return (pl.multiple_of(n * group, DIL_BLOCK) + r,
                pl.multiple_of(jnp.maximum(n - 1, 0) * group, DIL_BLOCK) + r, n == 0)

    if per_residue >= DIL_GROUP:
        assert per_residue % DIL_GROUP == 0
        for r in range(dilation):
            def trip(it, carry, r=r):
                run_group([block(r, it * DIL_GROUP + j) for j in range(DIL_GROUP)])
                return carry
            lax.fori_loop(0, per_residue // DIL_GROUP, trip, 0)
    else:
        residues = DIL_GROUP // per_residue
        assert residues * per_residue == DIL_GROUP and dilation % residues == 0
        for r0 in range(0, dilation, residues):
            run_group([block(r, n) for r in range(r0, r0 + residues) for n in range(per_residue)])


def _dil_fwd(qkv, rel_bias, n_heads, col0, *, name, carry=None):
    s = qkv.shape[0]
    e = HEAD_DIM
    scale = e ** -0.5
    n_pat = len(DIL_PATTERNS)
    for window, d in DIL_PATTERNS:
        assert window // d == DIL_BLOCK and s % (d * DIL_BLOCK) == 0
    bands = [_bucket_bands(d, w // d) for w, d in DIL_PATTERNS]

    def body(rb_ref, q_ref, k_ref, v_ref, y_ref, lse_ref, qf, kf, vf, *scr):
        o_scr, l_scr = scr[:n_pat], scr[n_pat:]
        h = pl.program_id(0)
        qf[...] = q_ref[...].astype(F32)
        kf[...] = k_ref[...].astype(F32)
        vf[...] = v_ref[...].astype(F32)
        for pi, (window, d) in enumerate(DIL_PATTERNS):
            t_cur, t_prev = _bias_tiles(rb_ref, h, bands[pi], window // d)

            def run_group(blocks, d=d, pi=pi, t_cur=t_cur, t_prev=t_prev):
                rows = [(_dil_rows(q0, d), _dil_rows(p0, d)) for q0, p0, _ in blocks]
                qb = [qf[cur, :].astype(BF16) for cur, _ in rows]
                s_c = [_nt(q, kf[cur, :].astype(BF16)) for q, (cur, _) in zip(qb, rows)]
                s_p = [_nt(q, kf[prev, :].astype(BF16)) for q, (_, prev) in zip(qb, rows)]
                s_c = [x * scale + t_cur for x in s_c]
                s_p = [x * scale + (t_prev + jnp.where(first, NEG_INF, 0.0)) for x, (_, _, first) in zip(s_p, blocks)]
                m = [jnp.max(jnp.maximum(a, b), axis=-1, keepdims=True) for a, b in zip(s_c, s_p)]
                p_c = [jnp.exp(a - mm) for a, mm in zip(s_c, m)]
                p_p = [jnp.exp(b - mm) for b, mm in zip(s_p, m)]
                l = [jnp.sum(a + b, axis=-1, keepdims=True) for a, b in zip(p_c, p_p)]
                o = [_nn(a.astype(BF16), vf[cur, :].astype(BF16)) + _nn(b.astype(BF16), vf[prev, :].astype(BF16))
                     for a, b, (cur, prev) in zip(p_c, p_p, rows)]
                for (cur, _), oo, ll, mm in zip(rows, o, l, m):
                    o_scr[pi][cur, :] = oo / ll
                    l_scr[pi][cur, :] = mm + jnp.log(ll)

            _dil_block_groups(s, d, run_group)
        lses = [l_scr[pi][...] for pi in range(n_pat)]
        m = functools.reduce(jnp.maximum, lses)
        ws = [jnp.exp(l - m) for l in lses]
        tot = functools.reduce(jnp.add, ws)
        y = functools.reduce(jnp.add, [w * o_scr[pi][...] for pi, w in enumerate(ws)])
        y_ref[...] = y / tot
        lse_ref[...] = m + jnp.log(tot)

    h_ = n_heads
    return _call(
        body, name=name, grid=(h_,),
        in_specs=[pl.BlockSpec(memory_space=pltpu.SMEM),
                  pl.BlockSpec((s, e), lambda h: (0, col0 + h)),
                  pl.BlockSpec((s, e), lambda h: (0, col0 + h_ + h)),
                  pl.BlockSpec((s, e), lambda h: (0, col0 + 2 * h_ + h))],
        out_specs=[pl.BlockSpec((s, e), lambda h: (0, h)), pl.BlockSpec((None, s, 1), lambda h: (h, 0, 0))],
        out_shape=[jax.ShapeDtypeStruct((s, h_ * e), F32), jax.ShapeDtypeStruct((h_, s, 1), F32)],
        operands=(rel_bias, qkv, qkv, qkv),
        scratch=[pltpu.VMEM((s, e), F32)] * 3 + [pltpu.VMEM((s, e), F32)] * n_pat
        + [pltpu.VMEM((s, 1), F32)] * n_pat,
        semantics=("parallel",), carry=carry)


def _dil_bwd(qkv, rel_bias, lse, y, dy, n_heads, col0, *, name, carry=None):
    s = qkv.shape[0]
    e = HEAD_DIM
    scale = e ** -0.5
    bands = [_bucket_bands(d, w // d) for w, d in DIL_PATTERNS]

    def body(rb_ref, q_ref, k_ref, v_ref, lse_ref, y_ref, dy_ref, dq_ref, dk_ref, dv_ref, drb_ref,
             qf, kf, vf, dqf, dkf, dvf, delta, dt_cur, dt_prev):
        h = pl.program_id(0)
        qf[...] = q_ref[...].astype(F32)
        kf[...] = k_ref[...].astype(F32)
        vf[...] = v_ref[...].astype(F32)
        dqf[...] = jnp.zeros_like(dqf)
        dkf[...] = jnp.zeros_like(dkf)
        dvf[...] = jnp.zeros_like(dvf)
        delta[...] = jnp.sum(dy_ref[...] * y_ref[...], axis=-1, keepdims=True)
        lane = lax.broadcasted_iota(jnp.int32, (1, 128), 1)
        drb = jnp.zeros((1, 128), F32)
        n_cur, n_prev = _steps_back()
        for pi, (window, d) in enumerate(DIL_PATTERNS):
            t_cur, t_prev = _bias_tiles(rb_ref, h, bands[pi], window // d)
            dt_cur[...] = jnp.zeros_like(dt_cur)
            dt_prev[...] = jnp.zeros_like(dt_prev)

            def run_group(blocks, d=d, t_cur=t_cur, t_prev=t_prev):
                rows = [(_dil_rows(q0, d), _dil_rows(p0, d)) for q0, p0, _ in blocks]
                qb = [qf[cur, :].astype(BF16) for cur, _ in rows]
                kc = [kf[cur, :].astype(BF16) for cur, _ in rows]
                kp = [kf[prev, :].astype(BF16) for _, prev in rows]
                vc = [vf[cur, :].astype(BF16) for cur, _ in rows]
                vp = [vf[prev, :].astype(BF16) for _, prev in rows]
                do = [dy_ref[cur, :].astype(BF16) for cur, _ in rows]
                lse_b = [lse_ref[cur, :] for cur, _ in rows]
                delta_b = [delta[cur, :] for cur, _ in rows]
                s_c = [_nt(q, k) for q, k in zip(qb, kc)]
                s_p = [_nt(q, k) for q, k in zip(qb, kp)]
                dp_c = [_nt(g, v) for g, v in zip(do, vc)]
                dp_p = [_nt(g, v) for g, v in zip(do, vp)]
                p_c = [jnp.exp(x * scale + t_cur - ls) for x, ls in zip(s_c, lse_b)]
                p_p = [jnp.exp(x * scale + (t_prev + jnp.where(first, NEG_INF, 0.0)) - ls)
                       for x, ls, (_, _, first) in zip(s_p, lse_b, blocks)]
                ds_c = [p * (g - dl) for p, g, dl in zip(p_c, dp_c, delta_b)]
                ds_p = [p * (g - dl) for p, g, dl in zip(p_p, dp_p, delta_b)]
                dt_cur[...] += functools.reduce(jnp.add, ds_c)
                dt_prev[...] += functools.reduce(jnp.add, ds_p)
                p_c, p_p = [p.astype(BF16) for p in p_c], [p.astype(BF16) for p in p_p]
                ds_c, ds_p = [x.astype(BF16) for x in ds_c], [x.astype(BF16) for x in ds_p]
                dq = [scale * (_nn(a, k1) + _nn(b, k2)) for a, b, k1, k2 in zip(ds_c, ds_p, kc, kp)]
                dk_c = [scale * _tn(a, q) for a, q in zip(ds_c, qb)]
                dv_c = [_tn(p, g) for p, g in zip(p_c, do)]
                dk_p = [scale * _tn(b, q) for b, q in zip(ds_p, qb)]
                dv_p = [_tn(p, g) for p, g in zip(p_p, do)]
                for i, (cur, prev) in enumerate(rows):
                    dqf[cur, :] += dq[i]
                    dkf[cur, :] += dk_c[i]
                    dvf[cur, :] += dv_c[i]
                    dkf[prev, :] += dk_p[i]
                    dvf[prev, :] += dv_p[i]

            _dil_block_groups(s, d, run_group)
            dtc, dtp = dt_cur[...], dt_prev[...]
            for b, lo, hi in bands[pi]:
                tot = (jnp.sum(jnp.where((n_cur >= lo) & (n_cur <= hi), dtc, 0.0))
                       + jnp.sum(jnp.where((n_prev >= lo) & (n_prev <= hi), dtp, 0.0)))
                drb = drb + jnp.where(lane == b, tot, 0.0)
        dq_ref[...] = dqf[...].astype(BF16)
        dk_ref[...] = dkf[...].astype(BF16)
        dv_ref[...] = dvf[...].astype(BF16)
        drb_ref[...] = drb

    h_ = n_heads
    col = pl.BlockSpec((s, e), lambda h: (0, h))
    return _call(
        body, name=name, grid=(h_,),
        in_specs=[pl.BlockSpec(memory_space=pltpu.SMEM),
                  pl.BlockSpec((s, e), lambda h: (0, col0 + h)),
                  pl.BlockSpec((s, e), lambda h: (0, col0 + h_ + h)),
                  pl.BlockSpec((s, e), lambda h: (0, col0 + 2 * h_ + h)),
                  pl.BlockSpec((None, s, 1), lambda h: (h, 0, 0)), col, col],
        out_specs=[col, col, col, pl.BlockSpec((None, 1, 128), lambda h: (h, 0, 0))],
        out_shape=[jax.ShapeDtypeStruct((s, h_ * e), BF16)] * 3 + [jax.ShapeDtypeStruct((h_, 1, 128), F32)],
        operands=(rel_bias, qkv, qkv, qkv, lse, y, dy),
        scratch=[pltpu.VMEM((s, e), F32)] * 6 + [pltpu.VMEM((s, 1), F32)]
        + [pltpu.VMEM((DIL_BLOCK, DIL_BLOCK), F32)] * 2,
        semantics=("parallel",), carry=carry)


def _place():
    x, y, c = lax.axis_index("x"), lax.axis_index("y"), lax.axis_index("c")
    chips = [(1 - x, y), (x, 1 - y), (1 - x, 1 - y)]
    return x, y, c, chips


def _block_index(px, py, pc):
    return 4 * px + 2 * py + pc


def _gather_task(blocks, *, row0=0, rows=None, into=None):
    n = len(blocks)
    part = blocks[0].shape[0]
    rows = rows or part

    def copies(ins, outs, sems):
        send_sems, recv_sems, local_sems = sems
        x, y, c, chips = _place()
        me, sibling = (x, y, c), (x, y, 1 - c)

        def place_of(a, block):
            slot = outs[a].at[_block_index(*block)]
            return slot if part == rows else slot.at[pl.ds(row0, part)]

        def copy(a, k, block, to, src=None):
            slot = place_of(a, block)
            return pltpu.make_async_remote_copy(
                src_ref=slot if src is None else src, dst_ref=slot,
                send_sem=send_sems.at[a, k], recv_sem=recv_sems.at[a, k],
                device_id=to, device_id_type=MESH)

        local = lambda a: pltpu.make_async_copy(ins[a], place_of(a, me), local_sems.at[a])
        return copy, local, me, sibling, c, chips

    def start(ins, outs, sems):
        copy, local, me, sibling, c, chips = copies(ins, outs, sems)
        for a in range(n):
            local(a).start()
            copy(a, 0, me, sibling, src=ins[a]).start()
            for j, chip in enumerate(chips):
                copy(a, 1 + j, me, (*chip, c), src=ins[a]).start()

    def forward(ins, outs, sems):
        copy, local, me, sibling, c, chips = copies(ins, outs, sems)
        for a in range(n):
            for j, chip in enumerate(chips):
                copy(a, 1 + j, (*chip, c), me).wait_recv()
                copy(a, 4 + j, (*chip, c), sibling).start()

    def finish(ins, outs, sems):
        copy, local, me, sibling, c, chips = copies(ins, outs, sems)
        for a in range(n):
            copy(a, 0, sibling, me).wait_recv()
            for j, chip in enumerate(chips):
                copy(a, 4 + j, (*chip, 1 - c), me).wait_recv()
        for a in range(n):
            for k in range(7):
                copy(a, k, me, sibling, src=ins[a]).wait_send()
            local(a).wait()

    return _Carried(
        list(blocks) + list(into or []),
        [jax.ShapeDtypeStruct((N_DEV, rows) + b.shape[1:], b.dtype) for b in blocks],
        [pltpu.SemaphoreType.DMA((n, 7)), pltpu.SemaphoreType.DMA((n, 7)), pltpu.SemaphoreType.DMA((n,))],
        start, forward, finish, aliases={n + a: a for a in range(n)} if into else None)


def _exchange_task(arrays, n_slots, route):
    n = len(arrays)

    def copies(ins, outs, sems):
        send_sems, recv_sems = sems
        out = []
        for a in range(n):
            for j in range(n_slots):
                src, to = route(j)
                out.append(pltpu.make_async_remote_copy(
                    src_ref=ins[a].at[src], dst_ref=outs[a].at[j],
                    send_sem=send_sems.at[a, j], recv_sem=recv_sems.at[a, j],
                    device_id=to, device_id_type=MESH))
        return out

    def start(ins, outs, sems):
        for cp in copies(ins, outs, sems):
            cp.start()

    def finish(ins, outs, sems):
        for cp in copies(ins, outs, sems):
            cp.wait()

    return _Carried(
        arrays, [jax.ShapeDtypeStruct((n_slots,) + g.shape[1:], g.dtype) for g in arrays],
        [pltpu.SemaphoreType.DMA((n, n_slots)), pltpu.SemaphoreType.DMA((n, n_slots))],
        start, lambda ins, outs, sems: None, finish)


def _sibling_task(grads):
    def route(q):
        x, y, c, _ = _place()
        return 2 * q + (1 - c), (x, y, 1 - c)
    return _exchange_task(grads, 4, route)


def _chips_task(parts):
    def route(j):
        x, y, c, chips = _place()
        px, py = chips[j]
        return 2 * px + py, (px, py, c)
    return _exchange_task(parts, 3, route)


def _chip_partials(grad, got, place, *, name):
    _, r, cdim = grad.shape
    tr = _row_tile(r, 512)

    def body(place_ref, g_ref, s_ref, o_ref):
        o_ref[...] = (g_ref[...] + s_ref[...]).astype(BF16)

    return pl.pallas_call(
        body, name=name,
        grid_spec=pltpu.PrefetchScalarGridSpec(
            num_scalar_prefetch=1, grid=(4, r // tr),
            in_specs=[pl.BlockSpec((None, tr, cdim), lambda q, i, pos: (2 * q + pos[2], i, 0)),
                      pl.BlockSpec((None, tr, cdim), lambda q, i, pos: (q, i, 0))],
            out_specs=pl.BlockSpec((None, tr, cdim), lambda q, i, pos: (q, i, 0))),
        out_shape=jax.ShapeDtypeStruct((4, r, cdim), BF16),
        compiler_params=_params("parallel", "parallel"),
    )(place, grad, got)


def _all_reduce_small(v, *, name):
    r, w = v.shape

    def body(v_ref, o_ref, buf, send_sems, recv_sems):
        x, y, c, _ = _place()
        me = _block_index(x, y, c)
        buf[me] = v_ref[...]
        copies = []
        for k in range(1, N_DEV):
            fx, fy, fc = (k >> 2) & 1, (k >> 1) & 1, k & 1
            peer = (x ^ fx, y ^ fy, c ^ fc)
            cp = pltpu.make_async_remote_copy(
                src_ref=v_ref, dst_ref=buf.at[me],
                send_sem=send_sems.at[k - 1], recv_sem=recv_sems.at[k - 1],
                device_id=peer, device_id_type=MESH)
            cp.start()
            copies.append(cp)
        for cp in copies:
            cp.wait()
        acc = buf[0]
        for p in range(1, N_DEV):
            acc = acc + buf[p]
        o_ref[...] = acc

    vmem = pl.BlockSpec(memory_space=pltpu.VMEM)
    return pl.pallas_call(
        body, name=name, in_specs=[vmem], out_specs=vmem,
        out_shape=jax.ShapeDtypeStruct((r, w), F32),
        scratch_shapes=[pltpu.VMEM((N_DEV, r, w), F32), pltpu.SemaphoreType.DMA((N_DEV - 1,)),
                        pltpu.SemaphoreType.DMA((N_DEV - 1,))],
    )(v)


def _adamw_math(w, g, m, v):
    m = ADAM_B1 * m + (1.0 - ADAM_B1) * g
    v = ADAM_B2 * v + (1.0 - ADAM_B2) * (g * g)
    m_hat = m / (1.0 - ADAM_B1 ** ADAM_STEP)
    v_hat = v / (1.0 - ADAM_B2 ** ADAM_STEP)
    delta = -ADAM_LR * (m_hat / (jnp.sqrt(v_hat) + ADAM_EPS) + ADAM_WD * w)
    return delta, m, v


def _adamw_sharded(w, m, v, part, got, place, *, name, carry=None):
    n_l, r, cdim = w.shape
    tr = _row_tile(r, 256)

    def body(place_ref, *refs):
        w_ref, m_ref, v_ref = refs[:3]
        p_refs, g_refs = refs[3:3 + n_l], refs[3 + n_l:3 + 2 * n_l]
        g_out, d_out, m_out, v_out = refs[3 + 2 * n_l:]
        for l in range(n_l):
            @pl.when(pl.program_id(0) == l)
            def _(l=l):
                g = p_refs[l][...].astype(F32)
                for j in range(3):
                    g = g + g_refs[l][j].astype(F32)
                delta, m_new, v_new = _adamw_math(w_ref[...], g, m_ref[...], v_ref[...])
                g_out[...] = g
                d_out[...] = delta
                m_out[...] = m_new
                v_out[...] = v_new

    shard = pl.BlockSpec((None, tr, cdim), lambda l, i, pos: (l, i, 0))
    p_specs = [pl.BlockSpec((None, tr, cdim),
                            lambda l, i, pos, li=li: (2 * pos[0] + pos[1], jnp.where(l == li, i, 0), 0))
               for li in range(n_l)]
    g_specs = [pl.BlockSpec((3, tr, cdim), lambda l, i, pos, li=li: (0, jnp.where(l == li, i, 0), 0))
               for li in range(n_l)]
    return _call(
        body, name=name, grid=(n_l, r // tr), tables=(place,),
        in_specs=[shard] * 3 + p_specs + g_specs, out_specs=[shard] * 4,
        out_shape=[jax.ShapeDtypeStruct(w.shape, F32)] * 4,
        operands=(w, m, v, *part, *got), semantics=("arbitrary", "parallel"), carry=carry)


def _adamw_small(w, g, m, v, *, name):
    def body(w_ref, g_ref, m_ref, v_ref, d_out, m_out, v_out):
        delta, m_new, v_new = _adamw_math(w_ref[...], g_ref[...], m_ref[...], v_ref[...])
        d_out[...] = delta
        m_out[...] = m_new
        v_out[...] = v_new

    vmem = pl.BlockSpec(memory_space=pltpu.VMEM)
    return pl.pallas_call(
        body, name=name, in_specs=[vmem] * 4, out_specs=[vmem] * 3,
        out_shape=[jax.ShapeDtypeStruct(w.shape, F32)] * 3,
    )(w, g, m, v)


def _pad_lanes(a, width=128):
    return jnp.pad(a, ((0, 0), (0, width - a.shape[1])))


def _train_step(x, target, small, shards, place):
    w_in, w_out, w_mlp_in, w_mlp_out = shards
    depth, d_model, in_shard = w_in.shape
    n_heads = small["forget_b"].shape[1]
    dh = n_heads * HEAD_DIM
    d_ff = w_mlp_in.shape[2] * N_DEV
    row = lambda a: a.reshape(1, -1)
    block = {"in": w_in.astype(BF16), "out": w_out.astype(BF16), "mi": w_mlp_in.astype(BF16),
             "mo": w_mlp_out.astype(BF16)}
    whole = {}

    gate_lo, gate_hi = 3 * dh, 3 * dh + n_heads

    def shard_columns(g, lo, hi):
        pieces = []
        while lo < hi:
            p, a = divmod(lo, in_shard)
            b = min(in_shard, a + hi - lo)
            pieces.append(g[p][:, a:b])
            lo += b - a
        return pieces

    def in_columns(d_qkv, d_gate, lo, hi):
        pieces = []
        if lo < gate_lo:
            pieces.append(d_qkv[:, lo:min(hi, gate_lo)])
        if lo < gate_hi and hi > gate_lo:
            pieces.append(d_gate[:, max(lo, gate_lo) - gate_lo:min(hi, gate_hi) - gate_lo])
        if hi > gate_hi:
            pieces.append(d_qkv[:, max(lo, gate_hi) - n_heads:hi - n_heads])
        return pieces

    def unpack(kind, l, g):
        if kind == "in":
            whole["qkv", l] = jnp.concatenate(
                shard_columns(g, 0, gate_lo) + shard_columns(g, gate_hi, N_DEV * in_shard), axis=1)
            whole["f", l] = _pad_lanes(jnp.concatenate(shard_columns(g, gate_lo, gate_hi), axis=1))
        elif kind == "out":
            whole["out", l] = g.reshape(2 * dh, d_model)
        elif kind == "mi":
            whole["mi", l] = g
        else:
            whole["mo", l] = g.reshape(d_ff, d_model)

    half_done = {}

    def gathering(kind, l, part):
        blk = block[kind][l]
        rows = blk.shape[0]
        if part == "all":
            return _gather_task([blk]), lambda got: unpack(kind, l, got[0])
        if part == "first":
            return (_gather_task([blk[:rows // 2]], rows=rows),
                    lambda got: half_done.__setitem__((kind, l), got[0]))
        return (_gather_task([blk[rows // 2:]], row0=rows // 2, rows=rows, into=[half_done.pop((kind, l))]),
                lambda got: unpack(kind, l, got[0]))

    def carried(call, *wanted):
        wanted = [gathering(*w) for w in wanted if w[1] < depth]
        if not wanted:
            return call()
        res, got = call(carry=_join([task for task, _ in wanted]))
        for (_, done), g in zip(wanted, got):
            done([g])
        return res

    saved = []
    h1 = carried(functools.partial(_rmsnorm_fwd, x, row(small["norm1_g"][0]), name="norm1_fwd_0"), ("in", 0, "all"))
    for l in range(depth):
        qkv = carried(functools.partial(_matmul, h1, whole["qkv", l], out_dtype=BF16, name=f"qkv_proj_{l}"),
                      ("mi", l, "first"))
        f = _matmul(h1, whole["f", l], name=f"gate_proj_{l}")
        fb = _pad_lanes(row(small["forget_b"][l]))
        c = _gates_fwd(f, fb, name=f"gates_fwd_{l}")
        c_heads = c[:, :n_heads].T
        cq = c_heads[:, None, :]
        ck = jnp.broadcast_to(c_heads[:, :, None] * LOG2E, c_heads.shape + (128,))
        ya, lse_a = carried(functools.partial(_fox_fwd, qkv, cq, ck, n_heads, name=f"fox_fwd_{l}"),
                            ("mi", l, "second"), ("out", l, "all"))
        yb, lse_b = carried(functools.partial(_dil_fwd, qkv, small["rel_bias"], n_heads, 3 * n_heads,
                                              name=f"dil_fwd_{l}"), ("mo", l, "first"))
        mixed = _outnorm_fwd(ya, yb, row(small["outnorm_a_g"][l]), row(small["outnorm_b_g"][l]),
                             name=f"outnorm_fwd_{l}")
        x1, h2 = _matmul(mixed, whole["out", l], addend=x, tm=512, tn=d_model, norm_gain=row(small["norm2_g"][l]),
                         name=f"out_proj_{l}")
        relu, act = carried(functools.partial(_matmul, h2, whole["mi", l], b_chunked=True, out_dtype=BF16,
                                              mode="relu_sq", name=f"mlp_in_{l}"),
                            ("mo", l, "second"))
        mlp_out = functools.partial(_matmul, act, whole["mo", l], addend=x1, name=f"mlp_out_{l}")
        saved.append((x, h1, qkv, f, fb, cq, ck, ya, lse_a, yb, lse_b, mixed, x1, h2, relu, act))
        if l + 1 < depth:
            x, h1 = carried(functools.partial(mlp_out, tm=512, tn=d_model, norm_gain=row(small["norm1_g"][l + 1])),
                            ("in", l + 1, "all"))
        else:
            x = mlp_out()
    dx, dx16, loss_tile, d_final = _loss_head(x, row(small["final_norm_g"]), target, name="loss_head")

    parts = {k: [None] * depth for k in ("in", "out", "mi", "mo")}
    got = {k: [None] * depth for k in ("in", "out", "mi", "mo")}
    small_grads = [None] * depth

    def to_sibling(call, kind, l, grad):
        res, (from_sibling,) = call(carry=_sibling_task([grad]))
        parts[kind][l] = _chip_partials(grad, from_sibling, place, name=f"chip_partials_{kind}_{l}")
        return res

    def to_chips(call, kind, l):
        if l >= depth:
            return call()
        res, (got[kind][l],) = call(carry=_chips_task([parts[kind][l]]))
        return res

    for l in reversed(range(depth)):
        x0, h1, qkv, f, fb, cq, ck, ya, lse_a, yb, lse_b, mixed, x1, h2, relu, act = saved[l]
        du = to_chips(functools.partial(_matmul, dx16, whole["mo", l], tb=True, out_dtype=BF16, mode="mul2",
                                        aux=relu, name=f"mlp_out_dx_{l}"), "in", l + 1)
        dw_mo = _matmul(act, dx16, ta=True, name=f"mlp_out_dw_{l}")
        dw_mo = dw_mo.reshape(N_DEV, d_ff // N_DEV, d_model)
        dh2 = to_sibling(functools.partial(_matmul, du, whole["mi", l], tb=True, b_chunked=True,
                                           name=f"mlp_in_dx_{l}"),
                         "mo", l, dw_mo)
        dw_mi = to_chips(functools.partial(_matmul, h2, du, ta=True, out_chunks=N_DEV,
                                           name=f"mlp_in_dw_{l}"), "mo", l)
        dx1, dx1_16, d_norm2 = _rmsnorm_bwd(x1, row(small["norm2_g"][l]), dh2, dx, name=f"norm2_bwd_{l}")
        dmixed = to_sibling(functools.partial(_matmul, dx1_16, whole["out", l], tb=True, name=f"out_proj_dx_{l}"),
                            "mi", l, dw_mi)
        dw_out = _matmul(mixed, dx1_16, ta=True, name=f"out_proj_dw_{l}")
        dw_out = dw_out.reshape(N_DEV, 2 * dh // N_DEV, d_model)
        dya, dyb, d_ga, d_gb = _outnorm_bwd(ya, yb, row(small["outnorm_a_g"][l]), row(small["outnorm_b_g"][l]),
                                            dmixed, name=f"outnorm_bwd_{l}")
        dqa, dka, dva, dcq, dck = to_chips(functools.partial(
            _fox_bwd, qkv, cq, ck, lse_a, ya, dya, n_heads, name=f"fox_bwd_{l}"), "mi", l)
        dqb, dkb, dvb, d_rb = to_sibling(functools.partial(
            _dil_bwd, qkv, small["rel_bias"], lse_b, yb, dyb, n_heads, 3 * n_heads, name=f"dil_bwd_{l}"),
            "out", l, dw_out)
        dc = _pad_lanes((dcq[:, 0, :] + dck[:, 0, :]).T)
        df, dfb = _gates_bwd(f, fb, dc, name=f"gates_bwd_{l}")
        dqkv = jnp.concatenate([dqa, dka, dva, dqb, dkb, dvb], axis=1)
        dw_qkv = to_chips(functools.partial(_matmul, h1, dqkv, ta=True, name=f"qkv_proj_dw_{l}"), "out", l)
        dw_f = _matmul(h1, df, ta=True, name=f"gate_proj_dw_{l}")
        dw_in = jnp.stack([jnp.concatenate(in_columns(dw_qkv, dw_f, p * in_shard, (p + 1) * in_shard), axis=1)
                           for p in range(N_DEV)])
        dh1 = _matmul(df, whole["f", l], tb=True, name=f"gate_proj_dx_{l}")
        dh1 = to_sibling(functools.partial(_matmul, dqkv, whole["qkv", l], tb=True, addend=dh1,
                                           name=f"qkv_proj_dx_{l}"), "in", l, dw_in)
        dx, dx16, d_norm1 = _rmsnorm_bwd(x0, row(small["norm1_g"][l]), dh1, dx1, name=f"norm1_bwd_{l}")
        small_grads[l] = dict(norm1_g=d_norm1[0], forget_b=dfb[0, :n_heads], rel_bias=d_rb[:, 0, :REL_BUCKETS].T,
                              outnorm_a_g=d_ga[0], outnorm_b_g=d_gb[0], norm2_g=d_norm2[0])
    return loss_tile[0, 0], dx, d_final[0], small_grads, parts, got


def _pack_small(parts, rows):
    flat = jnp.concatenate([p.reshape(-1).astype(F32) for p in parts])
    return jnp.pad(flat, (0, rows * 128 - flat.shape[0])).reshape(rows, 128)


def _unpack_small(packed, shapes):
    flat = packed.reshape(-1)
    out, pos = [], 0
    for shp in shapes:
        size = int(np.prod(shp))
        out.append(flat[pos:pos + size].reshape(shp))
        pos += size
    return out


def kernel(x, norm1_g, w_in, forget_b, rel_bias, outnorm_a_g, outnorm_b_g, w_out, norm2_g, w_mlp_in, w_mlp_out, final_norm_g, loss_target, m_norm1_g, m_w_in, m_forget_b, m_rel_bias, m_outnorm_a_g, m_outnorm_b_g, m_w_out, m_norm2_g, m_w_mlp_in, m_w_mlp_out, m_final_norm_g, v_norm1_g, v_w_in, v_forget_b, v_rel_bias, v_outnorm_a_g, v_outnorm_b_g, v_w_out, v_norm2_g, v_w_mlp_in, v_w_mlp_out, v_final_norm_g):
    depth, d_model, in_shard = w_in.shape
    n_heads = forget_b.shape[1]
    assert in_shard * N_DEV == 6 * n_heads * HEAD_DIM + n_heads and x.shape[0] == 1
    place = jnp.stack([lax.axis_index("x"), lax.axis_index("y"), lax.axis_index("c")]).astype(jnp.int32)

    small_names = ["norm1_g", "forget_b", "rel_bias", "outnorm_a_g", "outnorm_b_g", "norm2_g", "final_norm_g"]
    small_w = dict(norm1_g=norm1_g, forget_b=forget_b, rel_bias=rel_bias, outnorm_a_g=outnorm_a_g,
                   outnorm_b_g=outnorm_b_g, norm2_g=norm2_g, final_norm_g=final_norm_g)
    loss_part, dx, d_final, grads, parts, got = _train_step(
        x[0], loss_target[0], small_w, (w_in, w_out, w_mlp_in, w_mlp_out), place)

    small_m = dict(norm1_g=m_norm1_g, forget_b=m_forget_b, rel_bias=m_rel_bias, outnorm_a_g=m_outnorm_a_g,
                   outnorm_b_g=m_outnorm_b_g, norm2_g=m_norm2_g, final_norm_g=m_final_norm_g)
    small_v = dict(norm1_g=v_norm1_g, forget_b=v_forget_b, rel_bias=v_rel_bias, outnorm_a_g=v_outnorm_a_g,
                   outnorm_b_g=v_outnorm_b_g, norm2_g=v_norm2_g, final_norm_g=v_final_norm_g)
    small_g = {k: jnp.stack([g[k] for g in grads]) for k in small_names if k not in ("rel_bias", "final_norm_g")}
    small_g["rel_bias"] = functools.reduce(jnp.add, [g["rel_bias"] for g in grads])
    small_g["final_norm_g"] = d_final
    shapes = [small_w[k].shape for k in small_names]
    total = sum(int(np.prod(s)) for s in shapes) + 1
    rows = -(-total // (8 * 128)) * 8
    packed_g = _pack_small([small_g[k] for k in small_names] + [loss_part], rows)
    packed_g = _all_reduce_small(packed_g, name="reduce_small")
    zero = jnp.zeros((1,), F32)
    packed_w = _pack_small([small_w[k] for k in small_names] + [zero], rows)
    packed_m = _pack_small([small_m[k] for k in small_names] + [zero], rows)
    packed_v = _pack_small([small_v[k] for k in small_names] + [zero + 1.0], rows)
    packed_d, packed_nm, packed_nv = _adamw_small(packed_w, packed_g, packed_m, packed_v, name="adamw_small")
    g_small = dict(zip(small_names, _unpack_small(packed_g, shapes)))
    d_small = dict(zip(small_names, _unpack_small(packed_d, shapes)))
    nm_small = dict(zip(small_names, _unpack_small(packed_nm, shapes)))
    nv_small = dict(zip(small_names, _unpack_small(packed_nv, shapes)))
    loss = packed_g.reshape(-1)[total - 1]

    big_out = {}
    last = parts["in"][0]
    last_halves = [last[:, :last.shape[1] // 2], last[:, last.shape[1] // 2:]]
    last_got = []
    for kind, nm, w, m, v in [("mi", "w_mlp_in", w_mlp_in, m_w_mlp_in, v_w_mlp_in),
                              ("mo", "w_mlp_out", w_mlp_out, m_w_mlp_out, v_w_mlp_out),
                              ("out", "w_out", w_out, m_w_out, v_w_out), ("in", "w_in", w_in, m_w_in, v_w_in)]:
        adamw = functools.partial(_adamw_sharded, w, m, v, parts[kind], got[kind], place, name=f"adamw_{nm}")
        if last_halves:
            big_out[nm], (half,) = adamw(carry=_chips_task([last_halves.pop(0)]))
            last_got.append(half)
            if not last_halves:
                got["in"][0] = jnp.concatenate(last_got, axis=1)
        else:
            big_out[nm] = adamw()

    order = ["norm1_g", "w_in", "forget_b", "rel_bias", "outnorm_a_g", "outnorm_b_g", "w_out", "norm2_g",
             "w_mlp_in", "w_mlp_out", "final_norm_g"]
    pick = lambda k, idx, small: big_out[k][idx] if k in big_out else small[k]
    outs = [loss, dx[None]]
    outs += [pick(k, 0, g_small) for k in order]
    outs += [pick(k, 1, d_small) for k in order]
    outs += [pick(k, 2, nm_small) for k in order]
    outs += [pick(k, 3, nv_small) for k in order]
    return tuple(outs)
```

```python
import functools
import math

import numpy as np
import jax
import jax.numpy as jnp
from jax import lax
from jax.experimental import pallas as pl
from jax.experimental.pallas import tpu as pltpu

F32 = jnp.float32
BF16 = jnp.bfloat16

HEAD_DIM = 128
DIL_PATTERNS = ((128, 1), (512, 4), (2048, 16))
DIL_BLOCK = 128
REL_BUCKETS = 32
REL_MAX_DISTANCE = 2048
NORM_EPS = 1e-6
NEG_INF = -1e30
ADAM_LR = 0.001
ADAM_B1 = 0.9
ADAM_B2 = 0.999
ADAM_EPS = 1e-08
ADAM_WD = 0.01
ADAM_STEP = 10

N_DEV = 8
V7X_VMEM_LIMIT_BYTES = 56 * 1024 * 1024
MESH = pl.DeviceIdType.MESH


def _params(*semantics):
    return pltpu.CompilerParams(dimension_semantics=semantics, vmem_limit_bytes=V7X_VMEM_LIMIT_BYTES)


HBM_SPEC = pl.BlockSpec(memory_space=pltpu.HBM)


class _Carried:
    def __init__(self, operands, out_shape, scratch, start, forward, finish, aliases=None):
        self.operands, self.out_shape, self.scratch = list(operands), list(out_shape), list(scratch)
        self.start, self.forward, self.finish = start, forward, finish
        self.aliases = dict(aliases or {})


def _join(tasks):
    if len(tasks) == 1:
        return tasks[0]
    spans, aliases = [], {}
    i0 = o0 = s0 = 0
    for t in tasks:
        i1, o1, s1 = i0 + len(t.operands), o0 + len(t.out_shape), s0 + len(t.scratch)
        spans.append((slice(i0, i1), slice(o0, o1), slice(s0, s1)))
        aliases.update({i0 + i: o0 + o for i, o in t.aliases.items()})
        i0, o0, s0 = i1, o1, s1

    def phase(which):
        def run(ins, outs, sems):
            for t, (si, so, ss) in zip(tasks, spans):
                getattr(t, which)(ins[si], outs[so], sems[ss])
        return run

    return _Carried(sum((t.operands for t in tasks), []), sum((t.out_shape for t in tasks), []),
                    sum((t.scratch for t in tasks), []), phase("start"), phase("forward"), phase("finish"),
                    aliases)


FORWARD_AT = 0.8


def _call(body, *, name, grid, in_specs, out_specs, out_shape, operands, scratch=(), semantics, carry=None,
          tables=()):
    in_specs, out_specs, out_shape, scratch = list(in_specs), list(out_specs), list(out_shape), list(scratch)
    n_tab = len(tables)

    def run(fn, in_specs, out_specs, out_shape, scratch, operands, semantics, aliases):
        return pl.pallas_call(
            fn, name=name,
            grid_spec=pltpu.PrefetchScalarGridSpec(
                num_scalar_prefetch=n_tab, grid=grid, in_specs=in_specs, out_specs=out_specs,
                scratch_shapes=scratch),
            out_shape=out_shape, input_output_aliases=aliases,
            compiler_params=_params(*semantics))(*tables, *operands)

    if carry is None:
        return run(body, in_specs, out_specs, out_shape, scratch, operands, semantics, {})
    n_in, n_out = len(in_specs), len(out_specs)
    ci, co, cs = len(carry.operands), len(carry.out_shape), len(carry.scratch)
    steps = int(np.prod(grid))
    forward_step = min(int(steps * FORWARD_AT), steps - 1)

    def carrying(*refs):
        tabs, refs = refs[:n_tab], refs[n_tab:]
        main_in, c_in = refs[:n_in], refs[n_in:n_in + ci]
        main_out = refs[n_in + ci:n_in + ci + n_out]
        c_out = refs[n_in + ci + n_out:n_in + ci + n_out + co]
        rest = refs[n_in + ci + n_out + co:]
        main_scr, c_scr = rest[:len(rest) - cs], rest[len(rest) - cs:]
        step = 0
        for axis, extent in enumerate(grid):
            step = step * extent + pl.program_id(axis)

        @pl.when(step == 0)
        def _():
            carry.start(c_in, c_out, c_scr)

        body(*tabs, *main_in, *main_out, *main_scr)

        @pl.when(step == forward_step)
        def _():
            carry.forward(c_in, c_out, c_scr)

        @pl.when(step == steps - 1)
        def _():
            carry.finish(c_in, c_out, c_scr)

    aliases = {n_tab + n_in + i: n_out + o for i, o in carry.aliases.items()}
    res = run(carrying, in_specs + [HBM_SPEC] * ci, out_specs + [HBM_SPEC] * co, out_shape + carry.out_shape,
              scratch + carry.scratch, (*operands, *carry.operands), ["arbitrary"] * len(grid), aliases)
    return res[:n_out], res[n_out:]


def _fit(dim, want, unit=128):
    if dim <= want:
        return dim
    t = want - want % unit
    while dim % t:
        t -= unit
    return t


def _matmul(a, b, *, name, ta=False, tb=False, out_dtype=F32, tm=1024, tn=1024, tk=2048,
            addend=None, mode=None, aux=None, out_chunks=None, b_chunked=False, norm_gain=None, carry=None):
    m_dim, k_dim = (a.shape[1], a.shape[0]) if ta else a.shape
    b_rows, b_cols = (b.shape[1], b.shape[0] * b.shape[2]) if b_chunked else b.shape
    n_dim = b_rows if tb else b_cols
    assert (b_cols if tb else b_rows) == k_dim
    b_chunk = b.shape[2] if b_chunked else b_cols
    tm, tn, tk = _fit(m_dim, tm), _fit(n_dim // (out_chunks or 1), tn), _fit(k_dim, tk)
    k_chunks = 1
    if b_chunked and tb:
        assert tk % b_chunk == 0 and not ta
        k_chunks = tk // b_chunk
    elif b_chunked:
        tn = _fit(b_chunk, tn)
    assert m_dim % tm == 0 and n_dim % tn == 0 and k_dim % tk == 0, (name, a.shape, b.shape)
    nk = k_dim // tk
    a_spec = (pl.BlockSpec((tk, tm), lambda i, j, k: (k, i)) if ta
              else pl.BlockSpec((tm, tk), lambda i, j, k: (i, k)))
    if b_chunked and tb:
        b_spec = pl.BlockSpec((k_chunks, tn, b_chunk), lambda i, j, k: (k, j, 0))
    elif b_chunked:
        per_b = b_chunk // tn
        b_spec = pl.BlockSpec((None, tk, tn), lambda i, j, k: (j // per_b, k, j % per_b))
    else:
        b_spec = (pl.BlockSpec((tn, tk), lambda i, j, k: (j, k)) if tb
                  else pl.BlockSpec((tk, tn), lambda i, j, k: (k, j)))
    mn_spec = pl.BlockSpec((tm, tn), lambda i, j, k: (i, j))
    if out_chunks is None:
        o_spec, o_shape = mn_spec, (m_dim, n_dim)
    else:
        per = n_dim // out_chunks // tn
        assert per * tn * out_chunks == n_dim
        o_spec = pl.BlockSpec((None, tm, tn), lambda i, j, k: (j // per, i, j % per))
        o_shape = (out_chunks, m_dim, n_dim // out_chunks)
    dims = (((0 if ta else 1,), (1 if tb else 0,)), ((), ()))
    n_out = 2 if (mode == "relu_sq" or norm_gain is not None) else 1
    in_place = mode is None and out_dtype == F32
    in_specs, operands = [a_spec, b_spec], [a, b]
    if addend is not None:
        in_specs.append(mn_spec)
        operands.append(addend)
    if mode == "mul2":
        in_specs.append(mn_spec)
        operands.append(aux)
    if norm_gain is not None:
        assert in_place and tn == n_dim and out_chunks is None
        in_specs.append(pl.BlockSpec((1, tn), lambda i, j, k: (0, 0)))
        operands.append(norm_gain)

    def body(*refs):
        a_ref, b_ref = refs[0], refs[1]
        pos = 2
        add_ref = aux_ref = gain_ref = None
        if addend is not None:
            add_ref, pos = refs[pos], pos + 1
        if mode == "mul2":
            aux_ref, pos = refs[pos], pos + 1
        if norm_gain is not None:
            gain_ref, pos = refs[pos], pos + 1
        outs = refs[pos:pos + n_out]

        def normed(c):
            if gain_ref is not None:
                outs[1][...] = _norm_fwd_math(c, gain_ref[...]).astype(BF16)

        def finish(acc):
            if add_ref is not None:
                acc = acc + add_ref[...].astype(F32)
            if mode == "relu_sq":
                r = jnp.maximum(acc, 0.0)
                outs[0][...] = r.astype(outs[0].dtype)
                outs[1][...] = (r * r).astype(outs[1].dtype)
            elif mode == "mul2":
                outs[0][...] = (acc * (2.0 * aux_ref[...].astype(F32))).astype(outs[0].dtype)
            else:
                outs[0][...] = acc.astype(outs[0].dtype)
                if nk == 1:
                    normed(acc)

        if k_chunks > 1:
            part = functools.reduce(jnp.add, [
                lax.dot_general(a_ref[:, c * b_chunk:(c + 1) * b_chunk].astype(BF16), b_ref[c].astype(BF16), dims,
                                preferred_element_type=F32) for c in range(k_chunks)])
        else:
            b_tile = b_ref[0] if (b_chunked and tb) else b_ref[...]
            part = lax.dot_general(a_ref[...].astype(BF16), b_tile.astype(BF16), dims,
                                   preferred_element_type=F32)
        k = pl.program_id(2)
        if nk == 1:
            finish(part)
        elif in_place:
            @pl.when(k == 0)
            def _():
                finish(part)

            @pl.when(k > 0)
            def _():
                outs[0][...] += part

            if gain_ref is not None:
                @pl.when(k == nk - 1)
                def _():
                    normed(outs[0][...])
        else:
            acc_ref = refs[-1]

            @pl.when(k == 0)
            def _():
                acc_ref[...] = part

            @pl.when(k > 0)
            def _():
                acc_ref[...] += part

            @pl.when(k == nk - 1)
            def _():
                finish(acc_ref[...])

    out_shape = [jax.ShapeDtypeStruct(o_shape, out_dtype)] * n_out
    if norm_gain is not None:
        out_shape[1] = jax.ShapeDtypeStruct(o_shape, BF16)
    res = _call(
        body, name=name, grid=(m_dim // tm, n_dim // tn, nk),
        in_specs=in_specs, out_specs=[o_spec] * n_out, out_shape=out_shape, operands=operands,
        scratch=[pltpu.VMEM((tm, tn), F32)] if nk > 1 and not in_place else [],
        semantics=("parallel", "parallel", "arbitrary"), carry=carry)
    if carry is not None:
        res, carried = res
        return (res if n_out > 1 else res[0]), carried
    return res if n_out > 1 else res[0]


def _norm_fwd_math(x, g):
    r = lax.rsqrt(jnp.mean(x * x, axis=-1, keepdims=True) + NORM_EPS)
    return (x * r) * g


def _norm_bwd_math(x, g, dy):
    r = lax.rsqrt(jnp.mean(x * x, axis=-1, keepdims=True) + NORM_EPS)
    xh = x * r
    dxh = dy * g
    dx = r * (dxh - xh * jnp.mean(dxh * xh, axis=-1, keepdims=True))
    return dx, jnp.sum(dy * xh, axis=0, keepdims=True)


def _row_tile(rows, want=512):
    t = min(rows, want)
    assert rows % t == 0
    return t


def _rmsnorm_fwd(x, g, *, name, carry=None):
    s, d = x.shape
    tr = _row_tile(s)

    def body(x_ref, g_ref, h_ref):
        h_ref[...] = _norm_fwd_math(x_ref[...], g_ref[...]).astype(BF16)

    res = _call(
        body, name=name, grid=(s // tr,),
        in_specs=[pl.BlockSpec((tr, d), lambda i: (i, 0)), pl.BlockSpec((1, d), lambda i: (0, 0))],
        out_specs=[pl.BlockSpec((tr, d), lambda i: (i, 0))],
        out_shape=[jax.ShapeDtypeStruct((s, d), BF16)],
        operands=(x, g), semantics=("parallel",), carry=carry)
    return res[0] if carry is None else (res[0][0], res[1])


def _rmsnorm_bwd(x, g, dh, dres, *, name, carry=None):
    s, d = x.shape
    tr = _row_tile(s)

    def body(x_ref, g_ref, dh_ref, dres_ref, dx_ref, dx16_ref, dg_ref):
        dx, dg = _norm_bwd_math(x_ref[...], g_ref[...], dh_ref[...])
        dx = dres_ref[...] + dx
        dx_ref[...] = dx
        dx16_ref[...] = dx.astype(BF16)

        @pl.when(pl.program_id(0) == 0)
        def _():
            dg_ref[...] = dg

        @pl.when(pl.program_id(0) > 0)
        def _():
            dg_ref[...] += dg

    row = pl.BlockSpec((tr, d), lambda i: (i, 0))
    vec = pl.BlockSpec((1, d), lambda i: (0, 0))
    return _call(
        body, name=name, grid=(s // tr,),
        in_specs=[row, vec, row, row], out_specs=[row, row, vec],
        out_shape=[jax.ShapeDtypeStruct((s, d), F32), jax.ShapeDtypeStruct((s, d), BF16),
                   jax.ShapeDtypeStruct((1, d), F32)],
        operands=(x, g, dh, dres), semantics=("arbitrary",), carry=carry)


def _outnorm_fwd(ya, yb, ga, gb, *, name):
    s, da = ya.shape
    db = yb.shape[1]
    tr = _row_tile(s)

    def body(ya_ref, yb_ref, ga_ref, gb_ref, o_ref):
        o_ref[:, :da] = _norm_fwd_math(ya_ref[...], ga_ref[...]).astype(BF16)
        o_ref[:, da:] = _norm_fwd_math(yb_ref[...], gb_ref[...]).astype(BF16)

    return pl.pallas_call(
        body, name=name, grid=(s // tr,),
        in_specs=[pl.BlockSpec((tr, da), lambda i: (i, 0)), pl.BlockSpec((tr, db), lambda i: (i, 0)),
                  pl.BlockSpec((1, da), lambda i: (0, 0)), pl.BlockSpec((1, db), lambda i: (0, 0))],
        out_specs=pl.BlockSpec((tr, da + db), lambda i: (i, 0)),
        out_shape=jax.ShapeDtypeStruct((s, da + db), BF16),
        compiler_params=_params("parallel"),
    )(ya, yb, ga, gb)


def _outnorm_bwd(ya, yb, ga, gb, dmixed, *, name):
    s, da = ya.shape
    db = yb.shape[1]
    tr = _row_tile(s)

    def body(ya_ref, yb_ref, ga_ref, gb_ref, dm_ref, dya_ref, dyb_ref, dga_ref, dgb_ref):
        dxa, dga = _norm_bwd_math(ya_ref[...], ga_ref[...], dm_ref[:, :da])
        dxb, dgb = _norm_bwd_math(yb_ref[...], gb_ref[...], dm_ref[:, da:])
        dya_ref[...] = dxa
        dyb_ref[...] = dxb

        @pl.when(pl.program_id(0) == 0)
        def _():
            dga_ref[...] = dga
            dgb_ref[...] = dgb

        @pl.when(pl.program_id(0) > 0)
        def _():
            dga_ref[...] += dga
            dgb_ref[...] += dgb

    ra = pl.BlockSpec((tr, da), lambda i: (i, 0))
    rb = pl.BlockSpec((tr, db), lambda i: (i, 0))
    va = pl.BlockSpec((1, da), lambda i: (0, 0))
    vb = pl.BlockSpec((1, db), lambda i: (0, 0))
    return pl.pallas_call(
        body, name=name, grid=(s // tr,),
        in_specs=[ra, rb, va, vb, pl.BlockSpec((tr, da + db), lambda i: (i, 0))],
        out_specs=[ra, rb, va, vb],
        out_shape=[jax.ShapeDtypeStruct((s, da), F32), jax.ShapeDtypeStruct((s, db), F32),
                   jax.ShapeDtypeStruct((1, da), F32), jax.ShapeDtypeStruct((1, db), F32)],
        compiler_params=_params("arbitrary"),
    )(ya, yb, ga, gb, dmixed)


def _loss_head(x, g, target, *, name):
    s, d = x.shape
    tr = _row_tile(s)

    def body(x_ref, g_ref, t_ref, dx_ref, dx16_ref, loss_ref, dg_ref):
        xv, gv = x_ref[...], g_ref[...]
        err = _norm_fwd_math(xv, gv) - t_ref[...]
        part = 0.5 * jnp.sum(jnp.mean(err * err, axis=-1, keepdims=True), axis=0, keepdims=True)
        dx, dg = _norm_bwd_math(xv, gv, err * (1.0 / d))
        dx_ref[...] = dx
        dx16_ref[...] = dx.astype(BF16)
        part = jnp.broadcast_to(part, (8, 128))

        @pl.when(pl.program_id(0) == 0)
        def _():
            dg_ref[...] = dg
            loss_ref[...] = part

        @pl.when(pl.program_id(0) > 0)
        def _():
            dg_ref[...] += dg
            loss_ref[...] += part

    row = pl.BlockSpec((tr, d), lambda i: (i, 0))
    vec = pl.BlockSpec((1, d), lambda i: (0, 0))
    return pl.pallas_call(
        body, name=name, grid=(s // tr,),
        in_specs=[row, vec, row],
        out_specs=[row, row, pl.BlockSpec((8, 128), lambda i: (0, 0)), vec],
        out_shape=[jax.ShapeDtypeStruct((s, d), F32), jax.ShapeDtypeStruct((s, d), BF16),
                   jax.ShapeDtypeStruct((8, 128), F32), jax.ShapeDtypeStruct((1, d), F32)],
        compiler_params=_params("arbitrary"),
    )(x, g, target)


def _split3(x):
    hi = x.astype(BF16)
    rem = x - hi.astype(F32)
    mid = rem.astype(BF16)
    lo = (rem - mid.astype(F32)).astype(BF16)
    return hi, mid, lo


def _tri_sum(tri, x):
    hi, mid, lo = _split3(x)
    dot = functools.partial(jnp.dot, preferred_element_type=F32)
    return dot(tri, hi) + dot(tri, mid) + dot(tri, lo)


def _gates_fwd(f, fb, *, name):
    s, w = f.shape
    tb = 128
    nb = s // tb

    def body(f_ref, fb_ref, c_ref, carry):
        @pl.when(pl.program_id(0) == 0)
        def _():
            carry[...] = jnp.zeros_like(carry)

        logf = jax.nn.log_sigmoid(f_ref[...] + fb_ref[...])
        row = lax.broadcasted_iota(jnp.int32, (tb, tb), 0)
        col = lax.broadcasted_iota(jnp.int32, (tb, tb), 1)
        tri = (row >= col).astype(BF16)
        c = _tri_sum(tri, logf) + carry[...]
        c_ref[...] = c
        carry[...] = c[tb - 1:tb, :]

    return pl.pallas_call(
        body, name=name, grid=(nb,),
        in_specs=[pl.BlockSpec((tb, w), lambda i: (i, 0)), pl.BlockSpec((1, w), lambda i: (0, 0))],
        out_specs=pl.BlockSpec((tb, w), lambda i: (i, 0)),
        out_shape=jax.ShapeDtypeStruct((s, w), F32),
        scratch_shapes=[pltpu.VMEM((1, w), F32)],
        compiler_params=_params("arbitrary"),
    )(f, fb)


def _gates_bwd(f, fb, dc, *, name):
    s, w = f.shape
    tb = 128
    nb = s // tb

    def body(f_ref, fb_ref, dc_ref, df_ref, dfb_ref, carry):
        @pl.when(pl.program_id(0) == 0)
        def _():
            carry[...] = jnp.zeros_like(carry)
            dfb_ref[...] = jnp.zeros_like(dfb_ref)

        row = lax.broadcasted_iota(jnp.int32, (tb, tb), 0)
        col = lax.broadcasted_iota(jnp.int32, (tb, tb), 1)
        tri = (row <= col).astype(BF16)
        dlogf = _tri_sum(tri, dc_ref[...]) + carry[...]
        carry[...] = dlogf[0:1, :]
        df = dlogf * jax.nn.sigmoid(-(f_ref[...] + fb_ref[...]))
        df_ref[...] = df
        dfb_ref[...] += jnp.sum(df, axis=0, keepdims=True)

    rev = pl.BlockSpec((tb, w), lambda i: (nb - 1 - i, 0))
    vec = pl.BlockSpec((1, w), lambda i: (0, 0))
    return pl.pallas_call(
        body, name=name, grid=(nb,),
        in_specs=[rev, vec, rev], out_specs=[rev, vec],
        out_shape=[jax.ShapeDtypeStruct((s, w), F32), jax.ShapeDtypeStruct((1, w), F32)],
        scratch_shapes=[pltpu.VMEM((1, w), F32)],
        compiler_params=_params("arbitrary"),
    )(f, fb, dc)


def _nt(a, b):
    return lax.dot_general(a, b, (((1,), (1,)), ((), ())), preferred_element_type=F32)


def _tn(a, b):
    return lax.dot_general(a, b, (((0,), (0,)), ((), ())), preferred_element_type=F32)


def _nn(a, b):
    return jnp.dot(a, b, preferred_element_type=F32)


DIL_GROUP = 8
FOX_TILE = 1024
LOG2E = 1.4426950408889634


def _causal_pairs(nt, key_major):
    if key_major:
        pairs = [(q, k) for k in range(nt) for q in range(k, nt)]
    else:
        pairs = [(q, k) for q in range(nt) for k in range(q + 1)]
    return (jnp.asarray([p[0] for p in pairs], jnp.int32), jnp.asarray([p[1] for p in pairs], jnp.int32))


def _fox_fwd(qkv, c_row, ck_lanes, n_heads, *, name, carry=None):
    s = qkv.shape[0]
    e = HEAD_DIM
    t = min(FOX_TILE, s)
    nt = s // t
    scale2 = e ** -0.5 * LOG2E
    lanes = 128
    q_tab, k_tab = _causal_pairs(nt, key_major=False)

    def body(q_tab, k_tab, q_ref, k_ref, v_ref, cq_ref, ck_ref, o_ref, lse_ref,
             m_scr, l_scr, acc_scr, s_scr, p_scr):
        pair = pl.program_id(1)
        qi, ki = q_tab[pair], k_tab[pair]

        @pl.when(ki == 0)
        def _():
            m_scr[...] = jnp.full_like(m_scr, NEG_INF)
            l_scr[...] = jnp.zeros_like(l_scr)
            acc_scr[...] = jnp.zeros_like(acc_scr)

        def update(diagonal):
            s_scr[...] = _nt(k_ref[...], q_ref[...])
            ck2 = ck_ref[...]
            for c0 in range(0, t, lanes):
                cols = pl.ds(c0, lanes)

                def logits2():
                    x = s_scr[:, cols] * scale2 - ck2
                    if diagonal:
                        key = lax.broadcasted_iota(jnp.int32, x.shape, 0)
                        qry = c0 + lax.broadcasted_iota(jnp.int32, x.shape, 1)
                        x = jnp.where(key <= qry, x, NEG_INF)
                    return x

                m_old = m_scr[:, cols]
                m_new = jnp.maximum(m_old, jnp.max(logits2(), axis=0, keepdims=True))
                p = jnp.exp2(logits2() - m_new)
                alpha = jnp.exp2(m_old - m_new)
                l_scr[:, cols] = alpha * l_scr[:, cols] + jnp.sum(p, axis=0, keepdims=True)
                m_scr[:, cols] = m_new
                acc_scr[:, cols] = alpha * acc_scr[:, cols]
                p_scr[:, cols] = p.astype(BF16)
            acc_scr[...] += _tn(v_ref[...], p_scr[...])

        @pl.when(ki < qi)
        def _():
            update(False)

        @pl.when(ki == qi)
        def _():
            update(True)
            o_ref[...] = (acc_scr[...] / l_scr[...]).T
            lse_ref[...] = (m_scr[...] + jnp.log2(l_scr[...])) * (1.0 / LOG2E) + cq_ref[...]

    h_ = n_heads
    return _call(
        body, name=name, grid=(h_, int(q_tab.shape[0])), tables=(q_tab, k_tab),
        in_specs=[
            pl.BlockSpec((t, e), lambda h, p, qt, kt: (qt[p], h)),
            pl.BlockSpec((t, e), lambda h, p, qt, kt: (kt[p], h_ + h)),
            pl.BlockSpec((t, e), lambda h, p, qt, kt: (kt[p], 2 * h_ + h)),
            pl.BlockSpec((None, 1, t), lambda h, p, qt, kt: (h, 0, qt[p])),
            pl.BlockSpec((None, t, lanes), lambda h, p, qt, kt: (h, kt[p], 0)),
        ],
        out_specs=[pl.BlockSpec((t, e), lambda h, p, qt, kt: (qt[p], h)),
                   pl.BlockSpec((None, 1, t), lambda h, p, qt, kt: (h, 0, qt[p]))],
        out_shape=[jax.ShapeDtypeStruct((s, h_ * e), F32), jax.ShapeDtypeStruct((h_, 1, s), F32)],
        operands=(qkv, qkv, qkv, c_row, ck_lanes),
        scratch=[pltpu.VMEM((1, t), F32), pltpu.VMEM((1, t), F32), pltpu.VMEM((e, t), F32),
                 pltpu.VMEM((t, t), F32), pltpu.VMEM((t, t), BF16)],
        semantics=("parallel", "arbitrary"), carry=carry)


def _fox_bwd(qkv, c_row, ck_lanes, lse, y, dy, n_heads, *, name, carry=None):
    s = qkv.shape[0]
    e = HEAD_DIM
    t = min(FOX_TILE, s)
    nt = s // t
    scale = e ** -0.5
    scale2 = scale * LOG2E
    lanes = 128
    q_tab, k_tab = _causal_pairs(nt, key_major=False)
    n_pairs = int(q_tab.shape[0])

    def body(q_tab, k_tab, q_ref, k_ref, v_ref, c_ref, ck_ref, lse_ref, y_ref, dy_ref,
             dq_ref, dk_ref, dv_ref, dcq_ref, dck_ref,
             dq_scr, dk_scr, dv_scr, dck_scr, s_scr, dp_scr, p_scr, ds_scr, do_scr, delta_scr, shift_scr):
        pair = pl.program_id(1)
        qi, ki = q_tab[pair], k_tab[pair]

        @pl.when(pair == 0)
        def _():
            dk_scr[...] = jnp.zeros_like(dk_scr)
            dv_scr[...] = jnp.zeros_like(dv_scr)
            dck_scr[...] = jnp.zeros_like(dck_scr)

        @pl.when(ki == 0)
        def _():
            do = dy_ref[...]
            delta_scr[...] = lax.dot_general(jnp.ones((8, e), F32), do * y_ref[...], (((1,), (1,)), ((), ())),
                                             precision=lax.Precision.HIGHEST, preferred_element_type=F32)
            shift_scr[...] = (lse_ref[...] - c_ref[...]) * LOG2E
            do_scr[...] = do.astype(BF16)
            dq_scr[...] = jnp.zeros_like(dq_scr)
            dcq_ref[...] = jnp.zeros_like(dcq_ref)

        def update(diagonal):
            s_scr[...] = _nt(k_ref[...], q_ref[...])
            dp_scr[...] = _nt(v_ref[...], do_scr[...])
            ck2 = ck_ref[...]
            k_rows = pl.ds(pl.multiple_of(ki * t, t), t)
            for c0 in range(0, t, lanes):
                cols = pl.ds(c0, lanes)
                x = s_scr[:, cols] * scale2 - ck2
                if diagonal:
                    key = lax.broadcasted_iota(jnp.int32, x.shape, 0)
                    qry = c0 + lax.broadcasted_iota(jnp.int32, x.shape, 1)
                    x = jnp.where(key <= qry, x, NEG_INF)
                p = jnp.exp2(x - shift_scr[:, cols])
                ds = p * (dp_scr[:, cols] - delta_scr[0:1, cols])
                dcq_ref[:, cols] += jnp.sum(ds, axis=0, keepdims=True)
                dck_scr[k_rows, :] += ds
                p_scr[:, cols] = p.astype(BF16)
                ds_scr[:, cols] = ds.astype(BF16)
            dv_scr[k_rows, :] += _nn(p_scr[...], do_scr[...])
            dk_scr[k_rows, :] += scale * _nn(ds_scr[...], q_ref[...])
            dq_scr[...] += scale * _tn(k_ref[...], ds_scr[...])

        @pl.when(ki < qi)
        def _():
            update(False)

        @pl.when(ki == qi)
        def _():
            update(True)
            dq_ref[...] = dq_scr[...].T.astype(BF16)

        @pl.when(pair == n_pairs - 1)
        def _():
            dk_ref[...] = dk_scr[...].astype(BF16)
            dv_ref[...] = dv_scr[...].astype(BF16)
            dck_ref[...] = -lax.dot_general(jnp.ones((8, lanes), F32), dck_scr[...], (((1,), (1,)), ((), ())),
                                            precision=lax.Precision.HIGHEST, preferred_element_type=F32)[0:1]

    h_ = n_heads
    q_row = pl.BlockSpec((None, 1, t), lambda h, p, qt, kt: (h, 0, qt[p]))
    q_blk = pl.BlockSpec((t, e), lambda h, p, qt, kt: (qt[p], h))
    whole = pl.BlockSpec((s, e), lambda h, p, qt, kt: (0, h))
    return _call(
        body, name=name, grid=(h_, n_pairs), tables=(q_tab, k_tab),
        in_specs=[
            q_blk,
            pl.BlockSpec((t, e), lambda h, p, qt, kt: (kt[p], h_ + h)),
            pl.BlockSpec((t, e), lambda h, p, qt, kt: (kt[p], 2 * h_ + h)),
            q_row,
            pl.BlockSpec((None, t, lanes), lambda h, p, qt, kt: (h, kt[p], 0)),
            q_row, q_blk, q_blk,
        ],
        out_specs=[q_blk, whole, whole, q_row,
                   pl.BlockSpec((None, 1, s), lambda h, p, qt, kt: (h, 0, 0))],
        out_shape=[jax.ShapeDtypeStruct((s, h_ * e), BF16)] * 3
        + [jax.ShapeDtypeStruct((h_, 1, s), F32), jax.ShapeDtypeStruct((h_, 1, s), F32)],
        operands=(qkv, qkv, qkv, c_row, ck_lanes, lse, y, dy),
        scratch=[pltpu.VMEM((e, t), F32), pltpu.VMEM((s, e), F32), pltpu.VMEM((s, e), F32),
                 pltpu.VMEM((s, lanes), F32), pltpu.VMEM((t, t), F32), pltpu.VMEM((t, t), F32), pltpu.VMEM((t, t), BF16),
                 pltpu.VMEM((t, t), BF16), pltpu.VMEM((t, e), BF16), pltpu.VMEM((8, t), F32),
                 pltpu.VMEM((1, t), F32)],
        semantics=("parallel", "arbitrary"), carry=carry)


def _rel_bucket_table(dilation, span):
    dist = np.arange(span + 1, dtype=np.int64) * dilation
    max_exact = REL_BUCKETS // 2
    d = np.maximum(dist.astype(np.float32), np.float32(1.0))
    large = max_exact + (np.log(d / np.float32(max_exact)) / np.float32(math.log(REL_MAX_DISTANCE / max_exact))
                         * np.float32(REL_BUCKETS - max_exact)).astype(np.int32)
    large = np.minimum(large, REL_BUCKETS - 1)
    return np.where(dist < max_exact, dist, large)


def _bucket_bands(dilation, span):
    table = _rel_bucket_table(dilation, span)
    bands = []
    for n, b in enumerate(table):
        if bands and bands[-1][0] == int(b):
            bands[-1][2] = n
        else:
            assert not any(bb[0] == int(b) for bb in bands)
            bands.append([int(b), n, n])
    return [tuple(b) for b in bands]


def _steps_back():
    i = lax.broadcasted_iota(jnp.int32, (DIL_BLOCK, DIL_BLOCK), 0)
    j = lax.broadcasted_iota(jnp.int32, (DIL_BLOCK, DIL_BLOCK), 1)
    return i - j, DIL_BLOCK + i - j


def _bias_tiles(rb_ref, h, bands, span):
    n_cur, n_prev = _steps_back()
    t_cur = jnp.zeros((DIL_BLOCK, DIL_BLOCK), F32)
    t_prev = jnp.zeros((DIL_BLOCK, DIL_BLOCK), F32)
    for b, lo, hi in bands:
        val = rb_ref[b, h]
        t_cur = jnp.where((n_cur >= lo) & (n_cur <= hi), val, t_cur)
        t_prev = jnp.where((n_prev >= lo) & (n_prev <= hi), val, t_prev)
    t_cur = jnp.where(n_cur >= 0, t_cur, NEG_INF)
    t_prev = jnp.where(n_prev <= span, t_prev, NEG_INF)
    return t_cur, t_prev


def _dil_rows(start, dilation):
    return pl.ds(start, DIL_BLOCK, stride=dilation) if dilation > 1 else pl.ds(start, DIL_BLOCK)


def _dil_block_groups(s, dilation, run_group):
    group = dilation * DIL_BLOCK
    per_residue = s // group

    def block(r, n):
        if isinstance(n, int):
            return n * group + r, max(n - 1, 0) * group + r, n == 0
        return (pl.multiple_of(n * group, DIL_BLOCK) + r,
                pl.multiple_of(jnp.maximum(n - 1, 0) * group, DIL_BLOCK) + r, n == 0)

    if per_residue >= DIL_GROUP:
        assert per_residue % DIL_GROUP == 0
        for r in range(dilation):
            def trip(it, carry, r=r):
                run_group([block(r, it * DIL_GROUP + j) for j in range(DIL_GROUP)])
                return carry
            lax.fori_loop(0, per_residue // DIL_GROUP, trip, 0)
    else:
        residues = DIL_GROUP // per_residue
        assert residues * per_residue == DIL_GROUP and dilation % residues == 0
        for r0 in range(0, dilation, residues):
            run_group([block(r, n) for r in range(r0, r0 + residues) for n in range(per_residue)])


def _dil_fwd(qkv, rel_bias, n_heads, col0, *, name, carry=None):
    s = qkv.shape[0]
    e = HEAD_DIM
    scale = e ** -0.5
    n_pat = len(DIL_PATTERNS)
    for window, d in DIL_PATTERNS:
        assert window // d == DIL_BLOCK and s % (d * DIL_BLOCK) == 0
    bands = [_bucket_bands(d, w // d) for w, d in DIL_PATTERNS]

    def body(rb_ref, q_ref, k_ref, v_ref, y_ref, lse_ref, qf, kf, vf, *scr):
        o_scr, l_scr = scr[:n_pat], scr[n_pat:]
        h = pl.program_id(0)
        qf[...] = q_ref[...].astype(F32)
        kf[...] = k_ref[...].astype(F32)
        vf[...] = v_ref[...].astype(F32)
        for pi, (window, d) in enumerate(DIL_PATTERNS):
            t_cur, t_prev = _bias_tiles(rb_ref, h, bands[pi], window // d)

            def run_group(blocks, d=d, pi=pi, t_cur=t_cur, t_prev=t_prev):
                rows = [(_dil_rows(q0, d), _dil_rows(p0, d)) for q0, p0, _ in blocks]
                qb = [qf[cur, :].astype(BF16) for cur, _ in rows]
                s_c = [_nt(q, kf[cur, :].astype(BF16)) for q, (cur, _) in zip(qb, rows)]
                s_p = [_nt(q, kf[prev, :].astype(BF16)) for q, (_, prev) in zip(qb, rows)]
                s_c = [x * scale + t_cur for x in s_c]
                s_p = [x * scale + (t_prev + jnp.where(first, NEG_INF, 0.0)) for x, (_, _, first) in zip(s_p, blocks)]
                m = [jnp.max(jnp.maximum(a, b), axis=-1, keepdims=True) for a, b in zip(s_c, s_p)]
                p_c = [jnp.exp(a - mm) for a, mm in zip(s_c, m)]
                p_p = [jnp.exp(b - mm) for b, mm in zip(s_p, m)]
                l = [jnp.sum(a + b, axis=-1, keepdims=True) for a, b in zip(p_c, p_p)]
                o = [_nn(a.astype(BF16), vf[cur, :].astype(BF16)) + _nn(b.astype(BF16), vf[prev, :].astype(BF16))
                     for a, b, (cur, prev) in zip(p_c, p_p, rows)]
                for (cur, _), oo, ll, mm in zip(rows, o, l, m):
                    o_scr[pi][cur, :] = oo / ll
                    l_scr[pi][cur, :] = mm + jnp.log(ll)

            _dil_block_groups(s, d, run_group)
        lses = [l_scr[pi][...] for pi in range(n_pat)]
        m = functools.reduce(jnp.maximum, lses)
        ws = [jnp.exp(l - m) for l in lses]
        tot = functools.reduce(jnp.add, ws)
        y = functools.reduce(jnp.add, [w * o_scr[pi][...] for pi, w in enumerate(ws)])
        y_ref[...] = y / tot
        lse_ref[...] = m + jnp.log(tot)

    h_ = n_heads
    return _call(
        body, name=name, grid=(h_,),
        in_specs=[pl.BlockSpec(memory_space=pltpu.SMEM),
                  pl.BlockSpec((s, e), lambda h: (0, col0 + h)),
                  pl.BlockSpec((s, e), lambda h: (0, col0 + h_ + h)),
                  pl.BlockSpec((s, e), lambda h: (0, col0 + 2 * h_ + h))],
        out_specs=[pl.BlockSpec((s, e), lambda h: (0, h)), pl.BlockSpec((None, s, 1), lambda h: (h, 0, 0))],
        out_shape=[jax.ShapeDtypeStruct((s, h_ * e), F32), jax.ShapeDtypeStruct((h_, s, 1), F32)],
        operands=(rel_bias, qkv, qkv, qkv),
        scratch=[pltpu.VMEM((s, e), F32)] * 3 + [pltpu.VMEM((s, e), F32)] * n_pat
        + [pltpu.VMEM((s, 1), F32)] * n_pat,
        semantics=("parallel",), carry=carry)


def _dil_bwd(qkv, rel_bias, lse, y, dy, n_heads, col0, *, name, carry=None):
    s = qkv.shape[0]
    e = HEAD_DIM
    scale = e ** -0.5
    bands = [_bucket_bands(d, w // d) for w, d in DIL_PATTERNS]

    def body(rb_ref, q_ref, k_ref, v_ref, lse_ref, y_ref, dy_ref, dq_ref, dk_ref, dv_ref, drb_ref,
             qf, kf, vf, dqf, dkf, dvf, delta, dt_cur, dt_prev):
        h = pl.program_id(0)
        qf[...] = q_ref[...].astype(F32)
        kf[...] = k_ref[...].astype(F32)
        vf[...] = v_ref[...].astype(F32)
        dqf[...] = jnp.zeros_like(dqf)
        dkf[...] = jnp.zeros_like(dkf)
        dvf[...] = jnp.zeros_like(dvf)
        delta[...] = jnp.sum(dy_ref[...] * y_ref[...], axis=-1, keepdims=True)
        lane = lax.broadcasted_iota(jnp.int32, (1, 128), 1)
        drb = jnp.zeros((1, 128), F32)
        n_cur, n_prev = _steps_back()
        for pi, (window, d) in enumerate(DIL_PATTERNS):
            t_cur, t_prev = _bias_tiles(rb_ref, h, bands[pi], window // d)
            dt_cur[...] = jnp.zeros_like(dt_cur)
            dt_prev[...] = jnp.zeros_like(dt_prev)

            def run_group(blocks, d=d, t_cur=t_cur, t_prev=t_prev):
                rows = [(_dil_rows(q0, d), _dil_rows(p0, d)) for q0, p0, _ in blocks]
                qb = [qf[cur, :].astype(BF16) for cur, _ in rows]
                kc = [kf[cur, :].astype(BF16) for cur, _ in rows]
                kp = [kf[prev, :].astype(BF16) for _, prev in rows]
                vc = [vf[cur, :].astype(BF16) for cur, _ in rows]
                vp = [vf[prev, :].astype(BF16) for _, prev in rows]
                do = [dy_ref[cur, :].astype(BF16) for cur, _ in rows]
                lse_b = [lse_ref[cur, :] for cur, _ in rows]
                delta_b = [delta[cur, :] for cur, _ in rows]
                s_c = [_nt(q, k) for q, k in zip(qb, kc)]
                s_p = [_nt(q, k) for q, k in zip(qb, kp)]
                dp_c = [_nt(g, v) for g, v in zip(do, vc)]
                dp_p = [_nt(g, v) for g, v in zip(do, vp)]
                p_c = [jnp.exp(x * scale + t_cur - ls) for x, ls in zip(s_c, lse_b)]
                p_p = [jnp.exp(x * scale + (t_prev + jnp.where(first, NEG_INF, 0.0)) - ls)
                       for x, ls, (_, _, first) in zip(s_p, lse_b, blocks)]
                ds_c = [p * (g - dl) for p, g, dl in zip(p_c, dp_c, delta_b)]
                ds_p = [p * (g - dl) for p, g, dl in zip(p_p, dp_p, delta_b)]
                dt_cur[...] += functools.reduce(jnp.add, ds_c)
                dt_prev[...] += functools.reduce(jnp.add, ds_p)
                p_c, p_p = [p.astype(BF16) for p in p_c], [p.astype(BF16) for p in p_p]
                ds_c, ds_p = [x.astype(BF16) for x in ds_c], [x.astype(BF16) for x in ds_p]
                dq = [scale * (_nn(a, k1) + _nn(b, k2)) for a, b, k1, k2 in zip(ds_c, ds_p, kc, kp)]
                dk_c = [scale * _tn(a, q) for a, q in zip(ds_c, qb)]
                dv_c = [_tn(p, g) for p, g in zip(p_c, do)]
                dk_p = [scale * _tn(b, q) for b, q in zip(ds_p, qb)]
                dv_p = [_tn(p, g) for p, g in zip(p_p, do)]
                for i, (cur, prev) in enumerate(rows):
                    dqf[cur, :] += dq[i]
                    dkf[cur, :] += dk_c[i]
                    dvf[cur, :] += dv_c[i]
                    dkf[prev, :] += dk_p[i]
                    dvf[prev, :] += dv_p[i]

            _dil_block_groups(s, d, run_group)
            dtc, dtp = dt_cur[...], dt_prev[...]
            for b, lo, hi in bands[pi]:
                tot = (jnp.sum(jnp.where((n_cur >= lo) & (n_cur <= hi), dtc, 0.0))
                       + jnp.sum(jnp.where((n_prev >= lo) & (n_prev <= hi), dtp, 0.0)))
                drb = drb + jnp.where(lane == b, tot, 0.0)
        dq_ref[...] = dqf[...].astype(BF16)
        dk_ref[...] = dkf[...].astype(BF16)
        dv_ref[...] = dvf[...].astype(BF16)
        drb_ref[...] = drb

    h_ = n_heads
    col = pl.BlockSpec((s, e), lambda h: (0, h))
    return _call(
        body, name=name, grid=(h_,),
        in_specs=[pl.BlockSpec(memory_space=pltpu.SMEM),
                  pl.BlockSpec((s, e), lambda h: (0, col0 + h)),
                  pl.BlockSpec((s, e), lambda h: (0, col0 + h_ + h)),
                  pl.BlockSpec((s, e), lambda h: (0, col0 + 2 * h_ + h)),
                  pl.BlockSpec((None, s, 1), lambda h: (h, 0, 0)), col, col],
        out_specs=[col, col, col, pl.BlockSpec((None, 1, 128), lambda h: (h, 0, 0))],
        out_shape=[jax.ShapeDtypeStruct((s, h_ * e), BF16)] * 3 + [jax.ShapeDtypeStruct((h_, 1, 128), F32)],
        operands=(rel_bias, qkv, qkv, qkv, lse, y, dy),
        scratch=[pltpu.VMEM((s, e), F32)] * 6 + [pltpu.VMEM((s, 1), F32)]
        + [pltpu.VMEM((DIL_BLOCK, DIL_BLOCK), F32)] * 2,
        semantics=("parallel",), carry=carry)


def _place():
    x, y, c = lax.axis_index("x"), lax.axis_index("y"), lax.axis_index("c")
    chips = [(1 - x, y), (x, 1 - y), (1 - x, 1 - y)]
    return x, y, c, chips


def _block_index(px, py, pc):
    return 4 * px + 2 * py + pc


def _gather_task(blocks, *, row0=0, rows=None, into=None):
    n = len(blocks)
    part = blocks[0].shape[0]
    rows = rows or part

    def copies(ins, outs, sems):
        send_sems, recv_sems, local_sems = sems
        x, y, c, chips = _place()
        me, sibling = (x, y, c), (x, y, 1 - c)

        def place_of(a, block):
            slot = outs[a].at[_block_index(*block)]
            return slot if part == rows else slot.at[pl.ds(row0, part)]

        def copy(a, k, block, to, src=None):
            slot = place_of(a, block)
            return pltpu.make_async_remote_copy(
                src_ref=slot if src is None else src, dst_ref=slot,
                send_sem=send_sems.at[a, k], recv_sem=recv_sems.at[a, k],
                device_id=to, device_id_type=MESH)

        local = lambda a: pltpu.make_async_copy(ins[a], place_of(a, me), local_sems.at[a])
        return copy, local, me, sibling, c, chips

    def start(ins, outs, sems):
        copy, local, me, sibling, c, chips = copies(ins, outs, sems)
        for a in range(n):
            local(a).start()
            copy(a, 0, me, sibling, src=ins[a]).start()
            for j, chip in enumerate(chips):
                copy(a, 1 + j, me, (*chip, c), src=ins[a]).start()

    def forward(ins, outs, sems):
        copy, local, me, sibling, c, chips = copies(ins, outs, sems)
        for a in range(n):
            for j, chip in enumerate(chips):
                copy(a, 1 + j, (*chip, c), me).wait_recv()
                copy(a, 4 + j, (*chip, c), sibling).start()

    def finish(ins, outs, sems):
        copy, local, me, sibling, c, chips = copies(ins, outs, sems)
        for a in range(n):
            copy(a, 0, sibling, me).wait_recv()
            for j, chip in enumerate(chips):
                copy(a, 4 + j, (*chip, 1 - c), me).wait_recv()
        for a in range(n):
            for k in range(7):
                copy(a, k, me, sibling, src=ins[a]).wait_send()
            local(a).wait()

    return _Carried(
        list(blocks) + list(into or []),
        [jax.ShapeDtypeStruct((N_DEV, rows) + b.shape[1:], b.dtype) for b in blocks],
        [pltpu.SemaphoreType.DMA((n, 7)), pltpu.SemaphoreType.DMA((n, 7)), pltpu.SemaphoreType.DMA((n,))],
        start, forward, finish, aliases={n + a: a for a in range(n)} if into else None)


def _exchange_task(arrays, n_slots, route):
    n = len(arrays)

    def copies(ins, outs, sems):
        send_sems, recv_sems = sems
        out = []
        for a in range(n):
            for j in range(n_slots):
                src, to = route(j)
                out.append(pltpu.make_async_remote_copy(
                    src_ref=ins[a].at[src], dst_ref=outs[a].at[j],
                    send_sem=send_sems.at[a, j], recv_sem=recv_sems.at[a, j],
                    device_id=to, device_id_type=MESH))
        return out

    def start(ins, outs, sems):
        for cp in copies(ins, outs, sems):
            cp.start()

    def finish(ins, outs, sems):
        for cp in copies(ins, outs, sems):
            cp.wait()

    return _Carried(
        arrays, [jax.ShapeDtypeStruct((n_slots,) + g.shape[1:], g.dtype) for g in arrays],
        [pltpu.SemaphoreType.DMA((n, n_slots)), pltpu.SemaphoreType.DMA((n, n_slots))],
        start, lambda ins, outs, sems: None, finish)


def _sibling_task(grads):
    def route(q):
        x, y, c, _ = _place()
        return 2 * q + (1 - c), (x, y, 1 - c)
    return _exchange_task(grads, 4, route)


def _chips_task(parts):
    def route(j):
        x, y, c, chips = _place()
        px, py = chips[j]
        return 2 * px + py, (px, py, c)
    return _exchange_task(parts, 3, route)


def _chip_partials(grad, got, place, *, name):
    _, r, cdim = grad.shape
    tr = _row_tile(r, 512)

    def body(place_ref, g_ref, s_ref, o_ref):
        o_ref[...] = (g_ref[...] + s_ref[...]).astype(BF16)

    return pl.pallas_call(
        body, name=name,
        grid_spec=pltpu.PrefetchScalarGridSpec(
            num_scalar_prefetch=1, grid=(4, r // tr),
            in_specs=[pl.BlockSpec((None, tr, cdim), lambda q, i, pos: (2 * q + pos[2], i, 0)),
                      pl.BlockSpec((None, tr, cdim), lambda q, i, pos: (q, i, 0))],
            out_specs=pl.BlockSpec((None, tr, cdim), lambda q, i, pos: (q, i, 0))),
        out_shape=jax.ShapeDtypeStruct((4, r, cdim), BF16),
        compiler_params=_params("parallel", "parallel"),
    )(place, grad, got)


def _all_reduce_small(v, *, name):
    r, w = v.shape

    def body(v_ref, o_ref, buf, send_sems, recv_sems):
        x, y, c, _ = _place()
        me = _block_index(x, y, c)
        buf[me] = v_ref[...]
        copies = []
        for k in range(1, N_DEV):
            fx, fy, fc = (k >> 2) & 1, (k >> 1) & 1, k & 1
            peer = (x ^ fx, y ^ fy, c ^ fc)
            cp = pltpu.make_async_remote_copy(
                src_ref=v_ref, dst_ref=buf.at[me],
                send_sem=send_sems.at[k - 1], recv_sem=recv_sems.at[k - 1],
                device_id=peer, device_id_type=MESH)
            cp.start()
            copies.append(cp)
        for cp in copies:
            cp.wait()
        acc = buf[0]
        for p in range(1, N_DEV):
            acc = acc + buf[p]
        o_ref[...] = acc

    vmem = pl.BlockSpec(memory_space=pltpu.VMEM)
    return pl.pallas_call(
        body, name=name, in_specs=[vmem], out_specs=vmem,
        out_shape=jax.ShapeDtypeStruct((r, w), F32),
        scratch_shapes=[pltpu.VMEM((N_DEV, r, w), F32), pltpu.SemaphoreType.DMA((N_DEV - 1,)),
                        pltpu.SemaphoreType.DMA((N_DEV - 1,))],
    )(v)


def _adamw_math(w, g, m, v):
    m = ADAM_B1 * m + (1.0 - ADAM_B1) * g
    v = ADAM_B2 * v + (1.0 - ADAM_B2) * (g * g)
    m_hat = m / (1.0 - ADAM_B1 ** ADAM_STEP)
    v_hat = v / (1.0 - ADAM_B2 ** ADAM_STEP)
    delta = -ADAM_LR * (m_hat / (jnp.sqrt(v_hat) + ADAM_EPS) + ADAM_WD * w)
    return delta, m, v


def _adamw_sharded(w, m, v, part, got, place, *, name, carry=None):
    n_l, r, cdim = w.shape
    tr = _row_tile(r, 256)

    def body(place_ref, *refs):
        w_ref, m_ref, v_ref = refs[:3]
        p_refs, g_refs = refs[3:3 + n_l], refs[3 + n_l:3 + 2 * n_l]
        g_out, d_out, m_out, v_out = refs[3 + 2 * n_l:]
        for l in range(n_l):
            @pl.when(pl.program_id(0) == l)
            def _(l=l):
                g = p_refs[l][...].astype(F32)
                for j in range(3):
                    g = g + g_refs[l][j].astype(F32)
                delta, m_new, v_new = _adamw_math(w_ref[...], g, m_ref[...], v_ref[...])
                g_out[...] = g
                d_out[...] = delta
                m_out[...] = m_new
                v_out[...] = v_new

    shard = pl.BlockSpec((None, tr, cdim), lambda l, i, pos: (l, i, 0))
    p_specs = [pl.BlockSpec((None, tr, cdim),
                            lambda l, i, pos, li=li: (2 * pos[0] + pos[1], jnp.where(l == li, i, 0), 0))
               for li in range(n_l)]
    g_specs = [pl.BlockSpec((3, tr, cdim), lambda l, i, pos, li=li: (0, jnp.where(l == li, i, 0), 0))
               for li in range(n_l)]
    return _call(
        body, name=name, grid=(n_l, r // tr), tables=(place,),
        in_specs=[shard] * 3 + p_specs + g_specs, out_specs=[shard] * 4,
        out_shape=[jax.ShapeDtypeStruct(w.shape, F32)] * 4,
        operands=(w, m, v, *part, *got), semantics=("arbitrary", "parallel"), carry=carry)


def _adamw_small(w, g, m, v, *, name):
    def body(w_ref, g_ref, m_ref, v_ref, d_out, m_out, v_out):
        delta, m_new, v_new = _adamw_math(w_ref[...], g_ref[...], m_ref[...], v_ref[...])
        d_out[...] = delta
        m_out[...] = m_new
        v_out[...] = v_new

    vmem = pl.BlockSpec(memory_space=pltpu.VMEM)
    return pl.pallas_call(
        body, name=name, in_specs=[vmem] * 4, out_specs=[vmem] * 3,
        out_shape=[jax.ShapeDtypeStruct(w.shape, F32)] * 3,
    )(w, g, m, v)


def _pad_lanes(a, width=128):
    return jnp.pad(a, ((0, 0), (0, width - a.shape[1])))


def _train_step(x, target, small, shards, place):
    w_in, w_out, w_mlp_in, w_mlp_out = shards
    depth, d_model, in_shard = w_in.shape
    n_heads = small["forget_b"].shape[1]
    dh = n_heads * HEAD_DIM
    d_ff = w_mlp_in.shape[2] * N_DEV
    row = lambda a: a.reshape(1, -1)
    block = {"in": w_in.astype(BF16), "out": w_out.astype(BF16), "mi": w_mlp_in.astype(BF16),
             "mo": w_mlp_out.astype(BF16)}
    whole = {}

    gate_lo, gate_hi = 3 * dh, 3 * dh + n_heads

    def shard_columns(g, lo, hi):
        pieces = []
        while lo < hi:
            p, a = divmod(lo, in_shard)
            b = min(in_shard, a + hi - lo)
            pieces.append(g[p][:, a:b])
            lo += b - a
        return pieces

    def in_columns(d_qkv, d_gate, lo, hi):
        pieces = []
        if lo < gate_lo:
            pieces.append(d_qkv[:, lo:min(hi, gate_lo)])
        if lo < gate_hi and hi > gate_lo:
            pieces.append(d_gate[:, max(lo, gate_lo) - gate_lo:min(hi, gate_hi) - gate_lo])
        if hi > gate_hi:
            pieces.append(d_qkv[:, max(lo, gate_hi) - n_heads:hi - n_heads])
        return pieces

    def unpack(kind, l, g):
        if kind == "in":
            whole["qkv", l] = jnp.concatenate(
                shard_columns(g, 0, gate_lo) + shard_columns(g, gate_hi, N_DEV * in_shard), axis=1)
            whole["f", l] = _pad_lanes(jnp.concatenate(shard_columns(g, gate_lo, gate_hi), axis=1))
        elif kind == "out":
            whole["out", l] = g.reshape(2 * dh, d_model)
        elif kind == "mi":
            whole["mi", l] = g
        else:
            whole["mo", l] = g.reshape(d_ff, d_model)

    half_done = {}

    def gathering(kind, l, part):
        blk = block[kind][l]
        rows = blk.shape[0]
        if part == "all":
            return _gather_task([blk]), lambda got: unpack(kind, l, got[0])
        if part == "first":
            return (_gather_task([blk[:rows // 2]], rows=rows),
                    lambda got: half_done.__setitem__((kind, l), got[0]))
        return (_gather_task([blk[rows // 2:]], row0=rows // 2, rows=rows, into=[half_done.pop((kind, l))]),
                lambda got: unpack(kind, l, got[0]))

    def carried(call, *wanted):
        wanted = [gathering(*w) for w in wanted if w[1] < depth]
        if not wanted:
            return call()
        res, got = call(carry=_join([task for task, _ in wanted]))
        for (_, done), g in zip(wanted, got):
            done([g])
        return res

    saved = []
    h1 = carried(functools.partial(_rmsnorm_fwd, x, row(small["norm1_g"][0]), name="norm1_fwd_0"), ("in", 0, "all"))
    for l in range(depth):
        qkv = carried(functools.partial(_matmul, h1, whole["qkv", l], out_dtype=BF16, name=f"qkv_proj_{l}"),
                      ("mi", l, "first"))
        f = _matmul(h1, whole["f", l], name=f"gate_proj_{l}")
        fb = _pad_lanes(row(small["forget_b"][l]))
        c = _gates_fwd(f, fb, name=f"gates_fwd_{l}")
        c_heads = c[:, :n_heads].T
        cq = c_heads[:, None, :]
        ck = jnp.broadcast_to(c_heads[:, :, None] * LOG2E, c_heads.shape + (128,))
        ya, lse_a = carried(functools.partial(_fox_fwd, qkv, cq, ck, n_heads, name=f"fox_fwd_{l}"),
                            ("mi", l, "second"), ("out", l, "all"))
        yb, lse_b = carried(functools.partial(_dil_fwd, qkv, small["rel_bias"], n_heads, 3 * n_heads,
                                              name=f"dil_fwd_{l}"), ("mo", l, "first"))
        mixed = _outnorm_fwd(ya, yb, row(small["outnorm_a_g"][l]), row(small["outnorm_b_g"][l]),
                             name=f"outnorm_fwd_{l}")
        x1, h2 = _matmul(mixed, whole["out", l], addend=x, tm=512, tn=d_model, norm_gain=row(small["norm2_g"][l]),
                         name=f"out_proj_{l}")
        relu, act = carried(functools.partial(_matmul, h2, whole["mi", l], b_chunked=True, out_dtype=BF16,
                                              mode="relu_sq", name=f"mlp_in_{l}"),
                            ("mo", l, "second"))
        mlp_out = functools.partial(_matmul, act, whole["mo", l], addend=x1, name=f"mlp_out_{l}")
        saved.append((x, h1, qkv, f, fb, cq, ck, ya, lse_a, yb, lse_b, mixed, x1, h2, relu, act))
        if l + 1 < depth:
            x, h1 = carried(functools.partial(mlp_out, tm=512, tn=d_model, norm_gain=row(small["norm1_g"][l + 1])),
                            ("in", l + 1, "all"))
        else:
            x = mlp_out()
    dx, dx16, loss_tile, d_final = _loss_head(x, row(small["final_norm_g"]), target, name="loss_head")

    parts = {k: [None] * depth for k in ("in", "out", "mi", "mo")}
    got = {k: [None] * depth for k in ("in", "out", "mi", "mo")}
    small_grads = [None] * depth

    def to_sibling(call, kind, l, grad):
        res, (from_sibling,) = call(carry=_sibling_task([grad]))
        parts[kind][l] = _chip_partials(grad, from_sibling, place, name=f"chip_partials_{kind}_{l}")
        return res

    def to_chips(call, kind, l):
        if l >= depth:
            return call()
        res, (got[kind][l],) = call(carry=_chips_task([parts[kind][l]]))
        return res

    for l in reversed(range(depth)):
        x0, h1, qkv, f, fb, cq, ck, ya, lse_a, yb, lse_b, mixed, x1, h2, relu, act = saved[l]
        du = to_chips(functools.partial(_matmul, dx16, whole["mo", l], tb=True, out_dtype=BF16, mode="mul2",
                                        aux=relu, name=f"mlp_out_dx_{l}"), "in", l + 1)
        dw_mo = _matmul(act, dx16, ta=True, name=f"mlp_out_dw_{l}")
        dw_mo = dw_mo.reshape(N_DEV, d_ff // N_DEV, d_model)
        dh2 = to_sibling(functools.partial(_matmul, du, whole["mi", l], tb=True, b_chunked=True,
                                           name=f"mlp_in_dx_{l}"),
                         "mo", l, dw_mo)
        dw_mi = to_chips(functools.partial(_matmul, h2, du, ta=True, out_chunks=N_DEV,
                                           name=f"mlp_in_dw_{l}"), "mo", l)
        dx1, dx1_16, d_norm2 = _rmsnorm_bwd(x1, row(small["norm2_g"][l]), dh2, dx, name=f"norm2_bwd_{l}")
        dmixed = to_sibling(functools.partial(_matmul, dx1_16, whole["out", l], tb=True, name=f"out_proj_dx_{l}"),
                            "mi", l, dw_mi)
        dw_out = _matmul(mixed, dx1_16, ta=True, name=f"out_proj_dw_{l}")
        dw_out = dw_out.reshape(N_DEV, 2 * dh // N_DEV, d_model)
        dya, dyb, d_ga, d_gb = _outnorm_bwd(ya, yb, row(small["outnorm_a_g"][l]), row(small["outnorm_b_g"][l]),
                                            dmixed, name=f"outnorm_bwd_{l}")
        dqa, dka, dva, dcq, dck = to_chips(functools.partial(
            _fox_bwd, qkv, cq, ck, lse_a, ya, dya, n_heads, name=f"fox_bwd_{l}"), "mi", l)
        dqb, dkb, dvb, d_rb = to_sibling(functools.partial(
            _dil_bwd, qkv, small["rel_bias"], lse_b, yb, dyb, n_heads, 3 * n_heads, name=f"dil_bwd_{l}"),
            "out", l, dw_out)
        dc = _pad_lanes((dcq[:, 0, :] + dck[:, 0, :]).T)
        df, dfb = _gates_bwd(f, fb, dc, name=f"gates_bwd_{l}")
        dqkv = jnp.concatenate([dqa, dka, dva, dqb, dkb, dvb], axis=1)
        dw_qkv = to_chips(functools.partial(_matmul, h1, dqkv, ta=True, name=f"qkv_proj_dw_{l}"), "out", l)
        dw_f = _matmul(h1, df, ta=True, name=f"gate_proj_dw_{l}")
        dw_in = jnp.stack([jnp.concatenate(in_columns(dw_qkv, dw_f, p * in_shard, (p + 1) * in_shard), axis=1)
                           for p in range(N_DEV)])
        dh1 = _matmul(df, whole["f", l], tb=True, name=f"gate_proj_dx_{l}")
        dh1 = to_sibling(functools.partial(_matmul, dqkv, whole["qkv", l], tb=True, addend=dh1,
                                           name=f"qkv_proj_dx_{l}"), "in", l, dw_in)
        dx, dx16, d_norm1 = _rmsnorm_bwd(x0, row(small["norm1_g"][l]), dh1, dx1, name=f"norm1_bwd_{l}")
        small_grads[l] = dict(norm1_g=d_norm1[0], forget_b=dfb[0, :n_heads], rel_bias=d_rb[:, 0, :REL_BUCKETS].T,
                              outnorm_a_g=d_ga[0], outnorm_b_g=d_gb[0], norm2_g=d_norm2[0])
    return loss_tile[0, 0], dx, d_final[0], small_grads, parts, got


def _pack_small(parts, rows):
    flat = jnp.concatenate([p.reshape(-1).astype(F32) for p in parts])
    return jnp.pad(flat, (0, rows * 128 - flat.shape[0])).reshape(rows, 128)


def _unpack_small(packed, shapes):
    flat = packed.reshape(-1)
    out, pos = [], 0
    for shp in shapes:
        size = int(np.prod(shp))
        out.append(flat[pos:pos + size].reshape(shp))
        pos += size
    return out


def kernel(x, norm1_g, w_in, forget_b, rel_bias, outnorm_a_g, outnorm_b_g, w_out, norm2_g, w_mlp_in, w_mlp_out, final_norm_g, loss_target, m_norm1_g, m_w_in, m_forget_b, m_rel_bias, m_outnorm_a_g, m_outnorm_b_g, m_w_out, m_norm2_g, m_w_mlp_in, m_w_mlp_out, m_final_norm_g, v_norm1_g, v_w_in, v_forget_b, v_rel_bias, v_outnorm_a_g, v_outnorm_b_g, v_w_out, v_norm2_g, v_w_mlp_in, v_w_mlp_out, v_final_norm_g):
    depth, d_model, in_shard = w_in.shape
    n_heads = forget_b.shape[1]
    assert in_shard * N_DEV == 6 * n_heads * HEAD_DIM + n_heads and x.shape[0] == 1
    place = jnp.stack([lax.axis_index("x"), lax.axis_index("y"), lax.axis_index("c")]).astype(jnp.int32)

    small_names = ["norm1_g", "forget_b", "rel_bias", "outnorm_a_g", "outnorm_b_g", "norm2_g", "final_norm_g"]
    small_w = dict(norm1_g=norm1_g, forget_b=forget_b, rel_bias=rel_bias, outnorm_a_g=outnorm_a_g,
                   outnorm_b_g=outnorm_b_g, norm2_g=norm2_g, final_norm_g=final_norm_g)
    loss_part, dx, d_final, grads, parts, got = _train_step(
        x[0], loss_target[0], small_w, (w_in, w_out, w_mlp_in, w_mlp_out), place)

    small_m = dict(norm1_g=m_norm1_g, forget_b=m_forget_b, rel_bias=m_rel_bias, outnorm_a_g=m_outnorm_a_g,
                   outnorm_b_g=m_outnorm_b_g, norm2_g=m_norm2_g, final_norm_g=m_final_norm_g)
    small_v = dict(norm1_g=v_norm1_g, forget_b=v_forget_b, rel_bias=v_rel_bias, outnorm_a_g=v_outnorm_a_g,
                   outnorm_b_g=v_outnorm_b_g, norm2_g=v_norm2_g, final_norm_g=v_final_norm_g)
    small_g = {k: jnp.stack([g[k] for g in grads]) for k in small_names if k not in ("rel_bias", "final_norm_g")}
    small_g["rel_bias"] = functools.reduce(jnp.add, [g["rel_bias"] for g in grads])
    small_g["final_norm_g"] = d_final
    shapes = [small_w[k].shape for k in small_names]
    total = sum(int(np.prod(s)) for s in shapes) + 1
    rows = -(-total // (8 * 128)) * 8
    packed_g = _pack_small([small_g[k] for k in small_names] + [loss_part], rows)
    packed_g = _all_reduce_small(packed_g, name="reduce_small")
    zero = jnp.zeros((1,), F32)
    packed_w = _pack_small([small_w[k] for k in small_names] + [zero], rows)
    packed_m = _pack_small([small_m[k] for k in small_names] + [zero], rows)
    packed_v = _pack_small([small_v[k] for k in small_names] + [zero + 1.0], rows)
    packed_d, packed_nm, packed_nv = _adamw_small(packed_w, packed_g, packed_m, packed_v, name="adamw_small")
    g_small = dict(zip(small_names, _unpack_small(packed_g, shapes)))
    d_small = dict(zip(small_names, _unpack_small(packed_d, shapes)))
    nm_small = dict(zip(small_names, _unpack_small(packed_nm, shapes)))
    nv_small = dict(zip(small_names, _unpack_small(packed_nv, shapes)))
    loss = packed_g.reshape(-1)[total - 1]

    big_out = {}
    last = parts["in"][0]
    last_halves = [last[:, :last.shape[1] // 2], last[:, last.shape[1] // 2:]]
    last_got = []
    for kind, nm, w, m, v in [("mi", "w_mlp_in", w_mlp_in, m_w_mlp_in, v_w_mlp_in),
                              ("mo", "w_mlp_out", w_mlp_out, m_w_mlp_out, v_w_mlp_out),
                              ("out", "w_out", w_out, m_w_out, v_w_out), ("in", "w_in", w_in, m_w_in, v_w_in)]:
        adamw = functools.partial(_adamw_sharded, w, m, v, parts[kind], got[kind], place, name=f"adamw_{nm}")
        if last_halves:
            big_out[nm], (half,) = adamw(carry=_chips_task([last_halves.pop(0)]))
            last_got.append(half)
            if not last_halves:
                got["in"][0] = jnp.concatenate(last_got, axis=1)
        else:
            big_out[nm] = adamw()

    order = ["norm1_g", "w_in", "forget_b", "rel_bias", "outnorm_a_g", "outnorm_b_g", "w_out", "norm2_g",
             "w_mlp_in", "w_mlp_out", "final_norm_g"]
    pick = lambda k, idx, small: big_out[k][idx] if k in big_out else small[k]
    outs = [loss, dx[None]]
    outs += [pick(k, 0, g_small) for k in order]
    outs += [pick(k, 1, d_small) for k in order]
    outs += [pick(k, 2, nm_small) for k in order]
    outs += [pick(k, 3, nv_small) for k in order]
    return tuple(outs)
```
